```python
import math
import jax, jax.numpy as jnp
from jax import lax
import numpy as np

D_MODEL = 1024
BATCH = 8
SEQ = 8192
DEPTH = 2

D_SSD = D_MODEL
SSD_HEADDIM = 64
SSD_HEADS = D_SSD // SSD_HEADDIM
SSD_GROUPS = 4
SSD_HPG = SSD_HEADS // SSD_GROUPS
D_STATE = 128
SSD_CONV = 4
CHUNK = 128
D_FOX = D_MODEL
FOX_HEADDIM = 64
FOX_HEADS = D_FOX // FOX_HEADDIM
Q_BLOCK = 128
D_MIX = D_SSD + D_FOX
EVEN_SIZES = (D_MIX, D_SSD + 2 * SSD_GROUPS * D_STATE, SSD_HEADS, D_FOX, D_FOX, D_FOX, FOX_HEADS)
EVEN_IN = D_MIX + D_SSD + 2 * SSD_GROUPS * D_STATE + SSD_HEADS + 3 * D_FOX + FOX_HEADS
D_CONV = 2 * D_MODEL
CONV_WIDTH = 31
ODD_IN = 3 * D_CONV
N_EVEN = (DEPTH + 1) // 2
N_ODD = DEPTH // 2
EPS = 1e-6

kernel_name = "hybrid_ssd_fox_conformer_trunk"


def _split_points(sizes):
    pts, acc = [], 0
    for s in sizes[:-1]:
        acc += s
        pts.append(acc)
    return pts


def rmsnorm(x, g):
    xf = x.astype(jnp.float32)
    y = xf * lax.rsqrt(jnp.mean(xf * xf, axis=-1, keepdims=True) + EPS)
    return (y * g.astype(jnp.float32)).astype(x.dtype)


def layernorm(x, g, b):
    xf = x.astype(jnp.float32)
    mu = jnp.mean(xf, axis=-1, keepdims=True)
    xc = xf - mu
    y = xc * lax.rsqrt(jnp.mean(xc * xc, axis=-1, keepdims=True) + EPS)
    return (y * g.astype(jnp.float32) + b.astype(jnp.float32)).astype(x.dtype)


def causal_depthwise_conv(x, w, b):
    k, c = w.shape
    y = lax.conv_general_dilated(x, w[:, None, :], window_strides=(1,), padding=[(k - 1, 0)],
                                 dimension_numbers=('NWC', 'WIO', 'NWC'), feature_group_count=c)
    return y + b


def ssd_chunked(x, dt, a, bm, cm):
    bsz, s, g, r, p = x.shape
    n = bm.shape[-1]
    nc = s // CHUNK
    xd = (x * dt[..., None]).reshape(bsz, nc, CHUNK, g, r, p)
    da = (dt * a).reshape(bsz, nc, CHUNK, g, r)
    bc = bm.reshape(bsz, nc, CHUNK, g, n)
    cc = cm.reshape(bsz, nc, CHUNK, g, n)
    cs = jnp.cumsum(da, axis=2)
    li = jnp.arange(CHUNK)
    causal = (li[:, None] >= li[None, :])[None, None, :, :, None, None]
    seg = cs[:, :, :, None] - cs[:, :, None, :]
    decay = jnp.exp(jnp.where(causal, seg, -jnp.inf))
    cb = jnp.einsum('bclgn,bcsgn->bclsg', cc, bc)
    y_diag = jnp.einsum('bclsgr,bcsgrp->bclgrp', cb[..., None] * decay, xd)
    decay_to_end = jnp.exp(cs[:, :, -1:] - cs)
    chunk_states = jnp.einsum('bclgn,bclgrp->bcgrpn', bc, xd * decay_to_end[..., None])
    chunk_decay = jnp.exp(cs[:, :, -1])

    def step(h, inp):
        st, dec = inp
        h_new = h * dec[..., None, None] + st
        return h_new, h

    h0 = jnp.zeros((bsz, g, r, p, n), chunk_states.dtype)
    _, h_in = lax.scan(step, h0, (jnp.moveaxis(chunk_states, 1, 0), jnp.moveaxis(chunk_decay, 1, 0)))
    h_in = jnp.moveaxis(h_in, 0, 1)
    y_off = jnp.einsum('bclgn,bcgrpn->bclgrp', cc, h_in) * jnp.exp(cs)[..., None]
    return (y_diag + y_off).reshape(bsz, s, g, r, p)


def forgetting_attention(q, k, v, log_f):
    bsz, s, h, p = q.shape
    c = jnp.cumsum(log_f, axis=1)
    ck = jnp.transpose(c, (0, 2, 1))
    nb = s // Q_BLOCK
    qb = jnp.moveaxis(q.reshape(bsz, nb, Q_BLOCK, h, p), 1, 0)
    cqb = jnp.moveaxis(c.reshape(bsz, nb, Q_BLOCK, h), 1, 0)
    kpos = jnp.arange(s)
    scale = p ** -0.5

    def block(args):
        qi, cqi, i = args
        qpos = i * Q_BLOCK + jnp.arange(Q_BLOCK)
        logits = jnp.einsum('bqhp,bkhp->bhqk', qi, k).astype(jnp.float32) * scale
        logits = logits + jnp.transpose(cqi, (0, 2, 1))[..., None] - ck[:, :, None, :]
        logits = jnp.where(kpos[None, :] <= qpos[:, None], logits, -jnp.inf)
        w = jax.nn.softmax(logits, axis=-1).astype(v.dtype)
        return jnp.einsum('bhqk,bkhp->bqhp', w, v)

    out = lax.map(block, (qb, cqb, jnp.arange(nb)))
    return jnp.moveaxis(out, 0, 1).reshape(bsz, s, h, p)


def ssd_fox_layer(x, g_pre, w_in, conv_w, conv_b, dt_bias, a_log, d_skip, fgate_b, ssd_norm, w_out, g_post):
    bsz, s, _ = x.shape
    u = rmsnorm(x, g_pre)
    proj = jnp.einsum('bsd,de->bse', u, w_in)
    z, xbc, dt_raw, q, k, v, f_raw = jnp.split(proj, _split_points(EVEN_SIZES), axis=-1)
    z_ssd, z_fox = jnp.split(z, [D_SSD], axis=-1)
    xbc = jax.nn.silu(causal_depthwise_conv(xbc, conv_w, conv_b))
    xs, bm, cm = jnp.split(xbc, [D_SSD, D_SSD + SSD_GROUPS * D_STATE], axis=-1)
    dt = jax.nn.softplus(dt_raw + dt_bias).reshape(bsz, s, SSD_GROUPS, SSD_HPG)
    a = (-jnp.exp(a_log)).reshape(SSD_GROUPS, SSD_HPG)
    xs = xs.reshape(bsz, s, SSD_GROUPS, SSD_HPG, SSD_HEADDIM)
    y = ssd_chunked(xs, dt, a, bm.reshape(bsz, s, SSD_GROUPS, D_STATE), cm.reshape(bsz, s, SSD_GROUPS, D_STATE))
    y = (y + xs * d_skip.reshape(SSD_GROUPS, SSD_HPG)[:, :, None]).reshape(bsz, s, D_SSD)
    yg = (y * jax.nn.silu(z_ssd)).reshape(bsz, s, SSD_GROUPS, D_SSD // SSD_GROUPS).astype(jnp.float32)
    yg = yg * lax.rsqrt(jnp.mean(yg * yg, axis=-1, keepdims=True) + EPS)
    y = (yg.reshape(bsz, s, D_SSD) * ssd_norm.astype(jnp.float32)).astype(x.dtype)
    log_f = jax.nn.log_sigmoid((f_raw + fgate_b).astype(jnp.float32))
    o = forgetting_attention(q.reshape(bsz, s, FOX_HEADS, FOX_HEADDIM),
                             k.reshape(bsz, s, FOX_HEADS, FOX_HEADDIM),
                             v.reshape(bsz, s, FOX_HEADS, FOX_HEADDIM), log_f)
    o = o.reshape(bsz, s, D_FOX) * jax.nn.silu(z_fox)
    out = jnp.einsum('bse,ed->bsd', jnp.concatenate([y, o], axis=-1), w_out)
    return x + rmsnorm(out, g_post)


def conformer_conv_layer(x, g_pre, w_in, conv_w, conv_b, ln_g, ln_b, w_out, g_post):
    u = rmsnorm(x, g_pre)
    proj = jnp.einsum('bsd,de->bse', u, w_in)
    val, glu_gate, z = jnp.split(proj, [D_CONV, 2 * D_CONV], axis=-1)
    h = val * jax.nn.sigmoid(glu_gate)
    h = causal_depthwise_conv(h, conv_w, conv_b)
    h = jax.nn.silu(layernorm(h, ln_g, ln_b))
    h = h * jax.nn.silu(z)
    out = jnp.einsum('bse,ed->bsd', h, w_out)
    return x + rmsnorm(out, g_post)


def _fwd_setup_inputs(seed: int = 0) -> dict:
    key = jax.random.key(seed)
    ks = jax.random.split(key, 20)
    f32 = jnp.float32

    def nrm(k, shape, scale):
        return jax.random.normal(k, shape, f32) * scale

    def gain(k, shape):
        return 1.0 + 0.02 * jax.random.normal(k, shape, f32)

    ne, no = N_EVEN, N_ODD
    x = jax.random.normal(ks[0], (BATCH, SEQ, D_MODEL), f32)
    dt0 = jnp.exp(jax.random.uniform(ks[5], (ne, SSD_HEADS), f32, minval=math.log(1e-3), maxval=math.log(1e-1)))
    e_dt_bias = dt0 + jnp.log(-jnp.expm1(-dt0))
    e_a_log = jnp.log(jax.random.uniform(ks[6], (ne, SSD_HEADS), f32, minval=1.0, maxval=16.0))
    return {
        "x": x,
        "e_norm_pre": gain(ks[1], (ne, D_MODEL)),
        "e_w_in": nrm(ks[2], (ne, D_MODEL, EVEN_IN), D_MODEL ** -0.5),
        "e_conv_w": nrm(ks[3], (ne, SSD_CONV, D_SSD + 2 * SSD_GROUPS * D_STATE), SSD_CONV ** -0.5),
        "e_conv_b": nrm(ks[4], (ne, D_SSD + 2 * SSD_GROUPS * D_STATE), 0.02),
        "e_dt_bias": e_dt_bias,
        "e_a_log": e_a_log,
        "e_d_skip": 1.0 + 0.1 * jax.random.normal(ks[7], (ne, SSD_HEADS), f32),
        "e_fgate_b": jax.random.uniform(ks[8], (ne, FOX_HEADS), f32, minval=1.0, maxval=6.0),
        "e_ssd_norm": gain(ks[9], (ne, D_SSD)),
        "e_w_out": nrm(ks[10], (ne, D_MIX, D_MODEL), D_MIX ** -0.5),
        "e_norm_post": gain(ks[11], (ne, D_MODEL)),
        "o_norm_pre": gain(ks[12], (no, D_MODEL)),
        "o_w_in": nrm(ks[13], (no, D_MODEL, ODD_IN), D_MODEL ** -0.5),
        "o_conv_w": nrm(ks[14], (no, CONV_WIDTH, D_CONV), CONV_WIDTH ** -0.5),
        "o_conv_b": nrm(ks[15], (no, D_CONV), 0.02),
        "o_ln_g": gain(ks[16], (no, D_CONV)),
        "o_ln_b": nrm(ks[17], (no, D_CONV), 0.02),
        "o_w_out": nrm(ks[18], (no, D_CONV, D_MODEL), D_CONV ** -0.5),
        "o_norm_post": gain(ks[19], (no, D_MODEL)),
    }


def _fwd_reference(x, e_norm_pre, e_w_in, e_conv_w, e_conv_b, e_dt_bias, e_a_log, e_d_skip, e_fgate_b,
              e_ssd_norm, e_w_out, e_norm_post, o_norm_pre, o_w_in, o_conv_w, o_conv_b, o_ln_g, o_ln_b,
              o_w_out, o_norm_post):
    for layer in range(DEPTH):
        i = layer // 2
        if layer % 2 == 0:
            x = ssd_fox_layer(x, e_norm_pre[i], e_w_in[i], e_conv_w[i], e_conv_b[i], e_dt_bias[i],
                              e_a_log[i], e_d_skip[i], e_fgate_b[i], e_ssd_norm[i], e_w_out[i],
                              e_norm_post[i])
        else:
            x = conformer_conv_layer(x, o_norm_pre[i], o_w_in[i], o_conv_w[i], o_conv_b[i], o_ln_g[i],
                                     o_ln_b[i], o_w_out[i], o_norm_post[i])
    return x


import jax as _jax
import jax.numpy as _jnp

TWIN_FORMAT = 'train_step'
FWD_PARAMS = ['x', 'e_norm_pre', 'e_w_in', 'e_conv_w', 'e_conv_b', 'e_dt_bias', 'e_a_log', 'e_d_skip', 'e_fgate_b', 'e_ssd_norm', 'e_w_out', 'e_norm_post', 'o_norm_pre', 'o_w_in', 'o_conv_w', 'o_conv_b', 'o_ln_g', 'o_ln_b', 'o_w_out', 'o_norm_post']
TWIN_WEIGHTS = ['e_norm_pre', 'e_w_in', 'e_conv_w', 'e_conv_b', 'e_dt_bias', 'e_a_log', 'e_d_skip', 'e_fgate_b', 'e_ssd_norm', 'e_w_out', 'e_norm_post', 'o_norm_pre', 'o_w_in', 'o_conv_w', 'o_conv_b', 'o_ln_g', 'o_ln_b', 'o_w_out', 'o_norm_post']
TWIN_DIFF_INPUT = 'x'
TWIN_INPUTS = ['x', 'e_norm_pre', 'e_w_in', 'e_conv_w', 'e_conv_b', 'e_dt_bias', 'e_a_log', 'e_d_skip', 'e_fgate_b', 'e_ssd_norm', 'e_w_out', 'e_norm_post', 'o_norm_pre', 'o_w_in', 'o_conv_w', 'o_conv_b', 'o_ln_g', 'o_ln_b', 'o_w_out', 'o_norm_post', 'loss_target', 'm_e_norm_pre', 'm_e_w_in', 'm_e_conv_w', 'm_e_conv_b', 'm_e_dt_bias', 'm_e_a_log', 'm_e_d_skip', 'm_e_fgate_b', 'm_e_ssd_norm', 'm_e_w_out', 'm_e_norm_post', 'm_o_norm_pre', 'm_o_w_in', 'm_o_conv_w', 'm_o_conv_b', 'm_o_ln_g', 'm_o_ln_b', 'm_o_w_out', 'm_o_norm_post', 'v_e_norm_pre', 'v_e_w_in', 'v_e_conv_w', 'v_e_conv_b', 'v_e_dt_bias', 'v_e_a_log', 'v_e_d_skip', 'v_e_fgate_b', 'v_e_ssd_norm', 'v_e_w_out', 'v_e_norm_post', 'v_o_norm_pre', 'v_o_w_in', 'v_o_conv_w', 'v_o_conv_b', 'v_o_ln_g', 'v_o_ln_b', 'v_o_w_out', 'v_o_norm_post']
TWIN_OUTPUTS = ['loss', 'grad_x', 'grad_e_norm_pre', 'grad_e_w_in', 'grad_e_conv_w', 'grad_e_conv_b', 'grad_e_dt_bias', 'grad_e_a_log', 'grad_e_d_skip', 'grad_e_fgate_b', 'grad_e_ssd_norm', 'grad_e_w_out', 'grad_e_norm_post', 'grad_o_norm_pre', 'grad_o_w_in', 'grad_o_conv_w', 'grad_o_conv_b', 'grad_o_ln_g', 'grad_o_ln_b', 'grad_o_w_out', 'grad_o_norm_post', 'delta_e_norm_pre', 'delta_e_w_in', 'delta_e_conv_w', 'delta_e_conv_b', 'delta_e_dt_bias', 'delta_e_a_log', 'delta_e_d_skip', 'delta_e_fgate_b', 'delta_e_ssd_norm', 'delta_e_w_out', 'delta_e_norm_post', 'delta_o_norm_pre', 'delta_o_w_in', 'delta_o_conv_w', 'delta_o_conv_b', 'delta_o_ln_g', 'delta_o_ln_b', 'delta_o_w_out', 'delta_o_norm_post', 'new_m_e_norm_pre', 'new_m_e_w_in', 'new_m_e_conv_w', 'new_m_e_conv_b', 'new_m_e_dt_bias', 'new_m_e_a_log', 'new_m_e_d_skip', 'new_m_e_fgate_b', 'new_m_e_ssd_norm', 'new_m_e_w_out', 'new_m_e_norm_post', 'new_m_o_norm_pre', 'new_m_o_w_in', 'new_m_o_conv_w', 'new_m_o_conv_b', 'new_m_o_ln_g', 'new_m_o_ln_b', 'new_m_o_w_out', 'new_m_o_norm_post', 'new_v_e_norm_pre', 'new_v_e_w_in', 'new_v_e_conv_w', 'new_v_e_conv_b', 'new_v_e_dt_bias', 'new_v_e_a_log', 'new_v_e_d_skip', 'new_v_e_fgate_b', 'new_v_e_ssd_norm', 'new_v_e_w_out', 'new_v_e_norm_post', 'new_v_o_norm_pre', 'new_v_o_w_in', 'new_v_o_conv_w', 'new_v_o_conv_b', 'new_v_o_ln_g', 'new_v_o_ln_b', 'new_v_o_w_out', 'new_v_o_norm_post']
TWIN_LEAF_KINDS = {'loss': 'loss', 'grad_x': 'grad_x', 'grad_e_norm_pre': 'grad_w', 'grad_e_w_in': 'grad_w', 'grad_e_conv_w': 'grad_w', 'grad_e_conv_b': 'grad_w', 'grad_e_dt_bias': 'grad_w', 'grad_e_a_log': 'grad_w', 'grad_e_d_skip': 'grad_w', 'grad_e_fgate_b': 'grad_w', 'grad_e_ssd_norm': 'grad_w', 'grad_e_w_out': 'grad_w', 'grad_e_norm_post': 'grad_w', 'grad_o_norm_pre': 'grad_w', 'grad_o_w_in': 'grad_w', 'grad_o_conv_w': 'grad_w', 'grad_o_conv_b': 'grad_w', 'grad_o_ln_g': 'grad_w', 'grad_o_ln_b': 'grad_w', 'grad_o_w_out': 'grad_w', 'grad_o_norm_post': 'grad_w', 'delta_e_norm_pre': 'delta_w', 'delta_e_w_in': 'delta_w', 'delta_e_conv_w': 'delta_w', 'delta_e_conv_b': 'delta_w', 'delta_e_dt_bias': 'delta_w', 'delta_e_a_log': 'delta_w', 'delta_e_d_skip': 'delta_w', 'delta_e_fgate_b': 'delta_w', 'delta_e_ssd_norm': 'delta_w', 'delta_e_w_out': 'delta_w', 'delta_e_norm_post': 'delta_w', 'delta_o_norm_pre': 'delta_w', 'delta_o_w_in': 'delta_w', 'delta_o_conv_w': 'delta_w', 'delta_o_conv_b': 'delta_w', 'delta_o_ln_g': 'delta_w', 'delta_o_ln_b': 'delta_w', 'delta_o_w_out': 'delta_w', 'delta_o_norm_post': 'delta_w', 'new_m_e_norm_pre': 'new_m', 'new_m_e_w_in': 'new_m', 'new_m_e_conv_w': 'new_m', 'new_m_e_conv_b': 'new_m', 'new_m_e_dt_bias': 'new_m', 'new_m_e_a_log': 'new_m', 'new_m_e_d_skip': 'new_m', 'new_m_e_fgate_b': 'new_m', 'new_m_e_ssd_norm': 'new_m', 'new_m_e_w_out': 'new_m', 'new_m_e_norm_post': 'new_m', 'new_m_o_norm_pre': 'new_m', 'new_m_o_w_in': 'new_m', 'new_m_o_conv_w': 'new_m', 'new_m_o_conv_b': 'new_m', 'new_m_o_ln_g': 'new_m', 'new_m_o_ln_b': 'new_m', 'new_m_o_w_out': 'new_m', 'new_m_o_norm_post': 'new_m', 'new_v_e_norm_pre': 'new_v', 'new_v_e_w_in': 'new_v', 'new_v_e_conv_w': 'new_v', 'new_v_e_conv_b': 'new_v', 'new_v_e_dt_bias': 'new_v', 'new_v_e_a_log': 'new_v', 'new_v_e_d_skip': 'new_v', 'new_v_e_fgate_b': 'new_v', 'new_v_e_ssd_norm': 'new_v', 'new_v_e_w_out': 'new_v', 'new_v_e_norm_post': 'new_v', 'new_v_o_norm_pre': 'new_v', 'new_v_o_w_in': 'new_v', 'new_v_o_conv_w': 'new_v', 'new_v_o_conv_b': 'new_v', 'new_v_o_ln_g': 'new_v', 'new_v_o_ln_b': 'new_v', 'new_v_o_w_out': 'new_v', 'new_v_o_norm_post': 'new_v'}


def _forward(args):
    return _fwd_reference(*[args[k] for k in FWD_PARAMS])


def _output_shape():
    def fwd():
        inp = _fwd_setup_inputs(0)
        return _fwd_reference(*[inp[k] for k in FWD_PARAMS])
    out = _jax.eval_shape(fwd)
    return out.shape, out.dtype

N_MICROBATCH = 1
ADAM_LR = 0.001
ADAM_B1 = 0.9
ADAM_B2 = 0.999
ADAM_EPS = 1e-08
ADAM_WD = 0.01
ADAM_STEP = 10
PER_EXAMPLE_BATCH_AXIS = {'x': 0, 'loss_target': 0}
SHARED_INPUTS = []
_WEIGHT_DTYPES = {'e_norm_pre': _jnp.float32, 'e_w_in': _jnp.float32, 'e_conv_w': _jnp.float32, 'e_conv_b': _jnp.float32, 'e_dt_bias': _jnp.float32, 'e_a_log': _jnp.float32, 'e_d_skip': _jnp.float32, 'e_fgate_b': _jnp.float32, 'e_ssd_norm': _jnp.float32, 'e_w_out': _jnp.float32, 'e_norm_post': _jnp.float32, 'o_norm_pre': _jnp.float32, 'o_w_in': _jnp.float32, 'o_conv_w': _jnp.float32, 'o_conv_b': _jnp.float32, 'o_ln_g': _jnp.float32, 'o_ln_b': _jnp.float32, 'o_w_out': _jnp.float32, 'o_norm_post': _jnp.float32}
MOMENT_SCALE = {'e_norm_pre': 1.138338e+00, 'e_w_in': 4.158289e-01, 'e_conv_w': 7.125930e-01, 'e_conv_b': 2.245198e+00, 'e_dt_bias': 1.611077e+00, 'e_a_log': 4.421438e+00, 'e_d_skip': 1.004767e+01, 'e_fgate_b': 7.440676e-01, 'e_ssd_norm': 1.402445e+00, 'e_w_out': 1.353053e+00, 'e_norm_post': 6.380354e+01, 'o_norm_pre': 1.063055e+00, 'o_w_in': 3.883509e-01, 'o_conv_w': 4.155467e-01, 'o_conv_b': 4.099148e+00, 'o_ln_g': 1.564137e+00, 'o_ln_b': 2.426004e+00, 'o_w_out': 1.262142e+00, 'o_norm_post': 6.384428e+01}


def _to_microbatches(a, axis):
    t = _jnp.moveaxis(a, axis, 0)
    t = t.reshape((N_MICROBATCH, t.shape[0] // N_MICROBATCH) + t.shape[1:])
    return _jnp.moveaxis(t, 1, axis + 1)


def setup_inputs(seed: int = 0) -> dict:
    inp = _fwd_setup_inputs(seed)
    key = _jax.random.fold_in(_jax.random.key(seed), 7919)
    shape, _ = _output_shape()
    out = dict(inp)
    out["loss_target"] = _jax.random.normal(_jax.random.fold_in(key, 0), shape, _jnp.float32)
    for i, name in enumerate(TWIN_WEIGHTS):
        w = inp[name].astype(_jnp.float32)
        if MOMENT_SCALE is None:
            s = _jnp.sqrt(_jnp.mean(_jnp.square(w)) + 1e-30)
        else:
            s = MOMENT_SCALE[name]
        km, kv = _jax.random.split(_jax.random.fold_in(key, i + 1))
        out[name] = w
        out["m_" + name] = s * _jax.random.normal(km, w.shape, _jnp.float32)
        out["v_" + name] = (s * s) * _jax.random.uniform(kv, w.shape, _jnp.float32, 0.5, 1.5)
    if N_MICROBATCH > 1:
        for name, axis in PER_EXAMPLE_BATCH_AXIS.items():
            out[name] = _to_microbatches(out[name], axis)
    return {'x': out['x'], 'e_norm_pre': out['e_norm_pre'], 'e_w_in': out['e_w_in'], 'e_conv_w': out['e_conv_w'], 'e_conv_b': out['e_conv_b'], 'e_dt_bias': out['e_dt_bias'], 'e_a_log': out['e_a_log'], 'e_d_skip': out['e_d_skip'], 'e_fgate_b': out['e_fgate_b'], 'e_ssd_norm': out['e_ssd_norm'], 'e_w_out': out['e_w_out'], 'e_norm_post': out['e_norm_post'], 'o_norm_pre': out['o_norm_pre'], 'o_w_in': out['o_w_in'], 'o_conv_w': out['o_conv_w'], 'o_conv_b': out['o_conv_b'], 'o_ln_g': out['o_ln_g'], 'o_ln_b': out['o_ln_b'], 'o_w_out': out['o_w_out'], 'o_norm_post': out['o_norm_post'], 'loss_target': out['loss_target'], 'm_e_norm_pre': out['m_e_norm_pre'], 'm_e_w_in': out['m_e_w_in'], 'm_e_conv_w': out['m_e_conv_w'], 'm_e_conv_b': out['m_e_conv_b'], 'm_e_dt_bias': out['m_e_dt_bias'], 'm_e_a_log': out['m_e_a_log'], 'm_e_d_skip': out['m_e_d_skip'], 'm_e_fgate_b': out['m_e_fgate_b'], 'm_e_ssd_norm': out['m_e_ssd_norm'], 'm_e_w_out': out['m_e_w_out'], 'm_e_norm_post': out['m_e_norm_post'], 'm_o_norm_pre': out['m_o_norm_pre'], 'm_o_w_in': out['m_o_w_in'], 'm_o_conv_w': out['m_o_conv_w'], 'm_o_conv_b': out['m_o_conv_b'], 'm_o_ln_g': out['m_o_ln_g'], 'm_o_ln_b': out['m_o_ln_b'], 'm_o_w_out': out['m_o_w_out'], 'm_o_norm_post': out['m_o_norm_post'], 'v_e_norm_pre': out['v_e_norm_pre'], 'v_e_w_in': out['v_e_w_in'], 'v_e_conv_w': out['v_e_conv_w'], 'v_e_conv_b': out['v_e_conv_b'], 'v_e_dt_bias': out['v_e_dt_bias'], 'v_e_a_log': out['v_e_a_log'], 'v_e_d_skip': out['v_e_d_skip'], 'v_e_fgate_b': out['v_e_fgate_b'], 'v_e_ssd_norm': out['v_e_ssd_norm'], 'v_e_w_out': out['v_e_w_out'], 'v_e_norm_post': out['v_e_norm_post'], 'v_o_norm_pre': out['v_o_norm_pre'], 'v_o_w_in': out['v_o_w_in'], 'v_o_conv_w': out['v_o_conv_w'], 'v_o_conv_b': out['v_o_conv_b'], 'v_o_ln_g': out['v_o_ln_g'], 'v_o_ln_b': out['v_o_ln_b'], 'v_o_w_out': out['v_o_w_out'], 'v_o_norm_post': out['v_o_norm_post']}


def _loss(weights, diff, rest, loss_target):
    with _jax.named_scope("forward"):
        args = {**rest, TWIN_DIFF_INPUT: diff, **{k: w.astype(_WEIGHT_DTYPES[k]) for k, w in weights.items()}}
        y = _forward(args)
    with _jax.named_scope("loss_head"):
        err = _jnp.square(y.astype(_jnp.float32) - loss_target)
        return 0.5 * _jnp.sum(_jnp.mean(err, axis=-1)) if err.ndim else 0.5 * err


def _adamw(w, g, m, v):
    m = ADAM_B1 * m + (1.0 - ADAM_B1) * g
    v = ADAM_B2 * v + (1.0 - ADAM_B2) * _jnp.square(g)
    m_hat = m / (1.0 - ADAM_B1 ** ADAM_STEP)
    v_hat = v / (1.0 - ADAM_B2 ** ADAM_STEP)
    delta = -ADAM_LR * (m_hat / (_jnp.sqrt(v_hat) + ADAM_EPS) + ADAM_WD * w)
    return delta, m, v


def reference(x, e_norm_pre, e_w_in, e_conv_w, e_conv_b, e_dt_bias, e_a_log, e_d_skip, e_fgate_b, e_ssd_norm, e_w_out, e_norm_post, o_norm_pre, o_w_in, o_conv_w, o_conv_b, o_ln_g, o_ln_b, o_w_out, o_norm_post, loss_target, m_e_norm_pre, m_e_w_in, m_e_conv_w, m_e_conv_b, m_e_dt_bias, m_e_a_log, m_e_d_skip, m_e_fgate_b, m_e_ssd_norm, m_e_w_out, m_e_norm_post, m_o_norm_pre, m_o_w_in, m_o_conv_w, m_o_conv_b, m_o_ln_g, m_o_ln_b, m_o_w_out, m_o_norm_post, v_e_norm_pre, v_e_w_in, v_e_conv_w, v_e_conv_b, v_e_dt_bias, v_e_a_log, v_e_d_skip, v_e_fgate_b, v_e_ssd_norm, v_e_w_out, v_e_norm_post, v_o_norm_pre, v_o_w_in, v_o_conv_w, v_o_conv_b, v_o_ln_g, v_o_ln_b, v_o_w_out, v_o_norm_post):
    given = dict(x=x, e_norm_pre=e_norm_pre, e_w_in=e_w_in, e_conv_w=e_conv_w, e_conv_b=e_conv_b, e_dt_bias=e_dt_bias, e_a_log=e_a_log, e_d_skip=e_d_skip, e_fgate_b=e_fgate_b, e_ssd_norm=e_ssd_norm, e_w_out=e_w_out, e_norm_post=e_norm_post, o_norm_pre=o_norm_pre, o_w_in=o_w_in, o_conv_w=o_conv_w, o_conv_b=o_conv_b, o_ln_g=o_ln_g, o_ln_b=o_ln_b, o_w_out=o_w_out, o_norm_post=o_norm_post, loss_target=loss_target, m_e_norm_pre=m_e_norm_pre, m_e_w_in=m_e_w_in, m_e_conv_w=m_e_conv_w, m_e_conv_b=m_e_conv_b, m_e_dt_bias=m_e_dt_bias, m_e_a_log=m_e_a_log, m_e_d_skip=m_e_d_skip, m_e_fgate_b=m_e_fgate_b, m_e_ssd_norm=m_e_ssd_norm, m_e_w_out=m_e_w_out, m_e_norm_post=m_e_norm_post, m_o_norm_pre=m_o_norm_pre, m_o_w_in=m_o_w_in, m_o_conv_w=m_o_conv_w, m_o_conv_b=m_o_conv_b, m_o_ln_g=m_o_ln_g, m_o_ln_b=m_o_ln_b, m_o_w_out=m_o_w_out, m_o_norm_post=m_o_norm_post, v_e_norm_pre=v_e_norm_pre, v_e_w_in=v_e_w_in, v_e_conv_w=v_e_conv_w, v_e_conv_b=v_e_conv_b, v_e_dt_bias=v_e_dt_bias, v_e_a_log=v_e_a_log, v_e_d_skip=v_e_d_skip, v_e_fgate_b=v_e_fgate_b, v_e_ssd_norm=v_e_ssd_norm, v_e_w_out=v_e_w_out, v_e_norm_post=v_e_norm_post, v_o_norm_pre=v_o_norm_pre, v_o_w_in=v_o_w_in, v_o_conv_w=v_o_conv_w, v_o_conv_b=v_o_conv_b, v_o_ln_g=v_o_ln_g, v_o_ln_b=v_o_ln_b, v_o_w_out=v_o_w_out, v_o_norm_post=v_o_norm_post)
    weights = {n: given[n] for n in TWIN_WEIGHTS}
    shared = {n: given[n] for n in SHARED_INPUTS}
    per_example = {n: given[n] for n in ['x']}
    grad_fn = _jax.value_and_grad(_loss, argnums=(0, 1))

    def one_microbatch(ex, loss_target):
        ex = dict(ex)
        diff = ex.pop(TWIN_DIFF_INPUT)
        return grad_fn(weights, diff, {**shared, **ex}, loss_target)

    if N_MICROBATCH == 1:
        loss, (grad_w, grad_x) = one_microbatch(per_example, given["loss_target"])
    else:
        def body(carry, xs):
            loss_sum, grad_sum = carry
            l_k, (gw_k, gx_k) = one_microbatch(xs[0], xs[1])
            with _jax.named_scope("update"):
                return (loss_sum + l_k, _jax.tree.map(_jnp.add, grad_sum, gw_k)), gx_k

        init = (_jnp.zeros((), _jnp.float32), _jax.tree.map(_jnp.zeros_like, weights))
        (loss, grad_w), grad_x = _jax.lax.scan(body, init, (per_example, given["loss_target"]))
    with _jax.named_scope("update"):
        delta_w, new_m, new_v = {}, {}, {}
        for n in TWIN_WEIGHTS:
            delta_w[n], new_m[n], new_v[n] = _adamw(weights[n], grad_w[n], given["m_" + n], given["v_" + n])
    return (loss, grad_x, *[grad_w[n] for n in TWIN_WEIGHTS], *[delta_w[n] for n in TWIN_WEIGHTS],
            *[new_m[n] for n in TWIN_WEIGHTS], *[new_v[n] for n in TWIN_WEIGHTS])
```

```python
import jax
import jax.numpy as jnp
from jax import lax
from jax.experimental import pallas as pl
from jax.experimental.pallas import tpu as pltpu

f32 = jnp.float32
bf16 = jnp.bfloat16

N_DEV = 8
D_MODEL = 1024
N_HEADS = 16
HEAD_DIM = 64
N_GROUPS = 4
HEADS_PER_GROUP = 4
D_STATE = 128
CHUNK = 128
SSD_CONV = 4
CONV_WIDTH = 31
D_CONV = 2048
EPS = 1e-6
XBC_W = 2048
B_OFF = 1024
C_OFF = 1536
F_LANE = 16
HALO = 32

ADAM_LR = 0.001
ADAM_B1 = 0.9
ADAM_B2 = 0.999
ADAM_EPS = 1e-08
ADAM_WD = 0.01
ADAM_STEP = 10

VMEM_LIMIT_BYTES = 56 * 1024 * 1024
ROW_TILE = 512
CONV_ROW_TILE = 256
CONV_COL_TILE = 512
ATTN_TILE = 256

NT = (((1,), (1,)), ((), ()))
TN = (((0,), (0,)), ((), ()))
HIGHEST = lax.Precision.HIGHEST
NEG = -1e30


def _cp(*sem):
    return pltpu.CompilerParams(dimension_semantics=sem if sem else None, vmem_limit_bytes=VMEM_LIMIT_BYTES)


def _sigmoid(x):
    return jax.nn.sigmoid(x)


def _silu(x):
    return x * _sigmoid(x)


def _dsilu(x):
    s = _sigmoid(x)
    return s * (1.0 + x * (1.0 - s))


def _softplus(x):
    return jnp.maximum(x, 0.0) + jnp.log(1.0 + jnp.exp(-jnp.abs(x)))


def _log_sigmoid(x):
    return jnp.minimum(x, 0.0) - jnp.log(1.0 + jnp.exp(-jnp.abs(x)))


def _dot(a, b, dims=None, precision=None):
    if dims is None:
        return jnp.dot(a, b, preferred_element_type=f32, precision=precision)
    return lax.dot_general(a, b, dims, preferred_element_type=f32, precision=precision)


def _mm_nn(a, b, out_dtype, name, tm=512, tn=1024):
    m, k = a.shape
    n = b.shape[1]
    tm, tn = min(tm, m), min(tn, n)

    def body(a_ref, b_ref, o_ref):
        o_ref[...] = _dot(a_ref[...], b_ref[...]).astype(o_ref.dtype)

    return pl.pallas_call(
        body, name=name, grid=(n // tn, m // tm),
        in_specs=[pl.BlockSpec((tm, k), lambda j, i: (i, 0)), pl.BlockSpec((k, tn), lambda j, i: (0, j))],
        out_specs=pl.BlockSpec((tm, tn), lambda j, i: (i, j)),
        out_shape=jax.ShapeDtypeStruct((m, n), out_dtype), compiler_params=_cp("parallel", "parallel"))(a, b)


def _mm_nt(pairs, out_dtype, name, tm=512, tn=512):
    m = pairs[0][0].shape[0]
    n = pairs[0][2].shape[0]
    tm, tn = min(tm, m), min(tn, n)
    npair = len(pairs)

    def body(*refs):
        o_ref = refs[-1]
        acc = None
        for p in range(npair):
            d = _dot(refs[2 * p][...].astype(bf16), refs[2 * p + 1][...], NT)
            acc = d if acc is None else acc + d
        o_ref[...] = acc.astype(o_ref.dtype)

    in_specs, args = [], []
    for a, acb, b, bcb, k in pairs:
        in_specs.append(pl.BlockSpec((tm, k), lambda j, i, acb=acb: (i, acb)))
        in_specs.append(pl.BlockSpec((tn, k), lambda j, i, bcb=bcb: (j, bcb)))
        args += [a, b]
    return pl.pallas_call(
        body, name=name, grid=(n // tn, m // tm), in_specs=in_specs,
        out_specs=pl.BlockSpec((tm, tn), lambda j, i: (i, j)),
        out_shape=jax.ShapeDtypeStruct((m, n), out_dtype), compiler_params=_cp("parallel", "parallel"))(*args)


def _mm_tn(a, b, name, a_cb=0, am=None, b_cb=0, bn=None, tn=1024, tk=512, blocked=False):
    k = a.shape[0]
    am = a.shape[1] if am is None else am
    bn = b.shape[1] if bn is None else bn
    tm = min(1024, am)
    tn, tk = min(tn, bn), min(tk, k)
    a_off, b_off = a_cb * (am // tm), b_cb * (bn // tn)

    def body(a_ref, b_ref, o_ref):
        @pl.when(pl.program_id(2) == 0)
        def _():
            o_ref[...] = jnp.zeros_like(o_ref)
        d = _dot(a_ref[...].astype(bf16), b_ref[...].astype(bf16), TN)
        o_ref[...] += d.reshape(o_ref.shape)

    if blocked:
        out_spec = pl.BlockSpec((1, tm, tn), lambda i, j, kk: (j, i, 0))
        out_shape = jax.ShapeDtypeStruct((bn // tn, am, tn), f32)
    else:
        out_spec = pl.BlockSpec((tm, tn), lambda i, j, kk: (i, j))
        out_shape = jax.ShapeDtypeStruct((am, bn), f32)
    return pl.pallas_call(
        body, name=name, grid=(am // tm, bn // tn, k // tk),
        in_specs=[pl.BlockSpec((tk, tm), lambda i, j, kk: (kk, a_off + i)),
                  pl.BlockSpec((tk, tn), lambda i, j, kk: (kk, b_off + j))],
        out_specs=out_spec, out_shape=out_shape,
        compiler_params=_cp("parallel", "parallel", "arbitrary"))(a, b)


def _rowspec(ts, w, cb=0):
    return pl.BlockSpec((ts, w), lambda i: (i, cb))


def _vecspec(w):
    return pl.BlockSpec((1, w), lambda i: (0, 0))


def _rms_fwd(x, g, name):
    s, d = x.shape
    ts = min(ROW_TILE, s)

    def body(x_ref, g_ref, u_ref):
        xv = x_ref[...]
        r = lax.rsqrt(jnp.mean(xv * xv, axis=-1, keepdims=True) + EPS)
        u_ref[...] = (xv * r * g_ref[...]).astype(bf16)

    return pl.pallas_call(
        body, name=name, grid=(s // ts,), in_specs=[_rowspec(ts, d), _vecspec(d)], out_specs=_rowspec(ts, d),
        out_shape=jax.ShapeDtypeStruct((s, d), bf16), compiler_params=_cp("parallel"))(x, g)


def _rms_bwd_vals(xv, g, dy):
    r = lax.rsqrt(jnp.mean(xv * xv, axis=-1, keepdims=True) + EPS)
    xh = xv * r
    dg = jnp.sum(dy * xh, axis=0, keepdims=True)
    dxh = dy * g
    dx = r * (dxh - xh * jnp.mean(dxh * xh, axis=-1, keepdims=True))
    return dx, dg


def _gate0_fwd(y, z, o, ssd_norm):
    s = y.shape[0]
    ts = min(ROW_TILE, s)
    gw = D_MODEL // N_GROUPS

    def body(y_ref, zs_ref, zf_ref, o_ref, w_ref, cat_ref):
        yg = y_ref[...].astype(f32) * _silu(zs_ref[...].astype(f32))
        for g in range(N_GROUPS):
            seg = yg[:, gw * g:gw * (g + 1)]
            r = lax.rsqrt(jnp.mean(seg * seg, axis=-1, keepdims=True) + EPS)
            cat_ref[:, gw * g:gw * (g + 1)] = (seg * r * w_ref[:, gw * g:gw * (g + 1)]).astype(bf16)
        cat_ref[:, D_MODEL:] = (o_ref[...].astype(f32) * _silu(zf_ref[...].astype(f32))).astype(bf16)

    return pl.pallas_call(
        body, name="gate0_fwd", grid=(s // ts,),
        in_specs=[_rowspec(ts, D_MODEL), _rowspec(ts, D_MODEL, 0), _rowspec(ts, D_MODEL, 1), _rowspec(ts, D_MODEL),
                  _vecspec(D_MODEL)],
        out_specs=_rowspec(ts, 2 * D_MODEL),
        out_shape=jax.ShapeDtypeStruct((s, 2 * D_MODEL), bf16), compiler_params=_cp("parallel"))(y, z, z, o, ssd_norm)


def _post0_pre1(x, out0, g_post0, g_pre1):
    s, d = x.shape
    ts = min(ROW_TILE, s)

    def body(x_ref, o_ref, gp_ref, gn_ref, x1_ref, u1_ref):
        ov = o_ref[...]
        r = lax.rsqrt(jnp.mean(ov * ov, axis=-1, keepdims=True) + EPS)
        x1 = x_ref[...] + ov * r * gp_ref[...]
        x1_ref[...] = x1
        r1 = lax.rsqrt(jnp.mean(x1 * x1, axis=-1, keepdims=True) + EPS)
        u1_ref[...] = (x1 * r1 * gn_ref[...]).astype(bf16)

    return pl.pallas_call(
        body, name="post0_pre1", grid=(s // ts,),
        in_specs=[_rowspec(ts, d), _rowspec(ts, d), _vecspec(d), _vecspec(d)],
        out_specs=[_rowspec(ts, d), _rowspec(ts, d)],
        out_shape=[jax.ShapeDtypeStruct((s, d), f32), jax.ShapeDtypeStruct((s, d), bf16)],
        compiler_params=_cp("parallel"))(x, out0, g_post0, g_pre1)


def _ln_vals(hc, g, b):
    mu = jnp.mean(hc, axis=-1, keepdims=True)
    xc = hc - mu
    rstd = lax.rsqrt(jnp.mean(xc * xc, axis=-1, keepdims=True) + EPS)
    xh = xc * rstd
    return xh, rstd, xh * g + b


def _ln_gate_fwd(hc, proj1, ln_g, ln_b):
    s = hc.shape[0]
    ts = min(ROW_TILE, s)

    def body(hc_ref, z_ref, g_ref, b_ref, h3_ref):
        _, _, ln = _ln_vals(hc_ref[...].astype(f32), g_ref[...], b_ref[...])
        h3_ref[...] = (_silu(ln) * _silu(z_ref[...].astype(f32))).astype(bf16)

    return pl.pallas_call(
        body, name="ln_gate_fwd", grid=(s // ts,),
        in_specs=[_rowspec(ts, D_CONV), _rowspec(ts, D_CONV, 2), _vecspec(D_CONV), _vecspec(D_CONV)],
        out_specs=_rowspec(ts, D_CONV),
        out_shape=jax.ShapeDtypeStruct((s, D_CONV), bf16), compiler_params=_cp("parallel"))(hc, proj1, ln_g, ln_b)


def _final_loss(x1, out1, tgt, g_post1):
    s, d = x1.shape
    ts = min(ROW_TILE, s)

    def body(x1_ref, o_ref, t_ref, g_ref, dy_ref, do_ref, dg_ref, loss_ref):
        i = pl.program_id(0)

        @pl.when(i == 0)
        def _():
            dg_ref[...] = jnp.zeros_like(dg_ref)
            loss_ref[...] = jnp.zeros_like(loss_ref)
        ov = o_ref[...]
        g = g_ref[...]
        r = lax.rsqrt(jnp.mean(ov * ov, axis=-1, keepdims=True) + EPS)
        diff = x1_ref[...] + ov * r * g - t_ref[...]
        row = jnp.mean(diff * diff, axis=-1, keepdims=True)
        loss_ref[...] += jnp.broadcast_to(0.5 * jnp.sum(row, axis=0, keepdims=True), loss_ref.shape)
        dy = diff * (1.0 / d)
        dy_ref[...] = dy
        dx, dg = _rms_bwd_vals(ov, g, dy)
        do_ref[...] = dx.astype(bf16)
        dg_ref[...] += dg

    return pl.pallas_call(
        body, name="final_loss", grid=(s // ts,),
        in_specs=[_rowspec(ts, d), _rowspec(ts, d), _rowspec(ts, d), _vecspec(d)],
        out_specs=[_rowspec(ts, d), _rowspec(ts, d), _vecspec(d), _vecspec(128)],
        out_shape=[jax.ShapeDtypeStruct((s, d), f32), jax.ShapeDtypeStruct((s, d), bf16),
                   jax.ShapeDtypeStruct((1, d), f32), jax.ShapeDtypeStruct((1, 128), f32)],
        compiler_params=_cp("arbitrary"))(x1, out1, tgt, g_post1)


def _ln_gate_bwd(hc, proj1, dh3, ln_g, ln_b):
    s = hc.shape[0]
    ts = min(ROW_TILE, s)

    def body(hc_ref, z_ref, dh_ref, g_ref, b_ref, dhc_ref, dz_ref, dg_ref, db_ref):
        @pl.when(pl.program_id(0) == 0)
        def _():
            dg_ref[...] = jnp.zeros_like(dg_ref)
            db_ref[...] = jnp.zeros_like(db_ref)
        g = g_ref[...]
        xh, rstd, ln = _ln_vals(hc_ref[...].astype(f32), g, b_ref[...])
        zv = z_ref[...].astype(f32)
        dh3 = dh_ref[...].astype(f32)
        dz_ref[...] = (dh3 * _silu(ln) * _dsilu(zv)).astype(bf16)
        dln = dh3 * _silu(zv) * _dsilu(ln)
        dg_ref[...] += jnp.sum(dln * xh, axis=0, keepdims=True)
        db_ref[...] += jnp.sum(dln, axis=0, keepdims=True)
        dxh = dln * g
        dhc = rstd * (dxh - jnp.mean(dxh, axis=-1, keepdims=True) - xh * jnp.mean(dxh * xh, axis=-1, keepdims=True))
        dhc_ref[...] = dhc.astype(bf16)

    return pl.pallas_call(
        body, name="ln_gate_bwd", grid=(s // ts,),
        in_specs=[_rowspec(ts, D_CONV), _rowspec(ts, D_CONV, 2), _rowspec(ts, D_CONV), _vecspec(D_CONV),
                  _vecspec(D_CONV)],
        out_specs=[_rowspec(ts, D_CONV), _rowspec(ts, D_CONV), _vecspec(D_CONV), _vecspec(D_CONV)],
        out_shape=[jax.ShapeDtypeStruct((s, D_CONV), bf16), jax.ShapeDtypeStruct((s, D_CONV), bf16),
                   jax.ShapeDtypeStruct((1, D_CONV), f32), jax.ShapeDtypeStruct((1, D_CONV), f32)],
        compiler_params=_cp("arbitrary"))(hc, proj1, dh3, ln_g, ln_b)


def _mid_bwd(x1, du1, dy, out0, g_pre1, g_post0):
    s, d = x1.shape
    ts = min(ROW_TILE, s)

    def body(x1_ref, du_ref, dy_ref, o_ref, gn_ref, gp_ref, dx1_ref, do_ref, dgn_ref, dgp_ref):
        @pl.when(pl.program_id(0) == 0)
        def _():
            dgn_ref[...] = jnp.zeros_like(dgn_ref)
            dgp_ref[...] = jnp.zeros_like(dgp_ref)
        dxa, dgn = _rms_bwd_vals(x1_ref[...], gn_ref[...], du_ref[...])
        dx1 = dy_ref[...] + dxa
        dx1_ref[...] = dx1
        dgn_ref[...] += dgn
        dxo, dgp = _rms_bwd_vals(o_ref[...], gp_ref[...], dx1)
        do_ref[...] = dxo.astype(bf16)
        dgp_ref[...] += dgp

    return pl.pallas_call(
        body, name="mid_bwd", grid=(s // ts,),
        in_specs=[_rowspec(ts, d)] * 4 + [_vecspec(d), _vecspec(d)],
        out_specs=[_rowspec(ts, d), _rowspec(ts, d), _vecspec(d), _vecspec(d)],
        out_shape=[jax.ShapeDtypeStruct((s, d), f32), jax.ShapeDtypeStruct((s, d), bf16),
                   jax.ShapeDtypeStruct((1, d), f32), jax.ShapeDtypeStruct((1, d), f32)],
        compiler_params=_cp("arbitrary"))(x1, du1, dy, out0, g_pre1, g_post0)


def _first_bwd(x, du0, dx1, g_pre0):
    s, d = x.shape
    ts = min(ROW_TILE, s)

    def body(x_ref, du_ref, dx1_ref, g_ref, dx_ref, dg_ref):
        @pl.when(pl.program_id(0) == 0)
        def _():
            dg_ref[...] = jnp.zeros_like(dg_ref)
        dxa, dg = _rms_bwd_vals(x_ref[...], g_ref[...], du_ref[...])
        dx_ref[...] = dx1_ref[...] + dxa
        dg_ref[...] += dg

    return pl.pallas_call(
        body, name="first_bwd", grid=(s // ts,),
        in_specs=[_rowspec(ts, d)] * 3 + [_vecspec(d)],
        out_specs=[_rowspec(ts, d), _vecspec(d)],
        out_shape=[jax.ShapeDtypeStruct((s, d), f32), jax.ShapeDtypeStruct((1, d), f32)],
        compiler_params=_cp("arbitrary"))(x, du0, dx1, g_pre0)


def _gate0_bwd(y, z, o, dcat, ssd_norm):
    s = y.shape[0]
    ts = min(ROW_TILE, s)
    gw = D_MODEL // N_GROUPS

    def body(y_ref, zs_ref, zf_ref, o_ref, dn_ref, dg_ref, w_ref, dy_ref, do_ref, dz_ref, delta_ref, dw_ref):
        @pl.when(pl.program_id(0) == 0)
        def _():
            dw_ref[...] = jnp.zeros_like(dw_ref)
        yv = y_ref[...].astype(f32)
        zs = zs_ref[...].astype(f32)
        sz = _silu(zs)
        yg = yv * sz
        dyn = dn_ref[...].astype(f32)
        for g in range(N_GROUPS):
            sl = slice(gw * g, gw * (g + 1))
            seg = yg[:, sl]
            r = lax.rsqrt(jnp.mean(seg * seg, axis=-1, keepdims=True) + EPS)
            yh = seg * r
            dn = dyn[:, sl]
            dw_ref[:, sl] += jnp.sum(dn * yh, axis=0, keepdims=True)
            dyh = dn * w_ref[:, sl]
            dyg = r * (dyh - yh * jnp.mean(dyh * yh, axis=-1, keepdims=True))
            dy_ref[:, sl] = (dyg * sz[:, sl]).astype(bf16)
            dz_ref[:, sl] = (dyg * yv[:, sl] * _dsilu(zs[:, sl])).astype(bf16)
        zf = zf_ref[...].astype(f32)
        ov = o_ref[...].astype(f32)
        dog = dg_ref[...].astype(f32)
        dov = (dog * _silu(zf)).astype(bf16)
        do_ref[...] = dov
        dz_ref[:, D_MODEL:] = (dog * ov * _dsilu(zf)).astype(bf16)
        prod = dov.astype(f32) * ov
        lane = lax.broadcasted_iota(jnp.int32, (ts, 128), 1)
        delta = jnp.zeros((ts, 128), f32)
        for h in range(N_HEADS):
            dh = jnp.sum(prod[:, HEAD_DIM * h:HEAD_DIM * (h + 1)], axis=-1, keepdims=True)
            delta = delta + jnp.where(lane == h, dh, 0.0)
        delta_ref[...] = delta

    return pl.pallas_call(
        body, name="gate0_bwd", grid=(s // ts,),
        in_specs=[_rowspec(ts, D_MODEL), _rowspec(ts, D_MODEL, 0), _rowspec(ts, D_MODEL, 1), _rowspec(ts, D_MODEL),
                  _rowspec(ts, D_MODEL, 0), _rowspec(ts, D_MODEL, 1), _vecspec(D_MODEL)],
        out_specs=[_rowspec(ts, D_MODEL), _rowspec(ts, D_MODEL), _rowspec(ts, 2 * D_MODEL), _rowspec(ts, 128),
                   _vecspec(D_MODEL)],
        out_shape=[jax.ShapeDtypeStruct((s, D_MODEL), bf16), jax.ShapeDtypeStruct((s, D_MODEL), bf16),
                   jax.ShapeDtypeStruct((s, 2 * D_MODEL), bf16), jax.ShapeDtypeStruct((s, 128), f32),
                   jax.ShapeDtypeStruct((1, D_MODEL), f32)],
        compiler_params=_cp("arbitrary"))(y, z, z, o, dcat, dcat, ssd_norm)


def _conv_grid(s, c):
    ts, cb = min(CONV_ROW_TILE, s), min(CONV_COL_TILE, c)
    return ts, cb, (c // cb, s // ts)


def _cur(ts, cb, off=0):
    return pl.BlockSpec((ts, cb), lambda c, i: (i, c + off))


def _prev_halo(ts, cb, off=0):
    return pl.BlockSpec((HALO, cb), lambda c, i: (jnp.maximum(i * (ts // HALO) - 1, 0), c + off))


def _next_halo(ts, cb, s, off=0):
    return pl.BlockSpec((HALO, cb), lambda c, i: (jnp.minimum((i + 1) * (ts // HALO), s // HALO - 1), c + off))


def _wspec(k, cb):
    return pl.BlockSpec((k, cb), lambda c, i: (0, c))


def _conv_taps(ext_ref, w_ref, ts, k_taps, base):
    acc = None
    for k in range(k_taps):
        t = w_ref[k:k + 1, :] * ext_ref[pl.ds(base + k, ts), :]
        acc = t if acc is None else acc + t
    return acc


def _conv_ssd_fwd(xraw, w, b):
    s, c = xraw.shape
    ts, cb, grid = _conv_grid(s, c)

    def body(x_ref, xh_ref, w_ref, b_ref, pre_ref, act_ref, ext_ref):
        first = pl.program_id(1) == 0
        ext_ref[0:HALO, :] = jnp.where(first, 0.0, xh_ref[...].astype(f32))
        ext_ref[HALO:, :] = x_ref[...].astype(f32)
        pre = b_ref[...] + _conv_taps(ext_ref, w_ref, ts, SSD_CONV, HALO - (SSD_CONV - 1))
        pre_ref[...] = pre.astype(bf16)
        act_ref[...] = _silu(pre).astype(bf16)

    return pl.pallas_call(
        body, name="conv_ssd_fwd", grid=grid,
        in_specs=[_cur(ts, cb), _prev_halo(ts, cb), _wspec(SSD_CONV, cb), _wspec(1, cb)],
        out_specs=[_cur(ts, cb), _cur(ts, cb)],
        out_shape=[jax.ShapeDtypeStruct((s, c), bf16)] * 2,
        scratch_shapes=[pltpu.VMEM((HALO + ts, cb), f32)],
        compiler_params=_cp("parallel", "parallel"))(xraw, xraw, w, b)


def _conv_glu_fwd(proj1, w, b):
    s = proj1.shape[0]
    c = D_CONV
    ts, cb, grid = _conv_grid(s, c)
    goff = c // cb

    def body(v_ref, g_ref, vh_ref, gh_ref, w_ref, b_ref, hc_ref, ext_ref):
        first = pl.program_id(1) == 0
        hh = vh_ref[...].astype(f32) * _sigmoid(gh_ref[...].astype(f32))
        ext_ref[0:HALO, :] = jnp.where(first, 0.0, hh)
        ext_ref[HALO:, :] = v_ref[...].astype(f32) * _sigmoid(g_ref[...].astype(f32))
        hc = b_ref[...] + _conv_taps(ext_ref, w_ref, ts, CONV_WIDTH, HALO - (CONV_WIDTH - 1))
        hc_ref[...] = hc.astype(bf16)

    return pl.pallas_call(
        body, name="conv_glu_fwd", grid=grid,
        in_specs=[_cur(ts, cb), _cur(ts, cb, goff), _prev_halo(ts, cb), _prev_halo(ts, cb, goff),
                  _wspec(CONV_WIDTH, cb), _wspec(1, cb)],
        out_specs=_cur(ts, cb),
        out_shape=jax.ShapeDtypeStruct((s, c), bf16),
        scratch_shapes=[pltpu.VMEM((HALO + ts, cb), f32)],
        compiler_params=_cp("parallel", "parallel"))(proj1, proj1, proj1, proj1, w, b)


def _conv_bwd_core(dp, dpn_ref, last, w_ref, dext_ref, xext_ref, dw_ref, db_ref, ts, k_taps):
    dext_ref[0:ts, :] = dp
    dext_ref[ts:, :] = jnp.where(last, 0.0, dpn_ref[...].astype(f32))

    @pl.when(pl.program_id(1) == 0)
    def _():
        dw_ref[...] = jnp.zeros_like(dw_ref)
        db_ref[...] = jnp.zeros_like(db_ref)
    dx = None
    for k in range(k_taps):
        t = w_ref[k:k + 1, :] * dext_ref[pl.ds(k_taps - 1 - k, ts), :]
        dx = t if dx is None else dx + t
        dw_ref[k:k + 1, :] += jnp.sum(dp * xext_ref[pl.ds(HALO - (k_taps - 1) + k, ts), :], axis=0, keepdims=True)
    db_ref[...] += jnp.sum(dp, axis=0, keepdims=True)
    return dx


def _conv_ssd_bwd(dpre, xraw, w):
    s, c = xraw.shape
    ts, cb, grid = _conv_grid(s, c)
    nb = s // ts

    def body(dp_ref, dpn_ref, x_ref, xh_ref, w_ref, dx_ref, dw_ref, db_ref, dext_ref, xext_ref):
        i = pl.program_id(1)
        xext_ref[0:HALO, :] = jnp.where(i == 0, 0.0, xh_ref[...].astype(f32))
        xext_ref[HALO:, :] = x_ref[...].astype(f32)
        dx = _conv_bwd_core(dp_ref[...].astype(f32), dpn_ref, i == nb - 1, w_ref, dext_ref, xext_ref, dw_ref, db_ref,
                            ts, SSD_CONV)
        dx_ref[...] = dx.astype(bf16)

    return pl.pallas_call(
        body, name="conv_ssd_bwd", grid=grid,
        in_specs=[_cur(ts, cb), _next_halo(ts, cb, s), _cur(ts, cb), _prev_halo(ts, cb), _wspec(SSD_CONV, cb)],
        out_specs=[_cur(ts, cb), _wspec(SSD_CONV, cb), _wspec(1, cb)],
        out_shape=[jax.ShapeDtypeStruct((s, c), bf16), jax.ShapeDtypeStruct((SSD_CONV, c), f32),
                   jax.ShapeDtypeStruct((1, c), f32)],
        scratch_shapes=[pltpu.VMEM((ts + HALO, cb), f32), pltpu.VMEM((HALO + ts, cb), f32)],
        compiler_params=_cp("parallel", "arbitrary"))(dpre, dpre, xraw, xraw, w)


def _conv_glu_bwd(dhc, proj1, w):
    s = proj1.shape[0]
    c = D_CONV
    ts, cb, grid = _conv_grid(s, c)
    nb = s // ts
    goff = c // cb

    def body(dp_ref, dpn_ref, v_ref, g_ref, vh_ref, gh_ref, w_ref, dv_ref, dg_ref, dw_ref, db_ref, dext_ref, xext_ref):
        i = pl.program_id(1)
        val = v_ref[...].astype(f32)
        sg = _sigmoid(g_ref[...].astype(f32))
        xext_ref[0:HALO, :] = jnp.where(i == 0, 0.0, vh_ref[...].astype(f32) * _sigmoid(gh_ref[...].astype(f32)))
        xext_ref[HALO:, :] = val * sg
        dh = _conv_bwd_core(dp_ref[...].astype(f32), dpn_ref, i == nb - 1, w_ref, dext_ref, xext_ref, dw_ref, db_ref,
                            ts, CONV_WIDTH)
        dv_ref[...] = (dh * sg).astype(bf16)
        dg_ref[...] = (dh * val * sg * (1.0 - sg)).astype(bf16)

    return pl.pallas_call(
        body, name="conv_glu_bwd", grid=grid,
        in_specs=[_cur(ts, cb), _next_halo(ts, cb, s), _cur(ts, cb), _cur(ts, cb, goff), _prev_halo(ts, cb),
                  _prev_halo(ts, cb, goff), _wspec(CONV_WIDTH, cb)],
        out_specs=[_cur(ts, cb), _cur(ts, cb), _wspec(CONV_WIDTH, cb), _wspec(1, cb)],
        out_shape=[jax.ShapeDtypeStruct((s, c), bf16), jax.ShapeDtypeStruct((s, c), bf16),
                   jax.ShapeDtypeStruct((CONV_WIDTH, c), f32), jax.ShapeDtypeStruct((1, c), f32)],
        scratch_shapes=[pltpu.VMEM((ts + HALO, cb), f32), pltpu.VMEM((HALO + ts, cb), f32)],
        compiler_params=_cp("parallel", "arbitrary"))(dhc, dhc, proj1, proj1, proj1, proj1, w)


def _ssd_common(dt_ref, prm_ref):
    l = CHUNK
    dtb = prm_ref[0:1, :]
    a = -jnp.exp(prm_ref[1:2, :])
    dsk = prm_ref[2:3, :]
    zraw = dt_ref[...] + dtb
    dt = _softplus(zraw)
    da = dt * a
    row = lax.broadcasted_iota(jnp.int32, (l, l), 0)
    col = lax.broadcasted_iota(jnp.int32, (l, l), 1)
    causal = row >= col
    cs = _dot(causal.astype(f32), da, precision=HIGHEST)
    return a, dsk, zraw, dt, cs, cs.T, causal, row, col


def _ssd_fwd(act, dtf, prm):
    s = act.shape[0]
    nc = s // CHUNK
    l = CHUNK

    def body(xs_ref, dt_ref, prm_ref, y_ref, hs_ref, st_ref):
        @pl.when(pl.program_id(0) == 0)
        def _():
            st_ref[...] = jnp.zeros_like(st_ref)
        a, dsk, _, dt, cs, cst, causal, _, _ = _ssd_common(dt_ref, prm_ref)
        for g in range(N_GROUPS):
            bm = xs_ref[:, B_OFF + D_STATE * g:B_OFF + D_STATE * (g + 1)]
            cm = xs_ref[:, C_OFF + D_STATE * g:C_OFF + D_STATE * (g + 1)]
            gmat = _dot(cm, bm, NT)
            for r in range(HEADS_PER_GROUP):
                h = HEADS_PER_GROUP * g + r
                hsl = slice(HEAD_DIM * h, HEAD_DIM * (h + 1))
                xv = xs_ref[:, hsl].astype(f32)
                csc = cs[:, h:h + 1]
                csr = cst[h:h + 1, :]
                cl = cs[l - 1:l, h:h + 1]
                dk = jnp.exp(jnp.where(causal, csc - csr, NEG))
                xd = xv * dt[:, h:h + 1]
                hp = st_ref[h]
                hs_ref[0, h] = hp
                ydiag = _dot((gmat * dk).astype(bf16), xd.astype(bf16))
                yoff = _dot(cm, hp.astype(bf16), NT) * jnp.exp(csc)
                y_ref[:, hsl] = (ydiag + yoff + xv * dsk[:, h:h + 1]).astype(bf16)
                st = _dot((xd * jnp.exp(cl - csc)).astype(bf16), bm, TN)
                st_ref[h] = hp * jnp.exp(cl) + st

    return pl.pallas_call(
        body, name="ssd_fwd", grid=(nc,),
        in_specs=[pl.BlockSpec((l, XBC_W), lambda i: (i, 0)), pl.BlockSpec((l, 128), lambda i: (i, 0)),
                  pl.BlockSpec((8, 128), lambda i: (0, 0))],
        out_specs=[pl.BlockSpec((l, D_MODEL), lambda i: (i, 0)),
                   pl.BlockSpec((1, N_HEADS, HEAD_DIM, D_STATE), lambda i: (i, 0, 0, 0))],
        out_shape=[jax.ShapeDtypeStruct((s, D_MODEL), bf16),
                   jax.ShapeDtypeStruct((nc, N_HEADS, HEAD_DIM, D_STATE), f32)],
        scratch_shapes=[pltpu.VMEM((N_HEADS, HEAD_DIM, D_STATE), f32)],
        compiler_params=_cp("arbitrary"))(act, dtf, prm)


def _ssd_bwd(act, pre, dtf, prm, hs, dy):
    s = act.shape[0]
    nc = s // CHUNK
    l = CHUNK

    def body(xs_ref, pre_ref, dt_ref, prm_ref, hs_ref, dy_ref, dpre_ref, ddt_ref, dprm_ref, dh_ref):
        @pl.when(pl.program_id(0) == 0)
        def _():
            dh_ref[...] = jnp.zeros_like(dh_ref)
            dprm_ref[...] = jnp.zeros_like(dprm_ref)
        a, dsk, zraw, dt, cs, cst, causal, row, col = _ssd_common(dt_ref, prm_ref)
        lane = lax.broadcasted_iota(jnp.int32, (l, 128), 1)
        rowl = lax.broadcasted_iota(jnp.int32, (l, 128), 0)
        sub = lax.broadcasted_iota(jnp.int32, (128, l), 0)
        lane1 = lax.broadcasted_iota(jnp.int32, (1, 128), 1)
        dcs_c = jnp.zeros((l, 128), f32)
        dcs_r = jnp.zeros((128, l), f32)
        ddt_c = jnp.zeros((l, 128), f32)
        dd_row = jnp.zeros((1, 128), f32)
        for g in range(N_GROUPS):
            bsl = slice(B_OFF + D_STATE * g, B_OFF + D_STATE * (g + 1))
            csl = slice(C_OFF + D_STATE * g, C_OFF + D_STATE * (g + 1))
            bm = xs_ref[:, bsl]
            cm = xs_ref[:, csl]
            gmat = _dot(cm, bm, NT)
            dgm = jnp.zeros((l, l), f32)
            dbg = jnp.zeros((l, D_STATE), f32)
            dcg = jnp.zeros((l, D_STATE), f32)
            for r in range(HEADS_PER_GROUP):
                h = HEADS_PER_GROUP * g + r
                hsl = slice(HEAD_DIM * h, HEAD_DIM * (h + 1))
                xv = xs_ref[:, hsl].astype(f32)
                dyv = dy_ref[:, hsl].astype(f32)
                dyb = dyv.astype(bf16)
                csc = cs[:, h:h + 1]
                csr = cst[h:h + 1, :]
                cl = cs[l - 1:l, h:h + 1]
                dk = jnp.exp(jnp.where(causal, csc - csr, NEG))
                mf = gmat * dk
                dtc = dt[:, h:h + 1]
                xd = xv * dtc
                xdb = xd.astype(bf16)
                ecs = jnp.exp(csc)
                dec = jnp.exp(cl)
                e = jnp.exp(cl - csc)
                hp = hs_ref[0, h]
                hpb = hp.astype(bf16)
                dhn = dh_ref[h]
                dhnb = dhn.astype(bf16)
                dd_h = jnp.sum(jnp.sum(dyv * xv, axis=1, keepdims=True), axis=0, keepdims=True)
                dx = dyv * dsk[:, h:h + 1]
                ch = _dot(cm, hpb, NT)
                dye = dyv * ecs
                dyeb = dye.astype(bf16)
                dcg = dcg + _dot(dyeb, hpb)
                dhp = _dot(dyeb, cm, TN)
                dcs_col = jnp.sum(dye * ch, axis=1, keepdims=True)
                dm = _dot(dyb, xdb, NT)
                dxd = _dot(mf.astype(bf16), dyb, TN)
                dgm = dgm + dm * dk
                wmat = dm * mf
                dcs_col = dcs_col + jnp.sum(wmat, axis=1, keepdims=True)
                dcs_row = -jnp.sum(wmat, axis=0, keepdims=True)
                ddec = jnp.sum(jnp.sum(hp * dhn, axis=1, keepdims=True), axis=0, keepdims=True)
                dxe = _dot(bm, dhnb, NT)
                dxd = dxd + dxe * e
                de_e = jnp.sum(dxe * xd, axis=1, keepdims=True) * e
                dbg = dbg + _dot((xd * e).astype(bf16), dhnb)
                dcs_col = dcs_col - de_e
                dlast = ddec * dec + jnp.sum(de_e, axis=0, keepdims=True)
                dh_ref[h] = dhp + dec * dhn
                dx = dx + dxd * dtc
                ddt_h = jnp.sum(dxd * xv, axis=1, keepdims=True)
                is_h = lane == h
                dcs_c = dcs_c + jnp.where(is_h, dcs_col, 0.0) + jnp.where(is_h & (rowl == l - 1), dlast, 0.0)
                dcs_r = dcs_r + jnp.where(sub == h, dcs_row, 0.0)
                ddt_c = ddt_c + jnp.where(is_h, ddt_h, 0.0)
                dd_row = dd_row + jnp.where(lane1 == h, dd_h, 0.0)
                dpre_ref[:, hsl] = (dx * _dsilu(pre_ref[:, hsl].astype(f32))).astype(bf16)
            dgb = dgm.astype(bf16)
            dcg = dcg + _dot(dgb, bm)
            dbg = dbg + _dot(dgb, cm, TN)
            dpre_ref[:, bsl] = (dbg * _dsilu(pre_ref[:, bsl].astype(f32))).astype(bf16)
            dpre_ref[:, csl] = (dcg * _dsilu(pre_ref[:, csl].astype(f32))).astype(bf16)
        dcs = dcs_c + dcs_r.T
        dda = _dot((row <= col).astype(f32), dcs, precision=HIGHEST)
        ddt = ddt_c + dda * a
        ddtraw = jnp.where(lane < N_HEADS, ddt * _sigmoid(zraw), 0.0)
        ddt_ref[...] = ddtraw
        dprm_ref[0:1, :] += jnp.sum(ddtraw, axis=0, keepdims=True)
        dprm_ref[1:2, :] += jnp.where(lane1 < N_HEADS, jnp.sum(dda * dt, axis=0, keepdims=True) * a, 0.0)
        dprm_ref[2:3, :] += dd_row

    def rev(i):
        return (nc - 1 - i, 0)

    return pl.pallas_call(
        body, name="ssd_bwd", grid=(nc,),
        in_specs=[pl.BlockSpec((l, XBC_W), rev), pl.BlockSpec((l, XBC_W), rev),
                  pl.BlockSpec((l, 128), rev), pl.BlockSpec((8, 128), lambda i: (0, 0)),
                  pl.BlockSpec((1, N_HEADS, HEAD_DIM, D_STATE), lambda i: (nc - 1 - i, 0, 0, 0)),
                  pl.BlockSpec((l, D_MODEL), rev)],
        out_specs=[pl.BlockSpec((l, XBC_W), rev), pl.BlockSpec((l, 128), rev), pl.BlockSpec((8, 128), lambda i: (0, 0))],
        out_shape=[jax.ShapeDtypeStruct((s, XBC_W), bf16), jax.ShapeDtypeStruct((s, 128), f32),
                   jax.ShapeDtypeStruct((8, 128), f32)],
        scratch_shapes=[pltpu.VMEM((N_HEADS, HEAD_DIM, D_STATE), f32)],
        compiler_params=_cp("arbitrary"))(act, pre, dtf, prm, hs, dy)


def _fox_cumsum(dtf, prm):
    s = dtf.shape[0]
    l = CHUNK

    def body(f_ref, prm_ref, c_ref, ct_ref, carry_ref):
        @pl.when(pl.program_id(0) == 0)
        def _():
            carry_ref[...] = jnp.zeros_like(carry_ref)
        lf = _log_sigmoid(f_ref[...] + prm_ref[3:4, :])
        row = lax.broadcasted_iota(jnp.int32, (l, l), 0)
        col = lax.broadcasted_iota(jnp.int32, (l, l), 1)
        c = _dot((row >= col).astype(f32), lf, precision=HIGHEST) + carry_ref[...]
        c_ref[...] = c
        ct_ref[...] = c.T
        carry_ref[...] = c[l - 1:l, :]

    return pl.pallas_call(
        body, name="fox_cumsum", grid=(s // l,),
        in_specs=[pl.BlockSpec((l, 128), lambda i: (i, 0)), pl.BlockSpec((8, 128), lambda i: (0, 0))],
        out_specs=[pl.BlockSpec((l, 128), lambda i: (i, 0)), pl.BlockSpec((128, l), lambda i: (0, i))],
        out_shape=[jax.ShapeDtypeStruct((s, 128), f32), jax.ShapeDtypeStruct((128, s), f32)],
        scratch_shapes=[pltpu.VMEM((1, 128), f32)],
        compiler_params=_cp("arbitrary"))(dtf, prm)


def _fox_logits(q_ref, k_ref, cq_ref, ck_ref, h, mask):
    hsl = slice(HEAD_DIM * h, HEAD_DIM * (h + 1))
    sc = _dot(q_ref[:, hsl], k_ref[:, hsl], NT) * (HEAD_DIM ** -0.5)
    sc = sc + (cq_ref[:, F_LANE + h:F_LANE + h + 1] - ck_ref[F_LANE + h:F_LANE + h + 1, :])
    return jnp.where(mask, sc, NEG)


def _causal_mask(qi, ki, t):
    rows = qi * t + lax.broadcasted_iota(jnp.int32, (t, t), 0)
    cols = ki * t + lax.broadcasted_iota(jnp.int32, (t, t), 1)
    return cols <= rows


def _fox_fwd(qkv, c, ct):
    s = qkv.shape[0]
    t = min(ATTN_TILE, s)
    nq = s // t

    def body(q_ref, k_ref, v_ref, cq_ref, ck_ref, o_ref, lse_ref, acc_ref, m_ref, l_ref):
        qi, ki = pl.program_id(0), pl.program_id(1)

        @pl.when(ki == 0)
        def _():
            acc_ref[...] = jnp.zeros_like(acc_ref)
            m_ref[...] = jnp.full_like(m_ref, NEG)
            l_ref[...] = jnp.zeros_like(l_ref)

        @pl.when(ki <= qi)
        def _():
            mask = _causal_mask(qi, ki, t)
            for h in range(N_HEADS):
                hsl = slice(HEAD_DIM * h, HEAD_DIM * (h + 1))
                sc = _fox_logits(q_ref, k_ref, cq_ref, ck_ref, h, mask)
                m_old = m_ref[h]
                m_new = jnp.maximum(m_old, jnp.max(sc, axis=1, keepdims=True))
                p = jnp.exp(sc - m_new)
                alpha = jnp.exp(m_old - m_new)
                l_ref[h] = alpha * l_ref[h] + jnp.sum(p, axis=1, keepdims=True)
                acc_ref[:, hsl] = alpha * acc_ref[:, hsl] + _dot(p.astype(bf16), v_ref[:, hsl])
                m_ref[h] = m_new

        @pl.when(ki == qi)
        def _():
            lane = lax.broadcasted_iota(jnp.int32, (t, 128), 1)
            lse = jnp.zeros((t, 128), f32)
            for h in range(N_HEADS):
                hsl = slice(HEAD_DIM * h, HEAD_DIM * (h + 1))
                lv = l_ref[h]
                o_ref[:, hsl] = (acc_ref[:, hsl] / lv).astype(bf16)
                lse = lse + jnp.where(lane == h, m_ref[h] + jnp.log(lv), 0.0)
            lse_ref[...] = lse

    def kv(cb):
        return pl.BlockSpec((t, D_MODEL), lambda qi, ki: (jnp.minimum(ki, qi), cb))

    return pl.pallas_call(
        body, name="fox_fwd", grid=(nq, nq),
        in_specs=[pl.BlockSpec((t, D_MODEL), lambda qi, ki: (qi, 0)), kv(1), kv(2),
                  pl.BlockSpec((t, 128), lambda qi, ki: (qi, 0)),
                  pl.BlockSpec((128, t), lambda qi, ki: (0, jnp.minimum(ki, qi)))],
        out_specs=[pl.BlockSpec((t, D_MODEL), lambda qi, ki: (qi, 0)), pl.BlockSpec((t, 128), lambda qi, ki: (qi, 0))],
        out_shape=[jax.ShapeDtypeStruct((s, D_MODEL), bf16), jax.ShapeDtypeStruct((s, 128), f32)],
        scratch_shapes=[pltpu.VMEM((t, D_MODEL), f32), pltpu.VMEM((N_HEADS, t, 1), f32),
                        pltpu.VMEM((N_HEADS, t, 1), f32)],
        compiler_params=_cp("parallel", "arbitrary"))(qkv, qkv, qkv, c, ct)


def _fox_bwd_dq(qkv, do, c, ct, lse, delta):
    s = qkv.shape[0]
    t = min(ATTN_TILE, s)
    nq = s // t

    def body(q_ref, k_ref, v_ref, do_ref, cq_ref, ck_ref, lse_ref, dl_ref, dq_ref, dcq_ref, acc_ref, row_ref):
        qi, ki = pl.program_id(0), pl.program_id(1)

        @pl.when(ki == 0)
        def _():
            acc_ref[...] = jnp.zeros_like(acc_ref)
            row_ref[...] = jnp.zeros_like(row_ref)

        @pl.when(ki <= qi)
        def _():
            mask = _causal_mask(qi, ki, t)
            lane = lax.broadcasted_iota(jnp.int32, (t, 128), 1)
            rows = jnp.zeros((t, 128), f32)
            for h in range(N_HEADS):
                hsl = slice(HEAD_DIM * h, HEAD_DIM * (h + 1))
                sc = _fox_logits(q_ref, k_ref, cq_ref, ck_ref, h, mask)
                p = jnp.exp(sc - lse_ref[:, h:h + 1])
                dp = _dot(do_ref[:, hsl], v_ref[:, hsl], NT)
                ds = p * (dp - dl_ref[:, h:h + 1])
                rows = rows + jnp.where(lane == F_LANE + h, jnp.sum(ds, axis=1, keepdims=True), 0.0)
                acc_ref[:, hsl] += _dot(ds.astype(bf16), k_ref[:, hsl])
            row_ref[...] += rows

        @pl.when(ki == qi)
        def _():
            dq_ref[...] = (acc_ref[...] * (HEAD_DIM ** -0.5)).astype(bf16)
            dcq_ref[...] = row_ref[...]

    def kv(cb):
        return pl.BlockSpec((t, D_MODEL), lambda qi, ki: (jnp.minimum(ki, qi), cb))

    def qrow(w):
        return pl.BlockSpec((t, w), lambda qi, ki: (qi, 0))

    return pl.pallas_call(
        body, name="fox_bwd_dq", grid=(nq, nq),
        in_specs=[qrow(D_MODEL), kv(1), kv(2), qrow(D_MODEL), qrow(128),
                  pl.BlockSpec((128, t), lambda qi, ki: (0, jnp.minimum(ki, qi))), qrow(128), qrow(128)],
        out_specs=[qrow(D_MODEL), qrow(128)],
        out_shape=[jax.ShapeDtypeStruct((s, D_MODEL), bf16), jax.ShapeDtypeStruct((s, 128), f32)],
        scratch_shapes=[pltpu.VMEM((t, D_MODEL), f32), pltpu.VMEM((t, 128), f32)],
        compiler_params=_cp("parallel", "arbitrary"))(qkv, qkv, qkv, do, c, ct, lse, delta)


def _fox_bwd_dkv(qkv, do, c, ct, lse, delta):
    s = qkv.shape[0]
    t = min(ATTN_TILE, s)
    nq = s // t

    def body(q_ref, k_ref, v_ref, do_ref, cq_ref, ck_ref, lse_ref, dl_ref, dk_ref, dv_ref, dc_ref,
             dka_ref, dva_ref, dca_ref):
        ki, qi = pl.program_id(0), pl.program_id(1)

        @pl.when(qi == 0)
        def _():
            dka_ref[...] = jnp.zeros_like(dka_ref)
            dva_ref[...] = jnp.zeros_like(dva_ref)
            dca_ref[...] = jnp.zeros_like(dca_ref)

        @pl.when(qi >= ki)
        def _():
            mask = _causal_mask(qi, ki, t)
            sub = lax.broadcasted_iota(jnp.int32, (128, t), 0)
            dc = jnp.zeros((128, t), f32)
            for h in range(N_HEADS):
                hsl = slice(HEAD_DIM * h, HEAD_DIM * (h + 1))
                sc = _fox_logits(q_ref, k_ref, cq_ref, ck_ref, h, mask)
                p = jnp.exp(sc - lse_ref[:, h:h + 1])
                dob = do_ref[:, hsl]
                dp = _dot(dob, v_ref[:, hsl], NT)
                ds = p * (dp - dl_ref[:, h:h + 1])
                dva_ref[:, hsl] += _dot(p.astype(bf16), dob, TN)
                dka_ref[:, hsl] += _dot(ds.astype(bf16), q_ref[:, hsl], TN)
                dc = dc + jnp.where(sub == F_LANE + h, -jnp.sum(ds, axis=0, keepdims=True), 0.0)
            dca_ref[...] += dc

        @pl.when(qi == nq - 1)
        def _():
            dk_ref[...] = (dka_ref[...] * (HEAD_DIM ** -0.5)).astype(bf16)
            dv_ref[...] = dva_ref[...].astype(bf16)
            dc_ref[...] = dca_ref[...]

    def qrow(w, cb=0):
        return pl.BlockSpec((t, w), lambda ki, qi: (jnp.maximum(qi, ki), cb))

    def krow(cb):
        return pl.BlockSpec((t, D_MODEL), lambda ki, qi: (ki, cb))

    return pl.pallas_call(
        body, name="fox_bwd_dkv", grid=(nq, nq),
        in_specs=[qrow(D_MODEL), krow(1), krow(2), qrow(D_MODEL), qrow(128),
                  pl.BlockSpec((128, t), lambda ki, qi: (0, ki)), qrow(128), qrow(128)],
        out_specs=[pl.BlockSpec((t, D_MODEL), lambda ki, qi: (ki, 0)), pl.BlockSpec((t, D_MODEL), lambda ki, qi: (ki, 0)),
                   pl.BlockSpec((128, t), lambda ki, qi: (0, ki))],
        out_shape=[jax.ShapeDtypeStruct((s, D_MODEL), bf16), jax.ShapeDtypeStruct((s, D_MODEL), bf16),
                   jax.ShapeDtypeStruct((128, s), f32)],
        scratch_shapes=[pltpu.VMEM((t, D_MODEL), f32), pltpu.VMEM((t, D_MODEL), f32), pltpu.VMEM((128, t), f32)],
        compiler_params=_cp("parallel", "arbitrary"))(qkv, qkv, qkv, do, c, ct, lse, delta)


def _fox_gate_bwd(dct, dcq, dtf, prm, ddt_raw):
    s = dtf.shape[0]
    l = CHUNK
    nb = s // l

    def body(dc_ref, dcq_ref, f_ref, prm_ref, ddt_ref, out_ref, dfb_ref, carry_ref):
        @pl.when(pl.program_id(0) == 0)
        def _():
            carry_ref[...] = jnp.zeros_like(carry_ref)
            dfb_ref[...] = jnp.zeros_like(dfb_ref)
        dc = dc_ref[...].T + dcq_ref[...]
        row = lax.broadcasted_iota(jnp.int32, (l, l), 0)
        col = lax.broadcasted_iota(jnp.int32, (l, l), 1)
        dlf = _dot((row <= col).astype(f32), dc, precision=HIGHEST) + carry_ref[...]
        carry_ref[...] = dlf[0:1, :]
        lane = lax.broadcasted_iota(jnp.int32, (l, 128), 1)
        is_f = (lane >= F_LANE) & (lane < F_LANE + N_HEADS)
        dfr = jnp.where(is_f, dlf * _sigmoid(-(f_ref[...] + prm_ref[3:4, :])), 0.0)
        dfb_ref[...] += jnp.sum(dfr, axis=0, keepdims=True)
        out_ref[...] = ddt_ref[...] + dfr

    def rev(i):
        return (nb - 1 - i, 0)

    return pl.pallas_call(
        body, name="fox_gate_bwd", grid=(nb,),
        in_specs=[pl.BlockSpec((128, l), lambda i: (0, nb - 1 - i)), pl.BlockSpec((l, 128), rev),
                  pl.BlockSpec((l, 128), rev), pl.BlockSpec((8, 128), lambda i: (0, 0)), pl.BlockSpec((l, 128), rev)],
        out_specs=[pl.BlockSpec((l, 128), rev), pl.BlockSpec((1, 128), lambda i: (0, 0))],
        out_shape=[jax.ShapeDtypeStruct((s, 128), f32), jax.ShapeDtypeStruct((1, 128), f32)],
        scratch_shapes=[pltpu.VMEM((1, 128), f32)],
        compiler_params=_cp("arbitrary"))(dct, dcq, dtf, prm, ddt_raw)


def _position():
    return lax.axis_index("x"), lax.axis_index("y"), lax.axis_index("c")


def _all_gather(xl, name):
    r, c = xl.shape

    def body(x_ref, out_ref, send_sems, recv_sems, local_sem):
        x, y, cc = _position()
        me, sibling = (x, y, cc), (x, y, 1 - cc)
        chips = [(1 - x, y), (x, 1 - y), (1 - x, 1 - y)]

        def slot(px, py, pc):
            return out_ref.at[4 * px + 2 * py + pc]

        def copy(k, block, to, src=None):
            return pltpu.make_async_remote_copy(
                src_ref=slot(*block) if src is None else src, dst_ref=slot(*block),
                send_sem=send_sems.at[k], recv_sem=recv_sems.at[k],
                device_id=to, device_id_type=pl.DeviceIdType.MESH)

        mine = pltpu.make_async_copy(x_ref, slot(*me), local_sem)
        mine.start()
        first = [copy(0, me, sibling, src=x_ref)]
        first += [copy(1 + j, me, (*chip, cc), src=x_ref) for j, chip in enumerate(chips)]
        for cp in first:
            cp.start()
        passed = [copy(4 + j, (*chip, cc), sibling) for j, chip in enumerate(chips)]
        for j, chip in enumerate(chips):
            copy(1 + j, (*chip, cc), me).wait_recv()
            passed[j].start()
        copy(0, sibling, me).wait_recv()
        for j, chip in enumerate(chips):
            copy(4 + j, (*chip, 1 - cc), me).wait_recv()
        for cp in first + passed:
            cp.wait_send()
        mine.wait()

    return pl.pallas_call(
        body, name=name,
        out_shape=jax.ShapeDtypeStruct((N_DEV, r, c), xl.dtype),
        in_specs=[pl.BlockSpec(memory_space=pl.ANY)], out_specs=pl.BlockSpec(memory_space=pl.ANY),
        scratch_shapes=[pltpu.SemaphoreType.DMA((7,)), pltpu.SemaphoreType.DMA((7,)), pltpu.SemaphoreType.DMA],
    )(xl)


def _grad_exchange(gs):
    n = len(gs)

    def body(*refs):
        g_refs, r_refs = refs[:n], refs[n:2 * n]
        send_sems, recv_sems, local_sems = refs[2 * n:]
        x, y, cc = _position()
        me = 4 * x + 2 * y + cc
        local = [pltpu.make_async_copy(g_refs[a].at[me], r_refs[a].at[me], local_sems.at[a]) for a in range(n)]
        for cp in local:
            cp.start()
        sends, recvs = [], []
        for k in range(1, N_DEV):
            px = 1 - x if k & 4 else x
            py = 1 - y if k & 2 else y
            pc = 1 - cc if k & 1 else cc
            pid = 4 * px + 2 * py + pc
            for a in range(n):
                sends.append(pltpu.make_async_remote_copy(
                    src_ref=g_refs[a].at[pid], dst_ref=r_refs[a].at[me],
                    send_sem=send_sems.at[a, k - 1], recv_sem=recv_sems.at[a, k - 1],
                    device_id=(px, py, pc), device_id_type=pl.DeviceIdType.MESH))
                recvs.append(pltpu.make_async_remote_copy(
                    src_ref=g_refs[a].at[pid], dst_ref=r_refs[a].at[pid],
                    send_sem=send_sems.at[a, k - 1], recv_sem=recv_sems.at[a, k - 1],
                    device_id=(px, py, pc), device_id_type=pl.DeviceIdType.MESH))
        for cp in sends:
            cp.start()
        for cp in recvs:
            cp.wait_recv()
        for cp in sends:
            cp.wait_send()
        for cp in local:
            cp.wait()

    anyspec = pl.BlockSpec(memory_space=pl.ANY)
    return pl.pallas_call(
        body, name="grad_exchange",
        out_shape=[jax.ShapeDtypeStruct(g.shape, g.dtype) for g in gs],
        in_specs=[anyspec] * n, out_specs=[anyspec] * n,
        scratch_shapes=[pltpu.SemaphoreType.DMA((n, N_DEV - 1)), pltpu.SemaphoreType.DMA((n, N_DEV - 1)),
                        pltpu.SemaphoreType.DMA((n,))],
    )(*gs)


def _sum_parts(parts, name):
    n, r, c = parts.shape

    def body(p_ref, o_ref):
        g = p_ref[0]
        for i in range(1, n):
            g = g + p_ref[i]
        o_ref[...] = g

    return pl.pallas_call(body, name=name, out_shape=jax.ShapeDtypeStruct((r, c), f32))(parts)


def _adamw(w, m, v, parts, name, tr=128):
    r, c = w.shape
    n = parts.shape[0]
    tr = min(tr, r)
    c1 = 1.0 - ADAM_B1 ** ADAM_STEP
    c2 = 1.0 - ADAM_B2 ** ADAM_STEP

    def body(w_ref, m_ref, v_ref, p_ref, g_ref, d_ref, nm_ref, nv_ref):
        g = p_ref[0]
        for i in range(1, n):
            g = g + p_ref[i]
        g_ref[...] = g
        nm = ADAM_B1 * m_ref[...] + (1.0 - ADAM_B1) * g
        nv = ADAM_B2 * v_ref[...] + (1.0 - ADAM_B2) * (g * g)
        nm_ref[...] = nm
        nv_ref[...] = nv
        d_ref[...] = -ADAM_LR * ((nm / c1) / (jnp.sqrt(nv / c2) + ADAM_EPS) + ADAM_WD * w_ref[...])

    blk = pl.BlockSpec((tr, c), lambda i: (i, 0))
    return pl.pallas_call(
        body, name=name, grid=(r // tr,),
        in_specs=[blk, blk, blk, pl.BlockSpec((n, tr, c), lambda i: (0, i, 0))],
        out_specs=[blk] * 4, out_shape=[jax.ShapeDtypeStruct((r, c), f32)] * 4,
        compiler_params=_cp("parallel"))(w, m, v, parts)


def _lanes(w):
    return -(-w // 128) * 128


def _pack(arrs):
    rows = []
    for a in arrs:
        k, w = a.shape
        if w % 128:
            a = jnp.pad(a, ((0, 0), (0, _lanes(w) - w)))
        rows.append(a.reshape(-1, 128))
    out = jnp.concatenate(rows, axis=0)
    pad = -out.shape[0] % 8
    return jnp.pad(out, ((0, pad), (0, 0))) if pad else out


def _unpack(packed, shapes):
    outs, off = [], 0
    lead = packed.shape[:-2]
    for k, w in shapes:
        nrow = k * _lanes(w) // 128
        a = packed[..., off:off + nrow, :].reshape(*lead, k, _lanes(w))[..., :w]
        outs.append(a)
        off += nrow
    return outs


def _gathered_cols(a):
    n, k, wl = a.shape
    return jnp.transpose(a, (1, 0, 2)).reshape(k, n * wl)


def _col_shards(a):
    k, w = a.shape
    return jnp.transpose(a.reshape(k, N_DEV, w // N_DEV), (1, 0, 2))


SMALL_PARAMS = (
    ("e_norm_pre", 1, 1024, False), ("e_conv_w", 4, 2048, True), ("e_conv_b", 1, 2048, False),
    ("e_dt_bias", 1, 16, False), ("e_a_log", 1, 16, False), ("e_d_skip", 1, 16, False), ("e_fgate_b", 1, 16, False),
    ("e_ssd_norm", 1, 1024, False), ("e_norm_post", 1, 1024, False), ("o_norm_pre", 1, 1024, True),
    ("o_conv_w", 31, 2048, True), ("o_conv_b", 1, 2048, True), ("o_ln_g", 1, 2048, True), ("o_ln_b", 1, 2048, True),
    ("o_norm_post", 1, 1024, True),
)
BIG_PARAMS = ("e_w_in", "e_w_out", "o_w_in", "o_w_out")
WEIGHT_ORDER = ("e_norm_pre", "e_w_in", "e_conv_w", "e_conv_b", "e_dt_bias", "e_a_log", "e_d_skip", "e_fgate_b",
                "e_ssd_norm", "e_w_out", "e_norm_post", "o_norm_pre", "o_w_in", "o_conv_w", "o_conv_b", "o_ln_g",
                "o_ln_b", "o_w_out", "o_norm_post")
E_IN = 7200
O_IN = 6144


def kernel(x, e_norm_pre, e_w_in, e_conv_w, e_conv_b, e_dt_bias, e_a_log, e_d_skip, e_fgate_b, e_ssd_norm, e_w_out, e_norm_post, o_norm_pre, o_w_in, o_conv_w, o_conv_b, o_ln_g, o_ln_b, o_w_out, o_norm_post, loss_target, m_e_norm_pre, m_e_w_in, m_e_conv_w, m_e_conv_b, m_e_dt_bias, m_e_a_log, m_e_d_skip, m_e_fgate_b, m_e_ssd_norm, m_e_w_out, m_e_norm_post, m_o_norm_pre, m_o_w_in, m_o_conv_w, m_o_conv_b, m_o_ln_g, m_o_ln_b, m_o_w_out, m_o_norm_post, v_e_norm_pre, v_e_w_in, v_e_conv_w, v_e_conv_b, v_e_dt_bias, v_e_a_log, v_e_d_skip, v_e_fgate_b, v_e_ssd_norm, v_e_w_out, v_e_norm_post, v_o_norm_pre, v_o_w_in, v_o_conv_w, v_o_conv_b, v_o_ln_g, v_o_ln_b, v_o_w_out, v_o_norm_post):
    given = dict(locals())
    w_in = {n: given[n] for n in WEIGHT_ORDER}
    m_in = {n: given["m_" + n] for n in WEIGHT_ORDER}
    v_in = {n: given["v_" + n] for n in WEIGHT_ORDER}

    def mat(a):
        return a.reshape(a.shape[-2:])

    xs = mat(x)
    tgt = mat(loss_target)
    xi, yi, ci = _position()
    me = 4 * xi + 2 * yi + ci
    ew, ow = E_IN // N_DEV, O_IN // N_DEV
    wr = D_CONV // N_DEV

    big_local = jnp.concatenate([
        mat(e_w_in).astype(bf16).reshape(ew, D_MODEL), mat(o_w_in).astype(bf16).reshape(ow, D_MODEL),
        mat(e_w_out).astype(bf16), mat(o_w_out).astype(bf16)], axis=0)
    nbig = big_local.shape[0]
    big_local = jnp.pad(big_local, ((0, -nbig % 16), (0, 0)))
    wg = _all_gather(big_local, "gather_weights")
    o0, o1, o2, o3 = 0, ew, ew + ow, ew + ow + wr
    e_w_in_f = _gathered_cols(wg[:, o0:o1].reshape(N_DEV, D_MODEL, ew))
    o_w_in_f = _gathered_cols(wg[:, o1:o2].reshape(N_DEV, D_MODEL, ow))
    e_w_out_f = wg[:, o2:o3].reshape(D_CONV, D_MODEL)
    o_w_out_f = wg[:, o3:o3 + wr].reshape(D_CONV, D_MODEL)
    w_z, w_xbc = e_w_in_f[:, 0:2048], e_w_in_f[:, 2048:4096]
    w_qkv = e_w_in_f[:, 4112:7184]
    w_dtf = jnp.concatenate([e_w_in_f[:, 4096:4112], e_w_in_f[:, 7184:7200], jnp.zeros((D_MODEL, 96), bf16)], axis=1)

    sharded_small = [(n, k, w) for n, k, w, sh in SMALL_PARAMS if sh]
    sg = _all_gather(_pack([mat(w_in[n]) for n, _, _ in sharded_small]), "gather_small_weights")
    full_small = {n: _gathered_cols(a)
                  for (n, _, _), a in zip(sharded_small, _unpack(sg, [(k, w // N_DEV) for _, k, w in sharded_small]))}
    for n, _, _, sh in SMALL_PARAMS:
        if not sh:
            full_small[n] = mat(w_in[n])
    p = full_small
    prm = jnp.zeros((8, 128), f32)
    prm = prm.at[0, 0:16].set(p["e_dt_bias"][0]).at[1, 0:16].set(p["e_a_log"][0]).at[2, 0:16].set(p["e_d_skip"][0])
    prm = prm.at[3, F_LANE:F_LANE + 16].set(p["e_fgate_b"][0])

    u0 = _rms_fwd(xs, p["e_norm_pre"], "rms_pre0")
    z0 = _mm_nn(u0, w_z, bf16, "proj0_z")
    xraw = _mm_nn(u0, w_xbc, bf16, "proj0_xbc")
    qkv = _mm_nn(u0, w_qkv, bf16, "proj0_qkv")
    dtf = _mm_nn(u0, w_dtf, f32, "proj0_dtf")
    pre, act = _conv_ssd_fwd(xraw, p["e_conv_w"], p["e_conv_b"])
    y, hs = _ssd_fwd(act, dtf, prm)
    c, ct = _fox_cumsum(dtf, prm)
    o, lse = _fox_fwd(qkv, c, ct)
    cat = _gate0_fwd(y, z0, o, p["e_ssd_norm"])
    out0 = _mm_nn(cat, e_w_out_f, f32, "out0")
    x1, u1 = _post0_pre1(xs, out0, p["e_norm_post"], p["o_norm_pre"])

    proj1 = _mm_nn(u1, o_w_in_f, bf16, "proj1")
    hc = _conv_glu_fwd(proj1, p["o_conv_w"], p["o_conv_b"])
    h3 = _ln_gate_fwd(hc, proj1, p["o_ln_g"], p["o_ln_b"])
    out1 = _mm_nn(h3, o_w_out_f, f32, "out1")
    dy, d_out1, dg_post1, loss_part = _final_loss(x1, out1, tgt, p["o_norm_post"])

    dh3 = _mm_nt([(d_out1, 0, o_w_out_f, 0, D_MODEL)], bf16, "dh3")
    g_o_w_out = _mm_tn(h3, d_out1, "dw_out1")
    dhc, dz1, dg_ln, db_ln = _ln_gate_bwd(hc, proj1, dh3, p["o_ln_g"], p["o_ln_b"])
    dval, dgate, dw_conv1, db_conv1 = _conv_glu_bwd(dhc, proj1, p["o_conv_w"])
    dproj1 = jnp.concatenate([dval, dgate, dz1], axis=1)
    du1 = _mm_nt([(dproj1, 0, o_w_in_f, 0, O_IN)], f32, "du1")
    g_o_w_in = _mm_tn(u1, dproj1, "dw_in1", tn=ow, blocked=True)
    dx1, d_out0, dg_pre1, dg_post0 = _mid_bwd(x1, du1, dy, out0, p["o_norm_pre"], p["e_norm_post"])

    dcat = _mm_nt([(d_out0, 0, e_w_out_f, 0, D_MODEL)], bf16, "dcat")
    g_e_w_out = _mm_tn(cat, d_out0, "dw_out0")
    dy_ssd, do, dz0, delta, dg_ssd_norm = _gate0_bwd(y, z0, o, dcat, p["e_ssd_norm"])
    dq, dcq = _fox_bwd_dq(qkv, do, c, ct, lse, delta)
    dk, dv, dct = _fox_bwd_dkv(qkv, do, c, ct, lse, delta)
    dpre, ddt_raw, dprm = _ssd_bwd(act, pre, dtf, prm, hs, dy_ssd)
    ddtf, dfb = _fox_gate_bwd(dct, dcq, dtf, prm, ddt_raw)
    dxraw, dw_conv0, db_conv0 = _conv_ssd_bwd(dpre, xraw, p["e_conv_w"])
    du0 = _mm_nt([(dz0, 0, w_z, 0, 2048), (dxraw, 0, w_xbc, 0, 2048), (dq, 0, w_qkv, 0, 1024), (dk, 0, w_qkv, 1, 1024),
                  (dv, 0, w_qkv, 2, 1024), (ddtf, 0, w_dtf, 0, 128)], f32, "du0")
    gw_dtf = _mm_tn(u0, ddtf, "dw_in0_dtf")
    g_e_w_in_full = jnp.concatenate([
        _mm_tn(u0, dz0, "dw_in0_z"), _mm_tn(u0, dxraw, "dw_in0_xbc"), gw_dtf[:, 0:16],
        _mm_tn(u0, dq, "dw_in0_q"), _mm_tn(u0, dk, "dw_in0_k"), _mm_tn(u0, dv, "dw_in0_v"), gw_dtf[:, 16:32]], axis=1)
    grad_x, dg_pre0 = _first_bwd(xs, du0, dx1, p["e_norm_pre"])

    big_parts = _grad_exchange([
        _col_shards(g_e_w_in_full), g_e_w_out.reshape(N_DEV, wr, D_MODEL), g_o_w_in,
        g_o_w_out.reshape(N_DEV, wr, D_MODEL)])
    outs = {}
    for n, parts in zip(BIG_PARAMS, big_parts):
        outs[n] = _adamw(mat(w_in[n]), mat(m_in[n]), mat(v_in[n]), parts, "adamw_" + n)

    small_grads = {
        "e_norm_pre": dg_pre0, "e_conv_w": dw_conv0, "e_conv_b": db_conv0, "e_dt_bias": dprm[0:1, 0:16],
        "e_a_log": dprm[1:2, 0:16], "e_d_skip": dprm[2:3, 0:16], "e_fgate_b": dfb[:, F_LANE:F_LANE + 16],
        "e_ssd_norm": dg_ssd_norm, "e_norm_post": dg_post0, "o_norm_pre": dg_pre1, "o_conv_w": dw_conv1,
        "o_conv_b": db_conv1, "o_ln_g": dg_ln, "o_ln_b": db_ln, "o_norm_post": dg_post1,
    }
    gathered = _all_gather(_pack([small_grads[n] for n, _, _, _ in SMALL_PARAMS] + [loss_part]), "gather_small_grads")
    summed = _unpack(_sum_parts(gathered, "sum_small_grads"), [(k, w) for _, k, w, _ in SMALL_PARAMS] + [(1, 128)])
    loss = summed[-1][0, 0]
    g_local = []
    for (n, k, w, sh), g in zip(SMALL_PARAMS, summed):
        g_local.append(lax.dynamic_slice_in_dim(g, me * (w // N_DEV), w // N_DEV, axis=1) if sh else g)
    names = [n for n, _, _, _ in SMALL_PARAMS]
    local_shapes = [(k, w // N_DEV if sh else w) for _, k, w, sh in SMALL_PARAMS]
    res = _adamw(_pack([mat(w_in[n]) for n in names]), _pack([mat(m_in[n]) for n in names]),
                 _pack([mat(v_in[n]) for n in names]), _pack(g_local)[None], "adamw_small", tr=8)
    unpacked = [_unpack(r, local_shapes) for r in res]
    for i, n in enumerate(names):
        outs[n] = tuple(u[i] for u in unpacked)

    ret = [loss, grad_x.reshape(x.shape)]
    for j in range(4):
        ret += [outs[n][j].reshape(w_in[n].shape) for n in WEIGHT_ORDER]
    return tuple(ret)
```

```python
import jax
import jax.numpy as jnp
from jax import lax
from jax.experimental import pallas as pl
from jax.experimental.pallas import tpu as pltpu

f32 = jnp.float32
bf16 = jnp.bfloat16

N_DEV = 8
D_MODEL = 1024
N_HEADS = 16
HEAD_DIM = 64
N_GROUPS = 4
HEADS_PER_GROUP = 4
D_STATE = 128
CHUNK = 128
SSD_CONV = 4
CONV_WIDTH = 31
D_CONV = 2048
EPS = 1e-6
XBC_W = 2048
B_OFF = 1024
C_OFF = 1536
F_LANE = 16
HALO = 32

ADAM_LR = 0.001
ADAM_B1 = 0.9
ADAM_B2 = 0.999
ADAM_EPS = 1e-08
ADAM_WD = 0.01
ADAM_STEP = 10

VMEM_LIMIT_BYTES = 56 * 1024 * 1024
ROW_TILE = 512
CONV_ROW_TILE = 256
CONV_COL_TILE = 512
ATTN_TILE = 256

NT = (((1,), (1,)), ((), ()))
TN = (((0,), (0,)), ((), ()))
HIGHEST = lax.Precision.HIGHEST
NEG = -1e30


def _cp(*sem):
    return pltpu.CompilerParams(dimension_semantics=sem if sem else None, vmem_limit_bytes=VMEM_LIMIT_BYTES)


def _sigmoid(x):
    return jax.nn.sigmoid(x)


def _silu(x):
    return x * _sigmoid(x)


def _dsilu(x):
    s = _sigmoid(x)
    return s * (1.0 + x * (1.0 - s))


def _softplus(x):
    return jnp.maximum(x, 0.0) + jnp.log(1.0 + jnp.exp(-jnp.abs(x)))


def _log_sigmoid(x):
    return jnp.minimum(x, 0.0) - jnp.log(1.0 + jnp.exp(-jnp.abs(x)))


def _dot(a, b, dims=None, precision=None):
    if dims is None:
        return jnp.dot(a, b, preferred_element_type=f32, precision=precision)
    return lax.dot_general(a, b, dims, preferred_element_type=f32, precision=precision)


def _mm_nn(a, b, out_dtype, name, tm=512, tn=1024):
    m, k = a.shape
    n = b.shape[1]
    tm, tn = min(tm, m), min(tn, n)

    def body(a_ref, b_ref, o_ref):
        o_ref[...] = _dot(a_ref[...], b_ref[...]).astype(o_ref.dtype)

    return pl.pallas_call(
        body, name=name, grid=(n // tn, m // tm),
        in_specs=[pl.BlockSpec((tm, k), lambda j, i: (i, 0)), pl.BlockSpec((k, tn), lambda j, i: (0, j))],
        out_specs=pl.BlockSpec((tm, tn), lambda j, i: (i, j)),
        out_shape=jax.ShapeDtypeStruct((m, n), out_dtype), compiler_params=_cp("parallel", "parallel"))(a, b)


def _mm_nt(pairs, out_dtype, name, tm=512, tn=512):
    m = pairs[0][0].shape[0]
    n = pairs[0][2].shape[0]
    tm, tn = min(tm, m), min(tn, n)
    npair = len(pairs)

    def body(*refs):
        o_ref = refs[-1]
        acc = None
        for p in range(npair):
            d = _dot(refs[2 * p][...].astype(bf16), refs[2 * p + 1][...], NT)
            acc = d if acc is None else acc + d
        o_ref[...] = acc.astype(o_ref.dtype)

    in_specs, args = [], []
    for a, acb, b, bcb, k in pairs:
        in_specs.append(pl.BlockSpec((tm, k), lambda j, i, acb=acb: (i, acb)))
        in_specs.append(pl.BlockSpec((tn, k), lambda j, i, bcb=bcb: (j, bcb)))
        args += [a, b]
    return pl.pallas_call(
        body, name=name, grid=(n // tn, m // tm), in_specs=in_specs,
        out_specs=pl.BlockSpec((tm, tn), lambda j, i: (i, j)),
        out_shape=jax.ShapeDtypeStruct((m, n), out_dtype), compiler_params=_cp("parallel", "parallel"))(*args)


def _mm_tn(a, b, name, a_cb=0, am=None, b_cb=0, bn=None, tn=1024, tk=512, blocked=False):
    k = a.shape[0]
    am = a.shape[1] if am is None else am
    bn = b.shape[1] if bn is None else bn
    tm = min(1024, am)
    tn, tk = min(tn, bn), min(tk, k)
    a_off, b_off = a_cb * (am // tm), b_cb * (bn // tn)

    def body(a_ref, b_ref, o_ref):
        @pl.when(pl.program_id(2) == 0)
        def _():
            o_ref[...] = jnp.zeros_like(o_ref)
        d = _dot(a_ref[...].astype(bf16), b_ref[...].astype(bf16), TN)
        o_ref[...] += d.reshape(o_ref.shape)

    if blocked:
        out_spec = pl.BlockSpec((1, tm, tn), lambda i, j, kk: (j, i, 0))
        out_shape = jax.ShapeDtypeStruct((bn // tn, am, tn), f32)
    else:
        out_spec = pl.BlockSpec((tm, tn), lambda i, j, kk: (i, j))
        out_shape = jax.ShapeDtypeStruct((am, bn), f32)
    return pl.pallas_call(
        body, name=name, grid=(am // tm, bn // tn, k // tk),
        in_specs=[pl.BlockSpec((tk, tm), lambda i, j, kk: (kk, a_off + i)),
                  pl.BlockSpec((tk, tn), lambda i, j, kk: (kk, b_off + j))],
        out_specs=out_spec, out_shape=out_shape,
        compiler_params=_cp("parallel", "parallel", "arbitrary"))(a, b)


def _rowspec(ts, w, cb=0):
    return pl.BlockSpec((ts, w), lambda i: (i, cb))


def _vecspec(w):
    return pl.BlockSpec((1, w), lambda i: (0, 0))


def _rms_fwd(x, g, name):
    s, d = x.shape
    ts = min(ROW_TILE, s)

    def body(x_ref, g_ref, u_ref):
        xv = x_ref[...]
        r = lax.rsqrt(jnp.mean(xv * xv, axis=-1, keepdims=True) + EPS)
        u_ref[...] = (xv * r * g_ref[...]).astype(bf16)

    return pl.pallas_call(
        body, name=name, grid=(s // ts,), in_specs=[_rowspec(ts, d), _vecspec(d)], out_specs=_rowspec(ts, d),
        out_shape=jax.ShapeDtypeStruct((s, d), bf16), compiler_params=_cp("parallel"))(x, g)


def _rms_bwd_vals(xv, g, dy):
    r = lax.rsqrt(jnp.mean(xv * xv, axis=-1, keepdims=True) + EPS)
    xh = xv * r
    dg = jnp.sum(dy * xh, axis=0, keepdims=True)
    dxh = dy * g
    dx = r * (dxh - xh * jnp.mean(dxh * xh, axis=-1, keepdims=True))
    return dx, dg


def _gate0_fwd(y, z, o, ssd_norm):
    s = y.shape[0]
    ts = min(ROW_TILE, s)
    gw = D_MODEL // N_GROUPS

    def body(y_ref, zs_ref, zf_ref, o_ref, w_ref, cat_ref):
        yg = y_ref[...].astype(f32) * _silu(zs_ref[...].astype(f32))
        for g in range(N_GROUPS):
            seg = yg[:, gw * g:gw * (g + 1)]
            r = lax.rsqrt(jnp.mean(seg * seg, axis=-1, keepdims=True) + EPS)
            cat_ref[:, gw * g:gw * (g + 1)] = (seg * r * w_ref[:, gw * g:gw * (g + 1)]).astype(bf16)
        cat_ref[:, D_MODEL:] = (o_ref[...].astype(f32) * _silu(zf_ref[...].astype(f32))).astype(bf16)

    return pl.pallas_call(
        body, name="gate0_fwd", grid=(s // ts,),
        in_specs=[_rowspec(ts, D_MODEL), _rowspec(ts, D_MODEL, 0), _rowspec(ts, D_MODEL, 1), _rowspec(ts, D_MODEL),
                  _vecspec(D_MODEL)],
        out_specs=_rowspec(ts, 2 * D_MODEL),
        out_shape=jax.ShapeDtypeStruct((s, 2 * D_MODEL), bf16), compiler_params=_cp("parallel"))(y, z, z, o, ssd_norm)


def _post0_pre1(x, out0, g_post0, g_pre1):
    s, d = x.shape
    ts = min(ROW_TILE, s)

    def body(x_ref, o_ref, gp_ref, gn_ref, x1_ref, u1_ref):
        ov = o_ref[...]
        r = lax.rsqrt(jnp.mean(ov * ov, axis=-1, keepdims=True) + EPS)
        x1 = x_ref[...] + ov * r * gp_ref[...]
        x1_ref[...] = x1
        r1 = lax.rsqrt(jnp.mean(x1 * x1, axis=-1, keepdims=True) + EPS)
        u1_ref[...] = (x1 * r1 * gn_ref[...]).astype(bf16)

    return pl.pallas_call(
        body, name="post0_pre1", grid=(s // ts,),
        in_specs=[_rowspec(ts, d), _rowspec(ts, d), _vecspec(d), _vecspec(d)],
        out_specs=[_rowspec(ts, d), _rowspec(ts, d)],
        out_shape=[jax.ShapeDtypeStruct((s, d), f32), jax.ShapeDtypeStruct((s, d), bf16)],
        compiler_params=_cp("parallel"))(x, out0, g_post0, g_pre1)


def _ln_vals(hc, g, b):
    mu = jnp.mean(hc, axis=-1, keepdims=True)
    xc = hc - mu
    rstd = lax.rsqrt(jnp.mean(xc * xc, axis=-1, keepdims=True) + EPS)
    xh = xc * rstd
    return xh, rstd, xh * g + b


def _ln_gate_fwd(hc, proj1, ln_g, ln_b):
    s = hc.shape[0]
    ts = min(ROW_TILE, s)

    def body(hc_ref, z_ref, g_ref, b_ref, h3_ref):
        _, _, ln = _ln_vals(hc_ref[...].astype(f32), g_ref[...], b_ref[...])
        h3_ref[...] = (_silu(ln) * _silu(z_ref[...].astype(f32))).astype(bf16)

    return pl.pallas_call(
        body, name="ln_gate_fwd", grid=(s // ts,),
        in_specs=[_rowspec(ts, D_CONV), _rowspec(ts, D_CONV, 2), _vecspec(D_CONV), _vecspec(D_CONV)],
        out_specs=_rowspec(ts, D_CONV),
        out_shape=jax.ShapeDtypeStruct((s, D_CONV), bf16), compiler_params=_cp("parallel"))(hc, proj1, ln_g, ln_b)


def _final_loss(x1, out1, tgt, g_post1):
    s, d = x1.shape
    ts = min(ROW_TILE, s)

    def body(x1_ref, o_ref, t_ref, g_ref, dy_ref, do_ref, dg_ref, loss_ref):
        i = pl.program_id(0)

        @pl.when(i == 0)
        def _():
            dg_ref[...] = jnp.zeros_like(dg_ref)
            loss_ref[...] = jnp.zeros_like(loss_ref)
        ov = o_ref[...]
        g = g_ref[...]
        r = lax.rsqrt(jnp.mean(ov * ov, axis=-1, keepdims=True) + EPS)
        diff = x1_ref[...] + ov * r * g - t_ref[...]
        row = jnp.mean(diff * diff, axis=-1, keepdims=True)
        loss_ref[...] += jnp.broadcast_to(0.5 * jnp.sum(row, axis=0, keepdims=True), loss_ref.shape)
        dy = diff * (1.0 / d)
        dy_ref[...] = dy
        dx, dg = _rms_bwd_vals(ov, g, dy)
        do_ref[...] = dx.astype(bf16)
        dg_ref[...] += dg

    return pl.pallas_call(
        body, name="final_loss", grid=(s // ts,),
        in_specs=[_rowspec(ts, d), _rowspec(ts, d), _rowspec(ts, d), _vecspec(d)],
        out_specs=[_rowspec(ts, d), _rowspec(ts, d), _vecspec(d), _vecspec(128)],
        out_shape=[jax.ShapeDtypeStruct((s, d), f32), jax.ShapeDtypeStruct((s, d), bf16),
                   jax.ShapeDtypeStruct((1, d), f32), jax.ShapeDtypeStruct((1, 128), f32)],
        compiler_params=_cp("arbitrary"))(x1, out1, tgt, g_post1)


def _ln_gate_bwd(hc, proj1, dh3, ln_g, ln_b):
    s = hc.shape[0]
    ts = min(ROW_TILE, s)

    def body(hc_ref, z_ref, dh_ref, g_ref, b_ref, dhc_ref, dz_ref, dg_ref, db_ref):
        @pl.when(pl.program_id(0) == 0)
        def _():
            dg_ref[...] = jnp.zeros_like(dg_ref)
            db_ref[...] = jnp.zeros_like(db_ref)
        g = g_ref[...]
        xh, rstd, ln = _ln_vals(hc_ref[...].astype(f32), g, b_ref[...])
        zv = z_ref[...].astype(f32)
        dh3 = dh_ref[...].astype(f32)
        dz_ref[...] = (dh3 * _silu(ln) * _dsilu(zv)).astype(bf16)
        dln = dh3 * _silu(zv) * _dsilu(ln)
        dg_ref[...] += jnp.sum(dln * xh, axis=0, keepdims=True)
        db_ref[...] += jnp.sum(dln, axis=0, keepdims=True)
        dxh = dln * g
        dhc = rstd * (dxh - jnp.mean(dxh, axis=-1, keepdims=True) - xh * jnp.mean(dxh * xh, axis=-1, keepdims=True))
        dhc_ref[...] = dhc.astype(bf16)

    return pl.pallas_call(
        body, name="ln_gate_bwd", grid=(s // ts,),
        in_specs=[_rowspec(ts, D_CONV), _rowspec(ts, D_CONV, 2), _rowspec(ts, D_CONV), _vecspec(D_CONV),
                  _vecspec(D_CONV)],
        out_specs=[_rowspec(ts, D_CONV), _rowspec(ts, D_CONV), _vecspec(D_CONV), _vecspec(D_CONV)],
        out_shape=[jax.ShapeDtypeStruct((s, D_CONV), bf16), jax.ShapeDtypeStruct((s, D_CONV), bf16),
                   jax.ShapeDtypeStruct((1, D_CONV), f32), jax.ShapeDtypeStruct((1, D_CONV), f32)],
        compiler_params=_cp("arbitrary"))(hc, proj1, dh3, ln_g, ln_b)


def _mid_bwd(x1, du1, dy, out0, g_pre1, g_post0):
    s, d = x1.shape
    ts = min(ROW_TILE, s)

    def body(x1_ref, du_ref, dy_ref, o_ref, gn_ref, gp_ref, dx1_ref, do_ref, dgn_ref, dgp_ref):
        @pl.when(pl.program_id(0) == 0)
        def _():
            dgn_ref[...] = jnp.zeros_like(dgn_ref)
            dgp_ref[...] = jnp.zeros_like(dgp_ref)
        dxa, dgn = _rms_bwd_vals(x1_ref[...], gn_ref[...], du_ref[...])
        dx1 = dy_ref[...] + dxa
        dx1_ref[...] = dx1
        dgn_ref[...] += dgn
        dxo, dgp = _rms_bwd_vals(o_ref[...], gp_ref[...], dx1)
        do_ref[...] = dxo.astype(bf16)
        dgp_ref[...] += dgp

    return pl.pallas_call(
        body, name="mid_bwd", grid=(s // ts,),
        in_specs=[_rowspec(ts, d)] * 4 + [_vecspec(d), _vecspec(d)],
        out_specs=[_rowspec(ts, d), _rowspec(ts, d), _vecspec(d), _vecspec(d)],
        out_shape=[jax.ShapeDtypeStruct((s, d), f32), jax.ShapeDtypeStruct((s, d), bf16),
                   jax.ShapeDtypeStruct((1, d), f32), jax.ShapeDtypeStruct((1, d), f32)],
        compiler_params=_cp("arbitrary"))(x1, du1, dy, out0, g_pre1, g_post0)


def _first_bwd(x, du0, dx1, g_pre0):
    s, d = x.shape
    ts = min(ROW_TILE, s)

    def body(x_ref, du_ref, dx1_ref, g_ref, dx_ref, dg_ref):
        @pl.when(pl.program_id(0) == 0)
        def _():
            dg_ref[...] = jnp.zeros_like(dg_ref)
        dxa, dg = _rms_bwd_vals(x_ref[...], g_ref[...], du_ref[...])
        dx_ref[...] = dx1_ref[...] + dxa
        dg_ref[...] += dg

    return pl.pallas_call(
        body, name="first_bwd", grid=(s // ts,),
        in_specs=[_rowspec(ts, d)] * 3 + [_vecspec(d)],
        out_specs=[_rowspec(ts, d), _vecspec(d)],
        out_shape=[jax.ShapeDtypeStruct((s, d), f32), jax.ShapeDtypeStruct((1, d), f32)],
        compiler_params=_cp("arbitrary"))(x, du0, dx1, g_pre0)


def _gate0_bwd(y, z, o, dcat, ssd_norm):
    s = y.shape[0]
    ts = min(ROW_TILE, s)
    gw = D_MODEL // N_GROUPS

    def body(y_ref, zs_ref, zf_ref, o_ref, dn_ref, dg_ref, w_ref, dy_ref, do_ref, dz_ref, delta_ref, dw_ref):
        @pl.when(pl.program_id(0) == 0)
        def _():
            dw_ref[...] = jnp.zeros_like(dw_ref)
        yv = y_ref[...].astype(f32)
        zs = zs_ref[...].astype(f32)
        sz = _silu(zs)
        yg = yv * sz
        dyn = dn_ref[...].astype(f32)
        for g in range(N_GROUPS):
            sl = slice(gw * g, gw * (g + 1))
            seg = yg[:, sl]
            r = lax.rsqrt(jnp.mean(seg * seg, axis=-1, keepdims=True) + EPS)
            yh = seg * r
            dn = dyn[:, sl]
            dw_ref[:, sl] += jnp.sum(dn * yh, axis=0, keepdims=True)
            dyh = dn * w_ref[:, sl]
            dyg = r * (dyh - yh * jnp.mean(dyh * yh, axis=-1, keepdims=True))
            dy_ref[:, sl] = (dyg * sz[:, sl]).astype(bf16)
            dz_ref[:, sl] = (dyg * yv[:, sl] * _dsilu(zs[:, sl])).astype(bf16)
        zf = zf_ref[...].astype(f32)
        ov = o_ref[...].astype(f32)
        dog = dg_ref[...].astype(f32)
        dov = (dog * _silu(zf)).astype(bf16)
        do_ref[...] = dov
        dz_ref[:, D_MODEL:] = (dog * ov * _dsilu(zf)).astype(bf16)
        prod = dov.astype(f32) * ov
        lane = lax.broadcasted_iota(jnp.int32, (ts, 128), 1)
        delta = jnp.zeros((ts, 128), f32)
        for h in range(N_HEADS):
            dh = jnp.sum(prod[:, HEAD_DIM * h:HEAD_DIM * (h + 1)], axis=-1, keepdims=True)
            delta = delta + jnp.where(lane == h, dh, 0.0)
        delta_ref[...] = delta.T

    return pl.pallas_call(
        body, name="gate0_bwd", grid=(s // ts,),
        in_specs=[_rowspec(ts, D_MODEL), _rowspec(ts, D_MODEL, 0), _rowspec(ts, D_MODEL, 1), _rowspec(ts, D_MODEL),
                  _rowspec(ts, D_MODEL, 0), _rowspec(ts, D_MODEL, 1), _vecspec(D_MODEL)],
        out_specs=[_rowspec(ts, D_MODEL), _rowspec(ts, D_MODEL), _rowspec(ts, 2 * D_MODEL),
                   pl.BlockSpec((128, ts), lambda i: (0, i)), _vecspec(D_MODEL)],
        out_shape=[jax.ShapeDtypeStruct((s, D_MODEL), bf16), jax.ShapeDtypeStruct((s, D_MODEL), bf16),
                   jax.ShapeDtypeStruct((s, 2 * D_MODEL), bf16), jax.ShapeDtypeStruct((128, s), f32),
                   jax.ShapeDtypeStruct((1, D_MODEL), f32)],
        compiler_params=_cp("arbitrary"))(y, z, z, o, dcat, dcat, ssd_norm)


def _conv_grid(s, c):
    ts, cb = min(CONV_ROW_TILE, s), min(CONV_COL_TILE, c)
    return ts, cb, (c // cb, s // ts)


def _cur(ts, cb, off=0):
    return pl.BlockSpec((ts, cb), lambda c, i: (i, c + off))


def _prev_halo(ts, cb, off=0):
    return pl.BlockSpec((HALO, cb), lambda c, i: (jnp.maximum(i * (ts // HALO) - 1, 0), c + off))


def _next_halo(ts, cb, s, off=0):
    return pl.BlockSpec((HALO, cb), lambda c, i: (jnp.minimum((i + 1) * (ts // HALO), s // HALO - 1), c + off))


def _wspec(k, cb):
    return pl.BlockSpec((k, cb), lambda c, i: (0, c))


def _conv_taps(ext_ref, w_ref, ts, k_taps, base):
    acc = None
    for k in range(k_taps):
        t = w_ref[k:k + 1, :] * ext_ref[pl.ds(base + k, ts), :]
        acc = t if acc is None else acc + t
    return acc


def _conv_ssd_fwd(xraw, w, b):
    s, c = xraw.shape
    ts, cb, grid = _conv_grid(s, c)

    def body(x_ref, xh_ref, w_ref, b_ref, pre_ref, act_ref, ext_ref):
        first = pl.program_id(1) == 0
        ext_ref[0:HALO, :] = jnp.where(first, 0.0, xh_ref[...].astype(f32))
        ext_ref[HALO:, :] = x_ref[...].astype(f32)
        pre = b_ref[...] + _conv_taps(ext_ref, w_ref, ts, SSD_CONV, HALO - (SSD_CONV - 1))
        pre_ref[...] = pre.astype(bf16)
        act_ref[...] = _silu(pre).astype(bf16)

    return pl.pallas_call(
        body, name="conv_ssd_fwd", grid=grid,
        in_specs=[_cur(ts, cb), _prev_halo(ts, cb), _wspec(SSD_CONV, cb), _wspec(1, cb)],
        out_specs=[_cur(ts, cb), _cur(ts, cb)],
        out_shape=[jax.ShapeDtypeStruct((s, c), bf16)] * 2,
        scratch_shapes=[pltpu.VMEM((HALO + ts, cb), f32)],
        compiler_params=_cp("parallel", "parallel"))(xraw, xraw, w, b)


def _conv_glu_fwd(proj1, w, b):
    s = proj1.shape[0]
    c = D_CONV
    ts, cb, grid = _conv_grid(s, c)
    goff = c // cb

    def body(v_ref, g_ref, vh_ref, gh_ref, w_ref, b_ref, hc_ref, ext_ref):
        first = pl.program_id(1) == 0
        hh = vh_ref[...].astype(f32) * _sigmoid(gh_ref[...].astype(f32))
        ext_ref[0:HALO, :] = jnp.where(first, 0.0, hh)
        ext_ref[HALO:, :] = v_ref[...].astype(f32) * _sigmoid(g_ref[...].astype(f32))
        hc = b_ref[...] + _conv_taps(ext_ref, w_ref, ts, CONV_WIDTH, HALO - (CONV_WIDTH - 1))
        hc_ref[...] = hc.astype(bf16)

    return pl.pallas_call(
        body, name="conv_glu_fwd", grid=grid,
        in_specs=[_cur(ts, cb), _cur(ts, cb, goff), _prev_halo(ts, cb), _prev_halo(ts, cb, goff),
                  _wspec(CONV_WIDTH, cb), _wspec(1, cb)],
        out_specs=_cur(ts, cb),
        out_shape=jax.ShapeDtypeStruct((s, c), bf16),
        scratch_shapes=[pltpu.VMEM((HALO + ts, cb), f32)],
        compiler_params=_cp("parallel", "parallel"))(proj1, proj1, proj1, proj1, w, b)


def _conv_bwd_core(dp, dpn_ref, last, w_ref, dext_ref, xext_ref, dw_ref, db_ref, ts, k_taps):
    dext_ref[0:ts, :] = dp
    dext_ref[ts:, :] = jnp.where(last, 0.0, dpn_ref[...].astype(f32))

    @pl.when(pl.program_id(1) == 0)
    def _():
        dw_ref[...] = jnp.zeros_like(dw_ref)
        db_ref[...] = jnp.zeros_like(db_ref)
    dx = None
    for k in range(k_taps):
        t = w_ref[k:k + 1, :] * dext_ref[pl.ds(k_taps - 1 - k, ts), :]
        dx = t if dx is None else dx + t
        dw_ref[k:k + 1, :] += jnp.sum(dp * xext_ref[pl.ds(HALO - (k_taps - 1) + k, ts), :], axis=0, keepdims=True)
    db_ref[...] += jnp.sum(dp, axis=0, keepdims=True)
    return dx


def _conv_ssd_bwd(dpre, xraw, w):
    s, c = xraw.shape
    ts, cb, grid = _conv_grid(s, c)
    nb = s // ts

    def body(dp_ref, dpn_ref, x_ref, xh_ref, w_ref, dx_ref, dw_ref, db_ref, dext_ref, xext_ref):
        i = pl.program_id(1)
        xext_ref[0:HALO, :] = jnp.where(i == 0, 0.0, xh_ref[...].astype(f32))
        xext_ref[HALO:, :] = x_ref[...].astype(f32)
        dx = _conv_bwd_core(dp_ref[...].astype(f32), dpn_ref, i == nb - 1, w_ref, dext_ref, xext_ref, dw_ref, db_ref,
                            ts, SSD_CONV)
        dx_ref[...] = dx.astype(bf16)

    return pl.pallas_call(
        body, name="conv_ssd_bwd", grid=grid,
        in_specs=[_cur(ts, cb), _next_halo(ts, cb, s), _cur(ts, cb), _prev_halo(ts, cb), _wspec(SSD_CONV, cb)],
        out_specs=[_cur(ts, cb), _wspec(SSD_CONV, cb), _wspec(1, cb)],
        out_shape=[jax.ShapeDtypeStruct((s, c), bf16), jax.ShapeDtypeStruct((SSD_CONV, c), f32),
                   jax.ShapeDtypeStruct((1, c), f32)],
        scratch_shapes=[pltpu.VMEM((ts + HALO, cb), f32), pltpu.VMEM((HALO + ts, cb), f32)],
        compiler_params=_cp("parallel", "arbitrary"))(dpre, dpre, xraw, xraw, w)


def _conv_glu_bwd(dhc, proj1, w):
    s = proj1.shape[0]
    c = D_CONV
    ts, cb, grid = _conv_grid(s, c)
    nb = s // ts
    goff = c // cb

    def body(dp_ref, dpn_ref, v_ref, g_ref, vh_ref, gh_ref, w_ref, dv_ref, dg_ref, dw_ref, db_ref, dext_ref, xext_ref):
        i = pl.program_id(1)
        val = v_ref[...].astype(f32)
        sg = _sigmoid(g_ref[...].astype(f32))
        xext_ref[0:HALO, :] = jnp.where(i == 0, 0.0, vh_ref[...].astype(f32) * _sigmoid(gh_ref[...].astype(f32)))
        xext_ref[HALO:, :] = val * sg
        dh = _conv_bwd_core(dp_ref[...].astype(f32), dpn_ref, i == nb - 1, w_ref, dext_ref, xext_ref, dw_ref, db_ref,
                            ts, CONV_WIDTH)
        dv_ref[...] = (dh * sg).astype(bf16)
        dg_ref[...] = (dh * val * sg * (1.0 - sg)).astype(bf16)

    return pl.pallas_call(
        body, name="conv_glu_bwd", grid=grid,
        in_specs=[_cur(ts, cb), _next_halo(ts, cb, s), _cur(ts, cb), _cur(ts, cb, goff), _prev_halo(ts, cb),
                  _prev_halo(ts, cb, goff), _wspec(CONV_WIDTH, cb)],
        out_specs=[_cur(ts, cb), _cur(ts, cb), _wspec(CONV_WIDTH, cb), _wspec(1, cb)],
        out_shape=[jax.ShapeDtypeStruct((s, c), bf16), jax.ShapeDtypeStruct((s, c), bf16),
                   jax.ShapeDtypeStruct((CONV_WIDTH, c), f32), jax.ShapeDtypeStruct((1, c), f32)],
        scratch_shapes=[pltpu.VMEM((ts + HALO, cb), f32), pltpu.VMEM((HALO + ts, cb), f32)],
        compiler_params=_cp("parallel", "arbitrary"))(dhc, dhc, proj1, proj1, proj1, proj1, w)


def _ssd_common(dt_ref, prm_ref):
    l = CHUNK
    dtb = prm_ref[0:1, :]
    a = -jnp.exp(prm_ref[1:2, :])
    dsk = prm_ref[2:3, :]
    zraw = dt_ref[...] + dtb
    dt = _softplus(zraw)
    da = dt * a
    row = lax.broadcasted_iota(jnp.int32, (l, l), 0)
    col = lax.broadcasted_iota(jnp.int32, (l, l), 1)
    causal = row >= col
    cs = _dot(causal.astype(f32), da, precision=HIGHEST)
    return a, dsk, zraw, dt, cs, cs.T, causal, row, col


def _ssd_fwd(act, dtf, prm):
    s = act.shape[0]
    nc = s // CHUNK
    l = CHUNK

    def body(xs_ref, dt_ref, prm_ref, y_ref, hs_ref, st_ref):
        @pl.when(pl.program_id(0) == 0)
        def _():
            st_ref[...] = jnp.zeros_like(st_ref)
        a, dsk, _, dt, cs, cst, causal, _, _ = _ssd_common(dt_ref, prm_ref)
        for g in range(N_GROUPS):
            bm = xs_ref[:, B_OFF + D_STATE * g:B_OFF + D_STATE * (g + 1)]
            cm = xs_ref[:, C_OFF + D_STATE * g:C_OFF + D_STATE * (g + 1)]
            gmat = _dot(cm, bm, NT)
            for r in range(HEADS_PER_GROUP):
                h = HEADS_PER_GROUP * g + r
                hsl = slice(HEAD_DIM * h, HEAD_DIM * (h + 1))
                xv = xs_ref[:, hsl].astype(f32)
                csc = cs[:, h:h + 1]
                csr = cst[h:h + 1, :]
                cl = cs[l - 1:l, h:h + 1]
                dk = jnp.exp(jnp.where(causal, csc - csr, NEG))
                xd = xv * dt[:, h:h + 1]
                hp = st_ref[h]
                hs_ref[0, h] = hp
                ydiag = _dot((gmat * dk).astype(bf16), xd.astype(bf16))
                yoff = _dot(cm, hp.astype(bf16), NT) * jnp.exp(csc)
                y_ref[:, hsl] = (ydiag + yoff + xv * dsk[:, h:h + 1]).astype(bf16)
                st = _dot((xd * jnp.exp(cl - csc)).astype(bf16), bm, TN)
                st_ref[h] = hp * jnp.exp(cl) + st

    return pl.pallas_call(
        body, name="ssd_fwd", grid=(nc,),
        in_specs=[pl.BlockSpec((l, XBC_W), lambda i: (i, 0)), pl.BlockSpec((l, 128), lambda i: (i, 0)),
                  pl.BlockSpec((8, 128), lambda i: (0, 0))],
        out_specs=[pl.BlockSpec((l, D_MODEL), lambda i: (i, 0)),
                   pl.BlockSpec((1, N_HEADS, HEAD_DIM, D_STATE), lambda i: (i, 0, 0, 0))],
        out_shape=[jax.ShapeDtypeStruct((s, D_MODEL), bf16),
                   jax.ShapeDtypeStruct((nc, N_HEADS, HEAD_DIM, D_STATE), f32)],
        scratch_shapes=[pltpu.VMEM((N_HEADS, HEAD_DIM, D_STATE), f32)],
        compiler_params=_cp("arbitrary"))(act, dtf, prm)


def _ssd_bwd(act, pre, dtf, prm, hs, dy):
    s = act.shape[0]
    nc = s // CHUNK
    l = CHUNK

    def body(xs_ref, pre_ref, dt_ref, prm_ref, hs_ref, dy_ref, dpre_ref, ddt_ref, dprm_ref, dh_ref):
        @pl.when(pl.program_id(0) == 0)
        def _():
            dh_ref[...] = jnp.zeros_like(dh_ref)
            dprm_ref[...] = jnp.zeros_like(dprm_ref)
        a, dsk, zraw, dt, cs, cst, causal, row, col = _ssd_common(dt_ref, prm_ref)
        lane = lax.broadcasted_iota(jnp.int32, (l, 128), 1)
        rowl = lax.broadcasted_iota(jnp.int32, (l, 128), 0)
        sub = lax.broadcasted_iota(jnp.int32, (128, l), 0)
        lane1 = lax.broadcasted_iota(jnp.int32, (1, 128), 1)
        dcs_c = jnp.zeros((l, 128), f32)
        dcs_r = jnp.zeros((128, l), f32)
        ddt_c = jnp.zeros((l, 128), f32)
        dd_row = jnp.zeros((1, 128), f32)
        for g in range(N_GROUPS):
            bsl = slice(B_OFF + D_STATE * g, B_OFF + D_STATE * (g + 1))
            csl = slice(C_OFF + D_STATE * g, C_OFF + D_STATE * (g + 1))
            bm = xs_ref[:, bsl]
            cm = xs_ref[:, csl]
            gmat = _dot(cm, bm, NT)
            dgm = jnp.zeros((l, l), f32)
            dbg = jnp.zeros((l, D_STATE), f32)
            dcg = jnp.zeros((l, D_STATE), f32)
            for r in range(HEADS_PER_GROUP):
                h = HEADS_PER_GROUP * g + r
                hsl = slice(HEAD_DIM * h, HEAD_DIM * (h + 1))
                xv = xs_ref[:, hsl].astype(f32)
                dyv = dy_ref[:, hsl].astype(f32)
                dyb = dyv.astype(bf16)
                csc = cs[:, h:h + 1]
                csr = cst[h:h + 1, :]
                cl = cs[l - 1:l, h:h + 1]
                dk = jnp.exp(jnp.where(causal, csc - csr, NEG))
                mf = gmat * dk
                dtc = dt[:, h:h + 1]
                xd = xv * dtc
                xdb = xd.astype(bf16)
                ecs = jnp.exp(csc)
                dec = jnp.exp(cl)
                e = jnp.exp(cl - csc)
                hp = hs_ref[0, h]
                hpb = hp.astype(bf16)
                dhn = dh_ref[h]
                dhnb = dhn.astype(bf16)
                dd_h = jnp.sum(jnp.sum(dyv * xv, axis=1, keepdims=True), axis=0, keepdims=True)
                dx = dyv * dsk[:, h:h + 1]
                ch = _dot(cm, hpb, NT)
                dye = dyv * ecs
                dyeb = dye.astype(bf16)
                dcg = dcg + _dot(dyeb, hpb)
                dhp = _dot(dyeb, cm, TN)
                dcs_col = jnp.sum(dye * ch, axis=1, keepdims=True)
                dm = _dot(dyb, xdb, NT)
                dxd = _dot(mf.astype(bf16), dyb, TN)
                dgm = dgm + dm * dk
                wmat = dm * mf
                dcs_col = dcs_col + jnp.sum(wmat, axis=1, keepdims=True)
                dcs_row = -jnp.sum(wmat, axis=0, keepdims=True)
                ddec = jnp.sum(jnp.sum(hp * dhn, axis=1, keepdims=True), axis=0, keepdims=True)
                dxe = _dot(bm, dhnb, NT)
                dxd = dxd + dxe * e
                de_e = jnp.sum(dxe * xd, axis=1, keepdims=True) * e
                dbg = dbg + _dot((xd * e).astype(bf16), dhnb)
                dcs_col = dcs_col - de_e
                dlast = ddec * dec + jnp.sum(de_e, axis=0, keepdims=True)
                dh_ref[h] = dhp + dec * dhn
                dx = dx + dxd * dtc
                ddt_h = jnp.sum(dxd * xv, axis=1, keepdims=True)
                is_h = lane == h
                dcs_c = dcs_c + jnp.where(is_h, dcs_col, 0.0) + jnp.where(is_h & (rowl == l - 1), dlast, 0.0)
                dcs_r = dcs_r + jnp.where(sub == h, dcs_row, 0.0)
                ddt_c = ddt_c + jnp.where(is_h, ddt_h, 0.0)
                dd_row = dd_row + jnp.where(lane1 == h, dd_h, 0.0)
                dpre_ref[:, hsl] = (dx * _dsilu(pre_ref[:, hsl].astype(f32))).astype(bf16)
            dgb = dgm.astype(bf16)
            dcg = dcg + _dot(dgb, bm)
            dbg = dbg + _dot(dgb, cm, TN)
            dpre_ref[:, bsl] = (dbg * _dsilu(pre_ref[:, bsl].astype(f32))).astype(bf16)
            dpre_ref[:, csl] = (dcg * _dsilu(pre_ref[:, csl].astype(f32))).astype(bf16)
        dcs = dcs_c + dcs_r.T
        dda = _dot((row <= col).astype(f32), dcs, precision=HIGHEST)
        ddt = ddt_c + dda * a
        ddtraw = jnp.where(lane < N_HEADS, ddt * _sigmoid(zraw), 0.0)
        ddt_ref[...] = ddtraw
        dprm_ref[0:1, :] += jnp.sum(ddtraw, axis=0, keepdims=True)
        dprm_ref[1:2, :] += jnp.where(lane1 < N_HEADS, jnp.sum(dda * dt, axis=0, keepdims=True) * a, 0.0)
        dprm_ref[2:3, :] += dd_row

    def rev(i):
        return (nc - 1 - i, 0)

    return pl.pallas_call(
        body, name="ssd_bwd", grid=(nc,),
        in_specs=[pl.BlockSpec((l, XBC_W), rev), pl.BlockSpec((l, XBC_W), rev),
                  pl.BlockSpec((l, 128), rev), pl.BlockSpec((8, 128), lambda i: (0, 0)),
                  pl.BlockSpec((1, N_HEADS, HEAD_DIM, D_STATE), lambda i: (nc - 1 - i, 0, 0, 0)),
                  pl.BlockSpec((l, D_MODEL), rev)],
        out_specs=[pl.BlockSpec((l, XBC_W), rev), pl.BlockSpec((l, 128), rev), pl.BlockSpec((8, 128), lambda i: (0, 0))],
        out_shape=[jax.ShapeDtypeStruct((s, XBC_W), bf16), jax.ShapeDtypeStruct((s, 128), f32),
                   jax.ShapeDtypeStruct((8, 128), f32)],
        scratch_shapes=[pltpu.VMEM((N_HEADS, HEAD_DIM, D_STATE), f32)],
        compiler_params=_cp("arbitrary"))(act, pre, dtf, prm, hs, dy)


def _fox_cumsum(dtf, prm):
    s = dtf.shape[0]
    l = CHUNK

    def body(f_ref, prm_ref, c_ref, carry_ref):
        @pl.when(pl.program_id(0) == 0)
        def _():
            carry_ref[...] = jnp.zeros_like(carry_ref)
        lf = _log_sigmoid(f_ref[...] + prm_ref[3:4, :])
        row = lax.broadcasted_iota(jnp.int32, (l, l), 0)
        col = lax.broadcasted_iota(jnp.int32, (l, l), 1)
        c = _dot((row >= col).astype(f32), lf, precision=HIGHEST) + carry_ref[...]
        c_ref[...] = c
        carry_ref[...] = c[l - 1:l, :]

    return pl.pallas_call(
        body, name="fox_cumsum", grid=(s // l,),
        in_specs=[pl.BlockSpec((l, 128), lambda i: (i, 0)), pl.BlockSpec((8, 128), lambda i: (0, 0))],
        out_specs=pl.BlockSpec((l, 128), lambda i: (i, 0)),
        out_shape=jax.ShapeDtypeStruct((s, 128), f32),
        scratch_shapes=[pltpu.VMEM((1, 128), f32)],
        compiler_params=_cp("arbitrary"))(dtf, prm)


AUG = HEAD_DIM
N_PAIRS = N_HEADS // 2
V_BLOCK = 2 * D_MODEL // 128


def _split3(x):
    hi = x.astype(bf16)
    r1 = x - hi.astype(f32)
    mid = r1.astype(bf16)
    lo = (r1 - mid.astype(f32)).astype(bf16)
    return hi.astype(f32), mid.astype(f32), lo.astype(f32)


def _fox_prep(qkv, c):
    s = qkv.shape[0]
    ts = min(CONV_ROW_TILE, s)
    kb = D_MODEL // 128

    def body(q_ref, k_ref, c_ref, qa_ref, ka_ref):
        lane = lax.broadcasted_iota(jnp.int32, (ts, 128), 1)
        low = lane < HEAD_DIM
        for h in range(N_HEADS):
            psl = slice(128 * (h // 2), 128 * (h // 2 + 1))
            qv = q_ref[:, psl].astype(f32) * (HEAD_DIM ** -0.5)
            kv = k_ref[:, psl].astype(f32)
            if h % 2:
                qv = pltpu.roll(qv, HEAD_DIM, 1)
                kv = pltpu.roll(kv, HEAD_DIM, 1)
            hi, mid, lo = _split3(c_ref[:, F_LANE + h:F_LANE + h + 1])
            ones = jnp.where((lane >= AUG + 3) & (lane < AUG + 6), 1.0, 0.0)
            cq = jnp.where(lane == AUG, hi, jnp.where(lane == AUG + 1, mid, jnp.where(lane == AUG + 2, lo, ones)))
            qa_ref[h] = jnp.where(low, qv, cq).astype(bf16)
            onek = jnp.where((lane >= AUG) & (lane < AUG + 3), 1.0, 0.0)
            ck = jnp.where(lane == AUG + 3, -hi, jnp.where(lane == AUG + 4, -mid, jnp.where(lane == AUG + 5, -lo, onek)))
            ka_ref[h] = jnp.where(low, kv, ck).astype(bf16)

    hm = pl.BlockSpec((N_HEADS, ts, 128), lambda i: (0, i, 0))
    return pl.pallas_call(
        body, name="fox_prep", grid=(s // ts,),
        in_specs=[_rowspec(ts, D_MODEL, 0), _rowspec(ts, D_MODEL, 1), _rowspec(ts, 128)],
        out_specs=[hm, hm], out_shape=[jax.ShapeDtypeStruct((N_HEADS, s, 128), bf16)] * 2,
        compiler_params=_cp("parallel"))(qkv, qkv, c)


def _fox_fwd(qa, ka, qkv):
    s = qkv.shape[0]
    t = min(ATTN_TILE, s)
    nq = s // t

    def body(qa_ref, ka_ref, v_ref, o_ref, lse_ref):
        qi = pl.program_id(1)
        low = lax.broadcasted_iota(jnp.int32, (t, 128), 1) < HEAD_DIM
        row = lax.broadcasted_iota(jnp.int32, (t, t), 0)
        col = lax.broadcasted_iota(jnp.int32, (t, t), 1)

        def tile(ki, carry, diagonal):
            stats, acc = carry
            koff = pl.multiple_of(ki * t, t)
            v = v_ref[pl.ds(koff, t), :]
            vh = (jnp.where(low, v, jnp.zeros_like(v)), jnp.where(low, jnp.zeros_like(v), v))
            new_stats, alphas, pv = [], [], None
            for r in range(2):
                m_old, l_old = stats[r]
                sc = _dot(qa_ref[r], ka_ref[r, pl.ds(koff, t), :], NT)
                if diagonal:
                    sc = jnp.where(col <= row, sc, NEG)
                m_new = jnp.maximum(m_old, jnp.max(sc, axis=1, keepdims=True))
                p = jnp.exp(sc - m_new)
                alpha = jnp.exp(m_old - m_new)
                new_stats.append((m_new, alpha * l_old + jnp.sum(p, axis=1, keepdims=True)))
                alphas.append(alpha)
                d = _dot(p.astype(bf16), vh[r])
                pv = d if pv is None else pv + d
            acc = acc * jnp.where(low, alphas[0], alphas[1]) + pv
            return tuple(new_stats), acc

        init = (((jnp.full((t, 1), NEG, f32), jnp.zeros((t, 1), f32)),) * 2, jnp.zeros((t, 128), f32))
        carry = lax.fori_loop(0, qi, lambda ki, cr: tile(ki, cr, False), init)
        stats, acc = tile(qi, carry, True)
        o_ref[...] = (acc / jnp.where(low, stats[0][1], stats[1][1])).astype(bf16)
        for r in range(2):
            lse = stats[r][0] + jnp.log(stats[r][1])
            lse_ref[r] = jnp.broadcast_to(lse, (t, 128)).T[0:1, :]

    return pl.pallas_call(
        body, name="fox_fwd", grid=(N_PAIRS, nq),
        in_specs=[pl.BlockSpec((2, t, 128), lambda j, qi: (j, qi, 0)),
                  pl.BlockSpec((2, s, 128), lambda j, qi: (j, 0, 0)),
                  pl.BlockSpec((s, 128), lambda j, qi: (0, V_BLOCK + j))],
        out_specs=[pl.BlockSpec((t, 128), lambda j, qi: (qi, j)), pl.BlockSpec((2, 1, t), lambda j, qi: (j, 0, qi))],
        out_shape=[jax.ShapeDtypeStruct((s, D_MODEL), bf16), jax.ShapeDtypeStruct((N_HEADS, 1, s), f32)],
        compiler_params=_cp("parallel", "parallel"))(qa, ka, qkv)


def _fox_bwd(qa, ka, qkv, do, lse, delta):
    s = qkv.shape[0]
    t = min(ATTN_TILE, s)
    nq = s // t

    def body(qa_ref, ka_ref, v_ref, do_ref, lse_ref, dl_ref, dq_ref, dk_ref, dv_ref):
        ki = pl.program_id(1)

        @pl.when(ki == 0)
        def _():
            dq_ref[...] = jnp.zeros_like(dq_ref)
        low = lax.broadcasted_iota(jnp.int32, (t, 128), 1) < HEAD_DIM
        row = lax.broadcasted_iota(jnp.int32, (t, t), 0)
        col = lax.broadcasted_iota(jnp.int32, (t, t), 1)
        v = v_ref[...]
        zero = jnp.zeros_like(v)
        vh = (jnp.where(low, v, zero), jnp.where(low, zero, v))

        def tile(qi, carry, diagonal):
            dks, dv = carry
            qoff = pl.multiple_of(qi * t, t)
            dov = do_ref[pl.ds(qoff, t), :]
            doh = (jnp.where(low, dov, zero), jnp.where(low, zero, dov))
            new_dks = []
            for r in range(2):
                qt = qa_ref[r, pl.ds(qoff, t), :]
                sct = _dot(ka_ref[r], qt, NT)
                if diagonal:
                    sct = jnp.where(row <= col, sct, NEG)
                pt = jnp.exp(sct - lse_ref[r, :, pl.ds(qoff, t)])
                dpt = _dot(vh[r], dov, NT)
                dst = (pt * (dpt - dl_ref[r, :, pl.ds(qoff, t)])).astype(bf16)
                dv = dv + _dot(pt.astype(bf16), doh[r])
                new_dks.append(dks[r] + _dot(dst, qt))
                dq_ref[r, pl.ds(qoff, t), :] += _dot(dst, ka_ref[r], TN)
            return tuple(new_dks), dv

        zacc = jnp.zeros((t, 128), f32)
        carry = tile(ki, ((zacc, zacc), zacc), True)
        dks, dv = lax.fori_loop(ki + 1, nq, lambda qi, cr: tile(qi, cr, False), carry)
        dk_ref[0] = dks[0]
        dk_ref[1] = dks[1]
        dv_ref[...] = dv.astype(bf16)

    return pl.pallas_call(
        body, name="fox_bwd", grid=(N_PAIRS, nq),
        in_specs=[pl.BlockSpec((2, s, 128), lambda j, ki: (j, 0, 0)),
                  pl.BlockSpec((2, t, 128), lambda j, ki: (j, ki, 0)),
                  pl.BlockSpec((t, 128), lambda j, ki: (ki, V_BLOCK + j)),
                  pl.BlockSpec((s, 128), lambda j, ki: (0, j)),
                  pl.BlockSpec((2, 1, s), lambda j, ki: (j, 0, 0)),
                  pl.BlockSpec((2, 1, s), lambda j, ki: (j, 0, 0))],
        out_specs=[pl.BlockSpec((2, s, 128), lambda j, ki: (j, 0, 0)),
                   pl.BlockSpec((2, t, 128), lambda j, ki: (j, ki, 0)),
                   pl.BlockSpec((t, 128), lambda j, ki: (ki, j))],
        out_shape=[jax.ShapeDtypeStruct((N_HEADS, s, 128), f32), jax.ShapeDtypeStruct((N_HEADS, s, 128), f32),
                   jax.ShapeDtypeStruct((s, D_MODEL), bf16)],
        compiler_params=_cp("parallel", "arbitrary"))(qa, ka, qkv, do, lse, delta)


def _fox_bwd_post(dq_hm, dk_hm):
    s = dq_hm.shape[1]
    ts = min(CONV_ROW_TILE, s)

    def body(dq_ref, dk_ref, q_ref, k_ref, dc_ref):
        lane = lax.broadcasted_iota(jnp.int32, (ts, 128), 1)
        dc = jnp.zeros((ts, 128), f32)
        for h in range(N_HEADS):
            hsl = slice(HEAD_DIM * h, HEAD_DIM * (h + 1))
            dqv = dq_ref[h]
            dkv = dk_ref[h]
            q_ref[:, hsl] = (dqv[:, 0:HEAD_DIM] * (HEAD_DIM ** -0.5)).astype(bf16)
            k_ref[:, hsl] = dkv[:, 0:HEAD_DIM].astype(bf16)
            dc = dc + jnp.where(lane == F_LANE + h, dqv[:, AUG:AUG + 1] - dkv[:, AUG + 3:AUG + 4], 0.0)
        dc_ref[...] = dc

    hm = pl.BlockSpec((N_HEADS, ts, 128), lambda i: (0, i, 0))
    return pl.pallas_call(
        body, name="fox_bwd_post", grid=(s // ts,), in_specs=[hm, hm],
        out_specs=[_rowspec(ts, D_MODEL), _rowspec(ts, D_MODEL), _rowspec(ts, 128)],
        out_shape=[jax.ShapeDtypeStruct((s, D_MODEL), bf16), jax.ShapeDtypeStruct((s, D_MODEL), bf16),
                   jax.ShapeDtypeStruct((s, 128), f32)],
        compiler_params=_cp("parallel"))(dq_hm, dk_hm)


def _fox_gate_bwd(dc, dtf, prm, ddt_raw):
    s = dtf.shape[0]
    l = CHUNK
    nb = s // l

    def body(dc_ref, f_ref, prm_ref, ddt_ref, out_ref, dfb_ref, carry_ref):
        @pl.when(pl.program_id(0) == 0)
        def _():
            carry_ref[...] = jnp.zeros_like(carry_ref)
            dfb_ref[...] = jnp.zeros_like(dfb_ref)
        dc = dc_ref[...]
        row = lax.broadcasted_iota(jnp.int32, (l, l), 0)
        col = lax.broadcasted_iota(jnp.int32, (l, l), 1)
        dlf = _dot((row <= col).astype(f32), dc, precision=HIGHEST) + carry_ref[...]
        carry_ref[...] = dlf[0:1, :]
        lane = lax.broadcasted_iota(jnp.int32, (l, 128), 1)
        is_f = (lane >= F_LANE) & (lane < F_LANE + N_HEADS)
        dfr = jnp.where(is_f, dlf * _sigmoid(-(f_ref[...] + prm_ref[3:4, :])), 0.0)
        dfb_ref[...] += jnp.sum(dfr, axis=0, keepdims=True)
        out_ref[...] = ddt_ref[...] + dfr

    def rev(i):
        return (nb - 1 - i, 0)

    return pl.pallas_call(
        body, name="fox_gate_bwd", grid=(nb,),
        in_specs=[pl.BlockSpec((l, 128), rev), pl.BlockSpec((l, 128), rev), pl.BlockSpec((8, 128), lambda i: (0, 0)),
                  pl.BlockSpec((l, 128), rev)],
        out_specs=[pl.BlockSpec((l, 128), rev), pl.BlockSpec((1, 128), lambda i: (0, 0))],
        out_shape=[jax.ShapeDtypeStruct((s, 128), f32), jax.ShapeDtypeStruct((1, 128), f32)],
        scratch_shapes=[pltpu.VMEM((1, 128), f32)],
        compiler_params=_cp("arbitrary"))(dc, dtf, prm, ddt_raw)


def _position():
    return lax.axis_index("x"), lax.axis_index("y"), lax.axis_index("c")


def _all_gather(xl, name):
    r, c = xl.shape

    def body(x_ref, out_ref, send_sems, recv_sems, local_sem):
        x, y, cc = _position()
        me, sibling = (x, y, cc), (x, y, 1 - cc)
        chips = [(1 - x, y), (x, 1 - y), (1 - x, 1 - y)]

        def slot(px, py, pc):
            return out_ref.at[4 * px + 2 * py + pc]

        def copy(k, block, to, src=None):
            return pltpu.make_async_remote_copy(
                src_ref=slot(*block) if src is None else src, dst_ref=slot(*block),
                send_sem=send_sems.at[k], recv_sem=recv_sems.at[k],
                device_id=to, device_id_type=pl.DeviceIdType.MESH)

        mine = pltpu.make_async_copy(x_ref, slot(*me), local_sem)
        mine.start()
        first = [copy(0, me, sibling, src=x_ref)]
        first += [copy(1 + j, me, (*chip, cc), src=x_ref) for j, chip in enumerate(chips)]
        for cp in first:
            cp.start()
        passed = [copy(4 + j, (*chip, cc), sibling) for j, chip in enumerate(chips)]
        for j, chip in enumerate(chips):
            copy(1 + j, (*chip, cc), me).wait_recv()
            passed[j].start()
        copy(0, sibling, me).wait_recv()
        for j, chip in enumerate(chips):
            copy(4 + j, (*chip, 1 - cc), me).wait_recv()
        for cp in first + passed:
            cp.wait_send()
        mine.wait()

    return pl.pallas_call(
        body, name=name,
        out_shape=jax.ShapeDtypeStruct((N_DEV, r, c), xl.dtype),
        in_specs=[pl.BlockSpec(memory_space=pl.ANY)], out_specs=pl.BlockSpec(memory_space=pl.ANY),
        scratch_shapes=[pltpu.SemaphoreType.DMA((7,)), pltpu.SemaphoreType.DMA((7,)), pltpu.SemaphoreType.DMA],
    )(xl)


def _grad_exchange(gs):
    n = len(gs)

    def body(*refs):
        g_refs, r_refs = refs[:n], refs[n:2 * n]
        send_sems, recv_sems, local_sems = refs[2 * n:]
        x, y, cc = _position()
        me = 4 * x + 2 * y + cc
        local = [pltpu.make_async_copy(g_refs[a].at[me], r_refs[a].at[me], local_sems.at[a]) for a in range(n)]
        for cp in local:
            cp.start()
        sends, recvs = [], []
        for k in range(1, N_DEV):
            px = 1 - x if k & 4 else x
            py = 1 - y if k & 2 else y
            pc = 1 - cc if k & 1 else cc
            pid = 4 * px + 2 * py + pc
            for a in range(n):
                sends.append(pltpu.make_async_remote_copy(
                    src_ref=g_refs[a].at[pid], dst_ref=r_refs[a].at[me],
                    send_sem=send_sems.at[a, k - 1], recv_sem=recv_sems.at[a, k - 1],
                    device_id=(px, py, pc), device_id_type=pl.DeviceIdType.MESH))
                recvs.append(pltpu.make_async_remote_copy(
                    src_ref=g_refs[a].at[pid], dst_ref=r_refs[a].at[pid],
                    send_sem=send_sems.at[a, k - 1], recv_sem=recv_sems.at[a, k - 1],
                    device_id=(px, py, pc), device_id_type=pl.DeviceIdType.MESH))
        for cp in sends:
            cp.start()
        for cp in recvs:
            cp.wait_recv()
        for cp in sends:
            cp.wait_send()
        for cp in local:
            cp.wait()

    anyspec = pl.BlockSpec(memory_space=pl.ANY)
    return pl.pallas_call(
        body, name="grad_exchange",
        out_shape=[jax.ShapeDtypeStruct(g.shape, g.dtype) for g in gs],
        in_specs=[anyspec] * n, out_specs=[anyspec] * n,
        scratch_shapes=[pltpu.SemaphoreType.DMA((n, N_DEV - 1)), pltpu.SemaphoreType.DMA((n, N_DEV - 1)),
                        pltpu.SemaphoreType.DMA((n,))],
    )(*gs)


def _sum_parts(parts, name):
    n, r, c = parts.shape

    def body(p_ref, o_ref):
        g = p_ref[0]
        for i in range(1, n):
            g = g + p_ref[i]
        o_ref[...] = g

    return pl.pallas_call(body, name=name, out_shape=jax.ShapeDtypeStruct((r, c), f32))(parts)


def _adamw(w, m, v, parts, name, tr=128):
    r, c = w.shape
    n = parts.shape[0]
    tr = min(tr, r)
    c1 = 1.0 - ADAM_B1 ** ADAM_STEP
    c2 = 1.0 - ADAM_B2 ** ADAM_STEP

    def body(w_ref, m_ref, v_ref, p_ref, g_ref, d_ref, nm_ref, nv_ref):
        g = p_ref[0]
        for i in range(1, n):
            g = g + p_ref[i]
        g_ref[...] = g
        nm = ADAM_B1 * m_ref[...] + (1.0 - ADAM_B1) * g
        nv = ADAM_B2 * v_ref[...] + (1.0 - ADAM_B2) * (g * g)
        nm_ref[...] = nm
        nv_ref[...] = nv
        d_ref[...] = -ADAM_LR * ((nm / c1) / (jnp.sqrt(nv / c2) + ADAM_EPS) + ADAM_WD * w_ref[...])

    blk = pl.BlockSpec((tr, c), lambda i: (i, 0))
    return pl.pallas_call(
        body, name=name, grid=(r // tr,),
        in_specs=[blk, blk, blk, pl.BlockSpec((n, tr, c), lambda i: (0, i, 0))],
        out_specs=[blk] * 4, out_shape=[jax.ShapeDtypeStruct((r, c), f32)] * 4,
        compiler_params=_cp("parallel"))(w, m, v, parts)


def _lanes(w):
    return -(-w // 128) * 128


def _pack(arrs):
    rows = []
    for a in arrs:
        k, w = a.shape
        if w % 128:
            a = jnp.pad(a, ((0, 0), (0, _lanes(w) - w)))
        rows.append(a.reshape(-1, 128))
    out = jnp.concatenate(rows, axis=0)
    pad = -out.shape[0] % 8
    return jnp.pad(out, ((0, pad), (0, 0))) if pad else out


def _unpack(packed, shapes):
    outs, off = [], 0
    lead = packed.shape[:-2]
    for k, w in shapes:
        nrow = k * _lanes(w) // 128
        a = packed[..., off:off + nrow, :].reshape(*lead, k, _lanes(w))[..., :w]
        outs.append(a)
        off += nrow
    return outs


def _gathered_cols(a):
    n, k, wl = a.shape
    return jnp.transpose(a, (1, 0, 2)).reshape(k, n * wl)


def _col_shards(a):
    k, w = a.shape
    return jnp.transpose(a.reshape(k, N_DEV, w // N_DEV), (1, 0, 2))


SMALL_PARAMS = (
    ("e_norm_pre", 1, 1024, False), ("e_conv_w", 4, 2048, True), ("e_conv_b", 1, 2048, False),
    ("e_dt_bias", 1, 16, False), ("e_a_log", 1, 16, False), ("e_d_skip", 1, 16, False), ("e_fgate_b", 1, 16, False),
    ("e_ssd_norm", 1, 1024, False), ("e_norm_post", 1, 1024, False), ("o_norm_pre", 1, 1024, True),
    ("o_conv_w", 31, 2048, True), ("o_conv_b", 1, 2048, True), ("o_ln_g", 1, 2048, True), ("o_ln_b", 1, 2048, True),
    ("o_norm_post", 1, 1024, True),
)
BIG_PARAMS = ("e_w_in", "e_w_out", "o_w_in", "o_w_out")
WEIGHT_ORDER = ("e_norm_pre", "e_w_in", "e_conv_w", "e_conv_b", "e_dt_bias", "e_a_log", "e_d_skip", "e_fgate_b",
                "e_ssd_norm", "e_w_out", "e_norm_post", "o_norm_pre", "o_w_in", "o_conv_w", "o_conv_b", "o_ln_g",
                "o_ln_b", "o_w_out", "o_norm_post")
E_IN = 7200
O_IN = 6144


def kernel(x, e_norm_pre, e_w_in, e_conv_w, e_conv_b, e_dt_bias, e_a_log, e_d_skip, e_fgate_b, e_ssd_norm, e_w_out, e_norm_post, o_norm_pre, o_w_in, o_conv_w, o_conv_b, o_ln_g, o_ln_b, o_w_out, o_norm_post, loss_target, m_e_norm_pre, m_e_w_in, m_e_conv_w, m_e_conv_b, m_e_dt_bias, m_e_a_log, m_e_d_skip, m_e_fgate_b, m_e_ssd_norm, m_e_w_out, m_e_norm_post, m_o_norm_pre, m_o_w_in, m_o_conv_w, m_o_conv_b, m_o_ln_g, m_o_ln_b, m_o_w_out, m_o_norm_post, v_e_norm_pre, v_e_w_in, v_e_conv_w, v_e_conv_b, v_e_dt_bias, v_e_a_log, v_e_d_skip, v_e_fgate_b, v_e_ssd_norm, v_e_w_out, v_e_norm_post, v_o_norm_pre, v_o_w_in, v_o_conv_w, v_o_conv_b, v_o_ln_g, v_o_ln_b, v_o_w_out, v_o_norm_post):
    given = dict(locals())
    w_in = {n: given[n] for n in WEIGHT_ORDER}
    m_in = {n: given["m_" + n] for n in WEIGHT_ORDER}
    v_in = {n: given["v_" + n] for n in WEIGHT_ORDER}

    def mat(a):
        return a.reshape(a.shape[-2:])

    xs = mat(x)
    tgt = mat(loss_target)
    xi, yi, ci = _position()
    me = 4 * xi + 2 * yi + ci
    ew, ow = E_IN // N_DEV, O_IN // N_DEV
    wr = D_CONV // N_DEV

    big_local = jnp.concatenate([
        mat(e_w_in).astype(bf16).reshape(ew, D_MODEL), mat(o_w_in).astype(bf16).reshape(ow, D_MODEL),
        mat(e_w_out).astype(bf16), mat(o_w_out).astype(bf16)], axis=0)
    nbig = big_local.shape[0]
    big_local = jnp.pad(big_local, ((0, -nbig % 16), (0, 0)))
    wg = _all_gather(big_local, "gather_weights")
    o0, o1, o2, o3 = 0, ew, ew + ow, ew + ow + wr
    e_w_in_f = _gathered_cols(wg[:, o0:o1].reshape(N_DEV, D_MODEL, ew))
    o_w_in_f = _gathered_cols(wg[:, o1:o2].reshape(N_DEV, D_MODEL, ow))
    e_w_out_f = wg[:, o2:o3].reshape(D_CONV, D_MODEL)
    o_w_out_f = wg[:, o3:o3 + wr].reshape(D_CONV, D_MODEL)
    w_z, w_xbc = e_w_in_f[:, 0:2048], e_w_in_f[:, 2048:4096]
    w_qkv = e_w_in_f[:, 4112:7184]
    w_dtf = jnp.concatenate([e_w_in_f[:, 4096:4112], e_w_in_f[:, 7184:7200], jnp.zeros((D_MODEL, 96), bf16)], axis=1)

    sharded_small = [(n, k, w) for n, k, w, sh in SMALL_PARAMS if sh]
    sg = _all_gather(_pack([mat(w_in[n]) for n, _, _ in sharded_small]), "gather_small_weights")
    full_small = {n: _gathered_cols(a)
                  for (n, _, _), a in zip(sharded_small, _unpack(sg, [(k, w // N_DEV) for _, k, w in sharded_small]))}
    for n, _, _, sh in SMALL_PARAMS:
        if not sh:
            full_small[n] = mat(w_in[n])
    p = full_small
    prm = jnp.zeros((8, 128), f32)
    prm = prm.at[0, 0:16].set(p["e_dt_bias"][0]).at[1, 0:16].set(p["e_a_log"][0]).at[2, 0:16].set(p["e_d_skip"][0])
    prm = prm.at[3, F_LANE:F_LANE + 16].set(p["e_fgate_b"][0])

    u0 = _rms_fwd(xs, p["e_norm_pre"], "rms_pre0")
    z0 = _mm_nn(u0, w_z, bf16, "proj0_z")
    xraw = _mm_nn(u0, w_xbc, bf16, "proj0_xbc")
    qkv = _mm_nn(u0, w_qkv, bf16, "proj0_qkv")
    dtf = _mm_nn(u0, w_dtf, f32, "proj0_dtf")
    pre, act = _conv_ssd_fwd(xraw, p["e_conv_w"], p["e_conv_b"])
    y, hs = _ssd_fwd(act, dtf, prm)
    qa, ka = _fox_prep(qkv, _fox_cumsum(dtf, prm))
    o, lse = _fox_fwd(qa, ka, qkv)
    cat = _gate0_fwd(y, z0, o, p["e_ssd_norm"])
    out0 = _mm_nn(cat, e_w_out_f, f32, "out0")
    x1, u1 = _post0_pre1(xs, out0, p["e_norm_post"], p["o_norm_pre"])

    proj1 = _mm_nn(u1, o_w_in_f, bf16, "proj1")
    hc = _conv_glu_fwd(proj1, p["o_conv_w"], p["o_conv_b"])
    h3 = _ln_gate_fwd(hc, proj1, p["o_ln_g"], p["o_ln_b"])
    out1 = _mm_nn(h3, o_w_out_f, f32, "out1")
    dy, d_out1, dg_post1, loss_part = _final_loss(x1, out1, tgt, p["o_norm_post"])

    dh3 = _mm_nt([(d_out1, 0, o_w_out_f, 0, D_MODEL)], bf16, "dh3")
    g_o_w_out = _mm_tn(h3, d_out1, "dw_out1")
    dhc, dz1, dg_ln, db_ln = _ln_gate_bwd(hc, proj1, dh3, p["o_ln_g"], p["o_ln_b"])
    dval, dgate, dw_conv1, db_conv1 = _conv_glu_bwd(dhc, proj1, p["o_conv_w"])
    dproj1 = jnp.concatenate([dval, dgate, dz1], axis=1)
    du1 = _mm_nt([(dproj1, 0, o_w_in_f, 0, O_IN)], f32, "du1")
    g_o_w_in = _mm_tn(u1, dproj1, "dw_in1", tn=ow, blocked=True)
    dx1, d_out0, dg_pre1, dg_post0 = _mid_bwd(x1, du1, dy, out0, p["o_norm_pre"], p["e_norm_post"])

    dcat = _mm_nt([(d_out0, 0, e_w_out_f, 0, D_MODEL)], bf16, "dcat")
    g_e_w_out = _mm_tn(cat, d_out0, "dw_out0")
    dy_ssd, do, dz0, delta, dg_ssd_norm = _gate0_bwd(y, z0, o, dcat, p["e_ssd_norm"])
    dq_hm, dk_hm, dv = _fox_bwd(qa, ka, qkv, do, lse, delta[0:N_HEADS].reshape(N_HEADS, 1, -1))
    dq, dk, dc = _fox_bwd_post(dq_hm, dk_hm)
    dpre, ddt_raw, dprm = _ssd_bwd(act, pre, dtf, prm, hs, dy_ssd)
    ddtf, dfb = _fox_gate_bwd(dc, dtf, prm, ddt_raw)
    dxraw, dw_conv0, db_conv0 = _conv_ssd_bwd(dpre, xraw, p["e_conv_w"])
    du0 = _mm_nt([(dz0, 0, w_z, 0, 2048), (dxraw, 0, w_xbc, 0, 2048), (dq, 0, w_qkv, 0, 1024), (dk, 0, w_qkv, 1, 1024),
                  (dv, 0, w_qkv, 2, 1024), (ddtf, 0, w_dtf, 0, 128)], f32, "du0")
    gw_dtf = _mm_tn(u0, ddtf, "dw_in0_dtf")
    g_e_w_in_full = jnp.concatenate([
        _mm_tn(u0, dz0, "dw_in0_z"), _mm_tn(u0, dxraw, "dw_in0_xbc"), gw_dtf[:, 0:16],
        _mm_tn(u0, dq, "dw_in0_q"), _mm_tn(u0, dk, "dw_in0_k"), _mm_tn(u0, dv, "dw_in0_v"), gw_dtf[:, 16:32]], axis=1)
    grad_x, dg_pre0 = _first_bwd(xs, du0, dx1, p["e_norm_pre"])

    big_parts = _grad_exchange([
        _col_shards(g_e_w_in_full), g_e_w_out.reshape(N_DEV, wr, D_MODEL), g_o_w_in,
        g_o_w_out.reshape(N_DEV, wr, D_MODEL)])
    outs = {}
    for n, parts in zip(BIG_PARAMS, big_parts):
        outs[n] = _adamw(mat(w_in[n]), mat(m_in[n]), mat(v_in[n]), parts, "adamw_" + n)

    small_grads = {
        "e_norm_pre": dg_pre0, "e_conv_w": dw_conv0, "e_conv_b": db_conv0, "e_dt_bias": dprm[0:1, 0:16],
        "e_a_log": dprm[1:2, 0:16], "e_d_skip": dprm[2:3, 0:16], "e_fgate_b": dfb[:, F_LANE:F_LANE + 16],
        "e_ssd_norm": dg_ssd_norm, "e_norm_post": dg_post0, "o_norm_pre": dg_pre1, "o_conv_w": dw_conv1,
        "o_conv_b": db_conv1, "o_ln_g": dg_ln, "o_ln_b": db_ln, "o_norm_post": dg_post1,
    }
    gathered = _all_gather(_pack([small_grads[n] for n, _, _, _ in SMALL_PARAMS] + [loss_part]), "gather_small_grads")
    summed = _unpack(_sum_parts(gathered, "sum_small_grads"), [(k, w) for _, k, w, _ in SMALL_PARAMS] + [(1, 128)])
    loss = summed[-1][0, 0]
    g_local = []
    for (n, k, w, sh), g in zip(SMALL_PARAMS, summed):
        g_local.append(lax.dynamic_slice_in_dim(g, me * (w // N_DEV), w // N_DEV, axis=1) if sh else g)
    names = [n for n, _, _, _ in SMALL_PARAMS]
    local_shapes = [(k, w // N_DEV if sh else w) for _, k, w, sh in SMALL_PARAMS]
    res = _adamw(_pack([mat(w_in[n]) for n in names]), _pack([mat(m_in[n]) for n in names]),
                 _pack([mat(v_in[n]) for n in names]), _pack(g_local)[None], "adamw_small", tr=8)
    unpacked = [_unpack(r, local_shapes) for r in res]
    for i, n in enumerate(names):
        outs[n] = tuple(u[i] for u in unpacked)

    ret = [loss, grad_x.reshape(x.shape)]
    for j in range(4):
        ret += [outs[n][j].reshape(w_in[n].shape) for n in WEIGHT_ORDER]
    return tuple(ret)
```

```python
import jax
import jax.numpy as jnp
from jax import lax
from jax.experimental import pallas as pl
from jax.experimental.pallas import tpu as pltpu

f32 = jnp.float32
bf16 = jnp.bfloat16

N_DEV = 8
D_MODEL = 1024
N_HEADS = 16
HEAD_DIM = 64
N_GROUPS = 4
HEADS_PER_GROUP = 4
D_STATE = 128
CHUNK = 128
SSD_CONV = 4
CONV_WIDTH = 31
D_CONV = 2048
EPS = 1e-6
XBC_W = 2048
B_OFF = 1024
C_OFF = 1536
F_LANE = 16
HALO = 32

ADAM_LR = 0.001
ADAM_B1 = 0.9
ADAM_B2 = 0.999
ADAM_EPS = 1e-08
ADAM_WD = 0.01
ADAM_STEP = 10

VMEM_LIMIT_BYTES = 56 * 1024 * 1024
ROW_TILE = 512
CONV_ROW_TILE = 256
CONV_COL_TILE = 512
CONV_SUB = 64
ATTN_TILE = 256

NT = (((1,), (1,)), ((), ()))
TN = (((0,), (0,)), ((), ()))
HIGHEST = lax.Precision.HIGHEST
NEG = -1e30


def _cp(*sem):
    return pltpu.CompilerParams(dimension_semantics=sem if sem else None, vmem_limit_bytes=VMEM_LIMIT_BYTES)


def _sigmoid(x):
    return jax.nn.sigmoid(x)


def _silu(x):
    return x * _sigmoid(x)


def _dsilu(x):
    s = _sigmoid(x)
    return s * (1.0 + x * (1.0 - s))


def _softplus(x):
    return jnp.maximum(x, 0.0) + jnp.log(1.0 + jnp.exp(-jnp.abs(x)))


def _log_sigmoid(x):
    return jnp.minimum(x, 0.0) - jnp.log(1.0 + jnp.exp(-jnp.abs(x)))


def _dot(a, b, dims=None, precision=None):
    if dims is None:
        return jnp.dot(a, b, preferred_element_type=f32, precision=precision)
    return lax.dot_general(a, b, dims, preferred_element_type=f32, precision=precision)


def _mm_nn(a, b, out_dtype, name, tm=512, tn=1024):
    m, k = a.shape
    n = b.shape[1]
    tm, tn = min(tm, m), min(tn, n)

    def body(a_ref, b_ref, o_ref):
        o_ref[...] = _dot(a_ref[...], b_ref[...]).astype(o_ref.dtype)

    return pl.pallas_call(
        body, name=name, grid=(n // tn, m // tm),
        in_specs=[pl.BlockSpec((tm, k), lambda j, i: (i, 0)), pl.BlockSpec((k, tn), lambda j, i: (0, j))],
        out_specs=pl.BlockSpec((tm, tn), lambda j, i: (i, j)),
        out_shape=jax.ShapeDtypeStruct((m, n), out_dtype), compiler_params=_cp("parallel", "parallel"))(a, b)


def _mm_nt(pairs, out_dtype, name, tm=512, tn=512):
    m = pairs[0][0].shape[0]
    n = pairs[0][2].shape[0]
    tm, tn = min(tm, m), min(tn, n)
    npair = len(pairs)

    def body(*refs):
        o_ref = refs[-1]
        acc = None
        for p in range(npair):
            d = _dot(refs[2 * p][...].astype(bf16), refs[2 * p + 1][...], NT)
            acc = d if acc is None else acc + d
        o_ref[...] = acc.astype(o_ref.dtype)

    in_specs, args = [], []
    for a, acb, b, bcb, k in pairs:
        in_specs.append(pl.BlockSpec((tm, k), lambda j, i, acb=acb: (i, acb)))
        in_specs.append(pl.BlockSpec((tn, k), lambda j, i, bcb=bcb: (j, bcb)))
        args += [a, b]
    return pl.pallas_call(
        body, name=name, grid=(n // tn, m // tm), in_specs=in_specs,
        out_specs=pl.BlockSpec((tm, tn), lambda j, i: (i, j)),
        out_shape=jax.ShapeDtypeStruct((m, n), out_dtype), compiler_params=_cp("parallel", "parallel"))(*args)


def _mm_tn(a, b, name, a_cb=0, am=None, b_cb=0, bn=None, tn=1024, tk=512, blocked=False):
    k = a.shape[0]
    am = a.shape[1] if am is None else am
    bn = b.shape[1] if bn is None else bn
    tm = min(1024, am)
    tn, tk = min(tn, bn), min(tk, k)
    a_off, b_off = a_cb * (am // tm), b_cb * (bn // tn)

    def body(a_ref, b_ref, o_ref):
        @pl.when(pl.program_id(2) == 0)
        def _():
            o_ref[...] = jnp.zeros_like(o_ref)
        d = _dot(a_ref[...].astype(bf16), b_ref[...].astype(bf16), TN)
        o_ref[...] += d.reshape(o_ref.shape)

    if blocked:
        out_spec = pl.BlockSpec((1, tm, tn), lambda i, j, kk: (j, i, 0))
        out_shape = jax.ShapeDtypeStruct((bn // tn, am, tn), f32)
    else:
        out_spec = pl.BlockSpec((tm, tn), lambda i, j, kk: (i, j))
        out_shape = jax.ShapeDtypeStruct((am, bn), f32)
    return pl.pallas_call(
        body, name=name, grid=(am // tm, bn // tn, k // tk),
        in_specs=[pl.BlockSpec((tk, tm), lambda i, j, kk: (kk, a_off + i)),
                  pl.BlockSpec((tk, tn), lambda i, j, kk: (kk, b_off + j))],
        out_specs=out_spec, out_shape=out_shape,
        compiler_params=_cp("parallel", "parallel", "arbitrary"))(a, b)


def _rowspec(ts, w, cb=0):
    return pl.BlockSpec((ts, w), lambda i: (i, cb))


def _vecspec(w):
    return pl.BlockSpec((1, w), lambda i: (0, 0))


def _rms_fwd(x, g, name):
    s, d = x.shape
    ts = min(ROW_TILE, s)

    def body(x_ref, g_ref, u_ref):
        xv = x_ref[...]
        r = lax.rsqrt(jnp.mean(xv * xv, axis=-1, keepdims=True) + EPS)
        u_ref[...] = (xv * r * g_ref[...]).astype(bf16)

    return pl.pallas_call(
        body, name=name, grid=(s // ts,), in_specs=[_rowspec(ts, d), _vecspec(d)], out_specs=_rowspec(ts, d),
        out_shape=jax.ShapeDtypeStruct((s, d), bf16), compiler_params=_cp("parallel"))(x, g)


def _rms_bwd_vals(xv, g, dy):
    r = lax.rsqrt(jnp.mean(xv * xv, axis=-1, keepdims=True) + EPS)
    xh = xv * r
    dg = jnp.sum(dy * xh, axis=0, keepdims=True)
    dxh = dy * g
    dx = r * (dxh - xh * jnp.mean(dxh * xh, axis=-1, keepdims=True))
    return dx, dg


def _gate0_fwd(y, z, o, ssd_norm):
    s = y.shape[0]
    ts = min(ROW_TILE, s)
    gw = D_MODEL // N_GROUPS

    def body(y_ref, zs_ref, zf_ref, o_ref, w_ref, cat_ref):
        yg = y_ref[...].astype(f32) * _silu(zs_ref[...].astype(f32))
        for g in range(N_GROUPS):
            seg = yg[:, gw * g:gw * (g + 1)]
            r = lax.rsqrt(jnp.mean(seg * seg, axis=-1, keepdims=True) + EPS)
            cat_ref[:, gw * g:gw * (g + 1)] = (seg * r * w_ref[:, gw * g:gw * (g + 1)]).astype(bf16)
        cat_ref[:, D_MODEL:] = (o_ref[...].astype(f32) * _silu(zf_ref[...].astype(f32))).astype(bf16)

    return pl.pallas_call(
        body, name="gate0_fwd", grid=(s // ts,),
        in_specs=[_rowspec(ts, D_MODEL), _rowspec(ts, D_MODEL, 0), _rowspec(ts, D_MODEL, 1), _rowspec(ts, D_MODEL),
                  _vecspec(D_MODEL)],
        out_specs=_rowspec(ts, 2 * D_MODEL),
        out_shape=jax.ShapeDtypeStruct((s, 2 * D_MODEL), bf16), compiler_params=_cp("parallel"))(y, z, z, o, ssd_norm)


def _post0_pre1(x, out0, g_post0, g_pre1):
    s, d = x.shape
    ts = min(ROW_TILE, s)

    def body(x_ref, o_ref, gp_ref, gn_ref, x1_ref, u1_ref):
        ov = o_ref[...]
        r = lax.rsqrt(jnp.mean(ov * ov, axis=-1, keepdims=True) + EPS)
        x1 = x_ref[...] + ov * r * gp_ref[...]
        x1_ref[...] = x1
        r1 = lax.rsqrt(jnp.mean(x1 * x1, axis=-1, keepdims=True) + EPS)
        u1_ref[...] = (x1 * r1 * gn_ref[...]).astype(bf16)

    return pl.pallas_call(
        body, name="post0_pre1", grid=(s // ts,),
        in_specs=[_rowspec(ts, d), _rowspec(ts, d), _vecspec(d), _vecspec(d)],
        out_specs=[_rowspec(ts, d), _rowspec(ts, d)],
        out_shape=[jax.ShapeDtypeStruct((s, d), f32), jax.ShapeDtypeStruct((s, d), bf16)],
        compiler_params=_cp("parallel"))(x, out0, g_post0, g_pre1)


def _ln_vals(hc, g, b):
    mu = jnp.mean(hc, axis=-1, keepdims=True)
    xc = hc - mu
    rstd = lax.rsqrt(jnp.mean(xc * xc, axis=-1, keepdims=True) + EPS)
    xh = xc * rstd
    return xh, rstd, xh * g + b


def _ln_gate_fwd(hc, proj1, ln_g, ln_b):
    s = hc.shape[0]
    ts = min(ROW_TILE, s)

    def body(hc_ref, z_ref, g_ref, b_ref, h3_ref):
        _, _, ln = _ln_vals(hc_ref[...].astype(f32), g_ref[...], b_ref[...])
        h3_ref[...] = (_silu(ln) * _silu(z_ref[...].astype(f32))).astype(bf16)

    return pl.pallas_call(
        body, name="ln_gate_fwd", grid=(s // ts,),
        in_specs=[_rowspec(ts, D_CONV), _rowspec(ts, D_CONV, 2), _vecspec(D_CONV), _vecspec(D_CONV)],
        out_specs=_rowspec(ts, D_CONV),
        out_shape=jax.ShapeDtypeStruct((s, D_CONV), bf16), compiler_params=_cp("parallel"))(hc, proj1, ln_g, ln_b)


def _final_loss(x1, out1, tgt, g_post1):
    s, d = x1.shape
    ts = min(ROW_TILE, s)

    def body(x1_ref, o_ref, t_ref, g_ref, dy_ref, do_ref, dg_ref, loss_ref):
        i = pl.program_id(0)

        @pl.when(i == 0)
        def _():
            dg_ref[...] = jnp.zeros_like(dg_ref)
            loss_ref[...] = jnp.zeros_like(loss_ref)
        ov = o_ref[...]
        g = g_ref[...]
        r = lax.rsqrt(jnp.mean(ov * ov, axis=-1, keepdims=True) + EPS)
        diff = x1_ref[...] + ov * r * g - t_ref[...]
        row = jnp.mean(diff * diff, axis=-1, keepdims=True)
        loss_ref[...] += jnp.broadcast_to(0.5 * jnp.sum(row, axis=0, keepdims=True), loss_ref.shape)
        dy = diff * (1.0 / d)
        dy_ref[...] = dy
        dx, dg = _rms_bwd_vals(ov, g, dy)
        do_ref[...] = dx.astype(bf16)
        dg_ref[...] += dg

    return pl.pallas_call(
        body, name="final_loss", grid=(s // ts,),
        in_specs=[_rowspec(ts, d), _rowspec(ts, d), _rowspec(ts, d), _vecspec(d)],
        out_specs=[_rowspec(ts, d), _rowspec(ts, d), _vecspec(d), _vecspec(128)],
        out_shape=[jax.ShapeDtypeStruct((s, d), f32), jax.ShapeDtypeStruct((s, d), bf16),
                   jax.ShapeDtypeStruct((1, d), f32), jax.ShapeDtypeStruct((1, 128), f32)],
        compiler_params=_cp("arbitrary"))(x1, out1, tgt, g_post1)


def _ln_gate_bwd(hc, proj1, dh3, ln_g, ln_b):
    s = hc.shape[0]
    ts = min(ROW_TILE, s)

    def body(hc_ref, z_ref, dh_ref, g_ref, b_ref, dhc_ref, dz_ref, dg_ref, db_ref):
        @pl.when(pl.program_id(0) == 0)
        def _():
            dg_ref[...] = jnp.zeros_like(dg_ref)
            db_ref[...] = jnp.zeros_like(db_ref)
        g = g_ref[...]
        xh, rstd, ln = _ln_vals(hc_ref[...].astype(f32), g, b_ref[...])
        zv = z_ref[...].astype(f32)
        dh3 = dh_ref[...].astype(f32)
        dz_ref[...] = (dh3 * _silu(ln) * _dsilu(zv)).astype(bf16)
        dln = dh3 * _silu(zv) * _dsilu(ln)
        dg_ref[...] += jnp.sum(dln * xh, axis=0, keepdims=True)
        db_ref[...] += jnp.sum(dln, axis=0, keepdims=True)
        dxh = dln * g
        dhc = rstd * (dxh - jnp.mean(dxh, axis=-1, keepdims=True) - xh * jnp.mean(dxh * xh, axis=-1, keepdims=True))
        dhc_ref[...] = dhc.astype(bf16)

    return pl.pallas_call(
        body, name="ln_gate_bwd", grid=(s // ts,),
        in_specs=[_rowspec(ts, D_CONV), _rowspec(ts, D_CONV, 2), _rowspec(ts, D_CONV), _vecspec(D_CONV),
                  _vecspec(D_CONV)],
        out_specs=[_rowspec(ts, D_CONV), _rowspec(ts, D_CONV), _vecspec(D_CONV), _vecspec(D_CONV)],
        out_shape=[jax.ShapeDtypeStruct((s, D_CONV), bf16), jax.ShapeDtypeStruct((s, D_CONV), bf16),
                   jax.ShapeDtypeStruct((1, D_CONV), f32), jax.ShapeDtypeStruct((1, D_CONV), f32)],
        compiler_params=_cp("arbitrary"))(hc, proj1, dh3, ln_g, ln_b)


def _mid_bwd(x1, du1, dy, out0, g_pre1, g_post0):
    s, d = x1.shape
    ts = min(ROW_TILE, s)

    def body(x1_ref, du_ref, dy_ref, o_ref, gn_ref, gp_ref, dx1_ref, do_ref, dgn_ref, dgp_ref):
        @pl.when(pl.program_id(0) == 0)
        def _():
            dgn_ref[...] = jnp.zeros_like(dgn_ref)
            dgp_ref[...] = jnp.zeros_like(dgp_ref)
        dxa, dgn = _rms_bwd_vals(x1_ref[...], gn_ref[...], du_ref[...])
        dx1 = dy_ref[...] + dxa
        dx1_ref[...] = dx1
        dgn_ref[...] += dgn
        dxo, dgp = _rms_bwd_vals(o_ref[...], gp_ref[...], dx1)
        do_ref[...] = dxo.astype(bf16)
        dgp_ref[...] += dgp

    return pl.pallas_call(
        body, name="mid_bwd", grid=(s // ts,),
        in_specs=[_rowspec(ts, d)] * 4 + [_vecspec(d), _vecspec(d)],
        out_specs=[_rowspec(ts, d), _rowspec(ts, d), _vecspec(d), _vecspec(d)],
        out_shape=[jax.ShapeDtypeStruct((s, d), f32), jax.ShapeDtypeStruct((s, d), bf16),
                   jax.ShapeDtypeStruct((1, d), f32), jax.ShapeDtypeStruct((1, d), f32)],
        compiler_params=_cp("arbitrary"))(x1, du1, dy, out0, g_pre1, g_post0)


def _first_bwd(x, du0, dx1, g_pre0):
    s, d = x.shape
    ts = min(ROW_TILE, s)

    def body(x_ref, du_ref, dx1_ref, g_ref, dx_ref, dg_ref):
        @pl.when(pl.program_id(0) == 0)
        def _():
            dg_ref[...] = jnp.zeros_like(dg_ref)
        dxa, dg = _rms_bwd_vals(x_ref[...], g_ref[...], du_ref[...])
        dx_ref[...] = dx1_ref[...] + dxa
        dg_ref[...] += dg

    return pl.pallas_call(
        body, name="first_bwd", grid=(s // ts,),
        in_specs=[_rowspec(ts, d)] * 3 + [_vecspec(d)],
        out_specs=[_rowspec(ts, d), _vecspec(d)],
        out_shape=[jax.ShapeDtypeStruct((s, d), f32), jax.ShapeDtypeStruct((1, d), f32)],
        compiler_params=_cp("arbitrary"))(x, du0, dx1, g_pre0)


def _gate0_bwd(y, z, o, dcat, ssd_norm):
    s = y.shape[0]
    ts = min(ROW_TILE, s)
    gw = D_MODEL // N_GROUPS

    def body(y_ref, zs_ref, zf_ref, o_ref, dn_ref, dg_ref, w_ref, dy_ref, do_ref, dz_ref, delta_ref, dw_ref):
        @pl.when(pl.program_id(0) == 0)
        def _():
            dw_ref[...] = jnp.zeros_like(dw_ref)
        yv = y_ref[...].astype(f32)
        zs = zs_ref[...].astype(f32)
        sz = _silu(zs)
        yg = yv * sz
        dyn = dn_ref[...].astype(f32)
        for g in range(N_GROUPS):
            sl = slice(gw * g, gw * (g + 1))
            seg = yg[:, sl]
            r = lax.rsqrt(jnp.mean(seg * seg, axis=-1, keepdims=True) + EPS)
            yh = seg * r
            dn = dyn[:, sl]
            dw_ref[:, sl] += jnp.sum(dn * yh, axis=0, keepdims=True)
            dyh = dn * w_ref[:, sl]
            dyg = r * (dyh - yh * jnp.mean(dyh * yh, axis=-1, keepdims=True))
            dy_ref[:, sl] = (dyg * sz[:, sl]).astype(bf16)
            dz_ref[:, sl] = (dyg * yv[:, sl] * _dsilu(zs[:, sl])).astype(bf16)
        zf = zf_ref[...].astype(f32)
        ov = o_ref[...].astype(f32)
        dog = dg_ref[...].astype(f32)
        dov = (dog * _silu(zf)).astype(bf16)
        do_ref[...] = dov
        dz_ref[:, D_MODEL:] = (dog * ov * _dsilu(zf)).astype(bf16)
        prod = dov.astype(f32) * ov
        lane = lax.broadcasted_iota(jnp.int32, (ts, 128), 1)
        delta = jnp.zeros((ts, 128), f32)
        for h in range(N_HEADS):
            dh = jnp.sum(prod[:, HEAD_DIM * h:HEAD_DIM * (h + 1)], axis=-1, keepdims=True)
            delta = delta + jnp.where(lane == h, dh, 0.0)
        delta_ref[...] = delta.T

    return pl.pallas_call(
        body, name="gate0_bwd", grid=(s // ts,),
        in_specs=[_rowspec(ts, D_MODEL), _rowspec(ts, D_MODEL, 0), _rowspec(ts, D_MODEL, 1), _rowspec(ts, D_MODEL),
                  _rowspec(ts, D_MODEL, 0), _rowspec(ts, D_MODEL, 1), _vecspec(D_MODEL)],
        out_specs=[_rowspec(ts, D_MODEL), _rowspec(ts, D_MODEL), _rowspec(ts, 2 * D_MODEL),
                   pl.BlockSpec((128, ts), lambda i: (0, i)), _vecspec(D_MODEL)],
        out_shape=[jax.ShapeDtypeStruct((s, D_MODEL), bf16), jax.ShapeDtypeStruct((s, D_MODEL), bf16),
                   jax.ShapeDtypeStruct((s, 2 * D_MODEL), bf16), jax.ShapeDtypeStruct((128, s), f32),
                   jax.ShapeDtypeStruct((1, D_MODEL), f32)],
        compiler_params=_cp("arbitrary"))(y, z, z, o, dcat, dcat, ssd_norm)


def _conv_grid(s, c):
    ts, cb = min(CONV_ROW_TILE, s), min(CONV_COL_TILE, c)
    return ts, cb, (c // cb, s // ts)


def _cur(ts, cb, off=0):
    return pl.BlockSpec((ts, cb), lambda c, i: (i, c + off))


def _prev_halo(ts, cb, off=0):
    return pl.BlockSpec((HALO, cb), lambda c, i: (jnp.maximum(i * (ts // HALO) - 1, 0), c + off))


def _next_halo(ts, cb, s, off=0):
    return pl.BlockSpec((HALO, cb), lambda c, i: (jnp.minimum((i + 1) * (ts // HALO), s // HALO - 1), c + off))


def _wspec(k, cb):
    return pl.BlockSpec((k, cb), lambda c, i: (0, c))


def _phases(offsets):
    return sorted({o % 8 for o in offsets} - {0})


def _shift_scratch(offsets, ts, cb):
    return pltpu.VMEM((max(len(_phases(offsets)), 1), ts + HALO - 8, cb), f32)


def _fill_phases(ext_ref, sh_ref, offsets, ts):
    for j, r in enumerate(_phases(offsets)):
        sh_ref[j] = ext_ref[pl.ds(r, ts + HALO - 8), :]


def _slab(ext_ref, sh_ref, offsets, off, start):
    r = off % 8
    a = off - r + start
    if r == 0:
        return ext_ref[a:a + CONV_SUB, :]
    return sh_ref[_phases(offsets).index(r), a:a + CONV_SUB, :]


def _conv_taps(ext_ref, sh_ref, w_ref, b_ref, ts, k_taps, emit):
    offsets = [HALO - (k_taps - 1) + k for k in range(k_taps)]
    _fill_phases(ext_ref, sh_ref, offsets, ts)
    for sb in range(ts // CONV_SUB):
        acc = b_ref[...]
        for k in range(k_taps):
            acc = acc + w_ref[k:k + 1, :] * _slab(ext_ref, sh_ref, offsets, offsets[k], sb * CONV_SUB)
        emit(slice(sb * CONV_SUB, (sb + 1) * CONV_SUB), acc)


def _conv_ssd_fwd(xraw, w, b):
    s, c = xraw.shape
    ts, cb, grid = _conv_grid(s, c)
    offsets = [HALO - (SSD_CONV - 1) + k for k in range(SSD_CONV)]

    def body(x_ref, xh_ref, w_ref, b_ref, pre_ref, act_ref, ext_ref, sh_ref):
        first = pl.program_id(1) == 0
        ext_ref[0:HALO, :] = jnp.where(first, 0.0, xh_ref[...].astype(f32))
        ext_ref[HALO:, :] = x_ref[...].astype(f32)

        def emit(rows, pre):
            pre_ref[rows, :] = pre.astype(bf16)
            act_ref[rows, :] = _silu(pre).astype(bf16)
        _conv_taps(ext_ref, sh_ref, w_ref, b_ref, ts, SSD_CONV, emit)

    return pl.pallas_call(
        body, name="conv_ssd_fwd", grid=grid,
        in_specs=[_cur(ts, cb), _prev_halo(ts, cb), _wspec(SSD_CONV, cb), _wspec(1, cb)],
        out_specs=[_cur(ts, cb), _cur(ts, cb)],
        out_shape=[jax.ShapeDtypeStruct((s, c), bf16)] * 2,
        scratch_shapes=[pltpu.VMEM((HALO + ts, cb), f32), _shift_scratch(offsets, ts, cb)],
        compiler_params=_cp("parallel", "parallel"))(xraw, xraw, w, b)


def _conv_glu_fwd(proj1, w, b):
    s = proj1.shape[0]
    c = D_CONV
    ts, cb, grid = _conv_grid(s, c)
    goff = c // cb

    offsets = [HALO - (CONV_WIDTH - 1) + k for k in range(CONV_WIDTH)]

    def body(v_ref, g_ref, vh_ref, gh_ref, w_ref, b_ref, hc_ref, ext_ref, sh_ref):
        first = pl.program_id(1) == 0
        hh = vh_ref[...].astype(f32) * _sigmoid(gh_ref[...].astype(f32))
        ext_ref[0:HALO, :] = jnp.where(first, 0.0, hh)
        ext_ref[HALO:, :] = v_ref[...].astype(f32) * _sigmoid(g_ref[...].astype(f32))

        def emit(rows, hc):
            hc_ref[rows, :] = hc.astype(bf16)
        _conv_taps(ext_ref, sh_ref, w_ref, b_ref, ts, CONV_WIDTH, emit)

    return pl.pallas_call(
        body, name="conv_glu_fwd", grid=grid,
        in_specs=[_cur(ts, cb), _cur(ts, cb, goff), _prev_halo(ts, cb), _prev_halo(ts, cb, goff),
                  _wspec(CONV_WIDTH, cb), _wspec(1, cb)],
        out_specs=_cur(ts, cb),
        out_shape=jax.ShapeDtypeStruct((s, c), bf16),
        scratch_shapes=[pltpu.VMEM((HALO + ts, cb), f32), _shift_scratch(offsets, ts, cb)],
        compiler_params=_cp("parallel", "parallel"))(proj1, proj1, proj1, proj1, w, b)


def _conv_bwd_offsets(k_taps):
    return [k_taps - 1 - k for k in range(k_taps)], [HALO - (k_taps - 1) + k for k in range(k_taps)]


def _conv_bwd_scratch(k_taps, ts, cb):
    d_offs, x_offs = _conv_bwd_offsets(k_taps)
    return [pltpu.VMEM((ts + HALO, cb), f32), _shift_scratch(d_offs, ts, cb),
            pltpu.VMEM((HALO + ts, cb), f32), _shift_scratch(x_offs, ts, cb),
            pltpu.VMEM((k_taps, 8, cb), f32), pltpu.VMEM((8, cb), f32)]


def _conv_bwd_core(dp, dpn_ref, last, w_ref, scratch, dw_ref, db_ref, ts, k_taps, emit):
    dext_ref, dsh_ref, xext_ref, xsh_ref, dw8_ref, db8_ref = scratch
    d_offs, x_offs = _conv_bwd_offsets(k_taps)
    dext_ref[0:ts, :] = dp
    dext_ref[ts:, :] = jnp.where(last, 0.0, dpn_ref[...].astype(f32))
    _fill_phases(dext_ref, dsh_ref, d_offs, ts)
    _fill_phases(xext_ref, xsh_ref, x_offs, ts)

    @pl.when(pl.program_id(1) == 0)
    def _():
        dw8_ref[...] = jnp.zeros_like(dw8_ref)
        db8_ref[...] = jnp.zeros_like(db8_ref)
    cb = dp.shape[1]
    for sb in range(ts // CONV_SUB):
        start = sb * CONV_SUB
        dpv = dext_ref[start:start + CONV_SUB, :]
        dx = None
        for k in range(k_taps):
            t = w_ref[k:k + 1, :] * _slab(dext_ref, dsh_ref, d_offs, d_offs[k], start)
            dx = t if dx is None else dx + t
            prod = dpv * _slab(xext_ref, xsh_ref, x_offs, x_offs[k], start)
            dw8_ref[k] += jnp.sum(prod.reshape(CONV_SUB // 8, 8, cb), axis=0)
        db8_ref[...] += jnp.sum(dpv.reshape(CONV_SUB // 8, 8, cb), axis=0)
        emit(slice(start, start + CONV_SUB), dx)

    @pl.when(last)
    def _():
        dw_ref[...] = jnp.sum(dw8_ref[...], axis=1)
        db_ref[...] = jnp.sum(db8_ref[...], axis=0, keepdims=True)


def _conv_ssd_bwd(dpre, xraw, w):
    s, c = xraw.shape
    ts, cb, grid = _conv_grid(s, c)
    nb = s // ts

    def body(dp_ref, dpn_ref, x_ref, xh_ref, w_ref, dx_ref, dw_ref, db_ref, *scratch):
        i = pl.program_id(1)
        xext_ref = scratch[2]
        xext_ref[0:HALO, :] = jnp.where(i == 0, 0.0, xh_ref[...].astype(f32))
        xext_ref[HALO:, :] = x_ref[...].astype(f32)

        def emit(rows, dx):
            dx_ref[rows, :] = dx.astype(bf16)
        _conv_bwd_core(dp_ref[...].astype(f32), dpn_ref, i == nb - 1, w_ref, scratch, dw_ref, db_ref, ts, SSD_CONV, emit)

    return pl.pallas_call(
        body, name="conv_ssd_bwd", grid=grid,
        in_specs=[_cur(ts, cb), _next_halo(ts, cb, s), _cur(ts, cb), _prev_halo(ts, cb), _wspec(SSD_CONV, cb)],
        out_specs=[_cur(ts, cb), _wspec(SSD_CONV, cb), _wspec(1, cb)],
        out_shape=[jax.ShapeDtypeStruct((s, c), bf16), jax.ShapeDtypeStruct((SSD_CONV, c), f32),
                   jax.ShapeDtypeStruct((1, c), f32)],
        scratch_shapes=_conv_bwd_scratch(SSD_CONV, ts, cb),
        compiler_params=_cp("parallel", "arbitrary"))(dpre, dpre, xraw, xraw, w)


def _conv_glu_bwd(dhc, proj1, w):
    s = proj1.shape[0]
    c = D_CONV
    ts, cb, grid = _conv_grid(s, c)
    nb = s // ts
    goff = c // cb

    def body(dp_ref, dpn_ref, v_ref, g_ref, vh_ref, gh_ref, w_ref, dv_ref, dg_ref, dw_ref, db_ref, *scratch):
        i = pl.program_id(1)
        xext_ref = scratch[2]
        xext_ref[0:HALO, :] = jnp.where(i == 0, 0.0, vh_ref[...].astype(f32) * _sigmoid(gh_ref[...].astype(f32)))
        xext_ref[HALO:, :] = v_ref[...].astype(f32) * _sigmoid(g_ref[...].astype(f32))

        def emit(rows, dh):
            val = v_ref[rows, :].astype(f32)
            sg = _sigmoid(g_ref[rows, :].astype(f32))
            dv_ref[rows, :] = (dh * sg).astype(bf16)
            dg_ref[rows, :] = (dh * val * sg * (1.0 - sg)).astype(bf16)
        _conv_bwd_core(dp_ref[...].astype(f32), dpn_ref, i == nb - 1, w_ref, scratch, dw_ref, db_ref, ts, CONV_WIDTH, emit)

    return pl.pallas_call(
        body, name="conv_glu_bwd", grid=grid,
        in_specs=[_cur(ts, cb), _next_halo(ts, cb, s), _cur(ts, cb), _cur(ts, cb, goff), _prev_halo(ts, cb),
                  _prev_halo(ts, cb, goff), _wspec(CONV_WIDTH, cb)],
        out_specs=[_cur(ts, cb), _cur(ts, cb), _wspec(CONV_WIDTH, cb), _wspec(1, cb)],
        out_shape=[jax.ShapeDtypeStruct((s, c), bf16), jax.ShapeDtypeStruct((s, c), bf16),
                   jax.ShapeDtypeStruct((CONV_WIDTH, c), f32), jax.ShapeDtypeStruct((1, c), f32)],
        scratch_shapes=_conv_bwd_scratch(CONV_WIDTH, ts, cb),
        compiler_params=_cp("parallel", "arbitrary"))(dhc, dhc, proj1, proj1, proj1, proj1, w)


def _ssd_common(dt_ref, prm_ref):
    l = CHUNK
    dtb = prm_ref[0:1, :]
    a = -jnp.exp(prm_ref[1:2, :])
    dsk = prm_ref[2:3, :]
    zraw = dt_ref[...] + dtb
    dt = _softplus(zraw)
    da = dt * a
    row = lax.broadcasted_iota(jnp.int32, (l, l), 0)
    col = lax.broadcasted_iota(jnp.int32, (l, l), 1)
    causal = row >= col
    cs = _dot(causal.astype(f32), da, precision=HIGHEST)
    return a, dsk, zraw, dt, cs, cs.T, causal, row, col


def _ssd_fwd(act, dtf, prm):
    s = act.shape[0]
    nc = s // CHUNK
    l = CHUNK

    def body(xs_ref, dt_ref, prm_ref, y_ref, hs_ref, st_ref):
        @pl.when(pl.program_id(0) == 0)
        def _():
            st_ref[...] = jnp.zeros_like(st_ref)
        a, dsk, _, dt, cs, cst, causal, _, _ = _ssd_common(dt_ref, prm_ref)
        for g in range(N_GROUPS):
            bm = xs_ref[:, B_OFF + D_STATE * g:B_OFF + D_STATE * (g + 1)]
            cm = xs_ref[:, C_OFF + D_STATE * g:C_OFF + D_STATE * (g + 1)]
            gmat = _dot(cm, bm, NT)
            for r in range(HEADS_PER_GROUP):
                h = HEADS_PER_GROUP * g + r
                hsl = slice(HEAD_DIM * h, HEAD_DIM * (h + 1))
                xv = xs_ref[:, hsl].astype(f32)
                csc = cs[:, h:h + 1]
                csr = cst[h:h + 1, :]
                cl = cs[l - 1:l, h:h + 1]
                dk = jnp.exp(jnp.where(causal, csc - csr, NEG))
                xd = xv * dt[:, h:h + 1]
                hp = st_ref[h]
                hs_ref[0, h] = hp
                ydiag = _dot((gmat * dk).astype(bf16), xd.astype(bf16))
                yoff = _dot(cm, hp.astype(bf16), NT) * jnp.exp(csc)
                y_ref[:, hsl] = (ydiag + yoff + xv * dsk[:, h:h + 1]).astype(bf16)
                st = _dot((xd * jnp.exp(cl - csc)).astype(bf16), bm, TN)
                st_ref[h] = hp * jnp.exp(cl) + st

    return pl.pallas_call(
        body, name="ssd_fwd", grid=(nc,),
        in_specs=[pl.BlockSpec((l, XBC_W), lambda i: (i, 0)), pl.BlockSpec((l, 128), lambda i: (i, 0)),
                  pl.BlockSpec((8, 128), lambda i: (0, 0))],
        out_specs=[pl.BlockSpec((l, D_MODEL), lambda i: (i, 0)),
                   pl.BlockSpec((1, N_HEADS, HEAD_DIM, D_STATE), lambda i: (i, 0, 0, 0))],
        out_shape=[jax.ShapeDtypeStruct((s, D_MODEL), bf16),
                   jax.ShapeDtypeStruct((nc, N_HEADS, HEAD_DIM, D_STATE), f32)],
        scratch_shapes=[pltpu.VMEM((N_HEADS, HEAD_DIM, D_STATE), f32)],
        compiler_params=_cp("arbitrary"))(act, dtf, prm)


def _ssd_bwd(act, pre, dtf, prm, hs, dy):
    s = act.shape[0]
    nc = s // CHUNK
    l = CHUNK

    def body(xs_ref, pre_ref, dt_ref, prm_ref, hs_ref, dy_ref, dpre_ref, ddt_ref, dprm_ref, dh_ref):
        @pl.when(pl.program_id(0) == 0)
        def _():
            dh_ref[...] = jnp.zeros_like(dh_ref)
            dprm_ref[...] = jnp.zeros_like(dprm_ref)
        a, dsk, zraw, dt, cs, cst, causal, row, col = _ssd_common(dt_ref, prm_ref)
        lane = lax.broadcasted_iota(jnp.int32, (l, 128), 1)
        rowl = lax.broadcasted_iota(jnp.int32, (l, 128), 0)
        sub = lax.broadcasted_iota(jnp.int32, (128, l), 0)
        lane1 = lax.broadcasted_iota(jnp.int32, (1, 128), 1)
        dcs_c = jnp.zeros((l, 128), f32)
        dcs_r = jnp.zeros((128, l), f32)
        ddt_c = jnp.zeros((l, 128), f32)
        dd_row = jnp.zeros((1, 128), f32)
        for g in range(N_GROUPS):
            bsl = slice(B_OFF + D_STATE * g, B_OFF + D_STATE * (g + 1))
            csl = slice(C_OFF + D_STATE * g, C_OFF + D_STATE * (g + 1))
            bm = xs_ref[:, bsl]
            cm = xs_ref[:, csl]
            gmat = _dot(cm, bm, NT)
            dgm = jnp.zeros((l, l), f32)
            dbg = jnp.zeros((l, D_STATE), f32)
            dcg = jnp.zeros((l, D_STATE), f32)
            for r in range(HEADS_PER_GROUP):
                h = HEADS_PER_GROUP * g + r
                hsl = slice(HEAD_DIM * h, HEAD_DIM * (h + 1))
                xv = xs_ref[:, hsl].astype(f32)
                dyv = dy_ref[:, hsl].astype(f32)
                dyb = dyv.astype(bf16)
                csc = cs[:, h:h + 1]
                csr = cst[h:h + 1, :]
                cl = cs[l - 1:l, h:h + 1]
                dk = jnp.exp(jnp.where(causal, csc - csr, NEG))
                mf = gmat * dk
                dtc = dt[:, h:h + 1]
                xd = xv * dtc
                xdb = xd.astype(bf16)
                ecs = jnp.exp(csc)
                dec = jnp.exp(cl)
                e = jnp.exp(cl - csc)
                hp = hs_ref[0, h]
                hpb = hp.astype(bf16)
                dhn = dh_ref[h]
                dhnb = dhn.astype(bf16)
                dd_h = jnp.sum(jnp.sum(dyv * xv, axis=1, keepdims=True), axis=0, keepdims=True)
                dx = dyv * dsk[:, h:h + 1]
                ch = _dot(cm, hpb, NT)
                dye = dyv * ecs
                dyeb = dye.astype(bf16)
                dcg = dcg + _dot(dyeb, hpb)
                dhp = _dot(dyeb, cm, TN)
                dcs_col = jnp.sum(dye * ch, axis=1, keepdims=True)
                dm = _dot(dyb, xdb, NT)
                dxd = _dot(mf.astype(bf16), dyb, TN)
                dgm = dgm + dm * dk
                wmat = dm * mf
                dcs_col = dcs_col + jnp.sum(wmat, axis=1, keepdims=True)
                dcs_row = -jnp.sum(wmat, axis=0, keepdims=True)
                ddec = jnp.sum(jnp.sum(hp * dhn, axis=1, keepdims=True), axis=0, keepdims=True)
                dxe = _dot(bm, dhnb, NT)
                dxd = dxd + dxe * e
                de_e = jnp.sum(dxe * xd, axis=1, keepdims=True) * e
                dbg = dbg + _dot((xd * e).astype(bf16), dhnb)
                dcs_col = dcs_col - de_e
                dlast = ddec * dec + jnp.sum(de_e, axis=0, keepdims=True)
                dh_ref[h] = dhp + dec * dhn
                dx = dx + dxd * dtc
                ddt_h = jnp.sum(dxd * xv, axis=1, keepdims=True)
                is_h = lane == h
                dcs_c = dcs_c + jnp.where(is_h, dcs_col, 0.0) + jnp.where(is_h & (rowl == l - 1), dlast, 0.0)
                dcs_r = dcs_r + jnp.where(sub == h, dcs_row, 0.0)
                ddt_c = ddt_c + jnp.where(is_h, ddt_h, 0.0)
                dd_row = dd_row + jnp.where(lane1 == h, dd_h, 0.0)
                dpre_ref[:, hsl] = (dx * _dsilu(pre_ref[:, hsl].astype(f32))).astype(bf16)
            dgb = dgm.astype(bf16)
            dcg = dcg + _dot(dgb, bm)
            dbg = dbg + _dot(dgb, cm, TN)
            dpre_ref[:, bsl] = (dbg * _dsilu(pre_ref[:, bsl].astype(f32))).astype(bf16)
            dpre_ref[:, csl] = (dcg * _dsilu(pre_ref[:, csl].astype(f32))).astype(bf16)
        dcs = dcs_c + dcs_r.T
        dda = _dot((row <= col).astype(f32), dcs, precision=HIGHEST)
        ddt = ddt_c + dda * a
        ddtraw = jnp.where(lane < N_HEADS, ddt * _sigmoid(zraw), 0.0)
        ddt_ref[...] = ddtraw
        dprm_ref[0:1, :] += jnp.sum(ddtraw, axis=0, keepdims=True)
        dprm_ref[1:2, :] += jnp.where(lane1 < N_HEADS, jnp.sum(dda * dt, axis=0, keepdims=True) * a, 0.0)
        dprm_ref[2:3, :] += dd_row

    def rev(i):
        return (nc - 1 - i, 0)

    return pl.pallas_call(
        body, name="ssd_bwd", grid=(nc,),
        in_specs=[pl.BlockSpec((l, XBC_W), rev), pl.BlockSpec((l, XBC_W), rev),
                  pl.BlockSpec((l, 128), rev), pl.BlockSpec((8, 128), lambda i: (0, 0)),
                  pl.BlockSpec((1, N_HEADS, HEAD_DIM, D_STATE), lambda i: (nc - 1 - i, 0, 0, 0)),
                  pl.BlockSpec((l, D_MODEL), rev)],
        out_specs=[pl.BlockSpec((l, XBC_W), rev), pl.BlockSpec((l, 128), rev), pl.BlockSpec((8, 128), lambda i: (0, 0))],
        out_shape=[jax.ShapeDtypeStruct((s, XBC_W), bf16), jax.ShapeDtypeStruct((s, 128), f32),
                   jax.ShapeDtypeStruct((8, 128), f32)],
        scratch_shapes=[pltpu.VMEM((N_HEADS, HEAD_DIM, D_STATE), f32)],
        compiler_params=_cp("arbitrary"))(act, pre, dtf, prm, hs, dy)


def _fox_cumsum(dtf, prm):
    s = dtf.shape[0]
    l = CHUNK

    def body(f_ref, prm_ref, c_ref, carry_ref):
        @pl.when(pl.program_id(0) == 0)
        def _():
            carry_ref[...] = jnp.zeros_like(carry_ref)
        lf = _log_sigmoid(f_ref[...] + prm_ref[3:4, :])
        row = lax.broadcasted_iota(jnp.int32, (l, l), 0)
        col = lax.broadcasted_iota(jnp.int32, (l, l), 1)
        c = _dot((row >= col).astype(f32), lf, precision=HIGHEST) + carry_ref[...]
        c_ref[...] = c
        carry_ref[...] = c[l - 1:l, :]

    return pl.pallas_call(
        body, name="fox_cumsum", grid=(s // l,),
        in_specs=[pl.BlockSpec((l, 128), lambda i: (i, 0)), pl.BlockSpec((8, 128), lambda i: (0, 0))],
        out_specs=pl.BlockSpec((l, 128), lambda i: (i, 0)),
        out_shape=jax.ShapeDtypeStruct((s, 128), f32),
        scratch_shapes=[pltpu.VMEM((1, 128), f32)],
        compiler_params=_cp("arbitrary"))(dtf, prm)


AUG = HEAD_DIM
N_PAIRS = N_HEADS // 2
V_BLOCK = 2 * D_MODEL // 128


def _split3(x):
    hi = x.astype(bf16)
    r1 = x - hi.astype(f32)
    mid = r1.astype(bf16)
    lo = (r1 - mid.astype(f32)).astype(bf16)
    return hi.astype(f32), mid.astype(f32), lo.astype(f32)


def _fox_prep(qkv, c):
    s = qkv.shape[0]
    ts = min(CONV_ROW_TILE, s)
    kb = D_MODEL // 128

    def body(q_ref, k_ref, c_ref, qa_ref, ka_ref):
        lane = lax.broadcasted_iota(jnp.int32, (ts, 128), 1)
        low = lane < HEAD_DIM
        for h in range(N_HEADS):
            psl = slice(128 * (h // 2), 128 * (h // 2 + 1))
            qv = q_ref[:, psl].astype(f32) * (HEAD_DIM ** -0.5)
            kv = k_ref[:, psl].astype(f32)
            if h % 2:
                qv = pltpu.roll(qv, HEAD_DIM, 1)
                kv = pltpu.roll(kv, HEAD_DIM, 1)
            hi, mid, lo = _split3(c_ref[:, F_LANE + h:F_LANE + h + 1])
            ones = jnp.where((lane >= AUG + 3) & (lane < AUG + 6), 1.0, 0.0)
            cq = jnp.where(lane == AUG, hi, jnp.where(lane == AUG + 1, mid, jnp.where(lane == AUG + 2, lo, ones)))
            qa_ref[h] = jnp.where(low, qv, cq).astype(bf16)
            onek = jnp.where((lane >= AUG) & (lane < AUG + 3), 1.0, 0.0)
            ck = jnp.where(lane == AUG + 3, -hi, jnp.where(lane == AUG + 4, -mid, jnp.where(lane == AUG + 5, -lo, onek)))
            ka_ref[h] = jnp.where(low, kv, ck).astype(bf16)

    hm = pl.BlockSpec((N_HEADS, ts, 128), lambda i: (0, i, 0))
    return pl.pallas_call(
        body, name="fox_prep", grid=(s // ts,),
        in_specs=[_rowspec(ts, D_MODEL, 0), _rowspec(ts, D_MODEL, 1), _rowspec(ts, 128)],
        out_specs=[hm, hm], out_shape=[jax.ShapeDtypeStruct((N_HEADS, s, 128), bf16)] * 2,
        compiler_params=_cp("parallel"))(qkv, qkv, c)


def _fox_fwd(qa, ka, qkv):
    s = qkv.shape[0]
    t = min(ATTN_TILE, s)
    nq = s // t

    def body(qa_ref, ka_ref, v_ref, o_ref, lse_ref):
        qi = pl.program_id(1)
        low = lax.broadcasted_iota(jnp.int32, (t, 128), 1) < HEAD_DIM
        row = lax.broadcasted_iota(jnp.int32, (t, t), 0)
        col = lax.broadcasted_iota(jnp.int32, (t, t), 1)

        def tile(ki, carry, diagonal):
            stats, acc = carry
            koff = pl.multiple_of(ki * t, t)
            v = v_ref[pl.ds(koff, t), :]
            vh = (jnp.where(low, v, jnp.zeros_like(v)), jnp.where(low, jnp.zeros_like(v), v))
            new_stats, alphas, pv = [], [], None
            for r in range(2):
                m_old, l_old = stats[r]
                sc = _dot(qa_ref[r], ka_ref[r, pl.ds(koff, t), :], NT)
                if diagonal:
                    sc = jnp.where(col <= row, sc, NEG)
                m_new = jnp.maximum(m_old, jnp.max(sc, axis=1, keepdims=True))
                p = jnp.exp(sc - m_new)
                alpha = jnp.exp(m_old - m_new)
                new_stats.append((m_new, alpha * l_old + jnp.sum(p, axis=1, keepdims=True)))
                alphas.append(alpha)
                d = _dot(p.astype(bf16), vh[r])
                pv = d if pv is None else pv + d
            acc = acc * jnp.where(low, alphas[0], alphas[1]) + pv
            return tuple(new_stats), acc

        init = (((jnp.full((t, 1), NEG, f32), jnp.zeros((t, 1), f32)),) * 2, jnp.zeros((t, 128), f32))
        carry = lax.fori_loop(0, qi, lambda ki, cr: tile(ki, cr, False), init)
        stats, acc = tile(qi, carry, True)
        o_ref[...] = (acc / jnp.where(low, stats[0][1], stats[1][1])).astype(bf16)
        for r in range(2):
            lse = stats[r][0] + jnp.log(stats[r][1])
            lse_ref[r] = jnp.broadcast_to(lse, (t, 128)).T[0:1, :]

    return pl.pallas_call(
        body, name="fox_fwd", grid=(N_PAIRS, nq),
        in_specs=[pl.BlockSpec((2, t, 128), lambda j, qi: (j, qi, 0)),
                  pl.BlockSpec((2, s, 128), lambda j, qi: (j, 0, 0)),
                  pl.BlockSpec((s, 128), lambda j, qi: (0, V_BLOCK + j))],
        out_specs=[pl.BlockSpec((t, 128), lambda j, qi: (qi, j)), pl.BlockSpec((2, 1, t), lambda j, qi: (j, 0, qi))],
        out_shape=[jax.ShapeDtypeStruct((s, D_MODEL), bf16), jax.ShapeDtypeStruct((N_HEADS, 1, s), f32)],
        compiler_params=_cp("parallel", "parallel"))(qa, ka, qkv)


def _fox_bwd(qa, ka, qkv, do, lse, delta):
    s = qkv.shape[0]
    t = min(ATTN_TILE, s)
    nq = s // t

    def body(qa_ref, ka_ref, v_ref, do_ref, lse_ref, dl_ref, dq_ref, dk_ref, dv_ref):
        ki = pl.program_id(1)

        @pl.when(ki == 0)
        def _():
            dq_ref[...] = jnp.zeros_like(dq_ref)
        low = lax.broadcasted_iota(jnp.int32, (t, 128), 1) < HEAD_DIM
        row = lax.broadcasted_iota(jnp.int32, (t, t), 0)
        col = lax.broadcasted_iota(jnp.int32, (t, t), 1)
        v = v_ref[...]
        zero = jnp.zeros_like(v)
        vh = (jnp.where(low, v, zero), jnp.where(low, zero, v))

        def tile(qi, carry, diagonal):
            dks, dv = carry
            qoff = pl.multiple_of(qi * t, t)
            dov = do_ref[pl.ds(qoff, t), :]
            doh = (jnp.where(low, dov, zero), jnp.where(low, zero, dov))
            new_dks = []
            for r in range(2):
                qt = qa_ref[r, pl.ds(qoff, t), :]
                sct = _dot(ka_ref[r], qt, NT)
                if diagonal:
                    sct = jnp.where(row <= col, sct, NEG)
                pt = jnp.exp(sct - lse_ref[r, :, pl.ds(qoff, t)])
                dpt = _dot(vh[r], dov, NT)
                dst = (pt * (dpt - dl_ref[r, :, pl.ds(qoff, t)])).astype(bf16)
                dv = dv + _dot(pt.astype(bf16), doh[r])
                new_dks.append(dks[r] + _dot(dst, qt))
                dq_ref[r, pl.ds(qoff, t), :] += _dot(dst, ka_ref[r], TN)
            return tuple(new_dks), dv

        zacc = jnp.zeros((t, 128), f32)
        carry = tile(ki, ((zacc, zacc), zacc), True)
        dks, dv = lax.fori_loop(ki + 1, nq, lambda qi, cr: tile(qi, cr, False), carry)
        dk_ref[0] = dks[0]
        dk_ref[1] = dks[1]
        dv_ref[...] = dv.astype(bf16)

    return pl.pallas_call(
        body, name="fox_bwd", grid=(N_PAIRS, nq),
        in_specs=[pl.BlockSpec((2, s, 128), lambda j, ki: (j, 0, 0)),
                  pl.BlockSpec((2, t, 128), lambda j, ki: (j, ki, 0)),
                  pl.BlockSpec((t, 128), lambda j, ki: (ki, V_BLOCK + j)),
                  pl.BlockSpec((s, 128), lambda j, ki: (0, j)),
                  pl.BlockSpec((2, 1, s), lambda j, ki: (j, 0, 0)),
                  pl.BlockSpec((2, 1, s), lambda j, ki: (j, 0, 0))],
        out_specs=[pl.BlockSpec((2, s, 128), lambda j, ki: (j, 0, 0)),
                   pl.BlockSpec((2, t, 128), lambda j, ki: (j, ki, 0)),
                   pl.BlockSpec((t, 128), lambda j, ki: (ki, j))],
        out_shape=[jax.ShapeDtypeStruct((N_HEADS, s, 128), f32), jax.ShapeDtypeStruct((N_HEADS, s, 128), f32),
                   jax.ShapeDtypeStruct((s, D_MODEL), bf16)],
        compiler_params=_cp("parallel", "arbitrary"))(qa, ka, qkv, do, lse, delta)


def _fox_bwd_post(dq_hm, dk_hm):
    s = dq_hm.shape[1]
    ts = min(CONV_ROW_TILE, s)

    def body(dq_ref, dk_ref, q_ref, k_ref, dc_ref):
        lane = lax.broadcasted_iota(jnp.int32, (ts, 128), 1)
        dc = jnp.zeros((ts, 128), f32)
        for h in range(N_HEADS):
            hsl = slice(HEAD_DIM * h, HEAD_DIM * (h + 1))
            dqv = dq_ref[h]
            dkv = dk_ref[h]
            q_ref[:, hsl] = (dqv[:, 0:HEAD_DIM] * (HEAD_DIM ** -0.5)).astype(bf16)
            k_ref[:, hsl] = dkv[:, 0:HEAD_DIM].astype(bf16)
            dc = dc + jnp.where(lane == F_LANE + h, dqv[:, AUG:AUG + 1] - dkv[:, AUG + 3:AUG + 4], 0.0)
        dc_ref[...] = dc

    hm = pl.BlockSpec((N_HEADS, ts, 128), lambda i: (0, i, 0))
    return pl.pallas_call(
        body, name="fox_bwd_post", grid=(s // ts,), in_specs=[hm, hm],
        out_specs=[_rowspec(ts, D_MODEL), _rowspec(ts, D_MODEL), _rowspec(ts, 128)],
        out_shape=[jax.ShapeDtypeStruct((s, D_MODEL), bf16), jax.ShapeDtypeStruct((s, D_MODEL), bf16),
                   jax.ShapeDtypeStruct((s, 128), f32)],
        compiler_params=_cp("parallel"))(dq_hm, dk_hm)


def _fox_gate_bwd(dc, dtf, prm, ddt_raw):
    s = dtf.shape[0]
    l = CHUNK
    nb = s // l

    def body(dc_ref, f_ref, prm_ref, ddt_ref, out_ref, dfb_ref, carry_ref):
        @pl.when(pl.program_id(0) == 0)
        def _():
            carry_ref[...] = jnp.zeros_like(carry_ref)
            dfb_ref[...] = jnp.zeros_like(dfb_ref)
        dc = dc_ref[...]
        row = lax.broadcasted_iota(jnp.int32, (l, l), 0)
        col = lax.broadcasted_iota(jnp.int32, (l, l), 1)
        dlf = _dot((row <= col).astype(f32), dc, precision=HIGHEST) + carry_ref[...]
        carry_ref[...] = dlf[0:1, :]
        lane = lax.broadcasted_iota(jnp.int32, (l, 128), 1)
        is_f = (lane >= F_LANE) & (lane < F_LANE + N_HEADS)
        dfr = jnp.where(is_f, dlf * _sigmoid(-(f_ref[...] + prm_ref[3:4, :])), 0.0)
        dfb_ref[...] += jnp.sum(dfr, axis=0, keepdims=True)
        out_ref[...] = ddt_ref[...] + dfr

    def rev(i):
        return (nb - 1 - i, 0)

    return pl.pallas_call(
        body, name="fox_gate_bwd", grid=(nb,),
        in_specs=[pl.BlockSpec((l, 128), rev), pl.BlockSpec((l, 128), rev), pl.BlockSpec((8, 128), lambda i: (0, 0)),
                  pl.BlockSpec((l, 128), rev)],
        out_specs=[pl.BlockSpec((l, 128), rev), pl.BlockSpec((1, 128), lambda i: (0, 0))],
        out_shape=[jax.ShapeDtypeStruct((s, 128), f32), jax.ShapeDtypeStruct((1, 128), f32)],
        scratch_shapes=[pltpu.VMEM((1, 128), f32)],
        compiler_params=_cp("arbitrary"))(dc, dtf, prm, ddt_raw)


def _position():
    return lax.axis_index("x"), lax.axis_index("y"), lax.axis_index("c")


def _all_gather(xl, name):
    r, c = xl.shape

    def body(x_ref, out_ref, send_sems, recv_sems, local_sem):
        x, y, cc = _position()
        me, sibling = (x, y, cc), (x, y, 1 - cc)
        chips = [(1 - x, y), (x, 1 - y), (1 - x, 1 - y)]

        def slot(px, py, pc):
            return out_ref.at[4 * px + 2 * py + pc]

        def copy(k, block, to, src=None):
            return pltpu.make_async_remote_copy(
                src_ref=slot(*block) if src is None else src, dst_ref=slot(*block),
                send_sem=send_sems.at[k], recv_sem=recv_sems.at[k],
                device_id=to, device_id_type=pl.DeviceIdType.MESH)

        mine = pltpu.make_async_copy(x_ref, slot(*me), local_sem)
        mine.start()
        first = [copy(0, me, sibling, src=x_ref)]
        first += [copy(1 + j, me, (*chip, cc), src=x_ref) for j, chip in enumerate(chips)]
        for cp in first:
            cp.start()
        passed = [copy(4 + j, (*chip, cc), sibling) for j, chip in enumerate(chips)]
        for j, chip in enumerate(chips):
            copy(1 + j, (*chip, cc), me).wait_recv()
            passed[j].start()
        copy(0, sibling, me).wait_recv()
        for j, chip in enumerate(chips):
            copy(4 + j, (*chip, 1 - cc), me).wait_recv()
        for cp in first + passed:
            cp.wait_send()
        mine.wait()

    return pl.pallas_call(
        body, name=name,
        out_shape=jax.ShapeDtypeStruct((N_DEV, r, c), xl.dtype),
        in_specs=[pl.BlockSpec(memory_space=pl.ANY)], out_specs=pl.BlockSpec(memory_space=pl.ANY),
        scratch_shapes=[pltpu.SemaphoreType.DMA((7,)), pltpu.SemaphoreType.DMA((7,)), pltpu.SemaphoreType.DMA],
    )(xl)


def _grad_exchange(gs):
    n = len(gs)

    def body(*refs):
        g_refs, r_refs = refs[:n], refs[n:2 * n]
        send_sems, recv_sems, local_sems = refs[2 * n:]
        x, y, cc = _position()
        me = 4 * x + 2 * y + cc
        local = [pltpu.make_async_copy(g_refs[a].at[me], r_refs[a].at[me], local_sems.at[a]) for a in range(n)]
        for cp in local:
            cp.start()
        sends, recvs = [], []
        for k in range(1, N_DEV):
            px = 1 - x if k & 4 else x
            py = 1 - y if k & 2 else y
            pc = 1 - cc if k & 1 else cc
            pid = 4 * px + 2 * py + pc
            for a in range(n):
                sends.append(pltpu.make_async_remote_copy(
                    src_ref=g_refs[a].at[pid], dst_ref=r_refs[a].at[me],
                    send_sem=send_sems.at[a, k - 1], recv_sem=recv_sems.at[a, k - 1],
                    device_id=(px, py, pc), device_id_type=pl.DeviceIdType.MESH))
                recvs.append(pltpu.make_async_remote_copy(
                    src_ref=g_refs[a].at[pid], dst_ref=r_refs[a].at[pid],
                    send_sem=send_sems.at[a, k - 1], recv_sem=recv_sems.at[a, k - 1],
                    device_id=(px, py, pc), device_id_type=pl.DeviceIdType.MESH))
        for cp in sends:
            cp.start()
        for cp in recvs:
            cp.wait_recv()
        for cp in sends:
            cp.wait_send()
        for cp in local:
            cp.wait()

    anyspec = pl.BlockSpec(memory_space=pl.ANY)
    return pl.pallas_call(
        body, name="grad_exchange",
        out_shape=[jax.ShapeDtypeStruct(g.shape, g.dtype) for g in gs],
        in_specs=[anyspec] * n, out_specs=[anyspec] * n,
        scratch_shapes=[pltpu.SemaphoreType.DMA((n, N_DEV - 1)), pltpu.SemaphoreType.DMA((n, N_DEV - 1)),
                        pltpu.SemaphoreType.DMA((n,))],
    )(*gs)


def _sum_parts(parts, name):
    n, r, c = parts.shape

    def body(p_ref, o_ref):
        g = p_ref[0]
        for i in range(1, n):
            g = g + p_ref[i]
        o_ref[...] = g

    return pl.pallas_call(body, name=name, out_shape=jax.ShapeDtypeStruct((r, c), f32))(parts)


def _adamw(w, m, v, parts, name, tr=128):
    r, c = w.shape
    n = parts.shape[0]
    tr = min(tr, r)
    c1 = 1.0 - ADAM_B1 ** ADAM_STEP
    c2 = 1.0 - ADAM_B2 ** ADAM_STEP

    def body(w_ref, m_ref, v_ref, p_ref, g_ref, d_ref, nm_ref, nv_ref):
        g = p_ref[0].astype(f32)
        for i in range(1, n):
            g = g + p_ref[i].astype(f32)
        g_ref[...] = g
        nm = ADAM_B1 * m_ref[...] + (1.0 - ADAM_B1) * g
        nv = ADAM_B2 * v_ref[...] + (1.0 - ADAM_B2) * (g * g)
        nm_ref[...] = nm
        nv_ref[...] = nv
        d_ref[...] = -ADAM_LR * ((nm / c1) / (jnp.sqrt(nv / c2) + ADAM_EPS) + ADAM_WD * w_ref[...])

    blk = pl.BlockSpec((tr, c), lambda i: (i, 0))
    return pl.pallas_call(
        body, name=name, grid=(r // tr,),
        in_specs=[blk, blk, blk, pl.BlockSpec((n, tr, c), lambda i: (0, i, 0))],
        out_specs=[blk] * 4, out_shape=[jax.ShapeDtypeStruct((r, c), f32)] * 4,
        compiler_params=_cp("parallel"))(w, m, v, parts)


def _lanes(w):
    return -(-w // 128) * 128


def _pack(arrs):
    rows = []
    for a in arrs:
        k, w = a.shape
        if w % 128:
            a = jnp.pad(a, ((0, 0), (0, _lanes(w) - w)))
        rows.append(a.reshape(-1, 128))
    out = jnp.concatenate(rows, axis=0)
    pad = -out.shape[0] % 8
    return jnp.pad(out, ((0, pad), (0, 0))) if pad else out


def _unpack(packed, shapes):
    outs, off = [], 0
    lead = packed.shape[:-2]
    for k, w in shapes:
        nrow = k * _lanes(w) // 128
        a = packed[..., off:off + nrow, :].reshape(*lead, k, _lanes(w))[..., :w]
        outs.append(a)
        off += nrow
    return outs


def _gathered_cols(a):
    n, k, wl = a.shape
    return jnp.transpose(a, (1, 0, 2)).reshape(k, n * wl)


def _col_shards(a):
    k, w = a.shape
    return jnp.transpose(a.reshape(k, N_DEV, w // N_DEV), (1, 0, 2))


SMALL_PARAMS = (
    ("e_norm_pre", 1, 1024, False), ("e_conv_w", 4, 2048, True), ("e_conv_b", 1, 2048, False),
    ("e_dt_bias", 1, 16, False), ("e_a_log", 1, 16, False), ("e_d_skip", 1, 16, False), ("e_fgate_b", 1, 16, False),
    ("e_ssd_norm", 1, 1024, False), ("e_norm_post", 1, 1024, False), ("o_norm_pre", 1, 1024, True),
    ("o_conv_w", 31, 2048, True), ("o_conv_b", 1, 2048, True), ("o_ln_g", 1, 2048, True), ("o_ln_b", 1, 2048, True),
    ("o_norm_post", 1, 1024, True),
)
BIG_PARAMS = ("e_w_in", "e_w_out", "o_w_in", "o_w_out")
WEIGHT_ORDER = ("e_norm_pre", "e_w_in", "e_conv_w", "e_conv_b", "e_dt_bias", "e_a_log", "e_d_skip", "e_fgate_b",
                "e_ssd_norm", "e_w_out", "e_norm_post", "o_norm_pre", "o_w_in", "o_conv_w", "o_conv_b", "o_ln_g",
                "o_ln_b", "o_w_out", "o_norm_post")
E_IN = 7200
O_IN = 6144


def kernel(x, e_norm_pre, e_w_in, e_conv_w, e_conv_b, e_dt_bias, e_a_log, e_d_skip, e_fgate_b, e_ssd_norm, e_w_out, e_norm_post, o_norm_pre, o_w_in, o_conv_w, o_conv_b, o_ln_g, o_ln_b, o_w_out, o_norm_post, loss_target, m_e_norm_pre, m_e_w_in, m_e_conv_w, m_e_conv_b, m_e_dt_bias, m_e_a_log, m_e_d_skip, m_e_fgate_b, m_e_ssd_norm, m_e_w_out, m_e_norm_post, m_o_norm_pre, m_o_w_in, m_o_conv_w, m_o_conv_b, m_o_ln_g, m_o_ln_b, m_o_w_out, m_o_norm_post, v_e_norm_pre, v_e_w_in, v_e_conv_w, v_e_conv_b, v_e_dt_bias, v_e_a_log, v_e_d_skip, v_e_fgate_b, v_e_ssd_norm, v_e_w_out, v_e_norm_post, v_o_norm_pre, v_o_w_in, v_o_conv_w, v_o_conv_b, v_o_ln_g, v_o_ln_b, v_o_w_out, v_o_norm_post):
    given = dict(locals())
    w_in = {n: given[n] for n in WEIGHT_ORDER}
    m_in = {n: given["m_" + n] for n in WEIGHT_ORDER}
    v_in = {n: given["v_" + n] for n in WEIGHT_ORDER}

    def mat(a):
        return a.reshape(a.shape[-2:])

    xs = mat(x)
    tgt = mat(loss_target)
    xi, yi, ci = _position()
    me = 4 * xi + 2 * yi + ci
    ew, ow = E_IN // N_DEV, O_IN // N_DEV
    wr = D_CONV // N_DEV

    big_local = jnp.concatenate([
        mat(e_w_in).astype(bf16).reshape(ew, D_MODEL), mat(o_w_in).astype(bf16).reshape(ow, D_MODEL),
        mat(e_w_out).astype(bf16), mat(o_w_out).astype(bf16)], axis=0)
    nbig = big_local.shape[0]
    big_local = jnp.pad(big_local, ((0, -nbig % 16), (0, 0)))
    wg = _all_gather(big_local, "gather_weights")
    o0, o1, o2, o3 = 0, ew, ew + ow, ew + ow + wr
    e_w_in_f = _gathered_cols(wg[:, o0:o1].reshape(N_DEV, D_MODEL, ew))
    o_w_in_f = _gathered_cols(wg[:, o1:o2].reshape(N_DEV, D_MODEL, ow))
    e_w_out_f = wg[:, o2:o3].reshape(D_CONV, D_MODEL)
    o_w_out_f = wg[:, o3:o3 + wr].reshape(D_CONV, D_MODEL)
    w_z, w_xbc = e_w_in_f[:, 0:2048], e_w_in_f[:, 2048:4096]
    w_qkv = e_w_in_f[:, 4112:7184]
    w_dtf = jnp.concatenate([e_w_in_f[:, 4096:4112], e_w_in_f[:, 7184:7200], jnp.zeros((D_MODEL, 96), bf16)], axis=1)

    sharded_small = [(n, k, w) for n, k, w, sh in SMALL_PARAMS if sh]
    sg = _all_gather(_pack([mat(w_in[n]) for n, _, _ in sharded_small]), "gather_small_weights")
    full_small = {n: _gathered_cols(a)
                  for (n, _, _), a in zip(sharded_small, _unpack(sg, [(k, w // N_DEV) for _, k, w in sharded_small]))}
    for n, _, _, sh in SMALL_PARAMS:
        if not sh:
            full_small[n] = mat(w_in[n])
    p = full_small
    prm = jnp.zeros((8, 128), f32)
    prm = prm.at[0, 0:16].set(p["e_dt_bias"][0]).at[1, 0:16].set(p["e_a_log"][0]).at[2, 0:16].set(p["e_d_skip"][0])
    prm = prm.at[3, F_LANE:F_LANE + 16].set(p["e_fgate_b"][0])

    u0 = _rms_fwd(xs, p["e_norm_pre"], "rms_pre0")
    z0 = _mm_nn(u0, w_z, bf16, "proj0_z")
    xraw = _mm_nn(u0, w_xbc, bf16, "proj0_xbc")
    qkv = _mm_nn(u0, w_qkv, bf16, "proj0_qkv")
    dtf = _mm_nn(u0, w_dtf, f32, "proj0_dtf")
    pre, act = _conv_ssd_fwd(xraw, p["e_conv_w"], p["e_conv_b"])
    y, hs = _ssd_fwd(act, dtf, prm)
    qa, ka = _fox_prep(qkv, _fox_cumsum(dtf, prm))
    o, lse = _fox_fwd(qa, ka, qkv)
    cat = _gate0_fwd(y, z0, o, p["e_ssd_norm"])
    out0 = _mm_nn(cat, e_w_out_f, f32, "out0")
    x1, u1 = _post0_pre1(xs, out0, p["e_norm_post"], p["o_norm_pre"])

    proj1 = _mm_nn(u1, o_w_in_f, bf16, "proj1")
    hc = _conv_glu_fwd(proj1, p["o_conv_w"], p["o_conv_b"])
    h3 = _ln_gate_fwd(hc, proj1, p["o_ln_g"], p["o_ln_b"])
    out1 = _mm_nn(h3, o_w_out_f, f32, "out1")
    dy, d_out1, dg_post1, loss_part = _final_loss(x1, out1, tgt, p["o_norm_post"])

    dh3 = _mm_nt([(d_out1, 0, o_w_out_f, 0, D_MODEL)], bf16, "dh3")
    g_o_w_out = _mm_tn(h3, d_out1, "dw_out1")
    dhc, dz1, dg_ln, db_ln = _ln_gate_bwd(hc, proj1, dh3, p["o_ln_g"], p["o_ln_b"])
    dval, dgate, dw_conv1, db_conv1 = _conv_glu_bwd(dhc, proj1, p["o_conv_w"])
    dproj1 = jnp.concatenate([dval, dgate, dz1], axis=1)
    du1 = _mm_nt([(dproj1, 0, o_w_in_f, 0, O_IN)], f32, "du1")
    g_o_w_in = _mm_tn(u1, dproj1, "dw_in1", tn=ow, blocked=True)
    dx1, d_out0, dg_pre1, dg_post0 = _mid_bwd(x1, du1, dy, out0, p["o_norm_pre"], p["e_norm_post"])

    dcat = _mm_nt([(d_out0, 0, e_w_out_f, 0, D_MODEL)], bf16, "dcat")
    g_e_w_out = _mm_tn(cat, d_out0, "dw_out0")
    dy_ssd, do, dz0, delta, dg_ssd_norm = _gate0_bwd(y, z0, o, dcat, p["e_ssd_norm"])
    dq_hm, dk_hm, dv = _fox_bwd(qa, ka, qkv, do, lse, delta[0:N_HEADS].reshape(N_HEADS, 1, -1))
    dq, dk, dc = _fox_bwd_post(dq_hm, dk_hm)
    dpre, ddt_raw, dprm = _ssd_bwd(act, pre, dtf, prm, hs, dy_ssd)
    ddtf, dfb = _fox_gate_bwd(dc, dtf, prm, ddt_raw)
    dxraw, dw_conv0, db_conv0 = _conv_ssd_bwd(dpre, xraw, p["e_conv_w"])
    du0 = _mm_nt([(dz0, 0, w_z, 0, 2048), (dxraw, 0, w_xbc, 0, 2048), (dq, 0, w_qkv, 0, 1024), (dk, 0, w_qkv, 1, 1024),
                  (dv, 0, w_qkv, 2, 1024), (ddtf, 0, w_dtf, 0, 128)], f32, "du0")
    gw_dtf = _mm_tn(u0, ddtf, "dw_in0_dtf")
    g_e_w_in_full = jnp.concatenate([
        _mm_tn(u0, dz0, "dw_in0_z"), _mm_tn(u0, dxraw, "dw_in0_xbc"), gw_dtf[:, 0:16],
        _mm_tn(u0, dq, "dw_in0_q"), _mm_tn(u0, dk, "dw_in0_k"), _mm_tn(u0, dv, "dw_in0_v"), gw_dtf[:, 16:32]], axis=1)
    grad_x, dg_pre0 = _first_bwd(xs, du0, dx1, p["e_norm_pre"])

    big_parts = _grad_exchange([
        _col_shards(g_e_w_in_full).astype(bf16), g_e_w_out.reshape(N_DEV, wr, D_MODEL).astype(bf16),
        g_o_w_in.astype(bf16), g_o_w_out.reshape(N_DEV, wr, D_MODEL).astype(bf16)])
    outs = {}
    for n, parts in zip(BIG_PARAMS, big_parts):
        outs[n] = _adamw(mat(w_in[n]), mat(m_in[n]), mat(v_in[n]), parts, "adamw_" + n)

    small_grads = {
        "e_norm_pre": dg_pre0, "e_conv_w": dw_conv0, "e_conv_b": db_conv0, "e_dt_bias": dprm[0:1, 0:16],
        "e_a_log": dprm[1:2, 0:16], "e_d_skip": dprm[2:3, 0:16], "e_fgate_b": dfb[:, F_LANE:F_LANE + 16],
        "e_ssd_norm": dg_ssd_norm, "e_norm_post": dg_post0, "o_norm_pre": dg_pre1, "o_conv_w": dw_conv1,
        "o_conv_b": db_conv1, "o_ln_g": dg_ln, "o_ln_b": db_ln, "o_norm_post": dg_post1,
    }
    gathered = _all_gather(_pack([small_grads[n] for n, _, _, _ in SMALL_PARAMS] + [loss_part]), "gather_small_grads")
    summed = _unpack(_sum_parts(gathered, "sum_small_grads"), [(k, w) for _, k, w, _ in SMALL_PARAMS] + [(1, 128)])
    loss = summed[-1][0, 0]
    g_local = []
    for (n, k, w, sh), g in zip(SMALL_PARAMS, summed):
        g_local.append(lax.dynamic_slice_in_dim(g, me * (w // N_DEV), w // N_DEV, axis=1) if sh else g)
    names = [n for n, _, _, _ in SMALL_PARAMS]
    local_shapes = [(k, w // N_DEV if sh else w) for _, k, w, sh in SMALL_PARAMS]
    res = _adamw(_pack([mat(w_in[n]) for n in names]), _pack([mat(m_in[n]) for n in names]),
                 _pack([mat(v_in[n]) for n in names]), _pack(g_local)[None], "adamw_small", tr=8)
    unpacked = [_unpack(r, local_shapes) for r in res]
    for i, n in enumerate(names):
        outs[n] = tuple(u[i] for u in unpacked)

    ret = [loss, grad_x.reshape(x.shape)]
    for j in range(4):
        ret += [outs[n][j].reshape(w_in[n].shape) for n in WEIGHT_ORDER]
    return tuple(ret)
```

```python
import jax
import jax.numpy as jnp
from jax import lax
from jax.experimental import pallas as pl
from jax.experimental.pallas import tpu as pltpu

f32 = jnp.float32
bf16 = jnp.bfloat16

N_DEV = 8
D_MODEL = 1024
N_HEADS = 16
HEAD_DIM = 64
N_GROUPS = 4
HEADS_PER_GROUP = 4
D_STATE = 128
CHUNK = 128
SSD_CONV = 4
CONV_WIDTH = 31
D_CONV = 2048
EPS = 1e-6
XBC_W = 2048
B_OFF = 1024
C_OFF = 1536
F_LANE = 16
HALO = 32

ADAM_LR = 0.001
ADAM_B1 = 0.9
ADAM_B2 = 0.999
ADAM_EPS = 1e-08
ADAM_WD = 0.01
ADAM_STEP = 10

VMEM_LIMIT_BYTES = 56 * 1024 * 1024
ROW_TILE = 512
CONV_ROW_TILE = 256
CONV_COL_TILE = 512
CONV_SUB = 64
ATTN_TILE = 512

NT = (((1,), (1,)), ((), ()))
TN = (((0,), (0,)), ((), ()))
HIGHEST = lax.Precision.HIGHEST
NEG = -1e30


def _cp(*sem):
    return pltpu.CompilerParams(dimension_semantics=sem if sem else None, vmem_limit_bytes=VMEM_LIMIT_BYTES)


def _sigmoid(x):
    return jax.nn.sigmoid(x)


def _silu(x):
    return x * _sigmoid(x)


def _dsilu(x):
    s = _sigmoid(x)
    return s * (1.0 + x * (1.0 - s))


def _softplus(x):
    return jnp.maximum(x, 0.0) + jnp.log(1.0 + jnp.exp(-jnp.abs(x)))


def _log_sigmoid(x):
    return jnp.minimum(x, 0.0) - jnp.log(1.0 + jnp.exp(-jnp.abs(x)))


def _dot(a, b, dims=None, precision=None):
    if dims is None:
        return jnp.dot(a, b, preferred_element_type=f32, precision=precision)
    return lax.dot_general(a, b, dims, preferred_element_type=f32, precision=precision)


def _mm_nn(a, b, out_dtype, name, tm=512, tn=1024):
    m, k = a.shape
    n = b.shape[1]
    tm, tn = min(tm, m), min(tn, n)

    def body(a_ref, b_ref, o_ref):
        o_ref[...] = _dot(a_ref[...], b_ref[...]).astype(o_ref.dtype)

    return pl.pallas_call(
        body, name=name, grid=(n // tn, m // tm),
        in_specs=[pl.BlockSpec((tm, k), lambda j, i: (i, 0)), pl.BlockSpec((k, tn), lambda j, i: (0, j))],
        out_specs=pl.BlockSpec((tm, tn), lambda j, i: (i, j)),
        out_shape=jax.ShapeDtypeStruct((m, n), out_dtype), compiler_params=_cp("parallel", "parallel"))(a, b)


def _mm_nt(pairs, out_dtype, name, tm=512, tn=512):
    m = pairs[0][0].shape[0]
    n = pairs[0][2].shape[0]
    tm, tn = min(tm, m), min(tn, n)
    npair = len(pairs)

    def body(*refs):
        o_ref = refs[-1]
        acc = None
        for p in range(npair):
            d = _dot(refs[2 * p][...].astype(bf16), refs[2 * p + 1][...], NT)
            acc = d if acc is None else acc + d
        o_ref[...] = acc.astype(o_ref.dtype)

    in_specs, args = [], []
    for a, acb, b, bcb, k in pairs:
        in_specs.append(pl.BlockSpec((tm, k), lambda j, i, acb=acb: (i, acb)))
        in_specs.append(pl.BlockSpec((tn, k), lambda j, i, bcb=bcb: (j, bcb)))
        args += [a, b]
    return pl.pallas_call(
        body, name=name, grid=(n // tn, m // tm), in_specs=in_specs,
        out_specs=pl.BlockSpec((tm, tn), lambda j, i: (i, j)),
        out_shape=jax.ShapeDtypeStruct((m, n), out_dtype), compiler_params=_cp("parallel", "parallel"))(*args)


def _mm_tn(a, b, name, a_cb=0, am=None, b_cb=0, bn=None, tn=1024, tk=512, blocked=False):
    k = a.shape[0]
    am = a.shape[1] if am is None else am
    bn = b.shape[1] if bn is None else bn
    tm = min(1024, am)
    tn, tk = min(tn, bn), min(tk, k)
    a_off, b_off = a_cb * (am // tm), b_cb * (bn // tn)

    def body(a_ref, b_ref, o_ref):
        @pl.when(pl.program_id(2) == 0)
        def _():
            o_ref[...] = jnp.zeros_like(o_ref)
        d = _dot(a_ref[...].astype(bf16), b_ref[...].astype(bf16), TN)
        o_ref[...] += d.reshape(o_ref.shape)

    if blocked:
        out_spec = pl.BlockSpec((1, tm, tn), lambda i, j, kk: (j, i, 0))
        out_shape = jax.ShapeDtypeStruct((bn // tn, am, tn), f32)
    else:
        out_spec = pl.BlockSpec((tm, tn), lambda i, j, kk: (i, j))
        out_shape = jax.ShapeDtypeStruct((am, bn), f32)
    return pl.pallas_call(
        body, name=name, grid=(am // tm, bn // tn, k // tk),
        in_specs=[pl.BlockSpec((tk, tm), lambda i, j, kk: (kk, a_off + i)),
                  pl.BlockSpec((tk, tn), lambda i, j, kk: (kk, b_off + j))],
        out_specs=out_spec, out_shape=out_shape,
        compiler_params=_cp("parallel", "parallel", "arbitrary"))(a, b)


def _rowspec(ts, w, cb=0):
    return pl.BlockSpec((ts, w), lambda i: (i, cb))


def _vecspec(w):
    return pl.BlockSpec((1, w), lambda i: (0, 0))


def _rms_fwd(x, g, name):
    s, d = x.shape
    ts = min(ROW_TILE, s)

    def body(x_ref, g_ref, u_ref):
        xv = x_ref[...]
        r = lax.rsqrt(jnp.mean(xv * xv, axis=-1, keepdims=True) + EPS)
        u_ref[...] = (xv * r * g_ref[...]).astype(bf16)

    return pl.pallas_call(
        body, name=name, grid=(s // ts,), in_specs=[_rowspec(ts, d), _vecspec(d)], out_specs=_rowspec(ts, d),
        out_shape=jax.ShapeDtypeStruct((s, d), bf16), compiler_params=_cp("parallel"))(x, g)


def _rms_bwd_vals(xv, g, dy):
    r = lax.rsqrt(jnp.mean(xv * xv, axis=-1, keepdims=True) + EPS)
    xh = xv * r
    dg = jnp.sum(dy * xh, axis=0, keepdims=True)
    dxh = dy * g
    dx = r * (dxh - xh * jnp.mean(dxh * xh, axis=-1, keepdims=True))
    return dx, dg


def _gate0_fwd(y, z, o, ssd_norm):
    s = y.shape[0]
    ts = min(ROW_TILE, s)
    gw = D_MODEL // N_GROUPS

    def body(y_ref, zs_ref, zf_ref, o_ref, w_ref, cat_ref):
        yg = y_ref[...].astype(f32) * _silu(zs_ref[...].astype(f32))
        for g in range(N_GROUPS):
            seg = yg[:, gw * g:gw * (g + 1)]
            r = lax.rsqrt(jnp.mean(seg * seg, axis=-1, keepdims=True) + EPS)
            cat_ref[:, gw * g:gw * (g + 1)] = (seg * r * w_ref[:, gw * g:gw * (g + 1)]).astype(bf16)
        cat_ref[:, D_MODEL:] = (o_ref[...].astype(f32) * _silu(zf_ref[...].astype(f32))).astype(bf16)

    return pl.pallas_call(
        body, name="gate0_fwd", grid=(s // ts,),
        in_specs=[_rowspec(ts, D_MODEL), _rowspec(ts, D_MODEL, 0), _rowspec(ts, D_MODEL, 1), _rowspec(ts, D_MODEL),
                  _vecspec(D_MODEL)],
        out_specs=_rowspec(ts, 2 * D_MODEL),
        out_shape=jax.ShapeDtypeStruct((s, 2 * D_MODEL), bf16), compiler_params=_cp("parallel"))(y, z, z, o, ssd_norm)


def _post0_pre1(x, out0, g_post0, g_pre1):
    s, d = x.shape
    ts = min(ROW_TILE, s)

    def body(x_ref, o_ref, gp_ref, gn_ref, x1_ref, u1_ref):
        ov = o_ref[...]
        r = lax.rsqrt(jnp.mean(ov * ov, axis=-1, keepdims=True) + EPS)
        x1 = x_ref[...] + ov * r * gp_ref[...]
        x1_ref[...] = x1
        r1 = lax.rsqrt(jnp.mean(x1 * x1, axis=-1, keepdims=True) + EPS)
        u1_ref[...] = (x1 * r1 * gn_ref[...]).astype(bf16)

    return pl.pallas_call(
        body, name="post0_pre1", grid=(s // ts,),
        in_specs=[_rowspec(ts, d), _rowspec(ts, d), _vecspec(d), _vecspec(d)],
        out_specs=[_rowspec(ts, d), _rowspec(ts, d)],
        out_shape=[jax.ShapeDtypeStruct((s, d), f32), jax.ShapeDtypeStruct((s, d), bf16)],
        compiler_params=_cp("parallel"))(x, out0, g_post0, g_pre1)


def _ln_vals(hc, g, b):
    mu = jnp.mean(hc, axis=-1, keepdims=True)
    xc = hc - mu
    rstd = lax.rsqrt(jnp.mean(xc * xc, axis=-1, keepdims=True) + EPS)
    xh = xc * rstd
    return xh, rstd, xh * g + b


def _ln_gate_fwd(hc, proj1, ln_g, ln_b):
    s = hc.shape[0]
    ts = min(ROW_TILE, s)

    def body(hc_ref, z_ref, g_ref, b_ref, h3_ref):
        _, _, ln = _ln_vals(hc_ref[...].astype(f32), g_ref[...], b_ref[...])
        h3_ref[...] = (_silu(ln) * _silu(z_ref[...].astype(f32))).astype(bf16)

    return pl.pallas_call(
        body, name="ln_gate_fwd", grid=(s // ts,),
        in_specs=[_rowspec(ts, D_CONV), _rowspec(ts, D_CONV, 2), _vecspec(D_CONV), _vecspec(D_CONV)],
        out_specs=_rowspec(ts, D_CONV),
        out_shape=jax.ShapeDtypeStruct((s, D_CONV), bf16), compiler_params=_cp("parallel"))(hc, proj1, ln_g, ln_b)


def _final_loss(x1, out1, tgt, g_post1):
    s, d = x1.shape
    ts = min(ROW_TILE, s)

    def body(x1_ref, o_ref, t_ref, g_ref, dy_ref, do_ref, dg_ref, loss_ref):
        i = pl.program_id(0)

        @pl.when(i == 0)
        def _():
            dg_ref[...] = jnp.zeros_like(dg_ref)
            loss_ref[...] = jnp.zeros_like(loss_ref)
        ov = o_ref[...]
        g = g_ref[...]
        r = lax.rsqrt(jnp.mean(ov * ov, axis=-1, keepdims=True) + EPS)
        diff = x1_ref[...] + ov * r * g - t_ref[...]
        row = jnp.mean(diff * diff, axis=-1, keepdims=True)
        loss_ref[...] += jnp.broadcast_to(0.5 * jnp.sum(row, axis=0, keepdims=True), loss_ref.shape)
        dy = diff * (1.0 / d)
        dy_ref[...] = dy
        dx, dg = _rms_bwd_vals(ov, g, dy)
        do_ref[...] = dx.astype(bf16)
        dg_ref[...] += dg

    return pl.pallas_call(
        body, name="final_loss", grid=(s // ts,),
        in_specs=[_rowspec(ts, d), _rowspec(ts, d), _rowspec(ts, d), _vecspec(d)],
        out_specs=[_rowspec(ts, d), _rowspec(ts, d), _vecspec(d), _vecspec(128)],
        out_shape=[jax.ShapeDtypeStruct((s, d), f32), jax.ShapeDtypeStruct((s, d), bf16),
                   jax.ShapeDtypeStruct((1, d), f32), jax.ShapeDtypeStruct((1, 128), f32)],
        compiler_params=_cp("arbitrary"))(x1, out1, tgt, g_post1)


def _ln_gate_bwd(hc, proj1, dh3, ln_g, ln_b):
    s = hc.shape[0]
    ts = min(ROW_TILE, s)

    def body(hc_ref, z_ref, dh_ref, g_ref, b_ref, dhc_ref, dz_ref, dg_ref, db_ref):
        @pl.when(pl.program_id(0) == 0)
        def _():
            dg_ref[...] = jnp.zeros_like(dg_ref)
            db_ref[...] = jnp.zeros_like(db_ref)
        g = g_ref[...]
        xh, rstd, ln = _ln_vals(hc_ref[...].astype(f32), g, b_ref[...])
        zv = z_ref[...].astype(f32)
        dh3 = dh_ref[...].astype(f32)
        dz_ref[...] = (dh3 * _silu(ln) * _dsilu(zv)).astype(bf16)
        dln = dh3 * _silu(zv) * _dsilu(ln)
        dg_ref[...] += jnp.sum(dln * xh, axis=0, keepdims=True)
        db_ref[...] += jnp.sum(dln, axis=0, keepdims=True)
        dxh = dln * g
        dhc = rstd * (dxh - jnp.mean(dxh, axis=-1, keepdims=True) - xh * jnp.mean(dxh * xh, axis=-1, keepdims=True))
        dhc_ref[...] = dhc.astype(bf16)

    return pl.pallas_call(
        body, name="ln_gate_bwd", grid=(s // ts,),
        in_specs=[_rowspec(ts, D_CONV), _rowspec(ts, D_CONV, 2), _rowspec(ts, D_CONV), _vecspec(D_CONV),
                  _vecspec(D_CONV)],
        out_specs=[_rowspec(ts, D_CONV), _rowspec(ts, D_CONV), _vecspec(D_CONV), _vecspec(D_CONV)],
        out_shape=[jax.ShapeDtypeStruct((s, D_CONV), bf16), jax.ShapeDtypeStruct((s, D_CONV), bf16),
                   jax.ShapeDtypeStruct((1, D_CONV), f32), jax.ShapeDtypeStruct((1, D_CONV), f32)],
        compiler_params=_cp("arbitrary"))(hc, proj1, dh3, ln_g, ln_b)


def _mid_bwd(x1, du1, dy, out0, g_pre1, g_post0):
    s, d = x1.shape
    ts = min(ROW_TILE, s)

    def body(x1_ref, du_ref, dy_ref, o_ref, gn_ref, gp_ref, dx1_ref, do_ref, dgn_ref, dgp_ref):
        @pl.when(pl.program_id(0) == 0)
        def _():
            dgn_ref[...] = jnp.zeros_like(dgn_ref)
            dgp_ref[...] = jnp.zeros_like(dgp_ref)
        dxa, dgn = _rms_bwd_vals(x1_ref[...], gn_ref[...], du_ref[...])
        dx1 = dy_ref[...] + dxa
        dx1_ref[...] = dx1
        dgn_ref[...] += dgn
        dxo, dgp = _rms_bwd_vals(o_ref[...], gp_ref[...], dx1)
        do_ref[...] = dxo.astype(bf16)
        dgp_ref[...] += dgp

    return pl.pallas_call(
        body, name="mid_bwd", grid=(s // ts,),
        in_specs=[_rowspec(ts, d)] * 4 + [_vecspec(d), _vecspec(d)],
        out_specs=[_rowspec(ts, d), _rowspec(ts, d), _vecspec(d), _vecspec(d)],
        out_shape=[jax.ShapeDtypeStruct((s, d), f32), jax.ShapeDtypeStruct((s, d), bf16),
                   jax.ShapeDtypeStruct((1, d), f32), jax.ShapeDtypeStruct((1, d), f32)],
        compiler_params=_cp("arbitrary"))(x1, du1, dy, out0, g_pre1, g_post0)


def _first_bwd(x, du0, dx1, g_pre0):
    s, d = x.shape
    ts = min(ROW_TILE, s)

    def body(x_ref, du_ref, dx1_ref, g_ref, dx_ref, dg_ref):
        @pl.when(pl.program_id(0) == 0)
        def _():
            dg_ref[...] = jnp.zeros_like(dg_ref)
        dxa, dg = _rms_bwd_vals(x_ref[...], g_ref[...], du_ref[...])
        dx_ref[...] = dx1_ref[...] + dxa
        dg_ref[...] += dg

    return pl.pallas_call(
        body, name="first_bwd", grid=(s // ts,),
        in_specs=[_rowspec(ts, d)] * 3 + [_vecspec(d)],
        out_specs=[_rowspec(ts, d), _vecspec(d)],
        out_shape=[jax.ShapeDtypeStruct((s, d), f32), jax.ShapeDtypeStruct((1, d), f32)],
        compiler_params=_cp("arbitrary"))(x, du0, dx1, g_pre0)


def _gate0_bwd(y, z, o, dcat, ssd_norm):
    s = y.shape[0]
    ts = min(ROW_TILE, s)
    gw = D_MODEL // N_GROUPS

    def body(y_ref, zs_ref, zf_ref, o_ref, dn_ref, dg_ref, w_ref, dy_ref, do_ref, dz_ref, delta_ref, dw_ref):
        @pl.when(pl.program_id(0) == 0)
        def _():
            dw_ref[...] = jnp.zeros_like(dw_ref)
        yv = y_ref[...].astype(f32)
        zs = zs_ref[...].astype(f32)
        sz = _silu(zs)
        yg = yv * sz
        dyn = dn_ref[...].astype(f32)
        for g in range(N_GROUPS):
            sl = slice(gw * g, gw * (g + 1))
            seg = yg[:, sl]
            r = lax.rsqrt(jnp.mean(seg * seg, axis=-1, keepdims=True) + EPS)
            yh = seg * r
            dn = dyn[:, sl]
            dw_ref[:, sl] += jnp.sum(dn * yh, axis=0, keepdims=True)
            dyh = dn * w_ref[:, sl]
            dyg = r * (dyh - yh * jnp.mean(dyh * yh, axis=-1, keepdims=True))
            dy_ref[:, sl] = (dyg * sz[:, sl]).astype(bf16)
            dz_ref[:, sl] = (dyg * yv[:, sl] * _dsilu(zs[:, sl])).astype(bf16)
        zf = zf_ref[...].astype(f32)
        ov = o_ref[...].astype(f32)
        dog = dg_ref[...].astype(f32)
        dov = (dog * _silu(zf)).astype(bf16)
        do_ref[...] = dov
        dz_ref[:, D_MODEL:] = (dog * ov * _dsilu(zf)).astype(bf16)
        prod = dov.astype(f32) * ov
        lane = lax.broadcasted_iota(jnp.int32, (ts, 128), 1)
        delta = jnp.zeros((ts, 128), f32)
        for h in range(N_HEADS):
            dh = jnp.sum(prod[:, HEAD_DIM * h:HEAD_DIM * (h + 1)], axis=-1, keepdims=True)
            delta = delta + jnp.where(lane == h, dh, 0.0)
        delta_ref[...] = delta.T

    return pl.pallas_call(
        body, name="gate0_bwd", grid=(s // ts,),
        in_specs=[_rowspec(ts, D_MODEL), _rowspec(ts, D_MODEL, 0), _rowspec(ts, D_MODEL, 1), _rowspec(ts, D_MODEL),
                  _rowspec(ts, D_MODEL, 0), _rowspec(ts, D_MODEL, 1), _vecspec(D_MODEL)],
        out_specs=[_rowspec(ts, D_MODEL), _rowspec(ts, D_MODEL), _rowspec(ts, 2 * D_MODEL),
                   pl.BlockSpec((128, ts), lambda i: (0, i)), _vecspec(D_MODEL)],
        out_shape=[jax.ShapeDtypeStruct((s, D_MODEL), bf16), jax.ShapeDtypeStruct((s, D_MODEL), bf16),
                   jax.ShapeDtypeStruct((s, 2 * D_MODEL), bf16), jax.ShapeDtypeStruct((128, s), f32),
                   jax.ShapeDtypeStruct((1, D_MODEL), f32)],
        compiler_params=_cp("arbitrary"))(y, z, z, o, dcat, dcat, ssd_norm)


def _conv_grid(s, c):
    ts, cb = min(CONV_ROW_TILE, s), min(CONV_COL_TILE, c)
    return ts, cb, (c // cb, s // ts)


def _cur(ts, cb, off=0):
    return pl.BlockSpec((ts, cb), lambda c, i: (i, c + off))


def _prev_halo(ts, cb, off=0):
    return pl.BlockSpec((HALO, cb), lambda c, i: (jnp.maximum(i * (ts // HALO) - 1, 0), c + off))


def _next_halo(ts, cb, s, off=0):
    return pl.BlockSpec((HALO, cb), lambda c, i: (jnp.minimum((i + 1) * (ts // HALO), s // HALO - 1), c + off))


def _wspec(k, cb):
    return pl.BlockSpec((k, cb), lambda c, i: (0, c))


def _phases(offsets):
    return sorted({o % 8 for o in offsets} - {0})


def _shift_scratch(offsets, ts, cb):
    return pltpu.VMEM((max(len(_phases(offsets)), 1), ts + HALO - 8, cb), f32)


def _fill_phases(ext_ref, sh_ref, offsets, ts):
    for j, r in enumerate(_phases(offsets)):
        sh_ref[j] = ext_ref[pl.ds(r, ts + HALO - 8), :]


def _slab(ext_ref, sh_ref, offsets, off, start):
    r = off % 8
    a = off - r + start
    if r == 0:
        return ext_ref[a:a + CONV_SUB, :]
    return sh_ref[_phases(offsets).index(r), a:a + CONV_SUB, :]


def _conv_taps(ext_ref, sh_ref, w_ref, b_ref, ts, k_taps, emit):
    offsets = [HALO - (k_taps - 1) + k for k in range(k_taps)]
    _fill_phases(ext_ref, sh_ref, offsets, ts)
    for sb in range(ts // CONV_SUB):
        acc = b_ref[...]
        for k in range(k_taps):
            acc = acc + w_ref[k:k + 1, :] * _slab(ext_ref, sh_ref, offsets, offsets[k], sb * CONV_SUB)
        emit(slice(sb * CONV_SUB, (sb + 1) * CONV_SUB), acc)


def _conv_ssd_fwd(xraw, w, b):
    s, c = xraw.shape
    ts, cb, grid = _conv_grid(s, c)
    offsets = [HALO - (SSD_CONV - 1) + k for k in range(SSD_CONV)]

    def body(x_ref, xh_ref, w_ref, b_ref, pre_ref, act_ref, ext_ref, sh_ref):
        first = pl.program_id(1) == 0
        ext_ref[0:HALO, :] = jnp.where(first, 0.0, xh_ref[...].astype(f32))
        ext_ref[HALO:, :] = x_ref[...].astype(f32)

        def emit(rows, pre):
            pre_ref[rows, :] = pre.astype(bf16)
            act_ref[rows, :] = _silu(pre).astype(bf16)
        _conv_taps(ext_ref, sh_ref, w_ref, b_ref, ts, SSD_CONV, emit)

    return pl.pallas_call(
        body, name="conv_ssd_fwd", grid=grid,
        in_specs=[_cur(ts, cb), _prev_halo(ts, cb), _wspec(SSD_CONV, cb), _wspec(1, cb)],
        out_specs=[_cur(ts, cb), _cur(ts, cb)],
        out_shape=[jax.ShapeDtypeStruct((s, c), bf16)] * 2,
        scratch_shapes=[pltpu.VMEM((HALO + ts, cb), f32), _shift_scratch(offsets, ts, cb)],
        compiler_params=_cp("parallel", "parallel"))(xraw, xraw, w, b)


def _conv_glu_fwd(proj1, w, b):
    s = proj1.shape[0]
    c = D_CONV
    ts, cb, grid = _conv_grid(s, c)
    goff = c // cb

    offsets = [HALO - (CONV_WIDTH - 1) + k for k in range(CONV_WIDTH)]

    def body(v_ref, g_ref, vh_ref, gh_ref, w_ref, b_ref, hc_ref, ext_ref, sh_ref):
        first = pl.program_id(1) == 0
        hh = vh_ref[...].astype(f32) * _sigmoid(gh_ref[...].astype(f32))
        ext_ref[0:HALO, :] = jnp.where(first, 0.0, hh)
        ext_ref[HALO:, :] = v_ref[...].astype(f32) * _sigmoid(g_ref[...].astype(f32))

        def emit(rows, hc):
            hc_ref[rows, :] = hc.astype(bf16)
        _conv_taps(ext_ref, sh_ref, w_ref, b_ref, ts, CONV_WIDTH, emit)

    return pl.pallas_call(
        body, name="conv_glu_fwd", grid=grid,
        in_specs=[_cur(ts, cb), _cur(ts, cb, goff), _prev_halo(ts, cb), _prev_halo(ts, cb, goff),
                  _wspec(CONV_WIDTH, cb), _wspec(1, cb)],
        out_specs=_cur(ts, cb),
        out_shape=jax.ShapeDtypeStruct((s, c), bf16),
        scratch_shapes=[pltpu.VMEM((HALO + ts, cb), f32), _shift_scratch(offsets, ts, cb)],
        compiler_params=_cp("parallel", "parallel"))(proj1, proj1, proj1, proj1, w, b)


def _conv_bwd_offsets(k_taps):
    return [k_taps - 1 - k for k in range(k_taps)], [HALO - (k_taps - 1) + k for k in range(k_taps)]


def _conv_bwd_scratch(k_taps, ts, cb):
    d_offs, x_offs = _conv_bwd_offsets(k_taps)
    return [pltpu.VMEM((ts + HALO, cb), f32), _shift_scratch(d_offs, ts, cb),
            pltpu.VMEM((HALO + ts, cb), f32), _shift_scratch(x_offs, ts, cb),
            pltpu.VMEM((k_taps, 8, cb), f32), pltpu.VMEM((8, cb), f32)]


def _conv_bwd_core(dp, dpn_ref, last, w_ref, scratch, dw_ref, db_ref, ts, k_taps, emit):
    dext_ref, dsh_ref, xext_ref, xsh_ref, dw8_ref, db8_ref = scratch
    d_offs, x_offs = _conv_bwd_offsets(k_taps)
    dext_ref[0:ts, :] = dp
    dext_ref[ts:, :] = jnp.where(last, 0.0, dpn_ref[...].astype(f32))
    _fill_phases(dext_ref, dsh_ref, d_offs, ts)
    _fill_phases(xext_ref, xsh_ref, x_offs, ts)

    @pl.when(pl.program_id(1) == 0)
    def _():
        dw8_ref[...] = jnp.zeros_like(dw8_ref)
        db8_ref[...] = jnp.zeros_like(db8_ref)
    cb = dp.shape[1]
    for sb in range(ts // CONV_SUB):
        start = sb * CONV_SUB
        dpv = dext_ref[start:start + CONV_SUB, :]
        dx = None
        for k in range(k_taps):
            t = w_ref[k:k + 1, :] * _slab(dext_ref, dsh_ref, d_offs, d_offs[k], start)
            dx = t if dx is None else dx + t
            prod = dpv * _slab(xext_ref, xsh_ref, x_offs, x_offs[k], start)
            dw8_ref[k] += jnp.sum(prod.reshape(CONV_SUB // 8, 8, cb), axis=0)
        db8_ref[...] += jnp.sum(dpv.reshape(CONV_SUB // 8, 8, cb), axis=0)
        emit(slice(start, start + CONV_SUB), dx)

    @pl.when(last)
    def _():
        dw_ref[...] = jnp.sum(dw8_ref[...], axis=1)
        db_ref[...] = jnp.sum(db8_ref[...], axis=0, keepdims=True)


def _conv_ssd_bwd(dpre, xraw, w):
    s, c = xraw.shape
    ts, cb, grid = _conv_grid(s, c)
    nb = s // ts

    def body(dp_ref, dpn_ref, x_ref, xh_ref, w_ref, dx_ref, dw_ref, db_ref, *scratch):
        i = pl.program_id(1)
        xext_ref = scratch[2]
        xext_ref[0:HALO, :] = jnp.where(i == 0, 0.0, xh_ref[...].astype(f32))
        xext_ref[HALO:, :] = x_ref[...].astype(f32)

        def emit(rows, dx):
            dx_ref[rows, :] = dx.astype(bf16)
        _conv_bwd_core(dp_ref[...].astype(f32), dpn_ref, i == nb - 1, w_ref, scratch, dw_ref, db_ref, ts, SSD_CONV, emit)

    return pl.pallas_call(
        body, name="conv_ssd_bwd", grid=grid,
        in_specs=[_cur(ts, cb), _next_halo(ts, cb, s), _cur(ts, cb), _prev_halo(ts, cb), _wspec(SSD_CONV, cb)],
        out_specs=[_cur(ts, cb), _wspec(SSD_CONV, cb), _wspec(1, cb)],
        out_shape=[jax.ShapeDtypeStruct((s, c), bf16), jax.ShapeDtypeStruct((SSD_CONV, c), f32),
                   jax.ShapeDtypeStruct((1, c), f32)],
        scratch_shapes=_conv_bwd_scratch(SSD_CONV, ts, cb),
        compiler_params=_cp("parallel", "arbitrary"))(dpre, dpre, xraw, xraw, w)


def _conv_glu_bwd(dhc, proj1, w):
    s = proj1.shape[0]
    c = D_CONV
    ts, cb, grid = _conv_grid(s, c)
    nb = s // ts
    goff = c // cb

    def body(dp_ref, dpn_ref, v_ref, g_ref, vh_ref, gh_ref, w_ref, dv_ref, dg_ref, dw_ref, db_ref, *scratch):
        i = pl.program_id(1)
        xext_ref = scratch[2]
        xext_ref[0:HALO, :] = jnp.where(i == 0, 0.0, vh_ref[...].astype(f32) * _sigmoid(gh_ref[...].astype(f32)))
        xext_ref[HALO:, :] = v_ref[...].astype(f32) * _sigmoid(g_ref[...].astype(f32))

        def emit(rows, dh):
            val = v_ref[rows, :].astype(f32)
            sg = _sigmoid(g_ref[rows, :].astype(f32))
            dv_ref[rows, :] = (dh * sg).astype(bf16)
            dg_ref[rows, :] = (dh * val * sg * (1.0 - sg)).astype(bf16)
        _conv_bwd_core(dp_ref[...].astype(f32), dpn_ref, i == nb - 1, w_ref, scratch, dw_ref, db_ref, ts, CONV_WIDTH, emit)

    return pl.pallas_call(
        body, name="conv_glu_bwd", grid=grid,
        in_specs=[_cur(ts, cb), _next_halo(ts, cb, s), _cur(ts, cb), _cur(ts, cb, goff), _prev_halo(ts, cb),
                  _prev_halo(ts, cb, goff), _wspec(CONV_WIDTH, cb)],
        out_specs=[_cur(ts, cb), _cur(ts, cb), _wspec(CONV_WIDTH, cb), _wspec(1, cb)],
        out_shape=[jax.ShapeDtypeStruct((s, c), bf16), jax.ShapeDtypeStruct((s, c), bf16),
                   jax.ShapeDtypeStruct((CONV_WIDTH, c), f32), jax.ShapeDtypeStruct((1, c), f32)],
        scratch_shapes=_conv_bwd_scratch(CONV_WIDTH, ts, cb),
        compiler_params=_cp("parallel", "arbitrary"))(dhc, dhc, proj1, proj1, proj1, proj1, w)


def _ssd_common(dt_ref, prm_ref):
    l = CHUNK
    dtb = prm_ref[0:1, :]
    a = -jnp.exp(prm_ref[1:2, :])
    dsk = prm_ref[2:3, :]
    zraw = dt_ref[...] + dtb
    dt = _softplus(zraw)
    da = dt * a
    row = lax.broadcasted_iota(jnp.int32, (l, l), 0)
    col = lax.broadcasted_iota(jnp.int32, (l, l), 1)
    causal = row >= col
    cs = _dot(causal.astype(f32), da, precision=HIGHEST)
    return a, dsk, zraw, dt, cs, cs.T, causal, row, col


def _ssd_fwd(act, dtf, prm):
    s = act.shape[0]
    nc = s // CHUNK
    l = CHUNK

    def body(xs_ref, dt_ref, prm_ref, y_ref, hs_ref, st_ref):
        @pl.when(pl.program_id(0) == 0)
        def _():
            st_ref[...] = jnp.zeros_like(st_ref)
        a, dsk, _, dt, cs, cst, causal, _, _ = _ssd_common(dt_ref, prm_ref)
        for g in range(N_GROUPS):
            bm = xs_ref[:, B_OFF + D_STATE * g:B_OFF + D_STATE * (g + 1)]
            cm = xs_ref[:, C_OFF + D_STATE * g:C_OFF + D_STATE * (g + 1)]
            gmat = _dot(cm, bm, NT)
            for r in range(HEADS_PER_GROUP):
                h = HEADS_PER_GROUP * g + r
                hsl = slice(HEAD_DIM * h, HEAD_DIM * (h + 1))
                xv = xs_ref[:, hsl].astype(f32)
                csc = cs[:, h:h + 1]
                csr = cst[h:h + 1, :]
                cl = cs[l - 1:l, h:h + 1]
                dk = jnp.exp(jnp.where(causal, csc - csr, NEG))
                xd = xv * dt[:, h:h + 1]
                hp = st_ref[h]
                hs_ref[0, h] = hp
                ydiag = _dot((gmat * dk).astype(bf16), xd.astype(bf16))
                yoff = _dot(cm, hp.astype(bf16), NT) * jnp.exp(csc)
                y_ref[:, hsl] = (ydiag + yoff + xv * dsk[:, h:h + 1]).astype(bf16)
                st = _dot((xd * jnp.exp(cl - csc)).astype(bf16), bm, TN)
                st_ref[h] = hp * jnp.exp(cl) + st

    return pl.pallas_call(
        body, name="ssd_fwd", grid=(nc,),
        in_specs=[pl.BlockSpec((l, XBC_W), lambda i: (i, 0)), pl.BlockSpec((l, 128), lambda i: (i, 0)),
                  pl.BlockSpec((8, 128), lambda i: (0, 0))],
        out_specs=[pl.BlockSpec((l, D_MODEL), lambda i: (i, 0)),
                   pl.BlockSpec((1, N_HEADS, HEAD_DIM, D_STATE), lambda i: (i, 0, 0, 0))],
        out_shape=[jax.ShapeDtypeStruct((s, D_MODEL), bf16),
                   jax.ShapeDtypeStruct((nc, N_HEADS, HEAD_DIM, D_STATE), f32)],
        scratch_shapes=[pltpu.VMEM((N_HEADS, HEAD_DIM, D_STATE), f32)],
        compiler_params=_cp("arbitrary"))(act, dtf, prm)


def _ssd_bwd(act, pre, dtf, prm, hs, dy):
    s = act.shape[0]
    nc = s // CHUNK
    l = CHUNK

    def body(xs_ref, pre_ref, dt_ref, prm_ref, hs_ref, dy_ref, dpre_ref, ddt_ref, dprm_ref, dh_ref):
        @pl.when(pl.program_id(0) == 0)
        def _():
            dh_ref[...] = jnp.zeros_like(dh_ref)
            dprm_ref[...] = jnp.zeros_like(dprm_ref)
        a, dsk, zraw, dt, cs, cst, causal, row, col = _ssd_common(dt_ref, prm_ref)
        lane = lax.broadcasted_iota(jnp.int32, (l, 128), 1)
        rowl = lax.broadcasted_iota(jnp.int32, (l, 128), 0)
        sub = lax.broadcasted_iota(jnp.int32, (128, l), 0)
        lane1 = lax.broadcasted_iota(jnp.int32, (1, 128), 1)
        dcs_c = jnp.zeros((l, 128), f32)
        dcs_r = jnp.zeros((128, l), f32)
        ddt_c = jnp.zeros((l, 128), f32)
        dd_row = jnp.zeros((1, 128), f32)
        for g in range(N_GROUPS):
            bsl = slice(B_OFF + D_STATE * g, B_OFF + D_STATE * (g + 1))
            csl = slice(C_OFF + D_STATE * g, C_OFF + D_STATE * (g + 1))
            bm = xs_ref[:, bsl]
            cm = xs_ref[:, csl]
            gmat = _dot(cm, bm, NT)
            dgm = jnp.zeros((l, l), f32)
            dbg = jnp.zeros((l, D_STATE), f32)
            dcg = jnp.zeros((l, D_STATE), f32)
            for r in range(HEADS_PER_GROUP):
                h = HEADS_PER_GROUP * g + r
                hsl = slice(HEAD_DIM * h, HEAD_DIM * (h + 1))
                xv = xs_ref[:, hsl].astype(f32)
                dyv = dy_ref[:, hsl].astype(f32)
                dyb = dyv.astype(bf16)
                csc = cs[:, h:h + 1]
                csr = cst[h:h + 1, :]
                cl = cs[l - 1:l, h:h + 1]
                dk = jnp.exp(jnp.where(causal, csc - csr, NEG))
                mf = gmat * dk
                dtc = dt[:, h:h + 1]
                xd = xv * dtc
                xdb = xd.astype(bf16)
                ecs = jnp.exp(csc)
                dec = jnp.exp(cl)
                e = jnp.exp(cl - csc)
                hp = hs_ref[0, h]
                hpb = hp.astype(bf16)
                dhn = dh_ref[h]
                dhnb = dhn.astype(bf16)
                dd_h = jnp.sum(jnp.sum(dyv * xv, axis=1, keepdims=True), axis=0, keepdims=True)
                dx = dyv * dsk[:, h:h + 1]
                ch = _dot(cm, hpb, NT)
                dye = dyv * ecs
                dyeb = dye.astype(bf16)
                dcg = dcg + _dot(dyeb, hpb)
                dhp = _dot(dyeb, cm, TN)
                dcs_col = jnp.sum(dye * ch, axis=1, keepdims=True)
                dm = _dot(dyb, xdb, NT)
                dxd = _dot(mf.astype(bf16), dyb, TN)
                dgm = dgm + dm * dk
                wmat = dm * mf
                dcs_col = dcs_col + jnp.sum(wmat, axis=1, keepdims=True)
                dcs_row = -jnp.sum(wmat, axis=0, keepdims=True)
                ddec = jnp.sum(jnp.sum(hp * dhn, axis=1, keepdims=True), axis=0, keepdims=True)
                dxe = _dot(bm, dhnb, NT)
                dxd = dxd + dxe * e
                de_e = jnp.sum(dxe * xd, axis=1, keepdims=True) * e
                dbg = dbg + _dot((xd * e).astype(bf16), dhnb)
                dcs_col = dcs_col - de_e
                dlast = ddec * dec + jnp.sum(de_e, axis=0, keepdims=True)
                dh_ref[h] = dhp + dec * dhn
                dx = dx + dxd * dtc
                ddt_h = jnp.sum(dxd * xv, axis=1, keepdims=True)
                is_h = lane == h
                dcs_c = dcs_c + jnp.where(is_h, dcs_col, 0.0) + jnp.where(is_h & (rowl == l - 1), dlast, 0.0)
                dcs_r = dcs_r + jnp.where(sub == h, dcs_row, 0.0)
                ddt_c = ddt_c + jnp.where(is_h, ddt_h, 0.0)
                dd_row = dd_row + jnp.where(lane1 == h, dd_h, 0.0)
                dpre_ref[:, hsl] = (dx * _dsilu(pre_ref[:, hsl].astype(f32))).astype(bf16)
            dgb = dgm.astype(bf16)
            dcg = dcg + _dot(dgb, bm)
            dbg = dbg + _dot(dgb, cm, TN)
            dpre_ref[:, bsl] = (dbg * _dsilu(pre_ref[:, bsl].astype(f32))).astype(bf16)
            dpre_ref[:, csl] = (dcg * _dsilu(pre_ref[:, csl].astype(f32))).astype(bf16)
        dcs = dcs_c + dcs_r.T
        dda = _dot((row <= col).astype(f32), dcs, precision=HIGHEST)
        ddt = ddt_c + dda * a
        ddtraw = jnp.where(lane < N_HEADS, ddt * _sigmoid(zraw), 0.0)
        ddt_ref[...] = ddtraw
        dprm_ref[0:1, :] += jnp.sum(ddtraw, axis=0, keepdims=True)
        dprm_ref[1:2, :] += jnp.where(lane1 < N_HEADS, jnp.sum(dda * dt, axis=0, keepdims=True) * a, 0.0)
        dprm_ref[2:3, :] += dd_row

    def rev(i):
        return (nc - 1 - i, 0)

    return pl.pallas_call(
        body, name="ssd_bwd", grid=(nc,),
        in_specs=[pl.BlockSpec((l, XBC_W), rev), pl.BlockSpec((l, XBC_W), rev),
                  pl.BlockSpec((l, 128), rev), pl.BlockSpec((8, 128), lambda i: (0, 0)),
                  pl.BlockSpec((1, N_HEADS, HEAD_DIM, D_STATE), lambda i: (nc - 1 - i, 0, 0, 0)),
                  pl.BlockSpec((l, D_MODEL), rev)],
        out_specs=[pl.BlockSpec((l, XBC_W), rev), pl.BlockSpec((l, 128), rev), pl.BlockSpec((8, 128), lambda i: (0, 0))],
        out_shape=[jax.ShapeDtypeStruct((s, XBC_W), bf16), jax.ShapeDtypeStruct((s, 128), f32),
                   jax.ShapeDtypeStruct((8, 128), f32)],
        scratch_shapes=[pltpu.VMEM((N_HEADS, HEAD_DIM, D_STATE), f32)],
        compiler_params=_cp("arbitrary"))(act, pre, dtf, prm, hs, dy)


def _fox_cumsum(dtf, prm):
    s = dtf.shape[0]
    l = CHUNK

    def body(f_ref, prm_ref, c_ref, carry_ref):
        @pl.when(pl.program_id(0) == 0)
        def _():
            carry_ref[...] = jnp.zeros_like(carry_ref)
        lf = _log_sigmoid(f_ref[...] + prm_ref[3:4, :])
        row = lax.broadcasted_iota(jnp.int32, (l, l), 0)
        col = lax.broadcasted_iota(jnp.int32, (l, l), 1)
        c = _dot((row >= col).astype(f32), lf, precision=HIGHEST) + carry_ref[...]
        c_ref[...] = c
        carry_ref[...] = c[l - 1:l, :]

    return pl.pallas_call(
        body, name="fox_cumsum", grid=(s // l,),
        in_specs=[pl.BlockSpec((l, 128), lambda i: (i, 0)), pl.BlockSpec((8, 128), lambda i: (0, 0))],
        out_specs=pl.BlockSpec((l, 128), lambda i: (i, 0)),
        out_shape=jax.ShapeDtypeStruct((s, 128), f32),
        scratch_shapes=[pltpu.VMEM((1, 128), f32)],
        compiler_params=_cp("arbitrary"))(dtf, prm)


AUG = HEAD_DIM
N_PAIRS = N_HEADS // 2
V_BLOCK = 2 * D_MODEL // 128


def _split3(x):
    hi = x.astype(bf16)
    r1 = x - hi.astype(f32)
    mid = r1.astype(bf16)
    lo = (r1 - mid.astype(f32)).astype(bf16)
    return hi.astype(f32), mid.astype(f32), lo.astype(f32)


def _fox_prep(qkv, c):
    s = qkv.shape[0]
    ts = min(CONV_ROW_TILE, s)
    kb = D_MODEL // 128

    def body(q_ref, k_ref, c_ref, qa_ref, ka_ref):
        lane = lax.broadcasted_iota(jnp.int32, (ts, 128), 1)
        low = lane < HEAD_DIM
        for h in range(N_HEADS):
            psl = slice(128 * (h // 2), 128 * (h // 2 + 1))
            qv = q_ref[:, psl].astype(f32) * (HEAD_DIM ** -0.5)
            kv = k_ref[:, psl].astype(f32)
            if h % 2:
                qv = pltpu.roll(qv, HEAD_DIM, 1)
                kv = pltpu.roll(kv, HEAD_DIM, 1)
            hi, mid, lo = _split3(c_ref[:, F_LANE + h:F_LANE + h + 1])
            ones = jnp.where((lane >= AUG + 3) & (lane < AUG + 6), 1.0, 0.0)
            cq = jnp.where(lane == AUG, hi, jnp.where(lane == AUG + 1, mid, jnp.where(lane == AUG + 2, lo, ones)))
            qa_ref[h] = jnp.where(low, qv, cq).astype(bf16)
            onek = jnp.where((lane >= AUG) & (lane < AUG + 3), 1.0, 0.0)
            ck = jnp.where(lane == AUG + 3, -hi, jnp.where(lane == AUG + 4, -mid, jnp.where(lane == AUG + 5, -lo, onek)))
            ka_ref[h] = jnp.where(low, kv, ck).astype(bf16)

    hm = pl.BlockSpec((N_HEADS, ts, 128), lambda i: (0, i, 0))
    return pl.pallas_call(
        body, name="fox_prep", grid=(s // ts,),
        in_specs=[_rowspec(ts, D_MODEL, 0), _rowspec(ts, D_MODEL, 1), _rowspec(ts, 128)],
        out_specs=[hm, hm], out_shape=[jax.ShapeDtypeStruct((N_HEADS, s, 128), bf16)] * 2,
        compiler_params=_cp("parallel"))(qkv, qkv, c)


def _fox_fwd(qa, ka, qkv):
    s = qkv.shape[0]
    t = min(ATTN_TILE, s)
    nq = s // t

    def body(qa_ref, ka_ref, v_ref, o_ref, lse_ref):
        qi = pl.program_id(1)
        low = lax.broadcasted_iota(jnp.int32, (t, 128), 1) < HEAD_DIM
        row = lax.broadcasted_iota(jnp.int32, (t, t), 0)
        col = lax.broadcasted_iota(jnp.int32, (t, t), 1)

        def tile(ki, carry, diagonal):
            stats, acc = carry
            koff = pl.multiple_of(ki * t, t)
            v = v_ref[pl.ds(koff, t), :]
            vh = (jnp.where(low, v, jnp.zeros_like(v)), jnp.where(low, jnp.zeros_like(v), v))
            new_stats, alphas, pv = [], [], None
            for r in range(2):
                m_old, l_old = stats[r]
                sc = _dot(qa_ref[r], ka_ref[r, pl.ds(koff, t), :], NT)
                if diagonal:
                    sc = jnp.where(col <= row, sc, NEG)
                m_new = jnp.maximum(m_old, jnp.max(sc, axis=1, keepdims=True))
                p = jnp.exp(sc - m_new)
                alpha = jnp.exp(m_old - m_new)
                new_stats.append((m_new, alpha * l_old + jnp.sum(p, axis=1, keepdims=True)))
                alphas.append(alpha)
                d = _dot(p.astype(bf16), vh[r])
                pv = d if pv is None else pv + d
            acc = acc * jnp.where(low, alphas[0], alphas[1]) + pv
            return tuple(new_stats), acc

        init = (((jnp.full((t, 1), NEG, f32), jnp.zeros((t, 1), f32)),) * 2, jnp.zeros((t, 128), f32))
        carry = lax.fori_loop(0, qi, lambda ki, cr: tile(ki, cr, False), init)
        stats, acc = tile(qi, carry, True)
        o_ref[...] = (acc / jnp.where(low, stats[0][1], stats[1][1])).astype(bf16)
        for r in range(2):
            lse = stats[r][0] + jnp.log(stats[r][1])
            lse_ref[r] = jnp.broadcast_to(lse, (t, 128)).T[0:1, :]

    return pl.pallas_call(
        body, name="fox_fwd", grid=(N_PAIRS, nq),
        in_specs=[pl.BlockSpec((2, t, 128), lambda j, qi: (j, qi, 0)),
                  pl.BlockSpec((2, s, 128), lambda j, qi: (j, 0, 0)),
                  pl.BlockSpec((s, 128), lambda j, qi: (0, V_BLOCK + j))],
        out_specs=[pl.BlockSpec((t, 128), lambda j, qi: (qi, j)), pl.BlockSpec((2, 1, t), lambda j, qi: (j, 0, qi))],
        out_shape=[jax.ShapeDtypeStruct((s, D_MODEL), bf16), jax.ShapeDtypeStruct((N_HEADS, 1, s), f32)],
        compiler_params=_cp("parallel", "parallel"))(qa, ka, qkv)


def _fox_bwd(qa, ka, qkv, do, lse, delta):
    s = qkv.shape[0]
    t = min(ATTN_TILE, s)
    nq = s // t

    def body(qa_ref, ka_ref, v_ref, do_ref, lse_ref, dl_ref, dq_ref, dk_ref, dv_ref):
        ki = pl.program_id(1)

        @pl.when(ki == 0)
        def _():
            dq_ref[...] = jnp.zeros_like(dq_ref)
        low = lax.broadcasted_iota(jnp.int32, (t, 128), 1) < HEAD_DIM
        row = lax.broadcasted_iota(jnp.int32, (t, t), 0)
        col = lax.broadcasted_iota(jnp.int32, (t, t), 1)
        v = v_ref[...]
        zero = jnp.zeros_like(v)
        vh = (jnp.where(low, v, zero), jnp.where(low, zero, v))

        def tile(qi, carry, diagonal):
            dks, dv = carry
            qoff = pl.multiple_of(qi * t, t)
            dov = do_ref[pl.ds(qoff, t), :]
            doh = (jnp.where(low, dov, zero), jnp.where(low, zero, dov))
            new_dks = []
            for r in range(2):
                qt = qa_ref[r, pl.ds(qoff, t), :]
                sct = _dot(ka_ref[r], qt, NT)
                if diagonal:
                    sct = jnp.where(row <= col, sct, NEG)
                pt = jnp.exp(sct - lse_ref[r, :, pl.ds(qoff, t)])
                dpt = _dot(vh[r], dov, NT)
                dst = (pt * (dpt - dl_ref[r, :, pl.ds(qoff, t)])).astype(bf16)
                dv = dv + _dot(pt.astype(bf16), doh[r])
                new_dks.append(dks[r] + _dot(dst, qt))
                dq_ref[r, pl.ds(qoff, t), :] += _dot(dst, ka_ref[r], TN)
            return tuple(new_dks), dv

        zacc = jnp.zeros((t, 128), f32)
        carry = tile(ki, ((zacc, zacc), zacc), True)
        dks, dv = lax.fori_loop(ki + 1, nq, lambda qi, cr: tile(qi, cr, False), carry)
        dk_ref[0] = dks[0]
        dk_ref[1] = dks[1]
        dv_ref[...] = dv.astype(bf16)

    return pl.pallas_call(
        body, name="fox_bwd", grid=(N_PAIRS, nq),
        in_specs=[pl.BlockSpec((2, s, 128), lambda j, ki: (j, 0, 0)),
                  pl.BlockSpec((2, t, 128), lambda j, ki: (j, ki, 0)),
                  pl.BlockSpec((t, 128), lambda j, ki: (ki, V_BLOCK + j)),
                  pl.BlockSpec((s, 128), lambda j, ki: (0, j)),
                  pl.BlockSpec((2, 1, s), lambda j, ki: (j, 0, 0)),
                  pl.BlockSpec((2, 1, s), lambda j, ki: (j, 0, 0))],
        out_specs=[pl.BlockSpec((2, s, 128), lambda j, ki: (j, 0, 0)),
                   pl.BlockSpec((2, t, 128), lambda j, ki: (j, ki, 0)),
                   pl.BlockSpec((t, 128), lambda j, ki: (ki, j))],
        out_shape=[jax.ShapeDtypeStruct((N_HEADS, s, 128), f32), jax.ShapeDtypeStruct((N_HEADS, s, 128), f32),
                   jax.ShapeDtypeStruct((s, D_MODEL), bf16)],
        compiler_params=_cp("parallel", "arbitrary"))(qa, ka, qkv, do, lse, delta)


def _fox_bwd_post(dq_hm, dk_hm):
    s = dq_hm.shape[1]
    ts = min(CONV_ROW_TILE, s)

    def body(dq_ref, dk_ref, q_ref, k_ref, dc_ref):
        lane = lax.broadcasted_iota(jnp.int32, (ts, 128), 1)
        dc = jnp.zeros((ts, 128), f32)
        for h in range(N_HEADS):
            hsl = slice(HEAD_DIM * h, HEAD_DIM * (h + 1))
            dqv = dq_ref[h]
            dkv = dk_ref[h]
            q_ref[:, hsl] = (dqv[:, 0:HEAD_DIM] * (HEAD_DIM ** -0.5)).astype(bf16)
            k_ref[:, hsl] = dkv[:, 0:HEAD_DIM].astype(bf16)
            dc = dc + jnp.where(lane == F_LANE + h, dqv[:, AUG:AUG + 1] - dkv[:, AUG + 3:AUG + 4], 0.0)
        dc_ref[...] = dc

    hm = pl.BlockSpec((N_HEADS, ts, 128), lambda i: (0, i, 0))
    return pl.pallas_call(
        body, name="fox_bwd_post", grid=(s // ts,), in_specs=[hm, hm],
        out_specs=[_rowspec(ts, D_MODEL), _rowspec(ts, D_MODEL), _rowspec(ts, 128)],
        out_shape=[jax.ShapeDtypeStruct((s, D_MODEL), bf16), jax.ShapeDtypeStruct((s, D_MODEL), bf16),
                   jax.ShapeDtypeStruct((s, 128), f32)],
        compiler_params=_cp("parallel"))(dq_hm, dk_hm)


def _fox_gate_bwd(dc, dtf, prm, ddt_raw):
    s = dtf.shape[0]
    l = CHUNK
    nb = s // l

    def body(dc_ref, f_ref, prm_ref, ddt_ref, out_ref, dfb_ref, carry_ref):
        @pl.when(pl.program_id(0) == 0)
        def _():
            carry_ref[...] = jnp.zeros_like(carry_ref)
            dfb_ref[...] = jnp.zeros_like(dfb_ref)
        dc = dc_ref[...]
        row = lax.broadcasted_iota(jnp.int32, (l, l), 0)
        col = lax.broadcasted_iota(jnp.int32, (l, l), 1)
        dlf = _dot((row <= col).astype(f32), dc, precision=HIGHEST) + carry_ref[...]
        carry_ref[...] = dlf[0:1, :]
        lane = lax.broadcasted_iota(jnp.int32, (l, 128), 1)
        is_f = (lane >= F_LANE) & (lane < F_LANE + N_HEADS)
        dfr = jnp.where(is_f, dlf * _sigmoid(-(f_ref[...] + prm_ref[3:4, :])), 0.0)
        dfb_ref[...] += jnp.sum(dfr, axis=0, keepdims=True)
        out_ref[...] = ddt_ref[...] + dfr

    def rev(i):
        return (nb - 1 - i, 0)

    return pl.pallas_call(
        body, name="fox_gate_bwd", grid=(nb,),
        in_specs=[pl.BlockSpec((l, 128), rev), pl.BlockSpec((l, 128), rev), pl.BlockSpec((8, 128), lambda i: (0, 0)),
                  pl.BlockSpec((l, 128), rev)],
        out_specs=[pl.BlockSpec((l, 128), rev), pl.BlockSpec((1, 128), lambda i: (0, 0))],
        out_shape=[jax.ShapeDtypeStruct((s, 128), f32), jax.ShapeDtypeStruct((1, 128), f32)],
        scratch_shapes=[pltpu.VMEM((1, 128), f32)],
        compiler_params=_cp("arbitrary"))(dc, dtf, prm, ddt_raw)


def _position():
    return lax.axis_index("x"), lax.axis_index("y"), lax.axis_index("c")


def _all_gather(xl, name):
    r, c = xl.shape

    def body(x_ref, out_ref, send_sems, recv_sems, local_sem):
        x, y, cc = _position()
        me, sibling = (x, y, cc), (x, y, 1 - cc)
        chips = [(1 - x, y), (x, 1 - y), (1 - x, 1 - y)]

        def slot(px, py, pc):
            return out_ref.at[4 * px + 2 * py + pc]

        def copy(k, block, to, src=None):
            return pltpu.make_async_remote_copy(
                src_ref=slot(*block) if src is None else src, dst_ref=slot(*block),
                send_sem=send_sems.at[k], recv_sem=recv_sems.at[k],
                device_id=to, device_id_type=pl.DeviceIdType.MESH)

        mine = pltpu.make_async_copy(x_ref, slot(*me), local_sem)
        mine.start()
        first = [copy(0, me, sibling, src=x_ref)]
        first += [copy(1 + j, me, (*chip, cc), src=x_ref) for j, chip in enumerate(chips)]
        for cp in first:
            cp.start()
        passed = [copy(4 + j, (*chip, cc), sibling) for j, chip in enumerate(chips)]
        for j, chip in enumerate(chips):
            copy(1 + j, (*chip, cc), me).wait_recv()
            passed[j].start()
        copy(0, sibling, me).wait_recv()
        for j, chip in enumerate(chips):
            copy(4 + j, (*chip, 1 - cc), me).wait_recv()
        for cp in first + passed:
            cp.wait_send()
        mine.wait()

    return pl.pallas_call(
        body, name=name,
        out_shape=jax.ShapeDtypeStruct((N_DEV, r, c), xl.dtype),
        in_specs=[pl.BlockSpec(memory_space=pl.ANY)], out_specs=pl.BlockSpec(memory_space=pl.ANY),
        scratch_shapes=[pltpu.SemaphoreType.DMA((7,)), pltpu.SemaphoreType.DMA((7,)), pltpu.SemaphoreType.DMA],
    )(xl)


def _grad_exchange(gs):
    n = len(gs)

    def body(*refs):
        g_refs, r_refs = refs[:n], refs[n:2 * n]
        send_sems, recv_sems, local_sems = refs[2 * n:]
        x, y, cc = _position()
        me = 4 * x + 2 * y + cc
        local = [pltpu.make_async_copy(g_refs[a].at[me], r_refs[a].at[me], local_sems.at[a]) for a in range(n)]
        for cp in local:
            cp.start()
        sends, recvs = [], []
        for k in range(1, N_DEV):
            px = 1 - x if k & 4 else x
            py = 1 - y if k & 2 else y
            pc = 1 - cc if k & 1 else cc
            pid = 4 * px + 2 * py + pc
            for a in range(n):
                sends.append(pltpu.make_async_remote_copy(
                    src_ref=g_refs[a].at[pid], dst_ref=r_refs[a].at[me],
                    send_sem=send_sems.at[a, k - 1], recv_sem=recv_sems.at[a, k - 1],
                    device_id=(px, py, pc), device_id_type=pl.DeviceIdType.MESH))
                recvs.append(pltpu.make_async_remote_copy(
                    src_ref=g_refs[a].at[pid], dst_ref=r_refs[a].at[pid],
                    send_sem=send_sems.at[a, k - 1], recv_sem=recv_sems.at[a, k - 1],
                    device_id=(px, py, pc), device_id_type=pl.DeviceIdType.MESH))
        for cp in sends:
            cp.start()
        for cp in recvs:
            cp.wait_recv()
        for cp in sends:
            cp.wait_send()
        for cp in local:
            cp.wait()

    anyspec = pl.BlockSpec(memory_space=pl.ANY)
    return pl.pallas_call(
        body, name="grad_exchange",
        out_shape=[jax.ShapeDtypeStruct(g.shape, g.dtype) for g in gs],
        in_specs=[anyspec] * n, out_specs=[anyspec] * n,
        scratch_shapes=[pltpu.SemaphoreType.DMA((n, N_DEV - 1)), pltpu.SemaphoreType.DMA((n, N_DEV - 1)),
                        pltpu.SemaphoreType.DMA((n,))],
    )(*gs)


def _sum_parts(parts, name):
    n, r, c = parts.shape

    def body(p_ref, o_ref):
        g = p_ref[0]
        for i in range(1, n):
            g = g + p_ref[i]
        o_ref[...] = g

    return pl.pallas_call(body, name=name, out_shape=jax.ShapeDtypeStruct((r, c), f32))(parts)


def _adamw(w, m, v, parts, name, tr=128):
    r, c = w.shape
    n = parts.shape[0]
    tr = min(tr, r)
    c1 = 1.0 - ADAM_B1 ** ADAM_STEP
    c2 = 1.0 - ADAM_B2 ** ADAM_STEP

    def body(w_ref, m_ref, v_ref, p_ref, g_ref, d_ref, nm_ref, nv_ref):
        g = p_ref[0].astype(f32)
        for i in range(1, n):
            g = g + p_ref[i].astype(f32)
        g_ref[...] = g
        nm = ADAM_B1 * m_ref[...] + (1.0 - ADAM_B1) * g
        nv = ADAM_B2 * v_ref[...] + (1.0 - ADAM_B2) * (g * g)
        nm_ref[...] = nm
        nv_ref[...] = nv
        d_ref[...] = -ADAM_LR * ((nm / c1) / (jnp.sqrt(nv / c2) + ADAM_EPS) + ADAM_WD * w_ref[...])

    blk = pl.BlockSpec((tr, c), lambda i: (i, 0))
    return pl.pallas_call(
        body, name=name, grid=(r // tr,),
        in_specs=[blk, blk, blk, pl.BlockSpec((n, tr, c), lambda i: (0, i, 0))],
        out_specs=[blk] * 4, out_shape=[jax.ShapeDtypeStruct((r, c), f32)] * 4,
        compiler_params=_cp("parallel"))(w, m, v, parts)


def _lanes(w):
    return -(-w // 128) * 128


def _pack(arrs):
    rows = []
    for a in arrs:
        k, w = a.shape
        if w % 128:
            a = jnp.pad(a, ((0, 0), (0, _lanes(w) - w)))
        rows.append(a.reshape(-1, 128))
    out = jnp.concatenate(rows, axis=0)
    pad = -out.shape[0] % 8
    return jnp.pad(out, ((0, pad), (0, 0))) if pad else out


def _unpack(packed, shapes):
    outs, off = [], 0
    lead = packed.shape[:-2]
    for k, w in shapes:
        nrow = k * _lanes(w) // 128
        a = packed[..., off:off + nrow, :].reshape(*lead, k, _lanes(w))[..., :w]
        outs.append(a)
        off += nrow
    return outs


def _gathered_cols(a):
    n, k, wl = a.shape
    return jnp.transpose(a, (1, 0, 2)).reshape(k, n * wl)


def _col_shards(a):
    k, w = a.shape
    return jnp.transpose(a.reshape(k, N_DEV, w // N_DEV), (1, 0, 2))


SMALL_PARAMS = (
    ("e_norm_pre", 1, 1024, False), ("e_conv_w", 4, 2048, True), ("e_conv_b", 1, 2048, False),
    ("e_dt_bias", 1, 16, False), ("e_a_log", 1, 16, False), ("e_d_skip", 1, 16, False), ("e_fgate_b", 1, 16, False),
    ("e_ssd_norm", 1, 1024, False), ("e_norm_post", 1, 1024, False), ("o_norm_pre", 1, 1024, True),
    ("o_conv_w", 31, 2048, True), ("o_conv_b", 1, 2048, True), ("o_ln_g", 1, 2048, True), ("o_ln_b", 1, 2048, True),
    ("o_norm_post", 1, 1024, True),
)
BIG_PARAMS = ("e_w_in", "e_w_out", "o_w_in", "o_w_out")
WEIGHT_ORDER = ("e_norm_pre", "e_w_in", "e_conv_w", "e_conv_b", "e_dt_bias", "e_a_log", "e_d_skip", "e_fgate_b",
                "e_ssd_norm", "e_w_out", "e_norm_post", "o_norm_pre", "o_w_in", "o_conv_w", "o_conv_b", "o_ln_g",
                "o_ln_b", "o_w_out", "o_norm_post")
E_IN = 7200
O_IN = 6144


def kernel(x, e_norm_pre, e_w_in, e_conv_w, e_conv_b, e_dt_bias, e_a_log, e_d_skip, e_fgate_b, e_ssd_norm, e_w_out, e_norm_post, o_norm_pre, o_w_in, o_conv_w, o_conv_b, o_ln_g, o_ln_b, o_w_out, o_norm_post, loss_target, m_e_norm_pre, m_e_w_in, m_e_conv_w, m_e_conv_b, m_e_dt_bias, m_e_a_log, m_e_d_skip, m_e_fgate_b, m_e_ssd_norm, m_e_w_out, m_e_norm_post, m_o_norm_pre, m_o_w_in, m_o_conv_w, m_o_conv_b, m_o_ln_g, m_o_ln_b, m_o_w_out, m_o_norm_post, v_e_norm_pre, v_e_w_in, v_e_conv_w, v_e_conv_b, v_e_dt_bias, v_e_a_log, v_e_d_skip, v_e_fgate_b, v_e_ssd_norm, v_e_w_out, v_e_norm_post, v_o_norm_pre, v_o_w_in, v_o_conv_w, v_o_conv_b, v_o_ln_g, v_o_ln_b, v_o_w_out, v_o_norm_post):
    given = dict(locals())
    w_in = {n: given[n] for n in WEIGHT_ORDER}
    m_in = {n: given["m_" + n] for n in WEIGHT_ORDER}
    v_in = {n: given["v_" + n] for n in WEIGHT_ORDER}

    def mat(a):
        return a.reshape(a.shape[-2:])

    xs = mat(x)
    tgt = mat(loss_target)
    xi, yi, ci = _position()
    me = 4 * xi + 2 * yi + ci
    ew, ow = E_IN // N_DEV, O_IN // N_DEV
    wr = D_CONV // N_DEV

    big_local = jnp.concatenate([
        mat(e_w_in).astype(bf16).reshape(ew, D_MODEL), mat(o_w_in).astype(bf16).reshape(ow, D_MODEL),
        mat(e_w_out).astype(bf16), mat(o_w_out).astype(bf16)], axis=0)
    nbig = big_local.shape[0]
    big_local = jnp.pad(big_local, ((0, -nbig % 16), (0, 0)))
    wg = _all_gather(big_local, "gather_weights")
    o0, o1, o2, o3 = 0, ew, ew + ow, ew + ow + wr
    e_w_in_f = _gathered_cols(wg[:, o0:o1].reshape(N_DEV, D_MODEL, ew))
    o_w_in_f = _gathered_cols(wg[:, o1:o2].reshape(N_DEV, D_MODEL, ow))
    e_w_out_f = wg[:, o2:o3].reshape(D_CONV, D_MODEL)
    o_w_out_f = wg[:, o3:o3 + wr].reshape(D_CONV, D_MODEL)
    w_z, w_xbc = e_w_in_f[:, 0:2048], e_w_in_f[:, 2048:4096]
    w_qkv = e_w_in_f[:, 4112:7184]
    w_dtf = jnp.concatenate([e_w_in_f[:, 4096:4112], e_w_in_f[:, 7184:7200], jnp.zeros((D_MODEL, 96), bf16)], axis=1)

    sharded_small = [(n, k, w) for n, k, w, sh in SMALL_PARAMS if sh]
    sg = _all_gather(_pack([mat(w_in[n]) for n, _, _ in sharded_small]), "gather_small_weights")
    full_small = {n: _gathered_cols(a)
                  for (n, _, _), a in zip(sharded_small, _unpack(sg, [(k, w // N_DEV) for _, k, w in sharded_small]))}
    for n, _, _, sh in SMALL_PARAMS:
        if not sh:
            full_small[n] = mat(w_in[n])
    p = full_small
    prm = jnp.zeros((8, 128), f32)
    prm = prm.at[0, 0:16].set(p["e_dt_bias"][0]).at[1, 0:16].set(p["e_a_log"][0]).at[2, 0:16].set(p["e_d_skip"][0])
    prm = prm.at[3, F_LANE:F_LANE + 16].set(p["e_fgate_b"][0])

    u0 = _rms_fwd(xs, p["e_norm_pre"], "rms_pre0")
    z0 = _mm_nn(u0, w_z, bf16, "proj0_z")
    xraw = _mm_nn(u0, w_xbc, bf16, "proj0_xbc")
    qkv = _mm_nn(u0, w_qkv, bf16, "proj0_qkv")
    dtf = _mm_nn(u0, w_dtf, f32, "proj0_dtf")
    pre, act = _conv_ssd_fwd(xraw, p["e_conv_w"], p["e_conv_b"])
    y, hs = _ssd_fwd(act, dtf, prm)
    qa, ka = _fox_prep(qkv, _fox_cumsum(dtf, prm))
    o, lse = _fox_fwd(qa, ka, qkv)
    cat = _gate0_fwd(y, z0, o, p["e_ssd_norm"])
    out0 = _mm_nn(cat, e_w_out_f, f32, "out0")
    x1, u1 = _post0_pre1(xs, out0, p["e_norm_post"], p["o_norm_pre"])

    proj1 = _mm_nn(u1, o_w_in_f, bf16, "proj1")
    hc = _conv_glu_fwd(proj1, p["o_conv_w"], p["o_conv_b"])
    h3 = _ln_gate_fwd(hc, proj1, p["o_ln_g"], p["o_ln_b"])
    out1 = _mm_nn(h3, o_w_out_f, f32, "out1")
    dy, d_out1, dg_post1, loss_part = _final_loss(x1, out1, tgt, p["o_norm_post"])

    dh3 = _mm_nt([(d_out1, 0, o_w_out_f, 0, D_MODEL)], bf16, "dh3")
    g_o_w_out = _mm_tn(h3, d_out1, "dw_out1")
    dhc, dz1, dg_ln, db_ln = _ln_gate_bwd(hc, proj1, dh3, p["o_ln_g"], p["o_ln_b"])
    dval, dgate, dw_conv1, db_conv1 = _conv_glu_bwd(dhc, proj1, p["o_conv_w"])
    dproj1 = jnp.concatenate([dval, dgate, dz1], axis=1)
    du1 = _mm_nt([(dproj1, 0, o_w_in_f, 0, O_IN)], f32, "du1")
    g_o_w_in = _mm_tn(u1, dproj1, "dw_in1", tn=ow, blocked=True)
    dx1, d_out0, dg_pre1, dg_post0 = _mid_bwd(x1, du1, dy, out0, p["o_norm_pre"], p["e_norm_post"])

    dcat = _mm_nt([(d_out0, 0, e_w_out_f, 0, D_MODEL)], bf16, "dcat")
    g_e_w_out = _mm_tn(cat, d_out0, "dw_out0")
    dy_ssd, do, dz0, delta, dg_ssd_norm = _gate0_bwd(y, z0, o, dcat, p["e_ssd_norm"])
    dq_hm, dk_hm, dv = _fox_bwd(qa, ka, qkv, do, lse, delta[0:N_HEADS].reshape(N_HEADS, 1, -1))
    dq, dk, dc = _fox_bwd_post(dq_hm, dk_hm)
    dpre, ddt_raw, dprm = _ssd_bwd(act, pre, dtf, prm, hs, dy_ssd)
    ddtf, dfb = _fox_gate_bwd(dc, dtf, prm, ddt_raw)
    dxraw, dw_conv0, db_conv0 = _conv_ssd_bwd(dpre, xraw, p["e_conv_w"])
    du0 = _mm_nt([(dz0, 0, w_z, 0, 2048), (dxraw, 0, w_xbc, 0, 2048), (dq, 0, w_qkv, 0, 1024), (dk, 0, w_qkv, 1, 1024),
                  (dv, 0, w_qkv, 2, 1024), (ddtf, 0, w_dtf, 0, 128)], f32, "du0")
    gw_dtf = _mm_tn(u0, ddtf, "dw_in0_dtf")
    g_e_w_in_full = jnp.concatenate([
        _mm_tn(u0, dz0, "dw_in0_z"), _mm_tn(u0, dxraw, "dw_in0_xbc"), gw_dtf[:, 0:16],
        _mm_tn(u0, dq, "dw_in0_q"), _mm_tn(u0, dk, "dw_in0_k"), _mm_tn(u0, dv, "dw_in0_v"), gw_dtf[:, 16:32]], axis=1)
    grad_x, dg_pre0 = _first_bwd(xs, du0, dx1, p["e_norm_pre"])

    big_parts = _grad_exchange([
        _col_shards(g_e_w_in_full).astype(bf16), g_e_w_out.reshape(N_DEV, wr, D_MODEL).astype(bf16),
        g_o_w_in.astype(bf16), g_o_w_out.reshape(N_DEV, wr, D_MODEL).astype(bf16)])
    outs = {}
    for n, parts in zip(BIG_PARAMS, big_parts):
        outs[n] = _adamw(mat(w_in[n]), mat(m_in[n]), mat(v_in[n]), parts, "adamw_" + n)

    small_grads = {
        "e_norm_pre": dg_pre0, "e_conv_w": dw_conv0, "e_conv_b": db_conv0, "e_dt_bias": dprm[0:1, 0:16],
        "e_a_log": dprm[1:2, 0:16], "e_d_skip": dprm[2:3, 0:16], "e_fgate_b": dfb[:, F_LANE:F_LANE + 16],
        "e_ssd_norm": dg_ssd_norm, "e_norm_post": dg_post0, "o_norm_pre": dg_pre1, "o_conv_w": dw_conv1,
        "o_conv_b": db_conv1, "o_ln_g": dg_ln, "o_ln_b": db_ln, "o_norm_post": dg_post1,
    }
    gathered = _all_gather(_pack([small_grads[n] for n, _, _, _ in SMALL_PARAMS] + [loss_part]), "gather_small_grads")
    summed = _unpack(_sum_parts(gathered, "sum_small_grads"), [(k, w) for _, k, w, _ in SMALL_PARAMS] + [(1, 128)])
    loss = summed[-1][0, 0]
    g_local = []
    for (n, k, w, sh), g in zip(SMALL_PARAMS, summed):
        g_local.append(lax.dynamic_slice_in_dim(g, me * (w // N_DEV), w // N_DEV, axis=1) if sh else g)
    names = [n for n, _, _, _ in SMALL_PARAMS]
    local_shapes = [(k, w // N_DEV if sh else w) for _, k, w, sh in SMALL_PARAMS]
    res = _adamw(_pack([mat(w_in[n]) for n in names]), _pack([mat(m_in[n]) for n in names]),
                 _pack([mat(v_in[n]) for n in names]), _pack(g_local)[None], "adamw_small", tr=8)
    unpacked = [_unpack(r, local_shapes) for r in res]
    for i, n in enumerate(names):
        outs[n] = tuple(u[i] for u in unpacked)

    ret = [loss, grad_x.reshape(x.shape)]
    for j in range(4):
        ret += [outs[n][j].reshape(w_in[n].shape) for n in WEIGHT_ORDER]
    return tuple(ret)
```

```python
import jax
import jax.numpy as jnp
from jax import lax
from jax.experimental import pallas as pl
from jax.experimental.pallas import tpu as pltpu

f32 = jnp.float32
bf16 = jnp.bfloat16

N_DEV = 8
D_MODEL = 1024
N_HEADS = 16
HEAD_DIM = 64
N_GROUPS = 4
HEADS_PER_GROUP = 4
D_STATE = 128
CHUNK = 128
SSD_CONV = 4
CONV_WIDTH = 31
D_CONV = 2048
EPS = 1e-6
XBC_W = 2048
B_OFF = 1024
C_OFF = 1536
F_LANE = 16
HALO = 32

ADAM_LR = 0.001
ADAM_B1 = 0.9
ADAM_B2 = 0.999
ADAM_EPS = 1e-08
ADAM_WD = 0.01
ADAM_STEP = 10

VMEM_LIMIT_BYTES = 56 * 1024 * 1024
ROW_TILE = 512
CONV_ROW_TILE = 256
CONV_COL_TILE = 512
CONV_SUB = 32
ATTN_TILE = 512
ATTN_FWD_TILE = 1024

NT = (((1,), (1,)), ((), ()))
TN = (((0,), (0,)), ((), ()))
HIGHEST = lax.Precision.HIGHEST
NEG = -1e30


def _cp(*sem):
    return pltpu.CompilerParams(dimension_semantics=sem if sem else None, vmem_limit_bytes=VMEM_LIMIT_BYTES)


def _sigmoid(x):
    return jax.nn.sigmoid(x)


def _silu(x):
    return x * _sigmoid(x)


def _dsilu(x):
    s = _sigmoid(x)
    return s * (1.0 + x * (1.0 - s))


def _softplus(x):
    return jnp.maximum(x, 0.0) + jnp.log(1.0 + jnp.exp(-jnp.abs(x)))


def _log_sigmoid(x):
    return jnp.minimum(x, 0.0) - jnp.log(1.0 + jnp.exp(-jnp.abs(x)))


def _dot(a, b, dims=None, precision=None):
    if dims is None:
        return jnp.dot(a, b, preferred_element_type=f32, precision=precision)
    return lax.dot_general(a, b, dims, preferred_element_type=f32, precision=precision)


def _mm_nn(a, b, out_dtype, name, tm=512, tn=1024):
    m, k = a.shape
    n = b.shape[1]
    tm, tn = min(tm, m), min(tn, n)

    def body(a_ref, b_ref, o_ref):
        o_ref[...] = _dot(a_ref[...], b_ref[...]).astype(o_ref.dtype)

    return pl.pallas_call(
        body, name=name, grid=(n // tn, m // tm),
        in_specs=[pl.BlockSpec((tm, k), lambda j, i: (i, 0)), pl.BlockSpec((k, tn), lambda j, i: (0, j))],
        out_specs=pl.BlockSpec((tm, tn), lambda j, i: (i, j)),
        out_shape=jax.ShapeDtypeStruct((m, n), out_dtype), compiler_params=_cp("parallel", "parallel"))(a, b)


def _mm_nt(pairs, out_dtype, name, tm=512, tn=512):
    m = pairs[0][0].shape[0]
    n = pairs[0][2].shape[0]
    tm, tn = min(tm, m), min(tn, n)
    npair = len(pairs)

    def body(*refs):
        o_ref = refs[-1]
        acc = None
        for p in range(npair):
            d = _dot(refs[2 * p][...].astype(bf16), refs[2 * p + 1][...], NT)
            acc = d if acc is None else acc + d
        o_ref[...] = acc.astype(o_ref.dtype)

    in_specs, args = [], []
    for a, acb, b, bcb, k in pairs:
        in_specs.append(pl.BlockSpec((tm, k), lambda j, i, acb=acb: (i, acb)))
        in_specs.append(pl.BlockSpec((tn, k), lambda j, i, bcb=bcb: (j, bcb)))
        args += [a, b]
    return pl.pallas_call(
        body, name=name, grid=(n // tn, m // tm), in_specs=in_specs,
        out_specs=pl.BlockSpec((tm, tn), lambda j, i: (i, j)),
        out_shape=jax.ShapeDtypeStruct((m, n), out_dtype), compiler_params=_cp("parallel", "parallel"))(*args)


def _mm_tn(a, b, name, a_cb=0, am=None, b_cb=0, bn=None, tn=1024, tk=512, blocked=False):
    k = a.shape[0]
    am = a.shape[1] if am is None else am
    bn = b.shape[1] if bn is None else bn
    tm = min(1024, am)
    tn, tk = min(tn, bn), min(tk, k)
    a_off, b_off = a_cb * (am // tm), b_cb * (bn // tn)

    def body(a_ref, b_ref, o_ref):
        @pl.when(pl.program_id(2) == 0)
        def _():
            o_ref[...] = jnp.zeros_like(o_ref)
        d = _dot(a_ref[...].astype(bf16), b_ref[...].astype(bf16), TN)
        o_ref[...] += d.reshape(o_ref.shape)

    if blocked:
        out_spec = pl.BlockSpec((1, tm, tn), lambda i, j, kk: (j, i, 0))
        out_shape = jax.ShapeDtypeStruct((bn // tn, am, tn), f32)
    else:
        out_spec = pl.BlockSpec((tm, tn), lambda i, j, kk: (i, j))
        out_shape = jax.ShapeDtypeStruct((am, bn), f32)
    return pl.pallas_call(
        body, name=name, grid=(am // tm, bn // tn, k // tk),
        in_specs=[pl.BlockSpec((tk, tm), lambda i, j, kk: (kk, a_off + i)),
                  pl.BlockSpec((tk, tn), lambda i, j, kk: (kk, b_off + j))],
        out_specs=out_spec, out_shape=out_shape,
        compiler_params=_cp("parallel", "parallel", "arbitrary"))(a, b)


def _rowspec(ts, w, cb=0):
    return pl.BlockSpec((ts, w), lambda i: (i, cb))


def _vecspec(w):
    return pl.BlockSpec((1, w), lambda i: (0, 0))


def _rms_fwd(x, g, name):
    s, d = x.shape
    ts = min(ROW_TILE, s)

    def body(x_ref, g_ref, u_ref):
        xv = x_ref[...]
        r = lax.rsqrt(jnp.mean(xv * xv, axis=-1, keepdims=True) + EPS)
        u_ref[...] = (xv * r * g_ref[...]).astype(bf16)

    return pl.pallas_call(
        body, name=name, grid=(s // ts,), in_specs=[_rowspec(ts, d), _vecspec(d)], out_specs=_rowspec(ts, d),
        out_shape=jax.ShapeDtypeStruct((s, d), bf16), compiler_params=_cp("parallel"))(x, g)


def _rms_bwd_vals(xv, g, dy):
    r = lax.rsqrt(jnp.mean(xv * xv, axis=-1, keepdims=True) + EPS)
    xh = xv * r
    dg = jnp.sum(dy * xh, axis=0, keepdims=True)
    dxh = dy * g
    dx = r * (dxh - xh * jnp.mean(dxh * xh, axis=-1, keepdims=True))
    return dx, dg


def _gate0_fwd(y, z, o, ssd_norm):
    s = y.shape[0]
    ts = min(ROW_TILE, s)
    gw = D_MODEL // N_GROUPS

    def body(y_ref, zs_ref, zf_ref, o_ref, w_ref, cat_ref):
        yg = y_ref[...].astype(f32) * _silu(zs_ref[...].astype(f32))
        for g in range(N_GROUPS):
            seg = yg[:, gw * g:gw * (g + 1)]
            r = lax.rsqrt(jnp.mean(seg * seg, axis=-1, keepdims=True) + EPS)
            cat_ref[:, gw * g:gw * (g + 1)] = (seg * r * w_ref[:, gw * g:gw * (g + 1)]).astype(bf16)
        cat_ref[:, D_MODEL:] = (o_ref[...].astype(f32) * _silu(zf_ref[...].astype(f32))).astype(bf16)

    return pl.pallas_call(
        body, name="gate0_fwd", grid=(s // ts,),
        in_specs=[_rowspec(ts, D_MODEL), _rowspec(ts, D_MODEL, 0), _rowspec(ts, D_MODEL, 1), _rowspec(ts, D_MODEL),
                  _vecspec(D_MODEL)],
        out_specs=_rowspec(ts, 2 * D_MODEL),
        out_shape=jax.ShapeDtypeStruct((s, 2 * D_MODEL), bf16), compiler_params=_cp("parallel"))(y, z, z, o, ssd_norm)


def _post0_pre1(x, out0, g_post0, g_pre1):
    s, d = x.shape
    ts = min(ROW_TILE, s)

    def body(x_ref, o_ref, gp_ref, gn_ref, x1_ref, u1_ref):
        ov = o_ref[...]
        r = lax.rsqrt(jnp.mean(ov * ov, axis=-1, keepdims=True) + EPS)
        x1 = x_ref[...] + ov * r * gp_ref[...]
        x1_ref[...] = x1
        r1 = lax.rsqrt(jnp.mean(x1 * x1, axis=-1, keepdims=True) + EPS)
        u1_ref[...] = (x1 * r1 * gn_ref[...]).astype(bf16)

    return pl.pallas_call(
        body, name="post0_pre1", grid=(s // ts,),
        in_specs=[_rowspec(ts, d), _rowspec(ts, d), _vecspec(d), _vecspec(d)],
        out_specs=[_rowspec(ts, d), _rowspec(ts, d)],
        out_shape=[jax.ShapeDtypeStruct((s, d), f32), jax.ShapeDtypeStruct((s, d), bf16)],
        compiler_params=_cp("parallel"))(x, out0, g_post0, g_pre1)


def _ln_vals(hc, g, b):
    mu = jnp.mean(hc, axis=-1, keepdims=True)
    xc = hc - mu
    rstd = lax.rsqrt(jnp.mean(xc * xc, axis=-1, keepdims=True) + EPS)
    xh = xc * rstd
    return xh, rstd, xh * g + b


def _ln_gate_fwd(hc, proj1, ln_g, ln_b):
    s = hc.shape[0]
    ts = min(ROW_TILE, s)

    def body(hc_ref, z_ref, g_ref, b_ref, h3_ref):
        _, _, ln = _ln_vals(hc_ref[...].astype(f32), g_ref[...], b_ref[...])
        h3_ref[...] = (_silu(ln) * _silu(z_ref[...].astype(f32))).astype(bf16)

    return pl.pallas_call(
        body, name="ln_gate_fwd", grid=(s // ts,),
        in_specs=[_rowspec(ts, D_CONV), _rowspec(ts, D_CONV, 2), _vecspec(D_CONV), _vecspec(D_CONV)],
        out_specs=_rowspec(ts, D_CONV),
        out_shape=jax.ShapeDtypeStruct((s, D_CONV), bf16), compiler_params=_cp("parallel"))(hc, proj1, ln_g, ln_b)


def _final_loss(x1, out1, tgt, g_post1):
    s, d = x1.shape
    ts = min(ROW_TILE, s)

    def body(x1_ref, o_ref, t_ref, g_ref, dy_ref, do_ref, dg_ref, loss_ref):
        i = pl.program_id(0)

        @pl.when(i == 0)
        def _():
            dg_ref[...] = jnp.zeros_like(dg_ref)
            loss_ref[...] = jnp.zeros_like(loss_ref)
        ov = o_ref[...]
        g = g_ref[...]
        r = lax.rsqrt(jnp.mean(ov * ov, axis=-1, keepdims=True) + EPS)
        diff = x1_ref[...] + ov * r * g - t_ref[...]
        row = jnp.mean(diff * diff, axis=-1, keepdims=True)
        loss_ref[...] += jnp.broadcast_to(0.5 * jnp.sum(row, axis=0, keepdims=True), loss_ref.shape)
        dy = diff * (1.0 / d)
        dy_ref[...] = dy
        dx, dg = _rms_bwd_vals(ov, g, dy)
        do_ref[...] = dx.astype(bf16)
        dg_ref[...] += dg

    return pl.pallas_call(
        body, name="final_loss", grid=(s // ts,),
        in_specs=[_rowspec(ts, d), _rowspec(ts, d), _rowspec(ts, d), _vecspec(d)],
        out_specs=[_rowspec(ts, d), _rowspec(ts, d), _vecspec(d), _vecspec(128)],
        out_shape=[jax.ShapeDtypeStruct((s, d), f32), jax.ShapeDtypeStruct((s, d), bf16),
                   jax.ShapeDtypeStruct((1, d), f32), jax.ShapeDtypeStruct((1, 128), f32)],
        compiler_params=_cp("arbitrary"))(x1, out1, tgt, g_post1)


def _ln_gate_bwd(hc, proj1, dh3, ln_g, ln_b):
    s = hc.shape[0]
    ts = min(ROW_TILE, s)

    def body(hc_ref, z_ref, dh_ref, g_ref, b_ref, dhc_ref, dz_ref, dg_ref, db_ref):
        @pl.when(pl.program_id(0) == 0)
        def _():
            dg_ref[...] = jnp.zeros_like(dg_ref)
            db_ref[...] = jnp.zeros_like(db_ref)
        g = g_ref[...]
        xh, rstd, ln = _ln_vals(hc_ref[...].astype(f32), g, b_ref[...])
        zv = z_ref[...].astype(f32)
        dh3 = dh_ref[...].astype(f32)
        dz_ref[...] = (dh3 * _silu(ln) * _dsilu(zv)).astype(bf16)
        dln = dh3 * _silu(zv) * _dsilu(ln)
        dg_ref[...] += jnp.sum(dln * xh, axis=0, keepdims=True)
        db_ref[...] += jnp.sum(dln, axis=0, keepdims=True)
        dxh = dln * g
        dhc = rstd * (dxh - jnp.mean(dxh, axis=-1, keepdims=True) - xh * jnp.mean(dxh * xh, axis=-1, keepdims=True))
        dhc_ref[...] = dhc.astype(bf16)

    return pl.pallas_call(
        body, name="ln_gate_bwd", grid=(s // ts,),
        in_specs=[_rowspec(ts, D_CONV), _rowspec(ts, D_CONV, 2), _rowspec(ts, D_CONV), _vecspec(D_CONV),
                  _vecspec(D_CONV)],
        out_specs=[_rowspec(ts, D_CONV), _rowspec(ts, D_CONV), _vecspec(D_CONV), _vecspec(D_CONV)],
        out_shape=[jax.ShapeDtypeStruct((s, D_CONV), bf16), jax.ShapeDtypeStruct((s, D_CONV), bf16),
                   jax.ShapeDtypeStruct((1, D_CONV), f32), jax.ShapeDtypeStruct((1, D_CONV), f32)],
        compiler_params=_cp("arbitrary"))(hc, proj1, dh3, ln_g, ln_b)


def _mid_bwd(x1, du1, dy, out0, g_pre1, g_post0):
    s, d = x1.shape
    ts = min(ROW_TILE, s)

    def body(x1_ref, du_ref, dy_ref, o_ref, gn_ref, gp_ref, dx1_ref, do_ref, dgn_ref, dgp_ref):
        @pl.when(pl.program_id(0) == 0)
        def _():
            dgn_ref[...] = jnp.zeros_like(dgn_ref)
            dgp_ref[...] = jnp.zeros_like(dgp_ref)
        dxa, dgn = _rms_bwd_vals(x1_ref[...], gn_ref[...], du_ref[...])
        dx1 = dy_ref[...] + dxa
        dx1_ref[...] = dx1
        dgn_ref[...] += dgn
        dxo, dgp = _rms_bwd_vals(o_ref[...], gp_ref[...], dx1)
        do_ref[...] = dxo.astype(bf16)
        dgp_ref[...] += dgp

    return pl.pallas_call(
        body, name="mid_bwd", grid=(s // ts,),
        in_specs=[_rowspec(ts, d)] * 4 + [_vecspec(d), _vecspec(d)],
        out_specs=[_rowspec(ts, d), _rowspec(ts, d), _vecspec(d), _vecspec(d)],
        out_shape=[jax.ShapeDtypeStruct((s, d), f32), jax.ShapeDtypeStruct((s, d), bf16),
                   jax.ShapeDtypeStruct((1, d), f32), jax.ShapeDtypeStruct((1, d), f32)],
        compiler_params=_cp("arbitrary"))(x1, du1, dy, out0, g_pre1, g_post0)


def _first_bwd(x, du0, dx1, g_pre0):
    s, d = x.shape
    ts = min(ROW_TILE, s)

    def body(x_ref, du_ref, dx1_ref, g_ref, dx_ref, dg_ref):
        @pl.when(pl.program_id(0) == 0)
        def _():
            dg_ref[...] = jnp.zeros_like(dg_ref)
        dxa, dg = _rms_bwd_vals(x_ref[...], g_ref[...], du_ref[...])
        dx_ref[...] = dx1_ref[...] + dxa
        dg_ref[...] += dg

    return pl.pallas_call(
        body, name="first_bwd", grid=(s // ts,),
        in_specs=[_rowspec(ts, d)] * 3 + [_vecspec(d)],
        out_specs=[_rowspec(ts, d), _vecspec(d)],
        out_shape=[jax.ShapeDtypeStruct((s, d), f32), jax.ShapeDtypeStruct((1, d), f32)],
        compiler_params=_cp("arbitrary"))(x, du0, dx1, g_pre0)


def _gate0_bwd(y, z, o, dcat, ssd_norm):
    s = y.shape[0]
    ts = min(ROW_TILE, s)
    gw = D_MODEL // N_GROUPS

    def body(y_ref, zs_ref, zf_ref, o_ref, dn_ref, dg_ref, w_ref, dy_ref, do_ref, dz_ref, delta_ref, dw_ref):
        @pl.when(pl.program_id(0) == 0)
        def _():
            dw_ref[...] = jnp.zeros_like(dw_ref)
        yv = y_ref[...].astype(f32)
        zs = zs_ref[...].astype(f32)
        sz = _silu(zs)
        yg = yv * sz
        dyn = dn_ref[...].astype(f32)
        for g in range(N_GROUPS):
            sl = slice(gw * g, gw * (g + 1))
            seg = yg[:, sl]
            r = lax.rsqrt(jnp.mean(seg * seg, axis=-1, keepdims=True) + EPS)
            yh = seg * r
            dn = dyn[:, sl]
            dw_ref[:, sl] += jnp.sum(dn * yh, axis=0, keepdims=True)
            dyh = dn * w_ref[:, sl]
            dyg = r * (dyh - yh * jnp.mean(dyh * yh, axis=-1, keepdims=True))
            dy_ref[:, sl] = (dyg * sz[:, sl]).astype(bf16)
            dz_ref[:, sl] = (dyg * yv[:, sl] * _dsilu(zs[:, sl])).astype(bf16)
        zf = zf_ref[...].astype(f32)
        ov = o_ref[...].astype(f32)
        dog = dg_ref[...].astype(f32)
        dov = (dog * _silu(zf)).astype(bf16)
        do_ref[...] = dov
        dz_ref[:, D_MODEL:] = (dog * ov * _dsilu(zf)).astype(bf16)
        prod = dov.astype(f32) * ov
        lane = lax.broadcasted_iota(jnp.int32, (ts, 128), 1)
        delta = jnp.zeros((ts, 128), f32)
        for h in range(N_HEADS):
            dh = jnp.sum(prod[:, HEAD_DIM * h:HEAD_DIM * (h + 1)], axis=-1, keepdims=True)
            delta = delta + jnp.where(lane == h, dh, 0.0)
        delta_ref[...] = delta.T

    return pl.pallas_call(
        body, name="gate0_bwd", grid=(s // ts,),
        in_specs=[_rowspec(ts, D_MODEL), _rowspec(ts, D_MODEL, 0), _rowspec(ts, D_MODEL, 1), _rowspec(ts, D_MODEL),
                  _rowspec(ts, D_MODEL, 0), _rowspec(ts, D_MODEL, 1), _vecspec(D_MODEL)],
        out_specs=[_rowspec(ts, D_MODEL), _rowspec(ts, D_MODEL), _rowspec(ts, 2 * D_MODEL),
                   pl.BlockSpec((128, ts), lambda i: (0, i)), _vecspec(D_MODEL)],
        out_shape=[jax.ShapeDtypeStruct((s, D_MODEL), bf16), jax.ShapeDtypeStruct((s, D_MODEL), bf16),
                   jax.ShapeDtypeStruct((s, 2 * D_MODEL), bf16), jax.ShapeDtypeStruct((128, s), f32),
                   jax.ShapeDtypeStruct((1, D_MODEL), f32)],
        compiler_params=_cp("arbitrary"))(y, z, z, o, dcat, dcat, ssd_norm)


def _conv_grid(s, c):
    ts, cb = min(CONV_ROW_TILE, s), min(CONV_COL_TILE, c)
    return ts, cb, (c // cb, s // ts)


def _cur(ts, cb, off=0):
    return pl.BlockSpec((ts, cb), lambda c, i: (i, c + off))


def _prev_halo(ts, cb, off=0):
    return pl.BlockSpec((HALO, cb), lambda c, i: (jnp.maximum(i * (ts // HALO) - 1, 0), c + off))


def _next_halo(ts, cb, s, off=0):
    return pl.BlockSpec((HALO, cb), lambda c, i: (jnp.minimum((i + 1) * (ts // HALO), s // HALO - 1), c + off))


def _wspec(k, cb):
    return pl.BlockSpec((k, cb), lambda c, i: (0, c))


def _phases(offsets):
    return sorted({o % 8 for o in offsets} - {0})


def _shift_scratch(offsets, ts, cb):
    return pltpu.VMEM((max(len(_phases(offsets)), 1), ts + HALO - 8, cb), f32)


def _fill_phases(ext_ref, sh_ref, offsets, ts):
    for j, r in enumerate(_phases(offsets)):
        sh_ref[j] = ext_ref[pl.ds(r, ts + HALO - 8), :]


def _slab(ext_ref, sh_ref, offsets, off, start):
    r = off % 8
    a = off - r + start
    if r == 0:
        return ext_ref[a:a + CONV_SUB, :]
    return sh_ref[_phases(offsets).index(r), a:a + CONV_SUB, :]


def _conv_taps(ext_ref, sh_ref, w_ref, b_ref, ts, k_taps, emit):
    offsets = [HALO - (k_taps - 1) + k for k in range(k_taps)]
    _fill_phases(ext_ref, sh_ref, offsets, ts)
    for sb in range(ts // CONV_SUB):
        acc = b_ref[...]
        for k in range(k_taps):
            acc = acc + w_ref[k:k + 1, :] * _slab(ext_ref, sh_ref, offsets, offsets[k], sb * CONV_SUB)
        emit(slice(sb * CONV_SUB, (sb + 1) * CONV_SUB), acc)


def _conv_ssd_fwd(xraw, w, b):
    s, c = xraw.shape
    ts, cb, grid = _conv_grid(s, c)
    offsets = [HALO - (SSD_CONV - 1) + k for k in range(SSD_CONV)]

    def body(x_ref, xh_ref, w_ref, b_ref, pre_ref, act_ref, ext_ref, sh_ref):
        first = pl.program_id(1) == 0
        ext_ref[0:HALO, :] = jnp.where(first, 0.0, xh_ref[...].astype(f32))
        ext_ref[HALO:, :] = x_ref[...].astype(f32)

        def emit(rows, pre):
            pre_ref[rows, :] = pre.astype(bf16)
            act_ref[rows, :] = _silu(pre).astype(bf16)
        _conv_taps(ext_ref, sh_ref, w_ref, b_ref, ts, SSD_CONV, emit)

    return pl.pallas_call(
        body, name="conv_ssd_fwd", grid=grid,
        in_specs=[_cur(ts, cb), _prev_halo(ts, cb), _wspec(SSD_CONV, cb), _wspec(1, cb)],
        out_specs=[_cur(ts, cb), _cur(ts, cb)],
        out_shape=[jax.ShapeDtypeStruct((s, c), bf16)] * 2,
        scratch_shapes=[pltpu.VMEM((HALO + ts, cb), f32), _shift_scratch(offsets, ts, cb)],
        compiler_params=_cp("parallel", "parallel"))(xraw, xraw, w, b)


def _conv_glu_fwd(proj1, w, b):
    s = proj1.shape[0]
    c = D_CONV
    ts, cb, grid = _conv_grid(s, c)
    goff = c // cb

    offsets = [HALO - (CONV_WIDTH - 1) + k for k in range(CONV_WIDTH)]

    def body(v_ref, g_ref, vh_ref, gh_ref, w_ref, b_ref, hc_ref, ext_ref, sh_ref):
        first = pl.program_id(1) == 0
        hh = vh_ref[...].astype(f32) * _sigmoid(gh_ref[...].astype(f32))
        ext_ref[0:HALO, :] = jnp.where(first, 0.0, hh)
        ext_ref[HALO:, :] = v_ref[...].astype(f32) * _sigmoid(g_ref[...].astype(f32))

        def emit(rows, hc):
            hc_ref[rows, :] = hc.astype(bf16)
        _conv_taps(ext_ref, sh_ref, w_ref, b_ref, ts, CONV_WIDTH, emit)

    return pl.pallas_call(
        body, name="conv_glu_fwd", grid=grid,
        in_specs=[_cur(ts, cb), _cur(ts, cb, goff), _prev_halo(ts, cb), _prev_halo(ts, cb, goff),
                  _wspec(CONV_WIDTH, cb), _wspec(1, cb)],
        out_specs=_cur(ts, cb),
        out_shape=jax.ShapeDtypeStruct((s, c), bf16),
        scratch_shapes=[pltpu.VMEM((HALO + ts, cb), f32), _shift_scratch(offsets, ts, cb)],
        compiler_params=_cp("parallel", "parallel"))(proj1, proj1, proj1, proj1, w, b)


def _conv_bwd_offsets(k_taps):
    return [k_taps - 1 - k for k in range(k_taps)], [HALO - (k_taps - 1) + k for k in range(k_taps)]


def _conv_bwd_scratch(k_taps, ts, cb):
    d_offs, x_offs = _conv_bwd_offsets(k_taps)
    return [pltpu.VMEM((ts + HALO, cb), f32), _shift_scratch(d_offs, ts, cb),
            pltpu.VMEM((HALO + ts, cb), f32), _shift_scratch(x_offs, ts, cb),
            pltpu.VMEM((k_taps, 8, cb), f32), pltpu.VMEM((8, cb), f32)]


def _conv_bwd_core(dp, dpn_ref, last, w_ref, scratch, dw_ref, db_ref, ts, k_taps, emit):
    dext_ref, dsh_ref, xext_ref, xsh_ref, dw8_ref, db8_ref = scratch
    d_offs, x_offs = _conv_bwd_offsets(k_taps)
    dext_ref[0:ts, :] = dp
    dext_ref[ts:, :] = jnp.where(last, 0.0, dpn_ref[...].astype(f32))
    _fill_phases(dext_ref, dsh_ref, d_offs, ts)
    _fill_phases(xext_ref, xsh_ref, x_offs, ts)

    @pl.when(pl.program_id(1) == 0)
    def _():
        dw8_ref[...] = jnp.zeros_like(dw8_ref)
        db8_ref[...] = jnp.zeros_like(db8_ref)
    cb = dp.shape[1]
    for sb in range(ts // CONV_SUB):
        start = sb * CONV_SUB
        dpv = dext_ref[start:start + CONV_SUB, :]
        dx = None
        for k in range(k_taps):
            t = w_ref[k:k + 1, :] * _slab(dext_ref, dsh_ref, d_offs, d_offs[k], start)
            dx = t if dx is None else dx + t
            prod = dpv * _slab(xext_ref, xsh_ref, x_offs, x_offs[k], start)
            dw8_ref[k] += jnp.sum(prod.reshape(CONV_SUB // 8, 8, cb), axis=0)
        db8_ref[...] += jnp.sum(dpv.reshape(CONV_SUB // 8, 8, cb), axis=0)
        emit(slice(start, start + CONV_SUB), dx)

    @pl.when(last)
    def _():
        dw_ref[...] = jnp.sum(dw8_ref[...], axis=1)
        db_ref[...] = jnp.sum(db8_ref[...], axis=0, keepdims=True)


def _conv_ssd_bwd(dpre, xraw, w):
    s, c = xraw.shape
    ts, cb, grid = _conv_grid(s, c)
    nb = s // ts

    def body(dp_ref, dpn_ref, x_ref, xh_ref, w_ref, dx_ref, dw_ref, db_ref, *scratch):
        i = pl.program_id(1)
        xext_ref = scratch[2]
        xext_ref[0:HALO, :] = jnp.where(i == 0, 0.0, xh_ref[...].astype(f32))
        xext_ref[HALO:, :] = x_ref[...].astype(f32)

        def emit(rows, dx):
            dx_ref[rows, :] = dx.astype(bf16)
        _conv_bwd_core(dp_ref[...].astype(f32), dpn_ref, i == nb - 1, w_ref, scratch, dw_ref, db_ref, ts, SSD_CONV, emit)

    return pl.pallas_call(
        body, name="conv_ssd_bwd", grid=grid,
        in_specs=[_cur(ts, cb), _next_halo(ts, cb, s), _cur(ts, cb), _prev_halo(ts, cb), _wspec(SSD_CONV, cb)],
        out_specs=[_cur(ts, cb), _wspec(SSD_CONV, cb), _wspec(1, cb)],
        out_shape=[jax.ShapeDtypeStruct((s, c), bf16), jax.ShapeDtypeStruct((SSD_CONV, c), f32),
                   jax.ShapeDtypeStruct((1, c), f32)],
        scratch_shapes=_conv_bwd_scratch(SSD_CONV, ts, cb),
        compiler_params=_cp("parallel", "arbitrary"))(dpre, dpre, xraw, xraw, w)


def _conv_glu_bwd(dhc, proj1, w):
    s = proj1.shape[0]
    c = D_CONV
    ts, cb, grid = _conv_grid(s, c)
    nb = s // ts
    goff = c // cb

    def body(dp_ref, dpn_ref, v_ref, g_ref, vh_ref, gh_ref, w_ref, dv_ref, dg_ref, dw_ref, db_ref, *scratch):
        i = pl.program_id(1)
        xext_ref = scratch[2]
        xext_ref[0:HALO, :] = jnp.where(i == 0, 0.0, vh_ref[...].astype(f32) * _sigmoid(gh_ref[...].astype(f32)))
        xext_ref[HALO:, :] = v_ref[...].astype(f32) * _sigmoid(g_ref[...].astype(f32))

        def emit(rows, dh):
            val = v_ref[rows, :].astype(f32)
            sg = _sigmoid(g_ref[rows, :].astype(f32))
            dv_ref[rows, :] = (dh * sg).astype(bf16)
            dg_ref[rows, :] = (dh * val * sg * (1.0 - sg)).astype(bf16)
        _conv_bwd_core(dp_ref[...].astype(f32), dpn_ref, i == nb - 1, w_ref, scratch, dw_ref, db_ref, ts, CONV_WIDTH, emit)

    return pl.pallas_call(
        body, name="conv_glu_bwd", grid=grid,
        in_specs=[_cur(ts, cb), _next_halo(ts, cb, s), _cur(ts, cb), _cur(ts, cb, goff), _prev_halo(ts, cb),
                  _prev_halo(ts, cb, goff), _wspec(CONV_WIDTH, cb)],
        out_specs=[_cur(ts, cb), _cur(ts, cb), _wspec(CONV_WIDTH, cb), _wspec(1, cb)],
        out_shape=[jax.ShapeDtypeStruct((s, c), bf16), jax.ShapeDtypeStruct((s, c), bf16),
                   jax.ShapeDtypeStruct((CONV_WIDTH, c), f32), jax.ShapeDtypeStruct((1, c), f32)],
        scratch_shapes=_conv_bwd_scratch(CONV_WIDTH, ts, cb),
        compiler_params=_cp("parallel", "arbitrary"))(dhc, dhc, proj1, proj1, proj1, proj1, w)


def _ssd_common(dt_ref, prm_ref):
    l = CHUNK
    dtb = prm_ref[0:1, :]
    a = -jnp.exp(prm_ref[1:2, :])
    dsk = prm_ref[2:3, :]
    zraw = dt_ref[...] + dtb
    dt = _softplus(zraw)
    da = dt * a
    row = lax.broadcasted_iota(jnp.int32, (l, l), 0)
    col = lax.broadcasted_iota(jnp.int32, (l, l), 1)
    causal = row >= col
    cs = _dot(causal.astype(f32), da, precision=HIGHEST)
    return a, dsk, zraw, dt, cs, cs.T, causal, row, col


def _ssd_fwd(act, dtf, prm):
    s = act.shape[0]
    nc = s // CHUNK
    l = CHUNK

    def body(xs_ref, dt_ref, prm_ref, y_ref, hs_ref, st_ref):
        @pl.when(pl.program_id(0) == 0)
        def _():
            st_ref[...] = jnp.zeros_like(st_ref)
        a, dsk, _, dt, cs, cst, causal, _, _ = _ssd_common(dt_ref, prm_ref)
        for g in range(N_GROUPS):
            bm = xs_ref[:, B_OFF + D_STATE * g:B_OFF + D_STATE * (g + 1)]
            cm = xs_ref[:, C_OFF + D_STATE * g:C_OFF + D_STATE * (g + 1)]
            gmat = _dot(cm, bm, NT)
            for r in range(HEADS_PER_GROUP):
                h = HEADS_PER_GROUP * g + r
                hsl = slice(HEAD_DIM * h, HEAD_DIM * (h + 1))
                xv = xs_ref[:, hsl].astype(f32)
                csc = cs[:, h:h + 1]
                csr = cst[h:h + 1, :]
                cl = cs[l - 1:l, h:h + 1]
                dk = jnp.exp(jnp.where(causal, csc - csr, NEG))
                xd = xv * dt[:, h:h + 1]
                hp = st_ref[h]
                hs_ref[0, h] = hp
                ydiag = _dot((gmat * dk).astype(bf16), xd.astype(bf16))
                yoff = _dot(cm, hp.astype(bf16), NT) * jnp.exp(csc)
                y_ref[:, hsl] = (ydiag + yoff + xv * dsk[:, h:h + 1]).astype(bf16)
                st = _dot((xd * jnp.exp(cl - csc)).astype(bf16), bm, TN)
                st_ref[h] = hp * jnp.exp(cl) + st

    return pl.pallas_call(
        body, name="ssd_fwd", grid=(nc,),
        in_specs=[pl.BlockSpec((l, XBC_W), lambda i: (i, 0)), pl.BlockSpec((l, 128), lambda i: (i, 0)),
                  pl.BlockSpec((8, 128), lambda i: (0, 0))],
        out_specs=[pl.BlockSpec((l, D_MODEL), lambda i: (i, 0)),
                   pl.BlockSpec((1, N_HEADS, HEAD_DIM, D_STATE), lambda i: (i, 0, 0, 0))],
        out_shape=[jax.ShapeDtypeStruct((s, D_MODEL), bf16),
                   jax.ShapeDtypeStruct((nc, N_HEADS, HEAD_DIM, D_STATE), f32)],
        scratch_shapes=[pltpu.VMEM((N_HEADS, HEAD_DIM, D_STATE), f32)],
        compiler_params=_cp("arbitrary"))(act, dtf, prm)


def _ssd_bwd(act, pre, dtf, prm, hs, dy):
    s = act.shape[0]
    nc = s // CHUNK
    l = CHUNK

    def body(xs_ref, pre_ref, dt_ref, prm_ref, hs_ref, dy_ref, dpre_ref, ddt_ref, dprm_ref, dh_ref):
        @pl.when(pl.program_id(0) == 0)
        def _():
            dh_ref[...] = jnp.zeros_like(dh_ref)
            dprm_ref[...] = jnp.zeros_like(dprm_ref)
        a, dsk, zraw, dt, cs, cst, causal, row, col = _ssd_common(dt_ref, prm_ref)
        lane = lax.broadcasted_iota(jnp.int32, (l, 128), 1)
        rowl = lax.broadcasted_iota(jnp.int32, (l, 128), 0)
        sub = lax.broadcasted_iota(jnp.int32, (128, l), 0)
        lane1 = lax.broadcasted_iota(jnp.int32, (1, 128), 1)
        dcs_c = jnp.zeros((l, 128), f32)
        dcs_r = jnp.zeros((128, l), f32)
        ddt_c = jnp.zeros((l, 128), f32)
        dd_row = jnp.zeros((1, 128), f32)
        for g in range(N_GROUPS):
            bsl = slice(B_OFF + D_STATE * g, B_OFF + D_STATE * (g + 1))
            csl = slice(C_OFF + D_STATE * g, C_OFF + D_STATE * (g + 1))
            bm = xs_ref[:, bsl]
            cm = xs_ref[:, csl]
            gmat = _dot(cm, bm, NT)
            dgm = jnp.zeros((l, l), f32)
            dbg = jnp.zeros((l, D_STATE), f32)
            dcg = jnp.zeros((l, D_STATE), f32)
            for r in range(HEADS_PER_GROUP):
                h = HEADS_PER_GROUP * g + r
                hsl = slice(HEAD_DIM * h, HEAD_DIM * (h + 1))
                xv = xs_ref[:, hsl].astype(f32)
                dyv = dy_ref[:, hsl].astype(f32)
                dyb = dyv.astype(bf16)
                csc = cs[:, h:h + 1]
                csr = cst[h:h + 1, :]
                cl = cs[l - 1:l, h:h + 1]
                dk = jnp.exp(jnp.where(causal, csc - csr, NEG))
                mf = gmat * dk
                dtc = dt[:, h:h + 1]
                xd = xv * dtc
                xdb = xd.astype(bf16)
                ecs = jnp.exp(csc)
                dec = jnp.exp(cl)
                e = jnp.exp(cl - csc)
                hp = hs_ref[0, h]
                hpb = hp.astype(bf16)
                dhn = dh_ref[h]
                dhnb = dhn.astype(bf16)
                dd_h = jnp.sum(jnp.sum(dyv * xv, axis=1, keepdims=True), axis=0, keepdims=True)
                dx = dyv * dsk[:, h:h + 1]
                ch = _dot(cm, hpb, NT)
                dye = dyv * ecs
                dyeb = dye.astype(bf16)
                dcg = dcg + _dot(dyeb, hpb)
                dhp = _dot(dyeb, cm, TN)
                dcs_col = jnp.sum(dye * ch, axis=1, keepdims=True)
                dm = _dot(dyb, xdb, NT)
                dxd = _dot(mf.astype(bf16), dyb, TN)
                dgm = dgm + dm * dk
                wmat = dm * mf
                dcs_col = dcs_col + jnp.sum(wmat, axis=1, keepdims=True)
                dcs_row = -jnp.sum(wmat, axis=0, keepdims=True)
                ddec = jnp.sum(jnp.sum(hp * dhn, axis=1, keepdims=True), axis=0, keepdims=True)
                dxe = _dot(bm, dhnb, NT)
                dxd = dxd + dxe * e
                de_e = jnp.sum(dxe * xd, axis=1, keepdims=True) * e
                dbg = dbg + _dot((xd * e).astype(bf16), dhnb)
                dcs_col = dcs_col - de_e
                dlast = ddec * dec + jnp.sum(de_e, axis=0, keepdims=True)
                dh_ref[h] = dhp + dec * dhn
                dx = dx + dxd * dtc
                ddt_h = jnp.sum(dxd * xv, axis=1, keepdims=True)
                is_h = lane == h
                dcs_c = dcs_c + jnp.where(is_h, dcs_col, 0.0) + jnp.where(is_h & (rowl == l - 1), dlast, 0.0)
                dcs_r = dcs_r + jnp.where(sub == h, dcs_row, 0.0)
                ddt_c = ddt_c + jnp.where(is_h, ddt_h, 0.0)
                dd_row = dd_row + jnp.where(lane1 == h, dd_h, 0.0)
                dpre_ref[:, hsl] = (dx * _dsilu(pre_ref[:, hsl].astype(f32))).astype(bf16)
            dgb = dgm.astype(bf16)
            dcg = dcg + _dot(dgb, bm)
            dbg = dbg + _dot(dgb, cm, TN)
            dpre_ref[:, bsl] = (dbg * _dsilu(pre_ref[:, bsl].astype(f32))).astype(bf16)
            dpre_ref[:, csl] = (dcg * _dsilu(pre_ref[:, csl].astype(f32))).astype(bf16)
        dcs = dcs_c + dcs_r.T
        dda = _dot((row <= col).astype(f32), dcs, precision=HIGHEST)
        ddt = ddt_c + dda * a
        ddtraw = jnp.where(lane < N_HEADS, ddt * _sigmoid(zraw), 0.0)
        ddt_ref[...] = ddtraw
        dprm_ref[0:1, :] += jnp.sum(ddtraw, axis=0, keepdims=True)
        dprm_ref[1:2, :] += jnp.where(lane1 < N_HEADS, jnp.sum(dda * dt, axis=0, keepdims=True) * a, 0.0)
        dprm_ref[2:3, :] += dd_row

    def rev(i):
        return (nc - 1 - i, 0)

    return pl.pallas_call(
        body, name="ssd_bwd", grid=(nc,),
        in_specs=[pl.BlockSpec((l, XBC_W), rev), pl.BlockSpec((l, XBC_W), rev),
                  pl.BlockSpec((l, 128), rev), pl.BlockSpec((8, 128), lambda i: (0, 0)),
                  pl.BlockSpec((1, N_HEADS, HEAD_DIM, D_STATE), lambda i: (nc - 1 - i, 0, 0, 0)),
                  pl.BlockSpec((l, D_MODEL), rev)],
        out_specs=[pl.BlockSpec((l, XBC_W), rev), pl.BlockSpec((l, 128), rev), pl.BlockSpec((8, 128), lambda i: (0, 0))],
        out_shape=[jax.ShapeDtypeStruct((s, XBC_W), bf16), jax.ShapeDtypeStruct((s, 128), f32),
                   jax.ShapeDtypeStruct((8, 128), f32)],
        scratch_shapes=[pltpu.VMEM((N_HEADS, HEAD_DIM, D_STATE), f32)],
        compiler_params=_cp("arbitrary"))(act, pre, dtf, prm, hs, dy)


def _fox_cumsum(dtf, prm):
    s = dtf.shape[0]
    l = CHUNK

    def body(f_ref, prm_ref, c_ref, carry_ref):
        @pl.when(pl.program_id(0) == 0)
        def _():
            carry_ref[...] = jnp.zeros_like(carry_ref)
        lf = _log_sigmoid(f_ref[...] + prm_ref[3:4, :])
        row = lax.broadcasted_iota(jnp.int32, (l, l), 0)
        col = lax.broadcasted_iota(jnp.int32, (l, l), 1)
        c = _dot((row >= col).astype(f32), lf, precision=HIGHEST) + carry_ref[...]
        c_ref[...] = c
        carry_ref[...] = c[l - 1:l, :]

    return pl.pallas_call(
        body, name="fox_cumsum", grid=(s // l,),
        in_specs=[pl.BlockSpec((l, 128), lambda i: (i, 0)), pl.BlockSpec((8, 128), lambda i: (0, 0))],
        out_specs=pl.BlockSpec((l, 128), lambda i: (i, 0)),
        out_shape=jax.ShapeDtypeStruct((s, 128), f32),
        scratch_shapes=[pltpu.VMEM((1, 128), f32)],
        compiler_params=_cp("arbitrary"))(dtf, prm)


AUG = HEAD_DIM
N_PAIRS = N_HEADS // 2
V_BLOCK = 2 * D_MODEL // 128


def _split3(x):
    hi = x.astype(bf16)
    r1 = x - hi.astype(f32)
    mid = r1.astype(bf16)
    lo = (r1 - mid.astype(f32)).astype(bf16)
    return hi.astype(f32), mid.astype(f32), lo.astype(f32)


def _fox_prep(qkv, c):
    s = qkv.shape[0]
    ts = min(CONV_ROW_TILE, s)
    kb = D_MODEL // 128

    def body(q_ref, k_ref, c_ref, qa_ref, ka_ref):
        lane = lax.broadcasted_iota(jnp.int32, (ts, 128), 1)
        low = lane < HEAD_DIM
        for h in range(N_HEADS):
            psl = slice(128 * (h // 2), 128 * (h // 2 + 1))
            qv = q_ref[:, psl].astype(f32) * (HEAD_DIM ** -0.5)
            kv = k_ref[:, psl].astype(f32)
            if h % 2:
                qv = pltpu.roll(qv, HEAD_DIM, 1)
                kv = pltpu.roll(kv, HEAD_DIM, 1)
            hi, mid, lo = _split3(c_ref[:, F_LANE + h:F_LANE + h + 1])
            ones = jnp.where((lane >= AUG + 3) & (lane < AUG + 6), 1.0, 0.0)
            cq = jnp.where(lane == AUG, hi, jnp.where(lane == AUG + 1, mid, jnp.where(lane == AUG + 2, lo, ones)))
            qa_ref[h] = jnp.where(low, qv, cq).astype(bf16)
            onek = jnp.where((lane >= AUG) & (lane < AUG + 3), 1.0, 0.0)
            ck = jnp.where(lane == AUG + 3, -hi, jnp.where(lane == AUG + 4, -mid, jnp.where(lane == AUG + 5, -lo, onek)))
            ka_ref[h] = jnp.where(low, kv, ck).astype(bf16)

    hm = pl.BlockSpec((N_HEADS, ts, 128), lambda i: (0, i, 0))
    return pl.pallas_call(
        body, name="fox_prep", grid=(s // ts,),
        in_specs=[_rowspec(ts, D_MODEL, 0), _rowspec(ts, D_MODEL, 1), _rowspec(ts, 128)],
        out_specs=[hm, hm], out_shape=[jax.ShapeDtypeStruct((N_HEADS, s, 128), bf16)] * 2,
        compiler_params=_cp("parallel"))(qkv, qkv, c)


def _fox_fwd(qa, ka, qkv):
    s = qkv.shape[0]
    t = min(ATTN_FWD_TILE, s)
    nq = s // t

    def body(qa_ref, ka_ref, v_ref, o_ref, lse_ref):
        qi = pl.program_id(1)
        low = lax.broadcasted_iota(jnp.int32, (t, 128), 1) < HEAD_DIM
        row = lax.broadcasted_iota(jnp.int32, (t, t), 0)
        col = lax.broadcasted_iota(jnp.int32, (t, t), 1)

        def tile(ki, carry, diagonal):
            stats, acc = carry
            koff = pl.multiple_of(ki * t, t)
            v = v_ref[pl.ds(koff, t), :]
            vh = (jnp.where(low, v, jnp.zeros_like(v)), jnp.where(low, jnp.zeros_like(v), v))
            new_stats, alphas, pv = [], [], None
            for r in range(2):
                m_old, l_old = stats[r]
                sc = _dot(qa_ref[r], ka_ref[r, pl.ds(koff, t), :], NT)
                if diagonal:
                    sc = jnp.where(col <= row, sc, NEG)
                m_new = jnp.maximum(m_old, jnp.max(sc, axis=1, keepdims=True))
                p = jnp.exp(sc - m_new)
                alpha = jnp.exp(m_old - m_new)
                new_stats.append((m_new, alpha * l_old + jnp.sum(p, axis=1, keepdims=True)))
                alphas.append(alpha)
                d = _dot(p.astype(bf16), vh[r])
                pv = d if pv is None else pv + d
            acc = acc * jnp.where(low, alphas[0], alphas[1]) + pv
            return tuple(new_stats), acc

        init = (((jnp.full((t, 1), NEG, f32), jnp.zeros((t, 1), f32)),) * 2, jnp.zeros((t, 128), f32))
        carry = lax.fori_loop(0, qi, lambda ki, cr: tile(ki, cr, False), init)
        stats, acc = tile(qi, carry, True)
        o_ref[...] = (acc / jnp.where(low, stats[0][1], stats[1][1])).astype(bf16)
        for r in range(2):
            lse = stats[r][0] + jnp.log(stats[r][1])
            lse_ref[r] = jnp.broadcast_to(lse, (t, 128)).T[0:1, :]

    return pl.pallas_call(
        body, name="fox_fwd", grid=(N_PAIRS, nq),
        in_specs=[pl.BlockSpec((2, t, 128), lambda j, qi: (j, qi, 0)),
                  pl.BlockSpec((2, s, 128), lambda j, qi: (j, 0, 0)),
                  pl.BlockSpec((s, 128), lambda j, qi: (0, V_BLOCK + j))],
        out_specs=[pl.BlockSpec((t, 128), lambda j, qi: (qi, j)), pl.BlockSpec((2, 1, t), lambda j, qi: (j, 0, qi))],
        out_shape=[jax.ShapeDtypeStruct((s, D_MODEL), bf16), jax.ShapeDtypeStruct((N_HEADS, 1, s), f32)],
        compiler_params=_cp("parallel", "parallel"))(qa, ka, qkv)


def _fox_bwd(qa, ka, qkv, do, lse, delta):
    s = qkv.shape[0]
    t = min(ATTN_TILE, s)
    nq = s // t

    def body(qa_ref, ka_ref, v_ref, do_ref, lse_ref, dl_ref, dq_ref, dk_ref, dv_ref):
        ki = pl.program_id(1)

        @pl.when(ki == 0)
        def _():
            dq_ref[...] = jnp.zeros_like(dq_ref)
        low = lax.broadcasted_iota(jnp.int32, (t, 128), 1) < HEAD_DIM
        row = lax.broadcasted_iota(jnp.int32, (t, t), 0)
        col = lax.broadcasted_iota(jnp.int32, (t, t), 1)
        v = v_ref[...]
        zero = jnp.zeros_like(v)
        vh = (jnp.where(low, v, zero), jnp.where(low, zero, v))

        def tile(qi, carry, diagonal):
            dks, dv = carry
            qoff = pl.multiple_of(qi * t, t)
            dov = do_ref[pl.ds(qoff, t), :]
            doh = (jnp.where(low, dov, zero), jnp.where(low, zero, dov))
            new_dks = []
            for r in range(2):
                qt = qa_ref[r, pl.ds(qoff, t), :]
                sct = _dot(ka_ref[r], qt, NT)
                if diagonal:
                    sct = jnp.where(row <= col, sct, NEG)
                pt = jnp.exp(sct - lse_ref[r, :, pl.ds(qoff, t)])
                dpt = _dot(vh[r], dov, NT)
                dst = (pt * (dpt - dl_ref[r, :, pl.ds(qoff, t)])).astype(bf16)
                dv = dv + _dot(pt.astype(bf16), doh[r])
                new_dks.append(dks[r] + _dot(dst, qt))
                dq_ref[r, pl.ds(qoff, t), :] += _dot(dst, ka_ref[r], TN)
            return tuple(new_dks), dv

        zacc = jnp.zeros((t, 128), f32)
        carry = tile(ki, ((zacc, zacc), zacc), True)
        dks, dv = lax.fori_loop(ki + 1, nq, lambda qi, cr: tile(qi, cr, False), carry)
        dk_ref[0] = dks[0]
        dk_ref[1] = dks[1]
        dv_ref[...] = dv.astype(bf16)

    return pl.pallas_call(
        body, name="fox_bwd", grid=(N_PAIRS, nq),
        in_specs=[pl.BlockSpec((2, s, 128), lambda j, ki: (j, 0, 0)),
                  pl.BlockSpec((2, t, 128), lambda j, ki: (j, ki, 0)),
                  pl.BlockSpec((t, 128), lambda j, ki: (ki, V_BLOCK + j)),
                  pl.BlockSpec((s, 128), lambda j, ki: (0, j)),
                  pl.BlockSpec((2, 1, s), lambda j, ki: (j, 0, 0)),
                  pl.BlockSpec((2, 1, s), lambda j, ki: (j, 0, 0))],
        out_specs=[pl.BlockSpec((2, s, 128), lambda j, ki: (j, 0, 0)),
                   pl.BlockSpec((2, t, 128), lambda j, ki: (j, ki, 0)),
                   pl.BlockSpec((t, 128), lambda j, ki: (ki, j))],
        out_shape=[jax.ShapeDtypeStruct((N_HEADS, s, 128), f32), jax.ShapeDtypeStruct((N_HEADS, s, 128), f32),
                   jax.ShapeDtypeStruct((s, D_MODEL), bf16)],
        compiler_params=_cp("parallel", "arbitrary"))(qa, ka, qkv, do, lse, delta)


def _fox_bwd_post(dq_hm, dk_hm):
    s = dq_hm.shape[1]
    ts = min(CONV_ROW_TILE, s)

    def body(dq_ref, dk_ref, q_ref, k_ref, dc_ref):
        lane = lax.broadcasted_iota(jnp.int32, (ts, 128), 1)
        dc = jnp.zeros((ts, 128), f32)
        for h in range(N_HEADS):
            hsl = slice(HEAD_DIM * h, HEAD_DIM * (h + 1))
            dqv = dq_ref[h]
            dkv = dk_ref[h]
            q_ref[:, hsl] = (dqv[:, 0:HEAD_DIM] * (HEAD_DIM ** -0.5)).astype(bf16)
            k_ref[:, hsl] = dkv[:, 0:HEAD_DIM].astype(bf16)
            dc = dc + jnp.where(lane == F_LANE + h, dqv[:, AUG:AUG + 1] - dkv[:, AUG + 3:AUG + 4], 0.0)
        dc_ref[...] = dc

    hm = pl.BlockSpec((N_HEADS, ts, 128), lambda i: (0, i, 0))
    return pl.pallas_call(
        body, name="fox_bwd_post", grid=(s // ts,), in_specs=[hm, hm],
        out_specs=[_rowspec(ts, D_MODEL), _rowspec(ts, D_MODEL), _rowspec(ts, 128)],
        out_shape=[jax.ShapeDtypeStruct((s, D_MODEL), bf16), jax.ShapeDtypeStruct((s, D_MODEL), bf16),
                   jax.ShapeDtypeStruct((s, 128), f32)],
        compiler_params=_cp("parallel"))(dq_hm, dk_hm)


def _fox_gate_bwd(dc, dtf, prm, ddt_raw):
    s = dtf.shape[0]
    l = CHUNK
    nb = s // l

    def body(dc_ref, f_ref, prm_ref, ddt_ref, out_ref, dfb_ref, carry_ref):
        @pl.when(pl.program_id(0) == 0)
        def _():
            carry_ref[...] = jnp.zeros_like(carry_ref)
            dfb_ref[...] = jnp.zeros_like(dfb_ref)
        dc = dc_ref[...]
        row = lax.broadcasted_iota(jnp.int32, (l, l), 0)
        col = lax.broadcasted_iota(jnp.int32, (l, l), 1)
        dlf = _dot((row <= col).astype(f32), dc, precision=HIGHEST) + carry_ref[...]
        carry_ref[...] = dlf[0:1, :]
        lane = lax.broadcasted_iota(jnp.int32, (l, 128), 1)
        is_f = (lane >= F_LANE) & (lane < F_LANE + N_HEADS)
        dfr = jnp.where(is_f, dlf * _sigmoid(-(f_ref[...] + prm_ref[3:4, :])), 0.0)
        dfb_ref[...] += jnp.sum(dfr, axis=0, keepdims=True)
        out_ref[...] = ddt_ref[...] + dfr

    def rev(i):
        return (nb - 1 - i, 0)

    return pl.pallas_call(
        body, name="fox_gate_bwd", grid=(nb,),
        in_specs=[pl.BlockSpec((l, 128), rev), pl.BlockSpec((l, 128), rev), pl.BlockSpec((8, 128), lambda i: (0, 0)),
                  pl.BlockSpec((l, 128), rev)],
        out_specs=[pl.BlockSpec((l, 128), rev), pl.BlockSpec((1, 128), lambda i: (0, 0))],
        out_shape=[jax.ShapeDtypeStruct((s, 128), f32), jax.ShapeDtypeStruct((1, 128), f32)],
        scratch_shapes=[pltpu.VMEM((1, 128), f32)],
        compiler_params=_cp("arbitrary"))(dc, dtf, prm, ddt_raw)


def _position():
    return lax.axis_index("x"), lax.axis_index("y"), lax.axis_index("c")


def _all_gather(xl, name):
    r, c = xl.shape

    def body(x_ref, out_ref, send_sems, recv_sems, local_sem):
        x, y, cc = _position()
        me, sibling = (x, y, cc), (x, y, 1 - cc)
        chips = [(1 - x, y), (x, 1 - y), (1 - x, 1 - y)]

        def slot(px, py, pc):
            return out_ref.at[4 * px + 2 * py + pc]

        def copy(k, block, to, src=None):
            return pltpu.make_async_remote_copy(
                src_ref=slot(*block) if src is None else src, dst_ref=slot(*block),
                send_sem=send_sems.at[k], recv_sem=recv_sems.at[k],
                device_id=to, device_id_type=pl.DeviceIdType.MESH)

        mine = pltpu.make_async_copy(x_ref, slot(*me), local_sem)
        mine.start()
        first = [copy(0, me, sibling, src=x_ref)]
        first += [copy(1 + j, me, (*chip, cc), src=x_ref) for j, chip in enumerate(chips)]
        for cp in first:
            cp.start()
        passed = [copy(4 + j, (*chip, cc), sibling) for j, chip in enumerate(chips)]
        for j, chip in enumerate(chips):
            copy(1 + j, (*chip, cc), me).wait_recv()
            passed[j].start()
        copy(0, sibling, me).wait_recv()
        for j, chip in enumerate(chips):
            copy(4 + j, (*chip, 1 - cc), me).wait_recv()
        for cp in first + passed:
            cp.wait_send()
        mine.wait()

    return pl.pallas_call(
        body, name=name,
        out_shape=jax.ShapeDtypeStruct((N_DEV, r, c), xl.dtype),
        in_specs=[pl.BlockSpec(memory_space=pl.ANY)], out_specs=pl.BlockSpec(memory_space=pl.ANY),
        scratch_shapes=[pltpu.SemaphoreType.DMA((7,)), pltpu.SemaphoreType.DMA((7,)), pltpu.SemaphoreType.DMA],
    )(xl)


def _grad_exchange(gs):
    n = len(gs)

    def body(*refs):
        g_refs, r_refs = refs[:n], refs[n:2 * n]
        send_sems, recv_sems, local_sems = refs[2 * n:]
        x, y, cc = _position()
        me = 4 * x + 2 * y + cc
        local = [pltpu.make_async_copy(g_refs[a].at[me], r_refs[a].at[me], local_sems.at[a]) for a in range(n)]
        for cp in local:
            cp.start()
        sends, recvs = [], []
        for k in range(1, N_DEV):
            px = 1 - x if k & 4 else x
            py = 1 - y if k & 2 else y
            pc = 1 - cc if k & 1 else cc
            pid = 4 * px + 2 * py + pc
            for a in range(n):
                sends.append(pltpu.make_async_remote_copy(
                    src_ref=g_refs[a].at[pid], dst_ref=r_refs[a].at[me],
                    send_sem=send_sems.at[a, k - 1], recv_sem=recv_sems.at[a, k - 1],
                    device_id=(px, py, pc), device_id_type=pl.DeviceIdType.MESH))
                recvs.append(pltpu.make_async_remote_copy(
                    src_ref=g_refs[a].at[pid], dst_ref=r_refs[a].at[pid],
                    send_sem=send_sems.at[a, k - 1], recv_sem=recv_sems.at[a, k - 1],
                    device_id=(px, py, pc), device_id_type=pl.DeviceIdType.MESH))
        for cp in sends:
            cp.start()
        for cp in recvs:
            cp.wait_recv()
        for cp in sends:
            cp.wait_send()
        for cp in local:
            cp.wait()

    anyspec = pl.BlockSpec(memory_space=pl.ANY)
    return pl.pallas_call(
        body, name="grad_exchange",
        out_shape=[jax.ShapeDtypeStruct(g.shape, g.dtype) for g in gs],
        in_specs=[anyspec] * n, out_specs=[anyspec] * n,
        scratch_shapes=[pltpu.SemaphoreType.DMA((n, N_DEV - 1)), pltpu.SemaphoreType.DMA((n, N_DEV - 1)),
                        pltpu.SemaphoreType.DMA((n,))],
    )(*gs)


def _sum_parts(parts, name):
    n, r, c = parts.shape

    def body(p_ref, o_ref):
        g = p_ref[0]
        for i in range(1, n):
            g = g + p_ref[i]
        o_ref[...] = g

    return pl.pallas_call(body, name=name, out_shape=jax.ShapeDtypeStruct((r, c), f32))(parts)


def _adamw(w, m, v, parts, name, tr=128):
    r, c = w.shape
    n = parts.shape[0]
    tr = min(tr, r)
    c1 = 1.0 - ADAM_B1 ** ADAM_STEP
    c2 = 1.0 - ADAM_B2 ** ADAM_STEP

    def body(w_ref, m_ref, v_ref, p_ref, g_ref, d_ref, nm_ref, nv_ref):
        g = p_ref[0].astype(f32)
        for i in range(1, n):
            g = g + p_ref[i].astype(f32)
        g_ref[...] = g
        nm = ADAM_B1 * m_ref[...] + (1.0 - ADAM_B1) * g
        nv = ADAM_B2 * v_ref[...] + (1.0 - ADAM_B2) * (g * g)
        nm_ref[...] = nm
        nv_ref[...] = nv
        d_ref[...] = -ADAM_LR * ((nm / c1) / (jnp.sqrt(nv / c2) + ADAM_EPS) + ADAM_WD * w_ref[...])

    blk = pl.BlockSpec((tr, c), lambda i: (i, 0))
    return pl.pallas_call(
        body, name=name, grid=(r // tr,),
        in_specs=[blk, blk, blk, pl.BlockSpec((n, tr, c), lambda i: (0, i, 0))],
        out_specs=[blk] * 4, out_shape=[jax.ShapeDtypeStruct((r, c), f32)] * 4,
        compiler_params=_cp("parallel"))(w, m, v, parts)


def _lanes(w):
    return -(-w // 128) * 128


def _pack(arrs):
    rows = []
    for a in arrs:
        k, w = a.shape
        if w % 128:
            a = jnp.pad(a, ((0, 0), (0, _lanes(w) - w)))
        rows.append(a.reshape(-1, 128))
    out = jnp.concatenate(rows, axis=0)
    pad = -out.shape[0] % 8
    return jnp.pad(out, ((0, pad), (0, 0))) if pad else out


def _unpack(packed, shapes):
    outs, off = [], 0
    lead = packed.shape[:-2]
    for k, w in shapes:
        nrow = k * _lanes(w) // 128
        a = packed[..., off:off + nrow, :].reshape(*lead, k, _lanes(w))[..., :w]
        outs.append(a)
        off += nrow
    return outs


def _gathered_cols(a):
    n, k, wl = a.shape
    return jnp.transpose(a, (1, 0, 2)).reshape(k, n * wl)


def _col_shards(a):
    k, w = a.shape
    return jnp.transpose(a.reshape(k, N_DEV, w // N_DEV), (1, 0, 2))


SMALL_PARAMS = (
    ("e_norm_pre", 1, 1024, False), ("e_conv_w", 4, 2048, True), ("e_conv_b", 1, 2048, False),
    ("e_dt_bias", 1, 16, False), ("e_a_log", 1, 16, False), ("e_d_skip", 1, 16, False), ("e_fgate_b", 1, 16, False),
    ("e_ssd_norm", 1, 1024, False), ("e_norm_post", 1, 1024, False), ("o_norm_pre", 1, 1024, True),
    ("o_conv_w", 31, 2048, True), ("o_conv_b", 1, 2048, True), ("o_ln_g", 1, 2048, True), ("o_ln_b", 1, 2048, True),
    ("o_norm_post", 1, 1024, True),
)
BIG_PARAMS = ("e_w_in", "e_w_out", "o_w_in", "o_w_out")
WEIGHT_ORDER = ("e_norm_pre", "e_w_in", "e_conv_w", "e_conv_b", "e_dt_bias", "e_a_log", "e_d_skip", "e_fgate_b",
                "e_ssd_norm", "e_w_out", "e_norm_post", "o_norm_pre", "o_w_in", "o_conv_w", "o_conv_b", "o_ln_g",
                "o_ln_b", "o_w_out", "o_norm_post")
E_IN = 7200
O_IN = 6144


def kernel(x, e_norm_pre, e_w_in, e_conv_w, e_conv_b, e_dt_bias, e_a_log, e_d_skip, e_fgate_b, e_ssd_norm, e_w_out, e_norm_post, o_norm_pre, o_w_in, o_conv_w, o_conv_b, o_ln_g, o_ln_b, o_w_out, o_norm_post, loss_target, m_e_norm_pre, m_e_w_in, m_e_conv_w, m_e_conv_b, m_e_dt_bias, m_e_a_log, m_e_d_skip, m_e_fgate_b, m_e_ssd_norm, m_e_w_out, m_e_norm_post, m_o_norm_pre, m_o_w_in, m_o_conv_w, m_o_conv_b, m_o_ln_g, m_o_ln_b, m_o_w_out, m_o_norm_post, v_e_norm_pre, v_e_w_in, v_e_conv_w, v_e_conv_b, v_e_dt_bias, v_e_a_log, v_e_d_skip, v_e_fgate_b, v_e_ssd_norm, v_e_w_out, v_e_norm_post, v_o_norm_pre, v_o_w_in, v_o_conv_w, v_o_conv_b, v_o_ln_g, v_o_ln_b, v_o_w_out, v_o_norm_post):
    given = dict(locals())
    w_in = {n: given[n] for n in WEIGHT_ORDER}
    m_in = {n: given["m_" + n] for n in WEIGHT_ORDER}
    v_in = {n: given["v_" + n] for n in WEIGHT_ORDER}

    def mat(a):
        return a.reshape(a.shape[-2:])

    xs = mat(x)
    tgt = mat(loss_target)
    xi, yi, ci = _position()
    me = 4 * xi + 2 * yi + ci
    ew, ow = E_IN // N_DEV, O_IN // N_DEV
    wr = D_CONV // N_DEV

    big_local = jnp.concatenate([
        mat(e_w_in).astype(bf16).reshape(ew, D_MODEL), mat(o_w_in).astype(bf16).reshape(ow, D_MODEL),
        mat(e_w_out).astype(bf16), mat(o_w_out).astype(bf16)], axis=0)
    nbig = big_local.shape[0]
    big_local = jnp.pad(big_local, ((0, -nbig % 16), (0, 0)))
    wg = _all_gather(big_local, "gather_weights")
    o0, o1, o2, o3 = 0, ew, ew + ow, ew + ow + wr
    e_w_in_f = _gathered_cols(wg[:, o0:o1].reshape(N_DEV, D_MODEL, ew))
    o_w_in_f = _gathered_cols(wg[:, o1:o2].reshape(N_DEV, D_MODEL, ow))
    e_w_out_f = wg[:, o2:o3].reshape(D_CONV, D_MODEL)
    o_w_out_f = wg[:, o3:o3 + wr].reshape(D_CONV, D_MODEL)
    w_z, w_xbc = e_w_in_f[:, 0:2048], e_w_in_f[:, 2048:4096]
    w_qkv = e_w_in_f[:, 4112:7184]
    w_dtf = jnp.concatenate([e_w_in_f[:, 4096:4112], e_w_in_f[:, 7184:7200], jnp.zeros((D_MODEL, 96), bf16)], axis=1)

    sharded_small = [(n, k, w) for n, k, w, sh in SMALL_PARAMS if sh]
    sg = _all_gather(_pack([mat(w_in[n]) for n, _, _ in sharded_small]), "gather_small_weights")
    full_small = {n: _gathered_cols(a)
                  for (n, _, _), a in zip(sharded_small, _unpack(sg, [(k, w // N_DEV) for _, k, w in sharded_small]))}
    for n, _, _, sh in SMALL_PARAMS:
        if not sh:
            full_small[n] = mat(w_in[n])
    p = full_small
    prm = jnp.zeros((8, 128), f32)
    prm = prm.at[0, 0:16].set(p["e_dt_bias"][0]).at[1, 0:16].set(p["e_a_log"][0]).at[2, 0:16].set(p["e_d_skip"][0])
    prm = prm.at[3, F_LANE:F_LANE + 16].set(p["e_fgate_b"][0])

    u0 = _rms_fwd(xs, p["e_norm_pre"], "rms_pre0")
    z0 = _mm_nn(u0, w_z, bf16, "proj0_z")
    xraw = _mm_nn(u0, w_xbc, bf16, "proj0_xbc")
    qkv = _mm_nn(u0, w_qkv, bf16, "proj0_qkv")
    dtf = _mm_nn(u0, w_dtf, f32, "proj0_dtf")
    pre, act = _conv_ssd_fwd(xraw, p["e_conv_w"], p["e_conv_b"])
    y, hs = _ssd_fwd(act, dtf, prm)
    qa, ka = _fox_prep(qkv, _fox_cumsum(dtf, prm))
    o, lse = _fox_fwd(qa, ka, qkv)
    cat = _gate0_fwd(y, z0, o, p["e_ssd_norm"])
    out0 = _mm_nn(cat, e_w_out_f, f32, "out0")
    x1, u1 = _post0_pre1(xs, out0, p["e_norm_post"], p["o_norm_pre"])

    proj1 = _mm_nn(u1, o_w_in_f, bf16, "proj1")
    hc = _conv_glu_fwd(proj1, p["o_conv_w"], p["o_conv_b"])
    h3 = _ln_gate_fwd(hc, proj1, p["o_ln_g"], p["o_ln_b"])
    out1 = _mm_nn(h3, o_w_out_f, f32, "out1")
    dy, d_out1, dg_post1, loss_part = _final_loss(x1, out1, tgt, p["o_norm_post"])

    dh3 = _mm_nt([(d_out1, 0, o_w_out_f, 0, D_MODEL)], bf16, "dh3")
    g_o_w_out = _mm_tn(h3, d_out1, "dw_out1")
    dhc, dz1, dg_ln, db_ln = _ln_gate_bwd(hc, proj1, dh3, p["o_ln_g"], p["o_ln_b"])
    dval, dgate, dw_conv1, db_conv1 = _conv_glu_bwd(dhc, proj1, p["o_conv_w"])
    dproj1 = jnp.concatenate([dval, dgate, dz1], axis=1)
    du1 = _mm_nt([(dproj1, 0, o_w_in_f, 0, O_IN)], f32, "du1")
    g_o_w_in = _mm_tn(u1, dproj1, "dw_in1", tn=ow, blocked=True)
    dx1, d_out0, dg_pre1, dg_post0 = _mid_bwd(x1, du1, dy, out0, p["o_norm_pre"], p["e_norm_post"])

    dcat = _mm_nt([(d_out0, 0, e_w_out_f, 0, D_MODEL)], bf16, "dcat")
    g_e_w_out = _mm_tn(cat, d_out0, "dw_out0")
    dy_ssd, do, dz0, delta, dg_ssd_norm = _gate0_bwd(y, z0, o, dcat, p["e_ssd_norm"])
    dq_hm, dk_hm, dv = _fox_bwd(qa, ka, qkv, do, lse, delta[0:N_HEADS].reshape(N_HEADS, 1, -1))
    dq, dk, dc = _fox_bwd_post(dq_hm, dk_hm)
    dpre, ddt_raw, dprm = _ssd_bwd(act, pre, dtf, prm, hs, dy_ssd)
    ddtf, dfb = _fox_gate_bwd(dc, dtf, prm, ddt_raw)
    dxraw, dw_conv0, db_conv0 = _conv_ssd_bwd(dpre, xraw, p["e_conv_w"])
    du0 = _mm_nt([(dz0, 0, w_z, 0, 2048), (dxraw, 0, w_xbc, 0, 2048), (dq, 0, w_qkv, 0, 1024), (dk, 0, w_qkv, 1, 1024),
                  (dv, 0, w_qkv, 2, 1024), (ddtf, 0, w_dtf, 0, 128)], f32, "du0")
    gw_dtf = _mm_tn(u0, ddtf, "dw_in0_dtf")
    g_e_w_in_full = jnp.concatenate([
        _mm_tn(u0, dz0, "dw_in0_z"), _mm_tn(u0, dxraw, "dw_in0_xbc"), gw_dtf[:, 0:16],
        _mm_tn(u0, dq, "dw_in0_q"), _mm_tn(u0, dk, "dw_in0_k"), _mm_tn(u0, dv, "dw_in0_v"), gw_dtf[:, 16:32]], axis=1)
    grad_x, dg_pre0 = _first_bwd(xs, du0, dx1, p["e_norm_pre"])

    big_parts = _grad_exchange([
        _col_shards(g_e_w_in_full).astype(bf16), g_e_w_out.reshape(N_DEV, wr, D_MODEL).astype(bf16),
        g_o_w_in.astype(bf16), g_o_w_out.reshape(N_DEV, wr, D_MODEL).astype(bf16)])
    outs = {}
    for n, parts in zip(BIG_PARAMS, big_parts):
        outs[n] = _adamw(mat(w_in[n]), mat(m_in[n]), mat(v_in[n]), parts, "adamw_" + n)

    small_grads = {
        "e_norm_pre": dg_pre0, "e_conv_w": dw_conv0, "e_conv_b": db_conv0, "e_dt_bias": dprm[0:1, 0:16],
        "e_a_log": dprm[1:2, 0:16], "e_d_skip": dprm[2:3, 0:16], "e_fgate_b": dfb[:, F_LANE:F_LANE + 16],
        "e_ssd_norm": dg_ssd_norm, "e_norm_post": dg_post0, "o_norm_pre": dg_pre1, "o_conv_w": dw_conv1,
        "o_conv_b": db_conv1, "o_ln_g": dg_ln, "o_ln_b": db_ln, "o_norm_post": dg_post1,
    }
    gathered = _all_gather(_pack([small_grads[n] for n, _, _, _ in SMALL_PARAMS] + [loss_part]), "gather_small_grads")
    summed = _unpack(_sum_parts(gathered, "sum_small_grads"), [(k, w) for _, k, w, _ in SMALL_PARAMS] + [(1, 128)])
    loss = summed[-1][0, 0]
    g_local = []
    for (n, k, w, sh), g in zip(SMALL_PARAMS, summed):
        g_local.append(lax.dynamic_slice_in_dim(g, me * (w // N_DEV), w // N_DEV, axis=1) if sh else g)
    names = [n for n, _, _, _ in SMALL_PARAMS]
    local_shapes = [(k, w // N_DEV if sh else w) for _, k, w, sh in SMALL_PARAMS]
    res = _adamw(_pack([mat(w_in[n]) for n in names]), _pack([mat(m_in[n]) for n in names]),
                 _pack([mat(v_in[n]) for n in names]), _pack(g_local)[None], "adamw_small", tr=8)
    unpacked = [_unpack(r, local_shapes) for r in res]
    for i, n in enumerate(names):
        outs[n] = tuple(u[i] for u in unpacked)

    ret = [loss, grad_x.reshape(x.shape)]
    for j in range(4):
        ret += [outs[n][j].reshape(w_in[n].shape) for n in WEIGHT_ORDER]
    return tuple(ret)
```

```python
import jax
import jax.numpy as jnp
from jax import lax
from jax.experimental import pallas as pl
from jax.experimental.pallas import tpu as pltpu

f32 = jnp.float32
bf16 = jnp.bfloat16

N_DEV = 8
D_MODEL = 1024
N_HEADS = 16
HEAD_DIM = 64
N_GROUPS = 4
HEADS_PER_GROUP = 4
D_STATE = 128
CHUNK = 512
SSD_CONV = 4
CONV_WIDTH = 31
D_CONV = 2048
EPS = 1e-6
XBC_W = 2048
B_OFF = 1024
C_OFF = 1536
F_LANE = 16
HALO = 32

ADAM_LR = 0.001
ADAM_B1 = 0.9
ADAM_B2 = 0.999
ADAM_EPS = 1e-08
ADAM_WD = 0.01
ADAM_STEP = 10

VMEM_LIMIT_BYTES = 56 * 1024 * 1024
ROW_TILE = 512
CONV_ROW_TILE = 256
CONV_COL_TILE = 512
CONV_SUB = 32
ATTN_TILE = 512
ATTN_FWD_TILE = 1024

NT = (((1,), (1,)), ((), ()))
TN = (((0,), (0,)), ((), ()))
HIGHEST = lax.Precision.HIGHEST
NEG = -1e30


def _cp(*sem):
    return pltpu.CompilerParams(dimension_semantics=sem if sem else None, vmem_limit_bytes=VMEM_LIMIT_BYTES)


def _sigmoid(x):
    return jax.nn.sigmoid(x)


def _silu(x):
    return x * _sigmoid(x)


def _dsilu(x):
    s = _sigmoid(x)
    return s * (1.0 + x * (1.0 - s))


def _softplus(x):
    return jnp.maximum(x, 0.0) + jnp.log(1.0 + jnp.exp(-jnp.abs(x)))


def _log_sigmoid(x):
    return jnp.minimum(x, 0.0) - jnp.log(1.0 + jnp.exp(-jnp.abs(x)))


def _dot(a, b, dims=None, precision=None):
    if dims is None:
        return jnp.dot(a, b, preferred_element_type=f32, precision=precision)
    return lax.dot_general(a, b, dims, preferred_element_type=f32, precision=precision)


def _mm_nn(a, b, out_dtype, name, tm=512, tn=1024):
    m, k = a.shape
    n = b.shape[1]
    tm, tn = min(tm, m), min(tn, n)

    def body(a_ref, b_ref, o_ref):
        o_ref[...] = _dot(a_ref[...], b_ref[...]).astype(o_ref.dtype)

    return pl.pallas_call(
        body, name=name, grid=(n // tn, m // tm),
        in_specs=[pl.BlockSpec((tm, k), lambda j, i: (i, 0)), pl.BlockSpec((k, tn), lambda j, i: (0, j))],
        out_specs=pl.BlockSpec((tm, tn), lambda j, i: (i, j)),
        out_shape=jax.ShapeDtypeStruct((m, n), out_dtype), compiler_params=_cp("parallel", "parallel"))(a, b)


def _mm_nt(pairs, out_dtype, name, tm=512, tn=512):
    m = pairs[0][0].shape[0]
    n = pairs[0][2].shape[0]
    tm, tn = min(tm, m), min(tn, n)
    npair = len(pairs)

    def body(*refs):
        o_ref = refs[-1]
        acc = None
        for p in range(npair):
            d = _dot(refs[2 * p][...].astype(bf16), refs[2 * p + 1][...], NT)
            acc = d if acc is None else acc + d
        o_ref[...] = acc.astype(o_ref.dtype)

    in_specs, args = [], []
    for a, acb, b, bcb, k in pairs:
        in_specs.append(pl.BlockSpec((tm, k), lambda j, i, acb=acb: (i, acb)))
        in_specs.append(pl.BlockSpec((tn, k), lambda j, i, bcb=bcb: (j, bcb)))
        args += [a, b]
    return pl.pallas_call(
        body, name=name, grid=(n // tn, m // tm), in_specs=in_specs,
        out_specs=pl.BlockSpec((tm, tn), lambda j, i: (i, j)),
        out_shape=jax.ShapeDtypeStruct((m, n), out_dtype), compiler_params=_cp("parallel", "parallel"))(*args)


def _mm_tn(a, b, name, a_cb=0, am=None, b_cb=0, bn=None, tn=1024, tk=512, blocked=False):
    k = a.shape[0]
    am = a.shape[1] if am is None else am
    bn = b.shape[1] if bn is None else bn
    tm = min(1024, am)
    tn, tk = min(tn, bn), min(tk, k)
    a_off, b_off = a_cb * (am // tm), b_cb * (bn // tn)

    def body(a_ref, b_ref, o_ref):
        @pl.when(pl.program_id(2) == 0)
        def _():
            o_ref[...] = jnp.zeros_like(o_ref)
        d = _dot(a_ref[...].astype(bf16), b_ref[...].astype(bf16), TN)
        o_ref[...] += d.reshape(o_ref.shape)

    if blocked:
        out_spec = pl.BlockSpec((1, tm, tn), lambda i, j, kk: (j, i, 0))
        out_shape = jax.ShapeDtypeStruct((bn // tn, am, tn), f32)
    else:
        out_spec = pl.BlockSpec((tm, tn), lambda i, j, kk: (i, j))
        out_shape = jax.ShapeDtypeStruct((am, bn), f32)
    return pl.pallas_call(
        body, name=name, grid=(am // tm, bn // tn, k // tk),
        in_specs=[pl.BlockSpec((tk, tm), lambda i, j, kk: (kk, a_off + i)),
                  pl.BlockSpec((tk, tn), lambda i, j, kk: (kk, b_off + j))],
        out_specs=out_spec, out_shape=out_shape,
        compiler_params=_cp("parallel", "parallel", "arbitrary"))(a, b)


def _rowspec(ts, w, cb=0):
    return pl.BlockSpec((ts, w), lambda i: (i, cb))


def _vecspec(w):
    return pl.BlockSpec((1, w), lambda i: (0, 0))


def _rms_fwd(x, g, name):
    s, d = x.shape
    ts = min(ROW_TILE, s)

    def body(x_ref, g_ref, u_ref):
        xv = x_ref[...]
        r = lax.rsqrt(jnp.mean(xv * xv, axis=-1, keepdims=True) + EPS)
        u_ref[...] = (xv * r * g_ref[...]).astype(bf16)

    return pl.pallas_call(
        body, name=name, grid=(s // ts,), in_specs=[_rowspec(ts, d), _vecspec(d)], out_specs=_rowspec(ts, d),
        out_shape=jax.ShapeDtypeStruct((s, d), bf16), compiler_params=_cp("parallel"))(x, g)


def _rms_bwd_vals(xv, g, dy):
    r = lax.rsqrt(jnp.mean(xv * xv, axis=-1, keepdims=True) + EPS)
    xh = xv * r
    dg = jnp.sum(dy * xh, axis=0, keepdims=True)
    dxh = dy * g
    dx = r * (dxh - xh * jnp.mean(dxh * xh, axis=-1, keepdims=True))
    return dx, dg


def _gate0_fwd(y, z, o, ssd_norm):
    s = y.shape[0]
    ts = min(ROW_TILE, s)
    gw = D_MODEL // N_GROUPS

    def body(y_ref, zs_ref, zf_ref, o_ref, w_ref, cat_ref):
        yg = y_ref[...].astype(f32) * _silu(zs_ref[...].astype(f32))
        for g in range(N_GROUPS):
            seg = yg[:, gw * g:gw * (g + 1)]
            r = lax.rsqrt(jnp.mean(seg * seg, axis=-1, keepdims=True) + EPS)
            cat_ref[:, gw * g:gw * (g + 1)] = (seg * r * w_ref[:, gw * g:gw * (g + 1)]).astype(bf16)
        cat_ref[:, D_MODEL:] = (o_ref[...].astype(f32) * _silu(zf_ref[...].astype(f32))).astype(bf16)

    return pl.pallas_call(
        body, name="gate0_fwd", grid=(s // ts,),
        in_specs=[_rowspec(ts, D_MODEL), _rowspec(ts, D_MODEL, 0), _rowspec(ts, D_MODEL, 1), _rowspec(ts, D_MODEL),
                  _vecspec(D_MODEL)],
        out_specs=_rowspec(ts, 2 * D_MODEL),
        out_shape=jax.ShapeDtypeStruct((s, 2 * D_MODEL), bf16), compiler_params=_cp("parallel"))(y, z, z, o, ssd_norm)


def _post0_pre1(x, out0, g_post0, g_pre1):
    s, d = x.shape
    ts = min(ROW_TILE, s)

    def body(x_ref, o_ref, gp_ref, gn_ref, x1_ref, u1_ref):
        ov = o_ref[...]
        r = lax.rsqrt(jnp.mean(ov * ov, axis=-1, keepdims=True) + EPS)
        x1 = x_ref[...] + ov * r * gp_ref[...]
        x1_ref[...] = x1
        r1 = lax.rsqrt(jnp.mean(x1 * x1, axis=-1, keepdims=True) + EPS)
        u1_ref[...] = (x1 * r1 * gn_ref[...]).astype(bf16)

    return pl.pallas_call(
        body, name="post0_pre1", grid=(s // ts,),
        in_specs=[_rowspec(ts, d), _rowspec(ts, d), _vecspec(d), _vecspec(d)],
        out_specs=[_rowspec(ts, d), _rowspec(ts, d)],
        out_shape=[jax.ShapeDtypeStruct((s, d), f32), jax.ShapeDtypeStruct((s, d), bf16)],
        compiler_params=_cp("parallel"))(x, out0, g_post0, g_pre1)


def _ln_vals(hc, g, b):
    mu = jnp.mean(hc, axis=-1, keepdims=True)
    xc = hc - mu
    rstd = lax.rsqrt(jnp.mean(xc * xc, axis=-1, keepdims=True) + EPS)
    xh = xc * rstd
    return xh, rstd, xh * g + b


def _ln_gate_fwd(hc, proj1, ln_g, ln_b):
    s = hc.shape[0]
    ts = min(ROW_TILE, s)

    def body(hc_ref, z_ref, g_ref, b_ref, h3_ref):
        _, _, ln = _ln_vals(hc_ref[...].astype(f32), g_ref[...], b_ref[...])
        h3_ref[...] = (_silu(ln) * _silu(z_ref[...].astype(f32))).astype(bf16)

    return pl.pallas_call(
        body, name="ln_gate_fwd", grid=(s // ts,),
        in_specs=[_rowspec(ts, D_CONV), _rowspec(ts, D_CONV, 2), _vecspec(D_CONV), _vecspec(D_CONV)],
        out_specs=_rowspec(ts, D_CONV),
        out_shape=jax.ShapeDtypeStruct((s, D_CONV), bf16), compiler_params=_cp("parallel"))(hc, proj1, ln_g, ln_b)


def _final_loss(x1, out1, tgt, g_post1):
    s, d = x1.shape
    ts = min(ROW_TILE, s)

    def body(x1_ref, o_ref, t_ref, g_ref, dy_ref, do_ref, dg_ref, loss_ref):
        i = pl.program_id(0)

        @pl.when(i == 0)
        def _():
            dg_ref[...] = jnp.zeros_like(dg_ref)
            loss_ref[...] = jnp.zeros_like(loss_ref)
        ov = o_ref[...]
        g = g_ref[...]
        r = lax.rsqrt(jnp.mean(ov * ov, axis=-1, keepdims=True) + EPS)
        diff = x1_ref[...] + ov * r * g - t_ref[...]
        row = jnp.mean(diff * diff, axis=-1, keepdims=True)
        loss_ref[...] += jnp.broadcast_to(0.5 * jnp.sum(row, axis=0, keepdims=True), loss_ref.shape)
        dy = diff * (1.0 / d)
        dy_ref[...] = dy
        dx, dg = _rms_bwd_vals(ov, g, dy)
        do_ref[...] = dx.astype(bf16)
        dg_ref[...] += dg

    return pl.pallas_call(
        body, name="final_loss", grid=(s // ts,),
        in_specs=[_rowspec(ts, d), _rowspec(ts, d), _rowspec(ts, d), _vecspec(d)],
        out_specs=[_rowspec(ts, d), _rowspec(ts, d), _vecspec(d), _vecspec(128)],
        out_shape=[jax.ShapeDtypeStruct((s, d), f32), jax.ShapeDtypeStruct((s, d), bf16),
                   jax.ShapeDtypeStruct((1, d), f32), jax.ShapeDtypeStruct((1, 128), f32)],
        compiler_params=_cp("arbitrary"))(x1, out1, tgt, g_post1)


def _ln_gate_bwd(hc, proj1, dh3, ln_g, ln_b):
    s = hc.shape[0]
    ts = min(ROW_TILE, s)

    def body(hc_ref, z_ref, dh_ref, g_ref, b_ref, dhc_ref, dz_ref, dg_ref, db_ref):
        @pl.when(pl.program_id(0) == 0)
        def _():
            dg_ref[...] = jnp.zeros_like(dg_ref)
            db_ref[...] = jnp.zeros_like(db_ref)
        g = g_ref[...]
        xh, rstd, ln = _ln_vals(hc_ref[...].astype(f32), g, b_ref[...])
        zv = z_ref[...].astype(f32)
        dh3 = dh_ref[...].astype(f32)
        dz_ref[...] = (dh3 * _silu(ln) * _dsilu(zv)).astype(bf16)
        dln = dh3 * _silu(zv) * _dsilu(ln)
        dg_ref[...] += jnp.sum(dln * xh, axis=0, keepdims=True)
        db_ref[...] += jnp.sum(dln, axis=0, keepdims=True)
        dxh = dln * g
        dhc = rstd * (dxh - jnp.mean(dxh, axis=-1, keepdims=True) - xh * jnp.mean(dxh * xh, axis=-1, keepdims=True))
        dhc_ref[...] = dhc.astype(bf16)

    return pl.pallas_call(
        body, name="ln_gate_bwd", grid=(s // ts,),
        in_specs=[_rowspec(ts, D_CONV), _rowspec(ts, D_CONV, 2), _rowspec(ts, D_CONV), _vecspec(D_CONV),
                  _vecspec(D_CONV)],
        out_specs=[_rowspec(ts, D_CONV), _rowspec(ts, D_CONV), _vecspec(D_CONV), _vecspec(D_CONV)],
        out_shape=[jax.ShapeDtypeStruct((s, D_CONV), bf16), jax.ShapeDtypeStruct((s, D_CONV), bf16),
                   jax.ShapeDtypeStruct((1, D_CONV), f32), jax.ShapeDtypeStruct((1, D_CONV), f32)],
        compiler_params=_cp("arbitrary"))(hc, proj1, dh3, ln_g, ln_b)


def _mid_bwd(x1, du1, dy, out0, g_pre1, g_post0):
    s, d = x1.shape
    ts = min(ROW_TILE, s)

    def body(x1_ref, du_ref, dy_ref, o_ref, gn_ref, gp_ref, dx1_ref, do_ref, dgn_ref, dgp_ref):
        @pl.when(pl.program_id(0) == 0)
        def _():
            dgn_ref[...] = jnp.zeros_like(dgn_ref)
            dgp_ref[...] = jnp.zeros_like(dgp_ref)
        dxa, dgn = _rms_bwd_vals(x1_ref[...], gn_ref[...], du_ref[...])
        dx1 = dy_ref[...] + dxa
        dx1_ref[...] = dx1
        dgn_ref[...] += dgn
        dxo, dgp = _rms_bwd_vals(o_ref[...], gp_ref[...], dx1)
        do_ref[...] = dxo.astype(bf16)
        dgp_ref[...] += dgp

    return pl.pallas_call(
        body, name="mid_bwd", grid=(s // ts,),
        in_specs=[_rowspec(ts, d)] * 4 + [_vecspec(d), _vecspec(d)],
        out_specs=[_rowspec(ts, d), _rowspec(ts, d), _vecspec(d), _vecspec(d)],
        out_shape=[jax.ShapeDtypeStruct((s, d), f32), jax.ShapeDtypeStruct((s, d), bf16),
                   jax.ShapeDtypeStruct((1, d), f32), jax.ShapeDtypeStruct((1, d), f32)],
        compiler_params=_cp("arbitrary"))(x1, du1, dy, out0, g_pre1, g_post0)


def _first_bwd(x, du0, dx1, g_pre0):
    s, d = x.shape
    ts = min(ROW_TILE, s)

    def body(x_ref, du_ref, dx1_ref, g_ref, dx_ref, dg_ref):
        @pl.when(pl.program_id(0) == 0)
        def _():
            dg_ref[...] = jnp.zeros_like(dg_ref)
        dxa, dg = _rms_bwd_vals(x_ref[...], g_ref[...], du_ref[...])
        dx_ref[...] = dx1_ref[...] + dxa
        dg_ref[...] += dg

    return pl.pallas_call(
        body, name="first_bwd", grid=(s // ts,),
        in_specs=[_rowspec(ts, d)] * 3 + [_vecspec(d)],
        out_specs=[_rowspec(ts, d), _vecspec(d)],
        out_shape=[jax.ShapeDtypeStruct((s, d), f32), jax.ShapeDtypeStruct((1, d), f32)],
        compiler_params=_cp("arbitrary"))(x, du0, dx1, g_pre0)


def _gate0_bwd(y, z, o, dcat, ssd_norm):
    s = y.shape[0]
    ts = min(ROW_TILE, s)
    gw = D_MODEL // N_GROUPS

    def body(y_ref, zs_ref, zf_ref, o_ref, dn_ref, dg_ref, w_ref, dy_ref, do_ref, dz_ref, delta_ref, dw_ref):
        @pl.when(pl.program_id(0) == 0)
        def _():
            dw_ref[...] = jnp.zeros_like(dw_ref)
        yv = y_ref[...].astype(f32)
        zs = zs_ref[...].astype(f32)
        sz = _silu(zs)
        yg = yv * sz
        dyn = dn_ref[...].astype(f32)
        for g in range(N_GROUPS):
            sl = slice(gw * g, gw * (g + 1))
            seg = yg[:, sl]
            r = lax.rsqrt(jnp.mean(seg * seg, axis=-1, keepdims=True) + EPS)
            yh = seg * r
            dn = dyn[:, sl]
            dw_ref[:, sl] += jnp.sum(dn * yh, axis=0, keepdims=True)
            dyh = dn * w_ref[:, sl]
            dyg = r * (dyh - yh * jnp.mean(dyh * yh, axis=-1, keepdims=True))
            dy_ref[:, sl] = (dyg * sz[:, sl]).astype(bf16)
            dz_ref[:, sl] = (dyg * yv[:, sl] * _dsilu(zs[:, sl])).astype(bf16)
        zf = zf_ref[...].astype(f32)
        ov = o_ref[...].astype(f32)
        dog = dg_ref[...].astype(f32)
        dov = (dog * _silu(zf)).astype(bf16)
        do_ref[...] = dov
        dz_ref[:, D_MODEL:] = (dog * ov * _dsilu(zf)).astype(bf16)
        prod = dov.astype(f32) * ov
        lane = lax.broadcasted_iota(jnp.int32, (ts, 128), 1)
        delta = jnp.zeros((ts, 128), f32)
        for h in range(N_HEADS):
            dh = jnp.sum(prod[:, HEAD_DIM * h:HEAD_DIM * (h + 1)], axis=-1, keepdims=True)
            delta = delta + jnp.where(lane == h, dh, 0.0)
        delta_ref[...] = delta.T

    return pl.pallas_call(
        body, name="gate0_bwd", grid=(s // ts,),
        in_specs=[_rowspec(ts, D_MODEL), _rowspec(ts, D_MODEL, 0), _rowspec(ts, D_MODEL, 1), _rowspec(ts, D_MODEL),
                  _rowspec(ts, D_MODEL, 0), _rowspec(ts, D_MODEL, 1), _vecspec(D_MODEL)],
        out_specs=[_rowspec(ts, D_MODEL), _rowspec(ts, D_MODEL), _rowspec(ts, 2 * D_MODEL),
                   pl.BlockSpec((128, ts), lambda i: (0, i)), _vecspec(D_MODEL)],
        out_shape=[jax.ShapeDtypeStruct((s, D_MODEL), bf16), jax.ShapeDtypeStruct((s, D_MODEL), bf16),
                   jax.ShapeDtypeStruct((s, 2 * D_MODEL), bf16), jax.ShapeDtypeStruct((128, s), f32),
                   jax.ShapeDtypeStruct((1, D_MODEL), f32)],
        compiler_params=_cp("arbitrary"))(y, z, z, o, dcat, dcat, ssd_norm)


def _conv_grid(s, c):
    ts, cb = min(CONV_ROW_TILE, s), min(CONV_COL_TILE, c)
    return ts, cb, (c // cb, s // ts)


def _cur(ts, cb, off=0):
    return pl.BlockSpec((ts, cb), lambda c, i: (i, c + off))


def _prev_halo(ts, cb, off=0):
    return pl.BlockSpec((HALO, cb), lambda c, i: (jnp.maximum(i * (ts // HALO) - 1, 0), c + off))


def _next_halo(ts, cb, s, off=0):
    return pl.BlockSpec((HALO, cb), lambda c, i: (jnp.minimum((i + 1) * (ts // HALO), s // HALO - 1), c + off))


def _wspec(k, cb):
    return pl.BlockSpec((k, cb), lambda c, i: (0, c))


def _phases(offsets):
    return sorted({o % 8 for o in offsets} - {0})


def _shift_scratch(offsets, ts, cb):
    return pltpu.VMEM((max(len(_phases(offsets)), 1), ts + HALO - 8, cb), f32)


def _fill_phases(ext_ref, sh_ref, offsets, ts):
    for j, r in enumerate(_phases(offsets)):
        sh_ref[j] = ext_ref[pl.ds(r, ts + HALO - 8), :]


def _slab(ext_ref, sh_ref, offsets, off, start):
    r = off % 8
    a = off - r + start
    if r == 0:
        return ext_ref[a:a + CONV_SUB, :]
    return sh_ref[_phases(offsets).index(r), a:a + CONV_SUB, :]


def _conv_taps(ext_ref, sh_ref, w_ref, b_ref, ts, k_taps, emit):
    offsets = [HALO - (k_taps - 1) + k for k in range(k_taps)]
    _fill_phases(ext_ref, sh_ref, offsets, ts)
    for sb in range(ts // CONV_SUB):
        acc = b_ref[...]
        for k in range(k_taps):
            acc = acc + w_ref[k:k + 1, :] * _slab(ext_ref, sh_ref, offsets, offsets[k], sb * CONV_SUB)
        emit(slice(sb * CONV_SUB, (sb + 1) * CONV_SUB), acc)


def _conv_ssd_fwd(xraw, w, b):
    s, c = xraw.shape
    ts, cb, grid = _conv_grid(s, c)
    offsets = [HALO - (SSD_CONV - 1) + k for k in range(SSD_CONV)]

    def body(x_ref, xh_ref, w_ref, b_ref, pre_ref, act_ref, ext_ref, sh_ref):
        first = pl.program_id(1) == 0
        ext_ref[0:HALO, :] = jnp.where(first, 0.0, xh_ref[...].astype(f32))
        ext_ref[HALO:, :] = x_ref[...].astype(f32)

        def emit(rows, pre):
            pre_ref[rows, :] = pre.astype(bf16)
            act_ref[rows, :] = _silu(pre).astype(bf16)
        _conv_taps(ext_ref, sh_ref, w_ref, b_ref, ts, SSD_CONV, emit)

    return pl.pallas_call(
        body, name="conv_ssd_fwd", grid=grid,
        in_specs=[_cur(ts, cb), _prev_halo(ts, cb), _wspec(SSD_CONV, cb), _wspec(1, cb)],
        out_specs=[_cur(ts, cb), _cur(ts, cb)],
        out_shape=[jax.ShapeDtypeStruct((s, c), bf16)] * 2,
        scratch_shapes=[pltpu.VMEM((HALO + ts, cb), f32), _shift_scratch(offsets, ts, cb)],
        compiler_params=_cp("parallel", "parallel"))(xraw, xraw, w, b)


def _conv_glu_fwd(proj1, w, b):
    s = proj1.shape[0]
    c = D_CONV
    ts, cb, grid = _conv_grid(s, c)
    goff = c // cb

    offsets = [HALO - (CONV_WIDTH - 1) + k for k in range(CONV_WIDTH)]

    def body(v_ref, g_ref, vh_ref, gh_ref, w_ref, b_ref, hc_ref, ext_ref, sh_ref):
        first = pl.program_id(1) == 0
        hh = vh_ref[...].astype(f32) * _sigmoid(gh_ref[...].astype(f32))
        ext_ref[0:HALO, :] = jnp.where(first, 0.0, hh)
        ext_ref[HALO:, :] = v_ref[...].astype(f32) * _sigmoid(g_ref[...].astype(f32))

        def emit(rows, hc):
            hc_ref[rows, :] = hc.astype(bf16)
        _conv_taps(ext_ref, sh_ref, w_ref, b_ref, ts, CONV_WIDTH, emit)

    return pl.pallas_call(
        body, name="conv_glu_fwd", grid=grid,
        in_specs=[_cur(ts, cb), _cur(ts, cb, goff), _prev_halo(ts, cb), _prev_halo(ts, cb, goff),
                  _wspec(CONV_WIDTH, cb), _wspec(1, cb)],
        out_specs=_cur(ts, cb),
        out_shape=jax.ShapeDtypeStruct((s, c), bf16),
        scratch_shapes=[pltpu.VMEM((HALO + ts, cb), f32), _shift_scratch(offsets, ts, cb)],
        compiler_params=_cp("parallel", "parallel"))(proj1, proj1, proj1, proj1, w, b)


def _conv_bwd_offsets(k_taps):
    return [k_taps - 1 - k for k in range(k_taps)], [HALO - (k_taps - 1) + k for k in range(k_taps)]


def _conv_bwd_scratch(k_taps, ts, cb):
    d_offs, x_offs = _conv_bwd_offsets(k_taps)
    return [pltpu.VMEM((ts + HALO, cb), f32), _shift_scratch(d_offs, ts, cb),
            pltpu.VMEM((HALO + ts, cb), f32), _shift_scratch(x_offs, ts, cb),
            pltpu.VMEM((k_taps, 8, cb), f32), pltpu.VMEM((8, cb), f32)]


def _conv_bwd_core(dp, dpn_ref, last, w_ref, scratch, dw_ref, db_ref, ts, k_taps, emit):
    dext_ref, dsh_ref, xext_ref, xsh_ref, dw8_ref, db8_ref = scratch
    d_offs, x_offs = _conv_bwd_offsets(k_taps)
    dext_ref[0:ts, :] = dp
    dext_ref[ts:, :] = jnp.where(last, 0.0, dpn_ref[...].astype(f32))
    _fill_phases(dext_ref, dsh_ref, d_offs, ts)
    _fill_phases(xext_ref, xsh_ref, x_offs, ts)

    @pl.when(pl.program_id(1) == 0)
    def _():
        dw8_ref[...] = jnp.zeros_like(dw8_ref)
        db8_ref[...] = jnp.zeros_like(db8_ref)
    cb = dp.shape[1]
    for sb in range(ts // CONV_SUB):
        start = sb * CONV_SUB
        dpv = dext_ref[start:start + CONV_SUB, :]
        dx = None
        for k in range(k_taps):
            t = w_ref[k:k + 1, :] * _slab(dext_ref, dsh_ref, d_offs, d_offs[k], start)
            dx = t if dx is None else dx + t
            prod = dpv * _slab(xext_ref, xsh_ref, x_offs, x_offs[k], start)
            dw8_ref[k] += jnp.sum(prod.reshape(CONV_SUB // 8, 8, cb), axis=0)
        db8_ref[...] += jnp.sum(dpv.reshape(CONV_SUB // 8, 8, cb), axis=0)
        emit(slice(start, start + CONV_SUB), dx)

    @pl.when(last)
    def _():
        dw_ref[...] = jnp.sum(dw8_ref[...], axis=1)
        db_ref[...] = jnp.sum(db8_ref[...], axis=0, keepdims=True)


def _conv_ssd_bwd(dpre, xraw, w):
    s, c = xraw.shape
    ts, cb, grid = _conv_grid(s, c)
    nb = s // ts

    def body(dp_ref, dpn_ref, x_ref, xh_ref, w_ref, dx_ref, dw_ref, db_ref, *scratch):
        i = pl.program_id(1)
        xext_ref = scratch[2]
        xext_ref[0:HALO, :] = jnp.where(i == 0, 0.0, xh_ref[...].astype(f32))
        xext_ref[HALO:, :] = x_ref[...].astype(f32)

        def emit(rows, dx):
            dx_ref[rows, :] = dx.astype(bf16)
        _conv_bwd_core(dp_ref[...].astype(f32), dpn_ref, i == nb - 1, w_ref, scratch, dw_ref, db_ref, ts, SSD_CONV, emit)

    return pl.pallas_call(
        body, name="conv_ssd_bwd", grid=grid,
        in_specs=[_cur(ts, cb), _next_halo(ts, cb, s), _cur(ts, cb), _prev_halo(ts, cb), _wspec(SSD_CONV, cb)],
        out_specs=[_cur(ts, cb), _wspec(SSD_CONV, cb), _wspec(1, cb)],
        out_shape=[jax.ShapeDtypeStruct((s, c), bf16), jax.ShapeDtypeStruct((SSD_CONV, c), f32),
                   jax.ShapeDtypeStruct((1, c), f32)],
        scratch_shapes=_conv_bwd_scratch(SSD_CONV, ts, cb),
        compiler_params=_cp("parallel", "arbitrary"))(dpre, dpre, xraw, xraw, w)


def _conv_glu_bwd(dhc, proj1, w):
    s = proj1.shape[0]
    c = D_CONV
    ts, cb, grid = _conv_grid(s, c)
    nb = s // ts
    goff = c // cb

    def body(dp_ref, dpn_ref, v_ref, g_ref, vh_ref, gh_ref, w_ref, dv_ref, dg_ref, dw_ref, db_ref, *scratch):
        i = pl.program_id(1)
        xext_ref = scratch[2]
        xext_ref[0:HALO, :] = jnp.where(i == 0, 0.0, vh_ref[...].astype(f32) * _sigmoid(gh_ref[...].astype(f32)))
        xext_ref[HALO:, :] = v_ref[...].astype(f32) * _sigmoid(g_ref[...].astype(f32))

        def emit(rows, dh):
            val = v_ref[rows, :].astype(f32)
            sg = _sigmoid(g_ref[rows, :].astype(f32))
            dv_ref[rows, :] = (dh * sg).astype(bf16)
            dg_ref[rows, :] = (dh * val * sg * (1.0 - sg)).astype(bf16)
        _conv_bwd_core(dp_ref[...].astype(f32), dpn_ref, i == nb - 1, w_ref, scratch, dw_ref, db_ref, ts, CONV_WIDTH, emit)

    return pl.pallas_call(
        body, name="conv_glu_bwd", grid=grid,
        in_specs=[_cur(ts, cb), _next_halo(ts, cb, s), _cur(ts, cb), _cur(ts, cb, goff), _prev_halo(ts, cb),
                  _prev_halo(ts, cb, goff), _wspec(CONV_WIDTH, cb)],
        out_specs=[_cur(ts, cb), _cur(ts, cb), _wspec(CONV_WIDTH, cb), _wspec(1, cb)],
        out_shape=[jax.ShapeDtypeStruct((s, c), bf16), jax.ShapeDtypeStruct((s, c), bf16),
                   jax.ShapeDtypeStruct((CONV_WIDTH, c), f32), jax.ShapeDtypeStruct((1, c), f32)],
        scratch_shapes=_conv_bwd_scratch(CONV_WIDTH, ts, cb),
        compiler_params=_cp("parallel", "arbitrary"))(dhc, dhc, proj1, proj1, proj1, proj1, w)


def _ssd_common(dt_ref, prm_ref):
    l = CHUNK
    dtb = prm_ref[0:1, :]
    a = -jnp.exp(prm_ref[1:2, :])
    dsk = prm_ref[2:3, :]
    zraw = dt_ref[...] + dtb
    dt = _softplus(zraw)
    da = dt * a
    row = lax.broadcasted_iota(jnp.int32, (l, l), 0)
    col = lax.broadcasted_iota(jnp.int32, (l, l), 1)
    causal = row >= col
    cs = _dot(causal.astype(f32), da, precision=HIGHEST)
    return a, dsk, zraw, dt, cs, cs.T, causal, row, col


def _ssd_fwd(act, dtf, prm):
    s = act.shape[0]
    nc = s // CHUNK
    l = CHUNK

    def body(xs_ref, dt_ref, prm_ref, y_ref, hs_ref, st_ref):
        @pl.when(pl.program_id(0) == 0)
        def _():
            st_ref[...] = jnp.zeros_like(st_ref)
        a, dsk, _, dt, cs, cst, causal, _, _ = _ssd_common(dt_ref, prm_ref)
        for g in range(N_GROUPS):
            bm = xs_ref[:, B_OFF + D_STATE * g:B_OFF + D_STATE * (g + 1)]
            cm = xs_ref[:, C_OFF + D_STATE * g:C_OFF + D_STATE * (g + 1)]
            gmat = _dot(cm, bm, NT)
            for r in range(HEADS_PER_GROUP):
                h = HEADS_PER_GROUP * g + r
                hsl = slice(HEAD_DIM * h, HEAD_DIM * (h + 1))
                xv = xs_ref[:, hsl].astype(f32)
                csc = cs[:, h:h + 1]
                csr = cst[h:h + 1, :]
                cl = cs[l - 1:l, h:h + 1]
                dk = jnp.exp(jnp.where(causal, csc - csr, NEG))
                xd = xv * dt[:, h:h + 1]
                hp = st_ref[h]
                hs_ref[0, h] = hp
                ydiag = _dot((gmat * dk).astype(bf16), xd.astype(bf16))
                yoff = _dot(cm, hp.astype(bf16), NT) * jnp.exp(csc)
                y_ref[:, hsl] = (ydiag + yoff + xv * dsk[:, h:h + 1]).astype(bf16)
                st = _dot((xd * jnp.exp(cl - csc)).astype(bf16), bm, TN)
                st_ref[h] = hp * jnp.exp(cl) + st

    return pl.pallas_call(
        body, name="ssd_fwd", grid=(nc,),
        in_specs=[pl.BlockSpec((l, XBC_W), lambda i: (i, 0)), pl.BlockSpec((l, 128), lambda i: (i, 0)),
                  pl.BlockSpec((8, 128), lambda i: (0, 0))],
        out_specs=[pl.BlockSpec((l, D_MODEL), lambda i: (i, 0)),
                   pl.BlockSpec((1, N_HEADS, HEAD_DIM, D_STATE), lambda i: (i, 0, 0, 0))],
        out_shape=[jax.ShapeDtypeStruct((s, D_MODEL), bf16),
                   jax.ShapeDtypeStruct((nc, N_HEADS, HEAD_DIM, D_STATE), f32)],
        scratch_shapes=[pltpu.VMEM((N_HEADS, HEAD_DIM, D_STATE), f32)],
        compiler_params=_cp("arbitrary"))(act, dtf, prm)


def _ssd_bwd(act, pre, dtf, prm, hs, dy):
    s = act.shape[0]
    nc = s // CHUNK
    l = CHUNK

    def body(xs_ref, pre_ref, dt_ref, prm_ref, hs_ref, dy_ref, dpre_ref, ddt_ref, dprm_ref, dh_ref):
        @pl.when(pl.program_id(0) == 0)
        def _():
            dh_ref[...] = jnp.zeros_like(dh_ref)
            dprm_ref[...] = jnp.zeros_like(dprm_ref)
        a, dsk, zraw, dt, cs, cst, causal, row, col = _ssd_common(dt_ref, prm_ref)
        lane = lax.broadcasted_iota(jnp.int32, (l, 128), 1)
        rowl = lax.broadcasted_iota(jnp.int32, (l, 128), 0)
        sub = lax.broadcasted_iota(jnp.int32, (128, l), 0)
        lane1 = lax.broadcasted_iota(jnp.int32, (1, 128), 1)
        dcs_c = jnp.zeros((l, 128), f32)
        dcs_r = jnp.zeros((128, l), f32)
        ddt_c = jnp.zeros((l, 128), f32)
        dd_row = jnp.zeros((1, 128), f32)
        for g in range(N_GROUPS):
            bsl = slice(B_OFF + D_STATE * g, B_OFF + D_STATE * (g + 1))
            csl = slice(C_OFF + D_STATE * g, C_OFF + D_STATE * (g + 1))
            bm = xs_ref[:, bsl]
            cm = xs_ref[:, csl]
            gmat = _dot(cm, bm, NT)
            dgm = jnp.zeros((l, l), f32)
            dbg = jnp.zeros((l, D_STATE), f32)
            dcg = jnp.zeros((l, D_STATE), f32)
            for r in range(HEADS_PER_GROUP):
                h = HEADS_PER_GROUP * g + r
                hsl = slice(HEAD_DIM * h, HEAD_DIM * (h + 1))
                xv = xs_ref[:, hsl].astype(f32)
                dyv = dy_ref[:, hsl].astype(f32)
                dyb = dyv.astype(bf16)
                csc = cs[:, h:h + 1]
                csr = cst[h:h + 1, :]
                cl = cs[l - 1:l, h:h + 1]
                dk = jnp.exp(jnp.where(causal, csc - csr, NEG))
                mf = gmat * dk
                dtc = dt[:, h:h + 1]
                xd = xv * dtc
                xdb = xd.astype(bf16)
                ecs = jnp.exp(csc)
                dec = jnp.exp(cl)
                e = jnp.exp(cl - csc)
                hp = hs_ref[0, h]
                hpb = hp.astype(bf16)
                dhn = dh_ref[h]
                dhnb = dhn.astype(bf16)
                dd_h = jnp.sum(jnp.sum(dyv * xv, axis=1, keepdims=True), axis=0, keepdims=True)
                dx = dyv * dsk[:, h:h + 1]
                ch = _dot(cm, hpb, NT)
                dye = dyv * ecs
                dyeb = dye.astype(bf16)
                dcg = dcg + _dot(dyeb, hpb)
                dhp = _dot(dyeb, cm, TN)
                dcs_col = jnp.sum(dye * ch, axis=1, keepdims=True)
                dm = _dot(dyb, xdb, NT)
                dxd = _dot(mf.astype(bf16), dyb, TN)
                dgm = dgm + dm * dk
                wmat = dm * mf
                dcs_col = dcs_col + jnp.sum(wmat, axis=1, keepdims=True)
                dcs_row = -jnp.sum(wmat, axis=0, keepdims=True)
                ddec = jnp.sum(jnp.sum(hp * dhn, axis=1, keepdims=True), axis=0, keepdims=True)
                dxe = _dot(bm, dhnb, NT)
                dxd = dxd + dxe * e
                de_e = jnp.sum(dxe * xd, axis=1, keepdims=True) * e
                dbg = dbg + _dot((xd * e).astype(bf16), dhnb)
                dcs_col = dcs_col - de_e
                dlast = ddec * dec + jnp.sum(de_e, axis=0, keepdims=True)
                dh_ref[h] = dhp + dec * dhn
                dx = dx + dxd * dtc
                ddt_h = jnp.sum(dxd * xv, axis=1, keepdims=True)
                is_h = lane == h
                dcs_c = dcs_c + jnp.where(is_h, dcs_col, 0.0) + jnp.where(is_h & (rowl == l - 1), dlast, 0.0)
                dcs_r = dcs_r + jnp.where(sub == h, dcs_row, 0.0)
                ddt_c = ddt_c + jnp.where(is_h, ddt_h, 0.0)
                dd_row = dd_row + jnp.where(lane1 == h, dd_h, 0.0)
                dpre_ref[:, hsl] = (dx * _dsilu(pre_ref[:, hsl].astype(f32))).astype(bf16)
            dgb = dgm.astype(bf16)
            dcg = dcg + _dot(dgb, bm)
            dbg = dbg + _dot(dgb, cm, TN)
            dpre_ref[:, bsl] = (dbg * _dsilu(pre_ref[:, bsl].astype(f32))).astype(bf16)
            dpre_ref[:, csl] = (dcg * _dsilu(pre_ref[:, csl].astype(f32))).astype(bf16)
        dcs = dcs_c + dcs_r.T
        dda = _dot((row <= col).astype(f32), dcs, precision=HIGHEST)
        ddt = ddt_c + dda * a
        ddtraw = jnp.where(lane < N_HEADS, ddt * _sigmoid(zraw), 0.0)
        ddt_ref[...] = ddtraw
        dprm_ref[0:1, :] += jnp.sum(ddtraw, axis=0, keepdims=True)
        dprm_ref[1:2, :] += jnp.where(lane1 < N_HEADS, jnp.sum(dda * dt, axis=0, keepdims=True) * a, 0.0)
        dprm_ref[2:3, :] += dd_row

    def rev(i):
        return (nc - 1 - i, 0)

    return pl.pallas_call(
        body, name="ssd_bwd", grid=(nc,),
        in_specs=[pl.BlockSpec((l, XBC_W), rev), pl.BlockSpec((l, XBC_W), rev),
                  pl.BlockSpec((l, 128), rev), pl.BlockSpec((8, 128), lambda i: (0, 0)),
                  pl.BlockSpec((1, N_HEADS, HEAD_DIM, D_STATE), lambda i: (nc - 1 - i, 0, 0, 0)),
                  pl.BlockSpec((l, D_MODEL), rev)],
        out_specs=[pl.BlockSpec((l, XBC_W), rev), pl.BlockSpec((l, 128), rev), pl.BlockSpec((8, 128), lambda i: (0, 0))],
        out_shape=[jax.ShapeDtypeStruct((s, XBC_W), bf16), jax.ShapeDtypeStruct((s, 128), f32),
                   jax.ShapeDtypeStruct((8, 128), f32)],
        scratch_shapes=[pltpu.VMEM((N_HEADS, HEAD_DIM, D_STATE), f32)],
        compiler_params=_cp("arbitrary"))(act, pre, dtf, prm, hs, dy)


def _fox_cumsum(dtf, prm):
    s = dtf.shape[0]
    l = CHUNK

    def body(f_ref, prm_ref, c_ref, carry_ref):
        @pl.when(pl.program_id(0) == 0)
        def _():
            carry_ref[...] = jnp.zeros_like(carry_ref)
        lf = _log_sigmoid(f_ref[...] + prm_ref[3:4, :])
        row = lax.broadcasted_iota(jnp.int32, (l, l), 0)
        col = lax.broadcasted_iota(jnp.int32, (l, l), 1)
        c = _dot((row >= col).astype(f32), lf, precision=HIGHEST) + carry_ref[...]
        c_ref[...] = c
        carry_ref[...] = c[l - 1:l, :]

    return pl.pallas_call(
        body, name="fox_cumsum", grid=(s // l,),
        in_specs=[pl.BlockSpec((l, 128), lambda i: (i, 0)), pl.BlockSpec((8, 128), lambda i: (0, 0))],
        out_specs=pl.BlockSpec((l, 128), lambda i: (i, 0)),
        out_shape=jax.ShapeDtypeStruct((s, 128), f32),
        scratch_shapes=[pltpu.VMEM((1, 128), f32)],
        compiler_params=_cp("arbitrary"))(dtf, prm)


def _position():
    return lax.axis_index("x"), lax.axis_index("y"), lax.axis_index("c")


def _exchange_sems(n):
    return [pltpu.SemaphoreType.DMA((n, N_DEV - 1)), pltpu.SemaphoreType.DMA((n, N_DEV - 1)),
            pltpu.SemaphoreType.DMA((n,))]


def _exchange_copies(g_refs, r_refs, send_sems, recv_sems, local_sems):
    n = len(g_refs)
    x, y, cc = _position()
    me = 4 * x + 2 * y + cc
    local = [pltpu.make_async_copy(g_refs[a].at[me], r_refs[a].at[me], local_sems.at[a]) for a in range(n)]
    sends, recvs = [], []
    for k in range(1, N_DEV):
        px = 1 - x if k & 4 else x
        py = 1 - y if k & 2 else y
        pc = 1 - cc if k & 1 else cc
        pid = 4 * px + 2 * py + pc
        for a in range(n):
            sends.append(pltpu.make_async_remote_copy(
                src_ref=g_refs[a].at[pid], dst_ref=r_refs[a].at[me],
                send_sem=send_sems.at[a, k - 1], recv_sem=recv_sems.at[a, k - 1],
                device_id=(px, py, pc), device_id_type=pl.DeviceIdType.MESH))
            recvs.append(pltpu.make_async_remote_copy(
                src_ref=g_refs[a].at[pid], dst_ref=r_refs[a].at[pid],
                send_sem=send_sems.at[a, k - 1], recv_sem=recv_sems.at[a, k - 1],
                device_id=(px, py, pc), device_id_type=pl.DeviceIdType.MESH))
    return local, sends, recvs


def _exchange_start(copies):
    local, sends, _ = copies
    for cp in local + sends:
        cp.start()


def _exchange_wait(copies):
    local, sends, recvs = copies
    for cp in recvs:
        cp.wait_recv()
    for cp in sends:
        cp.wait_send()
    for cp in local:
        cp.wait()


AUG = HEAD_DIM
N_PAIRS = N_HEADS // 2
V_BLOCK = 2 * D_MODEL // 128


def _split3(x):
    hi = x.astype(bf16)
    r1 = x - hi.astype(f32)
    mid = r1.astype(bf16)
    lo = (r1 - mid.astype(f32)).astype(bf16)
    return hi.astype(f32), mid.astype(f32), lo.astype(f32)


def _fox_prep(qkv, c):
    s = qkv.shape[0]
    ts = min(CONV_ROW_TILE, s)
    kb = D_MODEL // 128

    def body(q_ref, k_ref, c_ref, qa_ref, ka_ref):
        lane = lax.broadcasted_iota(jnp.int32, (ts, 128), 1)
        low = lane < HEAD_DIM
        for h in range(N_HEADS):
            psl = slice(128 * (h // 2), 128 * (h // 2 + 1))
            qv = q_ref[:, psl].astype(f32) * (HEAD_DIM ** -0.5)
            kv = k_ref[:, psl].astype(f32)
            if h % 2:
                qv = pltpu.roll(qv, HEAD_DIM, 1)
                kv = pltpu.roll(kv, HEAD_DIM, 1)
            hi, mid, lo = _split3(c_ref[:, F_LANE + h:F_LANE + h + 1])
            ones = jnp.where((lane >= AUG + 3) & (lane < AUG + 6), 1.0, 0.0)
            cq = jnp.where(lane == AUG, hi, jnp.where(lane == AUG + 1, mid, jnp.where(lane == AUG + 2, lo, ones)))
            qa_ref[h] = jnp.where(low, qv, cq).astype(bf16)
            onek = jnp.where((lane >= AUG) & (lane < AUG + 3), 1.0, 0.0)
            ck = jnp.where(lane == AUG + 3, -hi, jnp.where(lane == AUG + 4, -mid, jnp.where(lane == AUG + 5, -lo, onek)))
            ka_ref[h] = jnp.where(low, kv, ck).astype(bf16)

    hm = pl.BlockSpec((N_HEADS, ts, 128), lambda i: (0, i, 0))
    return pl.pallas_call(
        body, name="fox_prep", grid=(s // ts,),
        in_specs=[_rowspec(ts, D_MODEL, 0), _rowspec(ts, D_MODEL, 1), _rowspec(ts, 128)],
        out_specs=[hm, hm], out_shape=[jax.ShapeDtypeStruct((N_HEADS, s, 128), bf16)] * 2,
        compiler_params=_cp("parallel"))(qkv, qkv, c)


def _fox_fwd(qa, ka, qkv):
    s = qkv.shape[0]
    t = min(ATTN_FWD_TILE, s)
    nq = s // t

    def body(qa_ref, ka_ref, v_ref, o_ref, lse_ref):
        qi = pl.program_id(1)
        low = lax.broadcasted_iota(jnp.int32, (t, 128), 1) < HEAD_DIM
        row = lax.broadcasted_iota(jnp.int32, (t, t), 0)
        col = lax.broadcasted_iota(jnp.int32, (t, t), 1)

        def tile(ki, carry, diagonal):
            stats, acc = carry
            koff = pl.multiple_of(ki * t, t)
            v = v_ref[pl.ds(koff, t), :]
            vh = (jnp.where(low, v, jnp.zeros_like(v)), jnp.where(low, jnp.zeros_like(v), v))
            new_stats, alphas, pv = [], [], None
            for r in range(2):
                m_old, l_old = stats[r]
                sc = _dot(qa_ref[r], ka_ref[r, pl.ds(koff, t), :], NT)
                if diagonal:
                    sc = jnp.where(col <= row, sc, NEG)
                m_new = jnp.maximum(m_old, jnp.max(sc, axis=1, keepdims=True))
                p = jnp.exp(sc - m_new)
                alpha = jnp.exp(m_old - m_new)
                new_stats.append((m_new, alpha * l_old + jnp.sum(p, axis=1, keepdims=True)))
                alphas.append(alpha)
                d = _dot(p.astype(bf16), vh[r])
                pv = d if pv is None else pv + d
            acc = acc * jnp.where(low, alphas[0], alphas[1]) + pv
            return tuple(new_stats), acc

        init = (((jnp.full((t, 1), NEG, f32), jnp.zeros((t, 1), f32)),) * 2, jnp.zeros((t, 128), f32))
        carry = lax.fori_loop(0, qi, lambda ki, cr: tile(ki, cr, False), init)
        stats, acc = tile(qi, carry, True)
        o_ref[...] = (acc / jnp.where(low, stats[0][1], stats[1][1])).astype(bf16)
        for r in range(2):
            lse = stats[r][0] + jnp.log(stats[r][1])
            lse_ref[r] = jnp.broadcast_to(lse, (t, 128)).T[0:1, :]

    return pl.pallas_call(
        body, name="fox_fwd", grid=(N_PAIRS, nq),
        in_specs=[pl.BlockSpec((2, t, 128), lambda j, qi: (j, qi, 0)),
                  pl.BlockSpec((2, s, 128), lambda j, qi: (j, 0, 0)),
                  pl.BlockSpec((s, 128), lambda j, qi: (0, V_BLOCK + j))],
        out_specs=[pl.BlockSpec((t, 128), lambda j, qi: (qi, j)), pl.BlockSpec((2, 1, t), lambda j, qi: (j, 0, qi))],
        out_shape=[jax.ShapeDtypeStruct((s, D_MODEL), bf16), jax.ShapeDtypeStruct((N_HEADS, 1, s), f32)],
        compiler_params=_cp("parallel", "parallel"))(qa, ka, qkv)


def _fox_bwd(qa, ka, qkv, do, lse, delta, gs):
    s = qkv.shape[0]
    t = min(ATTN_TILE, s)
    nq = s // t
    n = len(gs)

    def body(qa_ref, ka_ref, v_ref, do_ref, lse_ref, dl_ref, *rest):
        g_refs, (dq_ref, dk_ref, dv_ref), r_refs, sems = rest[:n], rest[n:n + 3], rest[n + 3:2 * n + 3], rest[2 * n + 3:]
        ki = pl.program_id(1)
        copies = _exchange_copies(g_refs, r_refs, *sems)

        @pl.when((pl.program_id(0) == 0) & (ki == 0))
        def _():
            _exchange_start(copies)

        @pl.when(ki == 0)
        def _():
            dq_ref[...] = jnp.zeros_like(dq_ref)
        low = lax.broadcasted_iota(jnp.int32, (t, 128), 1) < HEAD_DIM
        row = lax.broadcasted_iota(jnp.int32, (t, t), 0)
        col = lax.broadcasted_iota(jnp.int32, (t, t), 1)
        v = v_ref[...]
        zero = jnp.zeros_like(v)
        vh = (jnp.where(low, v, zero), jnp.where(low, zero, v))

        def tile(qi, carry, diagonal):
            dks, dv = carry
            qoff = pl.multiple_of(qi * t, t)
            dov = do_ref[pl.ds(qoff, t), :]
            doh = (jnp.where(low, dov, zero), jnp.where(low, zero, dov))
            new_dks = []
            for r in range(2):
                qt = qa_ref[r, pl.ds(qoff, t), :]
                sct = _dot(ka_ref[r], qt, NT)
                if diagonal:
                    sct = jnp.where(row <= col, sct, NEG)
                pt = jnp.exp(sct - lse_ref[r, :, pl.ds(qoff, t)])
                dpt = _dot(vh[r], dov, NT)
                dst = (pt * (dpt - dl_ref[r, :, pl.ds(qoff, t)])).astype(bf16)
                dv = dv + _dot(pt.astype(bf16), doh[r])
                new_dks.append(dks[r] + _dot(dst, qt))
                dq_ref[r, pl.ds(qoff, t), :] += _dot(dst, ka_ref[r], TN)
            return tuple(new_dks), dv

        zacc = jnp.zeros((t, 128), f32)
        carry = tile(ki, ((zacc, zacc), zacc), True)
        dks, dv = lax.fori_loop(ki + 1, nq, lambda qi, cr: tile(qi, cr, False), carry)
        dk_ref[0] = dks[0]
        dk_ref[1] = dks[1]
        dv_ref[...] = dv.astype(bf16)

        @pl.when((pl.program_id(0) == N_PAIRS - 1) & (ki == nq - 1))
        def _():
            _exchange_wait(copies)

    anyspec = pl.BlockSpec(memory_space=pl.ANY)
    outs = pl.pallas_call(
        body, name="fox_bwd", grid=(N_PAIRS, nq),
        in_specs=[pl.BlockSpec((2, s, 128), lambda j, ki: (j, 0, 0)),
                  pl.BlockSpec((2, t, 128), lambda j, ki: (j, ki, 0)),
                  pl.BlockSpec((t, 128), lambda j, ki: (ki, V_BLOCK + j)),
                  pl.BlockSpec((s, 128), lambda j, ki: (0, j)),
                  pl.BlockSpec((2, 1, s), lambda j, ki: (j, 0, 0)),
                  pl.BlockSpec((2, 1, s), lambda j, ki: (j, 0, 0))] + [anyspec] * n,
        out_specs=[pl.BlockSpec((2, s, 128), lambda j, ki: (j, 0, 0)),
                   pl.BlockSpec((2, t, 128), lambda j, ki: (j, ki, 0)),
                   pl.BlockSpec((t, 128), lambda j, ki: (ki, j))] + [anyspec] * n,
        out_shape=[jax.ShapeDtypeStruct((N_HEADS, s, 128), f32), jax.ShapeDtypeStruct((N_HEADS, s, 128), f32),
                   jax.ShapeDtypeStruct((s, D_MODEL), bf16)] + [jax.ShapeDtypeStruct(g.shape, g.dtype) for g in gs],
        scratch_shapes=_exchange_sems(n),
        compiler_params=_cp("arbitrary", "arbitrary"))(qa, ka, qkv, do, lse, delta, *gs)
    return outs[0], outs[1], outs[2], outs[3:]


def _fox_bwd_post(dq_hm, dk_hm):
    s = dq_hm.shape[1]
    ts = min(CONV_ROW_TILE, s)

    def body(dq_ref, dk_ref, q_ref, k_ref, dc_ref):
        lane = lax.broadcasted_iota(jnp.int32, (ts, 128), 1)
        dc = jnp.zeros((ts, 128), f32)
        for h in range(N_HEADS):
            hsl = slice(HEAD_DIM * h, HEAD_DIM * (h + 1))
            dqv = dq_ref[h]
            dkv = dk_ref[h]
            q_ref[:, hsl] = (dqv[:, 0:HEAD_DIM] * (HEAD_DIM ** -0.5)).astype(bf16)
            k_ref[:, hsl] = dkv[:, 0:HEAD_DIM].astype(bf16)
            dc = dc + jnp.where(lane == F_LANE + h, dqv[:, AUG:AUG + 1] - dkv[:, AUG + 3:AUG + 4], 0.0)
        dc_ref[...] = dc

    hm = pl.BlockSpec((N_HEADS, ts, 128), lambda i: (0, i, 0))
    return pl.pallas_call(
        body, name="fox_bwd_post", grid=(s // ts,), in_specs=[hm, hm],
        out_specs=[_rowspec(ts, D_MODEL), _rowspec(ts, D_MODEL), _rowspec(ts, 128)],
        out_shape=[jax.ShapeDtypeStruct((s, D_MODEL), bf16), jax.ShapeDtypeStruct((s, D_MODEL), bf16),
                   jax.ShapeDtypeStruct((s, 128), f32)],
        compiler_params=_cp("parallel"))(dq_hm, dk_hm)


def _fox_gate_bwd(dc, dtf, prm, ddt_raw):
    s = dtf.shape[0]
    l = CHUNK
    nb = s // l

    def body(dc_ref, f_ref, prm_ref, ddt_ref, out_ref, dfb_ref, carry_ref):
        @pl.when(pl.program_id(0) == 0)
        def _():
            carry_ref[...] = jnp.zeros_like(carry_ref)
            dfb_ref[...] = jnp.zeros_like(dfb_ref)
        dc = dc_ref[...]
        row = lax.broadcasted_iota(jnp.int32, (l, l), 0)
        col = lax.broadcasted_iota(jnp.int32, (l, l), 1)
        dlf = _dot((row <= col).astype(f32), dc, precision=HIGHEST) + carry_ref[...]
        carry_ref[...] = dlf[0:1, :]
        lane = lax.broadcasted_iota(jnp.int32, (l, 128), 1)
        is_f = (lane >= F_LANE) & (lane < F_LANE + N_HEADS)
        dfr = jnp.where(is_f, dlf * _sigmoid(-(f_ref[...] + prm_ref[3:4, :])), 0.0)
        dfb_ref[...] += jnp.sum(dfr, axis=0, keepdims=True)
        out_ref[...] = ddt_ref[...] + dfr

    def rev(i):
        return (nb - 1 - i, 0)

    return pl.pallas_call(
        body, name="fox_gate_bwd", grid=(nb,),
        in_specs=[pl.BlockSpec((l, 128), rev), pl.BlockSpec((l, 128), rev), pl.BlockSpec((8, 128), lambda i: (0, 0)),
                  pl.BlockSpec((l, 128), rev)],
        out_specs=[pl.BlockSpec((l, 128), rev), pl.BlockSpec((1, 128), lambda i: (0, 0))],
        out_shape=[jax.ShapeDtypeStruct((s, 128), f32), jax.ShapeDtypeStruct((1, 128), f32)],
        scratch_shapes=[pltpu.VMEM((1, 128), f32)],
        compiler_params=_cp("arbitrary"))(dc, dtf, prm, ddt_raw)


def _all_gather(xl, name):
    r, c = xl.shape

    def body(x_ref, out_ref, send_sems, recv_sems, local_sem):
        x, y, cc = _position()
        me, sibling = (x, y, cc), (x, y, 1 - cc)
        chips = [(1 - x, y), (x, 1 - y), (1 - x, 1 - y)]

        def slot(px, py, pc):
            return out_ref.at[4 * px + 2 * py + pc]

        def copy(k, block, to, src=None):
            return pltpu.make_async_remote_copy(
                src_ref=slot(*block) if src is None else src, dst_ref=slot(*block),
                send_sem=send_sems.at[k], recv_sem=recv_sems.at[k],
                device_id=to, device_id_type=pl.DeviceIdType.MESH)

        mine = pltpu.make_async_copy(x_ref, slot(*me), local_sem)
        mine.start()
        first = [copy(0, me, sibling, src=x_ref)]
        first += [copy(1 + j, me, (*chip, cc), src=x_ref) for j, chip in enumerate(chips)]
        for cp in first:
            cp.start()
        passed = [copy(4 + j, (*chip, cc), sibling) for j, chip in enumerate(chips)]
        for j, chip in enumerate(chips):
            copy(1 + j, (*chip, cc), me).wait_recv()
            passed[j].start()
        copy(0, sibling, me).wait_recv()
        for j, chip in enumerate(chips):
            copy(4 + j, (*chip, 1 - cc), me).wait_recv()
        for cp in first + passed:
            cp.wait_send()
        mine.wait()

    return pl.pallas_call(
        body, name=name,
        out_shape=jax.ShapeDtypeStruct((N_DEV, r, c), xl.dtype),
        in_specs=[pl.BlockSpec(memory_space=pl.ANY)], out_specs=pl.BlockSpec(memory_space=pl.ANY),
        scratch_shapes=[pltpu.SemaphoreType.DMA((7,)), pltpu.SemaphoreType.DMA((7,)), pltpu.SemaphoreType.DMA],
    )(xl)


def _grad_exchange(gs):
    n = len(gs)

    def body(*refs):
        copies = _exchange_copies(refs[:n], refs[n:2 * n], *refs[2 * n:])
        _exchange_start(copies)
        _exchange_wait(copies)

    anyspec = pl.BlockSpec(memory_space=pl.ANY)
    return pl.pallas_call(
        body, name="grad_exchange",
        out_shape=[jax.ShapeDtypeStruct(g.shape, g.dtype) for g in gs],
        in_specs=[anyspec] * n, out_specs=[anyspec] * n, scratch_shapes=_exchange_sems(n),
    )(*gs)


def _sum_parts(parts, name):
    n, r, c = parts.shape

    def body(p_ref, o_ref):
        g = p_ref[0]
        for i in range(1, n):
            g = g + p_ref[i]
        o_ref[...] = g

    return pl.pallas_call(body, name=name, out_shape=jax.ShapeDtypeStruct((r, c), f32))(parts)


def _adamw(w, m, v, parts, name, tr=128):
    r, c = w.shape
    n = parts.shape[0]
    tr = min(tr, r)
    c1 = 1.0 - ADAM_B1 ** ADAM_STEP
    c2 = 1.0 - ADAM_B2 ** ADAM_STEP

    def body(w_ref, m_ref, v_ref, p_ref, g_ref, d_ref, nm_ref, nv_ref):
        g = p_ref[0].astype(f32)
        for i in range(1, n):
            g = g + p_ref[i].astype(f32)
        g_ref[...] = g
        nm = ADAM_B1 * m_ref[...] + (1.0 - ADAM_B1) * g
        nv = ADAM_B2 * v_ref[...] + (1.0 - ADAM_B2) * (g * g)
        nm_ref[...] = nm
        nv_ref[...] = nv
        d_ref[...] = -ADAM_LR * ((nm / c1) / (jnp.sqrt(nv / c2) + ADAM_EPS) + ADAM_WD * w_ref[...])

    blk = pl.BlockSpec((tr, c), lambda i: (i, 0))
    return pl.pallas_call(
        body, name=name, grid=(r // tr,),
        in_specs=[blk, blk, blk, pl.BlockSpec((n, tr, c), lambda i: (0, i, 0))],
        out_specs=[blk] * 4, out_shape=[jax.ShapeDtypeStruct((r, c), f32)] * 4,
        compiler_params=_cp("parallel"))(w, m, v, parts)


def _lanes(w):
    return -(-w // 128) * 128


def _pack(arrs):
    rows = []
    for a in arrs:
        k, w = a.shape
        if w % 128:
            a = jnp.pad(a, ((0, 0), (0, _lanes(w) - w)))
        rows.append(a.reshape(-1, 128))
    out = jnp.concatenate(rows, axis=0)
    pad = -out.shape[0] % 8
    return jnp.pad(out, ((0, pad), (0, 0))) if pad else out


def _unpack(packed, shapes):
    outs, off = [], 0
    lead = packed.shape[:-2]
    for k, w in shapes:
        nrow = k * _lanes(w) // 128
        a = packed[..., off:off + nrow, :].reshape(*lead, k, _lanes(w))[..., :w]
        outs.append(a)
        off += nrow
    return outs


def _gathered_cols(a):
    n, k, wl = a.shape
    return jnp.transpose(a, (1, 0, 2)).reshape(k, n * wl)


def _col_shards(a):
    k, w = a.shape
    return jnp.transpose(a.reshape(k, N_DEV, w // N_DEV), (1, 0, 2))


SMALL_PARAMS = (
    ("e_norm_pre", 1, 1024, False), ("e_conv_w", 4, 2048, True), ("e_conv_b", 1, 2048, False),
    ("e_dt_bias", 1, 16, False), ("e_a_log", 1, 16, False), ("e_d_skip", 1, 16, False), ("e_fgate_b", 1, 16, False),
    ("e_ssd_norm", 1, 1024, False), ("e_norm_post", 1, 1024, False), ("o_norm_pre", 1, 1024, True),
    ("o_conv_w", 31, 2048, True), ("o_conv_b", 1, 2048, True), ("o_ln_g", 1, 2048, True), ("o_ln_b", 1, 2048, True),
    ("o_norm_post", 1, 1024, True),
)
BIG_PARAMS = ("e_w_in", "e_w_out", "o_w_in", "o_w_out")
WEIGHT_ORDER = ("e_norm_pre", "e_w_in", "e_conv_w", "e_conv_b", "e_dt_bias", "e_a_log", "e_d_skip", "e_fgate_b",
                "e_ssd_norm", "e_w_out", "e_norm_post", "o_norm_pre", "o_w_in", "o_conv_w", "o_conv_b", "o_ln_g",
                "o_ln_b", "o_w_out", "o_norm_post")
E_IN = 7200
O_IN = 6144


def kernel(x, e_norm_pre, e_w_in, e_conv_w, e_conv_b, e_dt_bias, e_a_log, e_d_skip, e_fgate_b, e_ssd_norm, e_w_out, e_norm_post, o_norm_pre, o_w_in, o_conv_w, o_conv_b, o_ln_g, o_ln_b, o_w_out, o_norm_post, loss_target, m_e_norm_pre, m_e_w_in, m_e_conv_w, m_e_conv_b, m_e_dt_bias, m_e_a_log, m_e_d_skip, m_e_fgate_b, m_e_ssd_norm, m_e_w_out, m_e_norm_post, m_o_norm_pre, m_o_w_in, m_o_conv_w, m_o_conv_b, m_o_ln_g, m_o_ln_b, m_o_w_out, m_o_norm_post, v_e_norm_pre, v_e_w_in, v_e_conv_w, v_e_conv_b, v_e_dt_bias, v_e_a_log, v_e_d_skip, v_e_fgate_b, v_e_ssd_norm, v_e_w_out, v_e_norm_post, v_o_norm_pre, v_o_w_in, v_o_conv_w, v_o_conv_b, v_o_ln_g, v_o_ln_b, v_o_w_out, v_o_norm_post):
    given = dict(locals())
    w_in = {n: given[n] for n in WEIGHT_ORDER}
    m_in = {n: given["m_" + n] for n in WEIGHT_ORDER}
    v_in = {n: given["v_" + n] for n in WEIGHT_ORDER}

    def mat(a):
        return a.reshape(a.shape[-2:])

    xs = mat(x)
    tgt = mat(loss_target)
    xi, yi, ci = _position()
    me = 4 * xi + 2 * yi + ci
    ew, ow = E_IN // N_DEV, O_IN // N_DEV
    wr = D_CONV // N_DEV

    big_local = jnp.concatenate([
        mat(e_w_in).astype(bf16).reshape(ew, D_MODEL), mat(o_w_in).astype(bf16).reshape(ow, D_MODEL),
        mat(e_w_out).astype(bf16), mat(o_w_out).astype(bf16)], axis=0)
    nbig = big_local.shape[0]
    big_local = jnp.pad(big_local, ((0, -nbig % 16), (0, 0)))
    wg = _all_gather(big_local, "gather_weights")
    o0, o1, o2, o3 = 0, ew, ew + ow, ew + ow + wr
    e_w_in_f = _gathered_cols(wg[:, o0:o1].reshape(N_DEV, D_MODEL, ew))
    o_w_in_f = _gathered_cols(wg[:, o1:o2].reshape(N_DEV, D_MODEL, ow))
    e_w_out_f = wg[:, o2:o3].reshape(D_CONV, D_MODEL)
    o_w_out_f = wg[:, o3:o3 + wr].reshape(D_CONV, D_MODEL)
    w_z, w_xbc = e_w_in_f[:, 0:2048], e_w_in_f[:, 2048:4096]
    w_qkv = e_w_in_f[:, 4112:7184]
    w_dtf = jnp.concatenate([e_w_in_f[:, 4096:4112], e_w_in_f[:, 7184:7200], jnp.zeros((D_MODEL, 96), bf16)], axis=1)

    sharded_small = [(n, k, w) for n, k, w, sh in SMALL_PARAMS if sh]
    sg = _all_gather(_pack([mat(w_in[n]) for n, _, _ in sharded_small]), "gather_small_weights")
    full_small = {n: _gathered_cols(a)
                  for (n, _, _), a in zip(sharded_small, _unpack(sg, [(k, w // N_DEV) for _, k, w in sharded_small]))}
    for n, _, _, sh in SMALL_PARAMS:
        if not sh:
            full_small[n] = mat(w_in[n])
    p = full_small
    prm = jnp.zeros((8, 128), f32)
    prm = prm.at[0, 0:16].set(p["e_dt_bias"][0]).at[1, 0:16].set(p["e_a_log"][0]).at[2, 0:16].set(p["e_d_skip"][0])
    prm = prm.at[3, F_LANE:F_LANE + 16].set(p["e_fgate_b"][0])

    u0 = _rms_fwd(xs, p["e_norm_pre"], "rms_pre0")
    z0 = _mm_nn(u0, w_z, bf16, "proj0_z")
    xraw = _mm_nn(u0, w_xbc, bf16, "proj0_xbc")
    qkv = _mm_nn(u0, w_qkv, bf16, "proj0_qkv")
    dtf = _mm_nn(u0, w_dtf, f32, "proj0_dtf")
    pre, act = _conv_ssd_fwd(xraw, p["e_conv_w"], p["e_conv_b"])
    y, hs = _ssd_fwd(act, dtf, prm)
    qa, ka = _fox_prep(qkv, _fox_cumsum(dtf, prm))
    o, lse = _fox_fwd(qa, ka, qkv)
    cat = _gate0_fwd(y, z0, o, p["e_ssd_norm"])
    out0 = _mm_nn(cat, e_w_out_f, f32, "out0")
    x1, u1 = _post0_pre1(xs, out0, p["e_norm_post"], p["o_norm_pre"])

    proj1 = _mm_nn(u1, o_w_in_f, bf16, "proj1")
    hc = _conv_glu_fwd(proj1, p["o_conv_w"], p["o_conv_b"])
    h3 = _ln_gate_fwd(hc, proj1, p["o_ln_g"], p["o_ln_b"])
    out1 = _mm_nn(h3, o_w_out_f, f32, "out1")
    dy, d_out1, dg_post1, loss_part = _final_loss(x1, out1, tgt, p["o_norm_post"])

    dh3 = _mm_nt([(d_out1, 0, o_w_out_f, 0, D_MODEL)], bf16, "dh3")
    g_o_w_out = _mm_tn(h3, d_out1, "dw_out1")
    dhc, dz1, dg_ln, db_ln = _ln_gate_bwd(hc, proj1, dh3, p["o_ln_g"], p["o_ln_b"])
    dval, dgate, dw_conv1, db_conv1 = _conv_glu_bwd(dhc, proj1, p["o_conv_w"])
    dproj1 = jnp.concatenate([dval, dgate, dz1], axis=1)
    du1 = _mm_nt([(dproj1, 0, o_w_in_f, 0, O_IN)], f32, "du1")
    g_o_w_in = _mm_tn(u1, dproj1, "dw_in1", tn=ow, blocked=True)
    dx1, d_out0, dg_pre1, dg_post0 = _mid_bwd(x1, du1, dy, out0, p["o_norm_pre"], p["e_norm_post"])

    dcat = _mm_nt([(d_out0, 0, e_w_out_f, 0, D_MODEL)], bf16, "dcat")
    g_e_w_out = _mm_tn(cat, d_out0, "dw_out0")
    dy_ssd, do, dz0, delta, dg_ssd_norm = _gate0_bwd(y, z0, o, dcat, p["e_ssd_norm"])
    early = [g_e_w_out.reshape(N_DEV, wr, D_MODEL).astype(bf16), g_o_w_in.astype(bf16),
             g_o_w_out.reshape(N_DEV, wr, D_MODEL).astype(bf16)]
    dq_hm, dk_hm, dv, early_parts = _fox_bwd(qa, ka, qkv, do, lse, delta[0:N_HEADS].reshape(N_HEADS, 1, -1), early)
    dq, dk, dc = _fox_bwd_post(dq_hm, dk_hm)
    dpre, ddt_raw, dprm = _ssd_bwd(act, pre, dtf, prm, hs, dy_ssd)
    ddtf, dfb = _fox_gate_bwd(dc, dtf, prm, ddt_raw)
    dxraw, dw_conv0, db_conv0 = _conv_ssd_bwd(dpre, xraw, p["e_conv_w"])
    du0 = _mm_nt([(dz0, 0, w_z, 0, 2048), (dxraw, 0, w_xbc, 0, 2048), (dq, 0, w_qkv, 0, 1024), (dk, 0, w_qkv, 1, 1024),
                  (dv, 0, w_qkv, 2, 1024), (ddtf, 0, w_dtf, 0, 128)], f32, "du0")
    gw_dtf = _mm_tn(u0, ddtf, "dw_in0_dtf")
    g_e_w_in_full = jnp.concatenate([
        _mm_tn(u0, dz0, "dw_in0_z"), _mm_tn(u0, dxraw, "dw_in0_xbc"), gw_dtf[:, 0:16],
        _mm_tn(u0, dq, "dw_in0_q"), _mm_tn(u0, dk, "dw_in0_k"), _mm_tn(u0, dv, "dw_in0_v"), gw_dtf[:, 16:32]], axis=1)
    grad_x, dg_pre0 = _first_bwd(xs, du0, dx1, p["e_norm_pre"])

    big_parts = list(_grad_exchange([_col_shards(g_e_w_in_full).astype(bf16)])) + list(early_parts)
    outs = {}
    for n, parts in zip(BIG_PARAMS, big_parts):
        outs[n] = _adamw(mat(w_in[n]), mat(m_in[n]), mat(v_in[n]), parts, "adamw_" + n)

    small_grads = {
        "e_norm_pre": dg_pre0, "e_conv_w": dw_conv0, "e_conv_b": db_conv0, "e_dt_bias": dprm[0:1, 0:16],
        "e_a_log": dprm[1:2, 0:16], "e_d_skip": dprm[2:3, 0:16], "e_fgate_b": dfb[:, F_LANE:F_LANE + 16],
        "e_ssd_norm": dg_ssd_norm, "e_norm_post": dg_post0, "o_norm_pre": dg_pre1, "o_conv_w": dw_conv1,
        "o_conv_b": db_conv1, "o_ln_g": dg_ln, "o_ln_b": db_ln, "o_norm_post": dg_post1,
    }
    gathered = _all_gather(_pack([small_grads[n] for n, _, _, _ in SMALL_PARAMS] + [loss_part]), "gather_small_grads")
    summed = _unpack(_sum_parts(gathered, "sum_small_grads"), [(k, w) for _, k, w, _ in SMALL_PARAMS] + [(1, 128)])
    loss = summed[-1][0, 0]
    g_local = []
    for (n, k, w, sh), g in zip(SMALL_PARAMS, summed):
        g_local.append(lax.dynamic_slice_in_dim(g, me * (w // N_DEV), w // N_DEV, axis=1) if sh else g)
    names = [n for n, _, _, _ in SMALL_PARAMS]
    local_shapes = [(k, w // N_DEV if sh else w) for _, k, w, sh in SMALL_PARAMS]
    res = _adamw(_pack([mat(w_in[n]) for n in names]), _pack([mat(m_in[n]) for n in names]),
                 _pack([mat(v_in[n]) for n in names]), _pack(g_local)[None], "adamw_small", tr=8)
    unpacked = [_unpack(r, local_shapes) for r in res]
    for i, n in enumerate(names):
        outs[n] = tuple(u[i] for u in unpacked)

    ret = [loss, grad_x.reshape(x.shape)]
    for j in range(4):
        ret += [outs[n][j].reshape(w_in[n].shape) for n in WEIGHT_ORDER]
    return tuple(ret)
```

```python
import jax
import jax.numpy as jnp
from jax import lax
from jax.experimental import pallas as pl
from jax.experimental.pallas import tpu as pltpu

f32 = jnp.float32
bf16 = jnp.bfloat16

N_DEV = 8
D_MODEL = 1024
N_HEADS = 16
HEAD_DIM = 64
N_GROUPS = 4
HEADS_PER_GROUP = 4
D_STATE = 128
CHUNK = 512
SSD_CONV = 4
CONV_WIDTH = 31
D_CONV = 2048
EPS = 1e-6
XBC_W = 2048
B_OFF = 1024
C_OFF = 1536
F_LANE = 16
HALO = 32

ADAM_LR = 0.001
ADAM_B1 = 0.9
ADAM_B2 = 0.999
ADAM_EPS = 1e-08
ADAM_WD = 0.01
ADAM_STEP = 10

VMEM_LIMIT_BYTES = 56 * 1024 * 1024
ROW_TILE = 512
CONV_ROW_TILE = 256
CONV_COL_TILE = 512
CONV_SUB = 32
ATTN_TILE = 512
ATTN_FWD_TILE = 1024

NT = (((1,), (1,)), ((), ()))
TN = (((0,), (0,)), ((), ()))
HIGHEST = lax.Precision.HIGHEST
NEG = -1e30


def _cp(*sem):
    return pltpu.CompilerParams(dimension_semantics=sem if sem else None, vmem_limit_bytes=VMEM_LIMIT_BYTES)


def _sigmoid(x):
    return jax.nn.sigmoid(x)


def _silu(x):
    return x * _sigmoid(x)


def _dsilu(x):
    s = _sigmoid(x)
    return s * (1.0 + x * (1.0 - s))


def _softplus(x):
    return jnp.maximum(x, 0.0) + jnp.log(1.0 + jnp.exp(-jnp.abs(x)))


def _log_sigmoid(x):
    return jnp.minimum(x, 0.0) - jnp.log(1.0 + jnp.exp(-jnp.abs(x)))


def _dot(a, b, dims=None, precision=None):
    if dims is None:
        return jnp.dot(a, b, preferred_element_type=f32, precision=precision)
    return lax.dot_general(a, b, dims, preferred_element_type=f32, precision=precision)


def _mm_nn(a, b, out_dtype, name, tm=512, tn=1024):
    m, k = a.shape
    n = b.shape[1]
    tm, tn = min(tm, m), min(tn, n)

    def body(a_ref, b_ref, o_ref):
        o_ref[...] = _dot(a_ref[...], b_ref[...]).astype(o_ref.dtype)

    return pl.pallas_call(
        body, name=name, grid=(n // tn, m // tm),
        in_specs=[pl.BlockSpec((tm, k), lambda j, i: (i, 0)), pl.BlockSpec((k, tn), lambda j, i: (0, j))],
        out_specs=pl.BlockSpec((tm, tn), lambda j, i: (i, j)),
        out_shape=jax.ShapeDtypeStruct((m, n), out_dtype), compiler_params=_cp("parallel", "parallel"))(a, b)


def _mm_nt(pairs, out_dtype, name, tm=512, tn=512, gs=()):
    m = pairs[0][0].shape[0]
    n = pairs[0][2].shape[0]
    tm, tn = min(tm, m), min(tn, n)
    npair = len(pairs)
    ng = len(gs)
    grid = (n // tn, m // tm)

    def body(*refs):
        g_refs = refs[2 * npair:2 * npair + ng]
        o_ref = refs[2 * npair + ng]
        r_refs = refs[2 * npair + ng + 1:2 * npair + 2 * ng + 1]
        sems = refs[2 * npair + 2 * ng + 1:]
        if ng:
            copies = _exchange_copies(g_refs, r_refs, *sems)

            @pl.when((pl.program_id(0) == 0) & (pl.program_id(1) == 0))
            def _():
                _exchange_start(copies)
        acc = None
        for p in range(npair):
            d = _dot(refs[2 * p][...].astype(bf16), refs[2 * p + 1][...], NT)
            acc = d if acc is None else acc + d
        o_ref[...] = acc.astype(o_ref.dtype)
        if ng:
            @pl.when((pl.program_id(0) == grid[0] - 1) & (pl.program_id(1) == grid[1] - 1))
            def _():
                _exchange_wait(copies)

    in_specs, args = [], []
    for a, acb, b, bcb, k in pairs:
        in_specs.append(pl.BlockSpec((tm, k), lambda j, i, acb=acb: (i, acb)))
        in_specs.append(pl.BlockSpec((tn, k), lambda j, i, bcb=bcb: (j, bcb)))
        args += [a, b]
    anyspec = pl.BlockSpec(memory_space=pl.ANY)
    outs = pl.pallas_call(
        body, name=name, grid=grid, in_specs=in_specs + [anyspec] * ng,
        out_specs=[pl.BlockSpec((tm, tn), lambda j, i: (i, j))] + [anyspec] * ng,
        out_shape=[jax.ShapeDtypeStruct((m, n), out_dtype)] + [jax.ShapeDtypeStruct(g.shape, g.dtype) for g in gs],
        scratch_shapes=_exchange_sems(ng) if ng else [],
        compiler_params=_cp("arbitrary", "arbitrary") if ng else _cp("parallel", "parallel"))(*args, *gs)
    return (outs[0], outs[1:]) if ng else outs[0]


def _mm_tn(a, b, name, a_cb=0, am=None, b_cb=0, bn=None, tn=1024, tk=512, blocked=False):
    k = a.shape[0]
    am = a.shape[1] if am is None else am
    bn = b.shape[1] if bn is None else bn
    tm = min(1024, am)
    tn, tk = min(tn, bn), min(tk, k)
    a_off, b_off = a_cb * (am // tm), b_cb * (bn // tn)

    def body(a_ref, b_ref, o_ref):
        @pl.when(pl.program_id(2) == 0)
        def _():
            o_ref[...] = jnp.zeros_like(o_ref)
        d = _dot(a_ref[...].astype(bf16), b_ref[...].astype(bf16), TN)
        o_ref[...] += d.reshape(o_ref.shape)

    if blocked:
        out_spec = pl.BlockSpec((1, tm, tn), lambda i, j, kk: (j, i, 0))
        out_shape = jax.ShapeDtypeStruct((bn // tn, am, tn), f32)
    else:
        out_spec = pl.BlockSpec((tm, tn), lambda i, j, kk: (i, j))
        out_shape = jax.ShapeDtypeStruct((am, bn), f32)
    return pl.pallas_call(
        body, name=name, grid=(am // tm, bn // tn, k // tk),
        in_specs=[pl.BlockSpec((tk, tm), lambda i, j, kk: (kk, a_off + i)),
                  pl.BlockSpec((tk, tn), lambda i, j, kk: (kk, b_off + j))],
        out_specs=out_spec, out_shape=out_shape,
        compiler_params=_cp("parallel", "parallel", "arbitrary"))(a, b)


def _rowspec(ts, w, cb=0):
    return pl.BlockSpec((ts, w), lambda i: (i, cb))


def _vecspec(w):
    return pl.BlockSpec((1, w), lambda i: (0, 0))


def _rms_fwd(x, g, name):
    s, d = x.shape
    ts = min(ROW_TILE, s)

    def body(x_ref, g_ref, u_ref):
        xv = x_ref[...]
        r = lax.rsqrt(jnp.mean(xv * xv, axis=-1, keepdims=True) + EPS)
        u_ref[...] = (xv * r * g_ref[...]).astype(bf16)

    return pl.pallas_call(
        body, name=name, grid=(s // ts,), in_specs=[_rowspec(ts, d), _vecspec(d)], out_specs=_rowspec(ts, d),
        out_shape=jax.ShapeDtypeStruct((s, d), bf16), compiler_params=_cp("parallel"))(x, g)


def _rms_bwd_vals(xv, g, dy):
    r = lax.rsqrt(jnp.mean(xv * xv, axis=-1, keepdims=True) + EPS)
    xh = xv * r
    dg = jnp.sum(dy * xh, axis=0, keepdims=True)
    dxh = dy * g
    dx = r * (dxh - xh * jnp.mean(dxh * xh, axis=-1, keepdims=True))
    return dx, dg


def _gate0_fwd(y, z, o, ssd_norm):
    s = y.shape[0]
    ts = min(ROW_TILE, s)
    gw = D_MODEL // N_GROUPS

    def body(y_ref, zs_ref, zf_ref, o_ref, w_ref, cat_ref):
        yg = y_ref[...].astype(f32) * _silu(zs_ref[...].astype(f32))
        for g in range(N_GROUPS):
            seg = yg[:, gw * g:gw * (g + 1)]
            r = lax.rsqrt(jnp.mean(seg * seg, axis=-1, keepdims=True) + EPS)
            cat_ref[:, gw * g:gw * (g + 1)] = (seg * r * w_ref[:, gw * g:gw * (g + 1)]).astype(bf16)
        cat_ref[:, D_MODEL:] = (o_ref[...].astype(f32) * _silu(zf_ref[...].astype(f32))).astype(bf16)

    return pl.pallas_call(
        body, name="gate0_fwd", grid=(s // ts,),
        in_specs=[_rowspec(ts, D_MODEL), _rowspec(ts, D_MODEL, 0), _rowspec(ts, D_MODEL, 1), _rowspec(ts, D_MODEL),
                  _vecspec(D_MODEL)],
        out_specs=_rowspec(ts, 2 * D_MODEL),
        out_shape=jax.ShapeDtypeStruct((s, 2 * D_MODEL), bf16), compiler_params=_cp("parallel"))(y, z, z, o, ssd_norm)


def _post0_pre1(x, out0, g_post0, g_pre1):
    s, d = x.shape
    ts = min(ROW_TILE, s)

    def body(x_ref, o_ref, gp_ref, gn_ref, x1_ref, u1_ref):
        ov = o_ref[...]
        r = lax.rsqrt(jnp.mean(ov * ov, axis=-1, keepdims=True) + EPS)
        x1 = x_ref[...] + ov * r * gp_ref[...]
        x1_ref[...] = x1
        r1 = lax.rsqrt(jnp.mean(x1 * x1, axis=-1, keepdims=True) + EPS)
        u1_ref[...] = (x1 * r1 * gn_ref[...]).astype(bf16)

    return pl.pallas_call(
        body, name="post0_pre1", grid=(s // ts,),
        in_specs=[_rowspec(ts, d), _rowspec(ts, d), _vecspec(d), _vecspec(d)],
        out_specs=[_rowspec(ts, d), _rowspec(ts, d)],
        out_shape=[jax.ShapeDtypeStruct((s, d), f32), jax.ShapeDtypeStruct((s, d), bf16)],
        compiler_params=_cp("parallel"))(x, out0, g_post0, g_pre1)


def _ln_vals(hc, g, b):
    mu = jnp.mean(hc, axis=-1, keepdims=True)
    xc = hc - mu
    rstd = lax.rsqrt(jnp.mean(xc * xc, axis=-1, keepdims=True) + EPS)
    xh = xc * rstd
    return xh, rstd, xh * g + b


def _ln_gate_fwd(hc, proj1, ln_g, ln_b):
    s = hc.shape[0]
    ts = min(ROW_TILE, s)

    def body(hc_ref, z_ref, g_ref, b_ref, h3_ref):
        _, _, ln = _ln_vals(hc_ref[...].astype(f32), g_ref[...], b_ref[...])
        h3_ref[...] = (_silu(ln) * _silu(z_ref[...].astype(f32))).astype(bf16)

    return pl.pallas_call(
        body, name="ln_gate_fwd", grid=(s // ts,),
        in_specs=[_rowspec(ts, D_CONV), _rowspec(ts, D_CONV, 2), _vecspec(D_CONV), _vecspec(D_CONV)],
        out_specs=_rowspec(ts, D_CONV),
        out_shape=jax.ShapeDtypeStruct((s, D_CONV), bf16), compiler_params=_cp("parallel"))(hc, proj1, ln_g, ln_b)


def _final_loss(x1, out1, tgt, g_post1):
    s, d = x1.shape
    ts = min(ROW_TILE, s)

    def body(x1_ref, o_ref, t_ref, g_ref, dy_ref, do_ref, dg_ref, loss_ref):
        i = pl.program_id(0)

        @pl.when(i == 0)
        def _():
            dg_ref[...] = jnp.zeros_like(dg_ref)
            loss_ref[...] = jnp.zeros_like(loss_ref)
        ov = o_ref[...]
        g = g_ref[...]
        r = lax.rsqrt(jnp.mean(ov * ov, axis=-1, keepdims=True) + EPS)
        diff = x1_ref[...] + ov * r * g - t_ref[...]
        row = jnp.mean(diff * diff, axis=-1, keepdims=True)
        loss_ref[...] += jnp.broadcast_to(0.5 * jnp.sum(row, axis=0, keepdims=True), loss_ref.shape)
        dy = diff * (1.0 / d)
        dy_ref[...] = dy
        dx, dg = _rms_bwd_vals(ov, g, dy)
        do_ref[...] = dx.astype(bf16)
        dg_ref[...] += dg

    return pl.pallas_call(
        body, name="final_loss", grid=(s // ts,),
        in_specs=[_rowspec(ts, d), _rowspec(ts, d), _rowspec(ts, d), _vecspec(d)],
        out_specs=[_rowspec(ts, d), _rowspec(ts, d), _vecspec(d), _vecspec(128)],
        out_shape=[jax.ShapeDtypeStruct((s, d), f32), jax.ShapeDtypeStruct((s, d), bf16),
                   jax.ShapeDtypeStruct((1, d), f32), jax.ShapeDtypeStruct((1, 128), f32)],
        compiler_params=_cp("arbitrary"))(x1, out1, tgt, g_post1)


def _ln_gate_bwd(hc, proj1, dh3, ln_g, ln_b):
    s = hc.shape[0]
    ts = min(ROW_TILE, s)

    def body(hc_ref, z_ref, dh_ref, g_ref, b_ref, dhc_ref, dz_ref, dg_ref, db_ref):
        @pl.when(pl.program_id(0) == 0)
        def _():
            dg_ref[...] = jnp.zeros_like(dg_ref)
            db_ref[...] = jnp.zeros_like(db_ref)
        g = g_ref[...]
        xh, rstd, ln = _ln_vals(hc_ref[...].astype(f32), g, b_ref[...])
        zv = z_ref[...].astype(f32)
        dh3 = dh_ref[...].astype(f32)
        dz_ref[...] = (dh3 * _silu(ln) * _dsilu(zv)).astype(bf16)
        dln = dh3 * _silu(zv) * _dsilu(ln)
        dg_ref[...] += jnp.sum(dln * xh, axis=0, keepdims=True)
        db_ref[...] += jnp.sum(dln, axis=0, keepdims=True)
        dxh = dln * g
        dhc = rstd * (dxh - jnp.mean(dxh, axis=-1, keepdims=True) - xh * jnp.mean(dxh * xh, axis=-1, keepdims=True))
        dhc_ref[...] = dhc.astype(bf16)

    return pl.pallas_call(
        body, name="ln_gate_bwd", grid=(s // ts,),
        in_specs=[_rowspec(ts, D_CONV), _rowspec(ts, D_CONV, 2), _rowspec(ts, D_CONV), _vecspec(D_CONV),
                  _vecspec(D_CONV)],
        out_specs=[_rowspec(ts, D_CONV), _rowspec(ts, D_CONV), _vecspec(D_CONV), _vecspec(D_CONV)],
        out_shape=[jax.ShapeDtypeStruct((s, D_CONV), bf16), jax.ShapeDtypeStruct((s, D_CONV), bf16),
                   jax.ShapeDtypeStruct((1, D_CONV), f32), jax.ShapeDtypeStruct((1, D_CONV), f32)],
        compiler_params=_cp("arbitrary"))(hc, proj1, dh3, ln_g, ln_b)


def _mid_bwd(x1, du1, dy, out0, g_pre1, g_post0):
    s, d = x1.shape
    ts = min(ROW_TILE, s)

    def body(x1_ref, du_ref, dy_ref, o_ref, gn_ref, gp_ref, dx1_ref, do_ref, dgn_ref, dgp_ref):
        @pl.when(pl.program_id(0) == 0)
        def _():
            dgn_ref[...] = jnp.zeros_like(dgn_ref)
            dgp_ref[...] = jnp.zeros_like(dgp_ref)
        dxa, dgn = _rms_bwd_vals(x1_ref[...], gn_ref[...], du_ref[...])
        dx1 = dy_ref[...] + dxa
        dx1_ref[...] = dx1
        dgn_ref[...] += dgn
        dxo, dgp = _rms_bwd_vals(o_ref[...], gp_ref[...], dx1)
        do_ref[...] = dxo.astype(bf16)
        dgp_ref[...] += dgp

    return pl.pallas_call(
        body, name="mid_bwd", grid=(s // ts,),
        in_specs=[_rowspec(ts, d)] * 4 + [_vecspec(d), _vecspec(d)],
        out_specs=[_rowspec(ts, d), _rowspec(ts, d), _vecspec(d), _vecspec(d)],
        out_shape=[jax.ShapeDtypeStruct((s, d), f32), jax.ShapeDtypeStruct((s, d), bf16),
                   jax.ShapeDtypeStruct((1, d), f32), jax.ShapeDtypeStruct((1, d), f32)],
        compiler_params=_cp("arbitrary"))(x1, du1, dy, out0, g_pre1, g_post0)


def _first_bwd(x, du0, dx1, g_pre0):
    s, d = x.shape
    ts = min(ROW_TILE, s)

    def body(x_ref, du_ref, dx1_ref, g_ref, dx_ref, dg_ref):
        @pl.when(pl.program_id(0) == 0)
        def _():
            dg_ref[...] = jnp.zeros_like(dg_ref)
        dxa, dg = _rms_bwd_vals(x_ref[...], g_ref[...], du_ref[...])
        dx_ref[...] = dx1_ref[...] + dxa
        dg_ref[...] += dg

    return pl.pallas_call(
        body, name="first_bwd", grid=(s // ts,),
        in_specs=[_rowspec(ts, d)] * 3 + [_vecspec(d)],
        out_specs=[_rowspec(ts, d), _vecspec(d)],
        out_shape=[jax.ShapeDtypeStruct((s, d), f32), jax.ShapeDtypeStruct((1, d), f32)],
        compiler_params=_cp("arbitrary"))(x, du0, dx1, g_pre0)


def _gate0_bwd(y, z, o, dcat, ssd_norm):
    s = y.shape[0]
    ts = min(ROW_TILE, s)
    gw = D_MODEL // N_GROUPS

    def body(y_ref, zs_ref, zf_ref, o_ref, dn_ref, dg_ref, w_ref, dy_ref, do_ref, dz_ref, delta_ref, dw_ref):
        @pl.when(pl.program_id(0) == 0)
        def _():
            dw_ref[...] = jnp.zeros_like(dw_ref)
        yv = y_ref[...].astype(f32)
        zs = zs_ref[...].astype(f32)
        sz = _silu(zs)
        yg = yv * sz
        dyn = dn_ref[...].astype(f32)
        for g in range(N_GROUPS):
            sl = slice(gw * g, gw * (g + 1))
            seg = yg[:, sl]
            r = lax.rsqrt(jnp.mean(seg * seg, axis=-1, keepdims=True) + EPS)
            yh = seg * r
            dn = dyn[:, sl]
            dw_ref[:, sl] += jnp.sum(dn * yh, axis=0, keepdims=True)
            dyh = dn * w_ref[:, sl]
            dyg = r * (dyh - yh * jnp.mean(dyh * yh, axis=-1, keepdims=True))
            dy_ref[:, sl] = (dyg * sz[:, sl]).astype(bf16)
            dz_ref[:, sl] = (dyg * yv[:, sl] * _dsilu(zs[:, sl])).astype(bf16)
        zf = zf_ref[...].astype(f32)
        ov = o_ref[...].astype(f32)
        dog = dg_ref[...].astype(f32)
        dov = (dog * _silu(zf)).astype(bf16)
        do_ref[...] = dov
        dz_ref[:, D_MODEL:] = (dog * ov * _dsilu(zf)).astype(bf16)
        prod = dov.astype(f32) * ov
        lane = lax.broadcasted_iota(jnp.int32, (ts, 128), 1)
        delta = jnp.zeros((ts, 128), f32)
        for h in range(N_HEADS):
            dh = jnp.sum(prod[:, HEAD_DIM * h:HEAD_DIM * (h + 1)], axis=-1, keepdims=True)
            delta = delta + jnp.where(lane == h, dh, 0.0)
        delta_ref[...] = delta.T

    return pl.pallas_call(
        body, name="gate0_bwd", grid=(s // ts,),
        in_specs=[_rowspec(ts, D_MODEL), _rowspec(ts, D_MODEL, 0), _rowspec(ts, D_MODEL, 1), _rowspec(ts, D_MODEL),
                  _rowspec(ts, D_MODEL, 0), _rowspec(ts, D_MODEL, 1), _vecspec(D_MODEL)],
        out_specs=[_rowspec(ts, D_MODEL), _rowspec(ts, D_MODEL), _rowspec(ts, 2 * D_MODEL),
                   pl.BlockSpec((128, ts), lambda i: (0, i)), _vecspec(D_MODEL)],
        out_shape=[jax.ShapeDtypeStruct((s, D_MODEL), bf16), jax.ShapeDtypeStruct((s, D_MODEL), bf16),
                   jax.ShapeDtypeStruct((s, 2 * D_MODEL), bf16), jax.ShapeDtypeStruct((128, s), f32),
                   jax.ShapeDtypeStruct((1, D_MODEL), f32)],
        compiler_params=_cp("arbitrary"))(y, z, z, o, dcat, dcat, ssd_norm)


def _conv_grid(s, c):
    ts, cb = min(CONV_ROW_TILE, s), min(CONV_COL_TILE, c)
    return ts, cb, (c // cb, s // ts)


def _cur(ts, cb, off=0):
    return pl.BlockSpec((ts, cb), lambda c, i: (i, c + off))


def _prev_halo(ts, cb, off=0):
    return pl.BlockSpec((HALO, cb), lambda c, i: (jnp.maximum(i * (ts // HALO) - 1, 0), c + off))


def _next_halo(ts, cb, s, off=0):
    return pl.BlockSpec((HALO, cb), lambda c, i: (jnp.minimum((i + 1) * (ts // HALO), s // HALO - 1), c + off))


def _wspec(k, cb):
    return pl.BlockSpec((k, cb), lambda c, i: (0, c))


def _phases(offsets):
    return sorted({o % 8 for o in offsets} - {0})


def _shift_scratch(offsets, ts, cb):
    return pltpu.VMEM((max(len(_phases(offsets)), 1), ts + HALO - 8, cb), f32)


def _fill_phases(ext_ref, sh_ref, offsets, ts):
    for j, r in enumerate(_phases(offsets)):
        sh_ref[j] = ext_ref[pl.ds(r, ts + HALO - 8), :]


def _slab(ext_ref, sh_ref, offsets, off, start):
    r = off % 8
    a = off - r + start
    if r == 0:
        return ext_ref[a:a + CONV_SUB, :]
    return sh_ref[_phases(offsets).index(r), a:a + CONV_SUB, :]


def _conv_taps(ext_ref, sh_ref, w_ref, b_ref, ts, k_taps, emit):
    offsets = [HALO - (k_taps - 1) + k for k in range(k_taps)]
    _fill_phases(ext_ref, sh_ref, offsets, ts)
    for sb in range(ts // CONV_SUB):
        acc = b_ref[...]
        for k in range(k_taps):
            acc = acc + w_ref[k:k + 1, :] * _slab(ext_ref, sh_ref, offsets, offsets[k], sb * CONV_SUB)
        emit(slice(sb * CONV_SUB, (sb + 1) * CONV_SUB), acc)


def _conv_ssd_fwd(xraw, w, b):
    s, c = xraw.shape
    ts, cb, grid = _conv_grid(s, c)
    offsets = [HALO - (SSD_CONV - 1) + k for k in range(SSD_CONV)]

    def body(x_ref, xh_ref, w_ref, b_ref, pre_ref, act_ref, ext_ref, sh_ref):
        first = pl.program_id(1) == 0
        ext_ref[0:HALO, :] = jnp.where(first, 0.0, xh_ref[...].astype(f32))
        ext_ref[HALO:, :] = x_ref[...].astype(f32)

        def emit(rows, pre):
            pre_ref[rows, :] = pre.astype(bf16)
            act_ref[rows, :] = _silu(pre).astype(bf16)
        _conv_taps(ext_ref, sh_ref, w_ref, b_ref, ts, SSD_CONV, emit)

    return pl.pallas_call(
        body, name="conv_ssd_fwd", grid=grid,
        in_specs=[_cur(ts, cb), _prev_halo(ts, cb), _wspec(SSD_CONV, cb), _wspec(1, cb)],
        out_specs=[_cur(ts, cb), _cur(ts, cb)],
        out_shape=[jax.ShapeDtypeStruct((s, c), bf16)] * 2,
        scratch_shapes=[pltpu.VMEM((HALO + ts, cb), f32), _shift_scratch(offsets, ts, cb)],
        compiler_params=_cp("parallel", "parallel"))(xraw, xraw, w, b)


def _conv_glu_fwd(proj1, w, b):
    s = proj1.shape[0]
    c = D_CONV
    ts, cb, grid = _conv_grid(s, c)
    goff = c // cb

    offsets = [HALO - (CONV_WIDTH - 1) + k for k in range(CONV_WIDTH)]

    def body(v_ref, g_ref, vh_ref, gh_ref, w_ref, b_ref, hc_ref, ext_ref, sh_ref):
        first = pl.program_id(1) == 0
        hh = vh_ref[...].astype(f32) * _sigmoid(gh_ref[...].astype(f32))
        ext_ref[0:HALO, :] = jnp.where(first, 0.0, hh)
        ext_ref[HALO:, :] = v_ref[...].astype(f32) * _sigmoid(g_ref[...].astype(f32))

        def emit(rows, hc):
            hc_ref[rows, :] = hc.astype(bf16)
        _conv_taps(ext_ref, sh_ref, w_ref, b_ref, ts, CONV_WIDTH, emit)

    return pl.pallas_call(
        body, name="conv_glu_fwd", grid=grid,
        in_specs=[_cur(ts, cb), _cur(ts, cb, goff), _prev_halo(ts, cb), _prev_halo(ts, cb, goff),
                  _wspec(CONV_WIDTH, cb), _wspec(1, cb)],
        out_specs=_cur(ts, cb),
        out_shape=jax.ShapeDtypeStruct((s, c), bf16),
        scratch_shapes=[pltpu.VMEM((HALO + ts, cb), f32), _shift_scratch(offsets, ts, cb)],
        compiler_params=_cp("parallel", "parallel"))(proj1, proj1, proj1, proj1, w, b)


def _conv_bwd_offsets(k_taps):
    return [k_taps - 1 - k for k in range(k_taps)], [HALO - (k_taps - 1) + k for k in range(k_taps)]


def _conv_bwd_scratch(k_taps, ts, cb):
    d_offs, x_offs = _conv_bwd_offsets(k_taps)
    return [pltpu.VMEM((ts + HALO, cb), f32), _shift_scratch(d_offs, ts, cb),
            pltpu.VMEM((HALO + ts, cb), f32), _shift_scratch(x_offs, ts, cb),
            pltpu.VMEM((k_taps, 8, cb), f32), pltpu.VMEM((8, cb), f32)]


def _conv_bwd_core(dp, dpn_ref, last, w_ref, scratch, dw_ref, db_ref, ts, k_taps, emit):
    dext_ref, dsh_ref, xext_ref, xsh_ref, dw8_ref, db8_ref = scratch
    d_offs, x_offs = _conv_bwd_offsets(k_taps)
    dext_ref[0:ts, :] = dp
    dext_ref[ts:, :] = jnp.where(last, 0.0, dpn_ref[...].astype(f32))
    _fill_phases(dext_ref, dsh_ref, d_offs, ts)
    _fill_phases(xext_ref, xsh_ref, x_offs, ts)

    @pl.when(pl.program_id(1) == 0)
    def _():
        dw8_ref[...] = jnp.zeros_like(dw8_ref)
        db8_ref[...] = jnp.zeros_like(db8_ref)
    cb = dp.shape[1]
    for sb in range(ts // CONV_SUB):
        start = sb * CONV_SUB
        dpv = dext_ref[start:start + CONV_SUB, :]
        dx = None
        for k in range(k_taps):
            t = w_ref[k:k + 1, :] * _slab(dext_ref, dsh_ref, d_offs, d_offs[k], start)
            dx = t if dx is None else dx + t
            prod = dpv * _slab(xext_ref, xsh_ref, x_offs, x_offs[k], start)
            dw8_ref[k] += jnp.sum(prod.reshape(CONV_SUB // 8, 8, cb), axis=0)
        db8_ref[...] += jnp.sum(dpv.reshape(CONV_SUB // 8, 8, cb), axis=0)
        emit(slice(start, start + CONV_SUB), dx)

    @pl.when(last)
    def _():
        dw_ref[...] = jnp.sum(dw8_ref[...], axis=1)
        db_ref[...] = jnp.sum(db8_ref[...], axis=0, keepdims=True)


def _conv_ssd_bwd(dpre, xraw, w):
    s, c = xraw.shape
    ts, cb, grid = _conv_grid(s, c)
    nb = s // ts

    def body(dp_ref, dpn_ref, x_ref, xh_ref, w_ref, dx_ref, dw_ref, db_ref, *scratch):
        i = pl.program_id(1)
        xext_ref = scratch[2]
        xext_ref[0:HALO, :] = jnp.where(i == 0, 0.0, xh_ref[...].astype(f32))
        xext_ref[HALO:, :] = x_ref[...].astype(f32)

        def emit(rows, dx):
            dx_ref[rows, :] = dx.astype(bf16)
        _conv_bwd_core(dp_ref[...].astype(f32), dpn_ref, i == nb - 1, w_ref, scratch, dw_ref, db_ref, ts, SSD_CONV, emit)

    return pl.pallas_call(
        body, name="conv_ssd_bwd", grid=grid,
        in_specs=[_cur(ts, cb), _next_halo(ts, cb, s), _cur(ts, cb), _prev_halo(ts, cb), _wspec(SSD_CONV, cb)],
        out_specs=[_cur(ts, cb), _wspec(SSD_CONV, cb), _wspec(1, cb)],
        out_shape=[jax.ShapeDtypeStruct((s, c), bf16), jax.ShapeDtypeStruct((SSD_CONV, c), f32),
                   jax.ShapeDtypeStruct((1, c), f32)],
        scratch_shapes=_conv_bwd_scratch(SSD_CONV, ts, cb),
        compiler_params=_cp("parallel", "arbitrary"))(dpre, dpre, xraw, xraw, w)


def _conv_glu_bwd(dhc, proj1, w):
    s = proj1.shape[0]
    c = D_CONV
    ts, cb, grid = _conv_grid(s, c)
    nb = s // ts
    goff = c // cb

    def body(dp_ref, dpn_ref, v_ref, g_ref, vh_ref, gh_ref, w_ref, dv_ref, dg_ref, dw_ref, db_ref, *scratch):
        i = pl.program_id(1)
        xext_ref = scratch[2]
        xext_ref[0:HALO, :] = jnp.where(i == 0, 0.0, vh_ref[...].astype(f32) * _sigmoid(gh_ref[...].astype(f32)))
        xext_ref[HALO:, :] = v_ref[...].astype(f32) * _sigmoid(g_ref[...].astype(f32))

        def emit(rows, dh):
            val = v_ref[rows, :].astype(f32)
            sg = _sigmoid(g_ref[rows, :].astype(f32))
            dv_ref[rows, :] = (dh * sg).astype(bf16)
            dg_ref[rows, :] = (dh * val * sg * (1.0 - sg)).astype(bf16)
        _conv_bwd_core(dp_ref[...].astype(f32), dpn_ref, i == nb - 1, w_ref, scratch, dw_ref, db_ref, ts, CONV_WIDTH, emit)

    return pl.pallas_call(
        body, name="conv_glu_bwd", grid=grid,
        in_specs=[_cur(ts, cb), _next_halo(ts, cb, s), _cur(ts, cb), _cur(ts, cb, goff), _prev_halo(ts, cb),
                  _prev_halo(ts, cb, goff), _wspec(CONV_WIDTH, cb)],
        out_specs=[_cur(ts, cb), _cur(ts, cb), _wspec(CONV_WIDTH, cb), _wspec(1, cb)],
        out_shape=[jax.ShapeDtypeStruct((s, c), bf16), jax.ShapeDtypeStruct((s, c), bf16),
                   jax.ShapeDtypeStruct((CONV_WIDTH, c), f32), jax.ShapeDtypeStruct((1, c), f32)],
        scratch_shapes=_conv_bwd_scratch(CONV_WIDTH, ts, cb),
        compiler_params=_cp("parallel", "arbitrary"))(dhc, dhc, proj1, proj1, proj1, proj1, w)


def _ssd_common(dt_ref, prm_ref):
    l = CHUNK
    dtb = prm_ref[0:1, :]
    a = -jnp.exp(prm_ref[1:2, :])
    dsk = prm_ref[2:3, :]
    zraw = dt_ref[...] + dtb
    dt = _softplus(zraw)
    da = dt * a
    row = lax.broadcasted_iota(jnp.int32, (l, l), 0)
    col = lax.broadcasted_iota(jnp.int32, (l, l), 1)
    causal = row >= col
    cs = _dot(causal.astype(f32), da, precision=HIGHEST)
    return a, dsk, zraw, dt, cs, cs.T, causal, row, col


def _ssd_fwd(act, dtf, prm):
    s = act.shape[0]
    nc = s // CHUNK
    l = CHUNK

    def body(xs_ref, dt_ref, prm_ref, y_ref, hs_ref, st_ref):
        @pl.when(pl.program_id(0) == 0)
        def _():
            st_ref[...] = jnp.zeros_like(st_ref)
        a, dsk, _, dt, cs, cst, causal, _, _ = _ssd_common(dt_ref, prm_ref)
        for g in range(N_GROUPS):
            bm = xs_ref[:, B_OFF + D_STATE * g:B_OFF + D_STATE * (g + 1)]
            cm = xs_ref[:, C_OFF + D_STATE * g:C_OFF + D_STATE * (g + 1)]
            gmat = _dot(cm, bm, NT)
            for r in range(HEADS_PER_GROUP):
                h = HEADS_PER_GROUP * g + r
                hsl = slice(HEAD_DIM * h, HEAD_DIM * (h + 1))
                xv = xs_ref[:, hsl].astype(f32)
                csc = cs[:, h:h + 1]
                csr = cst[h:h + 1, :]
                cl = cs[l - 1:l, h:h + 1]
                dk = jnp.exp(jnp.where(causal, csc - csr, NEG))
                xd = xv * dt[:, h:h + 1]
                hp = st_ref[h]
                hs_ref[0, h] = hp
                ydiag = _dot((gmat * dk).astype(bf16), xd.astype(bf16))
                yoff = _dot(cm, hp.astype(bf16), NT) * jnp.exp(csc)
                y_ref[:, hsl] = (ydiag + yoff + xv * dsk[:, h:h + 1]).astype(bf16)
                st = _dot((xd * jnp.exp(cl - csc)).astype(bf16), bm, TN)
                st_ref[h] = hp * jnp.exp(cl) + st

    return pl.pallas_call(
        body, name="ssd_fwd", grid=(nc,),
        in_specs=[pl.BlockSpec((l, XBC_W), lambda i: (i, 0)), pl.BlockSpec((l, 128), lambda i: (i, 0)),
                  pl.BlockSpec((8, 128), lambda i: (0, 0))],
        out_specs=[pl.BlockSpec((l, D_MODEL), lambda i: (i, 0)),
                   pl.BlockSpec((1, N_HEADS, HEAD_DIM, D_STATE), lambda i: (i, 0, 0, 0))],
        out_shape=[jax.ShapeDtypeStruct((s, D_MODEL), bf16),
                   jax.ShapeDtypeStruct((nc, N_HEADS, HEAD_DIM, D_STATE), f32)],
        scratch_shapes=[pltpu.VMEM((N_HEADS, HEAD_DIM, D_STATE), f32)],
        compiler_params=_cp("arbitrary"))(act, dtf, prm)


def _ssd_bwd(act, pre, dtf, prm, hs, dy):
    s = act.shape[0]
    nc = s // CHUNK
    l = CHUNK

    def body(xs_ref, pre_ref, dt_ref, prm_ref, hs_ref, dy_ref, dpre_ref, ddt_ref, dprm_ref, dh_ref):
        @pl.when(pl.program_id(0) == 0)
        def _():
            dh_ref[...] = jnp.zeros_like(dh_ref)
            dprm_ref[...] = jnp.zeros_like(dprm_ref)
        a, dsk, zraw, dt, cs, cst, causal, row, col = _ssd_common(dt_ref, prm_ref)
        lane = lax.broadcasted_iota(jnp.int32, (l, 128), 1)
        rowl = lax.broadcasted_iota(jnp.int32, (l, 128), 0)
        sub = lax.broadcasted_iota(jnp.int32, (128, l), 0)
        lane1 = lax.broadcasted_iota(jnp.int32, (1, 128), 1)
        dcs_c = jnp.zeros((l, 128), f32)
        dcs_r = jnp.zeros((128, l), f32)
        ddt_c = jnp.zeros((l, 128), f32)
        dd_row = jnp.zeros((1, 128), f32)
        for g in range(N_GROUPS):
            bsl = slice(B_OFF + D_STATE * g, B_OFF + D_STATE * (g + 1))
            csl = slice(C_OFF + D_STATE * g, C_OFF + D_STATE * (g + 1))
            bm = xs_ref[:, bsl]
            cm = xs_ref[:, csl]
            gmat = _dot(cm, bm, NT)
            dgm = jnp.zeros((l, l), f32)
            dbg = jnp.zeros((l, D_STATE), f32)
            dcg = jnp.zeros((l, D_STATE), f32)
            for r in range(HEADS_PER_GROUP):
                h = HEADS_PER_GROUP * g + r
                hsl = slice(HEAD_DIM * h, HEAD_DIM * (h + 1))
                xv = xs_ref[:, hsl].astype(f32)
                dyv = dy_ref[:, hsl].astype(f32)
                dyb = dyv.astype(bf16)
                csc = cs[:, h:h + 1]
                csr = cst[h:h + 1, :]
                cl = cs[l - 1:l, h:h + 1]
                dk = jnp.exp(jnp.where(causal, csc - csr, NEG))
                mf = gmat * dk
                dtc = dt[:, h:h + 1]
                xd = xv * dtc
                xdb = xd.astype(bf16)
                ecs = jnp.exp(csc)
                dec = jnp.exp(cl)
                e = jnp.exp(cl - csc)
                hp = hs_ref[0, h]
                hpb = hp.astype(bf16)
                dhn = dh_ref[h]
                dhnb = dhn.astype(bf16)
                dd_h = jnp.sum(jnp.sum(dyv * xv, axis=1, keepdims=True), axis=0, keepdims=True)
                dx = dyv * dsk[:, h:h + 1]
                ch = _dot(cm, hpb, NT)
                dye = dyv * ecs
                dyeb = dye.astype(bf16)
                dcg = dcg + _dot(dyeb, hpb)
                dhp = _dot(dyeb, cm, TN)
                dcs_col = jnp.sum(dye * ch, axis=1, keepdims=True)
                dm = _dot(dyb, xdb, NT)
                dxd = _dot(mf.astype(bf16), dyb, TN)
                dgm = dgm + dm * dk
                wmat = dm * mf
                dcs_col = dcs_col + jnp.sum(wmat, axis=1, keepdims=True)
                dcs_row = -jnp.sum(wmat, axis=0, keepdims=True)
                ddec = jnp.sum(jnp.sum(hp * dhn, axis=1, keepdims=True), axis=0, keepdims=True)
                dxe = _dot(bm, dhnb, NT)
                dxd = dxd + dxe * e
                de_e = jnp.sum(dxe * xd, axis=1, keepdims=True) * e
                dbg = dbg + _dot((xd * e).astype(bf16), dhnb)
                dcs_col = dcs_col - de_e
                dlast = ddec * dec + jnp.sum(de_e, axis=0, keepdims=True)
                dh_ref[h] = dhp + dec * dhn
                dx = dx + dxd * dtc
                ddt_h = jnp.sum(dxd * xv, axis=1, keepdims=True)
                is_h = lane == h
                dcs_c = dcs_c + jnp.where(is_h, dcs_col, 0.0) + jnp.where(is_h & (rowl == l - 1), dlast, 0.0)
                dcs_r = dcs_r + jnp.where(sub == h, dcs_row, 0.0)
                ddt_c = ddt_c + jnp.where(is_h, ddt_h, 0.0)
                dd_row = dd_row + jnp.where(lane1 == h, dd_h, 0.0)
                dpre_ref[:, hsl] = (dx * _dsilu(pre_ref[:, hsl].astype(f32))).astype(bf16)
            dgb = dgm.astype(bf16)
            dcg = dcg + _dot(dgb, bm)
            dbg = dbg + _dot(dgb, cm, TN)
            dpre_ref[:, bsl] = (dbg * _dsilu(pre_ref[:, bsl].astype(f32))).astype(bf16)
            dpre_ref[:, csl] = (dcg * _dsilu(pre_ref[:, csl].astype(f32))).astype(bf16)
        dcs = dcs_c + dcs_r.T
        dda = _dot((row <= col).astype(f32), dcs, precision=HIGHEST)
        ddt = ddt_c + dda * a
        ddtraw = jnp.where(lane < N_HEADS, ddt * _sigmoid(zraw), 0.0)
        ddt_ref[...] = ddtraw
        dprm_ref[0:1, :] += jnp.sum(ddtraw, axis=0, keepdims=True)
        dprm_ref[1:2, :] += jnp.where(lane1 < N_HEADS, jnp.sum(dda * dt, axis=0, keepdims=True) * a, 0.0)
        dprm_ref[2:3, :] += dd_row

    def rev(i):
        return (nc - 1 - i, 0)

    return pl.pallas_call(
        body, name="ssd_bwd", grid=(nc,),
        in_specs=[pl.BlockSpec((l, XBC_W), rev), pl.BlockSpec((l, XBC_W), rev),
                  pl.BlockSpec((l, 128), rev), pl.BlockSpec((8, 128), lambda i: (0, 0)),
                  pl.BlockSpec((1, N_HEADS, HEAD_DIM, D_STATE), lambda i: (nc - 1 - i, 0, 0, 0)),
                  pl.BlockSpec((l, D_MODEL), rev)],
        out_specs=[pl.BlockSpec((l, XBC_W), rev), pl.BlockSpec((l, 128), rev), pl.BlockSpec((8, 128), lambda i: (0, 0))],
        out_shape=[jax.ShapeDtypeStruct((s, XBC_W), bf16), jax.ShapeDtypeStruct((s, 128), f32),
                   jax.ShapeDtypeStruct((8, 128), f32)],
        scratch_shapes=[pltpu.VMEM((N_HEADS, HEAD_DIM, D_STATE), f32)],
        compiler_params=_cp("arbitrary"))(act, pre, dtf, prm, hs, dy)


def _fox_cumsum(dtf, prm):
    s = dtf.shape[0]
    l = CHUNK

    def body(f_ref, prm_ref, c_ref, carry_ref):
        @pl.when(pl.program_id(0) == 0)
        def _():
            carry_ref[...] = jnp.zeros_like(carry_ref)
        lf = _log_sigmoid(f_ref[...] + prm_ref[3:4, :])
        row = lax.broadcasted_iota(jnp.int32, (l, l), 0)
        col = lax.broadcasted_iota(jnp.int32, (l, l), 1)
        c = _dot((row >= col).astype(f32), lf, precision=HIGHEST) + carry_ref[...]
        c_ref[...] = c
        carry_ref[...] = c[l - 1:l, :]

    return pl.pallas_call(
        body, name="fox_cumsum", grid=(s // l,),
        in_specs=[pl.BlockSpec((l, 128), lambda i: (i, 0)), pl.BlockSpec((8, 128), lambda i: (0, 0))],
        out_specs=pl.BlockSpec((l, 128), lambda i: (i, 0)),
        out_shape=jax.ShapeDtypeStruct((s, 128), f32),
        scratch_shapes=[pltpu.VMEM((1, 128), f32)],
        compiler_params=_cp("arbitrary"))(dtf, prm)


def _position():
    return lax.axis_index("x"), lax.axis_index("y"), lax.axis_index("c")


def _exchange_sems(n):
    return [pltpu.SemaphoreType.DMA((n, N_DEV - 1)), pltpu.SemaphoreType.DMA((n, N_DEV - 1)),
            pltpu.SemaphoreType.DMA((n,))]


def _exchange_copies(g_refs, r_refs, send_sems, recv_sems, local_sems, gather=False):
    n = len(g_refs)
    x, y, cc = _position()
    me = 4 * x + 2 * y + cc

    def src(a, j):
        return g_refs[a] if gather else g_refs[a].at[j]

    local = [pltpu.make_async_copy(src(a, me), r_refs[a].at[me], local_sems.at[a]) for a in range(n)]
    sends, recvs = [], []
    for k in range(1, N_DEV):
        px = 1 - x if k & 4 else x
        py = 1 - y if k & 2 else y
        pc = 1 - cc if k & 1 else cc
        pid = 4 * px + 2 * py + pc
        for a in range(n):
            sends.append(pltpu.make_async_remote_copy(
                src_ref=src(a, pid), dst_ref=r_refs[a].at[me],
                send_sem=send_sems.at[a, k - 1], recv_sem=recv_sems.at[a, k - 1],
                device_id=(px, py, pc), device_id_type=pl.DeviceIdType.MESH))
            recvs.append(pltpu.make_async_remote_copy(
                src_ref=src(a, pid), dst_ref=r_refs[a].at[pid],
                send_sem=send_sems.at[a, k - 1], recv_sem=recv_sems.at[a, k - 1],
                device_id=(px, py, pc), device_id_type=pl.DeviceIdType.MESH))
    return local, sends, recvs


def _exchange_start(copies):
    local, sends, _ = copies
    for cp in local + sends:
        cp.start()


def _exchange_wait(copies):
    local, sends, recvs = copies
    for cp in recvs:
        cp.wait_recv()
    for cp in sends:
        cp.wait_send()
    for cp in local:
        cp.wait()


AUG = HEAD_DIM
N_PAIRS = N_HEADS // 2
V_BLOCK = 2 * D_MODEL // 128


def _split3(x):
    hi = x.astype(bf16)
    r1 = x - hi.astype(f32)
    mid = r1.astype(bf16)
    lo = (r1 - mid.astype(f32)).astype(bf16)
    return hi.astype(f32), mid.astype(f32), lo.astype(f32)


def _fox_prep(qkv, c):
    s = qkv.shape[0]
    ts = min(CONV_ROW_TILE, s)
    kb = D_MODEL // 128

    def body(q_ref, k_ref, c_ref, qa_ref, ka_ref):
        lane = lax.broadcasted_iota(jnp.int32, (ts, 128), 1)
        low = lane < HEAD_DIM
        for h in range(N_HEADS):
            psl = slice(128 * (h // 2), 128 * (h // 2 + 1))
            qv = q_ref[:, psl].astype(f32) * (HEAD_DIM ** -0.5)
            kv = k_ref[:, psl].astype(f32)
            if h % 2:
                qv = pltpu.roll(qv, HEAD_DIM, 1)
                kv = pltpu.roll(kv, HEAD_DIM, 1)
            hi, mid, lo = _split3(c_ref[:, F_LANE + h:F_LANE + h + 1])
            ones = jnp.where((lane >= AUG + 3) & (lane < AUG + 6), 1.0, 0.0)
            cq = jnp.where(lane == AUG, hi, jnp.where(lane == AUG + 1, mid, jnp.where(lane == AUG + 2, lo, ones)))
            qa_ref[h] = jnp.where(low, qv, cq).astype(bf16)
            onek = jnp.where((lane >= AUG) & (lane < AUG + 3), 1.0, 0.0)
            ck = jnp.where(lane == AUG + 3, -hi, jnp.where(lane == AUG + 4, -mid, jnp.where(lane == AUG + 5, -lo, onek)))
            ka_ref[h] = jnp.where(low, kv, ck).astype(bf16)

    hm = pl.BlockSpec((N_HEADS, ts, 128), lambda i: (0, i, 0))
    return pl.pallas_call(
        body, name="fox_prep", grid=(s // ts,),
        in_specs=[_rowspec(ts, D_MODEL, 0), _rowspec(ts, D_MODEL, 1), _rowspec(ts, 128)],
        out_specs=[hm, hm], out_shape=[jax.ShapeDtypeStruct((N_HEADS, s, 128), bf16)] * 2,
        compiler_params=_cp("parallel"))(qkv, qkv, c)


def _fox_fwd(qa, ka, qkv, ws):
    s = qkv.shape[0]
    t = min(ATTN_FWD_TILE, s)
    nq = s // t
    n = len(ws)

    def body(qa_ref, ka_ref, v_ref, *rest):
        w_refs, (o_ref, lse_ref), wg_refs, sems = rest[:n], rest[n:n + 2], rest[n + 2:2 * n + 2], rest[2 * n + 2:]
        qi = pl.program_id(1)
        copies = _exchange_copies(w_refs, wg_refs, *sems, gather=True)

        @pl.when((pl.program_id(0) == 0) & (qi == 0))
        def _():
            _exchange_start(copies)
        low = lax.broadcasted_iota(jnp.int32, (t, 128), 1) < HEAD_DIM
        row = lax.broadcasted_iota(jnp.int32, (t, t), 0)
        col = lax.broadcasted_iota(jnp.int32, (t, t), 1)

        def tile(ki, carry, diagonal):
            stats, acc = carry
            koff = pl.multiple_of(ki * t, t)
            v = v_ref[pl.ds(koff, t), :]
            vh = (jnp.where(low, v, jnp.zeros_like(v)), jnp.where(low, jnp.zeros_like(v), v))
            new_stats, alphas, pv = [], [], None
            for r in range(2):
                m_old, l_old = stats[r]
                sc = _dot(qa_ref[r], ka_ref[r, pl.ds(koff, t), :], NT)
                if diagonal:
                    sc = jnp.where(col <= row, sc, NEG)
                m_new = jnp.maximum(m_old, jnp.max(sc, axis=1, keepdims=True))
                p = jnp.exp(sc - m_new)
                alpha = jnp.exp(m_old - m_new)
                new_stats.append((m_new, alpha * l_old + jnp.sum(p, axis=1, keepdims=True)))
                alphas.append(alpha)
                d = _dot(p.astype(bf16), vh[r])
                pv = d if pv is None else pv + d
            acc = acc * jnp.where(low, alphas[0], alphas[1]) + pv
            return tuple(new_stats), acc

        init = (((jnp.full((t, 1), NEG, f32), jnp.zeros((t, 1), f32)),) * 2, jnp.zeros((t, 128), f32))
        carry = lax.fori_loop(0, qi, lambda ki, cr: tile(ki, cr, False), init)
        stats, acc = tile(qi, carry, True)
        o_ref[...] = (acc / jnp.where(low, stats[0][1], stats[1][1])).astype(bf16)
        for r in range(2):
            lse = stats[r][0] + jnp.log(stats[r][1])
            lse_ref[r] = jnp.broadcast_to(lse, (t, 128)).T[0:1, :]

        @pl.when((pl.program_id(0) == N_PAIRS - 1) & (qi == nq - 1))
        def _():
            _exchange_wait(copies)

    anyspec = pl.BlockSpec(memory_space=pl.ANY)
    outs = pl.pallas_call(
        body, name="fox_fwd", grid=(N_PAIRS, nq),
        in_specs=[pl.BlockSpec((2, t, 128), lambda j, qi: (j, qi, 0)),
                  pl.BlockSpec((2, s, 128), lambda j, qi: (j, 0, 0)),
                  pl.BlockSpec((s, 128), lambda j, qi: (0, V_BLOCK + j))] + [anyspec] * n,
        out_specs=[pl.BlockSpec((t, 128), lambda j, qi: (qi, j)), pl.BlockSpec((2, 1, t), lambda j, qi: (j, 0, qi))]
        + [anyspec] * n,
        out_shape=[jax.ShapeDtypeStruct((s, D_MODEL), bf16), jax.ShapeDtypeStruct((N_HEADS, 1, s), f32)]
        + [jax.ShapeDtypeStruct((N_DEV,) + w.shape, w.dtype) for w in ws],
        scratch_shapes=_exchange_sems(n),
        compiler_params=_cp("arbitrary", "arbitrary"))(qa, ka, qkv, *ws)
    return outs[0], outs[1], outs[2:]


def _fox_bwd(qa, ka, qkv, do, lse, delta, gs):
    s = qkv.shape[0]
    t = min(ATTN_TILE, s)
    nq = s // t
    n = len(gs)

    def body(qa_ref, ka_ref, v_ref, do_ref, lse_ref, dl_ref, *rest):
        g_refs, (dq_ref, dk_ref, dv_ref), r_refs, sems = rest[:n], rest[n:n + 3], rest[n + 3:2 * n + 3], rest[2 * n + 3:]
        ki = pl.program_id(1)
        copies = _exchange_copies(g_refs, r_refs, *sems)

        @pl.when((pl.program_id(0) == 0) & (ki == 0))
        def _():
            _exchange_start(copies)

        @pl.when(ki == 0)
        def _():
            dq_ref[...] = jnp.zeros_like(dq_ref)
        low = lax.broadcasted_iota(jnp.int32, (t, 128), 1) < HEAD_DIM
        row = lax.broadcasted_iota(jnp.int32, (t, t), 0)
        col = lax.broadcasted_iota(jnp.int32, (t, t), 1)
        v = v_ref[...]
        zero = jnp.zeros_like(v)
        vh = (jnp.where(low, v, zero), jnp.where(low, zero, v))

        def tile(qi, carry, diagonal):
            dks, dv = carry
            qoff = pl.multiple_of(qi * t, t)
            dov = do_ref[pl.ds(qoff, t), :]
            doh = (jnp.where(low, dov, zero), jnp.where(low, zero, dov))
            new_dks = []
            for r in range(2):
                qt = qa_ref[r, pl.ds(qoff, t), :]
                sct = _dot(ka_ref[r], qt, NT)
                if diagonal:
                    sct = jnp.where(row <= col, sct, NEG)
                pt = jnp.exp(sct - lse_ref[r, :, pl.ds(qoff, t)])
                dpt = _dot(vh[r], dov, NT)
                dst = (pt * (dpt - dl_ref[r, :, pl.ds(qoff, t)])).astype(bf16)
                dv = dv + _dot(pt.astype(bf16), doh[r])
                new_dks.append(dks[r] + _dot(dst, qt))
                dq_ref[r, pl.ds(qoff, t), :] += _dot(dst, ka_ref[r], TN)
            return tuple(new_dks), dv

        zacc = jnp.zeros((t, 128), f32)
        carry = tile(ki, ((zacc, zacc), zacc), True)
        dks, dv = lax.fori_loop(ki + 1, nq, lambda qi, cr: tile(qi, cr, False), carry)
        dk_ref[0] = dks[0]
        dk_ref[1] = dks[1]
        dv_ref[...] = dv.astype(bf16)

        @pl.when((pl.program_id(0) == N_PAIRS - 1) & (ki == nq - 1))
        def _():
            _exchange_wait(copies)

    anyspec = pl.BlockSpec(memory_space=pl.ANY)
    outs = pl.pallas_call(
        body, name="fox_bwd", grid=(N_PAIRS, nq),
        in_specs=[pl.BlockSpec((2, s, 128), lambda j, ki: (j, 0, 0)),
                  pl.BlockSpec((2, t, 128), lambda j, ki: (j, ki, 0)),
                  pl.BlockSpec((t, 128), lambda j, ki: (ki, V_BLOCK + j)),
                  pl.BlockSpec((s, 128), lambda j, ki: (0, j)),
                  pl.BlockSpec((2, 1, s), lambda j, ki: (j, 0, 0)),
                  pl.BlockSpec((2, 1, s), lambda j, ki: (j, 0, 0))] + [anyspec] * n,
        out_specs=[pl.BlockSpec((2, s, 128), lambda j, ki: (j, 0, 0)),
                   pl.BlockSpec((2, t, 128), lambda j, ki: (j, ki, 0)),
                   pl.BlockSpec((t, 128), lambda j, ki: (ki, j))] + [anyspec] * n,
        out_shape=[jax.ShapeDtypeStruct((N_HEADS, s, 128), f32), jax.ShapeDtypeStruct((N_HEADS, s, 128), f32),
                   jax.ShapeDtypeStruct((s, D_MODEL), bf16)] + [jax.ShapeDtypeStruct(g.shape, g.dtype) for g in gs],
        scratch_shapes=_exchange_sems(n),
        compiler_params=_cp("arbitrary", "arbitrary"))(qa, ka, qkv, do, lse, delta, *gs)
    return outs[0], outs[1], outs[2], outs[3:]


def _fox_bwd_post(dq_hm, dk_hm):
    s = dq_hm.shape[1]
    ts = min(CONV_ROW_TILE, s)

    def body(dq_ref, dk_ref, q_ref, k_ref, dc_ref):
        lane = lax.broadcasted_iota(jnp.int32, (ts, 128), 1)
        dc = jnp.zeros((ts, 128), f32)
        for h in range(N_HEADS):
            hsl = slice(HEAD_DIM * h, HEAD_DIM * (h + 1))
            dqv = dq_ref[h]
            dkv = dk_ref[h]
            q_ref[:, hsl] = (dqv[:, 0:HEAD_DIM] * (HEAD_DIM ** -0.5)).astype(bf16)
            k_ref[:, hsl] = dkv[:, 0:HEAD_DIM].astype(bf16)
            dc = dc + jnp.where(lane == F_LANE + h, dqv[:, AUG:AUG + 1] - dkv[:, AUG + 3:AUG + 4], 0.0)
        dc_ref[...] = dc

    hm = pl.BlockSpec((N_HEADS, ts, 128), lambda i: (0, i, 0))
    return pl.pallas_call(
        body, name="fox_bwd_post", grid=(s // ts,), in_specs=[hm, hm],
        out_specs=[_rowspec(ts, D_MODEL), _rowspec(ts, D_MODEL), _rowspec(ts, 128)],
        out_shape=[jax.ShapeDtypeStruct((s, D_MODEL), bf16), jax.ShapeDtypeStruct((s, D_MODEL), bf16),
                   jax.ShapeDtypeStruct((s, 128), f32)],
        compiler_params=_cp("parallel"))(dq_hm, dk_hm)


def _fox_gate_bwd(dc, dtf, prm, ddt_raw):
    s = dtf.shape[0]
    l = CHUNK
    nb = s // l

    def body(dc_ref, f_ref, prm_ref, ddt_ref, out_ref, dfb_ref, carry_ref):
        @pl.when(pl.program_id(0) == 0)
        def _():
            carry_ref[...] = jnp.zeros_like(carry_ref)
            dfb_ref[...] = jnp.zeros_like(dfb_ref)
        dc = dc_ref[...]
        row = lax.broadcasted_iota(jnp.int32, (l, l), 0)
        col = lax.broadcasted_iota(jnp.int32, (l, l), 1)
        dlf = _dot((row <= col).astype(f32), dc, precision=HIGHEST) + carry_ref[...]
        carry_ref[...] = dlf[0:1, :]
        lane = lax.broadcasted_iota(jnp.int32, (l, 128), 1)
        is_f = (lane >= F_LANE) & (lane < F_LANE + N_HEADS)
        dfr = jnp.where(is_f, dlf * _sigmoid(-(f_ref[...] + prm_ref[3:4, :])), 0.0)
        dfb_ref[...] += jnp.sum(dfr, axis=0, keepdims=True)
        out_ref[...] = ddt_ref[...] + dfr

    def rev(i):
        return (nb - 1 - i, 0)

    return pl.pallas_call(
        body, name="fox_gate_bwd", grid=(nb,),
        in_specs=[pl.BlockSpec((l, 128), rev), pl.BlockSpec((l, 128), rev), pl.BlockSpec((8, 128), lambda i: (0, 0)),
                  pl.BlockSpec((l, 128), rev)],
        out_specs=[pl.BlockSpec((l, 128), rev), pl.BlockSpec((1, 128), lambda i: (0, 0))],
        out_shape=[jax.ShapeDtypeStruct((s, 128), f32), jax.ShapeDtypeStruct((1, 128), f32)],
        scratch_shapes=[pltpu.VMEM((1, 128), f32)],
        compiler_params=_cp("arbitrary"))(dc, dtf, prm, ddt_raw)


def _all_gather(xl, name):
    r, c = xl.shape

    def body(x_ref, out_ref, send_sems, recv_sems, local_sem):
        x, y, cc = _position()
        me, sibling = (x, y, cc), (x, y, 1 - cc)
        chips = [(1 - x, y), (x, 1 - y), (1 - x, 1 - y)]

        def slot(px, py, pc):
            return out_ref.at[4 * px + 2 * py + pc]

        def copy(k, block, to, src=None):
            return pltpu.make_async_remote_copy(
                src_ref=slot(*block) if src is None else src, dst_ref=slot(*block),
                send_sem=send_sems.at[k], recv_sem=recv_sems.at[k],
                device_id=to, device_id_type=pl.DeviceIdType.MESH)

        mine = pltpu.make_async_copy(x_ref, slot(*me), local_sem)
        mine.start()
        first = [copy(0, me, sibling, src=x_ref)]
        first += [copy(1 + j, me, (*chip, cc), src=x_ref) for j, chip in enumerate(chips)]
        for cp in first:
            cp.start()
        passed = [copy(4 + j, (*chip, cc), sibling) for j, chip in enumerate(chips)]
        for j, chip in enumerate(chips):
            copy(1 + j, (*chip, cc), me).wait_recv()
            passed[j].start()
        copy(0, sibling, me).wait_recv()
        for j, chip in enumerate(chips):
            copy(4 + j, (*chip, 1 - cc), me).wait_recv()
        for cp in first + passed:
            cp.wait_send()
        mine.wait()

    return pl.pallas_call(
        body, name=name,
        out_shape=jax.ShapeDtypeStruct((N_DEV, r, c), xl.dtype),
        in_specs=[pl.BlockSpec(memory_space=pl.ANY)], out_specs=pl.BlockSpec(memory_space=pl.ANY),
        scratch_shapes=[pltpu.SemaphoreType.DMA((7,)), pltpu.SemaphoreType.DMA((7,)), pltpu.SemaphoreType.DMA],
    )(xl)


def _sum_parts(parts, name):
    n, r, c = parts.shape

    def body(p_ref, o_ref):
        g = p_ref[0]
        for i in range(1, n):
            g = g + p_ref[i]
        o_ref[...] = g

    return pl.pallas_call(body, name=name, out_shape=jax.ShapeDtypeStruct((r, c), f32))(parts)


def _adamw(w, m, v, parts, name, tr=128):
    r, c = w.shape
    n = parts.shape[0]
    tr = min(tr, r)
    c1 = 1.0 - ADAM_B1 ** ADAM_STEP
    c2 = 1.0 - ADAM_B2 ** ADAM_STEP

    def body(w_ref, m_ref, v_ref, p_ref, g_ref, d_ref, nm_ref, nv_ref):
        g = p_ref[0].astype(f32)
        for i in range(1, n):
            g = g + p_ref[i].astype(f32)
        g_ref[...] = g
        nm = ADAM_B1 * m_ref[...] + (1.0 - ADAM_B1) * g
        nv = ADAM_B2 * v_ref[...] + (1.0 - ADAM_B2) * (g * g)
        nm_ref[...] = nm
        nv_ref[...] = nv
        d_ref[...] = -ADAM_LR * ((nm / c1) / (jnp.sqrt(nv / c2) + ADAM_EPS) + ADAM_WD * w_ref[...])

    blk = pl.BlockSpec((tr, c), lambda i: (i, 0))
    return pl.pallas_call(
        body, name=name, grid=(r // tr,),
        in_specs=[blk, blk, blk, pl.BlockSpec((n, tr, c), lambda i: (0, i, 0))],
        out_specs=[blk] * 4, out_shape=[jax.ShapeDtypeStruct((r, c), f32)] * 4,
        compiler_params=_cp("parallel"))(w, m, v, parts)


def _lanes(w):
    return -(-w // 128) * 128


def _pack(arrs):
    rows = []
    for a in arrs:
        k, w = a.shape
        if w % 128:
            a = jnp.pad(a, ((0, 0), (0, _lanes(w) - w)))
        rows.append(a.reshape(-1, 128))
    out = jnp.concatenate(rows, axis=0)
    pad = -out.shape[0] % 8
    return jnp.pad(out, ((0, pad), (0, 0))) if pad else out


def _unpack(packed, shapes):
    outs, off = [], 0
    lead = packed.shape[:-2]
    for k, w in shapes:
        nrow = k * _lanes(w) // 128
        a = packed[..., off:off + nrow, :].reshape(*lead, k, _lanes(w))[..., :w]
        outs.append(a)
        off += nrow
    return outs


def _gathered_cols(a):
    n, k, wl = a.shape
    return jnp.transpose(a, (1, 0, 2)).reshape(k, n * wl)


def _col_shards(a):
    k, w = a.shape
    return jnp.transpose(a.reshape(k, N_DEV, w // N_DEV), (1, 0, 2))


SMALL_PARAMS = (
    ("e_norm_pre", 1, 1024, False), ("e_conv_w", 4, 2048, True), ("e_conv_b", 1, 2048, False),
    ("e_dt_bias", 1, 16, False), ("e_a_log", 1, 16, False), ("e_d_skip", 1, 16, False), ("e_fgate_b", 1, 16, False),
    ("e_ssd_norm", 1, 1024, False), ("e_norm_post", 1, 1024, False), ("o_norm_pre", 1, 1024, True),
    ("o_conv_w", 31, 2048, True), ("o_conv_b", 1, 2048, True), ("o_ln_g", 1, 2048, True), ("o_ln_b", 1, 2048, True),
    ("o_norm_post", 1, 1024, True),
)
BIG_PARAMS = ("e_w_in", "e_w_out", "o_w_in", "o_w_out")
WEIGHT_ORDER = ("e_norm_pre", "e_w_in", "e_conv_w", "e_conv_b", "e_dt_bias", "e_a_log", "e_d_skip", "e_fgate_b",
                "e_ssd_norm", "e_w_out", "e_norm_post", "o_norm_pre", "o_w_in", "o_conv_w", "o_conv_b", "o_ln_g",
                "o_ln_b", "o_w_out", "o_norm_post")
E_IN = 7200
O_IN = 6144


def kernel(x, e_norm_pre, e_w_in, e_conv_w, e_conv_b, e_dt_bias, e_a_log, e_d_skip, e_fgate_b, e_ssd_norm, e_w_out, e_norm_post, o_norm_pre, o_w_in, o_conv_w, o_conv_b, o_ln_g, o_ln_b, o_w_out, o_norm_post, loss_target, m_e_norm_pre, m_e_w_in, m_e_conv_w, m_e_conv_b, m_e_dt_bias, m_e_a_log, m_e_d_skip, m_e_fgate_b, m_e_ssd_norm, m_e_w_out, m_e_norm_post, m_o_norm_pre, m_o_w_in, m_o_conv_w, m_o_conv_b, m_o_ln_g, m_o_ln_b, m_o_w_out, m_o_norm_post, v_e_norm_pre, v_e_w_in, v_e_conv_w, v_e_conv_b, v_e_dt_bias, v_e_a_log, v_e_d_skip, v_e_fgate_b, v_e_ssd_norm, v_e_w_out, v_e_norm_post, v_o_norm_pre, v_o_w_in, v_o_conv_w, v_o_conv_b, v_o_ln_g, v_o_ln_b, v_o_w_out, v_o_norm_post):
    given = dict(locals())
    w_in = {n: given[n] for n in WEIGHT_ORDER}
    m_in = {n: given["m_" + n] for n in WEIGHT_ORDER}
    v_in = {n: given["v_" + n] for n in WEIGHT_ORDER}

    def mat(a):
        return a.reshape(a.shape[-2:])

    xs = mat(x)
    tgt = mat(loss_target)
    xi, yi, ci = _position()
    me = 4 * xi + 2 * yi + ci
    ow = O_IN // N_DEV
    wr = D_CONV // N_DEV

    e_w_in_f = _gathered_cols(_all_gather(mat(e_w_in).astype(bf16), "gather_weights"))
    later_weights = [mat(e_w_out).astype(bf16), mat(o_w_in).astype(bf16), mat(o_w_out).astype(bf16)]
    w_z, w_xbc = e_w_in_f[:, 0:2048], e_w_in_f[:, 2048:4096]
    w_qkv = e_w_in_f[:, 4112:7184]
    w_dtf = jnp.concatenate([e_w_in_f[:, 4096:4112], e_w_in_f[:, 7184:7200], jnp.zeros((D_MODEL, 96), bf16)], axis=1)

    sharded_small = [(n, k, w) for n, k, w, sh in SMALL_PARAMS if sh]
    sg = _all_gather(_pack([mat(w_in[n]) for n, _, _ in sharded_small]), "gather_small_weights")
    full_small = {n: _gathered_cols(a)
                  for (n, _, _), a in zip(sharded_small, _unpack(sg, [(k, w // N_DEV) for _, k, w in sharded_small]))}
    for n, _, _, sh in SMALL_PARAMS:
        if not sh:
            full_small[n] = mat(w_in[n])
    p = full_small
    prm = jnp.zeros((8, 128), f32)
    prm = prm.at[0, 0:16].set(p["e_dt_bias"][0]).at[1, 0:16].set(p["e_a_log"][0]).at[2, 0:16].set(p["e_d_skip"][0])
    prm = prm.at[3, F_LANE:F_LANE + 16].set(p["e_fgate_b"][0])

    u0 = _rms_fwd(xs, p["e_norm_pre"], "rms_pre0")
    z0 = _mm_nn(u0, w_z, bf16, "proj0_z")
    xraw = _mm_nn(u0, w_xbc, bf16, "proj0_xbc")
    qkv = _mm_nn(u0, w_qkv, bf16, "proj0_qkv")
    dtf = _mm_nn(u0, w_dtf, f32, "proj0_dtf")
    pre, act = _conv_ssd_fwd(xraw, p["e_conv_w"], p["e_conv_b"])
    y, hs = _ssd_fwd(act, dtf, prm)
    qa, ka = _fox_prep(qkv, _fox_cumsum(dtf, prm))
    o, lse, (e_w_out_g, o_w_in_g, o_w_out_g) = _fox_fwd(qa, ka, qkv, later_weights)
    e_w_out_f = e_w_out_g.reshape(D_CONV, D_MODEL)
    o_w_in_f = _gathered_cols(o_w_in_g)
    o_w_out_f = o_w_out_g.reshape(D_CONV, D_MODEL)
    cat = _gate0_fwd(y, z0, o, p["e_ssd_norm"])
    out0 = _mm_nn(cat, e_w_out_f, f32, "out0")
    x1, u1 = _post0_pre1(xs, out0, p["e_norm_post"], p["o_norm_pre"])

    proj1 = _mm_nn(u1, o_w_in_f, bf16, "proj1")
    hc = _conv_glu_fwd(proj1, p["o_conv_w"], p["o_conv_b"])
    h3 = _ln_gate_fwd(hc, proj1, p["o_ln_g"], p["o_ln_b"])
    out1 = _mm_nn(h3, o_w_out_f, f32, "out1")
    dy, d_out1, dg_post1, loss_part = _final_loss(x1, out1, tgt, p["o_norm_post"])

    dh3 = _mm_nt([(d_out1, 0, o_w_out_f, 0, D_MODEL)], bf16, "dh3")
    g_o_w_out = _mm_tn(h3, d_out1, "dw_out1")
    dhc, dz1, dg_ln, db_ln = _ln_gate_bwd(hc, proj1, dh3, p["o_ln_g"], p["o_ln_b"])
    dval, dgate, dw_conv1, db_conv1 = _conv_glu_bwd(dhc, proj1, p["o_conv_w"])
    dproj1 = jnp.concatenate([dval, dgate, dz1], axis=1)
    du1 = _mm_nt([(dproj1, 0, o_w_in_f, 0, O_IN)], f32, "du1")
    g_o_w_in = _mm_tn(u1, dproj1, "dw_in1", tn=ow, blocked=True)
    dx1, d_out0, dg_pre1, dg_post0 = _mid_bwd(x1, du1, dy, out0, p["o_norm_pre"], p["e_norm_post"])

    dcat = _mm_nt([(d_out0, 0, e_w_out_f, 0, D_MODEL)], bf16, "dcat")
    g_e_w_out = _mm_tn(cat, d_out0, "dw_out0")
    dy_ssd, do, dz0, delta, dg_ssd_norm = _gate0_bwd(y, z0, o, dcat, p["e_ssd_norm"])
    early = [g_e_w_out.reshape(N_DEV, wr, D_MODEL).astype(bf16), g_o_w_in.astype(bf16),
             g_o_w_out.reshape(N_DEV, wr, D_MODEL).astype(bf16)]
    dq_hm, dk_hm, dv, early_parts = _fox_bwd(qa, ka, qkv, do, lse, delta[0:N_HEADS].reshape(N_HEADS, 1, -1), early)
    dq, dk, dc = _fox_bwd_post(dq_hm, dk_hm)
    dpre, ddt_raw, dprm = _ssd_bwd(act, pre, dtf, prm, hs, dy_ssd)
    ddtf, dfb = _fox_gate_bwd(dc, dtf, prm, ddt_raw)
    dxraw, dw_conv0, db_conv0 = _conv_ssd_bwd(dpre, xraw, p["e_conv_w"])
    gw_dtf = _mm_tn(u0, ddtf, "dw_in0_dtf")
    g_e_w_in_full = jnp.concatenate([
        _mm_tn(u0, dz0, "dw_in0_z"), _mm_tn(u0, dxraw, "dw_in0_xbc"), gw_dtf[:, 0:16],
        _mm_tn(u0, dq, "dw_in0_q"), _mm_tn(u0, dk, "dw_in0_k"), _mm_tn(u0, dv, "dw_in0_v"), gw_dtf[:, 16:32]], axis=1)
    du0, last_parts = _mm_nt(
        [(dz0, 0, w_z, 0, 2048), (dxraw, 0, w_xbc, 0, 2048), (dq, 0, w_qkv, 0, 1024), (dk, 0, w_qkv, 1, 1024),
         (dv, 0, w_qkv, 2, 1024), (ddtf, 0, w_dtf, 0, 128)], f32, "du0", gs=[_col_shards(g_e_w_in_full).astype(bf16)])
    grad_x, dg_pre0 = _first_bwd(xs, du0, dx1, p["e_norm_pre"])

    big_parts = list(last_parts) + list(early_parts)
    outs = {}
    for n, parts in zip(BIG_PARAMS, big_parts):
        outs[n] = _adamw(mat(w_in[n]), mat(m_in[n]), mat(v_in[n]), parts, "adamw_" + n)

    small_grads = {
        "e_norm_pre": dg_pre0, "e_conv_w": dw_conv0, "e_conv_b": db_conv0, "e_dt_bias": dprm[0:1, 0:16],
        "e_a_log": dprm[1:2, 0:16], "e_d_skip": dprm[2:3, 0:16], "e_fgate_b": dfb[:, F_LANE:F_LANE + 16],
        "e_ssd_norm": dg_ssd_norm, "e_norm_post": dg_post0, "o_norm_pre": dg_pre1, "o_conv_w": dw_conv1,
        "o_conv_b": db_conv1, "o_ln_g": dg_ln, "o_ln_b": db_ln, "o_norm_post": dg_post1,
    }
    gathered = _all_gather(_pack([small_grads[n] for n, _, _, _ in SMALL_PARAMS] + [loss_part]), "gather_small_grads")
    summed = _unpack(_sum_parts(gathered, "sum_small_grads"), [(k, w) for _, k, w, _ in SMALL_PARAMS] + [(1, 128)])
    loss = summed[-1][0, 0]
    g_local = []
    for (n, k, w, sh), g in zip(SMALL_PARAMS, summed):
        g_local.append(lax.dynamic_slice_in_dim(g, me * (w // N_DEV), w // N_DEV, axis=1) if sh else g)
    names = [n for n, _, _, _ in SMALL_PARAMS]
    local_shapes = [(k, w // N_DEV if sh else w) for _, k, w, sh in SMALL_PARAMS]
    res = _adamw(_pack([mat(w_in[n]) for n in names]), _pack([mat(m_in[n]) for n in names]),
                 _pack([mat(v_in[n]) for n in names]), _pack(g_local)[None], "adamw_small", tr=8)
    unpacked = [_unpack(r, local_shapes) for r in res]
    for i, n in enumerate(names):
        outs[n] = tuple(u[i] for u in unpacked)

    ret = [loss, grad_x.reshape(x.shape)]
    for j in range(4):
        ret += [outs[n][j].reshape(w_in[n].shape) for n in WEIGHT_ORDER]
    return tuple(ret)
```

```python
import jax
import jax.numpy as jnp
from jax import lax
from jax.experimental import pallas as pl
from jax.experimental.pallas import tpu as pltpu

f32 = jnp.float32
bf16 = jnp.bfloat16

N_DEV = 8
D_MODEL = 1024
N_HEADS = 16
HEAD_DIM = 64
N_GROUPS = 4
HEADS_PER_GROUP = 4
D_STATE = 128
CHUNK = 512
SSD_CONV = 4
CONV_WIDTH = 31
D_CONV = 2048
EPS = 1e-6
XBC_W = 2048
B_OFF = 1024
C_OFF = 1536
F_LANE = 16
HALO = 32

ADAM_LR = 0.001
ADAM_B1 = 0.9
ADAM_B2 = 0.999
ADAM_EPS = 1e-08
ADAM_WD = 0.01
ADAM_STEP = 10

VMEM_LIMIT_BYTES = 56 * 1024 * 1024
ROW_TILE = 512
CONV_ROW_TILE = 256
CONV_COL_TILE = 512
CONV_SUB = 32
ATTN_TILE = 512
ATTN_FWD_TILE = 1024

NT = (((1,), (1,)), ((), ()))
TN = (((0,), (0,)), ((), ()))
HIGHEST = lax.Precision.HIGHEST
NEG = -1e30


def _cp(*sem):
    return pltpu.CompilerParams(dimension_semantics=sem if sem else None, vmem_limit_bytes=VMEM_LIMIT_BYTES)


def _sigmoid(x):
    return jax.nn.sigmoid(x)


def _silu(x):
    return x * _sigmoid(x)


def _dsilu(x):
    s = _sigmoid(x)
    return s * (1.0 + x * (1.0 - s))


def _softplus(x):
    return jnp.maximum(x, 0.0) + jnp.log(1.0 + jnp.exp(-jnp.abs(x)))


def _log_sigmoid(x):
    return jnp.minimum(x, 0.0) - jnp.log(1.0 + jnp.exp(-jnp.abs(x)))


def _dot(a, b, dims=None, precision=None):
    if dims is None:
        return jnp.dot(a, b, preferred_element_type=f32, precision=precision)
    return lax.dot_general(a, b, dims, preferred_element_type=f32, precision=precision)


def _mm_nn(a, b, out_dtype, name, tm=1024, tn=1024):
    m, k = a.shape
    n = b.shape[1]
    tm, tn = min(tm, m), min(tn, n)

    def body(a_ref, b_ref, o_ref):
        o_ref[...] = _dot(a_ref[...], b_ref[...]).astype(o_ref.dtype)

    return pl.pallas_call(
        body, name=name, grid=(n // tn, m // tm),
        in_specs=[pl.BlockSpec((tm, k), lambda j, i: (i, 0)), pl.BlockSpec((k, tn), lambda j, i: (0, j))],
        out_specs=pl.BlockSpec((tm, tn), lambda j, i: (i, j)),
        out_shape=jax.ShapeDtypeStruct((m, n), out_dtype), compiler_params=_cp("parallel", "parallel"))(a, b)


def _mm_nt(pairs, out_dtype, name, tm=512, tn=512, gs=()):
    m = pairs[0][0].shape[0]
    n = pairs[0][2].shape[0]
    tm, tn = min(tm, m), min(tn, n)
    npair = len(pairs)
    ng = len(gs)
    grid = (n // tn, m // tm)

    def body(*refs):
        g_refs = refs[2 * npair:2 * npair + ng]
        o_ref = refs[2 * npair + ng]
        r_refs = refs[2 * npair + ng + 1:2 * npair + 2 * ng + 1]
        sems = refs[2 * npair + 2 * ng + 1:]
        if ng:
            copies = _exchange_copies(g_refs, r_refs, *sems)

            @pl.when((pl.program_id(0) == 0) & (pl.program_id(1) == 0))
            def _():
                _exchange_start(copies)
        acc = None
        for p in range(npair):
            d = _dot(refs[2 * p][...].astype(bf16), refs[2 * p + 1][...], NT)
            acc = d if acc is None else acc + d
        o_ref[...] = acc.astype(o_ref.dtype)
        if ng:
            @pl.when((pl.program_id(0) == grid[0] - 1) & (pl.program_id(1) == grid[1] - 1))
            def _():
                _exchange_wait(copies)

    in_specs, args = [], []
    for a, acb, b, bcb, k in pairs:
        in_specs.append(pl.BlockSpec((tm, k), lambda j, i, acb=acb: (i, acb)))
        in_specs.append(pl.BlockSpec((tn, k), lambda j, i, bcb=bcb: (j, bcb)))
        args += [a, b]
    anyspec = pl.BlockSpec(memory_space=pl.ANY)
    outs = pl.pallas_call(
        body, name=name, grid=grid, in_specs=in_specs + [anyspec] * ng,
        out_specs=[pl.BlockSpec((tm, tn), lambda j, i: (i, j))] + [anyspec] * ng,
        out_shape=[jax.ShapeDtypeStruct((m, n), out_dtype)] + [jax.ShapeDtypeStruct(g.shape, g.dtype) for g in gs],
        scratch_shapes=_exchange_sems(ng) if ng else [],
        compiler_params=_cp("arbitrary", "arbitrary") if ng else _cp("parallel", "parallel"))(*args, *gs)
    return (outs[0], outs[1:]) if ng else outs[0]


def _mm_tn(a, b, name, a_cb=0, am=None, b_cb=0, bn=None, tn=1024, tk=1024, blocked=False):
    k = a.shape[0]
    am = a.shape[1] if am is None else am
    bn = b.shape[1] if bn is None else bn
    tm = min(1024, am)
    tn, tk = min(tn, bn), min(tk, k)
    a_off, b_off = a_cb * (am // tm), b_cb * (bn // tn)

    def body(a_ref, b_ref, o_ref):
        @pl.when(pl.program_id(2) == 0)
        def _():
            o_ref[...] = jnp.zeros_like(o_ref)
        d = _dot(a_ref[...].astype(bf16), b_ref[...].astype(bf16), TN)
        o_ref[...] += d.reshape(o_ref.shape)

    if blocked:
        out_spec = pl.BlockSpec((1, tm, tn), lambda i, j, kk: (j, i, 0))
        out_shape = jax.ShapeDtypeStruct((bn // tn, am, tn), f32)
    else:
        out_spec = pl.BlockSpec((tm, tn), lambda i, j, kk: (i, j))
        out_shape = jax.ShapeDtypeStruct((am, bn), f32)
    return pl.pallas_call(
        body, name=name, grid=(am // tm, bn // tn, k // tk),
        in_specs=[pl.BlockSpec((tk, tm), lambda i, j, kk: (kk, a_off + i)),
                  pl.BlockSpec((tk, tn), lambda i, j, kk: (kk, b_off + j))],
        out_specs=out_spec, out_shape=out_shape,
        compiler_params=_cp("parallel", "parallel", "arbitrary"))(a, b)


def _rowspec(ts, w, cb=0):
    return pl.BlockSpec((ts, w), lambda i: (i, cb))


def _vecspec(w):
    return pl.BlockSpec((1, w), lambda i: (0, 0))


def _rms_fwd(x, g, name):
    s, d = x.shape
    ts = min(ROW_TILE, s)

    def body(x_ref, g_ref, u_ref):
        xv = x_ref[...]
        r = lax.rsqrt(jnp.mean(xv * xv, axis=-1, keepdims=True) + EPS)
        u_ref[...] = (xv * r * g_ref[...]).astype(bf16)

    return pl.pallas_call(
        body, name=name, grid=(s // ts,), in_specs=[_rowspec(ts, d), _vecspec(d)], out_specs=_rowspec(ts, d),
        out_shape=jax.ShapeDtypeStruct((s, d), bf16), compiler_params=_cp("parallel"))(x, g)


def _rms_bwd_vals(xv, g, dy):
    r = lax.rsqrt(jnp.mean(xv * xv, axis=-1, keepdims=True) + EPS)
    xh = xv * r
    dg = jnp.sum(dy * xh, axis=0, keepdims=True)
    dxh = dy * g
    dx = r * (dxh - xh * jnp.mean(dxh * xh, axis=-1, keepdims=True))
    return dx, dg


def _gate0_fwd(y, z, o, ssd_norm):
    s = y.shape[0]
    ts = min(ROW_TILE, s)
    gw = D_MODEL // N_GROUPS

    def body(y_ref, zs_ref, zf_ref, o_ref, w_ref, cat_ref):
        yg = y_ref[...].astype(f32) * _silu(zs_ref[...].astype(f32))
        for g in range(N_GROUPS):
            seg = yg[:, gw * g:gw * (g + 1)]
            r = lax.rsqrt(jnp.mean(seg * seg, axis=-1, keepdims=True) + EPS)
            cat_ref[:, gw * g:gw * (g + 1)] = (seg * r * w_ref[:, gw * g:gw * (g + 1)]).astype(bf16)
        cat_ref[:, D_MODEL:] = (o_ref[...].astype(f32) * _silu(zf_ref[...].astype(f32))).astype(bf16)

    return pl.pallas_call(
        body, name="gate0_fwd", grid=(s // ts,),
        in_specs=[_rowspec(ts, D_MODEL), _rowspec(ts, D_MODEL, 0), _rowspec(ts, D_MODEL, 1), _rowspec(ts, D_MODEL),
                  _vecspec(D_MODEL)],
        out_specs=_rowspec(ts, 2 * D_MODEL),
        out_shape=jax.ShapeDtypeStruct((s, 2 * D_MODEL), bf16), compiler_params=_cp("parallel"))(y, z, z, o, ssd_norm)


def _post0_pre1(x, out0, g_post0, g_pre1):
    s, d = x.shape
    ts = min(ROW_TILE, s)

    def body(x_ref, o_ref, gp_ref, gn_ref, x1_ref, u1_ref):
        ov = o_ref[...]
        r = lax.rsqrt(jnp.mean(ov * ov, axis=-1, keepdims=True) + EPS)
        x1 = x_ref[...] + ov * r * gp_ref[...]
        x1_ref[...] = x1
        r1 = lax.rsqrt(jnp.mean(x1 * x1, axis=-1, keepdims=True) + EPS)
        u1_ref[...] = (x1 * r1 * gn_ref[...]).astype(bf16)

    return pl.pallas_call(
        body, name="post0_pre1", grid=(s // ts,),
        in_specs=[_rowspec(ts, d), _rowspec(ts, d), _vecspec(d), _vecspec(d)],
        out_specs=[_rowspec(ts, d), _rowspec(ts, d)],
        out_shape=[jax.ShapeDtypeStruct((s, d), f32), jax.ShapeDtypeStruct((s, d), bf16)],
        compiler_params=_cp("parallel"))(x, out0, g_post0, g_pre1)


def _ln_vals(hc, g, b):
    mu = jnp.mean(hc, axis=-1, keepdims=True)
    xc = hc - mu
    rstd = lax.rsqrt(jnp.mean(xc * xc, axis=-1, keepdims=True) + EPS)
    xh = xc * rstd
    return xh, rstd, xh * g + b


def _ln_gate_fwd(hc, proj1, ln_g, ln_b):
    s = hc.shape[0]
    ts = min(ROW_TILE, s)

    def body(hc_ref, z_ref, g_ref, b_ref, h3_ref):
        _, _, ln = _ln_vals(hc_ref[...].astype(f32), g_ref[...], b_ref[...])
        h3_ref[...] = (_silu(ln) * _silu(z_ref[...].astype(f32))).astype(bf16)

    return pl.pallas_call(
        body, name="ln_gate_fwd", grid=(s // ts,),
        in_specs=[_rowspec(ts, D_CONV), _rowspec(ts, D_CONV, 2), _vecspec(D_CONV), _vecspec(D_CONV)],
        out_specs=_rowspec(ts, D_CONV),
        out_shape=jax.ShapeDtypeStruct((s, D_CONV), bf16), compiler_params=_cp("parallel"))(hc, proj1, ln_g, ln_b)


def _final_loss(x1, out1, tgt, g_post1):
    s, d = x1.shape
    ts = min(ROW_TILE, s)

    def body(x1_ref, o_ref, t_ref, g_ref, dy_ref, do_ref, dg_ref, loss_ref):
        i = pl.program_id(0)

        @pl.when(i == 0)
        def _():
            dg_ref[...] = jnp.zeros_like(dg_ref)
            loss_ref[...] = jnp.zeros_like(loss_ref)
        ov = o_ref[...]
        g = g_ref[...]
        r = lax.rsqrt(jnp.mean(ov * ov, axis=-1, keepdims=True) + EPS)
        diff = x1_ref[...] + ov * r * g - t_ref[...]
        row = jnp.mean(diff * diff, axis=-1, keepdims=True)
        loss_ref[...] += jnp.broadcast_to(0.5 * jnp.sum(row, axis=0, keepdims=True), loss_ref.shape)
        dy = diff * (1.0 / d)
        dy_ref[...] = dy
        dx, dg = _rms_bwd_vals(ov, g, dy)
        do_ref[...] = dx.astype(bf16)
        dg_ref[...] += dg

    return pl.pallas_call(
        body, name="final_loss", grid=(s // ts,),
        in_specs=[_rowspec(ts, d), _rowspec(ts, d), _rowspec(ts, d), _vecspec(d)],
        out_specs=[_rowspec(ts, d), _rowspec(ts, d), _vecspec(d), _vecspec(128)],
        out_shape=[jax.ShapeDtypeStruct((s, d), f32), jax.ShapeDtypeStruct((s, d), bf16),
                   jax.ShapeDtypeStruct((1, d), f32), jax.ShapeDtypeStruct((1, 128), f32)],
        compiler_params=_cp("arbitrary"))(x1, out1, tgt, g_post1)


def _ln_gate_bwd(hc, proj1, dh3, ln_g, ln_b):
    s = hc.shape[0]
    ts = min(ROW_TILE, s)

    def body(hc_ref, z_ref, dh_ref, g_ref, b_ref, dhc_ref, dz_ref, dg_ref, db_ref):
        @pl.when(pl.program_id(0) == 0)
        def _():
            dg_ref[...] = jnp.zeros_like(dg_ref)
            db_ref[...] = jnp.zeros_like(db_ref)
        g = g_ref[...]
        xh, rstd, ln = _ln_vals(hc_ref[...].astype(f32), g, b_ref[...])
        zv = z_ref[...].astype(f32)
        dh3 = dh_ref[...].astype(f32)
        dz_ref[...] = (dh3 * _silu(ln) * _dsilu(zv)).astype(bf16)
        dln = dh3 * _silu(zv) * _dsilu(ln)
        dg_ref[...] += jnp.sum(dln * xh, axis=0, keepdims=True)
        db_ref[...] += jnp.sum(dln, axis=0, keepdims=True)
        dxh = dln * g
        dhc = rstd * (dxh - jnp.mean(dxh, axis=-1, keepdims=True) - xh * jnp.mean(dxh * xh, axis=-1, keepdims=True))
        dhc_ref[...] = dhc.astype(bf16)

    return pl.pallas_call(
        body, name="ln_gate_bwd", grid=(s // ts,),
        in_specs=[_rowspec(ts, D_CONV), _rowspec(ts, D_CONV, 2), _rowspec(ts, D_CONV), _vecspec(D_CONV),
                  _vecspec(D_CONV)],
        out_specs=[_rowspec(ts, D_CONV), _rowspec(ts, D_CONV), _vecspec(D_CONV), _vecspec(D_CONV)],
        out_shape=[jax.ShapeDtypeStruct((s, D_CONV), bf16), jax.ShapeDtypeStruct((s, D_CONV), bf16),
                   jax.ShapeDtypeStruct((1, D_CONV), f32), jax.ShapeDtypeStruct((1, D_CONV), f32)],
        compiler_params=_cp("arbitrary"))(hc, proj1, dh3, ln_g, ln_b)


def _mid_bwd(x1, du1, dy, out0, g_pre1, g_post0):
    s, d = x1.shape
    ts = min(ROW_TILE, s)

    def body(x1_ref, du_ref, dy_ref, o_ref, gn_ref, gp_ref, dx1_ref, do_ref, dgn_ref, dgp_ref):
        @pl.when(pl.program_id(0) == 0)
        def _():
            dgn_ref[...] = jnp.zeros_like(dgn_ref)
            dgp_ref[...] = jnp.zeros_like(dgp_ref)
        dxa, dgn = _rms_bwd_vals(x1_ref[...], gn_ref[...], du_ref[...])
        dx1 = dy_ref[...] + dxa
        dx1_ref[...] = dx1
        dgn_ref[...] += dgn
        dxo, dgp = _rms_bwd_vals(o_ref[...], gp_ref[...], dx1)
        do_ref[...] = dxo.astype(bf16)
        dgp_ref[...] += dgp

    return pl.pallas_call(
        body, name="mid_bwd", grid=(s // ts,),
        in_specs=[_rowspec(ts, d)] * 4 + [_vecspec(d), _vecspec(d)],
        out_specs=[_rowspec(ts, d), _rowspec(ts, d), _vecspec(d), _vecspec(d)],
        out_shape=[jax.ShapeDtypeStruct((s, d), f32), jax.ShapeDtypeStruct((s, d), bf16),
                   jax.ShapeDtypeStruct((1, d), f32), jax.ShapeDtypeStruct((1, d), f32)],
        compiler_params=_cp("arbitrary"))(x1, du1, dy, out0, g_pre1, g_post0)


def _first_bwd(x, du0, dx1, g_pre0):
    s, d = x.shape
    ts = min(ROW_TILE, s)

    def body(x_ref, du_ref, dx1_ref, g_ref, dx_ref, dg_ref):
        @pl.when(pl.program_id(0) == 0)
        def _():
            dg_ref[...] = jnp.zeros_like(dg_ref)
        dxa, dg = _rms_bwd_vals(x_ref[...], g_ref[...], du_ref[...])
        dx_ref[...] = dx1_ref[...] + dxa
        dg_ref[...] += dg

    return pl.pallas_call(
        body, name="first_bwd", grid=(s // ts,),
        in_specs=[_rowspec(ts, d)] * 3 + [_vecspec(d)],
        out_specs=[_rowspec(ts, d), _vecspec(d)],
        out_shape=[jax.ShapeDtypeStruct((s, d), f32), jax.ShapeDtypeStruct((1, d), f32)],
        compiler_params=_cp("arbitrary"))(x, du0, dx1, g_pre0)


def _gate0_bwd(y, z, o, dcat, ssd_norm):
    s = y.shape[0]
    ts = min(ROW_TILE, s)
    gw = D_MODEL // N_GROUPS

    def body(y_ref, zs_ref, zf_ref, o_ref, dn_ref, dg_ref, w_ref, dy_ref, do_ref, dz_ref, delta_ref, dw_ref):
        @pl.when(pl.program_id(0) == 0)
        def _():
            dw_ref[...] = jnp.zeros_like(dw_ref)
        yv = y_ref[...].astype(f32)
        zs = zs_ref[...].astype(f32)
        sz = _silu(zs)
        yg = yv * sz
        dyn = dn_ref[...].astype(f32)
        for g in range(N_GROUPS):
            sl = slice(gw * g, gw * (g + 1))
            seg = yg[:, sl]
            r = lax.rsqrt(jnp.mean(seg * seg, axis=-1, keepdims=True) + EPS)
            yh = seg * r
            dn = dyn[:, sl]
            dw_ref[:, sl] += jnp.sum(dn * yh, axis=0, keepdims=True)
            dyh = dn * w_ref[:, sl]
            dyg = r * (dyh - yh * jnp.mean(dyh * yh, axis=-1, keepdims=True))
            dy_ref[:, sl] = (dyg * sz[:, sl]).astype(bf16)
            dz_ref[:, sl] = (dyg * yv[:, sl] * _dsilu(zs[:, sl])).astype(bf16)
        zf = zf_ref[...].astype(f32)
        ov = o_ref[...].astype(f32)
        dog = dg_ref[...].astype(f32)
        dov = (dog * _silu(zf)).astype(bf16)
        do_ref[...] = dov
        dz_ref[:, D_MODEL:] = (dog * ov * _dsilu(zf)).astype(bf16)
        prod = dov.astype(f32) * ov
        lane = lax.broadcasted_iota(jnp.int32, (ts, 128), 1)
        delta = jnp.zeros((ts, 128), f32)
        for h in range(N_HEADS):
            dh = jnp.sum(prod[:, HEAD_DIM * h:HEAD_DIM * (h + 1)], axis=-1, keepdims=True)
            delta = delta + jnp.where(lane == h, dh, 0.0)
        delta_ref[...] = delta.T

    return pl.pallas_call(
        body, name="gate0_bwd", grid=(s // ts,),
        in_specs=[_rowspec(ts, D_MODEL), _rowspec(ts, D_MODEL, 0), _rowspec(ts, D_MODEL, 1), _rowspec(ts, D_MODEL),
                  _rowspec(ts, D_MODEL, 0), _rowspec(ts, D_MODEL, 1), _vecspec(D_MODEL)],
        out_specs=[_rowspec(ts, D_MODEL), _rowspec(ts, D_MODEL), _rowspec(ts, 2 * D_MODEL),
                   pl.BlockSpec((128, ts), lambda i: (0, i)), _vecspec(D_MODEL)],
        out_shape=[jax.ShapeDtypeStruct((s, D_MODEL), bf16), jax.ShapeDtypeStruct((s, D_MODEL), bf16),
                   jax.ShapeDtypeStruct((s, 2 * D_MODEL), bf16), jax.ShapeDtypeStruct((128, s), f32),
                   jax.ShapeDtypeStruct((1, D_MODEL), f32)],
        compiler_params=_cp("arbitrary"))(y, z, z, o, dcat, dcat, ssd_norm)


def _conv_grid(s, c):
    ts, cb = min(CONV_ROW_TILE, s), min(CONV_COL_TILE, c)
    return ts, cb, (c // cb, s // ts)


def _cur(ts, cb, off=0):
    return pl.BlockSpec((ts, cb), lambda c, i: (i, c + off))


def _prev_halo(ts, cb, off=0):
    return pl.BlockSpec((HALO, cb), lambda c, i: (jnp.maximum(i * (ts // HALO) - 1, 0), c + off))


def _next_halo(ts, cb, s, off=0):
    return pl.BlockSpec((HALO, cb), lambda c, i: (jnp.minimum((i + 1) * (ts // HALO), s // HALO - 1), c + off))


def _wspec(k, cb):
    return pl.BlockSpec((k, cb), lambda c, i: (0, c))


def _phases(offsets):
    return sorted({o % 8 for o in offsets} - {0})


def _shift_scratch(offsets, ts, cb):
    return pltpu.VMEM((max(len(_phases(offsets)), 1), ts + HALO - 8, cb), f32)


def _fill_phases(ext_ref, sh_ref, offsets, ts):
    for j, r in enumerate(_phases(offsets)):
        sh_ref[j] = ext_ref[pl.ds(r, ts + HALO - 8), :]


def _slab(ext_ref, sh_ref, offsets, off, start):
    r = off % 8
    a = off - r + start
    if r == 0:
        return ext_ref[a:a + CONV_SUB, :]
    return sh_ref[_phases(offsets).index(r), a:a + CONV_SUB, :]


def _conv_taps(ext_ref, sh_ref, w_ref, b_ref, ts, k_taps, emit):
    offsets = [HALO - (k_taps - 1) + k for k in range(k_taps)]
    _fill_phases(ext_ref, sh_ref, offsets, ts)
    for sb in range(ts // CONV_SUB):
        acc = b_ref[...]
        for k in range(k_taps):
            acc = acc + w_ref[k:k + 1, :] * _slab(ext_ref, sh_ref, offsets, offsets[k], sb * CONV_SUB)
        emit(slice(sb * CONV_SUB, (sb + 1) * CONV_SUB), acc)


def _conv_ssd_fwd(xraw, w, b):
    s, c = xraw.shape
    ts, cb, grid = _conv_grid(s, c)
    offsets = [HALO - (SSD_CONV - 1) + k for k in range(SSD_CONV)]

    def body(x_ref, xh_ref, w_ref, b_ref, pre_ref, act_ref, ext_ref, sh_ref):
        first = pl.program_id(1) == 0
        ext_ref[0:HALO, :] = jnp.where(first, 0.0, xh_ref[...].astype(f32))
        ext_ref[HALO:, :] = x_ref[...].astype(f32)

        def emit(rows, pre):
            pre_ref[rows, :] = pre.astype(bf16)
            act_ref[rows, :] = _silu(pre).astype(bf16)
        _conv_taps(ext_ref, sh_ref, w_ref, b_ref, ts, SSD_CONV, emit)

    return pl.pallas_call(
        body, name="conv_ssd_fwd", grid=grid,
        in_specs=[_cur(ts, cb), _prev_halo(ts, cb), _wspec(SSD_CONV, cb), _wspec(1, cb)],
        out_specs=[_cur(ts, cb), _cur(ts, cb)],
        out_shape=[jax.ShapeDtypeStruct((s, c), bf16)] * 2,
        scratch_shapes=[pltpu.VMEM((HALO + ts, cb), f32), _shift_scratch(offsets, ts, cb)],
        compiler_params=_cp("parallel", "parallel"))(xraw, xraw, w, b)


def _conv_glu_fwd(proj1, w, b):
    s = proj1.shape[0]
    c = D_CONV
    ts, cb, grid = _conv_grid(s, c)
    goff = c // cb

    offsets = [HALO - (CONV_WIDTH - 1) + k for k in range(CONV_WIDTH)]

    def body(v_ref, g_ref, vh_ref, gh_ref, w_ref, b_ref, hc_ref, ext_ref, sh_ref):
        first = pl.program_id(1) == 0
        hh = vh_ref[...].astype(f32) * _sigmoid(gh_ref[...].astype(f32))
        ext_ref[0:HALO, :] = jnp.where(first, 0.0, hh)
        ext_ref[HALO:, :] = v_ref[...].astype(f32) * _sigmoid(g_ref[...].astype(f32))

        def emit(rows, hc):
            hc_ref[rows, :] = hc.astype(bf16)
        _conv_taps(ext_ref, sh_ref, w_ref, b_ref, ts, CONV_WIDTH, emit)

    return pl.pallas_call(
        body, name="conv_glu_fwd", grid=grid,
        in_specs=[_cur(ts, cb), _cur(ts, cb, goff), _prev_halo(ts, cb), _prev_halo(ts, cb, goff),
                  _wspec(CONV_WIDTH, cb), _wspec(1, cb)],
        out_specs=_cur(ts, cb),
        out_shape=jax.ShapeDtypeStruct((s, c), bf16),
        scratch_shapes=[pltpu.VMEM((HALO + ts, cb), f32), _shift_scratch(offsets, ts, cb)],
        compiler_params=_cp("parallel", "parallel"))(proj1, proj1, proj1, proj1, w, b)


def _conv_bwd_offsets(k_taps):
    return [k_taps - 1 - k for k in range(k_taps)], [HALO - (k_taps - 1) + k for k in range(k_taps)]


def _conv_bwd_scratch(k_taps, ts, cb):
    d_offs, x_offs = _conv_bwd_offsets(k_taps)
    return [pltpu.VMEM((ts + HALO, cb), f32), _shift_scratch(d_offs, ts, cb),
            pltpu.VMEM((HALO + ts, cb), f32), _shift_scratch(x_offs, ts, cb),
            pltpu.VMEM((k_taps, 8, cb), f32), pltpu.VMEM((8, cb), f32)]


def _conv_bwd_core(dp, dpn_ref, last, w_ref, scratch, dw_ref, db_ref, ts, k_taps, emit):
    dext_ref, dsh_ref, xext_ref, xsh_ref, dw8_ref, db8_ref = scratch
    d_offs, x_offs = _conv_bwd_offsets(k_taps)
    dext_ref[0:ts, :] = dp
    dext_ref[ts:, :] = jnp.where(last, 0.0, dpn_ref[...].astype(f32))
    _fill_phases(dext_ref, dsh_ref, d_offs, ts)
    _fill_phases(xext_ref, xsh_ref, x_offs, ts)

    @pl.when(pl.program_id(1) == 0)
    def _():
        dw8_ref[...] = jnp.zeros_like(dw8_ref)
        db8_ref[...] = jnp.zeros_like(db8_ref)
    cb = dp.shape[1]
    for sb in range(ts // CONV_SUB):
        start = sb * CONV_SUB
        dpv = dext_ref[start:start + CONV_SUB, :]
        dx = None
        for k in range(k_taps):
            t = w_ref[k:k + 1, :] * _slab(dext_ref, dsh_ref, d_offs, d_offs[k], start)
            dx = t if dx is None else dx + t
            prod = dpv * _slab(xext_ref, xsh_ref, x_offs, x_offs[k], start)
            dw8_ref[k] += jnp.sum(prod.reshape(CONV_SUB // 8, 8, cb), axis=0)
        db8_ref[...] += jnp.sum(dpv.reshape(CONV_SUB // 8, 8, cb), axis=0)
        emit(slice(start, start + CONV_SUB), dx)

    @pl.when(last)
    def _():
        dw_ref[...] = jnp.sum(dw8_ref[...], axis=1)
        db_ref[...] = jnp.sum(db8_ref[...], axis=0, keepdims=True)


def _conv_ssd_bwd(dpre, xraw, w):
    s, c = xraw.shape
    ts, cb, grid = _conv_grid(s, c)
    nb = s // ts

    def body(dp_ref, dpn_ref, x_ref, xh_ref, w_ref, dx_ref, dw_ref, db_ref, *scratch):
        i = pl.program_id(1)
        xext_ref = scratch[2]
        xext_ref[0:HALO, :] = jnp.where(i == 0, 0.0, xh_ref[...].astype(f32))
        xext_ref[HALO:, :] = x_ref[...].astype(f32)

        def emit(rows, dx):
            dx_ref[rows, :] = dx.astype(bf16)
        _conv_bwd_core(dp_ref[...].astype(f32), dpn_ref, i == nb - 1, w_ref, scratch, dw_ref, db_ref, ts, SSD_CONV, emit)

    return pl.pallas_call(
        body, name="conv_ssd_bwd", grid=grid,
        in_specs=[_cur(ts, cb), _next_halo(ts, cb, s), _cur(ts, cb), _prev_halo(ts, cb), _wspec(SSD_CONV, cb)],
        out_specs=[_cur(ts, cb), _wspec(SSD_CONV, cb), _wspec(1, cb)],
        out_shape=[jax.ShapeDtypeStruct((s, c), bf16), jax.ShapeDtypeStruct((SSD_CONV, c), f32),
                   jax.ShapeDtypeStruct((1, c), f32)],
        scratch_shapes=_conv_bwd_scratch(SSD_CONV, ts, cb),
        compiler_params=_cp("parallel", "arbitrary"))(dpre, dpre, xraw, xraw, w)


def _conv_glu_bwd(dhc, proj1, w):
    s = proj1.shape[0]
    c = D_CONV
    ts, cb, grid = _conv_grid(s, c)
    nb = s // ts
    goff = c // cb

    def body(dp_ref, dpn_ref, v_ref, g_ref, vh_ref, gh_ref, w_ref, dv_ref, dg_ref, dw_ref, db_ref, *scratch):
        i = pl.program_id(1)
        xext_ref = scratch[2]
        xext_ref[0:HALO, :] = jnp.where(i == 0, 0.0, vh_ref[...].astype(f32) * _sigmoid(gh_ref[...].astype(f32)))
        xext_ref[HALO:, :] = v_ref[...].astype(f32) * _sigmoid(g_ref[...].astype(f32))

        def emit(rows, dh):
            val = v_ref[rows, :].astype(f32)
            sg = _sigmoid(g_ref[rows, :].astype(f32))
            dv_ref[rows, :] = (dh * sg).astype(bf16)
            dg_ref[rows, :] = (dh * val * sg * (1.0 - sg)).astype(bf16)
        _conv_bwd_core(dp_ref[...].astype(f32), dpn_ref, i == nb - 1, w_ref, scratch, dw_ref, db_ref, ts, CONV_WIDTH, emit)

    return pl.pallas_call(
        body, name="conv_glu_bwd", grid=grid,
        in_specs=[_cur(ts, cb), _next_halo(ts, cb, s), _cur(ts, cb), _cur(ts, cb, goff), _prev_halo(ts, cb),
                  _prev_halo(ts, cb, goff), _wspec(CONV_WIDTH, cb)],
        out_specs=[_cur(ts, cb), _cur(ts, cb), _wspec(CONV_WIDTH, cb), _wspec(1, cb)],
        out_shape=[jax.ShapeDtypeStruct((s, c), bf16), jax.ShapeDtypeStruct((s, c), bf16),
                   jax.ShapeDtypeStruct((CONV_WIDTH, c), f32), jax.ShapeDtypeStruct((1, c), f32)],
        scratch_shapes=_conv_bwd_scratch(CONV_WIDTH, ts, cb),
        compiler_params=_cp("parallel", "arbitrary"))(dhc, dhc, proj1, proj1, proj1, proj1, w)


def _ssd_common(dt_ref, prm_ref):
    l = CHUNK
    dtb = prm_ref[0:1, :]
    a = -jnp.exp(prm_ref[1:2, :])
    dsk = prm_ref[2:3, :]
    zraw = dt_ref[...] + dtb
    dt = _softplus(zraw)
    da = dt * a
    row = lax.broadcasted_iota(jnp.int32, (l, l), 0)
    col = lax.broadcasted_iota(jnp.int32, (l, l), 1)
    causal = row >= col
    cs = _dot(causal.astype(f32), da, precision=HIGHEST)
    return a, dsk, zraw, dt, cs, cs.T, causal, row, col


def _ssd_fwd(act, dtf, prm):
    s = act.shape[0]
    nc = s // CHUNK
    l = CHUNK

    def body(xs_ref, dt_ref, prm_ref, y_ref, hs_ref, st_ref):
        @pl.when(pl.program_id(0) == 0)
        def _():
            st_ref[...] = jnp.zeros_like(st_ref)
        a, dsk, _, dt, cs, cst, causal, _, _ = _ssd_common(dt_ref, prm_ref)
        for g in range(N_GROUPS):
            bm = xs_ref[:, B_OFF + D_STATE * g:B_OFF + D_STATE * (g + 1)]
            cm = xs_ref[:, C_OFF + D_STATE * g:C_OFF + D_STATE * (g + 1)]
            gmat = _dot(cm, bm, NT)
            for r in range(HEADS_PER_GROUP):
                h = HEADS_PER_GROUP * g + r
                hsl = slice(HEAD_DIM * h, HEAD_DIM * (h + 1))
                xv = xs_ref[:, hsl].astype(f32)
                csc = cs[:, h:h + 1]
                csr = cst[h:h + 1, :]
                cl = cs[l - 1:l, h:h + 1]
                dk = jnp.exp(jnp.where(causal, csc - csr, NEG))
                xd = xv * dt[:, h:h + 1]
                hp = st_ref[h]
                hs_ref[0, h] = hp
                ydiag = _dot((gmat * dk).astype(bf16), xd.astype(bf16))
                yoff = _dot(cm, hp.astype(bf16), NT) * jnp.exp(csc)
                y_ref[:, hsl] = (ydiag + yoff + xv * dsk[:, h:h + 1]).astype(bf16)
                st = _dot((xd * jnp.exp(cl - csc)).astype(bf16), bm, TN)
                st_ref[h] = hp * jnp.exp(cl) + st

    return pl.pallas_call(
        body, name="ssd_fwd", grid=(nc,),
        in_specs=[pl.BlockSpec((l, XBC_W), lambda i: (i, 0)), pl.BlockSpec((l, 128), lambda i: (i, 0)),
                  pl.BlockSpec((8, 128), lambda i: (0, 0))],
        out_specs=[pl.BlockSpec((l, D_MODEL), lambda i: (i, 0)),
                   pl.BlockSpec((1, N_HEADS, HEAD_DIM, D_STATE), lambda i: (i, 0, 0, 0))],
        out_shape=[jax.ShapeDtypeStruct((s, D_MODEL), bf16),
                   jax.ShapeDtypeStruct((nc, N_HEADS, HEAD_DIM, D_STATE), f32)],
        scratch_shapes=[pltpu.VMEM((N_HEADS, HEAD_DIM, D_STATE), f32)],
        compiler_params=_cp("arbitrary"))(act, dtf, prm)


def _ssd_bwd(act, pre, dtf, prm, hs, dy):
    s = act.shape[0]
    nc = s // CHUNK
    l = CHUNK

    def body(xs_ref, pre_ref, dt_ref, prm_ref, hs_ref, dy_ref, dpre_ref, ddt_ref, dprm_ref, dh_ref):
        @pl.when(pl.program_id(0) == 0)
        def _():
            dh_ref[...] = jnp.zeros_like(dh_ref)
            dprm_ref[...] = jnp.zeros_like(dprm_ref)
        a, dsk, zraw, dt, cs, cst, causal, row, col = _ssd_common(dt_ref, prm_ref)
        lane = lax.broadcasted_iota(jnp.int32, (l, 128), 1)
        rowl = lax.broadcasted_iota(jnp.int32, (l, 128), 0)
        sub = lax.broadcasted_iota(jnp.int32, (128, l), 0)
        lane1 = lax.broadcasted_iota(jnp.int32, (1, 128), 1)
        dcs_c = jnp.zeros((l, 128), f32)
        dcs_r = jnp.zeros((128, l), f32)
        ddt_c = jnp.zeros((l, 128), f32)
        dd_row = jnp.zeros((1, 128), f32)
        for g in range(N_GROUPS):
            bsl = slice(B_OFF + D_STATE * g, B_OFF + D_STATE * (g + 1))
            csl = slice(C_OFF + D_STATE * g, C_OFF + D_STATE * (g + 1))
            bm = xs_ref[:, bsl]
            cm = xs_ref[:, csl]
            gmat = _dot(cm, bm, NT)
            dgm = jnp.zeros((l, l), f32)
            dbg = jnp.zeros((l, D_STATE), f32)
            dcg = jnp.zeros((l, D_STATE), f32)
            for r in range(HEADS_PER_GROUP):
                h = HEADS_PER_GROUP * g + r
                hsl = slice(HEAD_DIM * h, HEAD_DIM * (h + 1))
                xv = xs_ref[:, hsl].astype(f32)
                dyv = dy_ref[:, hsl].astype(f32)
                dyb = dyv.astype(bf16)
                csc = cs[:, h:h + 1]
                csr = cst[h:h + 1, :]
                cl = cs[l - 1:l, h:h + 1]
                dk = jnp.exp(jnp.where(causal, csc - csr, NEG))
                mf = gmat * dk
                dtc = dt[:, h:h + 1]
                xd = xv * dtc
                xdb = xd.astype(bf16)
                ecs = jnp.exp(csc)
                dec = jnp.exp(cl)
                e = jnp.exp(cl - csc)
                hp = hs_ref[0, h]
                hpb = hp.astype(bf16)
                dhn = dh_ref[h]
                dhnb = dhn.astype(bf16)
                dd_h = jnp.sum(jnp.sum(dyv * xv, axis=1, keepdims=True), axis=0, keepdims=True)
                dx = dyv * dsk[:, h:h + 1]
                ch = _dot(cm, hpb, NT)
                dye = dyv * ecs
                dyeb = dye.astype(bf16)
                dcg = dcg + _dot(dyeb, hpb)
                dhp = _dot(dyeb, cm, TN)
                dcs_col = jnp.sum(dye * ch, axis=1, keepdims=True)
                dm = _dot(dyb, xdb, NT)
                dxd = _dot(mf.astype(bf16), dyb, TN)
                dgm = dgm + dm * dk
                wmat = dm * mf
                dcs_col = dcs_col + jnp.sum(wmat, axis=1, keepdims=True)
                dcs_row = -jnp.sum(wmat, axis=0, keepdims=True)
                ddec = jnp.sum(jnp.sum(hp * dhn, axis=1, keepdims=True), axis=0, keepdims=True)
                dxe = _dot(bm, dhnb, NT)
                dxd = dxd + dxe * e
                de_e = jnp.sum(dxe * xd, axis=1, keepdims=True) * e
                dbg = dbg + _dot((xd * e).astype(bf16), dhnb)
                dcs_col = dcs_col - de_e
                dlast = ddec * dec + jnp.sum(de_e, axis=0, keepdims=True)
                dh_ref[h] = dhp + dec * dhn
                dx = dx + dxd * dtc
                ddt_h = jnp.sum(dxd * xv, axis=1, keepdims=True)
                is_h = lane == h
                dcs_c = dcs_c + jnp.where(is_h, dcs_col, 0.0) + jnp.where(is_h & (rowl == l - 1), dlast, 0.0)
                dcs_r = dcs_r + jnp.where(sub == h, dcs_row, 0.0)
                ddt_c = ddt_c + jnp.where(is_h, ddt_h, 0.0)
                dd_row = dd_row + jnp.where(lane1 == h, dd_h, 0.0)
                dpre_ref[:, hsl] = (dx * _dsilu(pre_ref[:, hsl].astype(f32))).astype(bf16)
            dgb = dgm.astype(bf16)
            dcg = dcg + _dot(dgb, bm)
            dbg = dbg + _dot(dgb, cm, TN)
            dpre_ref[:, bsl] = (dbg * _dsilu(pre_ref[:, bsl].astype(f32))).astype(bf16)
            dpre_ref[:, csl] = (dcg * _dsilu(pre_ref[:, csl].astype(f32))).astype(bf16)
        dcs = dcs_c + dcs_r.T
        dda = _dot((row <= col).astype(f32), dcs, precision=HIGHEST)
        ddt = ddt_c + dda * a
        ddtraw = jnp.where(lane < N_HEADS, ddt * _sigmoid(zraw), 0.0)
        ddt_ref[...] = ddtraw
        dprm_ref[0:1, :] += jnp.sum(ddtraw, axis=0, keepdims=True)
        dprm_ref[1:2, :] += jnp.where(lane1 < N_HEADS, jnp.sum(dda * dt, axis=0, keepdims=True) * a, 0.0)
        dprm_ref[2:3, :] += dd_row

    def rev(i):
        return (nc - 1 - i, 0)

    return pl.pallas_call(
        body, name="ssd_bwd", grid=(nc,),
        in_specs=[pl.BlockSpec((l, XBC_W), rev), pl.BlockSpec((l, XBC_W), rev),
                  pl.BlockSpec((l, 128), rev), pl.BlockSpec((8, 128), lambda i: (0, 0)),
                  pl.BlockSpec((1, N_HEADS, HEAD_DIM, D_STATE), lambda i: (nc - 1 - i, 0, 0, 0)),
                  pl.BlockSpec((l, D_MODEL), rev)],
        out_specs=[pl.BlockSpec((l, XBC_W), rev), pl.BlockSpec((l, 128), rev), pl.BlockSpec((8, 128), lambda i: (0, 0))],
        out_shape=[jax.ShapeDtypeStruct((s, XBC_W), bf16), jax.ShapeDtypeStruct((s, 128), f32),
                   jax.ShapeDtypeStruct((8, 128), f32)],
        scratch_shapes=[pltpu.VMEM((N_HEADS, HEAD_DIM, D_STATE), f32)],
        compiler_params=_cp("arbitrary"))(act, pre, dtf, prm, hs, dy)


def _fox_cumsum(dtf, prm):
    s = dtf.shape[0]
    l = CHUNK

    def body(f_ref, prm_ref, c_ref, carry_ref):
        @pl.when(pl.program_id(0) == 0)
        def _():
            carry_ref[...] = jnp.zeros_like(carry_ref)
        lf = _log_sigmoid(f_ref[...] + prm_ref[3:4, :])
        row = lax.broadcasted_iota(jnp.int32, (l, l), 0)
        col = lax.broadcasted_iota(jnp.int32, (l, l), 1)
        c = _dot((row >= col).astype(f32), lf, precision=HIGHEST) + carry_ref[...]
        c_ref[...] = c
        carry_ref[...] = c[l - 1:l, :]

    return pl.pallas_call(
        body, name="fox_cumsum", grid=(s // l,),
        in_specs=[pl.BlockSpec((l, 128), lambda i: (i, 0)), pl.BlockSpec((8, 128), lambda i: (0, 0))],
        out_specs=pl.BlockSpec((l, 128), lambda i: (i, 0)),
        out_shape=jax.ShapeDtypeStruct((s, 128), f32),
        scratch_shapes=[pltpu.VMEM((1, 128), f32)],
        compiler_params=_cp("arbitrary"))(dtf, prm)


def _position():
    return lax.axis_index("x"), lax.axis_index("y"), lax.axis_index("c")


def _exchange_sems(n):
    return [pltpu.SemaphoreType.DMA((n, N_DEV - 1)), pltpu.SemaphoreType.DMA((n, N_DEV - 1)),
            pltpu.SemaphoreType.DMA((n,))]


def _exchange_copies(g_refs, r_refs, send_sems, recv_sems, local_sems, gather=False):
    n = len(g_refs)
    x, y, cc = _position()
    me = 4 * x + 2 * y + cc

    def src(a, j):
        return g_refs[a] if gather else g_refs[a].at[j]

    local = [pltpu.make_async_copy(src(a, me), r_refs[a].at[me], local_sems.at[a]) for a in range(n)]
    sends, recvs = [], []
    for k in range(1, N_DEV):
        px = 1 - x if k & 4 else x
        py = 1 - y if k & 2 else y
        pc = 1 - cc if k & 1 else cc
        pid = 4 * px + 2 * py + pc
        for a in range(n):
            sends.append(pltpu.make_async_remote_copy(
                src_ref=src(a, pid), dst_ref=r_refs[a].at[me],
                send_sem=send_sems.at[a, k - 1], recv_sem=recv_sems.at[a, k - 1],
                device_id=(px, py, pc), device_id_type=pl.DeviceIdType.MESH))
            recvs.append(pltpu.make_async_remote_copy(
                src_ref=src(a, pid), dst_ref=r_refs[a].at[pid],
                send_sem=send_sems.at[a, k - 1], recv_sem=recv_sems.at[a, k - 1],
                device_id=(px, py, pc), device_id_type=pl.DeviceIdType.MESH))
    return local, sends, recvs


def _exchange_start(copies):
    local, sends, _ = copies
    for cp in local + sends:
        cp.start()


def _exchange_wait(copies):
    local, sends, recvs = copies
    for cp in recvs:
        cp.wait_recv()
    for cp in sends:
        cp.wait_send()
    for cp in local:
        cp.wait()


AUG = HEAD_DIM
N_PAIRS = N_HEADS // 2
V_BLOCK = 2 * D_MODEL // 128


def _split3(x):
    hi = x.astype(bf16)
    r1 = x - hi.astype(f32)
    mid = r1.astype(bf16)
    lo = (r1 - mid.astype(f32)).astype(bf16)
    return hi.astype(f32), mid.astype(f32), lo.astype(f32)


def _fox_prep(qkv, c):
    s = qkv.shape[0]
    ts = min(CONV_ROW_TILE, s)
    kb = D_MODEL // 128

    def body(q_ref, k_ref, c_ref, qa_ref, ka_ref):
        lane = lax.broadcasted_iota(jnp.int32, (ts, 128), 1)
        low = lane < HEAD_DIM
        for h in range(N_HEADS):
            psl = slice(128 * (h // 2), 128 * (h // 2 + 1))
            qv = q_ref[:, psl].astype(f32) * (HEAD_DIM ** -0.5)
            kv = k_ref[:, psl].astype(f32)
            if h % 2:
                qv = pltpu.roll(qv, HEAD_DIM, 1)
                kv = pltpu.roll(kv, HEAD_DIM, 1)
            hi, mid, lo = _split3(c_ref[:, F_LANE + h:F_LANE + h + 1])
            ones = jnp.where((lane >= AUG + 3) & (lane < AUG + 6), 1.0, 0.0)
            cq = jnp.where(lane == AUG, hi, jnp.where(lane == AUG + 1, mid, jnp.where(lane == AUG + 2, lo, ones)))
            qa_ref[h] = jnp.where(low, qv, cq).astype(bf16)
            onek = jnp.where((lane >= AUG) & (lane < AUG + 3), 1.0, 0.0)
            ck = jnp.where(lane == AUG + 3, -hi, jnp.where(lane == AUG + 4, -mid, jnp.where(lane == AUG + 5, -lo, onek)))
            ka_ref[h] = jnp.where(low, kv, ck).astype(bf16)

    hm = pl.BlockSpec((N_HEADS, ts, 128), lambda i: (0, i, 0))
    return pl.pallas_call(
        body, name="fox_prep", grid=(s // ts,),
        in_specs=[_rowspec(ts, D_MODEL, 0), _rowspec(ts, D_MODEL, 1), _rowspec(ts, 128)],
        out_specs=[hm, hm], out_shape=[jax.ShapeDtypeStruct((N_HEADS, s, 128), bf16)] * 2,
        compiler_params=_cp("parallel"))(qkv, qkv, c)


def _fox_fwd(qa, ka, qkv, ws):
    s = qkv.shape[0]
    t = min(ATTN_FWD_TILE, s)
    nq = s // t
    n = len(ws)

    def body(qa_ref, ka_ref, v_ref, *rest):
        w_refs, (o_ref, lse_ref), wg_refs, sems = rest[:n], rest[n:n + 2], rest[n + 2:2 * n + 2], rest[2 * n + 2:]
        qi = pl.program_id(1)
        copies = _exchange_copies(w_refs, wg_refs, *sems, gather=True)

        @pl.when((pl.program_id(0) == 0) & (qi == 0))
        def _():
            _exchange_start(copies)
        low = lax.broadcasted_iota(jnp.int32, (t, 128), 1) < HEAD_DIM
        row = lax.broadcasted_iota(jnp.int32, (t, t), 0)
        col = lax.broadcasted_iota(jnp.int32, (t, t), 1)

        def tile(ki, carry, diagonal):
            stats, acc = carry
            koff = pl.multiple_of(ki * t, t)
            v = v_ref[pl.ds(koff, t), :]
            vh = (jnp.where(low, v, jnp.zeros_like(v)), jnp.where(low, jnp.zeros_like(v), v))
            new_stats, alphas, pv = [], [], None
            for r in range(2):
                m_old, l_old = stats[r]
                sc = _dot(qa_ref[r], ka_ref[r, pl.ds(koff, t), :], NT)
                if diagonal:
                    sc = jnp.where(col <= row, sc, NEG)
                m_new = jnp.maximum(m_old, jnp.max(sc, axis=1, keepdims=True))
                p = jnp.exp(sc - m_new)
                alpha = jnp.exp(m_old - m_new)
                new_stats.append((m_new, alpha * l_old + jnp.sum(p, axis=1, keepdims=True)))
                alphas.append(alpha)
                d = _dot(p.astype(bf16), vh[r])
                pv = d if pv is None else pv + d
            acc = acc * jnp.where(low, alphas[0], alphas[1]) + pv
            return tuple(new_stats), acc

        init = (((jnp.full((t, 1), NEG, f32), jnp.zeros((t, 1), f32)),) * 2, jnp.zeros((t, 128), f32))
        carry = lax.fori_loop(0, qi, lambda ki, cr: tile(ki, cr, False), init)
        stats, acc = tile(qi, carry, True)
        o_ref[...] = (acc / jnp.where(low, stats[0][1], stats[1][1])).astype(bf16)
        for r in range(2):
            lse = stats[r][0] + jnp.log(stats[r][1])
            lse_ref[r] = jnp.broadcast_to(lse, (t, 128)).T[0:1, :]

        @pl.when((pl.program_id(0) == N_PAIRS - 1) & (qi == nq - 1))
        def _():
            _exchange_wait(copies)

    anyspec = pl.BlockSpec(memory_space=pl.ANY)
    outs = pl.pallas_call(
        body, name="fox_fwd", grid=(N_PAIRS, nq),
        in_specs=[pl.BlockSpec((2, t, 128), lambda j, qi: (j, qi, 0)),
                  pl.BlockSpec((2, s, 128), lambda j, qi: (j, 0, 0)),
                  pl.BlockSpec((s, 128), lambda j, qi: (0, V_BLOCK + j))] + [anyspec] * n,
        out_specs=[pl.BlockSpec((t, 128), lambda j, qi: (qi, j)), pl.BlockSpec((2, 1, t), lambda j, qi: (j, 0, qi))]
        + [anyspec] * n,
        out_shape=[jax.ShapeDtypeStruct((s, D_MODEL), bf16), jax.ShapeDtypeStruct((N_HEADS, 1, s), f32)]
        + [jax.ShapeDtypeStruct((N_DEV,) + w.shape, w.dtype) for w in ws],
        scratch_shapes=_exchange_sems(n),
        compiler_params=_cp("arbitrary", "arbitrary"))(qa, ka, qkv, *ws)
    return outs[0], outs[1], outs[2:]


def _fox_bwd(qa, ka, qkv, do, lse, delta, gs):
    s = qkv.shape[0]
    t = min(ATTN_TILE, s)
    nq = s // t
    n = len(gs)

    def body(qa_ref, ka_ref, v_ref, do_ref, lse_ref, dl_ref, *rest):
        g_refs, (dq_ref, dk_ref, dv_ref), r_refs, sems = rest[:n], rest[n:n + 3], rest[n + 3:2 * n + 3], rest[2 * n + 3:]
        ki = pl.program_id(1)
        copies = _exchange_copies(g_refs, r_refs, *sems)

        @pl.when((pl.program_id(0) == 0) & (ki == 0))
        def _():
            _exchange_start(copies)

        @pl.when(ki == 0)
        def _():
            dq_ref[...] = jnp.zeros_like(dq_ref)
        low = lax.broadcasted_iota(jnp.int32, (t, 128), 1) < HEAD_DIM
        row = lax.broadcasted_iota(jnp.int32, (t, t), 0)
        col = lax.broadcasted_iota(jnp.int32, (t, t), 1)
        v = v_ref[...]
        zero = jnp.zeros_like(v)
        vh = (jnp.where(low, v, zero), jnp.where(low, zero, v))

        def tile(qi, carry, diagonal):
            dks, dv = carry
            qoff = pl.multiple_of(qi * t, t)
            dov = do_ref[pl.ds(qoff, t), :]
            doh = (jnp.where(low, dov, zero), jnp.where(low, zero, dov))
            new_dks = []
            for r in range(2):
                qt = qa_ref[r, pl.ds(qoff, t), :]
                sct = _dot(ka_ref[r], qt, NT)
                if diagonal:
                    sct = jnp.where(row <= col, sct, NEG)
                pt = jnp.exp(sct - lse_ref[r, :, pl.ds(qoff, t)])
                dpt = _dot(vh[r], dov, NT)
                dst = (pt * (dpt - dl_ref[r, :, pl.ds(qoff, t)])).astype(bf16)
                dv = dv + _dot(pt.astype(bf16), doh[r])
                new_dks.append(dks[r] + _dot(dst, qt))
                dq_ref[r, pl.ds(qoff, t), :] += _dot(dst, ka_ref[r], TN)
            return tuple(new_dks), dv

        zacc = jnp.zeros((t, 128), f32)
        carry = tile(ki, ((zacc, zacc), zacc), True)
        dks, dv = lax.fori_loop(ki + 1, nq, lambda qi, cr: tile(qi, cr, False), carry)
        dk_ref[0] = dks[0]
        dk_ref[1] = dks[1]
        dv_ref[...] = dv.astype(bf16)

        @pl.when((pl.program_id(0) == N_PAIRS - 1) & (ki == nq - 1))
        def _():
            _exchange_wait(copies)

    anyspec = pl.BlockSpec(memory_space=pl.ANY)
    outs = pl.pallas_call(
        body, name="fox_bwd", grid=(N_PAIRS, nq),
        in_specs=[pl.BlockSpec((2, s, 128), lambda j, ki: (j, 0, 0)),
                  pl.BlockSpec((2, t, 128), lambda j, ki: (j, ki, 0)),
                  pl.BlockSpec((t, 128), lambda j, ki: (ki, V_BLOCK + j)),
                  pl.BlockSpec((s, 128), lambda j, ki: (0, j)),
                  pl.BlockSpec((2, 1, s), lambda j, ki: (j, 0, 0)),
                  pl.BlockSpec((2, 1, s), lambda j, ki: (j, 0, 0))] + [anyspec] * n,
        out_specs=[pl.BlockSpec((2, s, 128), lambda j, ki: (j, 0, 0)),
                   pl.BlockSpec((2, t, 128), lambda j, ki: (j, ki, 0)),
                   pl.BlockSpec((t, 128), lambda j, ki: (ki, j))] + [anyspec] * n,
        out_shape=[jax.ShapeDtypeStruct((N_HEADS, s, 128), f32), jax.ShapeDtypeStruct((N_HEADS, s, 128), f32),
                   jax.ShapeDtypeStruct((s, D_MODEL), bf16)] + [jax.ShapeDtypeStruct(g.shape, g.dtype) for g in gs],
        scratch_shapes=_exchange_sems(n),
        compiler_params=_cp("arbitrary", "arbitrary"))(qa, ka, qkv, do, lse, delta, *gs)
    return outs[0], outs[1], outs[2], outs[3:]


def _fox_bwd_post(dq_hm, dk_hm):
    s = dq_hm.shape[1]
    ts = min(CONV_ROW_TILE, s)

    def body(dq_ref, dk_ref, q_ref, k_ref, dc_ref):
        lane = lax.broadcasted_iota(jnp.int32, (ts, 128), 1)
        dc = jnp.zeros((ts, 128), f32)
        for h in range(N_HEADS):
            hsl = slice(HEAD_DIM * h, HEAD_DIM * (h + 1))
            dqv = dq_ref[h]
            dkv = dk_ref[h]
            q_ref[:, hsl] = (dqv[:, 0:HEAD_DIM] * (HEAD_DIM ** -0.5)).astype(bf16)
            k_ref[:, hsl] = dkv[:, 0:HEAD_DIM].astype(bf16)
            dc = dc + jnp.where(lane == F_LANE + h, dqv[:, AUG:AUG + 1] - dkv[:, AUG + 3:AUG + 4], 0.0)
        dc_ref[...] = dc

    hm = pl.BlockSpec((N_HEADS, ts, 128), lambda i: (0, i, 0))
    return pl.pallas_call(
        body, name="fox_bwd_post", grid=(s // ts,), in_specs=[hm, hm],
        out_specs=[_rowspec(ts, D_MODEL), _rowspec(ts, D_MODEL), _rowspec(ts, 128)],
        out_shape=[jax.ShapeDtypeStruct((s, D_MODEL), bf16), jax.ShapeDtypeStruct((s, D_MODEL), bf16),
                   jax.ShapeDtypeStruct((s, 128), f32)],
        compiler_params=_cp("parallel"))(dq_hm, dk_hm)


def _fox_gate_bwd(dc, dtf, prm, ddt_raw):
    s = dtf.shape[0]
    l = CHUNK
    nb = s // l

    def body(dc_ref, f_ref, prm_ref, ddt_ref, out_ref, dfb_ref, carry_ref):
        @pl.when(pl.program_id(0) == 0)
        def _():
            carry_ref[...] = jnp.zeros_like(carry_ref)
            dfb_ref[...] = jnp.zeros_like(dfb_ref)
        dc = dc_ref[...]
        row = lax.broadcasted_iota(jnp.int32, (l, l), 0)
        col = lax.broadcasted_iota(jnp.int32, (l, l), 1)
        dlf = _dot((row <= col).astype(f32), dc, precision=HIGHEST) + carry_ref[...]
        carry_ref[...] = dlf[0:1, :]
        lane = lax.broadcasted_iota(jnp.int32, (l, 128), 1)
        is_f = (lane >= F_LANE) & (lane < F_LANE + N_HEADS)
        dfr = jnp.where(is_f, dlf * _sigmoid(-(f_ref[...] + prm_ref[3:4, :])), 0.0)
        dfb_ref[...] += jnp.sum(dfr, axis=0, keepdims=True)
        out_ref[...] = ddt_ref[...] + dfr

    def rev(i):
        return (nb - 1 - i, 0)

    return pl.pallas_call(
        body, name="fox_gate_bwd", grid=(nb,),
        in_specs=[pl.BlockSpec((l, 128), rev), pl.BlockSpec((l, 128), rev), pl.BlockSpec((8, 128), lambda i: (0, 0)),
                  pl.BlockSpec((l, 128), rev)],
        out_specs=[pl.BlockSpec((l, 128), rev), pl.BlockSpec((1, 128), lambda i: (0, 0))],
        out_shape=[jax.ShapeDtypeStruct((s, 128), f32), jax.ShapeDtypeStruct((1, 128), f32)],
        scratch_shapes=[pltpu.VMEM((1, 128), f32)],
        compiler_params=_cp("arbitrary"))(dc, dtf, prm, ddt_raw)


def _all_gather(xl, name):
    r, c = xl.shape

    def body(x_ref, out_ref, send_sems, recv_sems, local_sem):
        x, y, cc = _position()
        me, sibling = (x, y, cc), (x, y, 1 - cc)
        chips = [(1 - x, y), (x, 1 - y), (1 - x, 1 - y)]

        def slot(px, py, pc):
            return out_ref.at[4 * px + 2 * py + pc]

        def copy(k, block, to, src=None):
            return pltpu.make_async_remote_copy(
                src_ref=slot(*block) if src is None else src, dst_ref=slot(*block),
                send_sem=send_sems.at[k], recv_sem=recv_sems.at[k],
                device_id=to, device_id_type=pl.DeviceIdType.MESH)

        mine = pltpu.make_async_copy(x_ref, slot(*me), local_sem)
        mine.start()
        first = [copy(0, me, sibling, src=x_ref)]
        first += [copy(1 + j, me, (*chip, cc), src=x_ref) for j, chip in enumerate(chips)]
        for cp in first:
            cp.start()
        passed = [copy(4 + j, (*chip, cc), sibling) for j, chip in enumerate(chips)]
        for j, chip in enumerate(chips):
            copy(1 + j, (*chip, cc), me).wait_recv()
            passed[j].start()
        copy(0, sibling, me).wait_recv()
        for j, chip in enumerate(chips):
            copy(4 + j, (*chip, 1 - cc), me).wait_recv()
        for cp in first + passed:
            cp.wait_send()
        mine.wait()

    return pl.pallas_call(
        body, name=name,
        out_shape=jax.ShapeDtypeStruct((N_DEV, r, c), xl.dtype),
        in_specs=[pl.BlockSpec(memory_space=pl.ANY)], out_specs=pl.BlockSpec(memory_space=pl.ANY),
        scratch_shapes=[pltpu.SemaphoreType.DMA((7,)), pltpu.SemaphoreType.DMA((7,)), pltpu.SemaphoreType.DMA],
    )(xl)


def _sum_parts(parts, name):
    n, r, c = parts.shape

    def body(p_ref, o_ref):
        g = p_ref[0]
        for i in range(1, n):
            g = g + p_ref[i]
        o_ref[...] = g

    return pl.pallas_call(body, name=name, out_shape=jax.ShapeDtypeStruct((r, c), f32))(parts)


def _adamw(w, m, v, parts, name, tr=128):
    r, c = w.shape
    n = parts.shape[0]
    tr = min(tr, r)
    c1 = 1.0 - ADAM_B1 ** ADAM_STEP
    c2 = 1.0 - ADAM_B2 ** ADAM_STEP

    def body(w_ref, m_ref, v_ref, p_ref, g_ref, d_ref, nm_ref, nv_ref):
        g = p_ref[0].astype(f32)
        for i in range(1, n):
            g = g + p_ref[i].astype(f32)
        g_ref[...] = g
        nm = ADAM_B1 * m_ref[...] + (1.0 - ADAM_B1) * g
        nv = ADAM_B2 * v_ref[...] + (1.0 - ADAM_B2) * (g * g)
        nm_ref[...] = nm
        nv_ref[...] = nv
        d_ref[...] = -ADAM_LR * ((nm / c1) / (jnp.sqrt(nv / c2) + ADAM_EPS) + ADAM_WD * w_ref[...])

    blk = pl.BlockSpec((tr, c), lambda i: (i, 0))
    return pl.pallas_call(
        body, name=name, grid=(r // tr,),
        in_specs=[blk, blk, blk, pl.BlockSpec((n, tr, c), lambda i: (0, i, 0))],
        out_specs=[blk] * 4, out_shape=[jax.ShapeDtypeStruct((r, c), f32)] * 4,
        compiler_params=_cp("parallel"))(w, m, v, parts)


def _lanes(w):
    return -(-w // 128) * 128


def _pack(arrs):
    rows = []
    for a in arrs:
        k, w = a.shape
        if w % 128:
            a = jnp.pad(a, ((0, 0), (0, _lanes(w) - w)))
        rows.append(a.reshape(-1, 128))
    out = jnp.concatenate(rows, axis=0)
    pad = -out.shape[0] % 8
    return jnp.pad(out, ((0, pad), (0, 0))) if pad else out


def _unpack(packed, shapes):
    outs, off = [], 0
    lead = packed.shape[:-2]
    for k, w in shapes:
        nrow = k * _lanes(w) // 128
        a = packed[..., off:off + nrow, :].reshape(*lead, k, _lanes(w))[..., :w]
        outs.append(a)
        off += nrow
    return outs


def _gathered_cols(a):
    n, k, wl = a.shape
    return jnp.transpose(a, (1, 0, 2)).reshape(k, n * wl)


def _col_shards(a):
    k, w = a.shape
    return jnp.transpose(a.reshape(k, N_DEV, w // N_DEV), (1, 0, 2))


SMALL_PARAMS = (
    ("e_norm_pre", 1, 1024, False), ("e_conv_w", 4, 2048, True), ("e_conv_b", 1, 2048, False),
    ("e_dt_bias", 1, 16, False), ("e_a_log", 1, 16, False), ("e_d_skip", 1, 16, False), ("e_fgate_b", 1, 16, False),
    ("e_ssd_norm", 1, 1024, False), ("e_norm_post", 1, 1024, False), ("o_norm_pre", 1, 1024, True),
    ("o_conv_w", 31, 2048, True), ("o_conv_b", 1, 2048, True), ("o_ln_g", 1, 2048, True), ("o_ln_b", 1, 2048, True),
    ("o_norm_post", 1, 1024, True),
)
BIG_PARAMS = ("e_w_in", "e_w_out", "o_w_in", "o_w_out")
WEIGHT_ORDER = ("e_norm_pre", "e_w_in", "e_conv_w", "e_conv_b", "e_dt_bias", "e_a_log", "e_d_skip", "e_fgate_b",
                "e_ssd_norm", "e_w_out", "e_norm_post", "o_norm_pre", "o_w_in", "o_conv_w", "o_conv_b", "o_ln_g",
                "o_ln_b", "o_w_out", "o_norm_post")
E_IN = 7200
O_IN = 6144


def kernel(x, e_norm_pre, e_w_in, e_conv_w, e_conv_b, e_dt_bias, e_a_log, e_d_skip, e_fgate_b, e_ssd_norm, e_w_out, e_norm_post, o_norm_pre, o_w_in, o_conv_w, o_conv_b, o_ln_g, o_ln_b, o_w_out, o_norm_post, loss_target, m_e_norm_pre, m_e_w_in, m_e_conv_w, m_e_conv_b, m_e_dt_bias, m_e_a_log, m_e_d_skip, m_e_fgate_b, m_e_ssd_norm, m_e_w_out, m_e_norm_post, m_o_norm_pre, m_o_w_in, m_o_conv_w, m_o_conv_b, m_o_ln_g, m_o_ln_b, m_o_w_out, m_o_norm_post, v_e_norm_pre, v_e_w_in, v_e_conv_w, v_e_conv_b, v_e_dt_bias, v_e_a_log, v_e_d_skip, v_e_fgate_b, v_e_ssd_norm, v_e_w_out, v_e_norm_post, v_o_norm_pre, v_o_w_in, v_o_conv_w, v_o_conv_b, v_o_ln_g, v_o_ln_b, v_o_w_out, v_o_norm_post):
    given = dict(locals())
    w_in = {n: given[n] for n in WEIGHT_ORDER}
    m_in = {n: given["m_" + n] for n in WEIGHT_ORDER}
    v_in = {n: given["v_" + n] for n in WEIGHT_ORDER}

    def mat(a):
        return a.reshape(a.shape[-2:])

    xs = mat(x)
    tgt = mat(loss_target)
    xi, yi, ci = _position()
    me = 4 * xi + 2 * yi + ci
    ow = O_IN // N_DEV
    wr = D_CONV // N_DEV

    e_w_in_f = _gathered_cols(_all_gather(mat(e_w_in).astype(bf16), "gather_weights"))
    later_weights = [mat(e_w_out).astype(bf16), mat(o_w_in).astype(bf16), mat(o_w_out).astype(bf16)]
    w_z, w_xbc = e_w_in_f[:, 0:2048], e_w_in_f[:, 2048:4096]
    w_qkv = e_w_in_f[:, 4112:7184]
    w_dtf = jnp.concatenate([e_w_in_f[:, 4096:4112], e_w_in_f[:, 7184:7200], jnp.zeros((D_MODEL, 96), bf16)], axis=1)

    sharded_small = [(n, k, w) for n, k, w, sh in SMALL_PARAMS if sh]
    sg = _all_gather(_pack([mat(w_in[n]) for n, _, _ in sharded_small]), "gather_small_weights")
    full_small = {n: _gathered_cols(a)
                  for (n, _, _), a in zip(sharded_small, _unpack(sg, [(k, w // N_DEV) for _, k, w in sharded_small]))}
    for n, _, _, sh in SMALL_PARAMS:
        if not sh:
            full_small[n] = mat(w_in[n])
    p = full_small
    prm = jnp.zeros((8, 128), f32)
    prm = prm.at[0, 0:16].set(p["e_dt_bias"][0]).at[1, 0:16].set(p["e_a_log"][0]).at[2, 0:16].set(p["e_d_skip"][0])
    prm = prm.at[3, F_LANE:F_LANE + 16].set(p["e_fgate_b"][0])

    u0 = _rms_fwd(xs, p["e_norm_pre"], "rms_pre0")
    z0 = _mm_nn(u0, w_z, bf16, "proj0_z")
    xraw = _mm_nn(u0, w_xbc, bf16, "proj0_xbc")
    qkv = _mm_nn(u0, w_qkv, bf16, "proj0_qkv")
    dtf = _mm_nn(u0, w_dtf, f32, "proj0_dtf")
    pre, act = _conv_ssd_fwd(xraw, p["e_conv_w"], p["e_conv_b"])
    y, hs = _ssd_fwd(act, dtf, prm)
    qa, ka = _fox_prep(qkv, _fox_cumsum(dtf, prm))
    o, lse, (e_w_out_g, o_w_in_g, o_w_out_g) = _fox_fwd(qa, ka, qkv, later_weights)
    e_w_out_f = e_w_out_g.reshape(D_CONV, D_MODEL)
    o_w_in_f = _gathered_cols(o_w_in_g)
    o_w_out_f = o_w_out_g.reshape(D_CONV, D_MODEL)
    cat = _gate0_fwd(y, z0, o, p["e_ssd_norm"])
    out0 = _mm_nn(cat, e_w_out_f, f32, "out0")
    x1, u1 = _post0_pre1(xs, out0, p["e_norm_post"], p["o_norm_pre"])

    proj1 = _mm_nn(u1, o_w_in_f, bf16, "proj1")
    hc = _conv_glu_fwd(proj1, p["o_conv_w"], p["o_conv_b"])
    h3 = _ln_gate_fwd(hc, proj1, p["o_ln_g"], p["o_ln_b"])
    out1 = _mm_nn(h3, o_w_out_f, f32, "out1")
    dy, d_out1, dg_post1, loss_part = _final_loss(x1, out1, tgt, p["o_norm_post"])

    dh3 = _mm_nt([(d_out1, 0, o_w_out_f, 0, D_MODEL)], bf16, "dh3", tm=1024, tn=D_CONV)
    g_o_w_out = _mm_tn(h3, d_out1, "dw_out1")
    dhc, dz1, dg_ln, db_ln = _ln_gate_bwd(hc, proj1, dh3, p["o_ln_g"], p["o_ln_b"])
    dval, dgate, dw_conv1, db_conv1 = _conv_glu_bwd(dhc, proj1, p["o_conv_w"])
    dproj1 = jnp.concatenate([dval, dgate, dz1], axis=1)
    du1 = _mm_nt([(dproj1, 0, o_w_in_f, 0, O_IN)], f32, "du1")
    g_o_w_in = _mm_tn(u1, dproj1, "dw_in1", tn=ow, blocked=True)
    dx1, d_out0, dg_pre1, dg_post0 = _mid_bwd(x1, du1, dy, out0, p["o_norm_pre"], p["e_norm_post"])

    dcat = _mm_nt([(d_out0, 0, e_w_out_f, 0, D_MODEL)], bf16, "dcat", tm=1024, tn=D_CONV)
    g_e_w_out = _mm_tn(cat, d_out0, "dw_out0")
    dy_ssd, do, dz0, delta, dg_ssd_norm = _gate0_bwd(y, z0, o, dcat, p["e_ssd_norm"])
    early = [g_e_w_out.reshape(N_DEV, wr, D_MODEL).astype(bf16), g_o_w_in.astype(bf16),
             g_o_w_out.reshape(N_DEV, wr, D_MODEL).astype(bf16)]
    dq_hm, dk_hm, dv, early_parts = _fox_bwd(qa, ka, qkv, do, lse, delta[0:N_HEADS].reshape(N_HEADS, 1, -1), early)
    dq, dk, dc = _fox_bwd_post(dq_hm, dk_hm)
    dpre, ddt_raw, dprm = _ssd_bwd(act, pre, dtf, prm, hs, dy_ssd)
    ddtf, dfb = _fox_gate_bwd(dc, dtf, prm, ddt_raw)
    dxraw, dw_conv0, db_conv0 = _conv_ssd_bwd(dpre, xraw, p["e_conv_w"])
    gw_dtf = _mm_tn(u0, ddtf, "dw_in0_dtf")
    g_e_w_in_full = jnp.concatenate([
        _mm_tn(u0, dz0, "dw_in0_z"), _mm_tn(u0, dxraw, "dw_in0_xbc"), gw_dtf[:, 0:16],
        _mm_tn(u0, dq, "dw_in0_q"), _mm_tn(u0, dk, "dw_in0_k"), _mm_tn(u0, dv, "dw_in0_v"), gw_dtf[:, 16:32]], axis=1)
    du0, last_parts = _mm_nt(
        [(dz0, 0, w_z, 0, 2048), (dxraw, 0, w_xbc, 0, 2048), (dq, 0, w_qkv, 0, 1024), (dk, 0, w_qkv, 1, 1024),
         (dv, 0, w_qkv, 2, 1024), (ddtf, 0, w_dtf, 0, 128)], f32, "du0", gs=[_col_shards(g_e_w_in_full).astype(bf16)])
    grad_x, dg_pre0 = _first_bwd(xs, du0, dx1, p["e_norm_pre"])

    big_parts = list(last_parts) + list(early_parts)
    outs = {}
    for n, parts in zip(BIG_PARAMS, big_parts):
        outs[n] = _adamw(mat(w_in[n]), mat(m_in[n]), mat(v_in[n]), parts, "adamw_" + n)

    small_grads = {
        "e_norm_pre": dg_pre0, "e_conv_w": dw_conv0, "e_conv_b": db_conv0, "e_dt_bias": dprm[0:1, 0:16],
        "e_a_log": dprm[1:2, 0:16], "e_d_skip": dprm[2:3, 0:16], "e_fgate_b": dfb[:, F_LANE:F_LANE + 16],
        "e_ssd_norm": dg_ssd_norm, "e_norm_post": dg_post0, "o_norm_pre": dg_pre1, "o_conv_w": dw_conv1,
        "o_conv_b": db_conv1, "o_ln_g": dg_ln, "o_ln_b": db_ln, "o_norm_post": dg_post1,
    }
    gathered = _all_gather(_pack([small_grads[n] for n, _, _, _ in SMALL_PARAMS] + [loss_part]), "gather_small_grads")
    summed = _unpack(_sum_parts(gathered, "sum_small_grads"), [(k, w) for _, k, w, _ in SMALL_PARAMS] + [(1, 128)])
    loss = summed[-1][0, 0]
    g_local = []
    for (n, k, w, sh), g in zip(SMALL_PARAMS, summed):
        g_local.append(lax.dynamic_slice_in_dim(g, me * (w // N_DEV), w // N_DEV, axis=1) if sh else g)
    names = [n for n, _, _, _ in SMALL_PARAMS]
    local_shapes = [(k, w // N_DEV if sh else w) for _, k, w, sh in SMALL_PARAMS]
    res = _adamw(_pack([mat(w_in[n]) for n in names]), _pack([mat(m_in[n]) for n in names]),
                 _pack([mat(v_in[n]) for n in names]), _pack(g_local)[None], "adamw_small", tr=8)
    unpacked = [_unpack(r, local_shapes) for r in res]
    for i, n in enumerate(names):
        outs[n] = tuple(u[i] for u in unpacked)

    ret = [loss, grad_x.reshape(x.shape)]
    for j in range(4):
        ret += [outs[n][j].reshape(w_in[n].shape) for n in WEIGHT_ORDER]
    return tuple(ret)
```

```python
import jax
import jax.numpy as jnp
from jax import lax
from jax.experimental import pallas as pl
from jax.experimental.pallas import tpu as pltpu

f32 = jnp.float32
bf16 = jnp.bfloat16

N_DEV = 8
D_MODEL = 1024
N_HEADS = 16
HEAD_DIM = 64
N_GROUPS = 4
HEADS_PER_GROUP = 4
D_STATE = 128
CHUNK = 512
SSD_CONV = 4
CONV_WIDTH = 31
D_CONV = 2048
EPS = 1e-6
XBC_W = 2048
B_OFF = 1024
C_OFF = 1536
F_LANE = 16
HALO = 32

ADAM_LR = 0.001
ADAM_B1 = 0.9
ADAM_B2 = 0.999
ADAM_EPS = 1e-08
ADAM_WD = 0.01
ADAM_STEP = 10

VMEM_LIMIT_BYTES = 56 * 1024 * 1024
ROW_TILE = 512
CONV_ROW_TILE = 256
CONV_COL_TILE = 512
CONV_SUB = 32
ATTN_TILE = 512
ATTN_FWD_TILE = 1024

NT = (((1,), (1,)), ((), ()))
TN = (((0,), (0,)), ((), ()))
HIGHEST = lax.Precision.HIGHEST
NEG = -1e30


def _cp(*sem):
    return pltpu.CompilerParams(dimension_semantics=sem if sem else None, vmem_limit_bytes=VMEM_LIMIT_BYTES)


def _sigmoid(x):
    return jax.nn.sigmoid(x)


def _silu(x):
    return x * _sigmoid(x)


def _dsilu(x):
    s = _sigmoid(x)
    return s * (1.0 + x * (1.0 - s))


def _softplus(x):
    return jnp.maximum(x, 0.0) + jnp.log(1.0 + jnp.exp(-jnp.abs(x)))


def _log_sigmoid(x):
    return jnp.minimum(x, 0.0) - jnp.log(1.0 + jnp.exp(-jnp.abs(x)))


def _dot(a, b, dims=None, precision=None):
    if dims is None:
        return jnp.dot(a, b, preferred_element_type=f32, precision=precision)
    return lax.dot_general(a, b, dims, preferred_element_type=f32, precision=precision)


def _mm_nn(a, b, out_dtype, name, tm=1024, tn=1024):
    m, k = a.shape
    n = b.shape[1]
    tm, tn = min(tm, m), min(tn, n)

    def body(a_ref, b_ref, o_ref):
        o_ref[...] = _dot(a_ref[...], b_ref[...]).astype(o_ref.dtype)

    return pl.pallas_call(
        body, name=name, grid=(n // tn, m // tm),
        in_specs=[pl.BlockSpec((tm, k), lambda j, i: (i, 0)), pl.BlockSpec((k, tn), lambda j, i: (0, j))],
        out_specs=pl.BlockSpec((tm, tn), lambda j, i: (i, j)),
        out_shape=jax.ShapeDtypeStruct((m, n), out_dtype), compiler_params=_cp("parallel", "parallel"))(a, b)


def _mm_nt(pairs, out_dtype, name, tm=512, tn=512, gs=(), b_kn=False):
    m = pairs[0][0].shape[0]
    n = pairs[0][2].shape[1] if b_kn else pairs[0][2].shape[0]
    tm, tn = min(tm, m), min(tn, n)
    npair = len(pairs)
    ng = len(gs)
    grid = (n // tn, m // tm)

    def body(*refs):
        g_refs = refs[2 * npair:2 * npair + ng]
        o_ref = refs[2 * npair + ng]
        r_refs = refs[2 * npair + ng + 1:2 * npair + 2 * ng + 1]
        sems = refs[2 * npair + 2 * ng + 1:]
        if ng:
            copies = _exchange_copies(g_refs, r_refs, *sems)

            @pl.when((pl.program_id(0) == 0) & (pl.program_id(1) == 0))
            def _():
                _exchange_start(copies)
        acc = None
        for p in range(npair):
            d = _dot(refs[2 * p][...].astype(bf16), refs[2 * p + 1][...], None if b_kn else NT)
            acc = d if acc is None else acc + d
        o_ref[...] = acc.astype(o_ref.dtype)
        if ng:
            @pl.when((pl.program_id(0) == grid[0] - 1) & (pl.program_id(1) == grid[1] - 1))
            def _():
                _exchange_wait(copies)

    in_specs, args = [], []
    for a, acb, b, bcb, k in pairs:
        in_specs.append(pl.BlockSpec((tm, k), lambda j, i, acb=acb: (i, acb)))
        if b_kn:
            in_specs.append(pl.BlockSpec((k, tn), lambda j, i, bcb=bcb: (bcb, j)))
        else:
            in_specs.append(pl.BlockSpec((tn, k), lambda j, i, bcb=bcb: (j, bcb)))
        args += [a, b]
    anyspec = pl.BlockSpec(memory_space=pl.ANY)
    outs = pl.pallas_call(
        body, name=name, grid=grid, in_specs=in_specs + [anyspec] * ng,
        out_specs=[pl.BlockSpec((tm, tn), lambda j, i: (i, j))] + [anyspec] * ng,
        out_shape=[jax.ShapeDtypeStruct((m, n), out_dtype)] + [jax.ShapeDtypeStruct(g.shape, g.dtype) for g in gs],
        scratch_shapes=_exchange_sems(ng) if ng else [],
        compiler_params=_cp("arbitrary", "arbitrary") if ng else _cp("parallel", "parallel"))(*args, *gs)
    return (outs[0], outs[1:]) if ng else outs[0]


def _mm_tn(a, b, name, a_cb=0, am=None, b_cb=0, bn=None, tn=1024, tk=1024, blocked=False):
    k = a.shape[0]
    am = a.shape[1] if am is None else am
    bn = b.shape[1] if bn is None else bn
    tm = min(1024, am)
    tn, tk = min(tn, bn), min(tk, k)
    a_off, b_off = a_cb * (am // tm), b_cb * (bn // tn)

    def body(a_ref, b_ref, o_ref):
        @pl.when(pl.program_id(2) == 0)
        def _():
            o_ref[...] = jnp.zeros_like(o_ref)
        d = _dot(a_ref[...].astype(bf16), b_ref[...].astype(bf16), TN)
        o_ref[...] += d.reshape(o_ref.shape)

    if blocked:
        out_spec = pl.BlockSpec((1, tm, tn), lambda i, j, kk: (j, i, 0))
        out_shape = jax.ShapeDtypeStruct((bn // tn, am, tn), f32)
    else:
        out_spec = pl.BlockSpec((tm, tn), lambda i, j, kk: (i, j))
        out_shape = jax.ShapeDtypeStruct((am, bn), f32)
    return pl.pallas_call(
        body, name=name, grid=(am // tm, bn // tn, k // tk),
        in_specs=[pl.BlockSpec((tk, tm), lambda i, j, kk: (kk, a_off + i)),
                  pl.BlockSpec((tk, tn), lambda i, j, kk: (kk, b_off + j))],
        out_specs=out_spec, out_shape=out_shape,
        compiler_params=_cp("parallel", "parallel", "arbitrary"))(a, b)


def _rowspec(ts, w, cb=0):
    return pl.BlockSpec((ts, w), lambda i: (i, cb))


def _vecspec(w):
    return pl.BlockSpec((1, w), lambda i: (0, 0))


def _rms_fwd(x, g, name):
    s, d = x.shape
    ts = min(ROW_TILE, s)

    def body(x_ref, g_ref, u_ref):
        xv = x_ref[...]
        r = lax.rsqrt(jnp.mean(xv * xv, axis=-1, keepdims=True) + EPS)
        u_ref[...] = (xv * r * g_ref[...]).astype(bf16)

    return pl.pallas_call(
        body, name=name, grid=(s // ts,), in_specs=[_rowspec(ts, d), _vecspec(d)], out_specs=_rowspec(ts, d),
        out_shape=jax.ShapeDtypeStruct((s, d), bf16), compiler_params=_cp("parallel"))(x, g)


def _rms_bwd_vals(xv, g, dy):
    r = lax.rsqrt(jnp.mean(xv * xv, axis=-1, keepdims=True) + EPS)
    xh = xv * r
    dg = jnp.sum(dy * xh, axis=0, keepdims=True)
    dxh = dy * g
    dx = r * (dxh - xh * jnp.mean(dxh * xh, axis=-1, keepdims=True))
    return dx, dg


def _gate0_fwd(y, z, o, ssd_norm):
    s = y.shape[0]
    ts = min(ROW_TILE, s)
    gw = D_MODEL // N_GROUPS

    def body(y_ref, zs_ref, zf_ref, o_ref, w_ref, cat_ref):
        yg = y_ref[...].astype(f32) * _silu(zs_ref[...].astype(f32))
        for g in range(N_GROUPS):
            seg = yg[:, gw * g:gw * (g + 1)]
            r = lax.rsqrt(jnp.mean(seg * seg, axis=-1, keepdims=True) + EPS)
            cat_ref[:, gw * g:gw * (g + 1)] = (seg * r * w_ref[:, gw * g:gw * (g + 1)]).astype(bf16)
        cat_ref[:, D_MODEL:] = (o_ref[...].astype(f32) * _silu(zf_ref[...].astype(f32))).astype(bf16)

    return pl.pallas_call(
        body, name="gate0_fwd", grid=(s // ts,),
        in_specs=[_rowspec(ts, D_MODEL), _rowspec(ts, D_MODEL, 0), _rowspec(ts, D_MODEL, 1), _rowspec(ts, D_MODEL),
                  _vecspec(D_MODEL)],
        out_specs=_rowspec(ts, 2 * D_MODEL),
        out_shape=jax.ShapeDtypeStruct((s, 2 * D_MODEL), bf16), compiler_params=_cp("parallel"))(y, z, z, o, ssd_norm)


def _post0_pre1(x, out0, g_post0, g_pre1):
    s, d = x.shape
    ts = min(ROW_TILE, s)

    def body(x_ref, o_ref, gp_ref, gn_ref, x1_ref, u1_ref):
        ov = o_ref[...]
        r = lax.rsqrt(jnp.mean(ov * ov, axis=-1, keepdims=True) + EPS)
        x1 = x_ref[...] + ov * r * gp_ref[...]
        x1_ref[...] = x1
        r1 = lax.rsqrt(jnp.mean(x1 * x1, axis=-1, keepdims=True) + EPS)
        u1_ref[...] = (x1 * r1 * gn_ref[...]).astype(bf16)

    return pl.pallas_call(
        body, name="post0_pre1", grid=(s // ts,),
        in_specs=[_rowspec(ts, d), _rowspec(ts, d), _vecspec(d), _vecspec(d)],
        out_specs=[_rowspec(ts, d), _rowspec(ts, d)],
        out_shape=[jax.ShapeDtypeStruct((s, d), f32), jax.ShapeDtypeStruct((s, d), bf16)],
        compiler_params=_cp("parallel"))(x, out0, g_post0, g_pre1)


def _ln_vals(hc, g, b):
    mu = jnp.mean(hc, axis=-1, keepdims=True)
    xc = hc - mu
    rstd = lax.rsqrt(jnp.mean(xc * xc, axis=-1, keepdims=True) + EPS)
    xh = xc * rstd
    return xh, rstd, xh * g + b


def _ln_gate_fwd(hc, proj1, ln_g, ln_b):
    s = hc.shape[0]
    ts = min(ROW_TILE, s)

    def body(hc_ref, z_ref, g_ref, b_ref, h3_ref):
        _, _, ln = _ln_vals(hc_ref[...].astype(f32), g_ref[...], b_ref[...])
        h3_ref[...] = (_silu(ln) * _silu(z_ref[...].astype(f32))).astype(bf16)

    return pl.pallas_call(
        body, name="ln_gate_fwd", grid=(s // ts,),
        in_specs=[_rowspec(ts, D_CONV), _rowspec(ts, D_CONV, 2), _vecspec(D_CONV), _vecspec(D_CONV)],
        out_specs=_rowspec(ts, D_CONV),
        out_shape=jax.ShapeDtypeStruct((s, D_CONV), bf16), compiler_params=_cp("parallel"))(hc, proj1, ln_g, ln_b)


def _final_loss(x1, out1, tgt, g_post1):
    s, d = x1.shape
    ts = min(ROW_TILE, s)

    def body(x1_ref, o_ref, t_ref, g_ref, dy_ref, do_ref, dg_ref, loss_ref):
        i = pl.program_id(0)

        @pl.when(i == 0)
        def _():
            dg_ref[...] = jnp.zeros_like(dg_ref)
            loss_ref[...] = jnp.zeros_like(loss_ref)
        ov = o_ref[...]
        g = g_ref[...]
        r = lax.rsqrt(jnp.mean(ov * ov, axis=-1, keepdims=True) + EPS)
        diff = x1_ref[...] + ov * r * g - t_ref[...]
        row = jnp.mean(diff * diff, axis=-1, keepdims=True)
        loss_ref[...] += jnp.broadcast_to(0.5 * jnp.sum(row, axis=0, keepdims=True), loss_ref.shape)
        dy = diff * (1.0 / d)
        dy_ref[...] = dy
        dx, dg = _rms_bwd_vals(ov, g, dy)
        do_ref[...] = dx.astype(bf16)
        dg_ref[...] += dg

    return pl.pallas_call(
        body, name="final_loss", grid=(s // ts,),
        in_specs=[_rowspec(ts, d), _rowspec(ts, d), _rowspec(ts, d), _vecspec(d)],
        out_specs=[_rowspec(ts, d), _rowspec(ts, d), _vecspec(d), _vecspec(128)],
        out_shape=[jax.ShapeDtypeStruct((s, d), f32), jax.ShapeDtypeStruct((s, d), bf16),
                   jax.ShapeDtypeStruct((1, d), f32), jax.ShapeDtypeStruct((1, 128), f32)],
        compiler_params=_cp("arbitrary"))(x1, out1, tgt, g_post1)


def _ln_gate_bwd(hc, proj1, dh3, ln_g, ln_b):
    s = hc.shape[0]
    ts = min(ROW_TILE, s)

    def body(hc_ref, z_ref, dh_ref, g_ref, b_ref, dhc_ref, dz_ref, dg_ref, db_ref):
        @pl.when(pl.program_id(0) == 0)
        def _():
            dg_ref[...] = jnp.zeros_like(dg_ref)
            db_ref[...] = jnp.zeros_like(db_ref)
        g = g_ref[...]
        xh, rstd, ln = _ln_vals(hc_ref[...].astype(f32), g, b_ref[...])
        zv = z_ref[...].astype(f32)
        dh3 = dh_ref[...].astype(f32)
        dz_ref[...] = (dh3 * _silu(ln) * _dsilu(zv)).astype(bf16)
        dln = dh3 * _silu(zv) * _dsilu(ln)
        dg_ref[...] += jnp.sum(dln * xh, axis=0, keepdims=True)
        db_ref[...] += jnp.sum(dln, axis=0, keepdims=True)
        dxh = dln * g
        dhc = rstd * (dxh - jnp.mean(dxh, axis=-1, keepdims=True) - xh * jnp.mean(dxh * xh, axis=-1, keepdims=True))
        dhc_ref[...] = dhc.astype(bf16)

    return pl.pallas_call(
        body, name="ln_gate_bwd", grid=(s // ts,),
        in_specs=[_rowspec(ts, D_CONV), _rowspec(ts, D_CONV, 2), _rowspec(ts, D_CONV), _vecspec(D_CONV),
                  _vecspec(D_CONV)],
        out_specs=[_rowspec(ts, D_CONV), _rowspec(ts, D_CONV), _vecspec(D_CONV), _vecspec(D_CONV)],
        out_shape=[jax.ShapeDtypeStruct((s, D_CONV), bf16), jax.ShapeDtypeStruct((s, D_CONV), bf16),
                   jax.ShapeDtypeStruct((1, D_CONV), f32), jax.ShapeDtypeStruct((1, D_CONV), f32)],
        compiler_params=_cp("arbitrary"))(hc, proj1, dh3, ln_g, ln_b)


def _mid_bwd(x1, du1, dy, out0, g_pre1, g_post0):
    s, d = x1.shape
    ts = min(ROW_TILE, s)

    def body(x1_ref, du_ref, dy_ref, o_ref, gn_ref, gp_ref, dx1_ref, do_ref, dgn_ref, dgp_ref):
        @pl.when(pl.program_id(0) == 0)
        def _():
            dgn_ref[...] = jnp.zeros_like(dgn_ref)
            dgp_ref[...] = jnp.zeros_like(dgp_ref)
        dxa, dgn = _rms_bwd_vals(x1_ref[...], gn_ref[...], du_ref[...])
        dx1 = dy_ref[...] + dxa
        dx1_ref[...] = dx1
        dgn_ref[...] += dgn
        dxo, dgp = _rms_bwd_vals(o_ref[...], gp_ref[...], dx1)
        do_ref[...] = dxo.astype(bf16)
        dgp_ref[...] += dgp

    return pl.pallas_call(
        body, name="mid_bwd", grid=(s // ts,),
        in_specs=[_rowspec(ts, d)] * 4 + [_vecspec(d), _vecspec(d)],
        out_specs=[_rowspec(ts, d), _rowspec(ts, d), _vecspec(d), _vecspec(d)],
        out_shape=[jax.ShapeDtypeStruct((s, d), f32), jax.ShapeDtypeStruct((s, d), bf16),
                   jax.ShapeDtypeStruct((1, d), f32), jax.ShapeDtypeStruct((1, d), f32)],
        compiler_params=_cp("arbitrary"))(x1, du1, dy, out0, g_pre1, g_post0)


def _first_bwd(x, du0, dx1, g_pre0):
    s, d = x.shape
    ts = min(ROW_TILE, s)

    def body(x_ref, du_ref, dx1_ref, g_ref, dx_ref, dg_ref):
        @pl.when(pl.program_id(0) == 0)
        def _():
            dg_ref[...] = jnp.zeros_like(dg_ref)
        dxa, dg = _rms_bwd_vals(x_ref[...], g_ref[...], du_ref[...])
        dx_ref[...] = dx1_ref[...] + dxa
        dg_ref[...] += dg

    return pl.pallas_call(
        body, name="first_bwd", grid=(s // ts,),
        in_specs=[_rowspec(ts, d)] * 3 + [_vecspec(d)],
        out_specs=[_rowspec(ts, d), _vecspec(d)],
        out_shape=[jax.ShapeDtypeStruct((s, d), f32), jax.ShapeDtypeStruct((1, d), f32)],
        compiler_params=_cp("arbitrary"))(x, du0, dx1, g_pre0)


def _gate0_bwd(y, z, o, dcat, ssd_norm):
    s = y.shape[0]
    ts = min(ROW_TILE, s)
    gw = D_MODEL // N_GROUPS

    def body(y_ref, zs_ref, zf_ref, o_ref, dn_ref, dg_ref, w_ref, dy_ref, do_ref, dz_ref, delta_ref, dw_ref):
        @pl.when(pl.program_id(0) == 0)
        def _():
            dw_ref[...] = jnp.zeros_like(dw_ref)
        yv = y_ref[...].astype(f32)
        zs = zs_ref[...].astype(f32)
        sz = _silu(zs)
        yg = yv * sz
        dyn = dn_ref[...].astype(f32)
        for g in range(N_GROUPS):
            sl = slice(gw * g, gw * (g + 1))
            seg = yg[:, sl]
            r = lax.rsqrt(jnp.mean(seg * seg, axis=-1, keepdims=True) + EPS)
            yh = seg * r
            dn = dyn[:, sl]
            dw_ref[:, sl] += jnp.sum(dn * yh, axis=0, keepdims=True)
            dyh = dn * w_ref[:, sl]
            dyg = r * (dyh - yh * jnp.mean(dyh * yh, axis=-1, keepdims=True))
            dy_ref[:, sl] = (dyg * sz[:, sl]).astype(bf16)
            dz_ref[:, sl] = (dyg * yv[:, sl] * _dsilu(zs[:, sl])).astype(bf16)
        zf = zf_ref[...].astype(f32)
        ov = o_ref[...].astype(f32)
        dog = dg_ref[...].astype(f32)
        dov = (dog * _silu(zf)).astype(bf16)
        do_ref[...] = dov
        dz_ref[:, D_MODEL:] = (dog * ov * _dsilu(zf)).astype(bf16)
        prod = dov.astype(f32) * ov
        lane = lax.broadcasted_iota(jnp.int32, (ts, 128), 1)
        delta = jnp.zeros((ts, 128), f32)
        for h in range(N_HEADS):
            dh = jnp.sum(prod[:, HEAD_DIM * h:HEAD_DIM * (h + 1)], axis=-1, keepdims=True)
            delta = delta + jnp.where(lane == h, dh, 0.0)
        delta_ref[...] = delta.T

    return pl.pallas_call(
        body, name="gate0_bwd", grid=(s // ts,),
        in_specs=[_rowspec(ts, D_MODEL), _rowspec(ts, D_MODEL, 0), _rowspec(ts, D_MODEL, 1), _rowspec(ts, D_MODEL),
                  _rowspec(ts, D_MODEL, 0), _rowspec(ts, D_MODEL, 1), _vecspec(D_MODEL)],
        out_specs=[_rowspec(ts, D_MODEL), _rowspec(ts, D_MODEL), _rowspec(ts, 2 * D_MODEL),
                   pl.BlockSpec((128, ts), lambda i: (0, i)), _vecspec(D_MODEL)],
        out_shape=[jax.ShapeDtypeStruct((s, D_MODEL), bf16), jax.ShapeDtypeStruct((s, D_MODEL), bf16),
                   jax.ShapeDtypeStruct((s, 2 * D_MODEL), bf16), jax.ShapeDtypeStruct((128, s), f32),
                   jax.ShapeDtypeStruct((1, D_MODEL), f32)],
        compiler_params=_cp("arbitrary"))(y, z, z, o, dcat, dcat, ssd_norm)


def _conv_grid(s, c):
    ts, cb = min(CONV_ROW_TILE, s), min(CONV_COL_TILE, c)
    return ts, cb, (c // cb, s // ts)


def _cur(ts, cb, off=0):
    return pl.BlockSpec((ts, cb), lambda c, i: (i, c + off))


def _prev_halo(ts, cb, off=0):
    return pl.BlockSpec((HALO, cb), lambda c, i: (jnp.maximum(i * (ts // HALO) - 1, 0), c + off))


def _next_halo(ts, cb, s, off=0):
    return pl.BlockSpec((HALO, cb), lambda c, i: (jnp.minimum((i + 1) * (ts // HALO), s // HALO - 1), c + off))


def _wspec(k, cb):
    return pl.BlockSpec((k, cb), lambda c, i: (0, c))


def _phases(offsets):
    return sorted({o % 8 for o in offsets} - {0})


def _shift_scratch(offsets, ts, cb):
    return pltpu.VMEM((max(len(_phases(offsets)), 1), ts + HALO - 8, cb), f32)


def _fill_phases(ext_ref, sh_ref, offsets, ts):
    for j, r in enumerate(_phases(offsets)):
        sh_ref[j] = ext_ref[pl.ds(r, ts + HALO - 8), :]


def _slab(ext_ref, sh_ref, offsets, off, start):
    r = off % 8
    a = off - r + start
    if r == 0:
        return ext_ref[a:a + CONV_SUB, :]
    return sh_ref[_phases(offsets).index(r), a:a + CONV_SUB, :]


def _conv_taps(ext_ref, sh_ref, w_ref, b_ref, ts, k_taps, emit):
    offsets = [HALO - (k_taps - 1) + k for k in range(k_taps)]
    _fill_phases(ext_ref, sh_ref, offsets, ts)
    for sb in range(ts // CONV_SUB):
        acc = b_ref[...]
        for k in range(k_taps):
            acc = acc + w_ref[k:k + 1, :] * _slab(ext_ref, sh_ref, offsets, offsets[k], sb * CONV_SUB)
        emit(slice(sb * CONV_SUB, (sb + 1) * CONV_SUB), acc)


def _conv_ssd_fwd(xraw, w, b):
    s, c = xraw.shape
    ts, cb, grid = _conv_grid(s, c)
    offsets = [HALO - (SSD_CONV - 1) + k for k in range(SSD_CONV)]

    def body(x_ref, xh_ref, w_ref, b_ref, pre_ref, act_ref, ext_ref, sh_ref):
        first = pl.program_id(1) == 0
        ext_ref[0:HALO, :] = jnp.where(first, 0.0, xh_ref[...].astype(f32))
        ext_ref[HALO:, :] = x_ref[...].astype(f32)

        def emit(rows, pre):
            pre_ref[rows, :] = pre.astype(bf16)
            act_ref[rows, :] = _silu(pre).astype(bf16)
        _conv_taps(ext_ref, sh_ref, w_ref, b_ref, ts, SSD_CONV, emit)

    return pl.pallas_call(
        body, name="conv_ssd_fwd", grid=grid,
        in_specs=[_cur(ts, cb), _prev_halo(ts, cb), _wspec(SSD_CONV, cb), _wspec(1, cb)],
        out_specs=[_cur(ts, cb), _cur(ts, cb)],
        out_shape=[jax.ShapeDtypeStruct((s, c), bf16)] * 2,
        scratch_shapes=[pltpu.VMEM((HALO + ts, cb), f32), _shift_scratch(offsets, ts, cb)],
        compiler_params=_cp("parallel", "parallel"))(xraw, xraw, w, b)


def _conv_glu_fwd(proj1, w, b):
    s = proj1.shape[0]
    c = D_CONV
    ts, cb, grid = _conv_grid(s, c)
    goff = c // cb

    offsets = [HALO - (CONV_WIDTH - 1) + k for k in range(CONV_WIDTH)]

    def body(v_ref, g_ref, vh_ref, gh_ref, w_ref, b_ref, hc_ref, ext_ref, sh_ref):
        first = pl.program_id(1) == 0
        hh = vh_ref[...].astype(f32) * _sigmoid(gh_ref[...].astype(f32))
        ext_ref[0:HALO, :] = jnp.where(first, 0.0, hh)
        ext_ref[HALO:, :] = v_ref[...].astype(f32) * _sigmoid(g_ref[...].astype(f32))

        def emit(rows, hc):
            hc_ref[rows, :] = hc.astype(bf16)
        _conv_taps(ext_ref, sh_ref, w_ref, b_ref, ts, CONV_WIDTH, emit)

    return pl.pallas_call(
        body, name="conv_glu_fwd", grid=grid,
        in_specs=[_cur(ts, cb), _cur(ts, cb, goff), _prev_halo(ts, cb), _prev_halo(ts, cb, goff),
                  _wspec(CONV_WIDTH, cb), _wspec(1, cb)],
        out_specs=_cur(ts, cb),
        out_shape=jax.ShapeDtypeStruct((s, c), bf16),
        scratch_shapes=[pltpu.VMEM((HALO + ts, cb), f32), _shift_scratch(offsets, ts, cb)],
        compiler_params=_cp("parallel", "parallel"))(proj1, proj1, proj1, proj1, w, b)


def _conv_bwd_offsets(k_taps):
    return [k_taps - 1 - k for k in range(k_taps)], [HALO - (k_taps - 1) + k for k in range(k_taps)]


def _conv_bwd_scratch(k_taps, ts, cb):
    d_offs, x_offs = _conv_bwd_offsets(k_taps)
    return [pltpu.VMEM((ts + HALO, cb), f32), _shift_scratch(d_offs, ts, cb),
            pltpu.VMEM((HALO + ts, cb), f32), _shift_scratch(x_offs, ts, cb),
            pltpu.VMEM((k_taps, 8, cb), f32), pltpu.VMEM((8, cb), f32)]


def _conv_bwd_core(dp, dpn_ref, last, w_ref, scratch, dw_ref, db_ref, ts, k_taps, emit):
    dext_ref, dsh_ref, xext_ref, xsh_ref, dw8_ref, db8_ref = scratch
    d_offs, x_offs = _conv_bwd_offsets(k_taps)
    dext_ref[0:ts, :] = dp
    dext_ref[ts:, :] = jnp.where(last, 0.0, dpn_ref[...].astype(f32))
    _fill_phases(dext_ref, dsh_ref, d_offs, ts)
    _fill_phases(xext_ref, xsh_ref, x_offs, ts)

    @pl.when(pl.program_id(1) == 0)
    def _():
        dw8_ref[...] = jnp.zeros_like(dw8_ref)
        db8_ref[...] = jnp.zeros_like(db8_ref)
    cb = dp.shape[1]
    for sb in range(ts // CONV_SUB):
        start = sb * CONV_SUB
        dpv = dext_ref[start:start + CONV_SUB, :]
        dx = None
        for k in range(k_taps):
            t = w_ref[k:k + 1, :] * _slab(dext_ref, dsh_ref, d_offs, d_offs[k], start)
            dx = t if dx is None else dx + t
            prod = dpv * _slab(xext_ref, xsh_ref, x_offs, x_offs[k], start)
            dw8_ref[k] += jnp.sum(prod.reshape(CONV_SUB // 8, 8, cb), axis=0)
        db8_ref[...] += jnp.sum(dpv.reshape(CONV_SUB // 8, 8, cb), axis=0)
        emit(slice(start, start + CONV_SUB), dx)

    @pl.when(last)
    def _():
        dw_ref[...] = jnp.sum(dw8_ref[...], axis=1)
        db_ref[...] = jnp.sum(db8_ref[...], axis=0, keepdims=True)


def _conv_ssd_bwd(dpre, xraw, w):
    s, c = xraw.shape
    ts, cb, grid = _conv_grid(s, c)
    nb = s // ts

    def body(dp_ref, dpn_ref, x_ref, xh_ref, w_ref, dx_ref, dw_ref, db_ref, *scratch):
        i = pl.program_id(1)
        xext_ref = scratch[2]
        xext_ref[0:HALO, :] = jnp.where(i == 0, 0.0, xh_ref[...].astype(f32))
        xext_ref[HALO:, :] = x_ref[...].astype(f32)

        def emit(rows, dx):
            dx_ref[rows, :] = dx.astype(bf16)
        _conv_bwd_core(dp_ref[...].astype(f32), dpn_ref, i == nb - 1, w_ref, scratch, dw_ref, db_ref, ts, SSD_CONV, emit)

    return pl.pallas_call(
        body, name="conv_ssd_bwd", grid=grid,
        in_specs=[_cur(ts, cb), _next_halo(ts, cb, s), _cur(ts, cb), _prev_halo(ts, cb), _wspec(SSD_CONV, cb)],
        out_specs=[_cur(ts, cb), _wspec(SSD_CONV, cb), _wspec(1, cb)],
        out_shape=[jax.ShapeDtypeStruct((s, c), bf16), jax.ShapeDtypeStruct((SSD_CONV, c), f32),
                   jax.ShapeDtypeStruct((1, c), f32)],
        scratch_shapes=_conv_bwd_scratch(SSD_CONV, ts, cb),
        compiler_params=_cp("parallel", "arbitrary"))(dpre, dpre, xraw, xraw, w)


def _conv_glu_bwd(dhc, proj1, w):
    s = proj1.shape[0]
    c = D_CONV
    ts, cb, grid = _conv_grid(s, c)
    nb = s // ts
    goff = c // cb

    def body(dp_ref, dpn_ref, v_ref, g_ref, vh_ref, gh_ref, w_ref, dv_ref, dg_ref, dw_ref, db_ref, *scratch):
        i = pl.program_id(1)
        xext_ref = scratch[2]
        xext_ref[0:HALO, :] = jnp.where(i == 0, 0.0, vh_ref[...].astype(f32) * _sigmoid(gh_ref[...].astype(f32)))
        xext_ref[HALO:, :] = v_ref[...].astype(f32) * _sigmoid(g_ref[...].astype(f32))

        def emit(rows, dh):
            val = v_ref[rows, :].astype(f32)
            sg = _sigmoid(g_ref[rows, :].astype(f32))
            dv_ref[rows, :] = (dh * sg).astype(bf16)
            dg_ref[rows, :] = (dh * val * sg * (1.0 - sg)).astype(bf16)
        _conv_bwd_core(dp_ref[...].astype(f32), dpn_ref, i == nb - 1, w_ref, scratch, dw_ref, db_ref, ts, CONV_WIDTH, emit)

    return pl.pallas_call(
        body, name="conv_glu_bwd", grid=grid,
        in_specs=[_cur(ts, cb), _next_halo(ts, cb, s), _cur(ts, cb), _cur(ts, cb, goff), _prev_halo(ts, cb),
                  _prev_halo(ts, cb, goff), _wspec(CONV_WIDTH, cb)],
        out_specs=[_cur(ts, cb), _cur(ts, cb), _wspec(CONV_WIDTH, cb), _wspec(1, cb)],
        out_shape=[jax.ShapeDtypeStruct((s, c), bf16), jax.ShapeDtypeStruct((s, c), bf16),
                   jax.ShapeDtypeStruct((CONV_WIDTH, c), f32), jax.ShapeDtypeStruct((1, c), f32)],
        scratch_shapes=_conv_bwd_scratch(CONV_WIDTH, ts, cb),
        compiler_params=_cp("parallel", "arbitrary"))(dhc, dhc, proj1, proj1, proj1, proj1, w)


def _ssd_common(dt_ref, prm_ref):
    l = CHUNK
    dtb = prm_ref[0:1, :]
    a = -jnp.exp(prm_ref[1:2, :])
    dsk = prm_ref[2:3, :]
    zraw = dt_ref[...] + dtb
    dt = _softplus(zraw)
    da = dt * a
    row = lax.broadcasted_iota(jnp.int32, (l, l), 0)
    col = lax.broadcasted_iota(jnp.int32, (l, l), 1)
    causal = row >= col
    cs = _dot(causal.astype(f32), da, precision=HIGHEST)
    return a, dsk, zraw, dt, cs, cs.T, causal, row, col


def _ssd_fwd(act, dtf, prm):
    s = act.shape[0]
    nc = s // CHUNK
    l = CHUNK

    def body(xs_ref, dt_ref, prm_ref, y_ref, hs_ref, st_ref):
        @pl.when(pl.program_id(0) == 0)
        def _():
            st_ref[...] = jnp.zeros_like(st_ref)
        a, dsk, _, dt, cs, cst, causal, _, _ = _ssd_common(dt_ref, prm_ref)
        for g in range(N_GROUPS):
            bm = xs_ref[:, B_OFF + D_STATE * g:B_OFF + D_STATE * (g + 1)]
            cm = xs_ref[:, C_OFF + D_STATE * g:C_OFF + D_STATE * (g + 1)]
            gmat = _dot(cm, bm, NT)
            for r in range(HEADS_PER_GROUP):
                h = HEADS_PER_GROUP * g + r
                hsl = slice(HEAD_DIM * h, HEAD_DIM * (h + 1))
                xv = xs_ref[:, hsl].astype(f32)
                csc = cs[:, h:h + 1]
                csr = cst[h:h + 1, :]
                cl = cs[l - 1:l, h:h + 1]
                dk = jnp.exp(jnp.where(causal, csc - csr, NEG))
                xd = xv * dt[:, h:h + 1]
                hp = st_ref[h]
                hs_ref[0, h] = hp
                ydiag = _dot((gmat * dk).astype(bf16), xd.astype(bf16))
                yoff = _dot(cm, hp.astype(bf16), NT) * jnp.exp(csc)
                y_ref[:, hsl] = (ydiag + yoff + xv * dsk[:, h:h + 1]).astype(bf16)
                st = _dot((xd * jnp.exp(cl - csc)).astype(bf16), bm, TN)
                st_ref[h] = hp * jnp.exp(cl) + st

    return pl.pallas_call(
        body, name="ssd_fwd", grid=(nc,),
        in_specs=[pl.BlockSpec((l, XBC_W), lambda i: (i, 0)), pl.BlockSpec((l, 128), lambda i: (i, 0)),
                  pl.BlockSpec((8, 128), lambda i: (0, 0))],
        out_specs=[pl.BlockSpec((l, D_MODEL), lambda i: (i, 0)),
                   pl.BlockSpec((1, N_HEADS, HEAD_DIM, D_STATE), lambda i: (i, 0, 0, 0))],
        out_shape=[jax.ShapeDtypeStruct((s, D_MODEL), bf16),
                   jax.ShapeDtypeStruct((nc, N_HEADS, HEAD_DIM, D_STATE), f32)],
        scratch_shapes=[pltpu.VMEM((N_HEADS, HEAD_DIM, D_STATE), f32)],
        compiler_params=_cp("arbitrary"))(act, dtf, prm)


def _ssd_bwd(act, pre, dtf, prm, hs, dy):
    s = act.shape[0]
    nc = s // CHUNK
    l = CHUNK

    def body(xs_ref, pre_ref, dt_ref, prm_ref, hs_ref, dy_ref, dpre_ref, ddt_ref, dprm_ref, dh_ref):
        @pl.when(pl.program_id(0) == 0)
        def _():
            dh_ref[...] = jnp.zeros_like(dh_ref)
            dprm_ref[...] = jnp.zeros_like(dprm_ref)
        a, dsk, zraw, dt, cs, cst, causal, row, col = _ssd_common(dt_ref, prm_ref)
        lane = lax.broadcasted_iota(jnp.int32, (l, 128), 1)
        rowl = lax.broadcasted_iota(jnp.int32, (l, 128), 0)
        sub = lax.broadcasted_iota(jnp.int32, (128, l), 0)
        lane1 = lax.broadcasted_iota(jnp.int32, (1, 128), 1)
        dcs_c = jnp.zeros((l, 128), f32)
        dcs_r = jnp.zeros((128, l), f32)
        ddt_c = jnp.zeros((l, 128), f32)
        dd_row = jnp.zeros((1, 128), f32)
        for g in range(N_GROUPS):
            bsl = slice(B_OFF + D_STATE * g, B_OFF + D_STATE * (g + 1))
            csl = slice(C_OFF + D_STATE * g, C_OFF + D_STATE * (g + 1))
            bm = xs_ref[:, bsl]
            cm = xs_ref[:, csl]
            gmat = _dot(cm, bm, NT)
            dgm = jnp.zeros((l, l), f32)
            dbg = jnp.zeros((l, D_STATE), f32)
            dcg = jnp.zeros((l, D_STATE), f32)
            for r in range(HEADS_PER_GROUP):
                h = HEADS_PER_GROUP * g + r
                hsl = slice(HEAD_DIM * h, HEAD_DIM * (h + 1))
                xv = xs_ref[:, hsl].astype(f32)
                dyv = dy_ref[:, hsl].astype(f32)
                dyb = dyv.astype(bf16)
                csc = cs[:, h:h + 1]
                csr = cst[h:h + 1, :]
                cl = cs[l - 1:l, h:h + 1]
                dk = jnp.exp(jnp.where(causal, csc - csr, NEG))
                mf = gmat * dk
                dtc = dt[:, h:h + 1]
                xd = xv * dtc
                xdb = xd.astype(bf16)
                ecs = jnp.exp(csc)
                dec = jnp.exp(cl)
                e = jnp.exp(cl - csc)
                hp = hs_ref[0, h]
                hpb = hp.astype(bf16)
                dhn = dh_ref[h]
                dhnb = dhn.astype(bf16)
                dd_h = jnp.sum(jnp.sum(dyv * xv, axis=1, keepdims=True), axis=0, keepdims=True)
                dx = dyv * dsk[:, h:h + 1]
                ch = _dot(cm, hpb, NT)
                dye = dyv * ecs
                dyeb = dye.astype(bf16)
                dcg = dcg + _dot(dyeb, hpb)
                dhp = _dot(dyeb, cm, TN)
                dcs_col = jnp.sum(dye * ch, axis=1, keepdims=True)
                dm = _dot(dyb, xdb, NT)
                dxd = _dot(mf.astype(bf16), dyb, TN)
                dgm = dgm + dm * dk
                wmat = dm * mf
                dcs_col = dcs_col + jnp.sum(wmat, axis=1, keepdims=True)
                dcs_row = -jnp.sum(wmat, axis=0, keepdims=True)
                ddec = jnp.sum(jnp.sum(hp * dhn, axis=1, keepdims=True), axis=0, keepdims=True)
                dxe = _dot(bm, dhnb, NT)
                dxd = dxd + dxe * e
                de_e = jnp.sum(dxe * xd, axis=1, keepdims=True) * e
                dbg = dbg + _dot((xd * e).astype(bf16), dhnb)
                dcs_col = dcs_col - de_e
                dlast = ddec * dec + jnp.sum(de_e, axis=0, keepdims=True)
                dh_ref[h] = dhp + dec * dhn
                dx = dx + dxd * dtc
                ddt_h = jnp.sum(dxd * xv, axis=1, keepdims=True)
                is_h = lane == h
                dcs_c = dcs_c + jnp.where(is_h, dcs_col, 0.0) + jnp.where(is_h & (rowl == l - 1), dlast, 0.0)
                dcs_r = dcs_r + jnp.where(sub == h, dcs_row, 0.0)
                ddt_c = ddt_c + jnp.where(is_h, ddt_h, 0.0)
                dd_row = dd_row + jnp.where(lane1 == h, dd_h, 0.0)
                dpre_ref[:, hsl] = (dx * _dsilu(pre_ref[:, hsl].astype(f32))).astype(bf16)
            dgb = dgm.astype(bf16)
            dcg = dcg + _dot(dgb, bm)
            dbg = dbg + _dot(dgb, cm, TN)
            dpre_ref[:, bsl] = (dbg * _dsilu(pre_ref[:, bsl].astype(f32))).astype(bf16)
            dpre_ref[:, csl] = (dcg * _dsilu(pre_ref[:, csl].astype(f32))).astype(bf16)
        dcs = dcs_c + dcs_r.T
        dda = _dot((row <= col).astype(f32), dcs, precision=HIGHEST)
        ddt = ddt_c + dda * a
        ddtraw = jnp.where(lane < N_HEADS, ddt * _sigmoid(zraw), 0.0)
        ddt_ref[...] = ddtraw
        dprm_ref[0:1, :] += jnp.sum(ddtraw, axis=0, keepdims=True)
        dprm_ref[1:2, :] += jnp.where(lane1 < N_HEADS, jnp.sum(dda * dt, axis=0, keepdims=True) * a, 0.0)
        dprm_ref[2:3, :] += dd_row

    def rev(i):
        return (nc - 1 - i, 0)

    return pl.pallas_call(
        body, name="ssd_bwd", grid=(nc,),
        in_specs=[pl.BlockSpec((l, XBC_W), rev), pl.BlockSpec((l, XBC_W), rev),
                  pl.BlockSpec((l, 128), rev), pl.BlockSpec((8, 128), lambda i: (0, 0)),
                  pl.BlockSpec((1, N_HEADS, HEAD_DIM, D_STATE), lambda i: (nc - 1 - i, 0, 0, 0)),
                  pl.BlockSpec((l, D_MODEL), rev)],
        out_specs=[pl.BlockSpec((l, XBC_W), rev), pl.BlockSpec((l, 128), rev), pl.BlockSpec((8, 128), lambda i: (0, 0))],
        out_shape=[jax.ShapeDtypeStruct((s, XBC_W), bf16), jax.ShapeDtypeStruct((s, 128), f32),
                   jax.ShapeDtypeStruct((8, 128), f32)],
        scratch_shapes=[pltpu.VMEM((N_HEADS, HEAD_DIM, D_STATE), f32)],
        compiler_params=_cp("arbitrary"))(act, pre, dtf, prm, hs, dy)


def _fox_cumsum(dtf, prm):
    s = dtf.shape[0]
    l = CHUNK

    def body(f_ref, prm_ref, c_ref, carry_ref):
        @pl.when(pl.program_id(0) == 0)
        def _():
            carry_ref[...] = jnp.zeros_like(carry_ref)
        lf = _log_sigmoid(f_ref[...] + prm_ref[3:4, :])
        row = lax.broadcasted_iota(jnp.int32, (l, l), 0)
        col = lax.broadcasted_iota(jnp.int32, (l, l), 1)
        c = _dot((row >= col).astype(f32), lf, precision=HIGHEST) + carry_ref[...]
        c_ref[...] = c
        carry_ref[...] = c[l - 1:l, :]

    return pl.pallas_call(
        body, name="fox_cumsum", grid=(s // l,),
        in_specs=[pl.BlockSpec((l, 128), lambda i: (i, 0)), pl.BlockSpec((8, 128), lambda i: (0, 0))],
        out_specs=pl.BlockSpec((l, 128), lambda i: (i, 0)),
        out_shape=jax.ShapeDtypeStruct((s, 128), f32),
        scratch_shapes=[pltpu.VMEM((1, 128), f32)],
        compiler_params=_cp("arbitrary"))(dtf, prm)


def _position():
    return lax.axis_index("x"), lax.axis_index("y"), lax.axis_index("c")


def _exchange_sems(n):
    return [pltpu.SemaphoreType.DMA((n, N_DEV - 1)), pltpu.SemaphoreType.DMA((n, N_DEV - 1)),
            pltpu.SemaphoreType.DMA((n,))]


def _exchange_copies(g_refs, r_refs, send_sems, recv_sems, local_sems, gather=False):
    n = len(g_refs)
    x, y, cc = _position()
    me = 4 * x + 2 * y + cc

    def src(a, j):
        return g_refs[a] if gather else g_refs[a].at[j]

    local = [pltpu.make_async_copy(src(a, me), r_refs[a].at[me], local_sems.at[a]) for a in range(n)]
    sends, recvs = [], []
    for k in range(1, N_DEV):
        px = 1 - x if k & 4 else x
        py = 1 - y if k & 2 else y
        pc = 1 - cc if k & 1 else cc
        pid = 4 * px + 2 * py + pc
        for a in range(n):
            sends.append(pltpu.make_async_remote_copy(
                src_ref=src(a, pid), dst_ref=r_refs[a].at[me],
                send_sem=send_sems.at[a, k - 1], recv_sem=recv_sems.at[a, k - 1],
                device_id=(px, py, pc), device_id_type=pl.DeviceIdType.MESH))
            recvs.append(pltpu.make_async_remote_copy(
                src_ref=src(a, pid), dst_ref=r_refs[a].at[pid],
                send_sem=send_sems.at[a, k - 1], recv_sem=recv_sems.at[a, k - 1],
                device_id=(px, py, pc), device_id_type=pl.DeviceIdType.MESH))
    return local, sends, recvs


def _exchange_start(copies):
    local, sends, _ = copies
    for cp in local + sends:
        cp.start()


def _exchange_wait(copies):
    local, sends, recvs = copies
    for cp in recvs:
        cp.wait_recv()
    for cp in sends:
        cp.wait_send()
    for cp in local:
        cp.wait()


AUG = HEAD_DIM
N_PAIRS = N_HEADS // 2
V_BLOCK = 2 * D_MODEL // 128


def _split3(x):
    hi = x.astype(bf16)
    r1 = x - hi.astype(f32)
    mid = r1.astype(bf16)
    lo = (r1 - mid.astype(f32)).astype(bf16)
    return hi.astype(f32), mid.astype(f32), lo.astype(f32)


def _fox_prep(qkv, c):
    s = qkv.shape[0]
    ts = min(CONV_ROW_TILE, s)
    kb = D_MODEL // 128

    def body(q_ref, k_ref, c_ref, qa_ref, ka_ref):
        lane = lax.broadcasted_iota(jnp.int32, (ts, 128), 1)
        low = lane < HEAD_DIM
        for h in range(N_HEADS):
            psl = slice(128 * (h // 2), 128 * (h // 2 + 1))
            qv = q_ref[:, psl].astype(f32) * (HEAD_DIM ** -0.5)
            kv = k_ref[:, psl].astype(f32)
            if h % 2:
                qv = pltpu.roll(qv, HEAD_DIM, 1)
                kv = pltpu.roll(kv, HEAD_DIM, 1)
            hi, mid, lo = _split3(c_ref[:, F_LANE + h:F_LANE + h + 1])
            ones = jnp.where((lane >= AUG + 3) & (lane < AUG + 6), 1.0, 0.0)
            cq = jnp.where(lane == AUG, hi, jnp.where(lane == AUG + 1, mid, jnp.where(lane == AUG + 2, lo, ones)))
            qa_ref[h] = jnp.where(low, qv, cq).astype(bf16)
            onek = jnp.where((lane >= AUG) & (lane < AUG + 3), 1.0, 0.0)
            ck = jnp.where(lane == AUG + 3, -hi, jnp.where(lane == AUG + 4, -mid, jnp.where(lane == AUG + 5, -lo, onek)))
            ka_ref[h] = jnp.where(low, kv, ck).astype(bf16)

    hm = pl.BlockSpec((N_HEADS, ts, 128), lambda i: (0, i, 0))
    return pl.pallas_call(
        body, name="fox_prep", grid=(s // ts,),
        in_specs=[_rowspec(ts, D_MODEL, 0), _rowspec(ts, D_MODEL, 1), _rowspec(ts, 128)],
        out_specs=[hm, hm], out_shape=[jax.ShapeDtypeStruct((N_HEADS, s, 128), bf16)] * 2,
        compiler_params=_cp("parallel"))(qkv, qkv, c)


def _fox_fwd(qa, ka, qkv, ws):
    s = qkv.shape[0]
    t = min(ATTN_FWD_TILE, s)
    nq = s // t
    n = len(ws)

    def body(qa_ref, ka_ref, v_ref, *rest):
        w_refs, (o_ref, lse_ref), wg_refs, sems = rest[:n], rest[n:n + 2], rest[n + 2:2 * n + 2], rest[2 * n + 2:]
        qi = pl.program_id(1)
        copies = _exchange_copies(w_refs, wg_refs, *sems, gather=True)

        @pl.when((pl.program_id(0) == 0) & (qi == 0))
        def _():
            _exchange_start(copies)
        low = lax.broadcasted_iota(jnp.int32, (t, 128), 1) < HEAD_DIM
        row = lax.broadcasted_iota(jnp.int32, (t, t), 0)
        col = lax.broadcasted_iota(jnp.int32, (t, t), 1)

        def tile(ki, carry, diagonal):
            stats, acc = carry
            koff = pl.multiple_of(ki * t, t)
            v = v_ref[pl.ds(koff, t), :]
            vh = (jnp.where(low, v, jnp.zeros_like(v)), jnp.where(low, jnp.zeros_like(v), v))
            new_stats, alphas, pv = [], [], None
            for r in range(2):
                m_old, l_old = stats[r]
                sc = _dot(qa_ref[r], ka_ref[r, pl.ds(koff, t), :], NT)
                if diagonal:
                    sc = jnp.where(col <= row, sc, NEG)
                m_new = jnp.maximum(m_old, jnp.max(sc, axis=1, keepdims=True))
                p = jnp.exp(sc - m_new)
                alpha = jnp.exp(m_old - m_new)
                new_stats.append((m_new, alpha * l_old + jnp.sum(p, axis=1, keepdims=True)))
                alphas.append(alpha)
                d = _dot(p.astype(bf16), vh[r])
                pv = d if pv is None else pv + d
            acc = acc * jnp.where(low, alphas[0], alphas[1]) + pv
            return tuple(new_stats), acc

        init = (((jnp.full((t, 1), NEG, f32), jnp.zeros((t, 1), f32)),) * 2, jnp.zeros((t, 128), f32))
        carry = lax.fori_loop(0, qi, lambda ki, cr: tile(ki, cr, False), init)
        stats, acc = tile(qi, carry, True)
        o_ref[...] = (acc / jnp.where(low, stats[0][1], stats[1][1])).astype(bf16)
        for r in range(2):
            lse = stats[r][0] + jnp.log(stats[r][1])
            lse_ref[r] = jnp.broadcast_to(lse, (t, 128)).T[0:1, :]

        @pl.when((pl.program_id(0) == N_PAIRS - 1) & (qi == nq - 1))
        def _():
            _exchange_wait(copies)

    anyspec = pl.BlockSpec(memory_space=pl.ANY)
    outs = pl.pallas_call(
        body, name="fox_fwd", grid=(N_PAIRS, nq),
        in_specs=[pl.BlockSpec((2, t, 128), lambda j, qi: (j, qi, 0)),
                  pl.BlockSpec((2, s, 128), lambda j, qi: (j, 0, 0)),
                  pl.BlockSpec((s, 128), lambda j, qi: (0, V_BLOCK + j))] + [anyspec] * n,
        out_specs=[pl.BlockSpec((t, 128), lambda j, qi: (qi, j)), pl.BlockSpec((2, 1, t), lambda j, qi: (j, 0, qi))]
        + [anyspec] * n,
        out_shape=[jax.ShapeDtypeStruct((s, D_MODEL), bf16), jax.ShapeDtypeStruct((N_HEADS, 1, s), f32)]
        + [jax.ShapeDtypeStruct((N_DEV,) + w.shape, w.dtype) for w in ws],
        scratch_shapes=_exchange_sems(n),
        compiler_params=_cp("arbitrary", "arbitrary"))(qa, ka, qkv, *ws)
    return outs[0], outs[1], outs[2:]


def _fox_bwd(qa, ka, qkv, do, lse, delta, gs):
    s = qkv.shape[0]
    t = min(ATTN_TILE, s)
    nq = s // t
    n = len(gs)

    def body(qa_ref, ka_ref, v_ref, do_ref, lse_ref, dl_ref, *rest):
        g_refs, (dq_ref, dk_ref, dv_ref), r_refs, sems = rest[:n], rest[n:n + 3], rest[n + 3:2 * n + 3], rest[2 * n + 3:]
        ki = pl.program_id(1)
        copies = _exchange_copies(g_refs, r_refs, *sems)

        @pl.when((pl.program_id(0) == 0) & (ki == 0))
        def _():
            _exchange_start(copies)

        @pl.when(ki == 0)
        def _():
            dq_ref[...] = jnp.zeros_like(dq_ref)
        low = lax.broadcasted_iota(jnp.int32, (t, 128), 1) < HEAD_DIM
        row = lax.broadcasted_iota(jnp.int32, (t, t), 0)
        col = lax.broadcasted_iota(jnp.int32, (t, t), 1)
        v = v_ref[...]
        zero = jnp.zeros_like(v)
        vh = (jnp.where(low, v, zero), jnp.where(low, zero, v))

        def tile(qi, carry, diagonal):
            dks, dv = carry
            qoff = pl.multiple_of(qi * t, t)
            dov = do_ref[pl.ds(qoff, t), :]
            doh = (jnp.where(low, dov, zero), jnp.where(low, zero, dov))
            new_dks = []
            for r in range(2):
                qt = qa_ref[r, pl.ds(qoff, t), :]
                sct = _dot(ka_ref[r], qt, NT)
                if diagonal:
                    sct = jnp.where(row <= col, sct, NEG)
                pt = jnp.exp(sct - lse_ref[r, :, pl.ds(qoff, t)])
                dpt = _dot(vh[r], dov, NT)
                dst = (pt * (dpt - dl_ref[r, :, pl.ds(qoff, t)])).astype(bf16)
                dv = dv + _dot(pt.astype(bf16), doh[r])
                new_dks.append(dks[r] + _dot(dst, qt))
                dq_ref[r, pl.ds(qoff, t), :] += _dot(dst, ka_ref[r], TN)
            return tuple(new_dks), dv

        zacc = jnp.zeros((t, 128), f32)
        carry = tile(ki, ((zacc, zacc), zacc), True)
        dks, dv = lax.fori_loop(ki + 1, nq, lambda qi, cr: tile(qi, cr, False), carry)
        dk_ref[0] = dks[0]
        dk_ref[1] = dks[1]
        dv_ref[...] = dv.astype(bf16)

        @pl.when((pl.program_id(0) == N_PAIRS - 1) & (ki == nq - 1))
        def _():
            _exchange_wait(copies)

    anyspec = pl.BlockSpec(memory_space=pl.ANY)
    outs = pl.pallas_call(
        body, name="fox_bwd", grid=(N_PAIRS, nq),
        in_specs=[pl.BlockSpec((2, s, 128), lambda j, ki: (j, 0, 0)),
                  pl.BlockSpec((2, t, 128), lambda j, ki: (j, ki, 0)),
                  pl.BlockSpec((t, 128), lambda j, ki: (ki, V_BLOCK + j)),
                  pl.BlockSpec((s, 128), lambda j, ki: (0, j)),
                  pl.BlockSpec((2, 1, s), lambda j, ki: (j, 0, 0)),
                  pl.BlockSpec((2, 1, s), lambda j, ki: (j, 0, 0))] + [anyspec] * n,
        out_specs=[pl.BlockSpec((2, s, 128), lambda j, ki: (j, 0, 0)),
                   pl.BlockSpec((2, t, 128), lambda j, ki: (j, ki, 0)),
                   pl.BlockSpec((t, 128), lambda j, ki: (ki, j))] + [anyspec] * n,
        out_shape=[jax.ShapeDtypeStruct((N_HEADS, s, 128), f32), jax.ShapeDtypeStruct((N_HEADS, s, 128), f32),
                   jax.ShapeDtypeStruct((s, D_MODEL), bf16)] + [jax.ShapeDtypeStruct(g.shape, g.dtype) for g in gs],
        scratch_shapes=_exchange_sems(n),
        compiler_params=_cp("arbitrary", "arbitrary"))(qa, ka, qkv, do, lse, delta, *gs)
    return outs[0], outs[1], outs[2], outs[3:]


def _fox_bwd_post(dq_hm, dk_hm):
    s = dq_hm.shape[1]
    ts = min(CONV_ROW_TILE, s)

    def body(dq_ref, dk_ref, q_ref, k_ref, dc_ref):
        lane = lax.broadcasted_iota(jnp.int32, (ts, 128), 1)
        dc = jnp.zeros((ts, 128), f32)
        for h in range(N_HEADS):
            hsl = slice(HEAD_DIM * h, HEAD_DIM * (h + 1))
            dqv = dq_ref[h]
            dkv = dk_ref[h]
            q_ref[:, hsl] = (dqv[:, 0:HEAD_DIM] * (HEAD_DIM ** -0.5)).astype(bf16)
            k_ref[:, hsl] = dkv[:, 0:HEAD_DIM].astype(bf16)
            dc = dc + jnp.where(lane == F_LANE + h, dqv[:, AUG:AUG + 1] - dkv[:, AUG + 3:AUG + 4], 0.0)
        dc_ref[...] = dc

    hm = pl.BlockSpec((N_HEADS, ts, 128), lambda i: (0, i, 0))
    return pl.pallas_call(
        body, name="fox_bwd_post", grid=(s // ts,), in_specs=[hm, hm],
        out_specs=[_rowspec(ts, D_MODEL), _rowspec(ts, D_MODEL), _rowspec(ts, 128)],
        out_shape=[jax.ShapeDtypeStruct((s, D_MODEL), bf16), jax.ShapeDtypeStruct((s, D_MODEL), bf16),
                   jax.ShapeDtypeStruct((s, 128), f32)],
        compiler_params=_cp("parallel"))(dq_hm, dk_hm)


def _fox_gate_bwd(dc, dtf, prm, ddt_raw):
    s = dtf.shape[0]
    l = CHUNK
    nb = s // l

    def body(dc_ref, f_ref, prm_ref, ddt_ref, out_ref, dfb_ref, carry_ref):
        @pl.when(pl.program_id(0) == 0)
        def _():
            carry_ref[...] = jnp.zeros_like(carry_ref)
            dfb_ref[...] = jnp.zeros_like(dfb_ref)
        dc = dc_ref[...]
        row = lax.broadcasted_iota(jnp.int32, (l, l), 0)
        col = lax.broadcasted_iota(jnp.int32, (l, l), 1)
        dlf = _dot((row <= col).astype(f32), dc, precision=HIGHEST) + carry_ref[...]
        carry_ref[...] = dlf[0:1, :]
        lane = lax.broadcasted_iota(jnp.int32, (l, 128), 1)
        is_f = (lane >= F_LANE) & (lane < F_LANE + N_HEADS)
        dfr = jnp.where(is_f, dlf * _sigmoid(-(f_ref[...] + prm_ref[3:4, :])), 0.0)
        dfb_ref[...] += jnp.sum(dfr, axis=0, keepdims=True)
        out_ref[...] = ddt_ref[...] + dfr

    def rev(i):
        return (nb - 1 - i, 0)

    return pl.pallas_call(
        body, name="fox_gate_bwd", grid=(nb,),
        in_specs=[pl.BlockSpec((l, 128), rev), pl.BlockSpec((l, 128), rev), pl.BlockSpec((8, 128), lambda i: (0, 0)),
                  pl.BlockSpec((l, 128), rev)],
        out_specs=[pl.BlockSpec((l, 128), rev), pl.BlockSpec((1, 128), lambda i: (0, 0))],
        out_shape=[jax.ShapeDtypeStruct((s, 128), f32), jax.ShapeDtypeStruct((1, 128), f32)],
        scratch_shapes=[pltpu.VMEM((1, 128), f32)],
        compiler_params=_cp("arbitrary"))(dc, dtf, prm, ddt_raw)


def _all_gather(xl, name):
    r, c = xl.shape

    def body(x_ref, out_ref, send_sems, recv_sems, local_sem):
        x, y, cc = _position()
        me, sibling = (x, y, cc), (x, y, 1 - cc)
        chips = [(1 - x, y), (x, 1 - y), (1 - x, 1 - y)]

        def slot(px, py, pc):
            return out_ref.at[4 * px + 2 * py + pc]

        def copy(k, block, to, src=None):
            return pltpu.make_async_remote_copy(
                src_ref=slot(*block) if src is None else src, dst_ref=slot(*block),
                send_sem=send_sems.at[k], recv_sem=recv_sems.at[k],
                device_id=to, device_id_type=pl.DeviceIdType.MESH)

        mine = pltpu.make_async_copy(x_ref, slot(*me), local_sem)
        mine.start()
        first = [copy(0, me, sibling, src=x_ref)]
        first += [copy(1 + j, me, (*chip, cc), src=x_ref) for j, chip in enumerate(chips)]
        for cp in first:
            cp.start()
        passed = [copy(4 + j, (*chip, cc), sibling) for j, chip in enumerate(chips)]
        for j, chip in enumerate(chips):
            copy(1 + j, (*chip, cc), me).wait_recv()
            passed[j].start()
        copy(0, sibling, me).wait_recv()
        for j, chip in enumerate(chips):
            copy(4 + j, (*chip, 1 - cc), me).wait_recv()
        for cp in first + passed:
            cp.wait_send()
        mine.wait()

    return pl.pallas_call(
        body, name=name,
        out_shape=jax.ShapeDtypeStruct((N_DEV, r, c), xl.dtype),
        in_specs=[pl.BlockSpec(memory_space=pl.ANY)], out_specs=pl.BlockSpec(memory_space=pl.ANY),
        scratch_shapes=[pltpu.SemaphoreType.DMA((7,)), pltpu.SemaphoreType.DMA((7,)), pltpu.SemaphoreType.DMA],
    )(xl)


def _sum_parts(parts, name):
    n, r, c = parts.shape

    def body(p_ref, o_ref):
        g = p_ref[0]
        for i in range(1, n):
            g = g + p_ref[i]
        o_ref[...] = g

    return pl.pallas_call(body, name=name, out_shape=jax.ShapeDtypeStruct((r, c), f32))(parts)


def _adamw(w, m, v, parts, name, tr=128, by_columns=False):
    r, c = w.shape
    n = parts.shape[0]
    tr = min(tr, r)
    c1 = 1.0 - ADAM_B1 ** ADAM_STEP
    c2 = 1.0 - ADAM_B2 ** ADAM_STEP

    def body(w_ref, m_ref, v_ref, p_ref, g_ref, d_ref, nm_ref, nv_ref):
        g = p_ref[0].astype(f32)
        for i in range(1, n):
            g = g + p_ref[i].astype(f32)
        g_ref[...] = g
        nm = ADAM_B1 * m_ref[...] + (1.0 - ADAM_B1) * g
        nv = ADAM_B2 * v_ref[...] + (1.0 - ADAM_B2) * (g * g)
        nm_ref[...] = nm
        nv_ref[...] = nv
        d_ref[...] = -ADAM_LR * ((nm / c1) / (jnp.sqrt(nv / c2) + ADAM_EPS) + ADAM_WD * w_ref[...])

    if by_columns:
        blk = pl.BlockSpec((r, 128), lambda i: (0, i))
        pblk = pl.BlockSpec((n, r, 128), lambda i: (0, 0, i))
        steps = c // 128
    else:
        blk = pl.BlockSpec((tr, c), lambda i: (i, 0))
        pblk = pl.BlockSpec((n, tr, c), lambda i: (0, i, 0))
        steps = r // tr
    return pl.pallas_call(
        body, name=name, grid=(steps,),
        in_specs=[blk, blk, blk, pblk],
        out_specs=[blk] * 4, out_shape=[jax.ShapeDtypeStruct((r, c), f32)] * 4,
        compiler_params=_cp("parallel"))(w, m, v, parts)


def _lanes(w):
    return -(-w // 128) * 128


def _pack(arrs):
    rows = []
    for a in arrs:
        k, w = a.shape
        if w % 128:
            a = jnp.pad(a, ((0, 0), (0, _lanes(w) - w)))
        rows.append(a.reshape(-1, 128))
    out = jnp.concatenate(rows, axis=0)
    pad = -out.shape[0] % 8
    return jnp.pad(out, ((0, pad), (0, 0))) if pad else out


def _unpack(packed, shapes):
    outs, off = [], 0
    lead = packed.shape[:-2]
    for k, w in shapes:
        nrow = k * _lanes(w) // 128
        a = packed[..., off:off + nrow, :].reshape(*lead, k, _lanes(w))[..., :w]
        outs.append(a)
        off += nrow
    return outs


def _gathered_cols(a):
    n, k, wl = a.shape
    return jnp.transpose(a, (1, 0, 2)).reshape(k, n * wl)


def _col_shards(a):
    k, w = a.shape
    return jnp.transpose(a.reshape(k, N_DEV, w // N_DEV), (1, 0, 2))


SMALL_PARAMS = (
    ("e_norm_pre", 1, 1024, False), ("e_conv_w", 4, 2048, True), ("e_conv_b", 1, 2048, False),
    ("e_dt_bias", 1, 16, False), ("e_a_log", 1, 16, False), ("e_d_skip", 1, 16, False), ("e_fgate_b", 1, 16, False),
    ("e_ssd_norm", 1, 1024, False), ("e_norm_post", 1, 1024, False), ("o_norm_pre", 1, 1024, True),
    ("o_conv_w", 31, 2048, True), ("o_conv_b", 1, 2048, True), ("o_ln_g", 1, 2048, True), ("o_ln_b", 1, 2048, True),
    ("o_norm_post", 1, 1024, True),
)
BIG_PARAMS = ("e_w_in", "e_w_out", "o_w_in", "o_w_out")
WEIGHT_ORDER = ("e_norm_pre", "e_w_in", "e_conv_w", "e_conv_b", "e_dt_bias", "e_a_log", "e_d_skip", "e_fgate_b",
                "e_ssd_norm", "e_w_out", "e_norm_post", "o_norm_pre", "o_w_in", "o_conv_w", "o_conv_b", "o_ln_g",
                "o_ln_b", "o_w_out", "o_norm_post")
E_IN = 7200
O_IN = 6144


def kernel(x, e_norm_pre, e_w_in, e_conv_w, e_conv_b, e_dt_bias, e_a_log, e_d_skip, e_fgate_b, e_ssd_norm, e_w_out, e_norm_post, o_norm_pre, o_w_in, o_conv_w, o_conv_b, o_ln_g, o_ln_b, o_w_out, o_norm_post, loss_target, m_e_norm_pre, m_e_w_in, m_e_conv_w, m_e_conv_b, m_e_dt_bias, m_e_a_log, m_e_d_skip, m_e_fgate_b, m_e_ssd_norm, m_e_w_out, m_e_norm_post, m_o_norm_pre, m_o_w_in, m_o_conv_w, m_o_conv_b, m_o_ln_g, m_o_ln_b, m_o_w_out, m_o_norm_post, v_e_norm_pre, v_e_w_in, v_e_conv_w, v_e_conv_b, v_e_dt_bias, v_e_a_log, v_e_d_skip, v_e_fgate_b, v_e_ssd_norm, v_e_w_out, v_e_norm_post, v_o_norm_pre, v_o_w_in, v_o_conv_w, v_o_conv_b, v_o_ln_g, v_o_ln_b, v_o_w_out, v_o_norm_post):
    given = dict(locals())
    w_in = {n: given[n] for n in WEIGHT_ORDER}
    m_in = {n: given["m_" + n] for n in WEIGHT_ORDER}
    v_in = {n: given["v_" + n] for n in WEIGHT_ORDER}

    def mat(a):
        return a.reshape(a.shape[-2:])

    xs = mat(x)
    tgt = mat(loss_target)
    xi, yi, ci = _position()
    me = 4 * xi + 2 * yi + ci
    ow = O_IN // N_DEV
    wr = D_CONV // N_DEV

    ew = E_IN // N_DEV
    w_t = _all_gather(jnp.transpose(mat(e_w_in)).astype(bf16), "gather_weights").reshape(E_IN, D_MODEL)
    later_weights = [mat(e_w_out).astype(bf16), mat(o_w_in).astype(bf16), mat(o_w_out).astype(bf16)]
    w_z, w_xbc, w_qkv = w_t[0:2048], w_t[2048:4096], w_t[4112:7184]
    w_dtf = jnp.concatenate([w_t[4096:4112], w_t[7184:7200], jnp.zeros((96, D_MODEL), bf16)], axis=0)

    sharded_small = [(n, k, w) for n, k, w, sh in SMALL_PARAMS if sh]
    sg = _all_gather(_pack([mat(w_in[n]) for n, _, _ in sharded_small]), "gather_small_weights")
    full_small = {n: _gathered_cols(a)
                  for (n, _, _), a in zip(sharded_small, _unpack(sg, [(k, w // N_DEV) for _, k, w in sharded_small]))}
    for n, _, _, sh in SMALL_PARAMS:
        if not sh:
            full_small[n] = mat(w_in[n])
    p = full_small
    prm = jnp.zeros((8, 128), f32)
    prm = prm.at[0, 0:16].set(p["e_dt_bias"][0]).at[1, 0:16].set(p["e_a_log"][0]).at[2, 0:16].set(p["e_d_skip"][0])
    prm = prm.at[3, F_LANE:F_LANE + 16].set(p["e_fgate_b"][0])

    u0 = _rms_fwd(xs, p["e_norm_pre"], "rms_pre0")
    z0 = _mm_nt([(u0, 0, w_z, 0, D_MODEL)], bf16, "proj0_z", tm=1024, tn=1024)
    xraw = _mm_nt([(u0, 0, w_xbc, 0, D_MODEL)], bf16, "proj0_xbc", tm=1024, tn=1024)
    qkv = _mm_nt([(u0, 0, w_qkv, 0, D_MODEL)], bf16, "proj0_qkv", tm=1024, tn=1024)
    dtf = _mm_nt([(u0, 0, w_dtf, 0, D_MODEL)], f32, "proj0_dtf", tm=1024, tn=128)
    pre, act = _conv_ssd_fwd(xraw, p["e_conv_w"], p["e_conv_b"])
    y, hs = _ssd_fwd(act, dtf, prm)
    qa, ka = _fox_prep(qkv, _fox_cumsum(dtf, prm))
    o, lse, (e_w_out_g, o_w_in_g, o_w_out_g) = _fox_fwd(qa, ka, qkv, later_weights)
    e_w_out_f = e_w_out_g.reshape(D_CONV, D_MODEL)
    o_w_in_f = _gathered_cols(o_w_in_g)
    o_w_out_f = o_w_out_g.reshape(D_CONV, D_MODEL)
    cat = _gate0_fwd(y, z0, o, p["e_ssd_norm"])
    out0 = _mm_nn(cat, e_w_out_f, f32, "out0")
    x1, u1 = _post0_pre1(xs, out0, p["e_norm_post"], p["o_norm_pre"])

    proj1 = _mm_nn(u1, o_w_in_f, bf16, "proj1")
    hc = _conv_glu_fwd(proj1, p["o_conv_w"], p["o_conv_b"])
    h3 = _ln_gate_fwd(hc, proj1, p["o_ln_g"], p["o_ln_b"])
    out1 = _mm_nn(h3, o_w_out_f, f32, "out1")
    dy, d_out1, dg_post1, loss_part = _final_loss(x1, out1, tgt, p["o_norm_post"])

    dh3 = _mm_nt([(d_out1, 0, o_w_out_f, 0, D_MODEL)], bf16, "dh3", tm=1024, tn=D_CONV)
    g_o_w_out = _mm_tn(h3, d_out1, "dw_out1")
    dhc, dz1, dg_ln, db_ln = _ln_gate_bwd(hc, proj1, dh3, p["o_ln_g"], p["o_ln_b"])
    dval, dgate, dw_conv1, db_conv1 = _conv_glu_bwd(dhc, proj1, p["o_conv_w"])
    dproj1 = jnp.concatenate([dval, dgate, dz1], axis=1)
    du1 = _mm_nt([(dproj1, 0, o_w_in_f, 0, O_IN)], f32, "du1")
    g_o_w_in = _mm_tn(u1, dproj1, "dw_in1", tn=ow, blocked=True)
    dx1, d_out0, dg_pre1, dg_post0 = _mid_bwd(x1, du1, dy, out0, p["o_norm_pre"], p["e_norm_post"])

    dcat = _mm_nt([(d_out0, 0, e_w_out_f, 0, D_MODEL)], bf16, "dcat", tm=1024, tn=D_CONV)
    g_e_w_out = _mm_tn(cat, d_out0, "dw_out0")
    dy_ssd, do, dz0, delta, dg_ssd_norm = _gate0_bwd(y, z0, o, dcat, p["e_ssd_norm"])
    early = [g_e_w_out.reshape(N_DEV, wr, D_MODEL).astype(bf16), g_o_w_in.astype(bf16),
             g_o_w_out.reshape(N_DEV, wr, D_MODEL).astype(bf16)]
    dq_hm, dk_hm, dv, early_parts = _fox_bwd(qa, ka, qkv, do, lse, delta[0:N_HEADS].reshape(N_HEADS, 1, -1), early)
    dq, dk, dc = _fox_bwd_post(dq_hm, dk_hm)
    dpre, ddt_raw, dprm = _ssd_bwd(act, pre, dtf, prm, hs, dy_ssd)
    ddtf, dfb = _fox_gate_bwd(dc, dtf, prm, ddt_raw)
    dxraw, dw_conv0, db_conv0 = _conv_ssd_bwd(dpre, xraw, p["e_conv_w"])
    gw_dtf = _mm_tn(ddtf, u0, "dw_in0_dtf")
    g_e_w_in_t = jnp.concatenate([
        _mm_tn(dz0, u0, "dw_in0_z"), _mm_tn(dxraw, u0, "dw_in0_xbc"), gw_dtf[0:16],
        _mm_tn(dq, u0, "dw_in0_q"), _mm_tn(dk, u0, "dw_in0_k"), _mm_tn(dv, u0, "dw_in0_v"), gw_dtf[16:32]], axis=0)
    du0, last_parts = _mm_nt(
        [(dz0, 0, w_z, 0, 2048), (dxraw, 0, w_xbc, 0, 2048), (dq, 0, w_qkv, 0, 1024), (dk, 0, w_qkv, 1, 1024),
         (dv, 0, w_qkv, 2, 1024), (ddtf, 0, w_dtf, 0, 128)], f32, "du0", b_kn=True,
        gs=[g_e_w_in_t.astype(bf16).reshape(N_DEV, ew, D_MODEL)])
    grad_x, dg_pre0 = _first_bwd(xs, du0, dx1, p["e_norm_pre"])

    outs = {"e_w_in": tuple(jnp.transpose(r) for r in _adamw(
        jnp.transpose(mat(e_w_in)), jnp.transpose(mat(m_e_w_in)), jnp.transpose(mat(v_e_w_in)), last_parts[0],
        "adamw_e_w_in", by_columns=True))}
    for n, parts in zip(BIG_PARAMS[1:], early_parts):
        outs[n] = _adamw(mat(w_in[n]), mat(m_in[n]), mat(v_in[n]), parts, "adamw_" + n)

    small_grads = {
        "e_norm_pre": dg_pre0, "e_conv_w": dw_conv0, "e_conv_b": db_conv0, "e_dt_bias": dprm[0:1, 0:16],
        "e_a_log": dprm[1:2, 0:16], "e_d_skip": dprm[2:3, 0:16], "e_fgate_b": dfb[:, F_LANE:F_LANE + 16],
        "e_ssd_norm": dg_ssd_norm, "e_norm_post": dg_post0, "o_norm_pre": dg_pre1, "o_conv_w": dw_conv1,
        "o_conv_b": db_conv1, "o_ln_g": dg_ln, "o_ln_b": db_ln, "o_norm_post": dg_post1,
    }
    gathered = _all_gather(_pack([small_grads[n] for n, _, _, _ in SMALL_PARAMS] + [loss_part]), "gather_small_grads")
    summed = _unpack(_sum_parts(gathered, "sum_small_grads"), [(k, w) for _, k, w, _ in SMALL_PARAMS] + [(1, 128)])
    loss = summed[-1][0, 0]
    g_local = []
    for (n, k, w, sh), g in zip(SMALL_PARAMS, summed):
        g_local.append(lax.dynamic_slice_in_dim(g, me * (w // N_DEV), w // N_DEV, axis=1) if sh else g)
    names = [n for n, _, _, _ in SMALL_PARAMS]
    local_shapes = [(k, w // N_DEV if sh else w) for _, k, w, sh in SMALL_PARAMS]
    res = _adamw(_pack([mat(w_in[n]) for n in names]), _pack([mat(m_in[n]) for n in names]),
                 _pack([mat(v_in[n]) for n in names]), _pack(g_local)[None], "adamw_small", tr=8)
    unpacked = [_unpack(r, local_shapes) for r in res]
    for i, n in enumerate(names):
        outs[n] = tuple(u[i] for u in unpacked)

    ret = [loss, grad_x.reshape(x.shape)]
    for j in range(4):
        ret += [outs[n][j].reshape(w_in[n].shape) for n in WEIGHT_ORDER]
    return tuple(ret)
```

```python
import jax
import jax.numpy as jnp
from jax import lax
from jax.experimental import pallas as pl
from jax.experimental.pallas import tpu as pltpu

f32 = jnp.float32
bf16 = jnp.bfloat16

N_DEV = 8
D_MODEL = 1024
N_HEADS = 16
HEAD_DIM = 64
N_GROUPS = 4
HEADS_PER_GROUP = 4
D_STATE = 128
CHUNK = 512
SSD_CONV = 4
CONV_WIDTH = 31
D_CONV = 2048
EPS = 1e-6
XBC_W = 2048
B_OFF = 1024
C_OFF = 1536
F_LANE = 16
HALO = 32

ADAM_LR = 0.001
ADAM_B1 = 0.9
ADAM_B2 = 0.999
ADAM_EPS = 1e-08
ADAM_WD = 0.01
ADAM_STEP = 10

VMEM_LIMIT_BYTES = 56 * 1024 * 1024
ROW_TILE = 512
CONV_ROW_TILE = 512
CONV_COL_TILE = 512
CONV_SUB = 32
ATTN_TILE = 1024
ATTN_FWD_TILE = 1024

NT = (((1,), (1,)), ((), ()))
TN = (((0,), (0,)), ((), ()))
HIGHEST = lax.Precision.HIGHEST
NEG = -1e30


def _cp(*sem):
    return pltpu.CompilerParams(dimension_semantics=sem if sem else None, vmem_limit_bytes=VMEM_LIMIT_BYTES)


def _sigmoid(x):
    return jax.nn.sigmoid(x)


def _silu(x):
    return x * _sigmoid(x)


def _dsilu(x):
    s = _sigmoid(x)
    return s * (1.0 + x * (1.0 - s))


def _softplus(x):
    return jnp.maximum(x, 0.0) + jnp.log(1.0 + jnp.exp(-jnp.abs(x)))


def _log_sigmoid(x):
    return jnp.minimum(x, 0.0) - jnp.log(1.0 + jnp.exp(-jnp.abs(x)))


def _dot(a, b, dims=None, precision=None):
    if dims is None:
        return jnp.dot(a, b, preferred_element_type=f32, precision=precision)
    return lax.dot_general(a, b, dims, preferred_element_type=f32, precision=precision)


def _mm_nn(a, b, out_dtype, name, tm=1024, tn=1024):
    m, k = a.shape
    n = b.shape[1]
    tm, tn = min(tm, m), min(tn, n)

    def body(a_ref, b_ref, o_ref):
        o_ref[...] = _dot(a_ref[...], b_ref[...]).astype(o_ref.dtype)

    return pl.pallas_call(
        body, name=name, grid=(n // tn, m // tm),
        in_specs=[pl.BlockSpec((tm, k), lambda j, i: (i, 0)), pl.BlockSpec((k, tn), lambda j, i: (0, j))],
        out_specs=pl.BlockSpec((tm, tn), lambda j, i: (i, j)),
        out_shape=jax.ShapeDtypeStruct((m, n), out_dtype), compiler_params=_cp("parallel", "parallel"))(a, b)


def _mm_nt(pairs, out_dtype, name, tm=512, tn=512, gs=(), b_kn=False):
    m = pairs[0][0].shape[0]
    n = pairs[0][2].shape[1] if b_kn else pairs[0][2].shape[0]
    tm, tn = min(tm, m), min(tn, n)
    npair = len(pairs)
    ng = len(gs)
    grid = (n // tn, m // tm)

    def body(*refs):
        g_refs = refs[2 * npair:2 * npair + ng]
        o_ref = refs[2 * npair + ng]
        r_refs = refs[2 * npair + ng + 1:2 * npair + 2 * ng + 1]
        sems = refs[2 * npair + 2 * ng + 1:]
        if ng:
            copies = _exchange_copies(g_refs, r_refs, *sems)

            @pl.when((pl.program_id(0) == 0) & (pl.program_id(1) == 0))
            def _():
                _exchange_start(copies)
        acc = None
        for p in range(npair):
            d = _dot(refs[2 * p][...].astype(bf16), refs[2 * p + 1][...], None if b_kn else NT)
            acc = d if acc is None else acc + d
        o_ref[...] = acc.astype(o_ref.dtype)
        if ng:
            @pl.when((pl.program_id(0) == grid[0] - 1) & (pl.program_id(1) == grid[1] - 1))
            def _():
                _exchange_wait(copies)

    in_specs, args = [], []
    for a, acb, b, bcb, k in pairs:
        in_specs.append(pl.BlockSpec((tm, k), lambda j, i, acb=acb: (i, acb)))
        if b_kn:
            in_specs.append(pl.BlockSpec((k, tn), lambda j, i, bcb=bcb: (bcb, j)))
        else:
            in_specs.append(pl.BlockSpec((tn, k), lambda j, i, bcb=bcb: (j, bcb)))
        args += [a, b]
    anyspec = pl.BlockSpec(memory_space=pl.ANY)
    outs = pl.pallas_call(
        body, name=name, grid=grid, in_specs=in_specs + [anyspec] * ng,
        out_specs=[pl.BlockSpec((tm, tn), lambda j, i: (i, j))] + [anyspec] * ng,
        out_shape=[jax.ShapeDtypeStruct((m, n), out_dtype)] + [jax.ShapeDtypeStruct(g.shape, g.dtype) for g in gs],
        scratch_shapes=_exchange_sems(ng) if ng else [],
        compiler_params=_cp("arbitrary", "arbitrary") if ng else _cp("parallel", "parallel"))(*args, *gs)
    return (outs[0], outs[1:]) if ng else outs[0]


def _mm_tn(a, b, name, a_cb=0, am=None, b_cb=0, bn=None, tn=1024, tk=1024, blocked=False):
    k = a.shape[0]
    am = a.shape[1] if am is None else am
    bn = b.shape[1] if bn is None else bn
    tm = min(1024, am)
    tn, tk = min(tn, bn), min(tk, k)
    a_off, b_off = a_cb * (am // tm), b_cb * (bn // tn)

    def body(a_ref, b_ref, o_ref):
        @pl.when(pl.program_id(2) == 0)
        def _():
            o_ref[...] = jnp.zeros_like(o_ref)
        d = _dot(a_ref[...].astype(bf16), b_ref[...].astype(bf16), TN)
        o_ref[...] += d.reshape(o_ref.shape)

    if blocked:
        out_spec = pl.BlockSpec((1, tm, tn), lambda i, j, kk: (j, i, 0))
        out_shape = jax.ShapeDtypeStruct((bn // tn, am, tn), f32)
    else:
        out_spec = pl.BlockSpec((tm, tn), lambda i, j, kk: (i, j))
        out_shape = jax.ShapeDtypeStruct((am, bn), f32)
    return pl.pallas_call(
        body, name=name, grid=(am // tm, bn // tn, k // tk),
        in_specs=[pl.BlockSpec((tk, tm), lambda i, j, kk: (kk, a_off + i)),
                  pl.BlockSpec((tk, tn), lambda i, j, kk: (kk, b_off + j))],
        out_specs=out_spec, out_shape=out_shape,
        compiler_params=_cp("parallel", "parallel", "arbitrary"))(a, b)


def _rowspec(ts, w, cb=0):
    return pl.BlockSpec((ts, w), lambda i: (i, cb))


def _vecspec(w):
    return pl.BlockSpec((1, w), lambda i: (0, 0))


def _rms_fwd(x, g, name):
    s, d = x.shape
    ts = min(ROW_TILE, s)

    def body(x_ref, g_ref, u_ref):
        xv = x_ref[...]
        r = lax.rsqrt(jnp.mean(xv * xv, axis=-1, keepdims=True) + EPS)
        u_ref[...] = (xv * r * g_ref[...]).astype(bf16)

    return pl.pallas_call(
        body, name=name, grid=(s // ts,), in_specs=[_rowspec(ts, d), _vecspec(d)], out_specs=_rowspec(ts, d),
        out_shape=jax.ShapeDtypeStruct((s, d), bf16), compiler_params=_cp("parallel"))(x, g)


def _rms_bwd_vals(xv, g, dy):
    r = lax.rsqrt(jnp.mean(xv * xv, axis=-1, keepdims=True) + EPS)
    xh = xv * r
    dg = jnp.sum(dy * xh, axis=0, keepdims=True)
    dxh = dy * g
    dx = r * (dxh - xh * jnp.mean(dxh * xh, axis=-1, keepdims=True))
    return dx, dg


def _gate0_fwd(y, z, o, ssd_norm):
    s = y.shape[0]
    ts = min(ROW_TILE, s)
    gw = D_MODEL // N_GROUPS

    def body(y_ref, zs_ref, zf_ref, o_ref, w_ref, cat_ref):
        yg = y_ref[...].astype(f32) * _silu(zs_ref[...].astype(f32))
        for g in range(N_GROUPS):
            seg = yg[:, gw * g:gw * (g + 1)]
            r = lax.rsqrt(jnp.mean(seg * seg, axis=-1, keepdims=True) + EPS)
            cat_ref[:, gw * g:gw * (g + 1)] = (seg * r * w_ref[:, gw * g:gw * (g + 1)]).astype(bf16)
        cat_ref[:, D_MODEL:] = (o_ref[...].astype(f32) * _silu(zf_ref[...].astype(f32))).astype(bf16)

    return pl.pallas_call(
        body, name="gate0_fwd", grid=(s // ts,),
        in_specs=[_rowspec(ts, D_MODEL), _rowspec(ts, D_MODEL, 0), _rowspec(ts, D_MODEL, 1), _rowspec(ts, D_MODEL),
                  _vecspec(D_MODEL)],
        out_specs=_rowspec(ts, 2 * D_MODEL),
        out_shape=jax.ShapeDtypeStruct((s, 2 * D_MODEL), bf16), compiler_params=_cp("parallel"))(y, z, z, o, ssd_norm)


def _post0_pre1(x, out0, g_post0, g_pre1):
    s, d = x.shape
    ts = min(ROW_TILE, s)

    def body(x_ref, o_ref, gp_ref, gn_ref, x1_ref, u1_ref):
        ov = o_ref[...]
        r = lax.rsqrt(jnp.mean(ov * ov, axis=-1, keepdims=True) + EPS)
        x1 = x_ref[...] + ov * r * gp_ref[...]
        x1_ref[...] = x1
        r1 = lax.rsqrt(jnp.mean(x1 * x1, axis=-1, keepdims=True) + EPS)
        u1_ref[...] = (x1 * r1 * gn_ref[...]).astype(bf16)

    return pl.pallas_call(
        body, name="post0_pre1", grid=(s // ts,),
        in_specs=[_rowspec(ts, d), _rowspec(ts, d), _vecspec(d), _vecspec(d)],
        out_specs=[_rowspec(ts, d), _rowspec(ts, d)],
        out_shape=[jax.ShapeDtypeStruct((s, d), f32), jax.ShapeDtypeStruct((s, d), bf16)],
        compiler_params=_cp("parallel"))(x, out0, g_post0, g_pre1)


def _ln_vals(hc, g, b):
    mu = jnp.mean(hc, axis=-1, keepdims=True)
    xc = hc - mu
    rstd = lax.rsqrt(jnp.mean(xc * xc, axis=-1, keepdims=True) + EPS)
    xh = xc * rstd
    return xh, rstd, xh * g + b


def _ln_gate_fwd(hc, proj1, ln_g, ln_b):
    s = hc.shape[0]
    ts = min(ROW_TILE, s)

    def body(hc_ref, z_ref, g_ref, b_ref, h3_ref):
        _, _, ln = _ln_vals(hc_ref[...].astype(f32), g_ref[...], b_ref[...])
        h3_ref[...] = (_silu(ln) * _silu(z_ref[...].astype(f32))).astype(bf16)

    return pl.pallas_call(
        body, name="ln_gate_fwd", grid=(s // ts,),
        in_specs=[_rowspec(ts, D_CONV), _rowspec(ts, D_CONV, 2), _vecspec(D_CONV), _vecspec(D_CONV)],
        out_specs=_rowspec(ts, D_CONV),
        out_shape=jax.ShapeDtypeStruct((s, D_CONV), bf16), compiler_params=_cp("parallel"))(hc, proj1, ln_g, ln_b)


def _final_loss(x1, out1, tgt, g_post1):
    s, d = x1.shape
    ts = min(ROW_TILE, s)

    def body(x1_ref, o_ref, t_ref, g_ref, dy_ref, do_ref, dg_ref, loss_ref):
        i = pl.program_id(0)

        @pl.when(i == 0)
        def _():
            dg_ref[...] = jnp.zeros_like(dg_ref)
            loss_ref[...] = jnp.zeros_like(loss_ref)
        ov = o_ref[...]
        g = g_ref[...]
        r = lax.rsqrt(jnp.mean(ov * ov, axis=-1, keepdims=True) + EPS)
        diff = x1_ref[...] + ov * r * g - t_ref[...]
        row = jnp.mean(diff * diff, axis=-1, keepdims=True)
        loss_ref[...] += jnp.broadcast_to(0.5 * jnp.sum(row, axis=0, keepdims=True), loss_ref.shape)
        dy = diff * (1.0 / d)
        dy_ref[...] = dy
        dx, dg = _rms_bwd_vals(ov, g, dy)
        do_ref[...] = dx.astype(bf16)
        dg_ref[...] += dg

    return pl.pallas_call(
        body, name="final_loss", grid=(s // ts,),
        in_specs=[_rowspec(ts, d), _rowspec(ts, d), _rowspec(ts, d), _vecspec(d)],
        out_specs=[_rowspec(ts, d), _rowspec(ts, d), _vecspec(d), _vecspec(128)],
        out_shape=[jax.ShapeDtypeStruct((s, d), f32), jax.ShapeDtypeStruct((s, d), bf16),
                   jax.ShapeDtypeStruct((1, d), f32), jax.ShapeDtypeStruct((1, 128), f32)],
        compiler_params=_cp("arbitrary"))(x1, out1, tgt, g_post1)


def _ln_gate_bwd(hc, proj1, dh3, ln_g, ln_b):
    s = hc.shape[0]
    ts = min(ROW_TILE, s)

    def body(hc_ref, z_ref, dh_ref, g_ref, b_ref, dhc_ref, dz_ref, dg_ref, db_ref):
        @pl.when(pl.program_id(0) == 0)
        def _():
            dg_ref[...] = jnp.zeros_like(dg_ref)
            db_ref[...] = jnp.zeros_like(db_ref)
        g = g_ref[...]
        xh, rstd, ln = _ln_vals(hc_ref[...].astype(f32), g, b_ref[...])
        zv = z_ref[...].astype(f32)
        dh3 = dh_ref[...].astype(f32)
        dz_ref[...] = (dh3 * _silu(ln) * _dsilu(zv)).astype(bf16)
        dln = dh3 * _silu(zv) * _dsilu(ln)
        dg_ref[...] += jnp.sum(dln * xh, axis=0, keepdims=True)
        db_ref[...] += jnp.sum(dln, axis=0, keepdims=True)
        dxh = dln * g
        dhc = rstd * (dxh - jnp.mean(dxh, axis=-1, keepdims=True) - xh * jnp.mean(dxh * xh, axis=-1, keepdims=True))
        dhc_ref[...] = dhc.astype(bf16)

    return pl.pallas_call(
        body, name="ln_gate_bwd", grid=(s // ts,),
        in_specs=[_rowspec(ts, D_CONV), _rowspec(ts, D_CONV, 2), _rowspec(ts, D_CONV), _vecspec(D_CONV),
                  _vecspec(D_CONV)],
        out_specs=[_rowspec(ts, D_CONV), _rowspec(ts, D_CONV), _vecspec(D_CONV), _vecspec(D_CONV)],
        out_shape=[jax.ShapeDtypeStruct((s, D_CONV), bf16), jax.ShapeDtypeStruct((s, D_CONV), bf16),
                   jax.ShapeDtypeStruct((1, D_CONV), f32), jax.ShapeDtypeStruct((1, D_CONV), f32)],
        compiler_params=_cp("arbitrary"))(hc, proj1, dh3, ln_g, ln_b)


def _mid_bwd(x1, du1, dy, out0, g_pre1, g_post0):
    s, d = x1.shape
    ts = min(ROW_TILE, s)

    def body(x1_ref, du_ref, dy_ref, o_ref, gn_ref, gp_ref, dx1_ref, do_ref, dgn_ref, dgp_ref):
        @pl.when(pl.program_id(0) == 0)
        def _():
            dgn_ref[...] = jnp.zeros_like(dgn_ref)
            dgp_ref[...] = jnp.zeros_like(dgp_ref)
        dxa, dgn = _rms_bwd_vals(x1_ref[...], gn_ref[...], du_ref[...])
        dx1 = dy_ref[...] + dxa
        dx1_ref[...] = dx1
        dgn_ref[...] += dgn
        dxo, dgp = _rms_bwd_vals(o_ref[...], gp_ref[...], dx1)
        do_ref[...] = dxo.astype(bf16)
        dgp_ref[...] += dgp

    return pl.pallas_call(
        body, name="mid_bwd", grid=(s // ts,),
        in_specs=[_rowspec(ts, d)] * 4 + [_vecspec(d), _vecspec(d)],
        out_specs=[_rowspec(ts, d), _rowspec(ts, d), _vecspec(d), _vecspec(d)],
        out_shape=[jax.ShapeDtypeStruct((s, d), f32), jax.ShapeDtypeStruct((s, d), bf16),
                   jax.ShapeDtypeStruct((1, d), f32), jax.ShapeDtypeStruct((1, d), f32)],
        compiler_params=_cp("arbitrary"))(x1, du1, dy, out0, g_pre1, g_post0)


def _first_bwd(x, du0, dx1, g_pre0):
    s, d = x.shape
    ts = min(ROW_TILE, s)

    def body(x_ref, du_ref, dx1_ref, g_ref, dx_ref, dg_ref):
        @pl.when(pl.program_id(0) == 0)
        def _():
            dg_ref[...] = jnp.zeros_like(dg_ref)
        dxa, dg = _rms_bwd_vals(x_ref[...], g_ref[...], du_ref[...])
        dx_ref[...] = dx1_ref[...] + dxa
        dg_ref[...] += dg

    return pl.pallas_call(
        body, name="first_bwd", grid=(s // ts,),
        in_specs=[_rowspec(ts, d)] * 3 + [_vecspec(d)],
        out_specs=[_rowspec(ts, d), _vecspec(d)],
        out_shape=[jax.ShapeDtypeStruct((s, d), f32), jax.ShapeDtypeStruct((1, d), f32)],
        compiler_params=_cp("arbitrary"))(x, du0, dx1, g_pre0)


def _gate0_bwd(y, z, o, dcat, ssd_norm):
    s = y.shape[0]
    ts = min(ROW_TILE, s)
    gw = D_MODEL // N_GROUPS

    def body(y_ref, zs_ref, zf_ref, o_ref, dn_ref, dg_ref, w_ref, dy_ref, do_ref, dz_ref, delta_ref, dw_ref):
        @pl.when(pl.program_id(0) == 0)
        def _():
            dw_ref[...] = jnp.zeros_like(dw_ref)
        yv = y_ref[...].astype(f32)
        zs = zs_ref[...].astype(f32)
        sz = _silu(zs)
        yg = yv * sz
        dyn = dn_ref[...].astype(f32)
        for g in range(N_GROUPS):
            sl = slice(gw * g, gw * (g + 1))
            seg = yg[:, sl]
            r = lax.rsqrt(jnp.mean(seg * seg, axis=-1, keepdims=True) + EPS)
            yh = seg * r
            dn = dyn[:, sl]
            dw_ref[:, sl] += jnp.sum(dn * yh, axis=0, keepdims=True)
            dyh = dn * w_ref[:, sl]
            dyg = r * (dyh - yh * jnp.mean(dyh * yh, axis=-1, keepdims=True))
            dy_ref[:, sl] = (dyg * sz[:, sl]).astype(bf16)
            dz_ref[:, sl] = (dyg * yv[:, sl] * _dsilu(zs[:, sl])).astype(bf16)
        zf = zf_ref[...].astype(f32)
        ov = o_ref[...].astype(f32)
        dog = dg_ref[...].astype(f32)
        dov = (dog * _silu(zf)).astype(bf16)
        do_ref[...] = dov
        dz_ref[:, D_MODEL:] = (dog * ov * _dsilu(zf)).astype(bf16)
        prod = dov.astype(f32) * ov
        lane = lax.broadcasted_iota(jnp.int32, (ts, 128), 1)
        delta = jnp.zeros((ts, 128), f32)
        for h in range(N_HEADS):
            dh = jnp.sum(prod[:, HEAD_DIM * h:HEAD_DIM * (h + 1)], axis=-1, keepdims=True)
            delta = delta + jnp.where(lane == h, dh, 0.0)
        delta_ref[...] = delta.T

    return pl.pallas_call(
        body, name="gate0_bwd", grid=(s // ts,),
        in_specs=[_rowspec(ts, D_MODEL), _rowspec(ts, D_MODEL, 0), _rowspec(ts, D_MODEL, 1), _rowspec(ts, D_MODEL),
                  _rowspec(ts, D_MODEL, 0), _rowspec(ts, D_MODEL, 1), _vecspec(D_MODEL)],
        out_specs=[_rowspec(ts, D_MODEL), _rowspec(ts, D_MODEL), _rowspec(ts, 2 * D_MODEL),
                   pl.BlockSpec((128, ts), lambda i: (0, i)), _vecspec(D_MODEL)],
        out_shape=[jax.ShapeDtypeStruct((s, D_MODEL), bf16), jax.ShapeDtypeStruct((s, D_MODEL), bf16),
                   jax.ShapeDtypeStruct((s, 2 * D_MODEL), bf16), jax.ShapeDtypeStruct((128, s), f32),
                   jax.ShapeDtypeStruct((1, D_MODEL), f32)],
        compiler_params=_cp("arbitrary"))(y, z, z, o, dcat, dcat, ssd_norm)


def _conv_grid(s, c):
    ts, cb = min(CONV_ROW_TILE, s), min(CONV_COL_TILE, c)
    return ts, cb, (c // cb, s // ts)


def _cur(ts, cb, off=0):
    return pl.BlockSpec((ts, cb), lambda c, i: (i, c + off))


def _prev_halo(ts, cb, off=0):
    return pl.BlockSpec((HALO, cb), lambda c, i: (jnp.maximum(i * (ts // HALO) - 1, 0), c + off))


def _next_halo(ts, cb, s, off=0):
    return pl.BlockSpec((HALO, cb), lambda c, i: (jnp.minimum((i + 1) * (ts // HALO), s // HALO - 1), c + off))


def _wspec(k, cb):
    return pl.BlockSpec((k, cb), lambda c, i: (0, c))


def _phases(offsets):
    return sorted({o % 8 for o in offsets} - {0})


def _shift_scratch(offsets, ts, cb):
    return pltpu.VMEM((max(len(_phases(offsets)), 1), ts + HALO - 8, cb), f32)


def _fill_phases(ext_ref, sh_ref, offsets, ts):
    for j, r in enumerate(_phases(offsets)):
        sh_ref[j] = ext_ref[pl.ds(r, ts + HALO - 8), :]


def _slab(ext_ref, sh_ref, offsets, off, start):
    r = off % 8
    a = off - r + start
    if r == 0:
        return ext_ref[a:a + CONV_SUB, :]
    return sh_ref[_phases(offsets).index(r), a:a + CONV_SUB, :]


def _conv_taps(ext_ref, sh_ref, w_ref, b_ref, ts, k_taps, emit):
    offsets = [HALO - (k_taps - 1) + k for k in range(k_taps)]
    _fill_phases(ext_ref, sh_ref, offsets, ts)
    for sb in range(ts // CONV_SUB):
        acc = b_ref[...]
        for k in range(k_taps):
            acc = acc + w_ref[k:k + 1, :] * _slab(ext_ref, sh_ref, offsets, offsets[k], sb * CONV_SUB)
        emit(slice(sb * CONV_SUB, (sb + 1) * CONV_SUB), acc)


def _conv_ssd_fwd(xraw, w, b):
    s, c = xraw.shape
    ts, cb, grid = _conv_grid(s, c)
    offsets = [HALO - (SSD_CONV - 1) + k for k in range(SSD_CONV)]

    def body(x_ref, xh_ref, w_ref, b_ref, pre_ref, act_ref, ext_ref, sh_ref):
        first = pl.program_id(1) == 0
        ext_ref[0:HALO, :] = jnp.where(first, 0.0, xh_ref[...].astype(f32))
        ext_ref[HALO:, :] = x_ref[...].astype(f32)

        def emit(rows, pre):
            pre_ref[rows, :] = pre.astype(bf16)
            act_ref[rows, :] = _silu(pre).astype(bf16)
        _conv_taps(ext_ref, sh_ref, w_ref, b_ref, ts, SSD_CONV, emit)

    return pl.pallas_call(
        body, name="conv_ssd_fwd", grid=grid,
        in_specs=[_cur(ts, cb), _prev_halo(ts, cb), _wspec(SSD_CONV, cb), _wspec(1, cb)],
        out_specs=[_cur(ts, cb), _cur(ts, cb)],
        out_shape=[jax.ShapeDtypeStruct((s, c), bf16)] * 2,
        scratch_shapes=[pltpu.VMEM((HALO + ts, cb), f32), _shift_scratch(offsets, ts, cb)],
        compiler_params=_cp("parallel", "parallel"))(xraw, xraw, w, b)


def _conv_glu_fwd(proj1, w, b):
    s = proj1.shape[0]
    c = D_CONV
    ts, cb, grid = _conv_grid(s, c)
    goff = c // cb

    offsets = [HALO - (CONV_WIDTH - 1) + k for k in range(CONV_WIDTH)]

    def body(v_ref, g_ref, vh_ref, gh_ref, w_ref, b_ref, hc_ref, ext_ref, sh_ref):
        first = pl.program_id(1) == 0
        hh = vh_ref[...].astype(f32) * _sigmoid(gh_ref[...].astype(f32))
        ext_ref[0:HALO, :] = jnp.where(first, 0.0, hh)
        ext_ref[HALO:, :] = v_ref[...].astype(f32) * _sigmoid(g_ref[...].astype(f32))

        def emit(rows, hc):
            hc_ref[rows, :] = hc.astype(bf16)
        _conv_taps(ext_ref, sh_ref, w_ref, b_ref, ts, CONV_WIDTH, emit)

    return pl.pallas_call(
        body, name="conv_glu_fwd", grid=grid,
        in_specs=[_cur(ts, cb), _cur(ts, cb, goff), _prev_halo(ts, cb), _prev_halo(ts, cb, goff),
                  _wspec(CONV_WIDTH, cb), _wspec(1, cb)],
        out_specs=_cur(ts, cb),
        out_shape=jax.ShapeDtypeStruct((s, c), bf16),
        scratch_shapes=[pltpu.VMEM((HALO + ts, cb), f32), _shift_scratch(offsets, ts, cb)],
        compiler_params=_cp("parallel", "parallel"))(proj1, proj1, proj1, proj1, w, b)


def _conv_bwd_offsets(k_taps):
    return [k_taps - 1 - k for k in range(k_taps)], [HALO - (k_taps - 1) + k for k in range(k_taps)]


def _conv_bwd_scratch(k_taps, ts, cb):
    d_offs, x_offs = _conv_bwd_offsets(k_taps)
    return [pltpu.VMEM((ts + HALO, cb), f32), _shift_scratch(d_offs, ts, cb),
            pltpu.VMEM((HALO + ts, cb), f32), _shift_scratch(x_offs, ts, cb),
            pltpu.VMEM((k_taps, 8, cb), f32), pltpu.VMEM((8, cb), f32)]


def _conv_bwd_core(dp, dpn_ref, last, w_ref, scratch, dw_ref, db_ref, ts, k_taps, emit):
    dext_ref, dsh_ref, xext_ref, xsh_ref, dw8_ref, db8_ref = scratch
    d_offs, x_offs = _conv_bwd_offsets(k_taps)
    dext_ref[0:ts, :] = dp
    dext_ref[ts:, :] = jnp.where(last, 0.0, dpn_ref[...].astype(f32))
    _fill_phases(dext_ref, dsh_ref, d_offs, ts)
    _fill_phases(xext_ref, xsh_ref, x_offs, ts)

    @pl.when(pl.program_id(1) == 0)
    def _():
        dw8_ref[...] = jnp.zeros_like(dw8_ref)
        db8_ref[...] = jnp.zeros_like(db8_ref)
    cb = dp.shape[1]
    for sb in range(ts // CONV_SUB):
        start = sb * CONV_SUB
        dpv = dext_ref[start:start + CONV_SUB, :]
        dx = None
        for k in range(k_taps):
            t = w_ref[k:k + 1, :] * _slab(dext_ref, dsh_ref, d_offs, d_offs[k], start)
            dx = t if dx is None else dx + t
            prod = dpv * _slab(xext_ref, xsh_ref, x_offs, x_offs[k], start)
            dw8_ref[k] += jnp.sum(prod.reshape(CONV_SUB // 8, 8, cb), axis=0)
        db8_ref[...] += jnp.sum(dpv.reshape(CONV_SUB // 8, 8, cb), axis=0)
        emit(slice(start, start + CONV_SUB), dx)

    @pl.when(last)
    def _():
        dw_ref[...] = jnp.sum(dw8_ref[...], axis=1)
        db_ref[...] = jnp.sum(db8_ref[...], axis=0, keepdims=True)


def _conv_ssd_bwd(dpre, xraw, w):
    s, c = xraw.shape
    ts, cb, grid = _conv_grid(s, c)
    nb = s // ts

    def body(dp_ref, dpn_ref, x_ref, xh_ref, w_ref, dx_ref, dw_ref, db_ref, *scratch):
        i = pl.program_id(1)
        xext_ref = scratch[2]
        xext_ref[0:HALO, :] = jnp.where(i == 0, 0.0, xh_ref[...].astype(f32))
        xext_ref[HALO:, :] = x_ref[...].astype(f32)

        def emit(rows, dx):
            dx_ref[rows, :] = dx.astype(bf16)
        _conv_bwd_core(dp_ref[...].astype(f32), dpn_ref, i == nb - 1, w_ref, scratch, dw_ref, db_ref, ts, SSD_CONV, emit)

    return pl.pallas_call(
        body, name="conv_ssd_bwd", grid=grid,
        in_specs=[_cur(ts, cb), _next_halo(ts, cb, s), _cur(ts, cb), _prev_halo(ts, cb), _wspec(SSD_CONV, cb)],
        out_specs=[_cur(ts, cb), _wspec(SSD_CONV, cb), _wspec(1, cb)],
        out_shape=[jax.ShapeDtypeStruct((s, c), bf16), jax.ShapeDtypeStruct((SSD_CONV, c), f32),
                   jax.ShapeDtypeStruct((1, c), f32)],
        scratch_shapes=_conv_bwd_scratch(SSD_CONV, ts, cb),
        compiler_params=_cp("parallel", "arbitrary"))(dpre, dpre, xraw, xraw, w)


def _conv_glu_bwd(dhc, proj1, w):
    s = proj1.shape[0]
    c = D_CONV
    ts, cb, grid = _conv_grid(s, c)
    nb = s // ts
    goff = c // cb

    def body(dp_ref, dpn_ref, v_ref, g_ref, vh_ref, gh_ref, w_ref, dv_ref, dg_ref, dw_ref, db_ref, *scratch):
        i = pl.program_id(1)
        xext_ref = scratch[2]
        xext_ref[0:HALO, :] = jnp.where(i == 0, 0.0, vh_ref[...].astype(f32) * _sigmoid(gh_ref[...].astype(f32)))
        xext_ref[HALO:, :] = v_ref[...].astype(f32) * _sigmoid(g_ref[...].astype(f32))

        def emit(rows, dh):
            val = v_ref[rows, :].astype(f32)
            sg = _sigmoid(g_ref[rows, :].astype(f32))
            dv_ref[rows, :] = (dh * sg).astype(bf16)
            dg_ref[rows, :] = (dh * val * sg * (1.0 - sg)).astype(bf16)
        _conv_bwd_core(dp_ref[...].astype(f32), dpn_ref, i == nb - 1, w_ref, scratch, dw_ref, db_ref, ts, CONV_WIDTH, emit)

    return pl.pallas_call(
        body, name="conv_glu_bwd", grid=grid,
        in_specs=[_cur(ts, cb), _next_halo(ts, cb, s), _cur(ts, cb), _cur(ts, cb, goff), _prev_halo(ts, cb),
                  _prev_halo(ts, cb, goff), _wspec(CONV_WIDTH, cb)],
        out_specs=[_cur(ts, cb), _cur(ts, cb), _wspec(CONV_WIDTH, cb), _wspec(1, cb)],
        out_shape=[jax.ShapeDtypeStruct((s, c), bf16), jax.ShapeDtypeStruct((s, c), bf16),
                   jax.ShapeDtypeStruct((CONV_WIDTH, c), f32), jax.ShapeDtypeStruct((1, c), f32)],
        scratch_shapes=_conv_bwd_scratch(CONV_WIDTH, ts, cb),
        compiler_params=_cp("parallel", "arbitrary"))(dhc, dhc, proj1, proj1, proj1, proj1, w)


def _ssd_common(dt_ref, prm_ref):
    l = CHUNK
    dtb = prm_ref[0:1, :]
    a = -jnp.exp(prm_ref[1:2, :])
    dsk = prm_ref[2:3, :]
    zraw = dt_ref[...] + dtb
    dt = _softplus(zraw)
    da = dt * a
    row = lax.broadcasted_iota(jnp.int32, (l, l), 0)
    col = lax.broadcasted_iota(jnp.int32, (l, l), 1)
    causal = row >= col
    cs = _dot(causal.astype(f32), da, precision=HIGHEST)
    return a, dsk, zraw, dt, cs, cs.T, causal, row, col


def _ssd_fwd(act, dtf, prm):
    s = act.shape[0]
    nc = s // CHUNK
    l = CHUNK

    def body(xs_ref, dt_ref, prm_ref, y_ref, hs_ref, st_ref):
        @pl.when(pl.program_id(0) == 0)
        def _():
            st_ref[...] = jnp.zeros_like(st_ref)
        a, dsk, _, dt, cs, cst, causal, _, _ = _ssd_common(dt_ref, prm_ref)
        for g in range(N_GROUPS):
            bm = xs_ref[:, B_OFF + D_STATE * g:B_OFF + D_STATE * (g + 1)]
            cm = xs_ref[:, C_OFF + D_STATE * g:C_OFF + D_STATE * (g + 1)]
            gmat = _dot(cm, bm, NT)
            for r in range(HEADS_PER_GROUP):
                h = HEADS_PER_GROUP * g + r
                hsl = slice(HEAD_DIM * h, HEAD_DIM * (h + 1))
                xv = xs_ref[:, hsl].astype(f32)
                csc = cs[:, h:h + 1]
                csr = cst[h:h + 1, :]
                cl = cs[l - 1:l, h:h + 1]
                dk = jnp.exp(jnp.where(causal, csc - csr, NEG))
                xd = xv * dt[:, h:h + 1]
                hp = st_ref[h]
                hs_ref[0, h] = hp
                ydiag = _dot((gmat * dk).astype(bf16), xd.astype(bf16))
                yoff = _dot(cm, hp.astype(bf16), NT) * jnp.exp(csc)
                y_ref[:, hsl] = (ydiag + yoff + xv * dsk[:, h:h + 1]).astype(bf16)
                st = _dot((xd * jnp.exp(cl - csc)).astype(bf16), bm, TN)
                st_ref[h] = hp * jnp.exp(cl) + st

    return pl.pallas_call(
        body, name="ssd_fwd", grid=(nc,),
        in_specs=[pl.BlockSpec((l, XBC_W), lambda i: (i, 0)), pl.BlockSpec((l, 128), lambda i: (i, 0)),
                  pl.BlockSpec((8, 128), lambda i: (0, 0))],
        out_specs=[pl.BlockSpec((l, D_MODEL), lambda i: (i, 0)),
                   pl.BlockSpec((1, N_HEADS, HEAD_DIM, D_STATE), lambda i: (i, 0, 0, 0))],
        out_shape=[jax.ShapeDtypeStruct((s, D_MODEL), bf16),
                   jax.ShapeDtypeStruct((nc, N_HEADS, HEAD_DIM, D_STATE), f32)],
        scratch_shapes=[pltpu.VMEM((N_HEADS, HEAD_DIM, D_STATE), f32)],
        compiler_params=_cp("arbitrary"))(act, dtf, prm)


def _ssd_bwd(act, pre, dtf, prm, hs, dy):
    s = act.shape[0]
    nc = s // CHUNK
    l = CHUNK

    def body(xs_ref, pre_ref, dt_ref, prm_ref, hs_ref, dy_ref, dpre_ref, ddt_ref, dprm_ref, dh_ref):
        @pl.when(pl.program_id(0) == 0)
        def _():
            dh_ref[...] = jnp.zeros_like(dh_ref)
            dprm_ref[...] = jnp.zeros_like(dprm_ref)
        a, dsk, zraw, dt, cs, cst, causal, row, col = _ssd_common(dt_ref, prm_ref)
        lane = lax.broadcasted_iota(jnp.int32, (l, 128), 1)
        rowl = lax.broadcasted_iota(jnp.int32, (l, 128), 0)
        sub = lax.broadcasted_iota(jnp.int32, (128, l), 0)
        lane1 = lax.broadcasted_iota(jnp.int32, (1, 128), 1)
        dcs_c = jnp.zeros((l, 128), f32)
        dcs_r = jnp.zeros((128, l), f32)
        ddt_c = jnp.zeros((l, 128), f32)
        dd_row = jnp.zeros((1, 128), f32)
        for g in range(N_GROUPS):
            bsl = slice(B_OFF + D_STATE * g, B_OFF + D_STATE * (g + 1))
            csl = slice(C_OFF + D_STATE * g, C_OFF + D_STATE * (g + 1))
            bm = xs_ref[:, bsl]
            cm = xs_ref[:, csl]
            gmat = _dot(cm, bm, NT)
            dgm = jnp.zeros((l, l), f32)
            dbg = jnp.zeros((l, D_STATE), f32)
            dcg = jnp.zeros((l, D_STATE), f32)
            for r in range(HEADS_PER_GROUP):
                h = HEADS_PER_GROUP * g + r
                hsl = slice(HEAD_DIM * h, HEAD_DIM * (h + 1))
                xv = xs_ref[:, hsl].astype(f32)
                dyv = dy_ref[:, hsl].astype(f32)
                dyb = dyv.astype(bf16)
                csc = cs[:, h:h + 1]
                csr = cst[h:h + 1, :]
                cl = cs[l - 1:l, h:h + 1]
                dk = jnp.exp(jnp.where(causal, csc - csr, NEG))
                mf = gmat * dk
                dtc = dt[:, h:h + 1]
                xd = xv * dtc
                xdb = xd.astype(bf16)
                ecs = jnp.exp(csc)
                dec = jnp.exp(cl)
                e = jnp.exp(cl - csc)
                hp = hs_ref[0, h]
                hpb = hp.astype(bf16)
                dhn = dh_ref[h]
                dhnb = dhn.astype(bf16)
                dd_h = jnp.sum(jnp.sum(dyv * xv, axis=1, keepdims=True), axis=0, keepdims=True)
                dx = dyv * dsk[:, h:h + 1]
                ch = _dot(cm, hpb, NT)
                dye = dyv * ecs
                dyeb = dye.astype(bf16)
                dcg = dcg + _dot(dyeb, hpb)
                dhp = _dot(dyeb, cm, TN)
                dcs_col = jnp.sum(dye * ch, axis=1, keepdims=True)
                dm = _dot(dyb, xdb, NT)
                dxd = _dot(mf.astype(bf16), dyb, TN)
                dgm = dgm + dm * dk
                wmat = dm * mf
                dcs_col = dcs_col + jnp.sum(wmat, axis=1, keepdims=True)
                dcs_row = -jnp.sum(wmat, axis=0, keepdims=True)
                ddec = jnp.sum(jnp.sum(hp * dhn, axis=1, keepdims=True), axis=0, keepdims=True)
                dxe = _dot(bm, dhnb, NT)
                dxd = dxd + dxe * e
                de_e = jnp.sum(dxe * xd, axis=1, keepdims=True) * e
                dbg = dbg + _dot((xd * e).astype(bf16), dhnb)
                dcs_col = dcs_col - de_e
                dlast = ddec * dec + jnp.sum(de_e, axis=0, keepdims=True)
                dh_ref[h] = dhp + dec * dhn
                dx = dx + dxd * dtc
                ddt_h = jnp.sum(dxd * xv, axis=1, keepdims=True)
                is_h = lane == h
                dcs_c = dcs_c + jnp.where(is_h, dcs_col, 0.0) + jnp.where(is_h & (rowl == l - 1), dlast, 0.0)
                dcs_r = dcs_r + jnp.where(sub == h, dcs_row, 0.0)
                ddt_c = ddt_c + jnp.where(is_h, ddt_h, 0.0)
                dd_row = dd_row + jnp.where(lane1 == h, dd_h, 0.0)
                dpre_ref[:, hsl] = (dx * _dsilu(pre_ref[:, hsl].astype(f32))).astype(bf16)
            dgb = dgm.astype(bf16)
            dcg = dcg + _dot(dgb, bm)
            dbg = dbg + _dot(dgb, cm, TN)
            dpre_ref[:, bsl] = (dbg * _dsilu(pre_ref[:, bsl].astype(f32))).astype(bf16)
            dpre_ref[:, csl] = (dcg * _dsilu(pre_ref[:, csl].astype(f32))).astype(bf16)
        dcs = dcs_c + dcs_r.T
        dda = _dot((row <= col).astype(f32), dcs, precision=HIGHEST)
        ddt = ddt_c + dda * a
        ddtraw = jnp.where(lane < N_HEADS, ddt * _sigmoid(zraw), 0.0)
        ddt_ref[...] = ddtraw
        dprm_ref[0:1, :] += jnp.sum(ddtraw, axis=0, keepdims=True)
        dprm_ref[1:2, :] += jnp.where(lane1 < N_HEADS, jnp.sum(dda * dt, axis=0, keepdims=True) * a, 0.0)
        dprm_ref[2:3, :] += dd_row

    def rev(i):
        return (nc - 1 - i, 0)

    return pl.pallas_call(
        body, name="ssd_bwd", grid=(nc,),
        in_specs=[pl.BlockSpec((l, XBC_W), rev), pl.BlockSpec((l, XBC_W), rev),
                  pl.BlockSpec((l, 128), rev), pl.BlockSpec((8, 128), lambda i: (0, 0)),
                  pl.BlockSpec((1, N_HEADS, HEAD_DIM, D_STATE), lambda i: (nc - 1 - i, 0, 0, 0)),
                  pl.BlockSpec((l, D_MODEL), rev)],
        out_specs=[pl.BlockSpec((l, XBC_W), rev), pl.BlockSpec((l, 128), rev), pl.BlockSpec((8, 128), lambda i: (0, 0))],
        out_shape=[jax.ShapeDtypeStruct((s, XBC_W), bf16), jax.ShapeDtypeStruct((s, 128), f32),
                   jax.ShapeDtypeStruct((8, 128), f32)],
        scratch_shapes=[pltpu.VMEM((N_HEADS, HEAD_DIM, D_STATE), f32)],
        compiler_params=_cp("arbitrary"))(act, pre, dtf, prm, hs, dy)


def _fox_cumsum(dtf, prm):
    s = dtf.shape[0]
    l = CHUNK

    def body(f_ref, prm_ref, c_ref, carry_ref):
        @pl.when(pl.program_id(0) == 0)
        def _():
            carry_ref[...] = jnp.zeros_like(carry_ref)
        lf = _log_sigmoid(f_ref[...] + prm_ref[3:4, :])
        row = lax.broadcasted_iota(jnp.int32, (l, l), 0)
        col = lax.broadcasted_iota(jnp.int32, (l, l), 1)
        c = _dot((row >= col).astype(f32), lf, precision=HIGHEST) + carry_ref[...]
        c_ref[...] = c
        carry_ref[...] = c[l - 1:l, :]

    return pl.pallas_call(
        body, name="fox_cumsum", grid=(s // l,),
        in_specs=[pl.BlockSpec((l, 128), lambda i: (i, 0)), pl.BlockSpec((8, 128), lambda i: (0, 0))],
        out_specs=pl.BlockSpec((l, 128), lambda i: (i, 0)),
        out_shape=jax.ShapeDtypeStruct((s, 128), f32),
        scratch_shapes=[pltpu.VMEM((1, 128), f32)],
        compiler_params=_cp("arbitrary"))(dtf, prm)


def _position():
    return lax.axis_index("x"), lax.axis_index("y"), lax.axis_index("c")


def _exchange_sems(n):
    return [pltpu.SemaphoreType.DMA((n, N_DEV - 1)), pltpu.SemaphoreType.DMA((n, N_DEV - 1)),
            pltpu.SemaphoreType.DMA((n,))]


def _exchange_copies(g_refs, r_refs, send_sems, recv_sems, local_sems, gather=False):
    n = len(g_refs)
    x, y, cc = _position()
    me = 4 * x + 2 * y + cc

    def src(a, j):
        return g_refs[a] if gather else g_refs[a].at[j]

    local = [pltpu.make_async_copy(src(a, me), r_refs[a].at[me], local_sems.at[a]) for a in range(n)]
    sends, recvs = [], []
    for k in range(1, N_DEV):
        px = 1 - x if k & 4 else x
        py = 1 - y if k & 2 else y
        pc = 1 - cc if k & 1 else cc
        pid = 4 * px + 2 * py + pc
        for a in range(n):
            sends.append(pltpu.make_async_remote_copy(
                src_ref=src(a, pid), dst_ref=r_refs[a].at[me],
                send_sem=send_sems.at[a, k - 1], recv_sem=recv_sems.at[a, k - 1],
                device_id=(px, py, pc), device_id_type=pl.DeviceIdType.MESH))
            recvs.append(pltpu.make_async_remote_copy(
                src_ref=src(a, pid), dst_ref=r_refs[a].at[pid],
                send_sem=send_sems.at[a, k - 1], recv_sem=recv_sems.at[a, k - 1],
                device_id=(px, py, pc), device_id_type=pl.DeviceIdType.MESH))
    return local, sends, recvs


def _exchange_start(copies):
    local, sends, _ = copies
    for cp in local + sends:
        cp.start()


def _exchange_wait(copies):
    local, sends, recvs = copies
    for cp in recvs:
        cp.wait_recv()
    for cp in sends:
        cp.wait_send()
    for cp in local:
        cp.wait()


AUG = HEAD_DIM
N_PAIRS = N_HEADS // 2
V_BLOCK = 2 * D_MODEL // 128


def _split3(x):
    hi = x.astype(bf16)
    r1 = x - hi.astype(f32)
    mid = r1.astype(bf16)
    lo = (r1 - mid.astype(f32)).astype(bf16)
    return hi.astype(f32), mid.astype(f32), lo.astype(f32)


def _fox_prep(qkv, c):
    s = qkv.shape[0]
    ts = min(CONV_ROW_TILE, s)
    kb = D_MODEL // 128

    def body(q_ref, k_ref, c_ref, qa_ref, ka_ref):
        lane = lax.broadcasted_iota(jnp.int32, (ts, 128), 1)
        low = lane < HEAD_DIM
        for h in range(N_HEADS):
            psl = slice(128 * (h // 2), 128 * (h // 2 + 1))
            qv = q_ref[:, psl].astype(f32) * (HEAD_DIM ** -0.5)
            kv = k_ref[:, psl].astype(f32)
            if h % 2:
                qv = pltpu.roll(qv, HEAD_DIM, 1)
                kv = pltpu.roll(kv, HEAD_DIM, 1)
            hi, mid, lo = _split3(c_ref[:, F_LANE + h:F_LANE + h + 1])
            ones = jnp.where((lane >= AUG + 3) & (lane < AUG + 6), 1.0, 0.0)
            cq = jnp.where(lane == AUG, hi, jnp.where(lane == AUG + 1, mid, jnp.where(lane == AUG + 2, lo, ones)))
            qa_ref[h] = jnp.where(low, qv, cq).astype(bf16)
            onek = jnp.where((lane >= AUG) & (lane < AUG + 3), 1.0, 0.0)
            ck = jnp.where(lane == AUG + 3, -hi, jnp.where(lane == AUG + 4, -mid, jnp.where(lane == AUG + 5, -lo, onek)))
            ka_ref[h] = jnp.where(low, kv, ck).astype(bf16)

    hm = pl.BlockSpec((N_HEADS, ts, 128), lambda i: (0, i, 0))
    return pl.pallas_call(
        body, name="fox_prep", grid=(s // ts,),
        in_specs=[_rowspec(ts, D_MODEL, 0), _rowspec(ts, D_MODEL, 1), _rowspec(ts, 128)],
        out_specs=[hm, hm], out_shape=[jax.ShapeDtypeStruct((N_HEADS, s, 128), bf16)] * 2,
        compiler_params=_cp("parallel"))(qkv, qkv, c)


def _fox_fwd(qa, ka, qkv, ws):
    s = qkv.shape[0]
    t = min(ATTN_FWD_TILE, s)
    nq = s // t
    n = len(ws)

    def body(qa_ref, ka_ref, v_ref, *rest):
        w_refs, (o_ref, lse_ref), wg_refs, sems = rest[:n], rest[n:n + 2], rest[n + 2:2 * n + 2], rest[2 * n + 2:]
        qi = pl.program_id(1)
        copies = _exchange_copies(w_refs, wg_refs, *sems, gather=True)

        @pl.when((pl.program_id(0) == 0) & (qi == 0))
        def _():
            _exchange_start(copies)
        low = lax.broadcasted_iota(jnp.int32, (t, 128), 1) < HEAD_DIM
        row = lax.broadcasted_iota(jnp.int32, (t, t), 0)
        col = lax.broadcasted_iota(jnp.int32, (t, t), 1)

        def tile(ki, carry, diagonal):
            koff = pl.multiple_of(ki * t, t)
            v = v_ref[pl.ds(koff, t), :]
            vh = (jnp.where(low, v, jnp.ones_like(v)), jnp.where(low, jnp.ones_like(v), v))
            new = []
            for r in range(2):
                m_old, acc = carry[r]
                sc = _dot(qa_ref[r], ka_ref[r, pl.ds(koff, t), :], NT)
                if diagonal:
                    sc = jnp.where(col <= row, sc, NEG)
                m_new = jnp.maximum(m_old, jnp.max(sc, axis=1, keepdims=True))
                p = jnp.exp(sc - m_new)
                new.append((m_new, acc * jnp.exp(m_old - m_new) + _dot(p.astype(bf16), vh[r])))
            return tuple(new)

        init = ((jnp.full((t, 1), NEG, f32), jnp.zeros((t, 128), f32)),) * 2
        carry = lax.fori_loop(0, qi, lambda ki, cr: tile(ki, cr, False), init)
        (m_a, acc_a), (m_b, acc_b) = tile(qi, carry, True)
        l_a, l_b = acc_a[:, HEAD_DIM:HEAD_DIM + 1], acc_b[:, 0:1]
        o_ref[...] = jnp.where(low, acc_a / l_a, acc_b / l_b).astype(bf16)
        for r, lse in enumerate((m_a + jnp.log(l_a), m_b + jnp.log(l_b))):
            lse_ref[r] = jnp.broadcast_to(lse, (t, 128)).T[0:1, :]

        @pl.when((pl.program_id(0) == N_PAIRS - 1) & (qi == nq - 1))
        def _():
            _exchange_wait(copies)

    anyspec = pl.BlockSpec(memory_space=pl.ANY)
    outs = pl.pallas_call(
        body, name="fox_fwd", grid=(N_PAIRS, nq),
        in_specs=[pl.BlockSpec((2, t, 128), lambda j, qi: (j, qi, 0)),
                  pl.BlockSpec((2, s, 128), lambda j, qi: (j, 0, 0)),
                  pl.BlockSpec((s, 128), lambda j, qi: (0, V_BLOCK + j))] + [anyspec] * n,
        out_specs=[pl.BlockSpec((t, 128), lambda j, qi: (qi, j)), pl.BlockSpec((2, 1, t), lambda j, qi: (j, 0, qi))]
        + [anyspec] * n,
        out_shape=[jax.ShapeDtypeStruct((s, D_MODEL), bf16), jax.ShapeDtypeStruct((N_HEADS, 1, s), f32)]
        + [jax.ShapeDtypeStruct((N_DEV,) + w.shape, w.dtype) for w in ws],
        scratch_shapes=_exchange_sems(n),
        compiler_params=_cp("arbitrary", "arbitrary"))(qa, ka, qkv, *ws)
    return outs[0], outs[1], outs[2:]


def _fox_bwd(qa, ka, qkv, do, lse, delta, gs):
    s = qkv.shape[0]
    t = min(ATTN_TILE, s)
    nq = s // t
    n = len(gs)

    def body(qa_ref, ka_ref, v_ref, do_ref, lse_ref, dl_ref, *rest):
        g_refs, (dq_ref, dk_ref, dv_ref), r_refs, sems = rest[:n], rest[n:n + 3], rest[n + 3:2 * n + 3], rest[2 * n + 3:]
        ki = pl.program_id(1)
        copies = _exchange_copies(g_refs, r_refs, *sems)

        @pl.when((pl.program_id(0) == 0) & (ki == 0))
        def _():
            _exchange_start(copies)

        @pl.when(ki == 0)
        def _():
            dq_ref[...] = jnp.zeros_like(dq_ref)
        low = lax.broadcasted_iota(jnp.int32, (t, 128), 1) < HEAD_DIM
        row = lax.broadcasted_iota(jnp.int32, (t, t), 0)
        col = lax.broadcasted_iota(jnp.int32, (t, t), 1)
        v = v_ref[...]
        zero = jnp.zeros_like(v)
        vh = (jnp.where(low, v, zero), jnp.where(low, zero, v))

        def tile(qi, carry, diagonal):
            dks, dv = carry
            qoff = pl.multiple_of(qi * t, t)
            dov = do_ref[pl.ds(qoff, t), :]
            doh = (jnp.where(low, dov, zero), jnp.where(low, zero, dov))
            new_dks = []
            for r in range(2):
                qt = qa_ref[r, pl.ds(qoff, t), :]
                sct = _dot(ka_ref[r], qt, NT)
                if diagonal:
                    sct = jnp.where(row <= col, sct, NEG)
                pt = jnp.exp(sct - lse_ref[r, :, pl.ds(qoff, t)])
                dpt = _dot(vh[r], dov, NT)
                dst = (pt * (dpt - dl_ref[r, :, pl.ds(qoff, t)])).astype(bf16)
                dv = dv + _dot(pt.astype(bf16), doh[r])
                new_dks.append(dks[r] + _dot(dst, qt))
                dq_ref[r, pl.ds(qoff, t), :] += _dot(dst, ka_ref[r], TN)
            return tuple(new_dks), dv

        zacc = jnp.zeros((t, 128), f32)
        carry = tile(ki, ((zacc, zacc), zacc), True)
        dks, dv = lax.fori_loop(ki + 1, nq, lambda qi, cr: tile(qi, cr, False), carry)
        dk_ref[0] = dks[0]
        dk_ref[1] = dks[1]
        dv_ref[...] = dv.astype(bf16)

        @pl.when((pl.program_id(0) == N_PAIRS - 1) & (ki == nq - 1))
        def _():
            _exchange_wait(copies)

    anyspec = pl.BlockSpec(memory_space=pl.ANY)
    outs = pl.pallas_call(
        body, name="fox_bwd", grid=(N_PAIRS, nq),
        in_specs=[pl.BlockSpec((2, s, 128), lambda j, ki: (j, 0, 0)),
                  pl.BlockSpec((2, t, 128), lambda j, ki: (j, ki, 0)),
                  pl.BlockSpec((t, 128), lambda j, ki: (ki, V_BLOCK + j)),
                  pl.BlockSpec((s, 128), lambda j, ki: (0, j)),
                  pl.BlockSpec((2, 1, s), lambda j, ki: (j, 0, 0)),
                  pl.BlockSpec((2, 1, s), lambda j, ki: (j, 0, 0))] + [anyspec] * n,
        out_specs=[pl.BlockSpec((2, s, 128), lambda j, ki: (j, 0, 0)),
                   pl.BlockSpec((2, t, 128), lambda j, ki: (j, ki, 0)),
                   pl.BlockSpec((t, 128), lambda j, ki: (ki, j))] + [anyspec] * n,
        out_shape=[jax.ShapeDtypeStruct((N_HEADS, s, 128), f32), jax.ShapeDtypeStruct((N_HEADS, s, 128), f32),
                   jax.ShapeDtypeStruct((s, D_MODEL), bf16)] + [jax.ShapeDtypeStruct(g.shape, g.dtype) for g in gs],
        scratch_shapes=_exchange_sems(n),
        compiler_params=_cp("arbitrary", "arbitrary"))(qa, ka, qkv, do, lse, delta, *gs)
    return outs[0], outs[1], outs[2], outs[3:]


def _fox_bwd_post(dq_hm, dk_hm):
    s = dq_hm.shape[1]
    ts = min(CONV_ROW_TILE, s)

    def body(dq_ref, dk_ref, q_ref, k_ref, dc_ref):
        lane = lax.broadcasted_iota(jnp.int32, (ts, 128), 1)
        dc = jnp.zeros((ts, 128), f32)
        for h in range(N_HEADS):
            hsl = slice(HEAD_DIM * h, HEAD_DIM * (h + 1))
            dqv = dq_ref[h]
            dkv = dk_ref[h]
            q_ref[:, hsl] = (dqv[:, 0:HEAD_DIM] * (HEAD_DIM ** -0.5)).astype(bf16)
            k_ref[:, hsl] = dkv[:, 0:HEAD_DIM].astype(bf16)
            dc = dc + jnp.where(lane == F_LANE + h, dqv[:, AUG:AUG + 1] - dkv[:, AUG + 3:AUG + 4], 0.0)
        dc_ref[...] = dc

    hm = pl.BlockSpec((N_HEADS, ts, 128), lambda i: (0, i, 0))
    return pl.pallas_call(
        body, name="fox_bwd_post", grid=(s // ts,), in_specs=[hm, hm],
        out_specs=[_rowspec(ts, D_MODEL), _rowspec(ts, D_MODEL), _rowspec(ts, 128)],
        out_shape=[jax.ShapeDtypeStruct((s, D_MODEL), bf16), jax.ShapeDtypeStruct((s, D_MODEL), bf16),
                   jax.ShapeDtypeStruct((s, 128), f32)],
        compiler_params=_cp("parallel"))(dq_hm, dk_hm)


def _fox_gate_bwd(dc, dtf, prm, ddt_raw):
    s = dtf.shape[0]
    l = CHUNK
    nb = s // l

    def body(dc_ref, f_ref, prm_ref, ddt_ref, out_ref, dfb_ref, carry_ref):
        @pl.when(pl.program_id(0) == 0)
        def _():
            carry_ref[...] = jnp.zeros_like(carry_ref)
            dfb_ref[...] = jnp.zeros_like(dfb_ref)
        dc = dc_ref[...]
        row = lax.broadcasted_iota(jnp.int32, (l, l), 0)
        col = lax.broadcasted_iota(jnp.int32, (l, l), 1)
        dlf = _dot((row <= col).astype(f32), dc, precision=HIGHEST) + carry_ref[...]
        carry_ref[...] = dlf[0:1, :]
        lane = lax.broadcasted_iota(jnp.int32, (l, 128), 1)
        is_f = (lane >= F_LANE) & (lane < F_LANE + N_HEADS)
        dfr = jnp.where(is_f, dlf * _sigmoid(-(f_ref[...] + prm_ref[3:4, :])), 0.0)
        dfb_ref[...] += jnp.sum(dfr, axis=0, keepdims=True)
        out_ref[...] = ddt_ref[...] + dfr

    def rev(i):
        return (nb - 1 - i, 0)

    return pl.pallas_call(
        body, name="fox_gate_bwd", grid=(nb,),
        in_specs=[pl.BlockSpec((l, 128), rev), pl.BlockSpec((l, 128), rev), pl.BlockSpec((8, 128), lambda i: (0, 0)),
                  pl.BlockSpec((l, 128), rev)],
        out_specs=[pl.BlockSpec((l, 128), rev), pl.BlockSpec((1, 128), lambda i: (0, 0))],
        out_shape=[jax.ShapeDtypeStruct((s, 128), f32), jax.ShapeDtypeStruct((1, 128), f32)],
        scratch_shapes=[pltpu.VMEM((1, 128), f32)],
        compiler_params=_cp("arbitrary"))(dc, dtf, prm, ddt_raw)


def _all_gather(xl, name):
    r, c = xl.shape

    def body(x_ref, out_ref, send_sems, recv_sems, local_sem):
        x, y, cc = _position()
        me, sibling = (x, y, cc), (x, y, 1 - cc)
        chips = [(1 - x, y), (x, 1 - y), (1 - x, 1 - y)]

        def slot(px, py, pc):
            return out_ref.at[4 * px + 2 * py + pc]

        def copy(k, block, to, src=None):
            return pltpu.make_async_remote_copy(
                src_ref=slot(*block) if src is None else src, dst_ref=slot(*block),
                send_sem=send_sems.at[k], recv_sem=recv_sems.at[k],
                device_id=to, device_id_type=pl.DeviceIdType.MESH)

        mine = pltpu.make_async_copy(x_ref, slot(*me), local_sem)
        mine.start()
        first = [copy(0, me, sibling, src=x_ref)]
        first += [copy(1 + j, me, (*chip, cc), src=x_ref) for j, chip in enumerate(chips)]
        for cp in first:
            cp.start()
        passed = [copy(4 + j, (*chip, cc), sibling) for j, chip in enumerate(chips)]
        for j, chip in enumerate(chips):
            copy(1 + j, (*chip, cc), me).wait_recv()
            passed[j].start()
        copy(0, sibling, me).wait_recv()
        for j, chip in enumerate(chips):
            copy(4 + j, (*chip, 1 - cc), me).wait_recv()
        for cp in first + passed:
            cp.wait_send()
        mine.wait()

    return pl.pallas_call(
        body, name=name,
        out_shape=jax.ShapeDtypeStruct((N_DEV, r, c), xl.dtype),
        in_specs=[pl.BlockSpec(memory_space=pl.ANY)], out_specs=pl.BlockSpec(memory_space=pl.ANY),
        scratch_shapes=[pltpu.SemaphoreType.DMA((7,)), pltpu.SemaphoreType.DMA((7,)), pltpu.SemaphoreType.DMA],
    )(xl)


def _sum_parts(parts, name):
    n, r, c = parts.shape

    def body(p_ref, o_ref):
        g = p_ref[0]
        for i in range(1, n):
            g = g + p_ref[i]
        o_ref[...] = g

    return pl.pallas_call(body, name=name, out_shape=jax.ShapeDtypeStruct((r, c), f32))(parts)


def _adamw(w, m, v, parts, name, tr=128, by_columns=False):
    r, c = w.shape
    n = parts.shape[0]
    tr = min(tr, r)
    c1 = 1.0 - ADAM_B1 ** ADAM_STEP
    c2 = 1.0 - ADAM_B2 ** ADAM_STEP

    def body(w_ref, m_ref, v_ref, p_ref, g_ref, d_ref, nm_ref, nv_ref):
        g = p_ref[0].astype(f32)
        for i in range(1, n):
            g = g + p_ref[i].astype(f32)
        g_ref[...] = g
        nm = ADAM_B1 * m_ref[...] + (1.0 - ADAM_B1) * g
        nv = ADAM_B2 * v_ref[...] + (1.0 - ADAM_B2) * (g * g)
        nm_ref[...] = nm
        nv_ref[...] = nv
        d_ref[...] = -ADAM_LR * ((nm / c1) / (jnp.sqrt(nv / c2) + ADAM_EPS) + ADAM_WD * w_ref[...])

    if by_columns:
        blk = pl.BlockSpec((r, 128), lambda i: (0, i))
        pblk = pl.BlockSpec((n, r, 128), lambda i: (0, 0, i))
        steps = c // 128
    else:
        blk = pl.BlockSpec((tr, c), lambda i: (i, 0))
        pblk = pl.BlockSpec((n, tr, c), lambda i: (0, i, 0))
        steps = r // tr
    return pl.pallas_call(
        body, name=name, grid=(steps,),
        in_specs=[blk, blk, blk, pblk],
        out_specs=[blk] * 4, out_shape=[jax.ShapeDtypeStruct((r, c), f32)] * 4,
        compiler_params=_cp("parallel"))(w, m, v, parts)


def _lanes(w):
    return -(-w // 128) * 128


def _pack(arrs):
    rows = []
    for a in arrs:
        k, w = a.shape
        if w % 128:
            a = jnp.pad(a, ((0, 0), (0, _lanes(w) - w)))
        rows.append(a.reshape(-1, 128))
    out = jnp.concatenate(rows, axis=0)
    pad = -out.shape[0] % 8
    return jnp.pad(out, ((0, pad), (0, 0))) if pad else out


def _unpack(packed, shapes):
    outs, off = [], 0
    lead = packed.shape[:-2]
    for k, w in shapes:
        nrow = k * _lanes(w) // 128
        a = packed[..., off:off + nrow, :].reshape(*lead, k, _lanes(w))[..., :w]
        outs.append(a)
        off += nrow
    return outs


def _gathered_cols(a):
    n, k, wl = a.shape
    return jnp.transpose(a, (1, 0, 2)).reshape(k, n * wl)


def _col_shards(a):
    k, w = a.shape
    return jnp.transpose(a.reshape(k, N_DEV, w // N_DEV), (1, 0, 2))


SMALL_PARAMS = (
    ("e_norm_pre", 1, 1024, False), ("e_conv_w", 4, 2048, True), ("e_conv_b", 1, 2048, False),
    ("e_dt_bias", 1, 16, False), ("e_a_log", 1, 16, False), ("e_d_skip", 1, 16, False), ("e_fgate_b", 1, 16, False),
    ("e_ssd_norm", 1, 1024, False), ("e_norm_post", 1, 1024, False), ("o_norm_pre", 1, 1024, True),
    ("o_conv_w", 31, 2048, True), ("o_conv_b", 1, 2048, True), ("o_ln_g", 1, 2048, True), ("o_ln_b", 1, 2048, True),
    ("o_norm_post", 1, 1024, True),
)
BIG_PARAMS = ("e_w_in", "e_w_out", "o_w_in", "o_w_out")
WEIGHT_ORDER = ("e_norm_pre", "e_w_in", "e_conv_w", "e_conv_b", "e_dt_bias", "e_a_log", "e_d_skip", "e_fgate_b",
                "e_ssd_norm", "e_w_out", "e_norm_post", "o_norm_pre", "o_w_in", "o_conv_w", "o_conv_b", "o_ln_g",
                "o_ln_b", "o_w_out", "o_norm_post")
E_IN = 7200
O_IN = 6144


def kernel(x, e_norm_pre, e_w_in, e_conv_w, e_conv_b, e_dt_bias, e_a_log, e_d_skip, e_fgate_b, e_ssd_norm, e_w_out, e_norm_post, o_norm_pre, o_w_in, o_conv_w, o_conv_b, o_ln_g, o_ln_b, o_w_out, o_norm_post, loss_target, m_e_norm_pre, m_e_w_in, m_e_conv_w, m_e_conv_b, m_e_dt_bias, m_e_a_log, m_e_d_skip, m_e_fgate_b, m_e_ssd_norm, m_e_w_out, m_e_norm_post, m_o_norm_pre, m_o_w_in, m_o_conv_w, m_o_conv_b, m_o_ln_g, m_o_ln_b, m_o_w_out, m_o_norm_post, v_e_norm_pre, v_e_w_in, v_e_conv_w, v_e_conv_b, v_e_dt_bias, v_e_a_log, v_e_d_skip, v_e_fgate_b, v_e_ssd_norm, v_e_w_out, v_e_norm_post, v_o_norm_pre, v_o_w_in, v_o_conv_w, v_o_conv_b, v_o_ln_g, v_o_ln_b, v_o_w_out, v_o_norm_post):
    given = dict(locals())
    w_in = {n: given[n] for n in WEIGHT_ORDER}
    m_in = {n: given["m_" + n] for n in WEIGHT_ORDER}
    v_in = {n: given["v_" + n] for n in WEIGHT_ORDER}

    def mat(a):
        return a.reshape(a.shape[-2:])

    xs = mat(x)
    tgt = mat(loss_target)
    xi, yi, ci = _position()
    me = 4 * xi + 2 * yi + ci
    ow = O_IN // N_DEV
    wr = D_CONV // N_DEV

    ew = E_IN // N_DEV
    w_t = _all_gather(jnp.transpose(mat(e_w_in)).astype(bf16), "gather_weights").reshape(E_IN, D_MODEL)
    later_weights = [mat(e_w_out).astype(bf16), mat(o_w_in).astype(bf16), mat(o_w_out).astype(bf16)]
    w_z, w_xbc, w_qkv = w_t[0:2048], w_t[2048:4096], w_t[4112:7184]
    w_dtf = jnp.concatenate([w_t[4096:4112], w_t[7184:7200], jnp.zeros((96, D_MODEL), bf16)], axis=0)

    sharded_small = [(n, k, w) for n, k, w, sh in SMALL_PARAMS if sh]
    sg = _all_gather(_pack([mat(w_in[n]) for n, _, _ in sharded_small]), "gather_small_weights")
    full_small = {n: _gathered_cols(a)
                  for (n, _, _), a in zip(sharded_small, _unpack(sg, [(k, w // N_DEV) for _, k, w in sharded_small]))}
    for n, _, _, sh in SMALL_PARAMS:
        if not sh:
            full_small[n] = mat(w_in[n])
    p = full_small
    prm = jnp.zeros((8, 128), f32)
    prm = prm.at[0, 0:16].set(p["e_dt_bias"][0]).at[1, 0:16].set(p["e_a_log"][0]).at[2, 0:16].set(p["e_d_skip"][0])
    prm = prm.at[3, F_LANE:F_LANE + 16].set(p["e_fgate_b"][0])

    u0 = _rms_fwd(xs, p["e_norm_pre"], "rms_pre0")
    z0 = _mm_nt([(u0, 0, w_z, 0, D_MODEL)], bf16, "proj0_z", tm=1024, tn=1024)
    xraw = _mm_nt([(u0, 0, w_xbc, 0, D_MODEL)], bf16, "proj0_xbc", tm=1024, tn=1024)
    qkv = _mm_nt([(u0, 0, w_qkv, 0, D_MODEL)], bf16, "proj0_qkv", tm=1024, tn=1024)
    dtf = _mm_nt([(u0, 0, w_dtf, 0, D_MODEL)], f32, "proj0_dtf", tm=1024, tn=128)
    pre, act = _conv_ssd_fwd(xraw, p["e_conv_w"], p["e_conv_b"])
    y, hs = _ssd_fwd(act, dtf, prm)
    qa, ka = _fox_prep(qkv, _fox_cumsum(dtf, prm))
    o, lse, (e_w_out_g, o_w_in_g, o_w_out_g) = _fox_fwd(qa, ka, qkv, later_weights)
    e_w_out_f = e_w_out_g.reshape(D_CONV, D_MODEL)
    o_w_in_f = _gathered_cols(o_w_in_g)
    o_w_out_f = o_w_out_g.reshape(D_CONV, D_MODEL)
    cat = _gate0_fwd(y, z0, o, p["e_ssd_norm"])
    out0 = _mm_nn(cat, e_w_out_f, f32, "out0")
    x1, u1 = _post0_pre1(xs, out0, p["e_norm_post"], p["o_norm_pre"])

    proj1 = _mm_nn(u1, o_w_in_f, bf16, "proj1")
    hc = _conv_glu_fwd(proj1, p["o_conv_w"], p["o_conv_b"])
    h3 = _ln_gate_fwd(hc, proj1, p["o_ln_g"], p["o_ln_b"])
    out1 = _mm_nn(h3, o_w_out_f, f32, "out1")
    dy, d_out1, dg_post1, loss_part = _final_loss(x1, out1, tgt, p["o_norm_post"])

    dh3 = _mm_nt([(d_out1, 0, o_w_out_f, 0, D_MODEL)], bf16, "dh3", tm=1024, tn=D_CONV)
    g_o_w_out = _mm_tn(h3, d_out1, "dw_out1")
    dhc, dz1, dg_ln, db_ln = _ln_gate_bwd(hc, proj1, dh3, p["o_ln_g"], p["o_ln_b"])
    dval, dgate, dw_conv1, db_conv1 = _conv_glu_bwd(dhc, proj1, p["o_conv_w"])
    dproj1 = jnp.concatenate([dval, dgate, dz1], axis=1)
    du1 = _mm_nt([(dproj1, 0, o_w_in_f, 0, O_IN)], f32, "du1")
    g_o_w_in = _mm_tn(u1, dproj1, "dw_in1", tn=ow, blocked=True)
    dx1, d_out0, dg_pre1, dg_post0 = _mid_bwd(x1, du1, dy, out0, p["o_norm_pre"], p["e_norm_post"])

    dcat = _mm_nt([(d_out0, 0, e_w_out_f, 0, D_MODEL)], bf16, "dcat", tm=1024, tn=D_CONV)
    g_e_w_out = _mm_tn(cat, d_out0, "dw_out0")
    dy_ssd, do, dz0, delta, dg_ssd_norm = _gate0_bwd(y, z0, o, dcat, p["e_ssd_norm"])
    early = [g_e_w_out.reshape(N_DEV, wr, D_MODEL).astype(bf16), g_o_w_in.astype(bf16),
             g_o_w_out.reshape(N_DEV, wr, D_MODEL).astype(bf16)]
    dq_hm, dk_hm, dv, early_parts = _fox_bwd(qa, ka, qkv, do, lse, delta[0:N_HEADS].reshape(N_HEADS, 1, -1), early)
    dq, dk, dc = _fox_bwd_post(dq_hm, dk_hm)
    dpre, ddt_raw, dprm = _ssd_bwd(act, pre, dtf, prm, hs, dy_ssd)
    ddtf, dfb = _fox_gate_bwd(dc, dtf, prm, ddt_raw)
    dxraw, dw_conv0, db_conv0 = _conv_ssd_bwd(dpre, xraw, p["e_conv_w"])
    gw_dtf = _mm_tn(ddtf, u0, "dw_in0_dtf")
    g_e_w_in_t = jnp.concatenate([
        _mm_tn(dz0, u0, "dw_in0_z"), _mm_tn(dxraw, u0, "dw_in0_xbc"), gw_dtf[0:16],
        _mm_tn(dq, u0, "dw_in0_q"), _mm_tn(dk, u0, "dw_in0_k"), _mm_tn(dv, u0, "dw_in0_v"), gw_dtf[16:32]], axis=0)
    du0, last_parts = _mm_nt(
        [(dz0, 0, w_z, 0, 2048), (dxraw, 0, w_xbc, 0, 2048), (dq, 0, w_qkv, 0, 1024), (dk, 0, w_qkv, 1, 1024),
         (dv, 0, w_qkv, 2, 1024), (ddtf, 0, w_dtf, 0, 128)], f32, "du0", b_kn=True,
        gs=[g_e_w_in_t.astype(bf16).reshape(N_DEV, ew, D_MODEL)])
    grad_x, dg_pre0 = _first_bwd(xs, du0, dx1, p["e_norm_pre"])

    outs = {"e_w_in": tuple(jnp.transpose(r) for r in _adamw(
        jnp.transpose(mat(e_w_in)), jnp.transpose(mat(m_e_w_in)), jnp.transpose(mat(v_e_w_in)), last_parts[0],
        "adamw_e_w_in", by_columns=True))}
    for n, parts in zip(BIG_PARAMS[1:], early_parts):
        outs[n] = _adamw(mat(w_in[n]), mat(m_in[n]), mat(v_in[n]), parts, "adamw_" + n)

    small_grads = {
        "e_norm_pre": dg_pre0, "e_conv_w": dw_conv0, "e_conv_b": db_conv0, "e_dt_bias": dprm[0:1, 0:16],
        "e_a_log": dprm[1:2, 0:16], "e_d_skip": dprm[2:3, 0:16], "e_fgate_b": dfb[:, F_LANE:F_LANE + 16],
        "e_ssd_norm": dg_ssd_norm, "e_norm_post": dg_post0, "o_norm_pre": dg_pre1, "o_conv_w": dw_conv1,
        "o_conv_b": db_conv1, "o_ln_g": dg_ln, "o_ln_b": db_ln, "o_norm_post": dg_post1,
    }
    gathered = _all_gather(_pack([small_grads[n] for n, _, _, _ in SMALL_PARAMS] + [loss_part]), "gather_small_grads")
    summed = _unpack(_sum_parts(gathered, "sum_small_grads"), [(k, w) for _, k, w, _ in SMALL_PARAMS] + [(1, 128)])
    loss = summed[-1][0, 0]
    g_local = []
    for (n, k, w, sh), g in zip(SMALL_PARAMS, summed):
        g_local.append(lax.dynamic_slice_in_dim(g, me * (w // N_DEV), w // N_DEV, axis=1) if sh else g)
    names = [n for n, _, _, _ in SMALL_PARAMS]
    local_shapes = [(k, w // N_DEV if sh else w) for _, k, w, sh in SMALL_PARAMS]
    res = _adamw(_pack([mat(w_in[n]) for n in names]), _pack([mat(m_in[n]) for n in names]),
                 _pack([mat(v_in[n]) for n in names]), _pack(g_local)[None], "adamw_small", tr=8)
    unpacked = [_unpack(r, local_shapes) for r in res]
    for i, n in enumerate(names):
        outs[n] = tuple(u[i] for u in unpacked)

    ret = [loss, grad_x.reshape(x.shape)]
    for j in range(4):
        ret += [outs[n][j].reshape(w_in[n].shape) for n in WEIGHT_ORDER]
    return tuple(ret)
```

```python
import jax
import jax.numpy as jnp
from jax import lax
from jax.experimental import pallas as pl
from jax.experimental.pallas import tpu as pltpu

f32 = jnp.float32
bf16 = jnp.bfloat16

N_DEV = 8
D_MODEL = 1024
N_HEADS = 16
HEAD_DIM = 64
N_GROUPS = 4
HEADS_PER_GROUP = 4
D_STATE = 128
CHUNK = 512
SSD_CONV = 4
CONV_WIDTH = 31
D_CONV = 2048
EPS = 1e-6
XBC_W = 2048
B_OFF = 1024
C_OFF = 1536
F_LANE = 16
HALO = 32

ADAM_LR = 0.001
ADAM_B1 = 0.9
ADAM_B2 = 0.999
ADAM_EPS = 1e-08
ADAM_WD = 0.01
ADAM_STEP = 10

VMEM_LIMIT_BYTES = 56 * 1024 * 1024
ROW_TILE = 512
CONV_ROW_TILE = 512
CONV_COL_TILE = 512
CONV_SUB = 32
ATTN_TILE = 1024
ATTN_FWD_TILE = 1024

NT = (((1,), (1,)), ((), ()))
TN = (((0,), (0,)), ((), ()))
HIGHEST = lax.Precision.HIGHEST
NEG = -1e30


def _cp(*sem):
    return pltpu.CompilerParams(dimension_semantics=sem if sem else None, vmem_limit_bytes=VMEM_LIMIT_BYTES)


def _sigmoid(x):
    return jax.nn.sigmoid(x)


def _silu(x):
    return x * _sigmoid(x)


def _dsilu(x):
    s = _sigmoid(x)
    return s * (1.0 + x * (1.0 - s))


def _softplus(x):
    return jnp.maximum(x, 0.0) + jnp.log(1.0 + jnp.exp(-jnp.abs(x)))


def _log_sigmoid(x):
    return jnp.minimum(x, 0.0) - jnp.log(1.0 + jnp.exp(-jnp.abs(x)))


def _dot(a, b, dims=None, precision=None):
    if dims is None:
        return jnp.dot(a, b, preferred_element_type=f32, precision=precision)
    return lax.dot_general(a, b, dims, preferred_element_type=f32, precision=precision)


def _mm_nn(a, b, out_dtype, name, tm=1024, tn=1024):
    m, k = a.shape
    n = b.shape[1]
    tm, tn = min(tm, m), min(tn, n)

    def body(a_ref, b_ref, o_ref):
        o_ref[...] = _dot(a_ref[...], b_ref[...]).astype(o_ref.dtype)

    return pl.pallas_call(
        body, name=name, grid=(n // tn, m // tm),
        in_specs=[pl.BlockSpec((tm, k), lambda j, i: (i, 0)), pl.BlockSpec((k, tn), lambda j, i: (0, j))],
        out_specs=pl.BlockSpec((tm, tn), lambda j, i: (i, j)),
        out_shape=jax.ShapeDtypeStruct((m, n), out_dtype), compiler_params=_cp("parallel", "parallel"))(a, b)


def _mm_nt(pairs, out_dtype, name, tm=512, tn=512, gs=(), b_kn=False):
    m = pairs[0][0].shape[0]
    n = pairs[0][2].shape[1] if b_kn else pairs[0][2].shape[0]
    tm, tn = min(tm, m), min(tn, n)
    npair = len(pairs)
    ng = len(gs)
    grid = (n // tn, m // tm)

    def body(*refs):
        g_refs = refs[2 * npair:2 * npair + ng]
        o_ref = refs[2 * npair + ng]
        r_refs = refs[2 * npair + ng + 1:2 * npair + 2 * ng + 1]
        sems = refs[2 * npair + 2 * ng + 1:]
        if ng:
            copies = _exchange_copies(g_refs, r_refs, *sems)

            @pl.when((pl.program_id(0) == 0) & (pl.program_id(1) == 0))
            def _():
                _exchange_start(copies)
        acc = None
        for p in range(npair):
            d = _dot(refs[2 * p][...].astype(bf16), refs[2 * p + 1][...], None if b_kn else NT)
            acc = d if acc is None else acc + d
        o_ref[...] = acc.astype(o_ref.dtype)
        if ng:
            @pl.when((pl.program_id(0) == grid[0] - 1) & (pl.program_id(1) == grid[1] - 1))
            def _():
                _exchange_wait(copies)

    in_specs, args = [], []
    for a, acb, b, bcb, k in pairs:
        in_specs.append(pl.BlockSpec((tm, k), lambda j, i, acb=acb: (i, acb)))
        if b_kn:
            in_specs.append(pl.BlockSpec((k, tn), lambda j, i, bcb=bcb: (bcb, j)))
        else:
            in_specs.append(pl.BlockSpec((tn, k), lambda j, i, bcb=bcb: (j, bcb)))
        args += [a, b]
    anyspec = pl.BlockSpec(memory_space=pl.ANY)
    outs = pl.pallas_call(
        body, name=name, grid=grid, in_specs=in_specs + [anyspec] * ng,
        out_specs=[pl.BlockSpec((tm, tn), lambda j, i: (i, j))] + [anyspec] * ng,
        out_shape=[jax.ShapeDtypeStruct((m, n), out_dtype)] + [jax.ShapeDtypeStruct(g.shape, g.dtype) for g in gs],
        scratch_shapes=_exchange_sems(ng) if ng else [],
        compiler_params=_cp("arbitrary", "arbitrary") if ng else _cp("parallel", "parallel"))(*args, *gs)
    return (outs[0], outs[1:]) if ng else outs[0]


def _mm_tn(a, b, name, a_cb=0, am=None, b_cb=0, bn=None, tn=1024, tk=1024, blocked=False):
    k = a.shape[0]
    am = a.shape[1] if am is None else am
    bn = b.shape[1] if bn is None else bn
    tm = min(1024, am)
    tn, tk = min(tn, bn), min(tk, k)
    a_off, b_off = a_cb * (am // tm), b_cb * (bn // tn)

    def body(a_ref, b_ref, o_ref):
        @pl.when(pl.program_id(2) == 0)
        def _():
            o_ref[...] = jnp.zeros_like(o_ref)
        d = _dot(a_ref[...].astype(bf16), b_ref[...].astype(bf16), TN)
        o_ref[...] += d.reshape(o_ref.shape)

    if blocked:
        out_spec = pl.BlockSpec((1, tm, tn), lambda i, j, kk: (j, i, 0))
        out_shape = jax.ShapeDtypeStruct((bn // tn, am, tn), f32)
    else:
        out_spec = pl.BlockSpec((tm, tn), lambda i, j, kk: (i, j))
        out_shape = jax.ShapeDtypeStruct((am, bn), f32)
    return pl.pallas_call(
        body, name=name, grid=(am // tm, bn // tn, k // tk),
        in_specs=[pl.BlockSpec((tk, tm), lambda i, j, kk: (kk, a_off + i)),
                  pl.BlockSpec((tk, tn), lambda i, j, kk: (kk, b_off + j))],
        out_specs=out_spec, out_shape=out_shape,
        compiler_params=_cp("parallel", "parallel", "arbitrary"))(a, b)


def _rowspec(ts, w, cb=0):
    return pl.BlockSpec((ts, w), lambda i: (i, cb))


def _vecspec(w):
    return pl.BlockSpec((1, w), lambda i: (0, 0))


def _rms_fwd(x, g, name):
    s, d = x.shape
    ts = min(ROW_TILE, s)

    def body(x_ref, g_ref, u_ref):
        xv = x_ref[...]
        r = lax.rsqrt(jnp.mean(xv * xv, axis=-1, keepdims=True) + EPS)
        u_ref[...] = (xv * r * g_ref[...]).astype(bf16)

    return pl.pallas_call(
        body, name=name, grid=(s // ts,), in_specs=[_rowspec(ts, d), _vecspec(d)], out_specs=_rowspec(ts, d),
        out_shape=jax.ShapeDtypeStruct((s, d), bf16), compiler_params=_cp("parallel"))(x, g)


def _rms_bwd_vals(xv, g, dy):
    r = lax.rsqrt(jnp.mean(xv * xv, axis=-1, keepdims=True) + EPS)
    xh = xv * r
    dg = jnp.sum(dy * xh, axis=0, keepdims=True)
    dxh = dy * g
    dx = r * (dxh - xh * jnp.mean(dxh * xh, axis=-1, keepdims=True))
    return dx, dg


def _gate0_fwd(y, z, o, ssd_norm):
    s = y.shape[0]
    ts = min(ROW_TILE, s)
    gw = D_MODEL // N_GROUPS

    def body(y_ref, zs_ref, zf_ref, o_ref, w_ref, cat_ref):
        yg = y_ref[...].astype(f32) * _silu(zs_ref[...].astype(f32))
        for g in range(N_GROUPS):
            seg = yg[:, gw * g:gw * (g + 1)]
            r = lax.rsqrt(jnp.mean(seg * seg, axis=-1, keepdims=True) + EPS)
            cat_ref[:, gw * g:gw * (g + 1)] = (seg * r * w_ref[:, gw * g:gw * (g + 1)]).astype(bf16)
        cat_ref[:, D_MODEL:] = (o_ref[...].astype(f32) * _silu(zf_ref[...].astype(f32))).astype(bf16)

    return pl.pallas_call(
        body, name="gate0_fwd", grid=(s // ts,),
        in_specs=[_rowspec(ts, D_MODEL), _rowspec(ts, D_MODEL, 0), _rowspec(ts, D_MODEL, 1), _rowspec(ts, D_MODEL),
                  _vecspec(D_MODEL)],
        out_specs=_rowspec(ts, 2 * D_MODEL),
        out_shape=jax.ShapeDtypeStruct((s, 2 * D_MODEL), bf16), compiler_params=_cp("parallel"))(y, z, z, o, ssd_norm)


def _post0_pre1(x, out0, g_post0, g_pre1):
    s, d = x.shape
    ts = min(ROW_TILE, s)

    def body(x_ref, o_ref, gp_ref, gn_ref, x1_ref, u1_ref):
        ov = o_ref[...]
        r = lax.rsqrt(jnp.mean(ov * ov, axis=-1, keepdims=True) + EPS)
        x1 = x_ref[...] + ov * r * gp_ref[...]
        x1_ref[...] = x1
        r1 = lax.rsqrt(jnp.mean(x1 * x1, axis=-1, keepdims=True) + EPS)
        u1_ref[...] = (x1 * r1 * gn_ref[...]).astype(bf16)

    return pl.pallas_call(
        body, name="post0_pre1", grid=(s // ts,),
        in_specs=[_rowspec(ts, d), _rowspec(ts, d), _vecspec(d), _vecspec(d)],
        out_specs=[_rowspec(ts, d), _rowspec(ts, d)],
        out_shape=[jax.ShapeDtypeStruct((s, d), f32), jax.ShapeDtypeStruct((s, d), bf16)],
        compiler_params=_cp("parallel"))(x, out0, g_post0, g_pre1)


def _ln_vals(hc, g, b):
    mu = jnp.mean(hc, axis=-1, keepdims=True)
    xc = hc - mu
    rstd = lax.rsqrt(jnp.mean(xc * xc, axis=-1, keepdims=True) + EPS)
    xh = xc * rstd
    return xh, rstd, xh * g + b


def _ln_gate_fwd(hc, proj1, ln_g, ln_b):
    s = hc.shape[0]
    ts = min(ROW_TILE, s)

    def body(hc_ref, z_ref, g_ref, b_ref, h3_ref):
        _, _, ln = _ln_vals(hc_ref[...].astype(f32), g_ref[...], b_ref[...])
        h3_ref[...] = (_silu(ln) * _silu(z_ref[...].astype(f32))).astype(bf16)

    return pl.pallas_call(
        body, name="ln_gate_fwd", grid=(s // ts,),
        in_specs=[_rowspec(ts, D_CONV), _rowspec(ts, D_CONV, 2), _vecspec(D_CONV), _vecspec(D_CONV)],
        out_specs=_rowspec(ts, D_CONV),
        out_shape=jax.ShapeDtypeStruct((s, D_CONV), bf16), compiler_params=_cp("parallel"))(hc, proj1, ln_g, ln_b)


def _final_loss(x1, out1, tgt, g_post1):
    s, d = x1.shape
    ts = min(ROW_TILE, s)

    def body(x1_ref, o_ref, t_ref, g_ref, dy_ref, do_ref, dg_ref, loss_ref):
        i = pl.program_id(0)

        @pl.when(i == 0)
        def _():
            dg_ref[...] = jnp.zeros_like(dg_ref)
            loss_ref[...] = jnp.zeros_like(loss_ref)
        ov = o_ref[...]
        g = g_ref[...]
        r = lax.rsqrt(jnp.mean(ov * ov, axis=-1, keepdims=True) + EPS)
        diff = x1_ref[...] + ov * r * g - t_ref[...]
        row = jnp.mean(diff * diff, axis=-1, keepdims=True)
        loss_ref[...] += jnp.broadcast_to(0.5 * jnp.sum(row, axis=0, keepdims=True), loss_ref.shape)
        dy = diff * (1.0 / d)
        dy_ref[...] = dy
        dx, dg = _rms_bwd_vals(ov, g, dy)
        do_ref[...] = dx.astype(bf16)
        dg_ref[...] += dg

    return pl.pallas_call(
        body, name="final_loss", grid=(s // ts,),
        in_specs=[_rowspec(ts, d), _rowspec(ts, d), _rowspec(ts, d), _vecspec(d)],
        out_specs=[_rowspec(ts, d), _rowspec(ts, d), _vecspec(d), _vecspec(128)],
        out_shape=[jax.ShapeDtypeStruct((s, d), f32), jax.ShapeDtypeStruct((s, d), bf16),
                   jax.ShapeDtypeStruct((1, d), f32), jax.ShapeDtypeStruct((1, 128), f32)],
        compiler_params=_cp("arbitrary"))(x1, out1, tgt, g_post1)


def _ln_gate_bwd(hc, proj1, dh3, ln_g, ln_b):
    s = hc.shape[0]
    ts = min(ROW_TILE, s)

    def body(hc_ref, z_ref, dh_ref, g_ref, b_ref, dhc_ref, dz_ref, dg_ref, db_ref):
        @pl.when(pl.program_id(0) == 0)
        def _():
            dg_ref[...] = jnp.zeros_like(dg_ref)
            db_ref[...] = jnp.zeros_like(db_ref)
        g = g_ref[...]
        xh, rstd, ln = _ln_vals(hc_ref[...].astype(f32), g, b_ref[...])
        zv = z_ref[...].astype(f32)
        dh3 = dh_ref[...].astype(f32)
        dz_ref[...] = (dh3 * _silu(ln) * _dsilu(zv)).astype(bf16)
        dln = dh3 * _silu(zv) * _dsilu(ln)
        dg_ref[...] += jnp.sum(dln * xh, axis=0, keepdims=True)
        db_ref[...] += jnp.sum(dln, axis=0, keepdims=True)
        dxh = dln * g
        dhc = rstd * (dxh - jnp.mean(dxh, axis=-1, keepdims=True) - xh * jnp.mean(dxh * xh, axis=-1, keepdims=True))
        dhc_ref[...] = dhc.astype(bf16)

    return pl.pallas_call(
        body, name="ln_gate_bwd", grid=(s // ts,),
        in_specs=[_rowspec(ts, D_CONV), _rowspec(ts, D_CONV, 2), _rowspec(ts, D_CONV), _vecspec(D_CONV),
                  _vecspec(D_CONV)],
        out_specs=[_rowspec(ts, D_CONV), _rowspec(ts, D_CONV), _vecspec(D_CONV), _vecspec(D_CONV)],
        out_shape=[jax.ShapeDtypeStruct((s, D_CONV), bf16), jax.ShapeDtypeStruct((s, D_CONV), bf16),
                   jax.ShapeDtypeStruct((1, D_CONV), f32), jax.ShapeDtypeStruct((1, D_CONV), f32)],
        compiler_params=_cp("arbitrary"))(hc, proj1, dh3, ln_g, ln_b)


def _mid_bwd(x1, du1, dy, out0, g_pre1, g_post0):
    s, d = x1.shape
    ts = min(ROW_TILE, s)

    def body(x1_ref, du_ref, dy_ref, o_ref, gn_ref, gp_ref, dx1_ref, do_ref, dgn_ref, dgp_ref):
        @pl.when(pl.program_id(0) == 0)
        def _():
            dgn_ref[...] = jnp.zeros_like(dgn_ref)
            dgp_ref[...] = jnp.zeros_like(dgp_ref)
        dxa, dgn = _rms_bwd_vals(x1_ref[...], gn_ref[...], du_ref[...])
        dx1 = dy_ref[...] + dxa
        dx1_ref[...] = dx1
        dgn_ref[...] += dgn
        dxo, dgp = _rms_bwd_vals(o_ref[...], gp_ref[...], dx1)
        do_ref[...] = dxo.astype(bf16)
        dgp_ref[...] += dgp

    return pl.pallas_call(
        body, name="mid_bwd", grid=(s // ts,),
        in_specs=[_rowspec(ts, d)] * 4 + [_vecspec(d), _vecspec(d)],
        out_specs=[_rowspec(ts, d), _rowspec(ts, d), _vecspec(d), _vecspec(d)],
        out_shape=[jax.ShapeDtypeStruct((s, d), f32), jax.ShapeDtypeStruct((s, d), bf16),
                   jax.ShapeDtypeStruct((1, d), f32), jax.ShapeDtypeStruct((1, d), f32)],
        compiler_params=_cp("arbitrary"))(x1, du1, dy, out0, g_pre1, g_post0)


def _first_bwd(x, du0, dx1, g_pre0):
    s, d = x.shape
    ts = min(ROW_TILE, s)

    def body(x_ref, du_ref, dx1_ref, g_ref, dx_ref, dg_ref):
        @pl.when(pl.program_id(0) == 0)
        def _():
            dg_ref[...] = jnp.zeros_like(dg_ref)
        dxa, dg = _rms_bwd_vals(x_ref[...], g_ref[...], du_ref[...])
        dx_ref[...] = dx1_ref[...] + dxa
        dg_ref[...] += dg

    return pl.pallas_call(
        body, name="first_bwd", grid=(s // ts,),
        in_specs=[_rowspec(ts, d)] * 3 + [_vecspec(d)],
        out_specs=[_rowspec(ts, d), _vecspec(d)],
        out_shape=[jax.ShapeDtypeStruct((s, d), f32), jax.ShapeDtypeStruct((1, d), f32)],
        compiler_params=_cp("arbitrary"))(x, du0, dx1, g_pre0)


def _gate0_bwd(y, z, o, dcat, ssd_norm):
    s = y.shape[0]
    ts = min(ROW_TILE, s)
    gw = D_MODEL // N_GROUPS

    def body(y_ref, zs_ref, zf_ref, o_ref, dn_ref, dg_ref, w_ref, dy_ref, do_ref, dz_ref, delta_ref, dw_ref):
        @pl.when(pl.program_id(0) == 0)
        def _():
            dw_ref[...] = jnp.zeros_like(dw_ref)
        yv = y_ref[...].astype(f32)
        zs = zs_ref[...].astype(f32)
        sz = _silu(zs)
        yg = yv * sz
        dyn = dn_ref[...].astype(f32)
        for g in range(N_GROUPS):
            sl = slice(gw * g, gw * (g + 1))
            seg = yg[:, sl]
            r = lax.rsqrt(jnp.mean(seg * seg, axis=-1, keepdims=True) + EPS)
            yh = seg * r
            dn = dyn[:, sl]
            dw_ref[:, sl] += jnp.sum(dn * yh, axis=0, keepdims=True)
            dyh = dn * w_ref[:, sl]
            dyg = r * (dyh - yh * jnp.mean(dyh * yh, axis=-1, keepdims=True))
            dy_ref[:, sl] = (dyg * sz[:, sl]).astype(bf16)
            dz_ref[:, sl] = (dyg * yv[:, sl] * _dsilu(zs[:, sl])).astype(bf16)
        zf = zf_ref[...].astype(f32)
        ov = o_ref[...].astype(f32)
        dog = dg_ref[...].astype(f32)
        dov = (dog * _silu(zf)).astype(bf16)
        do_ref[...] = dov
        dz_ref[:, D_MODEL:] = (dog * ov * _dsilu(zf)).astype(bf16)
        prod = dov.astype(f32) * ov
        lane = lax.broadcasted_iota(jnp.int32, (ts, 128), 1)
        delta = jnp.zeros((ts, 128), f32)
        for h in range(N_HEADS):
            dh = jnp.sum(prod[:, HEAD_DIM * h:HEAD_DIM * (h + 1)], axis=-1, keepdims=True)
            delta = delta + jnp.where(lane == h, dh, 0.0)
        delta_ref[...] = delta.T

    return pl.pallas_call(
        body, name="gate0_bwd", grid=(s // ts,),
        in_specs=[_rowspec(ts, D_MODEL), _rowspec(ts, D_MODEL, 0), _rowspec(ts, D_MODEL, 1), _rowspec(ts, D_MODEL),
                  _rowspec(ts, D_MODEL, 0), _rowspec(ts, D_MODEL, 1), _vecspec(D_MODEL)],
        out_specs=[_rowspec(ts, D_MODEL), _rowspec(ts, D_MODEL), _rowspec(ts, 2 * D_MODEL),
                   pl.BlockSpec((128, ts), lambda i: (0, i)), _vecspec(D_MODEL)],
        out_shape=[jax.ShapeDtypeStruct((s, D_MODEL), bf16), jax.ShapeDtypeStruct((s, D_MODEL), bf16),
                   jax.ShapeDtypeStruct((s, 2 * D_MODEL), bf16), jax.ShapeDtypeStruct((128, s), f32),
                   jax.ShapeDtypeStruct((1, D_MODEL), f32)],
        compiler_params=_cp("arbitrary"))(y, z, z, o, dcat, dcat, ssd_norm)


def _conv_grid(s, c):
    ts, cb = min(CONV_ROW_TILE, s), min(CONV_COL_TILE, c)
    return ts, cb, (c // cb, s // ts)


def _cur(ts, cb, off=0):
    return pl.BlockSpec((ts, cb), lambda c, i: (i, c + off))


def _prev_halo(ts, cb, off=0):
    return pl.BlockSpec((HALO, cb), lambda c, i: (jnp.maximum(i * (ts // HALO) - 1, 0), c + off))


def _next_halo(ts, cb, s, off=0):
    return pl.BlockSpec((HALO, cb), lambda c, i: (jnp.minimum((i + 1) * (ts // HALO), s // HALO - 1), c + off))


def _wspec(k, cb):
    return pl.BlockSpec((k, cb), lambda c, i: (0, c))


def _phases(offsets):
    return sorted({o % 8 for o in offsets} - {0})


def _shift_scratch(offsets, ts, cb):
    return pltpu.VMEM((max(len(_phases(offsets)), 1), ts + HALO - 8, cb), f32)


def _fill_phases(ext_ref, sh_ref, offsets, ts):
    for j, r in enumerate(_phases(offsets)):
        sh_ref[j] = ext_ref[pl.ds(r, ts + HALO - 8), :]


def _slab(ext_ref, sh_ref, offsets, off, start):
    r = off % 8
    a = off - r + start
    if r == 0:
        return ext_ref[a:a + CONV_SUB, :]
    return sh_ref[_phases(offsets).index(r), a:a + CONV_SUB, :]


def _conv_taps(ext_ref, sh_ref, w_ref, b_ref, ts, k_taps, emit):
    offsets = [HALO - (k_taps - 1) + k for k in range(k_taps)]
    _fill_phases(ext_ref, sh_ref, offsets, ts)
    for sb in range(ts // CONV_SUB):
        acc = b_ref[...]
        for k in range(k_taps):
            acc = acc + w_ref[k:k + 1, :] * _slab(ext_ref, sh_ref, offsets, offsets[k], sb * CONV_SUB)
        emit(slice(sb * CONV_SUB, (sb + 1) * CONV_SUB), acc)


def _conv_ssd_fwd(xraw, w, b):
    s, c = xraw.shape
    ts, cb, grid = _conv_grid(s, c)
    offsets = [HALO - (SSD_CONV - 1) + k for k in range(SSD_CONV)]

    def body(x_ref, xh_ref, w_ref, b_ref, pre_ref, act_ref, ext_ref, sh_ref):
        first = pl.program_id(1) == 0
        ext_ref[0:HALO, :] = jnp.where(first, 0.0, xh_ref[...].astype(f32))
        ext_ref[HALO:, :] = x_ref[...].astype(f32)

        def emit(rows, pre):
            pre_ref[rows, :] = pre.astype(bf16)
            act_ref[rows, :] = _silu(pre).astype(bf16)
        _conv_taps(ext_ref, sh_ref, w_ref, b_ref, ts, SSD_CONV, emit)

    return pl.pallas_call(
        body, name="conv_ssd_fwd", grid=grid,
        in_specs=[_cur(ts, cb), _prev_halo(ts, cb), _wspec(SSD_CONV, cb), _wspec(1, cb)],
        out_specs=[_cur(ts, cb), _cur(ts, cb)],
        out_shape=[jax.ShapeDtypeStruct((s, c), bf16)] * 2,
        scratch_shapes=[pltpu.VMEM((HALO + ts, cb), f32), _shift_scratch(offsets, ts, cb)],
        compiler_params=_cp("parallel", "parallel"))(xraw, xraw, w, b)


def _conv_glu_fwd(proj1, w, b):
    s = proj1.shape[0]
    c = D_CONV
    ts, cb, grid = _conv_grid(s, c)
    goff = c // cb

    offsets = [HALO - (CONV_WIDTH - 1) + k for k in range(CONV_WIDTH)]

    def body(v_ref, g_ref, vh_ref, gh_ref, w_ref, b_ref, hc_ref, ext_ref, sh_ref):
        first = pl.program_id(1) == 0
        hh = vh_ref[...].astype(f32) * _sigmoid(gh_ref[...].astype(f32))
        ext_ref[0:HALO, :] = jnp.where(first, 0.0, hh)
        ext_ref[HALO:, :] = v_ref[...].astype(f32) * _sigmoid(g_ref[...].astype(f32))

        def emit(rows, hc):
            hc_ref[rows, :] = hc.astype(bf16)
        _conv_taps(ext_ref, sh_ref, w_ref, b_ref, ts, CONV_WIDTH, emit)

    return pl.pallas_call(
        body, name="conv_glu_fwd", grid=grid,
        in_specs=[_cur(ts, cb), _cur(ts, cb, goff), _prev_halo(ts, cb), _prev_halo(ts, cb, goff),
                  _wspec(CONV_WIDTH, cb), _wspec(1, cb)],
        out_specs=_cur(ts, cb),
        out_shape=jax.ShapeDtypeStruct((s, c), bf16),
        scratch_shapes=[pltpu.VMEM((HALO + ts, cb), f32), _shift_scratch(offsets, ts, cb)],
        compiler_params=_cp("parallel", "parallel"))(proj1, proj1, proj1, proj1, w, b)


def _conv_bwd_offsets(k_taps):
    return [k_taps - 1 - k for k in range(k_taps)], [HALO - (k_taps - 1) + k for k in range(k_taps)]


def _conv_bwd_scratch(k_taps, ts, cb):
    d_offs, x_offs = _conv_bwd_offsets(k_taps)
    return [pltpu.VMEM((ts + HALO, cb), f32), _shift_scratch(d_offs, ts, cb),
            pltpu.VMEM((HALO + ts, cb), f32), _shift_scratch(x_offs, ts, cb),
            pltpu.VMEM((k_taps, 8, cb), f32), pltpu.VMEM((8, cb), f32)]


def _conv_bwd_core(dp, dpn_ref, last, w_ref, scratch, dw_ref, db_ref, ts, k_taps, emit):
    dext_ref, dsh_ref, xext_ref, xsh_ref, dw8_ref, db8_ref = scratch
    d_offs, x_offs = _conv_bwd_offsets(k_taps)
    dext_ref[0:ts, :] = dp
    dext_ref[ts:, :] = jnp.where(last, 0.0, dpn_ref[...].astype(f32))
    _fill_phases(dext_ref, dsh_ref, d_offs, ts)
    _fill_phases(xext_ref, xsh_ref, x_offs, ts)

    @pl.when(pl.program_id(1) == 0)
    def _():
        dw8_ref[...] = jnp.zeros_like(dw8_ref)
        db8_ref[...] = jnp.zeros_like(db8_ref)
    cb = dp.shape[1]
    for sb in range(ts // CONV_SUB):
        start = sb * CONV_SUB
        dpv = dext_ref[start:start + CONV_SUB, :]
        dx = None
        for k in range(k_taps):
            t = w_ref[k:k + 1, :] * _slab(dext_ref, dsh_ref, d_offs, d_offs[k], start)
            dx = t if dx is None else dx + t
            prod = dpv * _slab(xext_ref, xsh_ref, x_offs, x_offs[k], start)
            dw8_ref[k] += jnp.sum(prod.reshape(CONV_SUB // 8, 8, cb), axis=0)
        db8_ref[...] += jnp.sum(dpv.reshape(CONV_SUB // 8, 8, cb), axis=0)
        emit(slice(start, start + CONV_SUB), dx)

    @pl.when(last)
    def _():
        dw_ref[...] = jnp.sum(dw8_ref[...], axis=1)
        db_ref[...] = jnp.sum(db8_ref[...], axis=0, keepdims=True)


def _conv_ssd_bwd(dpre, xraw, w):
    s, c = xraw.shape
    ts, cb, grid = _conv_grid(s, c)
    nb = s // ts

    def body(dp_ref, dpn_ref, x_ref, xh_ref, w_ref, dx_ref, dw_ref, db_ref, *scratch):
        i = pl.program_id(1)
        xext_ref = scratch[2]
        xext_ref[0:HALO, :] = jnp.where(i == 0, 0.0, xh_ref[...].astype(f32))
        xext_ref[HALO:, :] = x_ref[...].astype(f32)

        def emit(rows, dx):
            dx_ref[rows, :] = dx.astype(bf16)
        _conv_bwd_core(dp_ref[...].astype(f32), dpn_ref, i == nb - 1, w_ref, scratch, dw_ref, db_ref, ts, SSD_CONV, emit)

    return pl.pallas_call(
        body, name="conv_ssd_bwd", grid=grid,
        in_specs=[_cur(ts, cb), _next_halo(ts, cb, s), _cur(ts, cb), _prev_halo(ts, cb), _wspec(SSD_CONV, cb)],
        out_specs=[_cur(ts, cb), _wspec(SSD_CONV, cb), _wspec(1, cb)],
        out_shape=[jax.ShapeDtypeStruct((s, c), bf16), jax.ShapeDtypeStruct((SSD_CONV, c), f32),
                   jax.ShapeDtypeStruct((1, c), f32)],
        scratch_shapes=_conv_bwd_scratch(SSD_CONV, ts, cb),
        compiler_params=_cp("parallel", "arbitrary"))(dpre, dpre, xraw, xraw, w)


def _conv_glu_bwd(dhc, proj1, w):
    s = proj1.shape[0]
    c = D_CONV
    ts, cb, grid = _conv_grid(s, c)
    nb = s // ts
    goff = c // cb

    def body(dp_ref, dpn_ref, v_ref, g_ref, vh_ref, gh_ref, w_ref, dv_ref, dg_ref, dw_ref, db_ref, *scratch):
        i = pl.program_id(1)
        xext_ref = scratch[2]
        xext_ref[0:HALO, :] = jnp.where(i == 0, 0.0, vh_ref[...].astype(f32) * _sigmoid(gh_ref[...].astype(f32)))
        xext_ref[HALO:, :] = v_ref[...].astype(f32) * _sigmoid(g_ref[...].astype(f32))

        def emit(rows, dh):
            val = v_ref[rows, :].astype(f32)
            sg = _sigmoid(g_ref[rows, :].astype(f32))
            dv_ref[rows, :] = (dh * sg).astype(bf16)
            dg_ref[rows, :] = (dh * val * sg * (1.0 - sg)).astype(bf16)
        _conv_bwd_core(dp_ref[...].astype(f32), dpn_ref, i == nb - 1, w_ref, scratch, dw_ref, db_ref, ts, CONV_WIDTH, emit)

    return pl.pallas_call(
        body, name="conv_glu_bwd", grid=grid,
        in_specs=[_cur(ts, cb), _next_halo(ts, cb, s), _cur(ts, cb), _cur(ts, cb, goff), _prev_halo(ts, cb),
                  _prev_halo(ts, cb, goff), _wspec(CONV_WIDTH, cb)],
        out_specs=[_cur(ts, cb), _cur(ts, cb), _wspec(CONV_WIDTH, cb), _wspec(1, cb)],
        out_shape=[jax.ShapeDtypeStruct((s, c), bf16), jax.ShapeDtypeStruct((s, c), bf16),
                   jax.ShapeDtypeStruct((CONV_WIDTH, c), f32), jax.ShapeDtypeStruct((1, c), f32)],
        scratch_shapes=_conv_bwd_scratch(CONV_WIDTH, ts, cb),
        compiler_params=_cp("parallel", "arbitrary"))(dhc, dhc, proj1, proj1, proj1, proj1, w)


def _ssd_common(dt_ref, prm_ref):
    l = CHUNK
    dtb = prm_ref[0:1, :]
    a = -jnp.exp(prm_ref[1:2, :])
    dsk = prm_ref[2:3, :]
    zraw = dt_ref[...] + dtb
    dt = _softplus(zraw)
    da = dt * a
    row = lax.broadcasted_iota(jnp.int32, (l, l), 0)
    col = lax.broadcasted_iota(jnp.int32, (l, l), 1)
    causal = row >= col
    cs = _dot(causal.astype(f32), da, precision=HIGHEST)
    return a, dsk, zraw, dt, cs, cs.T, causal, row, col


def _ssd_fwd(act, dtf, prm):
    s = act.shape[0]
    nc = s // CHUNK
    l = CHUNK

    def body(xs_ref, dt_ref, prm_ref, y_ref, hs_ref, st_ref):
        @pl.when(pl.program_id(0) == 0)
        def _():
            st_ref[...] = jnp.zeros_like(st_ref)
        a, dsk, _, dt, cs, cst, causal, _, _ = _ssd_common(dt_ref, prm_ref)
        for g in range(N_GROUPS):
            bm = xs_ref[:, B_OFF + D_STATE * g:B_OFF + D_STATE * (g + 1)]
            cm = xs_ref[:, C_OFF + D_STATE * g:C_OFF + D_STATE * (g + 1)]
            gmat = _dot(cm, bm, NT)
            for r in range(HEADS_PER_GROUP):
                h = HEADS_PER_GROUP * g + r
                hsl = slice(HEAD_DIM * h, HEAD_DIM * (h + 1))
                xv = xs_ref[:, hsl].astype(f32)
                csc = cs[:, h:h + 1]
                csr = cst[h:h + 1, :]
                cl = cs[l - 1:l, h:h + 1]
                dk = jnp.exp(jnp.where(causal, csc - csr, NEG))
                xd = xv * dt[:, h:h + 1]
                hp = st_ref[h]
                hs_ref[0, h] = hp
                ydiag = _dot((gmat * dk).astype(bf16), xd.astype(bf16))
                yoff = _dot(cm, hp.astype(bf16), NT) * jnp.exp(csc)
                y_ref[:, hsl] = (ydiag + yoff + xv * dsk[:, h:h + 1]).astype(bf16)
                st = _dot((xd * jnp.exp(cl - csc)).astype(bf16), bm, TN)
                st_ref[h] = hp * jnp.exp(cl) + st

    return pl.pallas_call(
        body, name="ssd_fwd", grid=(nc,),
        in_specs=[pl.BlockSpec((l, XBC_W), lambda i: (i, 0)), pl.BlockSpec((l, 128), lambda i: (i, 0)),
                  pl.BlockSpec((8, 128), lambda i: (0, 0))],
        out_specs=[pl.BlockSpec((l, D_MODEL), lambda i: (i, 0)),
                   pl.BlockSpec((1, N_HEADS, HEAD_DIM, D_STATE), lambda i: (i, 0, 0, 0))],
        out_shape=[jax.ShapeDtypeStruct((s, D_MODEL), bf16),
                   jax.ShapeDtypeStruct((nc, N_HEADS, HEAD_DIM, D_STATE), f32)],
        scratch_shapes=[pltpu.VMEM((N_HEADS, HEAD_DIM, D_STATE), f32)],
        compiler_params=_cp("arbitrary"))(act, dtf, prm)


def _ssd_bwd(act, pre, dtf, prm, hs, dy):
    s = act.shape[0]
    nc = s // CHUNK
    l = CHUNK

    def body(xs_ref, pre_ref, dt_ref, prm_ref, hs_ref, dy_ref, dpre_ref, ddt_ref, dprm_ref, dh_ref):
        @pl.when(pl.program_id(0) == 0)
        def _():
            dh_ref[...] = jnp.zeros_like(dh_ref)
            dprm_ref[...] = jnp.zeros_like(dprm_ref)
        a, dsk, zraw, dt, cs, cst, causal, row, col = _ssd_common(dt_ref, prm_ref)
        lane = lax.broadcasted_iota(jnp.int32, (l, 128), 1)
        rowl = lax.broadcasted_iota(jnp.int32, (l, 128), 0)
        sub = lax.broadcasted_iota(jnp.int32, (128, l), 0)
        lane1 = lax.broadcasted_iota(jnp.int32, (1, 128), 1)
        dcs_c = jnp.zeros((l, 128), f32)
        dcs_r = jnp.zeros((128, l), f32)
        ddt_c = jnp.zeros((l, 128), f32)
        dd_row = jnp.zeros((1, 128), f32)
        for g in range(N_GROUPS):
            bsl = slice(B_OFF + D_STATE * g, B_OFF + D_STATE * (g + 1))
            csl = slice(C_OFF + D_STATE * g, C_OFF + D_STATE * (g + 1))
            bm = xs_ref[:, bsl]
            cm = xs_ref[:, csl]
            gmat = _dot(cm, bm, NT)
            dgm = jnp.zeros((l, l), f32)
            dbg = jnp.zeros((l, D_STATE), f32)
            dcg = jnp.zeros((l, D_STATE), f32)
            for r in range(HEADS_PER_GROUP):
                h = HEADS_PER_GROUP * g + r
                hsl = slice(HEAD_DIM * h, HEAD_DIM * (h + 1))
                xv = xs_ref[:, hsl].astype(f32)
                dyv = dy_ref[:, hsl].astype(f32)
                dyb = dyv.astype(bf16)
                csc = cs[:, h:h + 1]
                csr = cst[h:h + 1, :]
                cl = cs[l - 1:l, h:h + 1]
                dk = jnp.exp(jnp.where(causal, csc - csr, NEG))
                mf = gmat * dk
                dtc = dt[:, h:h + 1]
                xd = xv * dtc
                xdb = xd.astype(bf16)
                ecs = jnp.exp(csc)
                dec = jnp.exp(cl)
                e = jnp.exp(cl - csc)
                hp = hs_ref[0, h]
                hpb = hp.astype(bf16)
                dhn = dh_ref[h]
                dhnb = dhn.astype(bf16)
                dd_h = jnp.sum(jnp.sum(dyv * xv, axis=1, keepdims=True), axis=0, keepdims=True)
                dx = dyv * dsk[:, h:h + 1]
                ch = _dot(cm, hpb, NT)
                dye = dyv * ecs
                dyeb = dye.astype(bf16)
                dcg = dcg + _dot(dyeb, hpb)
                dhp = _dot(dyeb, cm, TN)
                dcs_col = jnp.sum(dye * ch, axis=1, keepdims=True)
                dm = _dot(dyb, xdb, NT)
                dxd = _dot(mf.astype(bf16), dyb, TN)
                dgm = dgm + dm * dk
                wmat = dm * mf
                dcs_col = dcs_col + jnp.sum(wmat, axis=1, keepdims=True)
                dcs_row = -jnp.sum(wmat, axis=0, keepdims=True)
                ddec = jnp.sum(jnp.sum(hp * dhn, axis=1, keepdims=True), axis=0, keepdims=True)
                dxe = _dot(bm, dhnb, NT)
                dxd = dxd + dxe * e
                de_e = jnp.sum(dxe * xd, axis=1, keepdims=True) * e
                dbg = dbg + _dot((xd * e).astype(bf16), dhnb)
                dcs_col = dcs_col - de_e
                dlast = ddec * dec + jnp.sum(de_e, axis=0, keepdims=True)
                dh_ref[h] = dhp + dec * dhn
                dx = dx + dxd * dtc
                ddt_h = jnp.sum(dxd * xv, axis=1, keepdims=True)
                is_h = lane == h
                dcs_c = dcs_c + jnp.where(is_h, dcs_col, 0.0) + jnp.where(is_h & (rowl == l - 1), dlast, 0.0)
                dcs_r = dcs_r + jnp.where(sub == h, dcs_row, 0.0)
                ddt_c = ddt_c + jnp.where(is_h, ddt_h, 0.0)
                dd_row = dd_row + jnp.where(lane1 == h, dd_h, 0.0)
                dpre_ref[:, hsl] = (dx * _dsilu(pre_ref[:, hsl].astype(f32))).astype(bf16)
            dgb = dgm.astype(bf16)
            dcg = dcg + _dot(dgb, bm)
            dbg = dbg + _dot(dgb, cm, TN)
            dpre_ref[:, bsl] = (dbg * _dsilu(pre_ref[:, bsl].astype(f32))).astype(bf16)
            dpre_ref[:, csl] = (dcg * _dsilu(pre_ref[:, csl].astype(f32))).astype(bf16)
        dcs = dcs_c + dcs_r.T
        dda = _dot((row <= col).astype(f32), dcs, precision=HIGHEST)
        ddt = ddt_c + dda * a
        ddtraw = jnp.where(lane < N_HEADS, ddt * _sigmoid(zraw), 0.0)
        ddt_ref[...] = ddtraw
        dprm_ref[0:1, :] += jnp.sum(ddtraw, axis=0, keepdims=True)
        dprm_ref[1:2, :] += jnp.where(lane1 < N_HEADS, jnp.sum(dda * dt, axis=0, keepdims=True) * a, 0.0)
        dprm_ref[2:3, :] += dd_row

    def rev(i):
        return (nc - 1 - i, 0)

    return pl.pallas_call(
        body, name="ssd_bwd", grid=(nc,),
        in_specs=[pl.BlockSpec((l, XBC_W), rev), pl.BlockSpec((l, XBC_W), rev),
                  pl.BlockSpec((l, 128), rev), pl.BlockSpec((8, 128), lambda i: (0, 0)),
                  pl.BlockSpec((1, N_HEADS, HEAD_DIM, D_STATE), lambda i: (nc - 1 - i, 0, 0, 0)),
                  pl.BlockSpec((l, D_MODEL), rev)],
        out_specs=[pl.BlockSpec((l, XBC_W), rev), pl.BlockSpec((l, 128), rev), pl.BlockSpec((8, 128), lambda i: (0, 0))],
        out_shape=[jax.ShapeDtypeStruct((s, XBC_W), bf16), jax.ShapeDtypeStruct((s, 128), f32),
                   jax.ShapeDtypeStruct((8, 128), f32)],
        scratch_shapes=[pltpu.VMEM((N_HEADS, HEAD_DIM, D_STATE), f32)],
        compiler_params=_cp("arbitrary"))(act, pre, dtf, prm, hs, dy)


def _fox_cumsum(dtf, prm):
    s = dtf.shape[0]
    l = CHUNK

    def body(f_ref, prm_ref, c_ref, carry_ref):
        @pl.when(pl.program_id(0) == 0)
        def _():
            carry_ref[...] = jnp.zeros_like(carry_ref)
        lf = _log_sigmoid(f_ref[...] + prm_ref[3:4, :])
        row = lax.broadcasted_iota(jnp.int32, (l, l), 0)
        col = lax.broadcasted_iota(jnp.int32, (l, l), 1)
        c = _dot((row >= col).astype(f32), lf, precision=HIGHEST) + carry_ref[...]
        c_ref[...] = c
        carry_ref[...] = c[l - 1:l, :]

    return pl.pallas_call(
        body, name="fox_cumsum", grid=(s // l,),
        in_specs=[pl.BlockSpec((l, 128), lambda i: (i, 0)), pl.BlockSpec((8, 128), lambda i: (0, 0))],
        out_specs=pl.BlockSpec((l, 128), lambda i: (i, 0)),
        out_shape=jax.ShapeDtypeStruct((s, 128), f32),
        scratch_shapes=[pltpu.VMEM((1, 128), f32)],
        compiler_params=_cp("arbitrary"))(dtf, prm)


def _position():
    return lax.axis_index("x"), lax.axis_index("y"), lax.axis_index("c")


def _exchange_sems(n):
    return [pltpu.SemaphoreType.DMA((n, N_DEV - 1)), pltpu.SemaphoreType.DMA((n, N_DEV - 1)),
            pltpu.SemaphoreType.DMA((n,))]


def _exchange_copies(g_refs, r_refs, send_sems, recv_sems, local_sems, gather=False):
    n = len(g_refs)
    x, y, cc = _position()
    me = 4 * x + 2 * y + cc

    def src(a, j):
        return g_refs[a] if gather else g_refs[a].at[j]

    local = [pltpu.make_async_copy(src(a, me), r_refs[a].at[me], local_sems.at[a]) for a in range(n)]
    sends, recvs = [], []
    for k in range(1, N_DEV):
        px = 1 - x if k & 4 else x
        py = 1 - y if k & 2 else y
        pc = 1 - cc if k & 1 else cc
        pid = 4 * px + 2 * py + pc
        for a in range(n):
            sends.append(pltpu.make_async_remote_copy(
                src_ref=src(a, pid), dst_ref=r_refs[a].at[me],
                send_sem=send_sems.at[a, k - 1], recv_sem=recv_sems.at[a, k - 1],
                device_id=(px, py, pc), device_id_type=pl.DeviceIdType.MESH))
            recvs.append(pltpu.make_async_remote_copy(
                src_ref=src(a, pid), dst_ref=r_refs[a].at[pid],
                send_sem=send_sems.at[a, k - 1], recv_sem=recv_sems.at[a, k - 1],
                device_id=(px, py, pc), device_id_type=pl.DeviceIdType.MESH))
    return local, sends, recvs


def _exchange_start(copies):
    local, sends, _ = copies
    for cp in local + sends:
        cp.start()


def _exchange_wait(copies):
    local, sends, recvs = copies
    for cp in recvs:
        cp.wait_recv()
    for cp in sends:
        cp.wait_send()
    for cp in local:
        cp.wait()


AUG = HEAD_DIM
N_PAIRS = N_HEADS // 2
V_BLOCK = 2 * D_MODEL // 128


def _split3(x):
    hi = x.astype(bf16)
    r1 = x - hi.astype(f32)
    mid = r1.astype(bf16)
    lo = (r1 - mid.astype(f32)).astype(bf16)
    return hi.astype(f32), mid.astype(f32), lo.astype(f32)


def _fox_prep(qkv, c):
    s = qkv.shape[0]
    ts = min(CONV_ROW_TILE, s)
    kb = D_MODEL // 128

    def body(q_ref, k_ref, c_ref, qa_ref, ka_ref):
        lane = lax.broadcasted_iota(jnp.int32, (ts, 128), 1)
        low = lane < HEAD_DIM
        for h in range(N_HEADS):
            psl = slice(128 * (h // 2), 128 * (h // 2 + 1))
            qv = q_ref[:, psl].astype(f32) * (HEAD_DIM ** -0.5)
            kv = k_ref[:, psl].astype(f32)
            if h % 2:
                qv = pltpu.roll(qv, HEAD_DIM, 1)
                kv = pltpu.roll(kv, HEAD_DIM, 1)
            hi, mid, lo = _split3(c_ref[:, F_LANE + h:F_LANE + h + 1])
            ones = jnp.where((lane >= AUG + 3) & (lane < AUG + 6), 1.0, 0.0)
            cq = jnp.where(lane == AUG, hi, jnp.where(lane == AUG + 1, mid, jnp.where(lane == AUG + 2, lo, ones)))
            qa_ref[h] = jnp.where(low, qv, cq).astype(bf16)
            onek = jnp.where((lane >= AUG) & (lane < AUG + 3), 1.0, 0.0)
            ck = jnp.where(lane == AUG + 3, -hi, jnp.where(lane == AUG + 4, -mid, jnp.where(lane == AUG + 5, -lo, onek)))
            ka_ref[h] = jnp.where(low, kv, ck).astype(bf16)

    hm = pl.BlockSpec((N_HEADS, ts, 128), lambda i: (0, i, 0))
    return pl.pallas_call(
        body, name="fox_prep", grid=(s // ts,),
        in_specs=[_rowspec(ts, D_MODEL, 0), _rowspec(ts, D_MODEL, 1), _rowspec(ts, 128)],
        out_specs=[hm, hm], out_shape=[jax.ShapeDtypeStruct((N_HEADS, s, 128), bf16)] * 2,
        compiler_params=_cp("parallel"))(qkv, qkv, c)


def _fox_fwd(qa, ka, qkv, ws):
    s = qkv.shape[0]
    t = min(ATTN_FWD_TILE, s)
    nq = s // t
    n = len(ws)

    def body(qa_ref, ka_ref, v_ref, *rest):
        w_refs, (o_ref, lse_ref), wg_refs, sems = rest[:n], rest[n:n + 2], rest[n + 2:2 * n + 2], rest[2 * n + 2:]
        qi = pl.program_id(1)
        copies = _exchange_copies(w_refs, wg_refs, *sems, gather=True)

        @pl.when((pl.program_id(0) == 0) & (qi == 0))
        def _():
            _exchange_start(copies)
        low = lax.broadcasted_iota(jnp.int32, (t, 128), 1) < HEAD_DIM
        row = lax.broadcasted_iota(jnp.int32, (t, t), 0)
        col = lax.broadcasted_iota(jnp.int32, (t, t), 1)

        def tile(ki, carry, diagonal):
            koff = pl.multiple_of(ki * t, t)
            v = v_ref[pl.ds(koff, t), :]
            vh = (jnp.where(low, v, jnp.ones_like(v)), jnp.where(low, jnp.ones_like(v), v))
            new = []
            for r in range(2):
                m_old, acc = carry[r]
                sc = _dot(qa_ref[r], ka_ref[r, pl.ds(koff, t), :], NT)
                if diagonal:
                    sc = jnp.where(col <= row, sc, NEG)
                m_new = jnp.maximum(m_old, jnp.max(sc, axis=1, keepdims=True))
                p = jnp.exp(sc - m_new)
                new.append((m_new, acc * jnp.exp(m_old - m_new) + _dot(p.astype(bf16), vh[r])))
            return tuple(new)

        init = ((jnp.full((t, 1), NEG, f32), jnp.zeros((t, 128), f32)),) * 2
        carry = lax.fori_loop(0, qi, lambda ki, cr: tile(ki, cr, False), init)
        (m_a, acc_a), (m_b, acc_b) = tile(qi, carry, True)
        l_a, l_b = acc_a[:, HEAD_DIM:HEAD_DIM + 1], acc_b[:, 0:1]
        o_ref[...] = jnp.where(low, acc_a / l_a, acc_b / l_b).astype(bf16)
        for r, lse in enumerate((m_a + jnp.log(l_a), m_b + jnp.log(l_b))):
            lse_ref[r] = jnp.broadcast_to(lse, (t, 128)).T[0:1, :]

        @pl.when((pl.program_id(0) == N_PAIRS - 1) & (qi == nq - 1))
        def _():
            _exchange_wait(copies)

    anyspec = pl.BlockSpec(memory_space=pl.ANY)
    outs = pl.pallas_call(
        body, name="fox_fwd", grid=(N_PAIRS, nq),
        in_specs=[pl.BlockSpec((2, t, 128), lambda j, qi: (j, qi, 0)),
                  pl.BlockSpec((2, s, 128), lambda j, qi: (j, 0, 0)),
                  pl.BlockSpec((s, 128), lambda j, qi: (0, V_BLOCK + j))] + [anyspec] * n,
        out_specs=[pl.BlockSpec((t, 128), lambda j, qi: (qi, j)), pl.BlockSpec((2, 1, t), lambda j, qi: (j, 0, qi))]
        + [anyspec] * n,
        out_shape=[jax.ShapeDtypeStruct((s, D_MODEL), bf16), jax.ShapeDtypeStruct((N_HEADS, 1, s), f32)]
        + [jax.ShapeDtypeStruct((N_DEV,) + w.shape, w.dtype) for w in ws],
        scratch_shapes=_exchange_sems(n),
        compiler_params=_cp("arbitrary", "arbitrary"))(qa, ka, qkv, *ws)
    return outs[0], outs[1], outs[2:]


def _fox_bwd(qa, ka, qkv, do, lse, delta, gs):
    s = qkv.shape[0]
    t = min(ATTN_TILE, s)
    nq = s // t
    n = len(gs)

    def body(qa_ref, ka_ref, v_ref, do_ref, lse_ref, dl_ref, *rest):
        g_refs, (dq_ref, dk_ref, dv_ref), r_refs, sems = rest[:n], rest[n:n + 3], rest[n + 3:2 * n + 3], rest[2 * n + 3:]
        ki = pl.program_id(1)
        copies = _exchange_copies(g_refs, r_refs, *sems)

        @pl.when((pl.program_id(0) == 0) & (ki == 0))
        def _():
            _exchange_start(copies)

        @pl.when(ki == 0)
        def _():
            dq_ref[...] = jnp.zeros_like(dq_ref)
        def block(carry, k_lo, n, qoff, diagonal):
            dks, dv = carry
            low = lax.broadcasted_iota(jnp.int32, (n, 128), 1) < HEAD_DIM
            v = v_ref[k_lo:k_lo + n, :]
            zero = jnp.zeros_like(v)
            vh = (jnp.where(low, v, zero), jnp.where(low, zero, v))
            dov = do_ref[pl.ds(qoff, n), :]
            doh = (jnp.where(low, dov, zero), jnp.where(low, zero, dov))
            new_dks = []
            for r in range(2):
                qt = qa_ref[r, pl.ds(qoff, n), :]
                kt = ka_ref[r, k_lo:k_lo + n, :]
                sct = _dot(kt, qt, NT)
                if diagonal:
                    row = lax.broadcasted_iota(jnp.int32, (n, n), 0)
                    col = lax.broadcasted_iota(jnp.int32, (n, n), 1)
                    sct = jnp.where(row <= col, sct, NEG)
                pt = jnp.exp(sct - lse_ref[r, :, pl.ds(qoff, n)])
                dpt = _dot(vh[r], dov, NT)
                dst = (pt * (dpt - dl_ref[r, :, pl.ds(qoff, n)])).astype(bf16)
                dv = dv + _dot(pt.astype(bf16), doh[r])
                new_dks.append(dks[r] + _dot(dst, qt))
                dq_ref[r, pl.ds(qoff, n), :] += _dot(dst, kt, TN)
            return tuple(new_dks), dv

        h = t // 2
        zacc = jnp.zeros((h, 128), f32)
        zero_carry = ((zacc, zacc), zacc)
        q0 = pl.multiple_of(ki * t, t)
        q1 = pl.multiple_of(ki * t + h, h)
        first = block(block(zero_carry, 0, h, q0, True), 0, h, q1, False)
        second = block(zero_carry, h, h, q1, True)
        carry = (tuple(jnp.concatenate([first[0][r], second[0][r]], axis=0) for r in range(2)),
                 jnp.concatenate([first[1], second[1]], axis=0))
        dks, dv = lax.fori_loop(
            ki + 1, nq, lambda qi, cr: block(cr, 0, t, pl.multiple_of(qi * t, t), False), carry)
        dk_ref[0] = dks[0]
        dk_ref[1] = dks[1]
        dv_ref[...] = dv.astype(bf16)

        @pl.when((pl.program_id(0) == N_PAIRS - 1) & (ki == nq - 1))
        def _():
            _exchange_wait(copies)

    anyspec = pl.BlockSpec(memory_space=pl.ANY)
    outs = pl.pallas_call(
        body, name="fox_bwd", grid=(N_PAIRS, nq),
        in_specs=[pl.BlockSpec((2, s, 128), lambda j, ki: (j, 0, 0)),
                  pl.BlockSpec((2, t, 128), lambda j, ki: (j, ki, 0)),
                  pl.BlockSpec((t, 128), lambda j, ki: (ki, V_BLOCK + j)),
                  pl.BlockSpec((s, 128), lambda j, ki: (0, j)),
                  pl.BlockSpec((2, 1, s), lambda j, ki: (j, 0, 0)),
                  pl.BlockSpec((2, 1, s), lambda j, ki: (j, 0, 0))] + [anyspec] * n,
        out_specs=[pl.BlockSpec((2, s, 128), lambda j, ki: (j, 0, 0)),
                   pl.BlockSpec((2, t, 128), lambda j, ki: (j, ki, 0)),
                   pl.BlockSpec((t, 128), lambda j, ki: (ki, j))] + [anyspec] * n,
        out_shape=[jax.ShapeDtypeStruct((N_HEADS, s, 128), f32), jax.ShapeDtypeStruct((N_HEADS, s, 128), f32),
                   jax.ShapeDtypeStruct((s, D_MODEL), bf16)] + [jax.ShapeDtypeStruct(g.shape, g.dtype) for g in gs],
        scratch_shapes=_exchange_sems(n),
        compiler_params=_cp("arbitrary", "arbitrary"))(qa, ka, qkv, do, lse, delta, *gs)
    return outs[0], outs[1], outs[2], outs[3:]


def _fox_bwd_post(dq_hm, dk_hm):
    s = dq_hm.shape[1]
    ts = min(CONV_ROW_TILE, s)

    def body(dq_ref, dk_ref, q_ref, k_ref, dc_ref):
        lane = lax.broadcasted_iota(jnp.int32, (ts, 128), 1)
        dc = jnp.zeros((ts, 128), f32)
        for h in range(N_HEADS):
            hsl = slice(HEAD_DIM * h, HEAD_DIM * (h + 1))
            dqv = dq_ref[h]
            dkv = dk_ref[h]
            q_ref[:, hsl] = (dqv[:, 0:HEAD_DIM] * (HEAD_DIM ** -0.5)).astype(bf16)
            k_ref[:, hsl] = dkv[:, 0:HEAD_DIM].astype(bf16)
            dc = dc + jnp.where(lane == F_LANE + h, dqv[:, AUG:AUG + 1] - dkv[:, AUG + 3:AUG + 4], 0.0)
        dc_ref[...] = dc

    hm = pl.BlockSpec((N_HEADS, ts, 128), lambda i: (0, i, 0))
    return pl.pallas_call(
        body, name="fox_bwd_post", grid=(s // ts,), in_specs=[hm, hm],
        out_specs=[_rowspec(ts, D_MODEL), _rowspec(ts, D_MODEL), _rowspec(ts, 128)],
        out_shape=[jax.ShapeDtypeStruct((s, D_MODEL), bf16), jax.ShapeDtypeStruct((s, D_MODEL), bf16),
                   jax.ShapeDtypeStruct((s, 128), f32)],
        compiler_params=_cp("parallel"))(dq_hm, dk_hm)


def _fox_gate_bwd(dc, dtf, prm, ddt_raw):
    s = dtf.shape[0]
    l = CHUNK
    nb = s // l

    def body(dc_ref, f_ref, prm_ref, ddt_ref, out_ref, dfb_ref, carry_ref):
        @pl.when(pl.program_id(0) == 0)
        def _():
            carry_ref[...] = jnp.zeros_like(carry_ref)
            dfb_ref[...] = jnp.zeros_like(dfb_ref)
        dc = dc_ref[...]
        row = lax.broadcasted_iota(jnp.int32, (l, l), 0)
        col = lax.broadcasted_iota(jnp.int32, (l, l), 1)
        dlf = _dot((row <= col).astype(f32), dc, precision=HIGHEST) + carry_ref[...]
        carry_ref[...] = dlf[0:1, :]
        lane = lax.broadcasted_iota(jnp.int32, (l, 128), 1)
        is_f = (lane >= F_LANE) & (lane < F_LANE + N_HEADS)
        dfr = jnp.where(is_f, dlf * _sigmoid(-(f_ref[...] + prm_ref[3:4, :])), 0.0)
        dfb_ref[...] += jnp.sum(dfr, axis=0, keepdims=True)
        out_ref[...] = ddt_ref[...] + dfr

    def rev(i):
        return (nb - 1 - i, 0)

    return pl.pallas_call(
        body, name="fox_gate_bwd", grid=(nb,),
        in_specs=[pl.BlockSpec((l, 128), rev), pl.BlockSpec((l, 128), rev), pl.BlockSpec((8, 128), lambda i: (0, 0)),
                  pl.BlockSpec((l, 128), rev)],
        out_specs=[pl.BlockSpec((l, 128), rev), pl.BlockSpec((1, 128), lambda i: (0, 0))],
        out_shape=[jax.ShapeDtypeStruct((s, 128), f32), jax.ShapeDtypeStruct((1, 128), f32)],
        scratch_shapes=[pltpu.VMEM((1, 128), f32)],
        compiler_params=_cp("arbitrary"))(dc, dtf, prm, ddt_raw)


def _all_gather(xl, name):
    r, c = xl.shape

    def body(x_ref, out_ref, send_sems, recv_sems, local_sem):
        x, y, cc = _position()
        me, sibling = (x, y, cc), (x, y, 1 - cc)
        chips = [(1 - x, y), (x, 1 - y), (1 - x, 1 - y)]

        def slot(px, py, pc):
            return out_ref.at[4 * px + 2 * py + pc]

        def copy(k, block, to, src=None):
            return pltpu.make_async_remote_copy(
                src_ref=slot(*block) if src is None else src, dst_ref=slot(*block),
                send_sem=send_sems.at[k], recv_sem=recv_sems.at[k],
                device_id=to, device_id_type=pl.DeviceIdType.MESH)

        mine = pltpu.make_async_copy(x_ref, slot(*me), local_sem)
        mine.start()
        first = [copy(0, me, sibling, src=x_ref)]
        first += [copy(1 + j, me, (*chip, cc), src=x_ref) for j, chip in enumerate(chips)]
        for cp in first:
            cp.start()
        passed = [copy(4 + j, (*chip, cc), sibling) for j, chip in enumerate(chips)]
        for j, chip in enumerate(chips):
            copy(1 + j, (*chip, cc), me).wait_recv()
            passed[j].start()
        copy(0, sibling, me).wait_recv()
        for j, chip in enumerate(chips):
            copy(4 + j, (*chip, 1 - cc), me).wait_recv()
        for cp in first + passed:
            cp.wait_send()
        mine.wait()

    return pl.pallas_call(
        body, name=name,
        out_shape=jax.ShapeDtypeStruct((N_DEV, r, c), xl.dtype),
        in_specs=[pl.BlockSpec(memory_space=pl.ANY)], out_specs=pl.BlockSpec(memory_space=pl.ANY),
        scratch_shapes=[pltpu.SemaphoreType.DMA((7,)), pltpu.SemaphoreType.DMA((7,)), pltpu.SemaphoreType.DMA],
    )(xl)


def _sum_parts(parts, name):
    n, r, c = parts.shape

    def body(p_ref, o_ref):
        g = p_ref[0]
        for i in range(1, n):
            g = g + p_ref[i]
        o_ref[...] = g

    return pl.pallas_call(body, name=name, out_shape=jax.ShapeDtypeStruct((r, c), f32))(parts)


def _adamw(w, m, v, parts, name, tr=128, by_columns=False):
    r, c = w.shape
    n = parts.shape[0]
    tr = min(tr, r)
    c1 = 1.0 - ADAM_B1 ** ADAM_STEP
    c2 = 1.0 - ADAM_B2 ** ADAM_STEP

    def body(w_ref, m_ref, v_ref, p_ref, g_ref, d_ref, nm_ref, nv_ref):
        g = p_ref[0].astype(f32)
        for i in range(1, n):
            g = g + p_ref[i].astype(f32)
        g_ref[...] = g
        nm = ADAM_B1 * m_ref[...] + (1.0 - ADAM_B1) * g
        nv = ADAM_B2 * v_ref[...] + (1.0 - ADAM_B2) * (g * g)
        nm_ref[...] = nm
        nv_ref[...] = nv
        d_ref[...] = -ADAM_LR * ((nm / c1) / (jnp.sqrt(nv / c2) + ADAM_EPS) + ADAM_WD * w_ref[...])

    if by_columns:
        blk = pl.BlockSpec((r, 128), lambda i: (0, i))
        pblk = pl.BlockSpec((n, r, 128), lambda i: (0, 0, i))
        steps = c // 128
    else:
        blk = pl.BlockSpec((tr, c), lambda i: (i, 0))
        pblk = pl.BlockSpec((n, tr, c), lambda i: (0, i, 0))
        steps = r // tr
    return pl.pallas_call(
        body, name=name, grid=(steps,),
        in_specs=[blk, blk, blk, pblk],
        out_specs=[blk] * 4, out_shape=[jax.ShapeDtypeStruct((r, c), f32)] * 4,
        compiler_params=_cp("parallel"))(w, m, v, parts)


def _lanes(w):
    return -(-w // 128) * 128


def _pack(arrs):
    rows = []
    for a in arrs:
        k, w = a.shape
        if w % 128:
            a = jnp.pad(a, ((0, 0), (0, _lanes(w) - w)))
        rows.append(a.reshape(-1, 128))
    out = jnp.concatenate(rows, axis=0)
    pad = -out.shape[0] % 8
    return jnp.pad(out, ((0, pad), (0, 0))) if pad else out


def _unpack(packed, shapes):
    outs, off = [], 0
    lead = packed.shape[:-2]
    for k, w in shapes:
        nrow = k * _lanes(w) // 128
        a = packed[..., off:off + nrow, :].reshape(*lead, k, _lanes(w))[..., :w]
        outs.append(a)
        off += nrow
    return outs


def _gathered_cols(a):
    n, k, wl = a.shape
    return jnp.transpose(a, (1, 0, 2)).reshape(k, n * wl)


def _col_shards(a):
    k, w = a.shape
    return jnp.transpose(a.reshape(k, N_DEV, w // N_DEV), (1, 0, 2))


SMALL_PARAMS = (
    ("e_norm_pre", 1, 1024, False), ("e_conv_w", 4, 2048, True), ("e_conv_b", 1, 2048, False),
    ("e_dt_bias", 1, 16, False), ("e_a_log", 1, 16, False), ("e_d_skip", 1, 16, False), ("e_fgate_b", 1, 16, False),
    ("e_ssd_norm", 1, 1024, False), ("e_norm_post", 1, 1024, False), ("o_norm_pre", 1, 1024, True),
    ("o_conv_w", 31, 2048, True), ("o_conv_b", 1, 2048, True), ("o_ln_g", 1, 2048, True), ("o_ln_b", 1, 2048, True),
    ("o_norm_post", 1, 1024, True),
)
BIG_PARAMS = ("e_w_in", "e_w_out", "o_w_in", "o_w_out")
WEIGHT_ORDER = ("e_norm_pre", "e_w_in", "e_conv_w", "e_conv_b", "e_dt_bias", "e_a_log", "e_d_skip", "e_fgate_b",
                "e_ssd_norm", "e_w_out", "e_norm_post", "o_norm_pre", "o_w_in", "o_conv_w", "o_conv_b", "o_ln_g",
                "o_ln_b", "o_w_out", "o_norm_post")
E_IN = 7200
O_IN = 6144


def kernel(x, e_norm_pre, e_w_in, e_conv_w, e_conv_b, e_dt_bias, e_a_log, e_d_skip, e_fgate_b, e_ssd_norm, e_w_out, e_norm_post, o_norm_pre, o_w_in, o_conv_w, o_conv_b, o_ln_g, o_ln_b, o_w_out, o_norm_post, loss_target, m_e_norm_pre, m_e_w_in, m_e_conv_w, m_e_conv_b, m_e_dt_bias, m_e_a_log, m_e_d_skip, m_e_fgate_b, m_e_ssd_norm, m_e_w_out, m_e_norm_post, m_o_norm_pre, m_o_w_in, m_o_conv_w, m_o_conv_b, m_o_ln_g, m_o_ln_b, m_o_w_out, m_o_norm_post, v_e_norm_pre, v_e_w_in, v_e_conv_w, v_e_conv_b, v_e_dt_bias, v_e_a_log, v_e_d_skip, v_e_fgate_b, v_e_ssd_norm, v_e_w_out, v_e_norm_post, v_o_norm_pre, v_o_w_in, v_o_conv_w, v_o_conv_b, v_o_ln_g, v_o_ln_b, v_o_w_out, v_o_norm_post):
    given = dict(locals())
    w_in = {n: given[n] for n in WEIGHT_ORDER}
    m_in = {n: given["m_" + n] for n in WEIGHT_ORDER}
    v_in = {n: given["v_" + n] for n in WEIGHT_ORDER}

    def mat(a):
        return a.reshape(a.shape[-2:])

    xs = mat(x)
    tgt = mat(loss_target)
    xi, yi, ci = _position()
    me = 4 * xi + 2 * yi + ci
    ow = O_IN // N_DEV
    wr = D_CONV // N_DEV

    ew = E_IN // N_DEV
    w_t = _all_gather(jnp.transpose(mat(e_w_in)).astype(bf16), "gather_weights").reshape(E_IN, D_MODEL)
    later_weights = [mat(e_w_out).astype(bf16), mat(o_w_in).astype(bf16), mat(o_w_out).astype(bf16)]
    w_z, w_xbc, w_qkv = w_t[0:2048], w_t[2048:4096], w_t[4112:7184]
    w_dtf = jnp.concatenate([w_t[4096:4112], w_t[7184:7200], jnp.zeros((96, D_MODEL), bf16)], axis=0)

    sharded_small = [(n, k, w) for n, k, w, sh in SMALL_PARAMS if sh]
    sg = _all_gather(_pack([mat(w_in[n]) for n, _, _ in sharded_small]), "gather_small_weights")
    full_small = {n: _gathered_cols(a)
                  for (n, _, _), a in zip(sharded_small, _unpack(sg, [(k, w // N_DEV) for _, k, w in sharded_small]))}
    for n, _, _, sh in SMALL_PARAMS:
        if not sh:
            full_small[n] = mat(w_in[n])
    p = full_small
    prm = jnp.zeros((8, 128), f32)
    prm = prm.at[0, 0:16].set(p["e_dt_bias"][0]).at[1, 0:16].set(p["e_a_log"][0]).at[2, 0:16].set(p["e_d_skip"][0])
    prm = prm.at[3, F_LANE:F_LANE + 16].set(p["e_fgate_b"][0])

    u0 = _rms_fwd(xs, p["e_norm_pre"], "rms_pre0")
    z0 = _mm_nt([(u0, 0, w_z, 0, D_MODEL)], bf16, "proj0_z", tm=1024, tn=1024)
    xraw = _mm_nt([(u0, 0, w_xbc, 0, D_MODEL)], bf16, "proj0_xbc", tm=1024, tn=1024)
    qkv = _mm_nt([(u0, 0, w_qkv, 0, D_MODEL)], bf16, "proj0_qkv", tm=1024, tn=1024)
    dtf = _mm_nt([(u0, 0, w_dtf, 0, D_MODEL)], f32, "proj0_dtf", tm=1024, tn=128)
    pre, act = _conv_ssd_fwd(xraw, p["e_conv_w"], p["e_conv_b"])
    y, hs = _ssd_fwd(act, dtf, prm)
    qa, ka = _fox_prep(qkv, _fox_cumsum(dtf, prm))
    o, lse, (e_w_out_g, o_w_in_g, o_w_out_g) = _fox_fwd(qa, ka, qkv, later_weights)
    e_w_out_f = e_w_out_g.reshape(D_CONV, D_MODEL)
    o_w_in_f = _gathered_cols(o_w_in_g)
    o_w_out_f = o_w_out_g.reshape(D_CONV, D_MODEL)
    cat = _gate0_fwd(y, z0, o, p["e_ssd_norm"])
    out0 = _mm_nn(cat, e_w_out_f, f32, "out0")
    x1, u1 = _post0_pre1(xs, out0, p["e_norm_post"], p["o_norm_pre"])

    proj1 = _mm_nn(u1, o_w_in_f, bf16, "proj1")
    hc = _conv_glu_fwd(proj1, p["o_conv_w"], p["o_conv_b"])
    h3 = _ln_gate_fwd(hc, proj1, p["o_ln_g"], p["o_ln_b"])
    out1 = _mm_nn(h3, o_w_out_f, f32, "out1")
    dy, d_out1, dg_post1, loss_part = _final_loss(x1, out1, tgt, p["o_norm_post"])

    dh3 = _mm_nt([(d_out1, 0, o_w_out_f, 0, D_MODEL)], bf16, "dh3", tm=1024, tn=D_CONV)
    g_o_w_out = _mm_tn(h3, d_out1, "dw_out1")
    dhc, dz1, dg_ln, db_ln = _ln_gate_bwd(hc, proj1, dh3, p["o_ln_g"], p["o_ln_b"])
    dval, dgate, dw_conv1, db_conv1 = _conv_glu_bwd(dhc, proj1, p["o_conv_w"])
    dproj1 = jnp.concatenate([dval, dgate, dz1], axis=1)
    du1 = _mm_nt([(dproj1, 0, o_w_in_f, 0, O_IN)], f32, "du1")
    g_o_w_in = _mm_tn(u1, dproj1, "dw_in1", tn=ow, blocked=True)
    dx1, d_out0, dg_pre1, dg_post0 = _mid_bwd(x1, du1, dy, out0, p["o_norm_pre"], p["e_norm_post"])

    dcat = _mm_nt([(d_out0, 0, e_w_out_f, 0, D_MODEL)], bf16, "dcat", tm=1024, tn=D_CONV)
    g_e_w_out = _mm_tn(cat, d_out0, "dw_out0")
    dy_ssd, do, dz0, delta, dg_ssd_norm = _gate0_bwd(y, z0, o, dcat, p["e_ssd_norm"])
    early = [g_e_w_out.reshape(N_DEV, wr, D_MODEL).astype(bf16), g_o_w_in.astype(bf16),
             g_o_w_out.reshape(N_DEV, wr, D_MODEL).astype(bf16)]
    dq_hm, dk_hm, dv, early_parts = _fox_bwd(qa, ka, qkv, do, lse, delta[0:N_HEADS].reshape(N_HEADS, 1, -1), early)
    dq, dk, dc = _fox_bwd_post(dq_hm, dk_hm)
    dpre, ddt_raw, dprm = _ssd_bwd(act, pre, dtf, prm, hs, dy_ssd)
    ddtf, dfb = _fox_gate_bwd(dc, dtf, prm, ddt_raw)
    dxraw, dw_conv0, db_conv0 = _conv_ssd_bwd(dpre, xraw, p["e_conv_w"])
    gw_dtf = _mm_tn(ddtf, u0, "dw_in0_dtf")
    g_e_w_in_t = jnp.concatenate([
        _mm_tn(dz0, u0, "dw_in0_z"), _mm_tn(dxraw, u0, "dw_in0_xbc"), gw_dtf[0:16],
        _mm_tn(dq, u0, "dw_in0_q"), _mm_tn(dk, u0, "dw_in0_k"), _mm_tn(dv, u0, "dw_in0_v"), gw_dtf[16:32]], axis=0)
    du0, last_parts = _mm_nt(
        [(dz0, 0, w_z, 0, 2048), (dxraw, 0, w_xbc, 0, 2048), (dq, 0, w_qkv, 0, 1024), (dk, 0, w_qkv, 1, 1024),
         (dv, 0, w_qkv, 2, 1024), (ddtf, 0, w_dtf, 0, 128)], f32, "du0", b_kn=True,
        gs=[g_e_w_in_t.astype(bf16).reshape(N_DEV, ew, D_MODEL)])
    grad_x, dg_pre0 = _first_bwd(xs, du0, dx1, p["e_norm_pre"])

    outs = {"e_w_in": tuple(jnp.transpose(r) for r in _adamw(
        jnp.transpose(mat(e_w_in)), jnp.transpose(mat(m_e_w_in)), jnp.transpose(mat(v_e_w_in)), last_parts[0],
        "adamw_e_w_in", by_columns=True))}
    for n, parts in zip(BIG_PARAMS[1:], early_parts):
        outs[n] = _adamw(mat(w_in[n]), mat(m_in[n]), mat(v_in[n]), parts, "adamw_" + n)

    small_grads = {
        "e_norm_pre": dg_pre0, "e_conv_w": dw_conv0, "e_conv_b": db_conv0, "e_dt_bias": dprm[0:1, 0:16],
        "e_a_log": dprm[1:2, 0:16], "e_d_skip": dprm[2:3, 0:16], "e_fgate_b": dfb[:, F_LANE:F_LANE + 16],
        "e_ssd_norm": dg_ssd_norm, "e_norm_post": dg_post0, "o_norm_pre": dg_pre1, "o_conv_w": dw_conv1,
        "o_conv_b": db_conv1, "o_ln_g": dg_ln, "o_ln_b": db_ln, "o_norm_post": dg_post1,
    }
    gathered = _all_gather(_pack([small_grads[n] for n, _, _, _ in SMALL_PARAMS] + [loss_part]), "gather_small_grads")
    summed = _unpack(_sum_parts(gathered, "sum_small_grads"), [(k, w) for _, k, w, _ in SMALL_PARAMS] + [(1, 128)])
    loss = summed[-1][0, 0]
    g_local = []
    for (n, k, w, sh), g in zip(SMALL_PARAMS, summed):
        g_local.append(lax.dynamic_slice_in_dim(g, me * (w // N_DEV), w // N_DEV, axis=1) if sh else g)
    names = [n for n, _, _, _ in SMALL_PARAMS]
    local_shapes = [(k, w // N_DEV if sh else w) for _, k, w, sh in SMALL_PARAMS]
    res = _adamw(_pack([mat(w_in[n]) for n in names]), _pack([mat(m_in[n]) for n in names]),
                 _pack([mat(v_in[n]) for n in names]), _pack(g_local)[None], "adamw_small", tr=8)
    unpacked = [_unpack(r, local_shapes) for r in res]
    for i, n in enumerate(names):
        outs[n] = tuple(u[i] for u in unpacked)

    ret = [loss, grad_x.reshape(x.shape)]
    for j in range(4):
        ret += [outs[n][j].reshape(w_in[n].shape) for n in WEIGHT_ORDER]
    return tuple(ret)
```

```python
import jax
import jax.numpy as jnp
from jax import lax
from jax.experimental import pallas as pl
from jax.experimental.pallas import tpu as pltpu

f32 = jnp.float32
bf16 = jnp.bfloat16

N_DEV = 8
D_MODEL = 1024
N_HEADS = 16
HEAD_DIM = 64
N_GROUPS = 4
HEADS_PER_GROUP = 4
D_STATE = 128
CHUNK = 512
SSD_CONV = 4
CONV_WIDTH = 31
D_CONV = 2048
EPS = 1e-6
XBC_W = 2048
B_OFF = 1024
C_OFF = 1536
F_LANE = 16
HALO = 32

ADAM_LR = 0.001
ADAM_B1 = 0.9
ADAM_B2 = 0.999
ADAM_EPS = 1e-08
ADAM_WD = 0.01
ADAM_STEP = 10

VMEM_LIMIT_BYTES = 56 * 1024 * 1024
ROW_TILE = 512
CONV_ROW_TILE = 512
CONV_COL_TILE = 512
CONV_SUB = 32
ATTN_TILE = 1024
ATTN_FWD_TILE = 1024

NT = (((1,), (1,)), ((), ()))
TN = (((0,), (0,)), ((), ()))
HIGHEST = lax.Precision.HIGHEST
NEG = -1e30


def _cp(*sem):
    return pltpu.CompilerParams(dimension_semantics=sem if sem else None, vmem_limit_bytes=VMEM_LIMIT_BYTES)


def _sigmoid(x):
    return jax.nn.sigmoid(x)


def _silu(x):
    return x * _sigmoid(x)


def _dsilu(x):
    s = _sigmoid(x)
    return s * (1.0 + x * (1.0 - s))


def _softplus(x):
    return jnp.maximum(x, 0.0) + jnp.log(1.0 + jnp.exp(-jnp.abs(x)))


def _log_sigmoid(x):
    return jnp.minimum(x, 0.0) - jnp.log(1.0 + jnp.exp(-jnp.abs(x)))


def _dot(a, b, dims=None, precision=None):
    if dims is None:
        return jnp.dot(a, b, preferred_element_type=f32, precision=precision)
    return lax.dot_general(a, b, dims, preferred_element_type=f32, precision=precision)


def _mm_nn(a, b, out_dtype, name, tm=1024, tn=1024):
    m, k = a.shape
    n = b.shape[1]
    tm, tn = min(tm, m), min(tn, n)

    def body(a_ref, b_ref, o_ref):
        o_ref[...] = _dot(a_ref[...], b_ref[...]).astype(o_ref.dtype)

    return pl.pallas_call(
        body, name=name, grid=(n // tn, m // tm),
        in_specs=[pl.BlockSpec((tm, k), lambda j, i: (i, 0)), pl.BlockSpec((k, tn), lambda j, i: (0, j))],
        out_specs=pl.BlockSpec((tm, tn), lambda j, i: (i, j)),
        out_shape=jax.ShapeDtypeStruct((m, n), out_dtype), compiler_params=_cp("parallel", "parallel"))(a, b)


def _mm_nt(pairs, out_dtype, name, tm=512, tn=512, gs=(), b_kn=False):
    m = pairs[0][0].shape[0]
    n = pairs[0][2].shape[1] if b_kn else pairs[0][2].shape[0]
    tm, tn = min(tm, m), min(tn, n)
    npair = len(pairs)
    ng = len(gs)
    grid = (n // tn, m // tm)

    def body(*refs):
        g_refs = refs[2 * npair:2 * npair + ng]
        o_ref = refs[2 * npair + ng]
        r_refs = refs[2 * npair + ng + 1:2 * npair + 2 * ng + 1]
        sems = refs[2 * npair + 2 * ng + 1:]
        if ng:
            copies = _exchange_copies(g_refs, r_refs, *sems)

            @pl.when((pl.program_id(0) == 0) & (pl.program_id(1) == 0))
            def _():
                _exchange_start(copies)
        acc = None
        for p in range(npair):
            d = _dot(refs[2 * p][...].astype(bf16), refs[2 * p + 1][...], None if b_kn else NT)
            acc = d if acc is None else acc + d
        o_ref[...] = acc.astype(o_ref.dtype)
        if ng:
            @pl.when((pl.program_id(0) == grid[0] - 1) & (pl.program_id(1) == grid[1] - 1))
            def _():
                _exchange_wait(copies)

    in_specs, args = [], []
    for a, acb, b, bcb, k in pairs:
        in_specs.append(pl.BlockSpec((tm, k), lambda j, i, acb=acb: (i, acb)))
        if b_kn:
            in_specs.append(pl.BlockSpec((k, tn), lambda j, i, bcb=bcb: (bcb, j)))
        else:
            in_specs.append(pl.BlockSpec((tn, k), lambda j, i, bcb=bcb: (j, bcb)))
        args += [a, b]
    anyspec = pl.BlockSpec(memory_space=pl.ANY)
    outs = pl.pallas_call(
        body, name=name, grid=grid, in_specs=in_specs + [anyspec] * ng,
        out_specs=[pl.BlockSpec((tm, tn), lambda j, i: (i, j))] + [anyspec] * ng,
        out_shape=[jax.ShapeDtypeStruct((m, n), out_dtype)] + [jax.ShapeDtypeStruct(g.shape, g.dtype) for g in gs],
        scratch_shapes=_exchange_sems(ng) if ng else [],
        compiler_params=_cp("arbitrary", "arbitrary") if ng else _cp("parallel", "parallel"))(*args, *gs)
    return (outs[0], outs[1:]) if ng else outs[0]


def _mm_tn(a, b, name, a_cb=0, am=None, b_cb=0, bn=None, tn=1024, tk=1024, blocked=False):
    k = a.shape[0]
    am = a.shape[1] if am is None else am
    bn = b.shape[1] if bn is None else bn
    tm = min(1024, am)
    tn, tk = min(tn, bn), min(tk, k)
    a_off, b_off = a_cb * (am // tm), b_cb * (bn // tn)

    def body(a_ref, b_ref, o_ref):
        @pl.when(pl.program_id(2) == 0)
        def _():
            o_ref[...] = jnp.zeros_like(o_ref)
        d = _dot(a_ref[...].astype(bf16), b_ref[...].astype(bf16), TN)
        o_ref[...] += d.reshape(o_ref.shape)

    if blocked:
        out_spec = pl.BlockSpec((1, tm, tn), lambda i, j, kk: (j, i, 0))
        out_shape = jax.ShapeDtypeStruct((bn // tn, am, tn), f32)
    else:
        out_spec = pl.BlockSpec((tm, tn), lambda i, j, kk: (i, j))
        out_shape = jax.ShapeDtypeStruct((am, bn), f32)
    return pl.pallas_call(
        body, name=name, grid=(am // tm, bn // tn, k // tk),
        in_specs=[pl.BlockSpec((tk, tm), lambda i, j, kk: (kk, a_off + i)),
                  pl.BlockSpec((tk, tn), lambda i, j, kk: (kk, b_off + j))],
        out_specs=out_spec, out_shape=out_shape,
        compiler_params=_cp("parallel", "parallel", "arbitrary"))(a, b)


def _rowspec(ts, w, cb=0):
    return pl.BlockSpec((ts, w), lambda i: (i, cb))


def _vecspec(w):
    return pl.BlockSpec((1, w), lambda i: (0, 0))


def _rms_fwd(x, g, name):
    s, d = x.shape
    ts = min(ROW_TILE, s)

    def body(x_ref, g_ref, u_ref):
        xv = x_ref[...]
        r = lax.rsqrt(jnp.mean(xv * xv, axis=-1, keepdims=True) + EPS)
        u_ref[...] = (xv * r * g_ref[...]).astype(bf16)

    return pl.pallas_call(
        body, name=name, grid=(s // ts,), in_specs=[_rowspec(ts, d), _vecspec(d)], out_specs=_rowspec(ts, d),
        out_shape=jax.ShapeDtypeStruct((s, d), bf16), compiler_params=_cp("parallel"))(x, g)


def _rms_bwd_vals(xv, g, dy):
    r = lax.rsqrt(jnp.mean(xv * xv, axis=-1, keepdims=True) + EPS)
    xh = xv * r
    dg = jnp.sum(dy * xh, axis=0, keepdims=True)
    dxh = dy * g
    dx = r * (dxh - xh * jnp.mean(dxh * xh, axis=-1, keepdims=True))
    return dx, dg


def _gate0_fwd(y, z, o, ssd_norm):
    s = y.shape[0]
    ts = min(ROW_TILE, s)
    gw = D_MODEL // N_GROUPS

    def body(y_ref, zs_ref, zf_ref, o_ref, w_ref, cat_ref):
        yg = y_ref[...].astype(f32) * _silu(zs_ref[...].astype(f32))
        for g in range(N_GROUPS):
            seg = yg[:, gw * g:gw * (g + 1)]
            r = lax.rsqrt(jnp.mean(seg * seg, axis=-1, keepdims=True) + EPS)
            cat_ref[:, gw * g:gw * (g + 1)] = (seg * r * w_ref[:, gw * g:gw * (g + 1)]).astype(bf16)
        cat_ref[:, D_MODEL:] = (o_ref[...].astype(f32) * _silu(zf_ref[...].astype(f32))).astype(bf16)

    return pl.pallas_call(
        body, name="gate0_fwd", grid=(s // ts,),
        in_specs=[_rowspec(ts, D_MODEL), _rowspec(ts, D_MODEL, 0), _rowspec(ts, D_MODEL, 1), _rowspec(ts, D_MODEL),
                  _vecspec(D_MODEL)],
        out_specs=_rowspec(ts, 2 * D_MODEL),
        out_shape=jax.ShapeDtypeStruct((s, 2 * D_MODEL), bf16), compiler_params=_cp("parallel"))(y, z, z, o, ssd_norm)


def _post0_pre1(x, out0, g_post0, g_pre1):
    s, d = x.shape
    ts = min(ROW_TILE, s)

    def body(x_ref, o_ref, gp_ref, gn_ref, x1_ref, u1_ref):
        ov = o_ref[...]
        r = lax.rsqrt(jnp.mean(ov * ov, axis=-1, keepdims=True) + EPS)
        x1 = x_ref[...] + ov * r * gp_ref[...]
        x1_ref[...] = x1
        r1 = lax.rsqrt(jnp.mean(x1 * x1, axis=-1, keepdims=True) + EPS)
        u1_ref[...] = (x1 * r1 * gn_ref[...]).astype(bf16)

    return pl.pallas_call(
        body, name="post0_pre1", grid=(s // ts,),
        in_specs=[_rowspec(ts, d), _rowspec(ts, d), _vecspec(d), _vecspec(d)],
        out_specs=[_rowspec(ts, d), _rowspec(ts, d)],
        out_shape=[jax.ShapeDtypeStruct((s, d), f32), jax.ShapeDtypeStruct((s, d), bf16)],
        compiler_params=_cp("parallel"))(x, out0, g_post0, g_pre1)


def _ln_vals(hc, g, b):
    mu = jnp.mean(hc, axis=-1, keepdims=True)
    xc = hc - mu
    rstd = lax.rsqrt(jnp.mean(xc * xc, axis=-1, keepdims=True) + EPS)
    xh = xc * rstd
    return xh, rstd, xh * g + b


def _ln_gate_fwd(hc, proj1, ln_g, ln_b):
    s = hc.shape[0]
    ts = min(ROW_TILE, s)

    def body(hc_ref, z_ref, g_ref, b_ref, h3_ref):
        _, _, ln = _ln_vals(hc_ref[...].astype(f32), g_ref[...], b_ref[...])
        h3_ref[...] = (_silu(ln) * _silu(z_ref[...].astype(f32))).astype(bf16)

    return pl.pallas_call(
        body, name="ln_gate_fwd", grid=(s // ts,),
        in_specs=[_rowspec(ts, D_CONV), _rowspec(ts, D_CONV, 2), _vecspec(D_CONV), _vecspec(D_CONV)],
        out_specs=_rowspec(ts, D_CONV),
        out_shape=jax.ShapeDtypeStruct((s, D_CONV), bf16), compiler_params=_cp("parallel"))(hc, proj1, ln_g, ln_b)


def _final_loss(x1, out1, tgt, g_post1):
    s, d = x1.shape
    ts = min(ROW_TILE, s)

    def body(x1_ref, o_ref, t_ref, g_ref, dy_ref, do_ref, dg_ref, loss_ref):
        i = pl.program_id(0)

        @pl.when(i == 0)
        def _():
            dg_ref[...] = jnp.zeros_like(dg_ref)
            loss_ref[...] = jnp.zeros_like(loss_ref)
        ov = o_ref[...]
        g = g_ref[...]
        r = lax.rsqrt(jnp.mean(ov * ov, axis=-1, keepdims=True) + EPS)
        diff = x1_ref[...] + ov * r * g - t_ref[...]
        row = jnp.mean(diff * diff, axis=-1, keepdims=True)
        loss_ref[...] += jnp.broadcast_to(0.5 * jnp.sum(row, axis=0, keepdims=True), loss_ref.shape)
        dy = diff * (1.0 / d)
        dy_ref[...] = dy
        dx, dg = _rms_bwd_vals(ov, g, dy)
        do_ref[...] = dx.astype(bf16)
        dg_ref[...] += dg

    return pl.pallas_call(
        body, name="final_loss", grid=(s // ts,),
        in_specs=[_rowspec(ts, d), _rowspec(ts, d), _rowspec(ts, d), _vecspec(d)],
        out_specs=[_rowspec(ts, d), _rowspec(ts, d), _vecspec(d), _vecspec(128)],
        out_shape=[jax.ShapeDtypeStruct((s, d), f32), jax.ShapeDtypeStruct((s, d), bf16),
                   jax.ShapeDtypeStruct((1, d), f32), jax.ShapeDtypeStruct((1, 128), f32)],
        compiler_params=_cp("arbitrary"))(x1, out1, tgt, g_post1)


def _ln_gate_bwd(hc, proj1, dh3, ln_g, ln_b):
    s = hc.shape[0]
    ts = min(ROW_TILE, s)

    def body(hc_ref, z_ref, dh_ref, g_ref, b_ref, dhc_ref, dz_ref, dg_ref, db_ref):
        @pl.when(pl.program_id(0) == 0)
        def _():
            dg_ref[...] = jnp.zeros_like(dg_ref)
            db_ref[...] = jnp.zeros_like(db_ref)
        g = g_ref[...]
        xh, rstd, ln = _ln_vals(hc_ref[...].astype(f32), g, b_ref[...])
        zv = z_ref[...].astype(f32)
        dh3 = dh_ref[...].astype(f32)
        dz_ref[...] = (dh3 * _silu(ln) * _dsilu(zv)).astype(bf16)
        dln = dh3 * _silu(zv) * _dsilu(ln)
        dg_ref[...] += jnp.sum(dln * xh, axis=0, keepdims=True)
        db_ref[...] += jnp.sum(dln, axis=0, keepdims=True)
        dxh = dln * g
        dhc = rstd * (dxh - jnp.mean(dxh, axis=-1, keepdims=True) - xh * jnp.mean(dxh * xh, axis=-1, keepdims=True))
        dhc_ref[...] = dhc.astype(bf16)

    return pl.pallas_call(
        body, name="ln_gate_bwd", grid=(s // ts,),
        in_specs=[_rowspec(ts, D_CONV), _rowspec(ts, D_CONV, 2), _rowspec(ts, D_CONV), _vecspec(D_CONV),
                  _vecspec(D_CONV)],
        out_specs=[_rowspec(ts, D_CONV), _rowspec(ts, D_CONV), _vecspec(D_CONV), _vecspec(D_CONV)],
        out_shape=[jax.ShapeDtypeStruct((s, D_CONV), bf16), jax.ShapeDtypeStruct((s, D_CONV), bf16),
                   jax.ShapeDtypeStruct((1, D_CONV), f32), jax.ShapeDtypeStruct((1, D_CONV), f32)],
        compiler_params=_cp("arbitrary"))(hc, proj1, dh3, ln_g, ln_b)


def _mid_bwd(x1, du1, dy, out0, g_pre1, g_post0):
    s, d = x1.shape
    ts = min(ROW_TILE, s)

    def body(x1_ref, du_ref, dy_ref, o_ref, gn_ref, gp_ref, dx1_ref, do_ref, dgn_ref, dgp_ref):
        @pl.when(pl.program_id(0) == 0)
        def _():
            dgn_ref[...] = jnp.zeros_like(dgn_ref)
            dgp_ref[...] = jnp.zeros_like(dgp_ref)
        dxa, dgn = _rms_bwd_vals(x1_ref[...], gn_ref[...], du_ref[...])
        dx1 = dy_ref[...] + dxa
        dx1_ref[...] = dx1
        dgn_ref[...] += dgn
        dxo, dgp = _rms_bwd_vals(o_ref[...], gp_ref[...], dx1)
        do_ref[...] = dxo.astype(bf16)
        dgp_ref[...] += dgp

    return pl.pallas_call(
        body, name="mid_bwd", grid=(s // ts,),
        in_specs=[_rowspec(ts, d)] * 4 + [_vecspec(d), _vecspec(d)],
        out_specs=[_rowspec(ts, d), _rowspec(ts, d), _vecspec(d), _vecspec(d)],
        out_shape=[jax.ShapeDtypeStruct((s, d), f32), jax.ShapeDtypeStruct((s, d), bf16),
                   jax.ShapeDtypeStruct((1, d), f32), jax.ShapeDtypeStruct((1, d), f32)],
        compiler_params=_cp("arbitrary"))(x1, du1, dy, out0, g_pre1, g_post0)


def _first_bwd(x, du0, dx1, g_pre0):
    s, d = x.shape
    ts = min(ROW_TILE, s)

    def body(x_ref, du_ref, dx1_ref, g_ref, dx_ref, dg_ref):
        @pl.when(pl.program_id(0) == 0)
        def _():
            dg_ref[...] = jnp.zeros_like(dg_ref)
        dxa, dg = _rms_bwd_vals(x_ref[...], g_ref[...], du_ref[...])
        dx_ref[...] = dx1_ref[...] + dxa
        dg_ref[...] += dg

    return pl.pallas_call(
        body, name="first_bwd", grid=(s // ts,),
        in_specs=[_rowspec(ts, d)] * 3 + [_vecspec(d)],
        out_specs=[_rowspec(ts, d), _vecspec(d)],
        out_shape=[jax.ShapeDtypeStruct((s, d), f32), jax.ShapeDtypeStruct((1, d), f32)],
        compiler_params=_cp("arbitrary"))(x, du0, dx1, g_pre0)


def _gate0_bwd(y, z, o, dcat, ssd_norm):
    s = y.shape[0]
    ts = min(ROW_TILE, s)
    gw = D_MODEL // N_GROUPS

    def body(y_ref, zs_ref, zf_ref, o_ref, dn_ref, dg_ref, w_ref, dy_ref, do_ref, dz_ref, delta_ref, dw_ref):
        @pl.when(pl.program_id(0) == 0)
        def _():
            dw_ref[...] = jnp.zeros_like(dw_ref)
        yv = y_ref[...].astype(f32)
        zs = zs_ref[...].astype(f32)
        sz = _silu(zs)
        yg = yv * sz
        dyn = dn_ref[...].astype(f32)
        for g in range(N_GROUPS):
            sl = slice(gw * g, gw * (g + 1))
            seg = yg[:, sl]
            r = lax.rsqrt(jnp.mean(seg * seg, axis=-1, keepdims=True) + EPS)
            yh = seg * r
            dn = dyn[:, sl]
            dw_ref[:, sl] += jnp.sum(dn * yh, axis=0, keepdims=True)
            dyh = dn * w_ref[:, sl]
            dyg = r * (dyh - yh * jnp.mean(dyh * yh, axis=-1, keepdims=True))
            dy_ref[:, sl] = (dyg * sz[:, sl]).astype(bf16)
            dz_ref[:, sl] = (dyg * yv[:, sl] * _dsilu(zs[:, sl])).astype(bf16)
        zf = zf_ref[...].astype(f32)
        ov = o_ref[...].astype(f32)
        dog = dg_ref[...].astype(f32)
        dov = (dog * _silu(zf)).astype(bf16)
        do_ref[...] = dov
        dz_ref[:, D_MODEL:] = (dog * ov * _dsilu(zf)).astype(bf16)
        prod = dov.astype(f32) * ov
        lane = lax.broadcasted_iota(jnp.int32, (ts, 128), 1)
        delta = jnp.zeros((ts, 128), f32)
        for h in range(N_HEADS):
            dh = jnp.sum(prod[:, HEAD_DIM * h:HEAD_DIM * (h + 1)], axis=-1, keepdims=True)
            delta = delta + jnp.where(lane == h, dh, 0.0)
        delta_ref[...] = delta.T

    return pl.pallas_call(
        body, name="gate0_bwd", grid=(s // ts,),
        in_specs=[_rowspec(ts, D_MODEL), _rowspec(ts, D_MODEL, 0), _rowspec(ts, D_MODEL, 1), _rowspec(ts, D_MODEL),
                  _rowspec(ts, D_MODEL, 0), _rowspec(ts, D_MODEL, 1), _vecspec(D_MODEL)],
        out_specs=[_rowspec(ts, D_MODEL), _rowspec(ts, D_MODEL), _rowspec(ts, 2 * D_MODEL),
                   pl.BlockSpec((128, ts), lambda i: (0, i)), _vecspec(D_MODEL)],
        out_shape=[jax.ShapeDtypeStruct((s, D_MODEL), bf16), jax.ShapeDtypeStruct((s, D_MODEL), bf16),
                   jax.ShapeDtypeStruct((s, 2 * D_MODEL), bf16), jax.ShapeDtypeStruct((128, s), f32),
                   jax.ShapeDtypeStruct((1, D_MODEL), f32)],
        compiler_params=_cp("arbitrary"))(y, z, z, o, dcat, dcat, ssd_norm)


def _conv_grid(s, c):
    ts, cb = min(CONV_ROW_TILE, s), min(CONV_COL_TILE, c)
    return ts, cb, (c // cb, s // ts)


def _cur(ts, cb, off=0):
    return pl.BlockSpec((ts, cb), lambda c, i: (i, c + off))


def _prev_halo(ts, cb, off=0):
    return pl.BlockSpec((HALO, cb), lambda c, i: (jnp.maximum(i * (ts // HALO) - 1, 0), c + off))


def _next_halo(ts, cb, s, off=0):
    return pl.BlockSpec((HALO, cb), lambda c, i: (jnp.minimum((i + 1) * (ts // HALO), s // HALO - 1), c + off))


def _wspec(k, cb):
    return pl.BlockSpec((k, cb), lambda c, i: (0, c))


def _phases(offsets):
    return sorted({o % 8 for o in offsets} - {0})


def _shift_scratch(offsets, ts, cb):
    return pltpu.VMEM((max(len(_phases(offsets)), 1), ts + HALO - 8, cb), f32)


def _fill_phases(ext_ref, sh_ref, offsets, ts):
    for j, r in enumerate(_phases(offsets)):
        sh_ref[j] = ext_ref[pl.ds(r, ts + HALO - 8), :]


def _slab(ext_ref, sh_ref, offsets, off, start):
    r = off % 8
    a = off - r + start
    if r == 0:
        return ext_ref[a:a + CONV_SUB, :]
    return sh_ref[_phases(offsets).index(r), a:a + CONV_SUB, :]


def _conv_taps(ext_ref, sh_ref, w_ref, b_ref, ts, k_taps, emit):
    offsets = [HALO - (k_taps - 1) + k for k in range(k_taps)]
    _fill_phases(ext_ref, sh_ref, offsets, ts)
    for sb in range(ts // CONV_SUB):
        acc = b_ref[...]
        for k in range(k_taps):
            acc = acc + w_ref[k:k + 1, :] * _slab(ext_ref, sh_ref, offsets, offsets[k], sb * CONV_SUB)
        emit(slice(sb * CONV_SUB, (sb + 1) * CONV_SUB), acc)


def _conv_ssd_fwd(xraw, w, b):
    s, c = xraw.shape
    ts, cb, grid = _conv_grid(s, c)
    offsets = [HALO - (SSD_CONV - 1) + k for k in range(SSD_CONV)]

    def body(x_ref, xh_ref, w_ref, b_ref, pre_ref, act_ref, ext_ref, sh_ref):
        first = pl.program_id(1) == 0
        ext_ref[0:HALO, :] = jnp.where(first, 0.0, xh_ref[...].astype(f32))
        ext_ref[HALO:, :] = x_ref[...].astype(f32)

        def emit(rows, pre):
            pre_ref[rows, :] = pre.astype(bf16)
            act_ref[rows, :] = _silu(pre).astype(bf16)
        _conv_taps(ext_ref, sh_ref, w_ref, b_ref, ts, SSD_CONV, emit)

    return pl.pallas_call(
        body, name="conv_ssd_fwd", grid=grid,
        in_specs=[_cur(ts, cb), _prev_halo(ts, cb), _wspec(SSD_CONV, cb), _wspec(1, cb)],
        out_specs=[_cur(ts, cb), _cur(ts, cb)],
        out_shape=[jax.ShapeDtypeStruct((s, c), bf16)] * 2,
        scratch_shapes=[pltpu.VMEM((HALO + ts, cb), f32), _shift_scratch(offsets, ts, cb)],
        compiler_params=_cp("parallel", "parallel"))(xraw, xraw, w, b)


def _conv_glu_fwd(proj1, w, b):
    s = proj1.shape[0]
    c = D_CONV
    ts, cb, grid = _conv_grid(s, c)
    goff = c // cb

    offsets = [HALO - (CONV_WIDTH - 1) + k for k in range(CONV_WIDTH)]

    def body(v_ref, g_ref, vh_ref, gh_ref, w_ref, b_ref, hc_ref, ext_ref, sh_ref):
        first = pl.program_id(1) == 0
        hh = vh_ref[...].astype(f32) * _sigmoid(gh_ref[...].astype(f32))
        ext_ref[0:HALO, :] = jnp.where(first, 0.0, hh)
        ext_ref[HALO:, :] = v_ref[...].astype(f32) * _sigmoid(g_ref[...].astype(f32))

        def emit(rows, hc):
            hc_ref[rows, :] = hc.astype(bf16)
        _conv_taps(ext_ref, sh_ref, w_ref, b_ref, ts, CONV_WIDTH, emit)

    return pl.pallas_call(
        body, name="conv_glu_fwd", grid=grid,
        in_specs=[_cur(ts, cb), _cur(ts, cb, goff), _prev_halo(ts, cb), _prev_halo(ts, cb, goff),
                  _wspec(CONV_WIDTH, cb), _wspec(1, cb)],
        out_specs=_cur(ts, cb),
        out_shape=jax.ShapeDtypeStruct((s, c), bf16),
        scratch_shapes=[pltpu.VMEM((HALO + ts, cb), f32), _shift_scratch(offsets, ts, cb)],
        compiler_params=_cp("parallel", "parallel"))(proj1, proj1, proj1, proj1, w, b)


def _conv_bwd_offsets(k_taps):
    return [k_taps - 1 - k for k in range(k_taps)], [HALO - (k_taps - 1) + k for k in range(k_taps)]


def _conv_bwd_scratch(k_taps, ts, cb):
    d_offs, x_offs = _conv_bwd_offsets(k_taps)
    return [pltpu.VMEM((ts + HALO, cb), f32), _shift_scratch(d_offs, ts, cb),
            pltpu.VMEM((HALO + ts, cb), f32), _shift_scratch(x_offs, ts, cb),
            pltpu.VMEM((k_taps, 8, cb), f32), pltpu.VMEM((8, cb), f32)]


def _conv_bwd_core(dp, dpn_ref, last, w_ref, scratch, dw_ref, db_ref, ts, k_taps, emit):
    dext_ref, dsh_ref, xext_ref, xsh_ref, dw8_ref, db8_ref = scratch
    d_offs, x_offs = _conv_bwd_offsets(k_taps)
    dext_ref[0:ts, :] = dp
    dext_ref[ts:, :] = jnp.where(last, 0.0, dpn_ref[...].astype(f32))
    _fill_phases(dext_ref, dsh_ref, d_offs, ts)
    _fill_phases(xext_ref, xsh_ref, x_offs, ts)

    @pl.when(pl.program_id(1) == 0)
    def _():
        dw8_ref[...] = jnp.zeros_like(dw8_ref)
        db8_ref[...] = jnp.zeros_like(db8_ref)
    cb = dp.shape[1]
    for sb in range(ts // CONV_SUB):
        start = sb * CONV_SUB
        dpv = dext_ref[start:start + CONV_SUB, :]
        dx = None
        for k in range(k_taps):
            t = w_ref[k:k + 1, :] * _slab(dext_ref, dsh_ref, d_offs, d_offs[k], start)
            dx = t if dx is None else dx + t
            prod = dpv * _slab(xext_ref, xsh_ref, x_offs, x_offs[k], start)
            dw8_ref[k] += jnp.sum(prod.reshape(CONV_SUB // 8, 8, cb), axis=0)
        db8_ref[...] += jnp.sum(dpv.reshape(CONV_SUB // 8, 8, cb), axis=0)
        emit(slice(start, start + CONV_SUB), dx)

    @pl.when(last)
    def _():
        dw_ref[...] = jnp.sum(dw8_ref[...], axis=1)
        db_ref[...] = jnp.sum(db8_ref[...], axis=0, keepdims=True)


def _conv_ssd_bwd(dpre, xraw, w):
    s, c = xraw.shape
    ts, cb, grid = _conv_grid(s, c)
    nb = s // ts

    def body(dp_ref, dpn_ref, x_ref, xh_ref, w_ref, dx_ref, dw_ref, db_ref, *scratch):
        i = pl.program_id(1)
        xext_ref = scratch[2]
        xext_ref[0:HALO, :] = jnp.where(i == 0, 0.0, xh_ref[...].astype(f32))
        xext_ref[HALO:, :] = x_ref[...].astype(f32)

        def emit(rows, dx):
            dx_ref[rows, :] = dx.astype(bf16)
        _conv_bwd_core(dp_ref[...].astype(f32), dpn_ref, i == nb - 1, w_ref, scratch, dw_ref, db_ref, ts, SSD_CONV, emit)

    return pl.pallas_call(
        body, name="conv_ssd_bwd", grid=grid,
        in_specs=[_cur(ts, cb), _next_halo(ts, cb, s), _cur(ts, cb), _prev_halo(ts, cb), _wspec(SSD_CONV, cb)],
        out_specs=[_cur(ts, cb), _wspec(SSD_CONV, cb), _wspec(1, cb)],
        out_shape=[jax.ShapeDtypeStruct((s, c), bf16), jax.ShapeDtypeStruct((SSD_CONV, c), f32),
                   jax.ShapeDtypeStruct((1, c), f32)],
        scratch_shapes=_conv_bwd_scratch(SSD_CONV, ts, cb),
        compiler_params=_cp("parallel", "arbitrary"))(dpre, dpre, xraw, xraw, w)


def _conv_glu_bwd(dhc, proj1, w):
    s = proj1.shape[0]
    c = D_CONV
    ts, cb, grid = _conv_grid(s, c)
    nb = s // ts
    goff = c // cb

    def body(dp_ref, dpn_ref, v_ref, g_ref, vh_ref, gh_ref, w_ref, dv_ref, dg_ref, dw_ref, db_ref, *scratch):
        i = pl.program_id(1)
        xext_ref = scratch[2]
        xext_ref[0:HALO, :] = jnp.where(i == 0, 0.0, vh_ref[...].astype(f32) * _sigmoid(gh_ref[...].astype(f32)))
        xext_ref[HALO:, :] = v_ref[...].astype(f32) * _sigmoid(g_ref[...].astype(f32))

        def emit(rows, dh):
            val = v_ref[rows, :].astype(f32)
            sg = _sigmoid(g_ref[rows, :].astype(f32))
            dv_ref[rows, :] = (dh * sg).astype(bf16)
            dg_ref[rows, :] = (dh * val * sg * (1.0 - sg)).astype(bf16)
        _conv_bwd_core(dp_ref[...].astype(f32), dpn_ref, i == nb - 1, w_ref, scratch, dw_ref, db_ref, ts, CONV_WIDTH, emit)

    return pl.pallas_call(
        body, name="conv_glu_bwd", grid=grid,
        in_specs=[_cur(ts, cb), _next_halo(ts, cb, s), _cur(ts, cb), _cur(ts, cb, goff), _prev_halo(ts, cb),
                  _prev_halo(ts, cb, goff), _wspec(CONV_WIDTH, cb)],
        out_specs=[_cur(ts, cb), _cur(ts, cb), _wspec(CONV_WIDTH, cb), _wspec(1, cb)],
        out_shape=[jax.ShapeDtypeStruct((s, c), bf16), jax.ShapeDtypeStruct((s, c), bf16),
                   jax.ShapeDtypeStruct((CONV_WIDTH, c), f32), jax.ShapeDtypeStruct((1, c), f32)],
        scratch_shapes=_conv_bwd_scratch(CONV_WIDTH, ts, cb),
        compiler_params=_cp("parallel", "arbitrary"))(dhc, dhc, proj1, proj1, proj1, proj1, w)


def _ssd_common(dt_ref, prm_ref):
    l = CHUNK
    dtb = prm_ref[0:1, :]
    a = -jnp.exp(prm_ref[1:2, :])
    dsk = prm_ref[2:3, :]
    zraw = dt_ref[...] + dtb
    dt = _softplus(zraw)
    da = dt * a
    row = lax.broadcasted_iota(jnp.int32, (l, l), 0)
    col = lax.broadcasted_iota(jnp.int32, (l, l), 1)
    causal = row >= col
    cs = _dot(causal.astype(f32), da, precision=HIGHEST)
    return a, dsk, zraw, dt, cs, cs.T, causal, row, col


def _ssd_fwd(act, dtf, prm):
    s = act.shape[0]
    nc = s // CHUNK
    l = CHUNK

    def body(xs_ref, dt_ref, prm_ref, y_ref, hs_ref, st_ref):
        @pl.when(pl.program_id(0) == 0)
        def _():
            st_ref[...] = jnp.zeros_like(st_ref)
        a, dsk, _, dt, cs, cst, causal, _, _ = _ssd_common(dt_ref, prm_ref)
        for g in range(N_GROUPS):
            bm = xs_ref[:, B_OFF + D_STATE * g:B_OFF + D_STATE * (g + 1)]
            cm = xs_ref[:, C_OFF + D_STATE * g:C_OFF + D_STATE * (g + 1)]
            gmat = _dot(cm, bm, NT)
            for r in range(HEADS_PER_GROUP):
                h = HEADS_PER_GROUP * g + r
                hsl = slice(HEAD_DIM * h, HEAD_DIM * (h + 1))
                xv = xs_ref[:, hsl].astype(f32)
                csc = cs[:, h:h + 1]
                csr = cst[h:h + 1, :]
                cl = cs[l - 1:l, h:h + 1]
                dk = jnp.exp(jnp.where(causal, csc - csr, NEG))
                xd = xv * dt[:, h:h + 1]
                hp = st_ref[h]
                hs_ref[0, h] = hp
                ydiag = _dot((gmat * dk).astype(bf16), xd.astype(bf16))
                yoff = _dot(cm, hp.astype(bf16), NT) * jnp.exp(csc)
                y_ref[:, hsl] = (ydiag + yoff + xv * dsk[:, h:h + 1]).astype(bf16)
                st = _dot((xd * jnp.exp(cl - csc)).astype(bf16), bm, TN)
                st_ref[h] = hp * jnp.exp(cl) + st

    return pl.pallas_call(
        body, name="ssd_fwd", grid=(nc,),
        in_specs=[pl.BlockSpec((l, XBC_W), lambda i: (i, 0)), pl.BlockSpec((l, 128), lambda i: (i, 0)),
                  pl.BlockSpec((8, 128), lambda i: (0, 0))],
        out_specs=[pl.BlockSpec((l, D_MODEL), lambda i: (i, 0)),
                   pl.BlockSpec((1, N_HEADS, HEAD_DIM, D_STATE), lambda i: (i, 0, 0, 0))],
        out_shape=[jax.ShapeDtypeStruct((s, D_MODEL), bf16),
                   jax.ShapeDtypeStruct((nc, N_HEADS, HEAD_DIM, D_STATE), f32)],
        scratch_shapes=[pltpu.VMEM((N_HEADS, HEAD_DIM, D_STATE), f32)],
        compiler_params=_cp("arbitrary"))(act, dtf, prm)


def _ssd_bwd(act, pre, dtf, prm, hs, dy):
    s = act.shape[0]
    nc = s // CHUNK
    l = CHUNK

    def body(xs_ref, pre_ref, dt_ref, prm_ref, hs_ref, dy_ref, dpre_ref, ddt_ref, dprm_ref, dh_ref):
        @pl.when(pl.program_id(0) == 0)
        def _():
            dh_ref[...] = jnp.zeros_like(dh_ref)
            dprm_ref[...] = jnp.zeros_like(dprm_ref)
        a, dsk, zraw, dt, cs, cst, causal, row, col = _ssd_common(dt_ref, prm_ref)
        lane = lax.broadcasted_iota(jnp.int32, (l, 128), 1)
        rowl = lax.broadcasted_iota(jnp.int32, (l, 128), 0)
        sub = lax.broadcasted_iota(jnp.int32, (128, l), 0)
        lane1 = lax.broadcasted_iota(jnp.int32, (1, 128), 1)
        dcs_c = jnp.zeros((l, 128), f32)
        dcs_r = jnp.zeros((128, l), f32)
        ddt_c = jnp.zeros((l, 128), f32)
        dd_row = jnp.zeros((1, 128), f32)
        for g in range(N_GROUPS):
            bsl = slice(B_OFF + D_STATE * g, B_OFF + D_STATE * (g + 1))
            csl = slice(C_OFF + D_STATE * g, C_OFF + D_STATE * (g + 1))
            bm = xs_ref[:, bsl]
            cm = xs_ref[:, csl]
            gmat = _dot(cm, bm, NT)
            gmat_t = _dot(bm, cm, NT)
            dgm = jnp.zeros((l, l), f32)
            dbg = jnp.zeros((l, D_STATE), f32)
            dcg = jnp.zeros((l, D_STATE), f32)
            for r in range(HEADS_PER_GROUP):
                h = HEADS_PER_GROUP * g + r
                hsl = slice(HEAD_DIM * h, HEAD_DIM * (h + 1))
                xv = xs_ref[:, hsl].astype(f32)
                dyv = dy_ref[:, hsl].astype(f32)
                dyb = dyv.astype(bf16)
                csc = cs[:, h:h + 1]
                csr = cst[h:h + 1, :]
                cl = cs[l - 1:l, h:h + 1]
                dk = jnp.exp(jnp.where(causal, csc - csr, NEG))
                mf = gmat * dk
                dtc = dt[:, h:h + 1]
                xd = xv * dtc
                xdb = xd.astype(bf16)
                ecs = jnp.exp(csc)
                dec = jnp.exp(cl)
                e = jnp.exp(cl - csc)
                hp = hs_ref[0, h]
                hpb = hp.astype(bf16)
                dhn = dh_ref[h]
                dhnb = dhn.astype(bf16)
                dd_h = jnp.sum(jnp.sum(dyv * xv, axis=1, keepdims=True), axis=0, keepdims=True)
                dx = dyv * dsk[:, h:h + 1]
                ch = _dot(cm, hpb, NT)
                dye = dyv * ecs
                dyeb = dye.astype(bf16)
                dcg = dcg + _dot(dyeb, hpb)
                dhp = _dot(dyeb, cm, TN)
                dcs_col = jnp.sum(dye * ch, axis=1, keepdims=True)
                dm = _dot(dyb, xdb, NT)
                mft = gmat_t * jnp.exp(jnp.where(row <= col, csr - csc, NEG))
                dxd = _dot(mft.astype(bf16), dyb)
                dgm = dgm + dm * dk
                wmat = dm * mf
                dcs_col = dcs_col + jnp.sum(wmat, axis=1, keepdims=True)
                dcs_row = -jnp.sum(wmat, axis=0, keepdims=True)
                ddec = jnp.sum(jnp.sum(hp * dhn, axis=1, keepdims=True), axis=0, keepdims=True)
                dxe = _dot(bm, dhnb, NT)
                dxd = dxd + dxe * e
                de_e = jnp.sum(dxe * xd, axis=1, keepdims=True) * e
                dbg = dbg + _dot((xd * e).astype(bf16), dhnb)
                dcs_col = dcs_col - de_e
                dlast = ddec * dec + jnp.sum(de_e, axis=0, keepdims=True)
                dh_ref[h] = dhp + dec * dhn
                dx = dx + dxd * dtc
                ddt_h = jnp.sum(dxd * xv, axis=1, keepdims=True)
                is_h = lane == h
                dcs_c = dcs_c + jnp.where(is_h, dcs_col, 0.0) + jnp.where(is_h & (rowl == l - 1), dlast, 0.0)
                dcs_r = dcs_r + jnp.where(sub == h, dcs_row, 0.0)
                ddt_c = ddt_c + jnp.where(is_h, ddt_h, 0.0)
                dd_row = dd_row + jnp.where(lane1 == h, dd_h, 0.0)
                dpre_ref[:, hsl] = (dx * _dsilu(pre_ref[:, hsl].astype(f32))).astype(bf16)
            dgb = dgm.astype(bf16)
            dcg = dcg + _dot(dgb, bm)
            dbg = dbg + _dot(dgb, cm, TN)
            dpre_ref[:, bsl] = (dbg * _dsilu(pre_ref[:, bsl].astype(f32))).astype(bf16)
            dpre_ref[:, csl] = (dcg * _dsilu(pre_ref[:, csl].astype(f32))).astype(bf16)
        dcs = dcs_c + dcs_r.T
        dda = _dot((row <= col).astype(f32), dcs, precision=HIGHEST)
        ddt = ddt_c + dda * a
        ddtraw = jnp.where(lane < N_HEADS, ddt * _sigmoid(zraw), 0.0)
        ddt_ref[...] = ddtraw
        dprm_ref[0:1, :] += jnp.sum(ddtraw, axis=0, keepdims=True)
        dprm_ref[1:2, :] += jnp.where(lane1 < N_HEADS, jnp.sum(dda * dt, axis=0, keepdims=True) * a, 0.0)
        dprm_ref[2:3, :] += dd_row

    def rev(i):
        return (nc - 1 - i, 0)

    return pl.pallas_call(
        body, name="ssd_bwd", grid=(nc,),
        in_specs=[pl.BlockSpec((l, XBC_W), rev), pl.BlockSpec((l, XBC_W), rev),
                  pl.BlockSpec((l, 128), rev), pl.BlockSpec((8, 128), lambda i: (0, 0)),
                  pl.BlockSpec((1, N_HEADS, HEAD_DIM, D_STATE), lambda i: (nc - 1 - i, 0, 0, 0)),
                  pl.BlockSpec((l, D_MODEL), rev)],
        out_specs=[pl.BlockSpec((l, XBC_W), rev), pl.BlockSpec((l, 128), rev), pl.BlockSpec((8, 128), lambda i: (0, 0))],
        out_shape=[jax.ShapeDtypeStruct((s, XBC_W), bf16), jax.ShapeDtypeStruct((s, 128), f32),
                   jax.ShapeDtypeStruct((8, 128), f32)],
        scratch_shapes=[pltpu.VMEM((N_HEADS, HEAD_DIM, D_STATE), f32)],
        compiler_params=_cp("arbitrary"))(act, pre, dtf, prm, hs, dy)


def _fox_cumsum(dtf, prm):
    s = dtf.shape[0]
    l = CHUNK

    def body(f_ref, prm_ref, c_ref, carry_ref):
        @pl.when(pl.program_id(0) == 0)
        def _():
            carry_ref[...] = jnp.zeros_like(carry_ref)
        lf = _log_sigmoid(f_ref[...] + prm_ref[3:4, :])
        row = lax.broadcasted_iota(jnp.int32, (l, l), 0)
        col = lax.broadcasted_iota(jnp.int32, (l, l), 1)
        c = _dot((row >= col).astype(f32), lf, precision=HIGHEST) + carry_ref[...]
        c_ref[...] = c
        carry_ref[...] = c[l - 1:l, :]

    return pl.pallas_call(
        body, name="fox_cumsum", grid=(s // l,),
        in_specs=[pl.BlockSpec((l, 128), lambda i: (i, 0)), pl.BlockSpec((8, 128), lambda i: (0, 0))],
        out_specs=pl.BlockSpec((l, 128), lambda i: (i, 0)),
        out_shape=jax.ShapeDtypeStruct((s, 128), f32),
        scratch_shapes=[pltpu.VMEM((1, 128), f32)],
        compiler_params=_cp("arbitrary"))(dtf, prm)


def _position():
    return lax.axis_index("x"), lax.axis_index("y"), lax.axis_index("c")


def _exchange_sems(n):
    return [pltpu.SemaphoreType.DMA((n, N_DEV - 1)), pltpu.SemaphoreType.DMA((n, N_DEV - 1)),
            pltpu.SemaphoreType.DMA((n,))]


def _exchange_copies(g_refs, r_refs, send_sems, recv_sems, local_sems, gather=False):
    n = len(g_refs)
    x, y, cc = _position()
    me = 4 * x + 2 * y + cc

    def src(a, j):
        return g_refs[a] if gather else g_refs[a].at[j]

    local = [pltpu.make_async_copy(src(a, me), r_refs[a].at[me], local_sems.at[a]) for a in range(n)]
    sends, recvs = [], []
    for k in range(1, N_DEV):
        px = 1 - x if k & 4 else x
        py = 1 - y if k & 2 else y
        pc = 1 - cc if k & 1 else cc
        pid = 4 * px + 2 * py + pc
        for a in range(n):
            sends.append(pltpu.make_async_remote_copy(
                src_ref=src(a, pid), dst_ref=r_refs[a].at[me],
                send_sem=send_sems.at[a, k - 1], recv_sem=recv_sems.at[a, k - 1],
                device_id=(px, py, pc), device_id_type=pl.DeviceIdType.MESH))
            recvs.append(pltpu.make_async_remote_copy(
                src_ref=src(a, pid), dst_ref=r_refs[a].at[pid],
                send_sem=send_sems.at[a, k - 1], recv_sem=recv_sems.at[a, k - 1],
                device_id=(px, py, pc), device_id_type=pl.DeviceIdType.MESH))
    return local, sends, recvs


def _exchange_start(copies):
    local, sends, _ = copies
    for cp in local + sends:
        cp.start()


def _exchange_wait(copies):
    local, sends, recvs = copies
    for cp in recvs:
        cp.wait_recv()
    for cp in sends:
        cp.wait_send()
    for cp in local:
        cp.wait()


AUG = HEAD_DIM
N_PAIRS = N_HEADS // 2
V_BLOCK = 2 * D_MODEL // 128


def _split3(x):
    hi = x.astype(bf16)
    r1 = x - hi.astype(f32)
    mid = r1.astype(bf16)
    lo = (r1 - mid.astype(f32)).astype(bf16)
    return hi.astype(f32), mid.astype(f32), lo.astype(f32)


def _fox_prep(qkv, c):
    s = qkv.shape[0]
    ts = min(CONV_ROW_TILE, s)

    def body(q_ref, k_ref, c_ref, qa_ref, ka_ref):
        lane = lax.broadcasted_iota(jnp.int32, (ts, 128), 1)
        low = lane < HEAD_DIM
        for h in range(N_HEADS):
            psl = slice(128 * (h // 2), 128 * (h // 2 + 1))
            qv = q_ref[:, psl].astype(f32) * (HEAD_DIM ** -0.5)
            kv = k_ref[:, psl].astype(f32)
            if h % 2:
                qv = pltpu.roll(qv, HEAD_DIM, 1)
                kv = pltpu.roll(kv, HEAD_DIM, 1)
            hi, mid, lo = _split3(c_ref[:, F_LANE + h:F_LANE + h + 1])
            ones = jnp.where((lane >= AUG + 3) & (lane < AUG + 6), 1.0, 0.0)
            cq = jnp.where(lane == AUG, hi, jnp.where(lane == AUG + 1, mid, jnp.where(lane == AUG + 2, lo, ones)))
            qa_ref[h] = jnp.where(low, qv, cq).astype(bf16)
            onek = jnp.where((lane >= AUG) & (lane < AUG + 3), 1.0, 0.0)
            ck = jnp.where(lane == AUG + 3, -hi, jnp.where(lane == AUG + 4, -mid, jnp.where(lane == AUG + 5, -lo, onek)))
            ka_ref[h] = jnp.where(low, kv, ck).astype(bf16)

    hm = pl.BlockSpec((N_HEADS, ts, 128), lambda i: (0, i, 0))
    return pl.pallas_call(
        body, name="fox_prep", grid=(s // ts,),
        in_specs=[_rowspec(ts, D_MODEL, 0), _rowspec(ts, D_MODEL, 1), _rowspec(ts, 128)],
        out_specs=[hm, hm], out_shape=[jax.ShapeDtypeStruct((N_HEADS, s, 128), bf16)] * 2,
        compiler_params=_cp("parallel"))(qkv, qkv, c)


def _fox_fwd(qa, ka, qkv, ws):
    s = qkv.shape[0]
    t = min(ATTN_FWD_TILE, s)
    nq = s // t
    n = len(ws)

    def body(qa_ref, ka_ref, v_ref, *rest):
        w_refs, (o_ref, lse_ref), wg_refs, sems = rest[:n], rest[n:n + 2], rest[n + 2:2 * n + 2], rest[2 * n + 2:]
        qi = pl.program_id(1)
        copies = _exchange_copies(w_refs, wg_refs, *sems, gather=True)

        @pl.when((pl.program_id(0) == 0) & (qi == 0))
        def _():
            _exchange_start(copies)
        low = lax.broadcasted_iota(jnp.int32, (t, 128), 1) < HEAD_DIM
        row = lax.broadcasted_iota(jnp.int32, (t, t), 0)
        col = lax.broadcasted_iota(jnp.int32, (t, t), 1)

        def tile(ki, carry, diagonal):
            koff = pl.multiple_of(ki * t, t)
            v = v_ref[pl.ds(koff, t), :]
            vh = (jnp.where(low, v, jnp.ones_like(v)), jnp.where(low, jnp.ones_like(v), v))
            new = []
            for r in range(2):
                m_old, acc = carry[r]
                sc = _dot(qa_ref[r], ka_ref[r, pl.ds(koff, t), :], NT)
                if diagonal:
                    sc = jnp.where(col <= row, sc, NEG)
                m_new = jnp.maximum(m_old, jnp.max(sc, axis=1, keepdims=True))
                p = jnp.exp(sc - m_new)
                new.append((m_new, acc * jnp.exp(m_old - m_new) + _dot(p.astype(bf16), vh[r])))
            return tuple(new)

        init = ((jnp.full((t, 1), NEG, f32), jnp.zeros((t, 128), f32)),) * 2
        carry = lax.fori_loop(0, qi, lambda ki, cr: tile(ki, cr, False), init)
        (m_a, acc_a), (m_b, acc_b) = tile(qi, carry, True)
        l_a, l_b = acc_a[:, HEAD_DIM:HEAD_DIM + 1], acc_b[:, 0:1]
        o_ref[...] = jnp.where(low, acc_a / l_a, acc_b / l_b).astype(bf16)
        for r, lse in enumerate((m_a + jnp.log(l_a), m_b + jnp.log(l_b))):
            lse_ref[r] = jnp.broadcast_to(lse, (t, 128)).T[0:1, :]

        @pl.when((pl.program_id(0) == N_PAIRS - 1) & (qi == nq - 1))
        def _():
            _exchange_wait(copies)

    anyspec = pl.BlockSpec(memory_space=pl.ANY)
    outs = pl.pallas_call(
        body, name="fox_fwd", grid=(N_PAIRS, nq),
        in_specs=[pl.BlockSpec((2, t, 128), lambda j, qi: (j, qi, 0)),
                  pl.BlockSpec((2, s, 128), lambda j, qi: (j, 0, 0)),
                  pl.BlockSpec((s, 128), lambda j, qi: (0, V_BLOCK + j))] + [anyspec] * n,
        out_specs=[pl.BlockSpec((t, 128), lambda j, qi: (qi, j)), pl.BlockSpec((2, 1, t), lambda j, qi: (j, 0, qi))]
        + [anyspec] * n,
        out_shape=[jax.ShapeDtypeStruct((s, D_MODEL), bf16), jax.ShapeDtypeStruct((N_HEADS, 1, s), f32)]
        + [jax.ShapeDtypeStruct((N_DEV,) + w.shape, w.dtype) for w in ws],
        scratch_shapes=_exchange_sems(n),
        compiler_params=_cp("arbitrary", "arbitrary"))(qa, ka, qkv, *ws)
    return outs[0], outs[1], outs[2:]


def _fox_bwd(qa, ka, qkv, do, lse, delta, gs):
    s = qkv.shape[0]
    t = min(ATTN_TILE, s)
    nq = s // t
    n = len(gs)

    def body(qa_ref, ka_ref, v_ref, do_ref, lse_ref, dl_ref, *rest):
        g_refs, (dq_ref, dk_ref, dv_ref), r_refs, sems = rest[:n], rest[n:n + 3], rest[n + 3:2 * n + 3], rest[2 * n + 3:]
        ki = pl.program_id(1)
        copies = _exchange_copies(g_refs, r_refs, *sems)

        @pl.when((pl.program_id(0) == 0) & (ki == 0))
        def _():
            _exchange_start(copies)

        @pl.when(ki == 0)
        def _():
            dq_ref[...] = jnp.zeros_like(dq_ref)
        def block(carry, k_lo, n, qoff, diagonal):
            dks, dv = carry
            low = lax.broadcasted_iota(jnp.int32, (n, 128), 1) < HEAD_DIM
            v = v_ref[k_lo:k_lo + n, :]
            zero = jnp.zeros_like(v)
            vh = (jnp.where(low, v, zero), jnp.where(low, zero, v))
            dov = do_ref[pl.ds(qoff, n), :]
            doh = (jnp.where(low, dov, zero), jnp.where(low, zero, dov))
            new_dks = []
            for r in range(2):
                qt = qa_ref[r, pl.ds(qoff, n), :]
                kt = ka_ref[r, k_lo:k_lo + n, :]
                sct = _dot(kt, qt, NT)
                if diagonal:
                    row = lax.broadcasted_iota(jnp.int32, (n, n), 0)
                    col = lax.broadcasted_iota(jnp.int32, (n, n), 1)
                    sct = jnp.where(row <= col, sct, NEG)
                pt = jnp.exp(sct - lse_ref[r, :, pl.ds(qoff, n)])
                dpt = _dot(vh[r], dov, NT)
                dst = (pt * (dpt - dl_ref[r, :, pl.ds(qoff, n)])).astype(bf16)
                dv = dv + _dot(pt.astype(bf16), doh[r])
                new_dks.append(dks[r] + _dot(dst, qt))
                dq_ref[r, pl.ds(qoff, n), :] += _dot(dst, kt, TN)
            return tuple(new_dks), dv

        h = t // 2
        zacc = jnp.zeros((h, 128), f32)
        zero_carry = ((zacc, zacc), zacc)
        q0 = pl.multiple_of(ki * t, t)
        q1 = pl.multiple_of(ki * t + h, h)
        first = block(block(zero_carry, 0, h, q0, True), 0, h, q1, False)
        second = block(zero_carry, h, h, q1, True)
        carry = (tuple(jnp.concatenate([first[0][r], second[0][r]], axis=0) for r in range(2)),
                 jnp.concatenate([first[1], second[1]], axis=0))
        dks, dv = lax.fori_loop(
            ki + 1, nq, lambda qi, cr: block(cr, 0, t, pl.multiple_of(qi * t, t), False), carry)
        dk_ref[0] = dks[0]
        dk_ref[1] = dks[1]
        dv_ref[...] = dv.astype(bf16)

        @pl.when((pl.program_id(0) == N_PAIRS - 1) & (ki == nq - 1))
        def _():
            _exchange_wait(copies)

    anyspec = pl.BlockSpec(memory_space=pl.ANY)
    outs = pl.pallas_call(
        body, name="fox_bwd", grid=(N_PAIRS, nq),
        in_specs=[pl.BlockSpec((2, s, 128), lambda j, ki: (j, 0, 0)),
                  pl.BlockSpec((2, t, 128), lambda j, ki: (j, ki, 0)),
                  pl.BlockSpec((t, 128), lambda j, ki: (ki, V_BLOCK + j)),
                  pl.BlockSpec((s, 128), lambda j, ki: (0, j)),
                  pl.BlockSpec((2, 1, s), lambda j, ki: (j, 0, 0)),
                  pl.BlockSpec((2, 1, s), lambda j, ki: (j, 0, 0))] + [anyspec] * n,
        out_specs=[pl.BlockSpec((2, s, 128), lambda j, ki: (j, 0, 0)),
                   pl.BlockSpec((2, t, 128), lambda j, ki: (j, ki, 0)),
                   pl.BlockSpec((t, 128), lambda j, ki: (ki, j))] + [anyspec] * n,
        out_shape=[jax.ShapeDtypeStruct((N_HEADS, s, 128), f32), jax.ShapeDtypeStruct((N_HEADS, s, 128), f32),
                   jax.ShapeDtypeStruct((s, D_MODEL), bf16)] + [jax.ShapeDtypeStruct(g.shape, g.dtype) for g in gs],
        scratch_shapes=_exchange_sems(n),
        compiler_params=_cp("arbitrary", "arbitrary"))(qa, ka, qkv, do, lse, delta, *gs)
    return outs[0], outs[1], outs[2], outs[3:]


def _fox_bwd_post(dq_hm, dk_hm):
    s = dq_hm.shape[1]
    ts = min(CONV_ROW_TILE, s)

    def body(dq_ref, dk_ref, q_ref, k_ref, dc_ref):
        lane = lax.broadcasted_iota(jnp.int32, (ts, 128), 1)
        dc = jnp.zeros((ts, 128), f32)
        for h in range(N_HEADS):
            hsl = slice(HEAD_DIM * h, HEAD_DIM * (h + 1))
            dqv = dq_ref[h]
            dkv = dk_ref[h]
            q_ref[:, hsl] = (dqv[:, 0:HEAD_DIM] * (HEAD_DIM ** -0.5)).astype(bf16)
            k_ref[:, hsl] = dkv[:, 0:HEAD_DIM].astype(bf16)
            dc = dc + jnp.where(lane == F_LANE + h, dqv[:, AUG:AUG + 1] - dkv[:, AUG + 3:AUG + 4], 0.0)
        dc_ref[...] = dc

    hm = pl.BlockSpec((N_HEADS, ts, 128), lambda i: (0, i, 0))
    return pl.pallas_call(
        body, name="fox_bwd_post", grid=(s // ts,), in_specs=[hm, hm],
        out_specs=[_rowspec(ts, D_MODEL), _rowspec(ts, D_MODEL), _rowspec(ts, 128)],
        out_shape=[jax.ShapeDtypeStruct((s, D_MODEL), bf16), jax.ShapeDtypeStruct((s, D_MODEL), bf16),
                   jax.ShapeDtypeStruct((s, 128), f32)],
        compiler_params=_cp("parallel"))(dq_hm, dk_hm)


def _fox_gate_bwd(dc, dtf, prm, ddt_raw):
    s = dtf.shape[0]
    l = CHUNK
    nb = s // l

    def body(dc_ref, f_ref, prm_ref, ddt_ref, out_ref, dfb_ref, carry_ref):
        @pl.when(pl.program_id(0) == 0)
        def _():
            carry_ref[...] = jnp.zeros_like(carry_ref)
            dfb_ref[...] = jnp.zeros_like(dfb_ref)
        dc = dc_ref[...]
        row = lax.broadcasted_iota(jnp.int32, (l, l), 0)
        col = lax.broadcasted_iota(jnp.int32, (l, l), 1)
        dlf = _dot((row <= col).astype(f32), dc, precision=HIGHEST) + carry_ref[...]
        carry_ref[...] = dlf[0:1, :]
        lane = lax.broadcasted_iota(jnp.int32, (l, 128), 1)
        is_f = (lane >= F_LANE) & (lane < F_LANE + N_HEADS)
        dfr = jnp.where(is_f, dlf * _sigmoid(-(f_ref[...] + prm_ref[3:4, :])), 0.0)
        dfb_ref[...] += jnp.sum(dfr, axis=0, keepdims=True)
        out_ref[...] = ddt_ref[...] + dfr

    def rev(i):
        return (nb - 1 - i, 0)

    return pl.pallas_call(
        body, name="fox_gate_bwd", grid=(nb,),
        in_specs=[pl.BlockSpec((l, 128), rev), pl.BlockSpec((l, 128), rev), pl.BlockSpec((8, 128), lambda i: (0, 0)),
                  pl.BlockSpec((l, 128), rev)],
        out_specs=[pl.BlockSpec((l, 128), rev), pl.BlockSpec((1, 128), lambda i: (0, 0))],
        out_shape=[jax.ShapeDtypeStruct((s, 128), f32), jax.ShapeDtypeStruct((1, 128), f32)],
        scratch_shapes=[pltpu.VMEM((1, 128), f32)],
        compiler_params=_cp("arbitrary"))(dc, dtf, prm, ddt_raw)


def _all_gather(xl, name):
    r, c = xl.shape

    def body(x_ref, out_ref, send_sems, recv_sems, local_sem):
        x, y, cc = _position()
        me, sibling = (x, y, cc), (x, y, 1 - cc)
        chips = [(1 - x, y), (x, 1 - y), (1 - x, 1 - y)]

        def slot(px, py, pc):
            return out_ref.at[4 * px + 2 * py + pc]

        def copy(k, block, to, src=None):
            return pltpu.make_async_remote_copy(
                src_ref=slot(*block) if src is None else src, dst_ref=slot(*block),
                send_sem=send_sems.at[k], recv_sem=recv_sems.at[k],
                device_id=to, device_id_type=pl.DeviceIdType.MESH)

        mine = pltpu.make_async_copy(x_ref, slot(*me), local_sem)
        mine.start()
        first = [copy(0, me, sibling, src=x_ref)]
        first += [copy(1 + j, me, (*chip, cc), src=x_ref) for j, chip in enumerate(chips)]
        for cp in first:
            cp.start()
        passed = [copy(4 + j, (*chip, cc), sibling) for j, chip in enumerate(chips)]
        for j, chip in enumerate(chips):
            copy(1 + j, (*chip, cc), me).wait_recv()
            passed[j].start()
        copy(0, sibling, me).wait_recv()
        for j, chip in enumerate(chips):
            copy(4 + j, (*chip, 1 - cc), me).wait_recv()
        for cp in first + passed:
            cp.wait_send()
        mine.wait()

    return pl.pallas_call(
        body, name=name,
        out_shape=jax.ShapeDtypeStruct((N_DEV, r, c), xl.dtype),
        in_specs=[pl.BlockSpec(memory_space=pl.ANY)], out_specs=pl.BlockSpec(memory_space=pl.ANY),
        scratch_shapes=[pltpu.SemaphoreType.DMA((7,)), pltpu.SemaphoreType.DMA((7,)), pltpu.SemaphoreType.DMA],
    )(xl)


def _sum_parts(parts, name):
    n, r, c = parts.shape

    def body(p_ref, o_ref):
        g = p_ref[0]
        for i in range(1, n):
            g = g + p_ref[i]
        o_ref[...] = g

    return pl.pallas_call(body, name=name, out_shape=jax.ShapeDtypeStruct((r, c), f32))(parts)


def _adamw(w, m, v, parts, name, tr=128, by_columns=False):
    r, c = w.shape
    n = parts.shape[0]
    tr = min(tr, r)
    c1 = 1.0 - ADAM_B1 ** ADAM_STEP
    c2 = 1.0 - ADAM_B2 ** ADAM_STEP

    def body(w_ref, m_ref, v_ref, p_ref, g_ref, d_ref, nm_ref, nv_ref):
        g = p_ref[0].astype(f32)
        for i in range(1, n):
            g = g + p_ref[i].astype(f32)
        g_ref[...] = g
        nm = ADAM_B1 * m_ref[...] + (1.0 - ADAM_B1) * g
        nv = ADAM_B2 * v_ref[...] + (1.0 - ADAM_B2) * (g * g)
        nm_ref[...] = nm
        nv_ref[...] = nv
        d_ref[...] = -ADAM_LR * ((nm / c1) / (jnp.sqrt(nv / c2) + ADAM_EPS) + ADAM_WD * w_ref[...])

    if by_columns:
        blk = pl.BlockSpec((r, 128), lambda i: (0, i))
        pblk = pl.BlockSpec((n, r, 128), lambda i: (0, 0, i))
        steps = c // 128
    else:
        blk = pl.BlockSpec((tr, c), lambda i: (i, 0))
        pblk = pl.BlockSpec((n, tr, c), lambda i: (0, i, 0))
        steps = r // tr
    return pl.pallas_call(
        body, name=name, grid=(steps,),
        in_specs=[blk, blk, blk, pblk],
        out_specs=[blk] * 4, out_shape=[jax.ShapeDtypeStruct((r, c), f32)] * 4,
        compiler_params=_cp("parallel"))(w, m, v, parts)


def _lanes(w):
    return -(-w // 128) * 128


def _pack(arrs):
    rows = []
    for a in arrs:
        k, w = a.shape
        if w % 128:
            a = jnp.pad(a, ((0, 0), (0, _lanes(w) - w)))
        rows.append(a.reshape(-1, 128))
    out = jnp.concatenate(rows, axis=0)
    pad = -out.shape[0] % 8
    return jnp.pad(out, ((0, pad), (0, 0))) if pad else out


def _unpack(packed, shapes):
    outs, off = [], 0
    lead = packed.shape[:-2]
    for k, w in shapes:
        nrow = k * _lanes(w) // 128
        a = packed[..., off:off + nrow, :].reshape(*lead, k, _lanes(w))[..., :w]
        outs.append(a)
        off += nrow
    return outs


def _gathered_cols(a):
    n, k, wl = a.shape
    return jnp.transpose(a, (1, 0, 2)).reshape(k, n * wl)


def _col_shards(a):
    k, w = a.shape
    return jnp.transpose(a.reshape(k, N_DEV, w // N_DEV), (1, 0, 2))


SMALL_PARAMS = (
    ("e_norm_pre", 1, 1024, False), ("e_conv_w", 4, 2048, True), ("e_conv_b", 1, 2048, False),
    ("e_dt_bias", 1, 16, False), ("e_a_log", 1, 16, False), ("e_d_skip", 1, 16, False), ("e_fgate_b", 1, 16, False),
    ("e_ssd_norm", 1, 1024, False), ("e_norm_post", 1, 1024, False), ("o_norm_pre", 1, 1024, True),
    ("o_conv_w", 31, 2048, True), ("o_conv_b", 1, 2048, True), ("o_ln_g", 1, 2048, True), ("o_ln_b", 1, 2048, True),
    ("o_norm_post", 1, 1024, True),
)
BIG_PARAMS = ("e_w_in", "e_w_out", "o_w_in", "o_w_out")
WEIGHT_ORDER = ("e_norm_pre", "e_w_in", "e_conv_w", "e_conv_b", "e_dt_bias", "e_a_log", "e_d_skip", "e_fgate_b",
                "e_ssd_norm", "e_w_out", "e_norm_post", "o_norm_pre", "o_w_in", "o_conv_w", "o_conv_b", "o_ln_g",
                "o_ln_b", "o_w_out", "o_norm_post")
E_IN = 7200
O_IN = 6144


def kernel(x, e_norm_pre, e_w_in, e_conv_w, e_conv_b, e_dt_bias, e_a_log, e_d_skip, e_fgate_b, e_ssd_norm, e_w_out, e_norm_post, o_norm_pre, o_w_in, o_conv_w, o_conv_b, o_ln_g, o_ln_b, o_w_out, o_norm_post, loss_target, m_e_norm_pre, m_e_w_in, m_e_conv_w, m_e_conv_b, m_e_dt_bias, m_e_a_log, m_e_d_skip, m_e_fgate_b, m_e_ssd_norm, m_e_w_out, m_e_norm_post, m_o_norm_pre, m_o_w_in, m_o_conv_w, m_o_conv_b, m_o_ln_g, m_o_ln_b, m_o_w_out, m_o_norm_post, v_e_norm_pre, v_e_w_in, v_e_conv_w, v_e_conv_b, v_e_dt_bias, v_e_a_log, v_e_d_skip, v_e_fgate_b, v_e_ssd_norm, v_e_w_out, v_e_norm_post, v_o_norm_pre, v_o_w_in, v_o_conv_w, v_o_conv_b, v_o_ln_g, v_o_ln_b, v_o_w_out, v_o_norm_post):
    given = dict(locals())
    w_in = {n: given[n] for n in WEIGHT_ORDER}
    m_in = {n: given["m_" + n] for n in WEIGHT_ORDER}
    v_in = {n: given["v_" + n] for n in WEIGHT_ORDER}

    def mat(a):
        return a.reshape(a.shape[-2:])

    xs = mat(x)
    tgt = mat(loss_target)
    xi, yi, ci = _position()
    me = 4 * xi + 2 * yi + ci
    ow = O_IN // N_DEV
    wr = D_CONV // N_DEV

    ew = E_IN // N_DEV
    w_t = _all_gather(jnp.transpose(mat(e_w_in)).astype(bf16), "gather_weights").reshape(E_IN, D_MODEL)
    later_weights = [mat(e_w_out).astype(bf16), mat(o_w_in).astype(bf16), mat(o_w_out).astype(bf16)]
    w_z, w_xbc, w_qkv = w_t[0:2048], w_t[2048:4096], w_t[4112:7184]
    w_dtf = jnp.concatenate([w_t[4096:4112], w_t[7184:7200], jnp.zeros((96, D_MODEL), bf16)], axis=0)

    sharded_small = [(n, k, w) for n, k, w, sh in SMALL_PARAMS if sh]
    sg = _all_gather(_pack([mat(w_in[n]) for n, _, _ in sharded_small]), "gather_small_weights")
    full_small = {n: _gathered_cols(a)
                  for (n, _, _), a in zip(sharded_small, _unpack(sg, [(k, w // N_DEV) for _, k, w in sharded_small]))}
    for n, _, _, sh in SMALL_PARAMS:
        if not sh:
            full_small[n] = mat(w_in[n])
    p = full_small
    prm = jnp.zeros((8, 128), f32)
    prm = prm.at[0, 0:16].set(p["e_dt_bias"][0]).at[1, 0:16].set(p["e_a_log"][0]).at[2, 0:16].set(p["e_d_skip"][0])
    prm = prm.at[3, F_LANE:F_LANE + 16].set(p["e_fgate_b"][0])

    u0 = _rms_fwd(xs, p["e_norm_pre"], "rms_pre0")
    z0 = _mm_nt([(u0, 0, w_z, 0, D_MODEL)], bf16, "proj0_z", tm=1024, tn=1024)
    xraw = _mm_nt([(u0, 0, w_xbc, 0, D_MODEL)], bf16, "proj0_xbc", tm=1024, tn=1024)
    qkv = _mm_nt([(u0, 0, w_qkv, 0, D_MODEL)], bf16, "proj0_qkv", tm=1024, tn=1024)
    dtf = _mm_nt([(u0, 0, w_dtf, 0, D_MODEL)], f32, "proj0_dtf", tm=1024, tn=128)
    pre, act = _conv_ssd_fwd(xraw, p["e_conv_w"], p["e_conv_b"])
    y, hs = _ssd_fwd(act, dtf, prm)
    qa, ka = _fox_prep(qkv, _fox_cumsum(dtf, prm))
    o, lse, (e_w_out_g, o_w_in_g, o_w_out_g) = _fox_fwd(qa, ka, qkv, later_weights)
    e_w_out_f = e_w_out_g.reshape(D_CONV, D_MODEL)
    o_w_in_f = _gathered_cols(o_w_in_g)
    o_w_out_f = o_w_out_g.reshape(D_CONV, D_MODEL)
    cat = _gate0_fwd(y, z0, o, p["e_ssd_norm"])
    out0 = _mm_nn(cat, e_w_out_f, f32, "out0")
    x1, u1 = _post0_pre1(xs, out0, p["e_norm_post"], p["o_norm_pre"])

    proj1 = _mm_nn(u1, o_w_in_f, bf16, "proj1")
    hc = _conv_glu_fwd(proj1, p["o_conv_w"], p["o_conv_b"])
    h3 = _ln_gate_fwd(hc, proj1, p["o_ln_g"], p["o_ln_b"])
    out1 = _mm_nn(h3, o_w_out_f, f32, "out1")
    dy, d_out1, dg_post1, loss_part = _final_loss(x1, out1, tgt, p["o_norm_post"])

    dh3 = _mm_nt([(d_out1, 0, o_w_out_f, 0, D_MODEL)], bf16, "dh3", tm=1024, tn=D_CONV)
    g_o_w_out = _mm_tn(h3, d_out1, "dw_out1")
    dhc, dz1, dg_ln, db_ln = _ln_gate_bwd(hc, proj1, dh3, p["o_ln_g"], p["o_ln_b"])
    dval, dgate, dw_conv1, db_conv1 = _conv_glu_bwd(dhc, proj1, p["o_conv_w"])
    dproj1 = jnp.concatenate([dval, dgate, dz1], axis=1)
    du1 = _mm_nt([(dproj1, 0, o_w_in_f, 0, O_IN)], f32, "du1")
    g_o_w_in = _mm_tn(u1, dproj1, "dw_in1", tn=ow, blocked=True)
    dx1, d_out0, dg_pre1, dg_post0 = _mid_bwd(x1, du1, dy, out0, p["o_norm_pre"], p["e_norm_post"])

    dcat = _mm_nt([(d_out0, 0, e_w_out_f, 0, D_MODEL)], bf16, "dcat", tm=1024, tn=D_CONV)
    g_e_w_out = _mm_tn(cat, d_out0, "dw_out0")
    dy_ssd, do, dz0, delta, dg_ssd_norm = _gate0_bwd(y, z0, o, dcat, p["e_ssd_norm"])
    early = [g_e_w_out.reshape(N_DEV, wr, D_MODEL).astype(bf16), g_o_w_in.astype(bf16),
             g_o_w_out.reshape(N_DEV, wr, D_MODEL).astype(bf16)]
    dq_hm, dk_hm, dv, early_parts = _fox_bwd(qa, ka, qkv, do, lse, delta[0:N_HEADS].reshape(N_HEADS, 1, -1), early)
    dq, dk, dc = _fox_bwd_post(dq_hm, dk_hm)
    dpre, ddt_raw, dprm = _ssd_bwd(act, pre, dtf, prm, hs, dy_ssd)
    ddtf, dfb = _fox_gate_bwd(dc, dtf, prm, ddt_raw)
    dxraw, dw_conv0, db_conv0 = _conv_ssd_bwd(dpre, xraw, p["e_conv_w"])
    gw_dtf = _mm_tn(ddtf, u0, "dw_in0_dtf")
    g_e_w_in_t = jnp.concatenate([
        _mm_tn(dz0, u0, "dw_in0_z"), _mm_tn(dxraw, u0, "dw_in0_xbc"), gw_dtf[0:16],
        _mm_tn(dq, u0, "dw_in0_q"), _mm_tn(dk, u0, "dw_in0_k"), _mm_tn(dv, u0, "dw_in0_v"), gw_dtf[16:32]], axis=0)
    du0, last_parts = _mm_nt(
        [(dz0, 0, w_z, 0, 2048), (dxraw, 0, w_xbc, 0, 2048), (dq, 0, w_qkv, 0, 1024), (dk, 0, w_qkv, 1, 1024),
         (dv, 0, w_qkv, 2, 1024), (ddtf, 0, w_dtf, 0, 128)], f32, "du0", b_kn=True,
        gs=[g_e_w_in_t.astype(bf16).reshape(N_DEV, ew, D_MODEL)])
    grad_x, dg_pre0 = _first_bwd(xs, du0, dx1, p["e_norm_pre"])

    outs = {"e_w_in": tuple(jnp.transpose(r) for r in _adamw(
        jnp.transpose(mat(e_w_in)), jnp.transpose(mat(m_e_w_in)), jnp.transpose(mat(v_e_w_in)), last_parts[0],
        "adamw_e_w_in", by_columns=True))}
    for n, parts in zip(BIG_PARAMS[1:], early_parts):
        outs[n] = _adamw(mat(w_in[n]), mat(m_in[n]), mat(v_in[n]), parts, "adamw_" + n)

    small_grads = {
        "e_norm_pre": dg_pre0, "e_conv_w": dw_conv0, "e_conv_b": db_conv0, "e_dt_bias": dprm[0:1, 0:16],
        "e_a_log": dprm[1:2, 0:16], "e_d_skip": dprm[2:3, 0:16], "e_fgate_b": dfb[:, F_LANE:F_LANE + 16],
        "e_ssd_norm": dg_ssd_norm, "e_norm_post": dg_post0, "o_norm_pre": dg_pre1, "o_conv_w": dw_conv1,
        "o_conv_b": db_conv1, "o_ln_g": dg_ln, "o_ln_b": db_ln, "o_norm_post": dg_post1,
    }
    gathered = _all_gather(_pack([small_grads[n] for n, _, _, _ in SMALL_PARAMS] + [loss_part]), "gather_small_grads")
    summed = _unpack(_sum_parts(gathered, "sum_small_grads"), [(k, w) for _, k, w, _ in SMALL_PARAMS] + [(1, 128)])
    loss = summed[-1][0, 0]
    g_local = []
    for (n, k, w, sh), g in zip(SMALL_PARAMS, summed):
        g_local.append(lax.dynamic_slice_in_dim(g, me * (w // N_DEV), w // N_DEV, axis=1) if sh else g)
    names = [n for n, _, _, _ in SMALL_PARAMS]
    local_shapes = [(k, w // N_DEV if sh else w) for _, k, w, sh in SMALL_PARAMS]
    res = _adamw(_pack([mat(w_in[n]) for n in names]), _pack([mat(m_in[n]) for n in names]),
                 _pack([mat(v_in[n]) for n in names]), _pack(g_local)[None], "adamw_small", tr=8)
    unpacked = [_unpack(r, local_shapes) for r in res]
    for i, n in enumerate(names):
        outs[n] = tuple(u[i] for u in unpacked)

    ret = [loss, grad_x.reshape(x.shape)]
    for j in range(4):
        ret += [outs[n][j].reshape(w_in[n].shape) for n in WEIGHT_ORDER]
    return tuple(ret)
```

```python
import jax
import jax.numpy as jnp
from jax import lax
from jax.experimental import pallas as pl
from jax.experimental.pallas import tpu as pltpu

f32 = jnp.float32
bf16 = jnp.bfloat16

N_DEV = 8
D_MODEL = 1024
N_HEADS = 16
HEAD_DIM = 64
N_GROUPS = 4
HEADS_PER_GROUP = 4
D_STATE = 128
CHUNK = 512
SSD_CONV = 4
CONV_WIDTH = 31
D_CONV = 2048
EPS = 1e-6
XBC_W = 2048
B_OFF = 1024
C_OFF = 1536
F_LANE = 16
HALO = 32

ADAM_LR = 0.001
ADAM_B1 = 0.9
ADAM_B2 = 0.999
ADAM_EPS = 1e-08
ADAM_WD = 0.01
ADAM_STEP = 10

VMEM_LIMIT_BYTES = 56 * 1024 * 1024
ROW_TILE = 512
CONV_ROW_TILE = 512
CONV_COL_TILE = 512
CONV_SUB = 32
ATTN_TILE = 1024
ATTN_FWD_TILE = 1024

NT = (((1,), (1,)), ((), ()))
TN = (((0,), (0,)), ((), ()))
HIGHEST = lax.Precision.HIGHEST
NEG = -1e30


def _cp(*sem):
    return pltpu.CompilerParams(dimension_semantics=sem if sem else None, vmem_limit_bytes=VMEM_LIMIT_BYTES)


def _sigmoid(x):
    return jax.nn.sigmoid(x)


def _silu(x):
    return x * _sigmoid(x)


def _dsilu(x):
    s = _sigmoid(x)
    return s * (1.0 + x * (1.0 - s))


def _softplus(x):
    return jnp.maximum(x, 0.0) + jnp.log(1.0 + jnp.exp(-jnp.abs(x)))


def _log_sigmoid(x):
    return jnp.minimum(x, 0.0) - jnp.log(1.0 + jnp.exp(-jnp.abs(x)))


def _dot(a, b, dims=None, precision=None):
    if dims is None:
        return jnp.dot(a, b, preferred_element_type=f32, precision=precision)
    return lax.dot_general(a, b, dims, preferred_element_type=f32, precision=precision)


def _mm_nn(a, b, out_dtype, name, tm=1024, tn=1024):
    m, k = a.shape
    n = b.shape[1]
    tm, tn = min(tm, m), min(tn, n)

    def body(a_ref, b_ref, o_ref):
        o_ref[...] = _dot(a_ref[...], b_ref[...]).astype(o_ref.dtype)

    return pl.pallas_call(
        body, name=name, grid=(n // tn, m // tm),
        in_specs=[pl.BlockSpec((tm, k), lambda j, i: (i, 0)), pl.BlockSpec((k, tn), lambda j, i: (0, j))],
        out_specs=pl.BlockSpec((tm, tn), lambda j, i: (i, j)),
        out_shape=jax.ShapeDtypeStruct((m, n), out_dtype), compiler_params=_cp("parallel", "parallel"))(a, b)


def _mm_nt(pairs, out_dtype, name, tm=512, tn=512, gs=(), b_kn=False):
    m = pairs[0][0].shape[0]
    n = pairs[0][2].shape[1] if b_kn else pairs[0][2].shape[0]
    tm, tn = min(tm, m), min(tn, n)
    npair = len(pairs)
    ng = len(gs)
    grid = (n // tn, m // tm)

    def body(*refs):
        g_refs = refs[2 * npair:2 * npair + ng]
        o_ref = refs[2 * npair + ng]
        r_refs = refs[2 * npair + ng + 1:2 * npair + 2 * ng + 1]
        sems = refs[2 * npair + 2 * ng + 1:]
        if ng:
            copies = _exchange_copies(g_refs, r_refs, *sems)

            @pl.when((pl.program_id(0) == 0) & (pl.program_id(1) == 0))
            def _():
                _exchange_start(copies)
        acc = None
        for p in range(npair):
            d = _dot(refs[2 * p][...].astype(bf16), refs[2 * p + 1][...], None if b_kn else NT)
            acc = d if acc is None else acc + d
        o_ref[...] = acc.astype(o_ref.dtype)
        if ng:
            @pl.when((pl.program_id(0) == grid[0] - 1) & (pl.program_id(1) == grid[1] - 1))
            def _():
                _exchange_wait(copies)

    in_specs, args = [], []
    for a, acb, b, bcb, k in pairs:
        in_specs.append(pl.BlockSpec((tm, k), lambda j, i, acb=acb: (i, acb)))
        if b_kn:
            in_specs.append(pl.BlockSpec((k, tn), lambda j, i, bcb=bcb: (bcb, j)))
        else:
            in_specs.append(pl.BlockSpec((tn, k), lambda j, i, bcb=bcb: (j, bcb)))
        args += [a, b]
    anyspec = pl.BlockSpec(memory_space=pl.ANY)
    outs = pl.pallas_call(
        body, name=name, grid=grid, in_specs=in_specs + [anyspec] * ng,
        out_specs=[pl.BlockSpec((tm, tn), lambda j, i: (i, j))] + [anyspec] * ng,
        out_shape=[jax.ShapeDtypeStruct((m, n), out_dtype)] + [jax.ShapeDtypeStruct(g.shape, g.dtype) for g in gs],
        scratch_shapes=_exchange_sems(ng) if ng else [],
        compiler_params=_cp("arbitrary", "arbitrary") if ng else _cp("parallel", "parallel"))(*args, *gs)
    return (outs[0], outs[1:]) if ng else outs[0]


def _mm_tn(a, b, name, a_cb=0, am=None, b_cb=0, bn=None, tn=1024, tk=1024, blocked=False):
    k = a.shape[0]
    am = a.shape[1] if am is None else am
    bn = b.shape[1] if bn is None else bn
    tm = min(1024, am)
    tn, tk = min(tn, bn), min(tk, k)
    a_off, b_off = a_cb * (am // tm), b_cb * (bn // tn)

    def body(a_ref, b_ref, o_ref):
        @pl.when(pl.program_id(2) == 0)
        def _():
            o_ref[...] = jnp.zeros_like(o_ref)
        d = _dot(a_ref[...].astype(bf16), b_ref[...].astype(bf16), TN)
        o_ref[...] += d.reshape(o_ref.shape)

    if blocked:
        out_spec = pl.BlockSpec((1, tm, tn), lambda i, j, kk: (j, i, 0))
        out_shape = jax.ShapeDtypeStruct((bn // tn, am, tn), f32)
    else:
        out_spec = pl.BlockSpec((tm, tn), lambda i, j, kk: (i, j))
        out_shape = jax.ShapeDtypeStruct((am, bn), f32)
    return pl.pallas_call(
        body, name=name, grid=(am // tm, bn // tn, k // tk),
        in_specs=[pl.BlockSpec((tk, tm), lambda i, j, kk: (kk, a_off + i)),
                  pl.BlockSpec((tk, tn), lambda i, j, kk: (kk, b_off + j))],
        out_specs=out_spec, out_shape=out_shape,
        compiler_params=_cp("parallel", "parallel", "arbitrary"))(a, b)


def _rowspec(ts, w, cb=0):
    return pl.BlockSpec((ts, w), lambda i: (i, cb))


def _vecspec(w):
    return pl.BlockSpec((1, w), lambda i: (0, 0))


def _rms_fwd(x, g, name):
    s, d = x.shape
    ts = min(ROW_TILE, s)

    def body(x_ref, g_ref, u_ref):
        xv = x_ref[...]
        r = lax.rsqrt(jnp.mean(xv * xv, axis=-1, keepdims=True) + EPS)
        u_ref[...] = (xv * r * g_ref[...]).astype(bf16)

    return pl.pallas_call(
        body, name=name, grid=(s // ts,), in_specs=[_rowspec(ts, d), _vecspec(d)], out_specs=_rowspec(ts, d),
        out_shape=jax.ShapeDtypeStruct((s, d), bf16), compiler_params=_cp("parallel"))(x, g)


def _rms_bwd_vals(xv, g, dy):
    r = lax.rsqrt(jnp.mean(xv * xv, axis=-1, keepdims=True) + EPS)
    xh = xv * r
    dg = jnp.sum(dy * xh, axis=0, keepdims=True)
    dxh = dy * g
    dx = r * (dxh - xh * jnp.mean(dxh * xh, axis=-1, keepdims=True))
    return dx, dg


def _gate0_fwd(y, z, o, ssd_norm):
    s = y.shape[0]
    ts = min(ROW_TILE, s)
    gw = D_MODEL // N_GROUPS

    def body(y_ref, zs_ref, zf_ref, o_ref, w_ref, cat_ref):
        yg = y_ref[...].astype(f32) * _silu(zs_ref[...].astype(f32))
        for g in range(N_GROUPS):
            seg = yg[:, gw * g:gw * (g + 1)]
            r = lax.rsqrt(jnp.mean(seg * seg, axis=-1, keepdims=True) + EPS)
            cat_ref[:, gw * g:gw * (g + 1)] = (seg * r * w_ref[:, gw * g:gw * (g + 1)]).astype(bf16)
        cat_ref[:, D_MODEL:] = (o_ref[...].astype(f32) * _silu(zf_ref[...].astype(f32))).astype(bf16)

    return pl.pallas_call(
        body, name="gate0_fwd", grid=(s // ts,),
        in_specs=[_rowspec(ts, D_MODEL), _rowspec(ts, D_MODEL, 0), _rowspec(ts, D_MODEL, 1), _rowspec(ts, D_MODEL),
                  _vecspec(D_MODEL)],
        out_specs=_rowspec(ts, 2 * D_MODEL),
        out_shape=jax.ShapeDtypeStruct((s, 2 * D_MODEL), bf16), compiler_params=_cp("parallel"))(y, z, z, o, ssd_norm)


def _post0_pre1(x, out0, g_post0, g_pre1):
    s, d = x.shape
    ts = min(ROW_TILE, s)

    def body(x_ref, o_ref, gp_ref, gn_ref, x1_ref, u1_ref):
        ov = o_ref[...]
        r = lax.rsqrt(jnp.mean(ov * ov, axis=-1, keepdims=True) + EPS)
        x1 = x_ref[...] + ov * r * gp_ref[...]
        x1_ref[...] = x1
        r1 = lax.rsqrt(jnp.mean(x1 * x1, axis=-1, keepdims=True) + EPS)
        u1_ref[...] = (x1 * r1 * gn_ref[...]).astype(bf16)

    return pl.pallas_call(
        body, name="post0_pre1", grid=(s // ts,),
        in_specs=[_rowspec(ts, d), _rowspec(ts, d), _vecspec(d), _vecspec(d)],
        out_specs=[_rowspec(ts, d), _rowspec(ts, d)],
        out_shape=[jax.ShapeDtypeStruct((s, d), f32), jax.ShapeDtypeStruct((s, d), bf16)],
        compiler_params=_cp("parallel"))(x, out0, g_post0, g_pre1)


def _ln_vals(hc, g, b):
    mu = jnp.mean(hc, axis=-1, keepdims=True)
    xc = hc - mu
    rstd = lax.rsqrt(jnp.mean(xc * xc, axis=-1, keepdims=True) + EPS)
    xh = xc * rstd
    return xh, rstd, xh * g + b


def _ln_gate_fwd(hc, proj1, ln_g, ln_b):
    s = hc.shape[0]
    ts = min(ROW_TILE, s)

    def body(hc_ref, z_ref, g_ref, b_ref, h3_ref):
        _, _, ln = _ln_vals(hc_ref[...].astype(f32), g_ref[...], b_ref[...])
        h3_ref[...] = (_silu(ln) * _silu(z_ref[...].astype(f32))).astype(bf16)

    return pl.pallas_call(
        body, name="ln_gate_fwd", grid=(s // ts,),
        in_specs=[_rowspec(ts, D_CONV), _rowspec(ts, D_CONV, 2), _vecspec(D_CONV), _vecspec(D_CONV)],
        out_specs=_rowspec(ts, D_CONV),
        out_shape=jax.ShapeDtypeStruct((s, D_CONV), bf16), compiler_params=_cp("parallel"))(hc, proj1, ln_g, ln_b)


def _final_loss(x1, out1, tgt, g_post1):
    s, d = x1.shape
    ts = min(ROW_TILE, s)

    def body(x1_ref, o_ref, t_ref, g_ref, dy_ref, do_ref, dg_ref, loss_ref):
        i = pl.program_id(0)

        @pl.when(i == 0)
        def _():
            dg_ref[...] = jnp.zeros_like(dg_ref)
            loss_ref[...] = jnp.zeros_like(loss_ref)
        ov = o_ref[...]
        g = g_ref[...]
        r = lax.rsqrt(jnp.mean(ov * ov, axis=-1, keepdims=True) + EPS)
        diff = x1_ref[...] + ov * r * g - t_ref[...]
        row = jnp.mean(diff * diff, axis=-1, keepdims=True)
        loss_ref[...] += jnp.broadcast_to(0.5 * jnp.sum(row, axis=0, keepdims=True), loss_ref.shape)
        dy = diff * (1.0 / d)
        dy_ref[...] = dy
        dx, dg = _rms_bwd_vals(ov, g, dy)
        do_ref[...] = dx.astype(bf16)
        dg_ref[...] += dg

    return pl.pallas_call(
        body, name="final_loss", grid=(s // ts,),
        in_specs=[_rowspec(ts, d), _rowspec(ts, d), _rowspec(ts, d), _vecspec(d)],
        out_specs=[_rowspec(ts, d), _rowspec(ts, d), _vecspec(d), _vecspec(128)],
        out_shape=[jax.ShapeDtypeStruct((s, d), f32), jax.ShapeDtypeStruct((s, d), bf16),
                   jax.ShapeDtypeStruct((1, d), f32), jax.ShapeDtypeStruct((1, 128), f32)],
        compiler_params=_cp("arbitrary"))(x1, out1, tgt, g_post1)


def _ln_gate_bwd(hc, proj1, dh3, ln_g, ln_b):
    s = hc.shape[0]
    ts = min(ROW_TILE, s)

    def body(hc_ref, z_ref, dh_ref, g_ref, b_ref, dhc_ref, dz_ref, dg_ref, db_ref):
        @pl.when(pl.program_id(0) == 0)
        def _():
            dg_ref[...] = jnp.zeros_like(dg_ref)
            db_ref[...] = jnp.zeros_like(db_ref)
        g = g_ref[...]
        xh, rstd, ln = _ln_vals(hc_ref[...].astype(f32), g, b_ref[...])
        zv = z_ref[...].astype(f32)
        dh3 = dh_ref[...].astype(f32)
        dz_ref[...] = (dh3 * _silu(ln) * _dsilu(zv)).astype(bf16)
        dln = dh3 * _silu(zv) * _dsilu(ln)
        dg_ref[...] += jnp.sum(dln * xh, axis=0, keepdims=True)
        db_ref[...] += jnp.sum(dln, axis=0, keepdims=True)
        dxh = dln * g
        dhc = rstd * (dxh - jnp.mean(dxh, axis=-1, keepdims=True) - xh * jnp.mean(dxh * xh, axis=-1, keepdims=True))
        dhc_ref[...] = dhc.astype(bf16)

    return pl.pallas_call(
        body, name="ln_gate_bwd", grid=(s // ts,),
        in_specs=[_rowspec(ts, D_CONV), _rowspec(ts, D_CONV, 2), _rowspec(ts, D_CONV), _vecspec(D_CONV),
                  _vecspec(D_CONV)],
        out_specs=[_rowspec(ts, D_CONV), _rowspec(ts, D_CONV), _vecspec(D_CONV), _vecspec(D_CONV)],
        out_shape=[jax.ShapeDtypeStruct((s, D_CONV), bf16), jax.ShapeDtypeStruct((s, D_CONV), bf16),
                   jax.ShapeDtypeStruct((1, D_CONV), f32), jax.ShapeDtypeStruct((1, D_CONV), f32)],
        compiler_params=_cp("arbitrary"))(hc, proj1, dh3, ln_g, ln_b)


def _mid_bwd(x1, du1, dy, out0, g_pre1, g_post0):
    s, d = x1.shape
    ts = min(ROW_TILE, s)

    def body(x1_ref, du_ref, dy_ref, o_ref, gn_ref, gp_ref, dx1_ref, do_ref, dgn_ref, dgp_ref):
        @pl.when(pl.program_id(0) == 0)
        def _():
            dgn_ref[...] = jnp.zeros_like(dgn_ref)
            dgp_ref[...] = jnp.zeros_like(dgp_ref)
        dxa, dgn = _rms_bwd_vals(x1_ref[...], gn_ref[...], du_ref[...])
        dx1 = dy_ref[...] + dxa
        dx1_ref[...] = dx1
        dgn_ref[...] += dgn
        dxo, dgp = _rms_bwd_vals(o_ref[...], gp_ref[...], dx1)
        do_ref[...] = dxo.astype(bf16)
        dgp_ref[...] += dgp

    return pl.pallas_call(
        body, name="mid_bwd", grid=(s // ts,),
        in_specs=[_rowspec(ts, d)] * 4 + [_vecspec(d), _vecspec(d)],
        out_specs=[_rowspec(ts, d), _rowspec(ts, d), _vecspec(d), _vecspec(d)],
        out_shape=[jax.ShapeDtypeStruct((s, d), f32), jax.ShapeDtypeStruct((s, d), bf16),
                   jax.ShapeDtypeStruct((1, d), f32), jax.ShapeDtypeStruct((1, d), f32)],
        compiler_params=_cp("arbitrary"))(x1, du1, dy, out0, g_pre1, g_post0)


def _first_bwd(x, du0, dx1, g_pre0):
    s, d = x.shape
    ts = min(ROW_TILE, s)

    def body(x_ref, du_ref, dx1_ref, g_ref, dx_ref, dg_ref):
        @pl.when(pl.program_id(0) == 0)
        def _():
            dg_ref[...] = jnp.zeros_like(dg_ref)
        dxa, dg = _rms_bwd_vals(x_ref[...], g_ref[...], du_ref[...])
        dx_ref[...] = dx1_ref[...] + dxa
        dg_ref[...] += dg

    return pl.pallas_call(
        body, name="first_bwd", grid=(s // ts,),
        in_specs=[_rowspec(ts, d)] * 3 + [_vecspec(d)],
        out_specs=[_rowspec(ts, d), _vecspec(d)],
        out_shape=[jax.ShapeDtypeStruct((s, d), f32), jax.ShapeDtypeStruct((1, d), f32)],
        compiler_params=_cp("arbitrary"))(x, du0, dx1, g_pre0)


def _gate0_bwd(y, z, o, dcat, ssd_norm):
    s = y.shape[0]
    ts = min(ROW_TILE, s)
    gw = D_MODEL // N_GROUPS

    def body(y_ref, zs_ref, zf_ref, o_ref, dn_ref, dg_ref, w_ref, dy_ref, do_ref, dz_ref, delta_ref, dw_ref):
        @pl.when(pl.program_id(0) == 0)
        def _():
            dw_ref[...] = jnp.zeros_like(dw_ref)
        yv = y_ref[...].astype(f32)
        zs = zs_ref[...].astype(f32)
        sz = _silu(zs)
        yg = yv * sz
        dyn = dn_ref[...].astype(f32)
        for g in range(N_GROUPS):
            sl = slice(gw * g, gw * (g + 1))
            seg = yg[:, sl]
            r = lax.rsqrt(jnp.mean(seg * seg, axis=-1, keepdims=True) + EPS)
            yh = seg * r
            dn = dyn[:, sl]
            dw_ref[:, sl] += jnp.sum(dn * yh, axis=0, keepdims=True)
            dyh = dn * w_ref[:, sl]
            dyg = r * (dyh - yh * jnp.mean(dyh * yh, axis=-1, keepdims=True))
            dy_ref[:, sl] = (dyg * sz[:, sl]).astype(bf16)
            dz_ref[:, sl] = (dyg * yv[:, sl] * _dsilu(zs[:, sl])).astype(bf16)
        zf = zf_ref[...].astype(f32)
        ov = o_ref[...].astype(f32)
        dog = dg_ref[...].astype(f32)
        dov = (dog * _silu(zf)).astype(bf16)
        do_ref[...] = dov
        dz_ref[:, D_MODEL:] = (dog * ov * _dsilu(zf)).astype(bf16)
        prod = dov.astype(f32) * ov
        lane = lax.broadcasted_iota(jnp.int32, (ts, 128), 1)
        delta = jnp.zeros((ts, 128), f32)
        for h in range(N_HEADS):
            dh = jnp.sum(prod[:, HEAD_DIM * h:HEAD_DIM * (h + 1)], axis=-1, keepdims=True)
            delta = delta + jnp.where(lane == h, dh, 0.0)
        delta_ref[...] = delta.T

    return pl.pallas_call(
        body, name="gate0_bwd", grid=(s // ts,),
        in_specs=[_rowspec(ts, D_MODEL), _rowspec(ts, D_MODEL, 0), _rowspec(ts, D_MODEL, 1), _rowspec(ts, D_MODEL),
                  _rowspec(ts, D_MODEL, 0), _rowspec(ts, D_MODEL, 1), _vecspec(D_MODEL)],
        out_specs=[_rowspec(ts, D_MODEL), _rowspec(ts, D_MODEL), _rowspec(ts, 2 * D_MODEL),
                   pl.BlockSpec((128, ts), lambda i: (0, i)), _vecspec(D_MODEL)],
        out_shape=[jax.ShapeDtypeStruct((s, D_MODEL), bf16), jax.ShapeDtypeStruct((s, D_MODEL), bf16),
                   jax.ShapeDtypeStruct((s, 2 * D_MODEL), bf16), jax.ShapeDtypeStruct((128, s), f32),
                   jax.ShapeDtypeStruct((1, D_MODEL), f32)],
        compiler_params=_cp("arbitrary"))(y, z, z, o, dcat, dcat, ssd_norm)


def _conv_grid(s, c):
    ts, cb = min(CONV_ROW_TILE, s), min(CONV_COL_TILE, c)
    return ts, cb, (c // cb, s // ts)


def _cur(ts, cb, off=0):
    return pl.BlockSpec((ts, cb), lambda c, i: (i, c + off))


def _prev_halo(ts, cb, off=0):
    return pl.BlockSpec((HALO, cb), lambda c, i: (jnp.maximum(i * (ts // HALO) - 1, 0), c + off))


def _next_halo(ts, cb, s, off=0):
    return pl.BlockSpec((HALO, cb), lambda c, i: (jnp.minimum((i + 1) * (ts // HALO), s // HALO - 1), c + off))


def _wspec(k, cb):
    return pl.BlockSpec((k, cb), lambda c, i: (0, c))


def _phases(offsets):
    return sorted({o % 8 for o in offsets} - {0})


def _shift_scratch(offsets, ts, cb):
    return pltpu.VMEM((max(len(_phases(offsets)), 1), ts + HALO - 8, cb), f32)


def _fill_phases(ext_ref, sh_ref, offsets, ts):
    for j, r in enumerate(_phases(offsets)):
        sh_ref[j] = ext_ref[pl.ds(r, ts + HALO - 8), :]


def _slab(ext_ref, sh_ref, offsets, off, start):
    r = off % 8
    a = off - r + start
    if r == 0:
        return ext_ref[a:a + CONV_SUB, :]
    return sh_ref[_phases(offsets).index(r), a:a + CONV_SUB, :]


def _conv_taps(ext_ref, sh_ref, w_ref, b_ref, ts, k_taps, emit):
    offsets = [HALO - (k_taps - 1) + k for k in range(k_taps)]
    _fill_phases(ext_ref, sh_ref, offsets, ts)
    for sb in range(ts // CONV_SUB):
        acc = b_ref[...]
        for k in range(k_taps):
            acc = acc + w_ref[k:k + 1, :] * _slab(ext_ref, sh_ref, offsets, offsets[k], sb * CONV_SUB)
        emit(slice(sb * CONV_SUB, (sb + 1) * CONV_SUB), acc)


def _conv_ssd_fwd(xraw, w, b):
    s, c = xraw.shape
    ts, cb, grid = _conv_grid(s, c)
    offsets = [HALO - (SSD_CONV - 1) + k for k in range(SSD_CONV)]

    def body(x_ref, xh_ref, w_ref, b_ref, pre_ref, act_ref, ext_ref, sh_ref):
        first = pl.program_id(1) == 0
        ext_ref[0:HALO, :] = jnp.where(first, 0.0, xh_ref[...].astype(f32))
        ext_ref[HALO:, :] = x_ref[...].astype(f32)

        def emit(rows, pre):
            pre_ref[rows, :] = pre.astype(bf16)
            act_ref[rows, :] = _silu(pre).astype(bf16)
        _conv_taps(ext_ref, sh_ref, w_ref, b_ref, ts, SSD_CONV, emit)

    return pl.pallas_call(
        body, name="conv_ssd_fwd", grid=grid,
        in_specs=[_cur(ts, cb), _prev_halo(ts, cb), _wspec(SSD_CONV, cb), _wspec(1, cb)],
        out_specs=[_cur(ts, cb), _cur(ts, cb)],
        out_shape=[jax.ShapeDtypeStruct((s, c), bf16)] * 2,
        scratch_shapes=[pltpu.VMEM((HALO + ts, cb), f32), _shift_scratch(offsets, ts, cb)],
        compiler_params=_cp("parallel", "parallel"))(xraw, xraw, w, b)


def _conv_glu_fwd(proj1, w, b):
    s = proj1.shape[0]
    c = D_CONV
    ts, cb, grid = _conv_grid(s, c)
    goff = c // cb

    offsets = [HALO - (CONV_WIDTH - 1) + k for k in range(CONV_WIDTH)]

    def body(v_ref, g_ref, vh_ref, gh_ref, w_ref, b_ref, hc_ref, ext_ref, sh_ref):
        first = pl.program_id(1) == 0
        hh = vh_ref[...].astype(f32) * _sigmoid(gh_ref[...].astype(f32))
        ext_ref[0:HALO, :] = jnp.where(first, 0.0, hh)
        ext_ref[HALO:, :] = v_ref[...].astype(f32) * _sigmoid(g_ref[...].astype(f32))

        def emit(rows, hc):
            hc_ref[rows, :] = hc.astype(bf16)
        _conv_taps(ext_ref, sh_ref, w_ref, b_ref, ts, CONV_WIDTH, emit)

    return pl.pallas_call(
        body, name="conv_glu_fwd", grid=grid,
        in_specs=[_cur(ts, cb), _cur(ts, cb, goff), _prev_halo(ts, cb), _prev_halo(ts, cb, goff),
                  _wspec(CONV_WIDTH, cb), _wspec(1, cb)],
        out_specs=_cur(ts, cb),
        out_shape=jax.ShapeDtypeStruct((s, c), bf16),
        scratch_shapes=[pltpu.VMEM((HALO + ts, cb), f32), _shift_scratch(offsets, ts, cb)],
        compiler_params=_cp("parallel", "parallel"))(proj1, proj1, proj1, proj1, w, b)


def _conv_bwd_offsets(k_taps):
    return [k_taps - 1 - k for k in range(k_taps)], [HALO - (k_taps - 1) + k for k in range(k_taps)]


def _conv_bwd_scratch(k_taps, ts, cb):
    d_offs, x_offs = _conv_bwd_offsets(k_taps)
    return [pltpu.VMEM((ts + HALO, cb), f32), _shift_scratch(d_offs, ts, cb),
            pltpu.VMEM((HALO + ts, cb), f32), _shift_scratch(x_offs, ts, cb),
            pltpu.VMEM((k_taps, 8, cb), f32), pltpu.VMEM((8, cb), f32)]


def _conv_bwd_core(dp, dpn_ref, last, w_ref, scratch, dw_ref, db_ref, ts, k_taps, emit):
    dext_ref, dsh_ref, xext_ref, xsh_ref, dw8_ref, db8_ref = scratch
    d_offs, x_offs = _conv_bwd_offsets(k_taps)
    dext_ref[0:ts, :] = dp
    dext_ref[ts:, :] = jnp.where(last, 0.0, dpn_ref[...].astype(f32))
    _fill_phases(dext_ref, dsh_ref, d_offs, ts)
    _fill_phases(xext_ref, xsh_ref, x_offs, ts)

    @pl.when(pl.program_id(1) == 0)
    def _():
        dw8_ref[...] = jnp.zeros_like(dw8_ref)
        db8_ref[...] = jnp.zeros_like(db8_ref)
    cb = dp.shape[1]
    for sb in range(ts // CONV_SUB):
        start = sb * CONV_SUB
        dpv = dext_ref[start:start + CONV_SUB, :]
        dx = None
        for k in range(k_taps):
            t = w_ref[k:k + 1, :] * _slab(dext_ref, dsh_ref, d_offs, d_offs[k], start)
            dx = t if dx is None else dx + t
            prod = dpv * _slab(xext_ref, xsh_ref, x_offs, x_offs[k], start)
            dw8_ref[k] += jnp.sum(prod.reshape(CONV_SUB // 8, 8, cb), axis=0)
        db8_ref[...] += jnp.sum(dpv.reshape(CONV_SUB // 8, 8, cb), axis=0)
        emit(slice(start, start + CONV_SUB), dx)

    @pl.when(last)
    def _():
        dw_ref[...] = jnp.sum(dw8_ref[...], axis=1)
        db_ref[...] = jnp.sum(db8_ref[...], axis=0, keepdims=True)


def _conv_ssd_bwd(dpre, xraw, w):
    s, c = xraw.shape
    ts, cb, grid = _conv_grid(s, c)
    nb = s // ts

    def body(dp_ref, dpn_ref, x_ref, xh_ref, w_ref, dx_ref, dw_ref, db_ref, *scratch):
        i = pl.program_id(1)
        xext_ref = scratch[2]
        xext_ref[0:HALO, :] = jnp.where(i == 0, 0.0, xh_ref[...].astype(f32))
        xext_ref[HALO:, :] = x_ref[...].astype(f32)

        def emit(rows, dx):
            dx_ref[rows, :] = dx.astype(bf16)
        _conv_bwd_core(dp_ref[...].astype(f32), dpn_ref, i == nb - 1, w_ref, scratch, dw_ref, db_ref, ts, SSD_CONV, emit)

    return pl.pallas_call(
        body, name="conv_ssd_bwd", grid=grid,
        in_specs=[_cur(ts, cb), _next_halo(ts, cb, s), _cur(ts, cb), _prev_halo(ts, cb), _wspec(SSD_CONV, cb)],
        out_specs=[_cur(ts, cb), _wspec(SSD_CONV, cb), _wspec(1, cb)],
        out_shape=[jax.ShapeDtypeStruct((s, c), bf16), jax.ShapeDtypeStruct((SSD_CONV, c), f32),
                   jax.ShapeDtypeStruct((1, c), f32)],
        scratch_shapes=_conv_bwd_scratch(SSD_CONV, ts, cb),
        compiler_params=_cp("parallel", "arbitrary"))(dpre, dpre, xraw, xraw, w)


def _conv_glu_bwd(dhc, proj1, w):
    s = proj1.shape[0]
    c = D_CONV
    ts, cb, grid = _conv_grid(s, c)
    nb = s // ts
    goff = c // cb

    def body(dp_ref, dpn_ref, v_ref, g_ref, vh_ref, gh_ref, w_ref, dv_ref, dg_ref, dw_ref, db_ref, *scratch):
        i = pl.program_id(1)
        xext_ref = scratch[2]
        xext_ref[0:HALO, :] = jnp.where(i == 0, 0.0, vh_ref[...].astype(f32) * _sigmoid(gh_ref[...].astype(f32)))
        xext_ref[HALO:, :] = v_ref[...].astype(f32) * _sigmoid(g_ref[...].astype(f32))

        def emit(rows, dh):
            val = v_ref[rows, :].astype(f32)
            sg = _sigmoid(g_ref[rows, :].astype(f32))
            dv_ref[rows, :] = (dh * sg).astype(bf16)
            dg_ref[rows, :] = (dh * val * sg * (1.0 - sg)).astype(bf16)
        _conv_bwd_core(dp_ref[...].astype(f32), dpn_ref, i == nb - 1, w_ref, scratch, dw_ref, db_ref, ts, CONV_WIDTH, emit)

    return pl.pallas_call(
        body, name="conv_glu_bwd", grid=grid,
        in_specs=[_cur(ts, cb), _next_halo(ts, cb, s), _cur(ts, cb), _cur(ts, cb, goff), _prev_halo(ts, cb),
                  _prev_halo(ts, cb, goff), _wspec(CONV_WIDTH, cb)],
        out_specs=[_cur(ts, cb), _cur(ts, cb), _wspec(CONV_WIDTH, cb), _wspec(1, cb)],
        out_shape=[jax.ShapeDtypeStruct((s, c), bf16), jax.ShapeDtypeStruct((s, c), bf16),
                   jax.ShapeDtypeStruct((CONV_WIDTH, c), f32), jax.ShapeDtypeStruct((1, c), f32)],
        scratch_shapes=_conv_bwd_scratch(CONV_WIDTH, ts, cb),
        compiler_params=_cp("parallel", "arbitrary"))(dhc, dhc, proj1, proj1, proj1, proj1, w)


def _ssd_common(dt_ref, prm_ref):
    l = CHUNK
    dtb = prm_ref[0:1, :]
    a = -jnp.exp(prm_ref[1:2, :])
    dsk = prm_ref[2:3, :]
    zraw = dt_ref[...] + dtb
    dt = _softplus(zraw)
    da = dt * a
    row = lax.broadcasted_iota(jnp.int32, (l, l), 0)
    col = lax.broadcasted_iota(jnp.int32, (l, l), 1)
    causal = row >= col
    cs = _dot(causal.astype(f32), da, precision=HIGHEST)
    return a, dsk, zraw, dt, cs, cs.T, causal, row, col


def _ssd_fwd(act, dtf, prm):
    s = act.shape[0]
    nc = s // CHUNK
    l = CHUNK

    def body(xs_ref, dt_ref, prm_ref, y_ref, hs_ref, st_ref):
        @pl.when(pl.program_id(0) == 0)
        def _():
            st_ref[...] = jnp.zeros_like(st_ref)
        a, dsk, _, dt, cs, cst, causal, _, _ = _ssd_common(dt_ref, prm_ref)
        for g in range(N_GROUPS):
            bm = xs_ref[:, B_OFF + D_STATE * g:B_OFF + D_STATE * (g + 1)]
            cm = xs_ref[:, C_OFF + D_STATE * g:C_OFF + D_STATE * (g + 1)]
            gmat = _dot(cm, bm, NT)
            for r in range(HEADS_PER_GROUP):
                h = HEADS_PER_GROUP * g + r
                hsl = slice(HEAD_DIM * h, HEAD_DIM * (h + 1))
                xv = xs_ref[:, hsl].astype(f32)
                csc = cs[:, h:h + 1]
                csr = cst[h:h + 1, :]
                cl = cs[l - 1:l, h:h + 1]
                dk = jnp.exp(jnp.where(causal, csc - csr, NEG))
                xd = xv * dt[:, h:h + 1]
                hp = st_ref[h]
                hs_ref[0, h] = hp
                ydiag = _dot((gmat * dk).astype(bf16), xd.astype(bf16))
                yoff = _dot(cm, hp.astype(bf16), NT) * jnp.exp(csc)
                y_ref[:, hsl] = (ydiag + yoff + xv * dsk[:, h:h + 1]).astype(bf16)
                st = _dot((xd * jnp.exp(cl - csc)).astype(bf16), bm, TN)
                st_ref[h] = hp * jnp.exp(cl) + st

    return pl.pallas_call(
        body, name="ssd_fwd", grid=(nc,),
        in_specs=[pl.BlockSpec((l, XBC_W), lambda i: (i, 0)), pl.BlockSpec((l, 128), lambda i: (i, 0)),
                  pl.BlockSpec((8, 128), lambda i: (0, 0))],
        out_specs=[pl.BlockSpec((l, D_MODEL), lambda i: (i, 0)),
                   pl.BlockSpec((1, N_HEADS, HEAD_DIM, D_STATE), lambda i: (i, 0, 0, 0))],
        out_shape=[jax.ShapeDtypeStruct((s, D_MODEL), bf16),
                   jax.ShapeDtypeStruct((nc, N_HEADS, HEAD_DIM, D_STATE), f32)],
        scratch_shapes=[pltpu.VMEM((N_HEADS, HEAD_DIM, D_STATE), f32)],
        compiler_params=_cp("arbitrary"))(act, dtf, prm)


def _ssd_bwd(act, pre, dtf, prm, hs, dy):
    s = act.shape[0]
    nc = s // CHUNK
    l = CHUNK

    def body(xs_ref, pre_ref, dt_ref, prm_ref, hs_ref, dy_ref, dpre_ref, ddt_ref, dprm_ref, dh_ref):
        @pl.when(pl.program_id(0) == 0)
        def _():
            dh_ref[...] = jnp.zeros_like(dh_ref)
            dprm_ref[...] = jnp.zeros_like(dprm_ref)
        a, dsk, zraw, dt, cs, cst, causal, row, col = _ssd_common(dt_ref, prm_ref)
        lane = lax.broadcasted_iota(jnp.int32, (l, 128), 1)
        rowl = lax.broadcasted_iota(jnp.int32, (l, 128), 0)
        sub = lax.broadcasted_iota(jnp.int32, (128, l), 0)
        lane1 = lax.broadcasted_iota(jnp.int32, (1, 128), 1)
        dcs_c = jnp.zeros((l, 128), f32)
        dcs_r = jnp.zeros((128, l), f32)
        ddt_c = jnp.zeros((l, 128), f32)
        dd_row = jnp.zeros((1, 128), f32)
        ones_cols = jnp.ones((HEAD_DIM, 128), bf16)

        def lanesum(xv64):
            hi = xv64.astype(bf16)
            lo = (xv64 - hi.astype(f32)).astype(bf16)
            return _dot(hi, ones_cols) + _dot(lo, ones_cols)

        for g in range(N_GROUPS):
            bsl = slice(B_OFF + D_STATE * g, B_OFF + D_STATE * (g + 1))
            csl = slice(C_OFF + D_STATE * g, C_OFF + D_STATE * (g + 1))
            bm = xs_ref[:, bsl]
            cm = xs_ref[:, csl]
            gmat = _dot(cm, bm, NT)
            gmat_t = _dot(bm, cm, NT)
            dgm = jnp.zeros((l, l), f32)
            dbg = jnp.zeros((l, D_STATE), f32)
            dcg = jnp.zeros((l, D_STATE), f32)
            for r in range(HEADS_PER_GROUP):
                h = HEADS_PER_GROUP * g + r
                hsl = slice(HEAD_DIM * h, HEAD_DIM * (h + 1))
                xv = xs_ref[:, hsl].astype(f32)
                dyv = dy_ref[:, hsl].astype(f32)
                dyb = dyv.astype(bf16)
                csc = cs[:, h:h + 1]
                csr = cst[h:h + 1, :]
                cl = cs[l - 1:l, h:h + 1]
                dk = jnp.exp(jnp.where(causal, csc - csr, NEG))
                mf = gmat * dk
                dtc = dt[:, h:h + 1]
                xd = xv * dtc
                xdb = xd.astype(bf16)
                ecs = jnp.exp(csc)
                dec = jnp.exp(cl)
                e = jnp.exp(cl - csc)
                hp = hs_ref[0, h]
                hpb = hp.astype(bf16)
                dhn = dh_ref[h]
                dhnb = dhn.astype(bf16)
                dd_h = jnp.sum(jnp.sum(dyv * xv, axis=1, keepdims=True), axis=0, keepdims=True)
                dx = dyv * dsk[:, h:h + 1]
                ch = _dot(cm, hpb, NT)
                dye = dyv * ecs
                dyeb = dye.astype(bf16)
                dcg = dcg + _dot(dyeb, hpb)
                dhp = _dot(dyeb, cm, TN)
                dcs_b = lanesum(dye * ch)
                dm = _dot(dyb, xdb, NT)
                mft = gmat_t * jnp.exp(jnp.where(row <= col, csr - csc, NEG))
                dxd = _dot(mft.astype(bf16), dyb)
                dgm = dgm + dm * dk
                wmat = dm * mf
                wm_col = jnp.sum(wmat, axis=1, keepdims=True)
                dcs_row = -jnp.sum(wmat, axis=0, keepdims=True)
                ddec = jnp.sum(jnp.sum(hp * dhn, axis=1, keepdims=True), axis=0, keepdims=True)
                dxe = _dot(bm, dhnb, NT)
                dxd = dxd + dxe * e
                xde = xd * e
                de_b = lanesum(dxe * xde)
                dbg = dbg + _dot(xde.astype(bf16), dhnb)
                dcs_b = dcs_b - de_b
                dlast = ddec * dec + jnp.sum(de_b, axis=0, keepdims=True)
                dh_ref[h] = dhp + dec * dhn
                dx = dx + dxd * dtc
                ddt_b = lanesum(dxd * xv)
                is_h = lane == h
                dcs_c = dcs_c + jnp.where(is_h, dcs_b + wm_col, 0.0) + jnp.where(is_h & (rowl == l - 1), dlast, 0.0)
                dcs_r = dcs_r + jnp.where(sub == h, dcs_row, 0.0)
                ddt_c = ddt_c + jnp.where(is_h, ddt_b, 0.0)
                dd_row = dd_row + jnp.where(lane1 == h, dd_h, 0.0)
                dpre_ref[:, hsl] = (dx * _dsilu(pre_ref[:, hsl].astype(f32))).astype(bf16)
            dgb = dgm.astype(bf16)
            dcg = dcg + _dot(dgb, bm)
            dbg = dbg + _dot(dgb, cm, TN)
            dpre_ref[:, bsl] = (dbg * _dsilu(pre_ref[:, bsl].astype(f32))).astype(bf16)
            dpre_ref[:, csl] = (dcg * _dsilu(pre_ref[:, csl].astype(f32))).astype(bf16)
        dcs = dcs_c + dcs_r.T
        dda = _dot((row <= col).astype(f32), dcs, precision=HIGHEST)
        ddt = ddt_c + dda * a
        ddtraw = jnp.where(lane < N_HEADS, ddt * _sigmoid(zraw), 0.0)
        ddt_ref[...] = ddtraw
        dprm_ref[0:1, :] += jnp.sum(ddtraw, axis=0, keepdims=True)
        dprm_ref[1:2, :] += jnp.where(lane1 < N_HEADS, jnp.sum(dda * dt, axis=0, keepdims=True) * a, 0.0)
        dprm_ref[2:3, :] += dd_row

    def rev(i):
        return (nc - 1 - i, 0)

    return pl.pallas_call(
        body, name="ssd_bwd", grid=(nc,),
        in_specs=[pl.BlockSpec((l, XBC_W), rev), pl.BlockSpec((l, XBC_W), rev),
                  pl.BlockSpec((l, 128), rev), pl.BlockSpec((8, 128), lambda i: (0, 0)),
                  pl.BlockSpec((1, N_HEADS, HEAD_DIM, D_STATE), lambda i: (nc - 1 - i, 0, 0, 0)),
                  pl.BlockSpec((l, D_MODEL), rev)],
        out_specs=[pl.BlockSpec((l, XBC_W), rev), pl.BlockSpec((l, 128), rev), pl.BlockSpec((8, 128), lambda i: (0, 0))],
        out_shape=[jax.ShapeDtypeStruct((s, XBC_W), bf16), jax.ShapeDtypeStruct((s, 128), f32),
                   jax.ShapeDtypeStruct((8, 128), f32)],
        scratch_shapes=[pltpu.VMEM((N_HEADS, HEAD_DIM, D_STATE), f32)],
        compiler_params=_cp("arbitrary"))(act, pre, dtf, prm, hs, dy)


def _fox_cumsum(dtf, prm):
    s = dtf.shape[0]
    l = CHUNK

    def body(f_ref, prm_ref, c_ref, carry_ref):
        @pl.when(pl.program_id(0) == 0)
        def _():
            carry_ref[...] = jnp.zeros_like(carry_ref)
        lf = _log_sigmoid(f_ref[...] + prm_ref[3:4, :])
        row = lax.broadcasted_iota(jnp.int32, (l, l), 0)
        col = lax.broadcasted_iota(jnp.int32, (l, l), 1)
        c = _dot((row >= col).astype(f32), lf, precision=HIGHEST) + carry_ref[...]
        c_ref[...] = c
        carry_ref[...] = c[l - 1:l, :]

    return pl.pallas_call(
        body, name="fox_cumsum", grid=(s // l,),
        in_specs=[pl.BlockSpec((l, 128), lambda i: (i, 0)), pl.BlockSpec((8, 128), lambda i: (0, 0))],
        out_specs=pl.BlockSpec((l, 128), lambda i: (i, 0)),
        out_shape=jax.ShapeDtypeStruct((s, 128), f32),
        scratch_shapes=[pltpu.VMEM((1, 128), f32)],
        compiler_params=_cp("arbitrary"))(dtf, prm)


def _position():
    return lax.axis_index("x"), lax.axis_index("y"), lax.axis_index("c")


def _exchange_sems(n):
    return [pltpu.SemaphoreType.DMA((n, N_DEV - 1)), pltpu.SemaphoreType.DMA((n, N_DEV - 1)),
            pltpu.SemaphoreType.DMA((n,))]


def _exchange_copies(g_refs, r_refs, send_sems, recv_sems, local_sems, gather=False):
    n = len(g_refs)
    x, y, cc = _position()
    me = 4 * x + 2 * y + cc

    def src(a, j):
        return g_refs[a] if gather else g_refs[a].at[j]

    local = [pltpu.make_async_copy(src(a, me), r_refs[a].at[me], local_sems.at[a]) for a in range(n)]
    sends, recvs = [], []
    for k in range(1, N_DEV):
        px = 1 - x if k & 4 else x
        py = 1 - y if k & 2 else y
        pc = 1 - cc if k & 1 else cc
        pid = 4 * px + 2 * py + pc
        for a in range(n):
            sends.append(pltpu.make_async_remote_copy(
                src_ref=src(a, pid), dst_ref=r_refs[a].at[me],
                send_sem=send_sems.at[a, k - 1], recv_sem=recv_sems.at[a, k - 1],
                device_id=(px, py, pc), device_id_type=pl.DeviceIdType.MESH))
            recvs.append(pltpu.make_async_remote_copy(
                src_ref=src(a, pid), dst_ref=r_refs[a].at[pid],
                send_sem=send_sems.at[a, k - 1], recv_sem=recv_sems.at[a, k - 1],
                device_id=(px, py, pc), device_id_type=pl.DeviceIdType.MESH))
    return local, sends, recvs


def _exchange_start(copies):
    local, sends, _ = copies
    for cp in local + sends:
        cp.start()


def _exchange_wait(copies):
    local, sends, recvs = copies
    for cp in recvs:
        cp.wait_recv()
    for cp in sends:
        cp.wait_send()
    for cp in local:
        cp.wait()


AUG = HEAD_DIM
N_PAIRS = N_HEADS // 2
V_BLOCK = 2 * D_MODEL // 128


def _split3(x):
    hi = x.astype(bf16)
    r1 = x - hi.astype(f32)
    mid = r1.astype(bf16)
    lo = (r1 - mid.astype(f32)).astype(bf16)
    return hi.astype(f32), mid.astype(f32), lo.astype(f32)


def _fox_prep(qkv, c):
    s = qkv.shape[0]
    ts = min(CONV_ROW_TILE, s)

    def body(q_ref, k_ref, c_ref, qa_ref, ka_ref):
        lane = lax.broadcasted_iota(jnp.int32, (ts, 128), 1)
        low = lane < HEAD_DIM
        for h in range(N_HEADS):
            psl = slice(128 * (h // 2), 128 * (h // 2 + 1))
            qv = q_ref[:, psl].astype(f32) * (HEAD_DIM ** -0.5)
            kv = k_ref[:, psl].astype(f32)
            if h % 2:
                qv = pltpu.roll(qv, HEAD_DIM, 1)
                kv = pltpu.roll(kv, HEAD_DIM, 1)
            hi, mid, lo = _split3(c_ref[:, F_LANE + h:F_LANE + h + 1])
            ones = jnp.where((lane >= AUG + 3) & (lane < AUG + 6), 1.0, 0.0)
            cq = jnp.where(lane == AUG, hi, jnp.where(lane == AUG + 1, mid, jnp.where(lane == AUG + 2, lo, ones)))
            qa_ref[h] = jnp.where(low, qv, cq).astype(bf16)
            onek = jnp.where((lane >= AUG) & (lane < AUG + 3), 1.0, 0.0)
            ck = jnp.where(lane == AUG + 3, -hi, jnp.where(lane == AUG + 4, -mid, jnp.where(lane == AUG + 5, -lo, onek)))
            ka_ref[h] = jnp.where(low, kv, ck).astype(bf16)

    hm = pl.BlockSpec((N_HEADS, ts, 128), lambda i: (0, i, 0))
    return pl.pallas_call(
        body, name="fox_prep", grid=(s // ts,),
        in_specs=[_rowspec(ts, D_MODEL, 0), _rowspec(ts, D_MODEL, 1), _rowspec(ts, 128)],
        out_specs=[hm, hm], out_shape=[jax.ShapeDtypeStruct((N_HEADS, s, 128), bf16)] * 2,
        compiler_params=_cp("parallel"))(qkv, qkv, c)


def _fox_fwd(qa, ka, qkv, ws):
    s = qkv.shape[0]
    t = min(ATTN_FWD_TILE, s)
    nq = s // t
    n = len(ws)

    def body(qa_ref, ka_ref, v_ref, *rest):
        w_refs, (o_ref, lse_ref), wg_refs, sems = rest[:n], rest[n:n + 2], rest[n + 2:2 * n + 2], rest[2 * n + 2:]
        qi = pl.program_id(1)
        copies = _exchange_copies(w_refs, wg_refs, *sems, gather=True)

        @pl.when((pl.program_id(0) == 0) & (qi == 0))
        def _():
            _exchange_start(copies)
        low = lax.broadcasted_iota(jnp.int32, (t, 128), 1) < HEAD_DIM
        row = lax.broadcasted_iota(jnp.int32, (t, t), 0)
        col = lax.broadcasted_iota(jnp.int32, (t, t), 1)

        def tile(ki, carry, diagonal):
            koff = pl.multiple_of(ki * t, t)
            v = v_ref[pl.ds(koff, t), :]
            vh = (jnp.where(low, v, jnp.ones_like(v)), jnp.where(low, jnp.ones_like(v), v))
            new = []
            for r in range(2):
                m_old, acc = carry[r]
                sc = _dot(qa_ref[r], ka_ref[r, pl.ds(koff, t), :], NT)
                if diagonal:
                    sc = jnp.where(col <= row, sc, NEG)
                m_new = jnp.maximum(m_old, jnp.max(sc, axis=1, keepdims=True))
                p = jnp.exp(sc - m_new)
                new.append((m_new, acc * jnp.exp(m_old - m_new) + _dot(p.astype(bf16), vh[r])))
            return tuple(new)

        init = ((jnp.full((t, 1), NEG, f32), jnp.zeros((t, 128), f32)),) * 2
        carry = lax.fori_loop(0, qi, lambda ki, cr: tile(ki, cr, False), init)
        (m_a, acc_a), (m_b, acc_b) = tile(qi, carry, True)
        l_a, l_b = acc_a[:, HEAD_DIM:HEAD_DIM + 1], acc_b[:, 0:1]
        o_ref[...] = jnp.where(low, acc_a / l_a, acc_b / l_b).astype(bf16)
        for r, lse in enumerate((m_a + jnp.log(l_a), m_b + jnp.log(l_b))):
            lse_ref[r] = jnp.broadcast_to(lse, (t, 128)).T[0:1, :]

        @pl.when((pl.program_id(0) == N_PAIRS - 1) & (qi == nq - 1))
        def _():
            _exchange_wait(copies)

    anyspec = pl.BlockSpec(memory_space=pl.ANY)
    outs = pl.pallas_call(
        body, name="fox_fwd", grid=(N_PAIRS, nq),
        in_specs=[pl.BlockSpec((2, t, 128), lambda j, qi: (j, qi, 0)),
                  pl.BlockSpec((2, s, 128), lambda j, qi: (j, 0, 0)),
                  pl.BlockSpec((s, 128), lambda j, qi: (0, V_BLOCK + j))] + [anyspec] * n,
        out_specs=[pl.BlockSpec((t, 128), lambda j, qi: (qi, j)), pl.BlockSpec((2, 1, t), lambda j, qi: (j, 0, qi))]
        + [anyspec] * n,
        out_shape=[jax.ShapeDtypeStruct((s, D_MODEL), bf16), jax.ShapeDtypeStruct((N_HEADS, 1, s), f32)]
        + [jax.ShapeDtypeStruct((N_DEV,) + w.shape, w.dtype) for w in ws],
        scratch_shapes=_exchange_sems(n),
        compiler_params=_cp("arbitrary", "arbitrary"))(qa, ka, qkv, *ws)
    return outs[0], outs[1], outs[2:]


def _fox_bwd(qa, ka, qkv, do, lse, delta, gs):
    s = qkv.shape[0]
    t = min(ATTN_TILE, s)
    nq = s // t
    n = len(gs)

    def body(qa_ref, ka_ref, v_ref, do_ref, lse_ref, dl_ref, *rest):
        g_refs, (dq_ref, dk_ref, dv_ref), r_refs, sems = rest[:n], rest[n:n + 3], rest[n + 3:2 * n + 3], rest[2 * n + 3:]
        ki = pl.program_id(1)
        copies = _exchange_copies(g_refs, r_refs, *sems)

        @pl.when((pl.program_id(0) == 0) & (ki == 0))
        def _():
            _exchange_start(copies)

        @pl.when(ki == 0)
        def _():
            dq_ref[...] = jnp.zeros_like(dq_ref)
        def block(carry, k_lo, n, qoff, diagonal):
            dks, dv = carry
            low = lax.broadcasted_iota(jnp.int32, (n, 128), 1) < HEAD_DIM
            v = v_ref[k_lo:k_lo + n, :]
            zero = jnp.zeros_like(v)
            vh = (jnp.where(low, v, zero), jnp.where(low, zero, v))
            dov = do_ref[pl.ds(qoff, n), :]
            doh = (jnp.where(low, dov, zero), jnp.where(low, zero, dov))
            new_dks = []
            for r in range(2):
                qt = qa_ref[r, pl.ds(qoff, n), :]
                kt = ka_ref[r, k_lo:k_lo + n, :]
                sct = _dot(kt, qt, NT)
                if diagonal:
                    row = lax.broadcasted_iota(jnp.int32, (n, n), 0)
                    col = lax.broadcasted_iota(jnp.int32, (n, n), 1)
                    sct = jnp.where(row <= col, sct, NEG)
                pt = jnp.exp(sct - lse_ref[r, :, pl.ds(qoff, n)])
                dpt = _dot(vh[r], dov, NT)
                dst = (pt * (dpt - dl_ref[r, :, pl.ds(qoff, n)])).astype(bf16)
                dv = dv + _dot(pt.astype(bf16), doh[r])
                new_dks.append(dks[r] + _dot(dst, qt))
                dq_ref[r, pl.ds(qoff, n), :] += _dot(dst, kt, TN)
            return tuple(new_dks), dv

        h = t // 2
        zacc = jnp.zeros((h, 128), f32)
        zero_carry = ((zacc, zacc), zacc)
        q0 = pl.multiple_of(ki * t, t)
        q1 = pl.multiple_of(ki * t + h, h)
        first = block(block(zero_carry, 0, h, q0, True), 0, h, q1, False)
        second = block(zero_carry, h, h, q1, True)
        carry = (tuple(jnp.concatenate([first[0][r], second[0][r]], axis=0) for r in range(2)),
                 jnp.concatenate([first[1], second[1]], axis=0))
        dks, dv = lax.fori_loop(
            ki + 1, nq, lambda qi, cr: block(cr, 0, t, pl.multiple_of(qi * t, t), False), carry)
        dk_ref[0] = dks[0]
        dk_ref[1] = dks[1]
        dv_ref[...] = dv.astype(bf16)

        @pl.when((pl.program_id(0) == N_PAIRS - 1) & (ki == nq - 1))
        def _():
            _exchange_wait(copies)

    anyspec = pl.BlockSpec(memory_space=pl.ANY)
    outs = pl.pallas_call(
        body, name="fox_bwd", grid=(N_PAIRS, nq),
        in_specs=[pl.BlockSpec((2, s, 128), lambda j, ki: (j, 0, 0)),
                  pl.BlockSpec((2, t, 128), lambda j, ki: (j, ki, 0)),
                  pl.BlockSpec((t, 128), lambda j, ki: (ki, V_BLOCK + j)),
                  pl.BlockSpec((s, 128), lambda j, ki: (0, j)),
                  pl.BlockSpec((2, 1, s), lambda j, ki: (j, 0, 0)),
                  pl.BlockSpec((2, 1, s), lambda j, ki: (j, 0, 0))] + [anyspec] * n,
        out_specs=[pl.BlockSpec((2, s, 128), lambda j, ki: (j, 0, 0)),
                   pl.BlockSpec((2, t, 128), lambda j, ki: (j, ki, 0)),
                   pl.BlockSpec((t, 128), lambda j, ki: (ki, j))] + [anyspec] * n,
        out_shape=[jax.ShapeDtypeStruct((N_HEADS, s, 128), f32), jax.ShapeDtypeStruct((N_HEADS, s, 128), f32),
                   jax.ShapeDtypeStruct((s, D_MODEL), bf16)] + [jax.ShapeDtypeStruct(g.shape, g.dtype) for g in gs],
        scratch_shapes=_exchange_sems(n),
        compiler_params=_cp("arbitrary", "arbitrary"))(qa, ka, qkv, do, lse, delta, *gs)
    return outs[0], outs[1], outs[2], outs[3:]


def _fox_bwd_post(dq_hm, dk_hm):
    s = dq_hm.shape[1]
    ts = min(CONV_ROW_TILE, s)

    def body(dq_ref, dk_ref, q_ref, k_ref, dc_ref):
        lane = lax.broadcasted_iota(jnp.int32, (ts, 128), 1)
        dc = jnp.zeros((ts, 128), f32)
        for h in range(N_HEADS):
            hsl = slice(HEAD_DIM * h, HEAD_DIM * (h + 1))
            dqv = dq_ref[h]
            dkv = dk_ref[h]
            q_ref[:, hsl] = (dqv[:, 0:HEAD_DIM] * (HEAD_DIM ** -0.5)).astype(bf16)
            k_ref[:, hsl] = dkv[:, 0:HEAD_DIM].astype(bf16)
            dc = dc + jnp.where(lane == F_LANE + h, dqv[:, AUG:AUG + 1] - dkv[:, AUG + 3:AUG + 4], 0.0)
        dc_ref[...] = dc

    hm = pl.BlockSpec((N_HEADS, ts, 128), lambda i: (0, i, 0))
    return pl.pallas_call(
        body, name="fox_bwd_post", grid=(s // ts,), in_specs=[hm, hm],
        out_specs=[_rowspec(ts, D_MODEL), _rowspec(ts, D_MODEL), _rowspec(ts, 128)],
        out_shape=[jax.ShapeDtypeStruct((s, D_MODEL), bf16), jax.ShapeDtypeStruct((s, D_MODEL), bf16),
                   jax.ShapeDtypeStruct((s, 128), f32)],
        compiler_params=_cp("parallel"))(dq_hm, dk_hm)


def _fox_gate_bwd(dc, dtf, prm, ddt_raw):
    s = dtf.shape[0]
    l = CHUNK
    nb = s // l

    def body(dc_ref, f_ref, prm_ref, ddt_ref, out_ref, dfb_ref, carry_ref):
        @pl.when(pl.program_id(0) == 0)
        def _():
            carry_ref[...] = jnp.zeros_like(carry_ref)
            dfb_ref[...] = jnp.zeros_like(dfb_ref)
        dc = dc_ref[...]
        row = lax.broadcasted_iota(jnp.int32, (l, l), 0)
        col = lax.broadcasted_iota(jnp.int32, (l, l), 1)
        dlf = _dot((row <= col).astype(f32), dc, precision=HIGHEST) + carry_ref[...]
        carry_ref[...] = dlf[0:1, :]
        lane = lax.broadcasted_iota(jnp.int32, (l, 128), 1)
        is_f = (lane >= F_LANE) & (lane < F_LANE + N_HEADS)
        dfr = jnp.where(is_f, dlf * _sigmoid(-(f_ref[...] + prm_ref[3:4, :])), 0.0)
        dfb_ref[...] += jnp.sum(dfr, axis=0, keepdims=True)
        out_ref[...] = ddt_ref[...] + dfr

    def rev(i):
        return (nb - 1 - i, 0)

    return pl.pallas_call(
        body, name="fox_gate_bwd", grid=(nb,),
        in_specs=[pl.BlockSpec((l, 128), rev), pl.BlockSpec((l, 128), rev), pl.BlockSpec((8, 128), lambda i: (0, 0)),
                  pl.BlockSpec((l, 128), rev)],
        out_specs=[pl.BlockSpec((l, 128), rev), pl.BlockSpec((1, 128), lambda i: (0, 0))],
        out_shape=[jax.ShapeDtypeStruct((s, 128), f32), jax.ShapeDtypeStruct((1, 128), f32)],
        scratch_shapes=[pltpu.VMEM((1, 128), f32)],
        compiler_params=_cp("arbitrary"))(dc, dtf, prm, ddt_raw)


def _all_gather(xl, name):
    r, c = xl.shape

    def body(x_ref, out_ref, send_sems, recv_sems, local_sem):
        x, y, cc = _position()
        me, sibling = (x, y, cc), (x, y, 1 - cc)
        chips = [(1 - x, y), (x, 1 - y), (1 - x, 1 - y)]

        def slot(px, py, pc):
            return out_ref.at[4 * px + 2 * py + pc]

        def copy(k, block, to, src=None):
            return pltpu.make_async_remote_copy(
                src_ref=slot(*block) if src is None else src, dst_ref=slot(*block),
                send_sem=send_sems.at[k], recv_sem=recv_sems.at[k],
                device_id=to, device_id_type=pl.DeviceIdType.MESH)

        mine = pltpu.make_async_copy(x_ref, slot(*me), local_sem)
        mine.start()
        first = [copy(0, me, sibling, src=x_ref)]
        first += [copy(1 + j, me, (*chip, cc), src=x_ref) for j, chip in enumerate(chips)]
        for cp in first:
            cp.start()
        passed = [copy(4 + j, (*chip, cc), sibling) for j, chip in enumerate(chips)]
        for j, chip in enumerate(chips):
            copy(1 + j, (*chip, cc), me).wait_recv()
            passed[j].start()
        copy(0, sibling, me).wait_recv()
        for j, chip in enumerate(chips):
            copy(4 + j, (*chip, 1 - cc), me).wait_recv()
        for cp in first + passed:
            cp.wait_send()
        mine.wait()

    return pl.pallas_call(
        body, name=name,
        out_shape=jax.ShapeDtypeStruct((N_DEV, r, c), xl.dtype),
        in_specs=[pl.BlockSpec(memory_space=pl.ANY)], out_specs=pl.BlockSpec(memory_space=pl.ANY),
        scratch_shapes=[pltpu.SemaphoreType.DMA((7,)), pltpu.SemaphoreType.DMA((7,)), pltpu.SemaphoreType.DMA],
    )(xl)


def _sum_parts(parts, name):
    n, r, c = parts.shape

    def body(p_ref, o_ref):
        g = p_ref[0]
        for i in range(1, n):
            g = g + p_ref[i]
        o_ref[...] = g

    return pl.pallas_call(body, name=name, out_shape=jax.ShapeDtypeStruct((r, c), f32))(parts)


def _adamw(w, m, v, parts, name, tr=128, by_columns=False):
    r, c = w.shape
    n = parts.shape[0]
    tr = min(tr, r)
    c1 = 1.0 - ADAM_B1 ** ADAM_STEP
    c2 = 1.0 - ADAM_B2 ** ADAM_STEP

    def body(w_ref, m_ref, v_ref, p_ref, g_ref, d_ref, nm_ref, nv_ref):
        g = p_ref[0].astype(f32)
        for i in range(1, n):
            g = g + p_ref[i].astype(f32)
        g_ref[...] = g
        nm = ADAM_B1 * m_ref[...] + (1.0 - ADAM_B1) * g
        nv = ADAM_B2 * v_ref[...] + (1.0 - ADAM_B2) * (g * g)
        nm_ref[...] = nm
        nv_ref[...] = nv
        d_ref[...] = -ADAM_LR * ((nm / c1) / (jnp.sqrt(nv / c2) + ADAM_EPS) + ADAM_WD * w_ref[...])

    if by_columns:
        blk = pl.BlockSpec((r, 128), lambda i: (0, i))
        pblk = pl.BlockSpec((n, r, 128), lambda i: (0, 0, i))
        steps = c // 128
    else:
        blk = pl.BlockSpec((tr, c), lambda i: (i, 0))
        pblk = pl.BlockSpec((n, tr, c), lambda i: (0, i, 0))
        steps = r // tr
    return pl.pallas_call(
        body, name=name, grid=(steps,),
        in_specs=[blk, blk, blk, pblk],
        out_specs=[blk] * 4, out_shape=[jax.ShapeDtypeStruct((r, c), f32)] * 4,
        compiler_params=_cp("parallel"))(w, m, v, parts)


def _lanes(w):
    return -(-w // 128) * 128


def _pack(arrs):
    rows = []
    for a in arrs:
        k, w = a.shape
        if w % 128:
            a = jnp.pad(a, ((0, 0), (0, _lanes(w) - w)))
        rows.append(a.reshape(-1, 128))
    out = jnp.concatenate(rows, axis=0)
    pad = -out.shape[0] % 8
    return jnp.pad(out, ((0, pad), (0, 0))) if pad else out


def _unpack(packed, shapes):
    outs, off = [], 0
    lead = packed.shape[:-2]
    for k, w in shapes:
        nrow = k * _lanes(w) // 128
        a = packed[..., off:off + nrow, :].reshape(*lead, k, _lanes(w))[..., :w]
        outs.append(a)
        off += nrow
    return outs


def _gathered_cols(a):
    n, k, wl = a.shape
    return jnp.transpose(a, (1, 0, 2)).reshape(k, n * wl)


def _col_shards(a):
    k, w = a.shape
    return jnp.transpose(a.reshape(k, N_DEV, w // N_DEV), (1, 0, 2))


SMALL_PARAMS = (
    ("e_norm_pre", 1, 1024, False), ("e_conv_w", 4, 2048, True), ("e_conv_b", 1, 2048, False),
    ("e_dt_bias", 1, 16, False), ("e_a_log", 1, 16, False), ("e_d_skip", 1, 16, False), ("e_fgate_b", 1, 16, False),
    ("e_ssd_norm", 1, 1024, False), ("e_norm_post", 1, 1024, False), ("o_norm_pre", 1, 1024, True),
    ("o_conv_w", 31, 2048, True), ("o_conv_b", 1, 2048, True), ("o_ln_g", 1, 2048, True), ("o_ln_b", 1, 2048, True),
    ("o_norm_post", 1, 1024, True),
)
BIG_PARAMS = ("e_w_in", "e_w_out", "o_w_in", "o_w_out")
WEIGHT_ORDER = ("e_norm_pre", "e_w_in", "e_conv_w", "e_conv_b", "e_dt_bias", "e_a_log", "e_d_skip", "e_fgate_b",
                "e_ssd_norm", "e_w_out", "e_norm_post", "o_norm_pre", "o_w_in", "o_conv_w", "o_conv_b", "o_ln_g",
                "o_ln_b", "o_w_out", "o_norm_post")
E_IN = 7200
O_IN = 6144


def kernel(x, e_norm_pre, e_w_in, e_conv_w, e_conv_b, e_dt_bias, e_a_log, e_d_skip, e_fgate_b, e_ssd_norm, e_w_out, e_norm_post, o_norm_pre, o_w_in, o_conv_w, o_conv_b, o_ln_g, o_ln_b, o_w_out, o_norm_post, loss_target, m_e_norm_pre, m_e_w_in, m_e_conv_w, m_e_conv_b, m_e_dt_bias, m_e_a_log, m_e_d_skip, m_e_fgate_b, m_e_ssd_norm, m_e_w_out, m_e_norm_post, m_o_norm_pre, m_o_w_in, m_o_conv_w, m_o_conv_b, m_o_ln_g, m_o_ln_b, m_o_w_out, m_o_norm_post, v_e_norm_pre, v_e_w_in, v_e_conv_w, v_e_conv_b, v_e_dt_bias, v_e_a_log, v_e_d_skip, v_e_fgate_b, v_e_ssd_norm, v_e_w_out, v_e_norm_post, v_o_norm_pre, v_o_w_in, v_o_conv_w, v_o_conv_b, v_o_ln_g, v_o_ln_b, v_o_w_out, v_o_norm_post):
    given = dict(locals())
    w_in = {n: given[n] for n in WEIGHT_ORDER}
    m_in = {n: given["m_" + n] for n in WEIGHT_ORDER}
    v_in = {n: given["v_" + n] for n in WEIGHT_ORDER}

    def mat(a):
        return a.reshape(a.shape[-2:])

    xs = mat(x)
    tgt = mat(loss_target)
    xi, yi, ci = _position()
    me = 4 * xi + 2 * yi + ci
    ow = O_IN // N_DEV
    wr = D_CONV // N_DEV

    ew = E_IN // N_DEV
    w_t = _all_gather(jnp.transpose(mat(e_w_in)).astype(bf16), "gather_weights").reshape(E_IN, D_MODEL)
    later_weights = [mat(e_w_out).astype(bf16), mat(o_w_in).astype(bf16), mat(o_w_out).astype(bf16)]
    w_z, w_xbc, w_qkv = w_t[0:2048], w_t[2048:4096], w_t[4112:7184]
    w_dtf = jnp.concatenate([w_t[4096:4112], w_t[7184:7200], jnp.zeros((96, D_MODEL), bf16)], axis=0)

    sharded_small = [(n, k, w) for n, k, w, sh in SMALL_PARAMS if sh]
    sg = _all_gather(_pack([mat(w_in[n]) for n, _, _ in sharded_small]), "gather_small_weights")
    full_small = {n: _gathered_cols(a)
                  for (n, _, _), a in zip(sharded_small, _unpack(sg, [(k, w // N_DEV) for _, k, w in sharded_small]))}
    for n, _, _, sh in SMALL_PARAMS:
        if not sh:
            full_small[n] = mat(w_in[n])
    p = full_small
    prm = jnp.zeros((8, 128), f32)
    prm = prm.at[0, 0:16].set(p["e_dt_bias"][0]).at[1, 0:16].set(p["e_a_log"][0]).at[2, 0:16].set(p["e_d_skip"][0])
    prm = prm.at[3, F_LANE:F_LANE + 16].set(p["e_fgate_b"][0])

    u0 = _rms_fwd(xs, p["e_norm_pre"], "rms_pre0")
    z0 = _mm_nt([(u0, 0, w_z, 0, D_MODEL)], bf16, "proj0_z", tm=1024, tn=1024)
    xraw = _mm_nt([(u0, 0, w_xbc, 0, D_MODEL)], bf16, "proj0_xbc", tm=1024, tn=1024)
    qkv = _mm_nt([(u0, 0, w_qkv, 0, D_MODEL)], bf16, "proj0_qkv", tm=1024, tn=1024)
    dtf = _mm_nt([(u0, 0, w_dtf, 0, D_MODEL)], f32, "proj0_dtf", tm=1024, tn=128)
    pre, act = _conv_ssd_fwd(xraw, p["e_conv_w"], p["e_conv_b"])
    y, hs = _ssd_fwd(act, dtf, prm)
    qa, ka = _fox_prep(qkv, _fox_cumsum(dtf, prm))
    o, lse, (e_w_out_g, o_w_in_g, o_w_out_g) = _fox_fwd(qa, ka, qkv, later_weights)
    e_w_out_f = e_w_out_g.reshape(D_CONV, D_MODEL)
    o_w_in_f = _gathered_cols(o_w_in_g)
    o_w_out_f = o_w_out_g.reshape(D_CONV, D_MODEL)
    cat = _gate0_fwd(y, z0, o, p["e_ssd_norm"])
    out0 = _mm_nn(cat, e_w_out_f, f32, "out0")
    x1, u1 = _post0_pre1(xs, out0, p["e_norm_post"], p["o_norm_pre"])

    proj1 = _mm_nn(u1, o_w_in_f, bf16, "proj1")
    hc = _conv_glu_fwd(proj1, p["o_conv_w"], p["o_conv_b"])
    h3 = _ln_gate_fwd(hc, proj1, p["o_ln_g"], p["o_ln_b"])
    out1 = _mm_nn(h3, o_w_out_f, f32, "out1")
    dy, d_out1, dg_post1, loss_part = _final_loss(x1, out1, tgt, p["o_norm_post"])

    dh3 = _mm_nt([(d_out1, 0, o_w_out_f, 0, D_MODEL)], bf16, "dh3", tm=1024, tn=D_CONV)
    g_o_w_out = _mm_tn(h3, d_out1, "dw_out1")
    dhc, dz1, dg_ln, db_ln = _ln_gate_bwd(hc, proj1, dh3, p["o_ln_g"], p["o_ln_b"])
    dval, dgate, dw_conv1, db_conv1 = _conv_glu_bwd(dhc, proj1, p["o_conv_w"])
    dproj1 = jnp.concatenate([dval, dgate, dz1], axis=1)
    du1 = _mm_nt([(dproj1, 0, o_w_in_f, 0, O_IN)], f32, "du1")
    g_o_w_in = _mm_tn(u1, dproj1, "dw_in1", tn=ow, blocked=True)
    dx1, d_out0, dg_pre1, dg_post0 = _mid_bwd(x1, du1, dy, out0, p["o_norm_pre"], p["e_norm_post"])

    dcat = _mm_nt([(d_out0, 0, e_w_out_f, 0, D_MODEL)], bf16, "dcat", tm=1024, tn=D_CONV)
    g_e_w_out = _mm_tn(cat, d_out0, "dw_out0")
    dy_ssd, do, dz0, delta, dg_ssd_norm = _gate0_bwd(y, z0, o, dcat, p["e_ssd_norm"])
    early = [g_e_w_out.reshape(N_DEV, wr, D_MODEL).astype(bf16), g_o_w_in.astype(bf16),
             g_o_w_out.reshape(N_DEV, wr, D_MODEL).astype(bf16)]
    dq_hm, dk_hm, dv, early_parts = _fox_bwd(qa, ka, qkv, do, lse, delta[0:N_HEADS].reshape(N_HEADS, 1, -1), early)
    dq, dk, dc = _fox_bwd_post(dq_hm, dk_hm)
    dpre, ddt_raw, dprm = _ssd_bwd(act, pre, dtf, prm, hs, dy_ssd)
    ddtf, dfb = _fox_gate_bwd(dc, dtf, prm, ddt_raw)
    dxraw, dw_conv0, db_conv0 = _conv_ssd_bwd(dpre, xraw, p["e_conv_w"])
    gw_dtf = _mm_tn(ddtf, u0, "dw_in0_dtf")
    g_e_w_in_t = jnp.concatenate([
        _mm_tn(dz0, u0, "dw_in0_z"), _mm_tn(dxraw, u0, "dw_in0_xbc"), gw_dtf[0:16],
        _mm_tn(dq, u0, "dw_in0_q"), _mm_tn(dk, u0, "dw_in0_k"), _mm_tn(dv, u0, "dw_in0_v"), gw_dtf[16:32]], axis=0)
    du0, last_parts = _mm_nt(
        [(dz0, 0, w_z, 0, 2048), (dxraw, 0, w_xbc, 0, 2048), (dq, 0, w_qkv, 0, 1024), (dk, 0, w_qkv, 1, 1024),
         (dv, 0, w_qkv, 2, 1024), (ddtf, 0, w_dtf, 0, 128)], f32, "du0", b_kn=True,
        gs=[g_e_w_in_t.astype(bf16).reshape(N_DEV, ew, D_MODEL)])
    grad_x, dg_pre0 = _first_bwd(xs, du0, dx1, p["e_norm_pre"])

    outs = {"e_w_in": tuple(jnp.transpose(r) for r in _adamw(
        jnp.transpose(mat(e_w_in)), jnp.transpose(mat(m_e_w_in)), jnp.transpose(mat(v_e_w_in)), last_parts[0],
        "adamw_e_w_in", by_columns=True))}
    for n, parts in zip(BIG_PARAMS[1:], early_parts):
        outs[n] = _adamw(mat(w_in[n]), mat(m_in[n]), mat(v_in[n]), parts, "adamw_" + n)

    small_grads = {
        "e_norm_pre": dg_pre0, "e_conv_w": dw_conv0, "e_conv_b": db_conv0, "e_dt_bias": dprm[0:1, 0:16],
        "e_a_log": dprm[1:2, 0:16], "e_d_skip": dprm[2:3, 0:16], "e_fgate_b": dfb[:, F_LANE:F_LANE + 16],
        "e_ssd_norm": dg_ssd_norm, "e_norm_post": dg_post0, "o_norm_pre": dg_pre1, "o_conv_w": dw_conv1,
        "o_conv_b": db_conv1, "o_ln_g": dg_ln, "o_ln_b": db_ln, "o_norm_post": dg_post1,
    }
    gathered = _all_gather(_pack([small_grads[n] for n, _, _, _ in SMALL_PARAMS] + [loss_part]), "gather_small_grads")
    summed = _unpack(_sum_parts(gathered, "sum_small_grads"), [(k, w) for _, k, w, _ in SMALL_PARAMS] + [(1, 128)])
    loss = summed[-1][0, 0]
    g_local = []
    for (n, k, w, sh), g in zip(SMALL_PARAMS, summed):
        g_local.append(lax.dynamic_slice_in_dim(g, me * (w // N_DEV), w // N_DEV, axis=1) if sh else g)
    names = [n for n, _, _, _ in SMALL_PARAMS]
    local_shapes = [(k, w // N_DEV if sh else w) for _, k, w, sh in SMALL_PARAMS]
    res = _adamw(_pack([mat(w_in[n]) for n in names]), _pack([mat(m_in[n]) for n in names]),
                 _pack([mat(v_in[n]) for n in names]), _pack(g_local)[None], "adamw_small", tr=8)
    unpacked = [_unpack(r, local_shapes) for r in res]
    for i, n in enumerate(names):
        outs[n] = tuple(u[i] for u in unpacked)

    ret = [loss, grad_x.reshape(x.shape)]
    for j in range(4):
        ret += [outs[n][j].reshape(w_in[n].shape) for n in WEIGHT_ORDER]
    return tuple(ret)
```

```python
import jax
import jax.numpy as jnp
from jax import lax
from jax.experimental import pallas as pl
from jax.experimental.pallas import tpu as pltpu

f32 = jnp.float32
bf16 = jnp.bfloat16

N_DEV = 8
D_MODEL = 1024
N_HEADS = 16
HEAD_DIM = 64
N_GROUPS = 4
HEADS_PER_GROUP = 4
D_STATE = 128
CHUNK = 512
SSD_CONV = 4
CONV_WIDTH = 31
D_CONV = 2048
EPS = 1e-6
XBC_W = 2048
B_OFF = 1024
C_OFF = 1536
F_LANE = 16
HALO = 32

ADAM_LR = 0.001
ADAM_B1 = 0.9
ADAM_B2 = 0.999
ADAM_EPS = 1e-08
ADAM_WD = 0.01
ADAM_STEP = 10

VMEM_LIMIT_BYTES = 56 * 1024 * 1024
ROW_TILE = 512
CONV_ROW_TILE = 512
CONV_COL_TILE = 512
CONV_SUB = 32
ATTN_TILE = 1024
ATTN_FWD_TILE = 1024

NT = (((1,), (1,)), ((), ()))
TN = (((0,), (0,)), ((), ()))
HIGHEST = lax.Precision.HIGHEST
NEG = -1e30


def _cp(*sem):
    return pltpu.CompilerParams(dimension_semantics=sem if sem else None, vmem_limit_bytes=VMEM_LIMIT_BYTES)


def _sigmoid(x):
    return jax.nn.sigmoid(x)


def _silu(x):
    return x * _sigmoid(x)


def _dsilu(x):
    s = _sigmoid(x)
    return s * (1.0 + x * (1.0 - s))


def _softplus(x):
    return jnp.maximum(x, 0.0) + jnp.log(1.0 + jnp.exp(-jnp.abs(x)))


def _log_sigmoid(x):
    return jnp.minimum(x, 0.0) - jnp.log(1.0 + jnp.exp(-jnp.abs(x)))


def _dot(a, b, dims=None, precision=None):
    if dims is None:
        return jnp.dot(a, b, preferred_element_type=f32, precision=precision)
    return lax.dot_general(a, b, dims, preferred_element_type=f32, precision=precision)


def _mm_nn(a, b, out_dtype, name, tm=1024, tn=1024):
    m, k = a.shape
    n = b.shape[1]
    tm, tn = min(tm, m), min(tn, n)

    def body(a_ref, b_ref, o_ref):
        o_ref[...] = _dot(a_ref[...], b_ref[...]).astype(o_ref.dtype)

    return pl.pallas_call(
        body, name=name, grid=(n // tn, m // tm),
        in_specs=[pl.BlockSpec((tm, k), lambda j, i: (i, 0)), pl.BlockSpec((k, tn), lambda j, i: (0, j))],
        out_specs=pl.BlockSpec((tm, tn), lambda j, i: (i, j)),
        out_shape=jax.ShapeDtypeStruct((m, n), out_dtype), compiler_params=_cp("parallel", "parallel"))(a, b)


def _mm_nt(pairs, out_dtype, name, tm=512, tn=512, gs=(), b_kn=False):
    m = pairs[0][0].shape[0]
    n = pairs[0][2].shape[1] if b_kn else pairs[0][2].shape[0]
    tm, tn = min(tm, m), min(tn, n)
    npair = len(pairs)
    ng = len(gs)
    grid = (n // tn, m // tm)

    def body(*refs):
        g_refs = refs[2 * npair:2 * npair + ng]
        o_ref = refs[2 * npair + ng]
        r_refs = refs[2 * npair + ng + 1:2 * npair + 2 * ng + 1]
        sems = refs[2 * npair + 2 * ng + 1:]
        if ng:
            copies = _exchange_copies(g_refs, r_refs, *sems)

            @pl.when((pl.program_id(0) == 0) & (pl.program_id(1) == 0))
            def _():
                _exchange_start(copies)
        acc = None
        for p in range(npair):
            d = _dot(refs[2 * p][...].astype(bf16), refs[2 * p + 1][...], None if b_kn else NT)
            acc = d if acc is None else acc + d
        o_ref[...] = acc.astype(o_ref.dtype)
        if ng:
            @pl.when((pl.program_id(0) == grid[0] - 1) & (pl.program_id(1) == grid[1] - 1))
            def _():
                _exchange_wait(copies)

    in_specs, args = [], []
    for a, acb, b, bcb, k in pairs:
        in_specs.append(pl.BlockSpec((tm, k), lambda j, i, acb=acb: (i, acb)))
        if b_kn:
            in_specs.append(pl.BlockSpec((k, tn), lambda j, i, bcb=bcb: (bcb, j)))
        else:
            in_specs.append(pl.BlockSpec((tn, k), lambda j, i, bcb=bcb: (j, bcb)))
        args += [a, b]
    anyspec = pl.BlockSpec(memory_space=pl.ANY)
    outs = pl.pallas_call(
        body, name=name, grid=grid, in_specs=in_specs + [anyspec] * ng,
        out_specs=[pl.BlockSpec((tm, tn), lambda j, i: (i, j))] + [anyspec] * ng,
        out_shape=[jax.ShapeDtypeStruct((m, n), out_dtype)] + [jax.ShapeDtypeStruct(g.shape, g.dtype) for g in gs],
        scratch_shapes=_exchange_sems(ng) if ng else [],
        compiler_params=_cp("arbitrary", "arbitrary") if ng else _cp("parallel", "parallel"))(*args, *gs)
    return (outs[0], outs[1:]) if ng else outs[0]


def _mm_tn(a, b, name, a_cb=0, am=None, b_cb=0, bn=None, tn=1024, tk=1024, blocked=False):
    k = a.shape[0]
    am = a.shape[1] if am is None else am
    bn = b.shape[1] if bn is None else bn
    tm = min(1024, am)
    tn, tk = min(tn, bn), min(tk, k)
    a_off, b_off = a_cb * (am // tm), b_cb * (bn // tn)

    def body(a_ref, b_ref, o_ref):
        @pl.when(pl.program_id(2) == 0)
        def _():
            o_ref[...] = jnp.zeros_like(o_ref)
        d = _dot(a_ref[...].astype(bf16), b_ref[...].astype(bf16), TN)
        o_ref[...] += d.reshape(o_ref.shape)

    if blocked:
        out_spec = pl.BlockSpec((1, tm, tn), lambda i, j, kk: (j, i, 0))
        out_shape = jax.ShapeDtypeStruct((bn // tn, am, tn), f32)
    else:
        out_spec = pl.BlockSpec((tm, tn), lambda i, j, kk: (i, j))
        out_shape = jax.ShapeDtypeStruct((am, bn), f32)
    return pl.pallas_call(
        body, name=name, grid=(am // tm, bn // tn, k // tk),
        in_specs=[pl.BlockSpec((tk, tm), lambda i, j, kk: (kk, a_off + i)),
                  pl.BlockSpec((tk, tn), lambda i, j, kk: (kk, b_off + j))],
        out_specs=out_spec, out_shape=out_shape,
        compiler_params=_cp("parallel", "parallel", "arbitrary"))(a, b)


def _rowspec(ts, w, cb=0):
    return pl.BlockSpec((ts, w), lambda i: (i, cb))


def _vecspec(w):
    return pl.BlockSpec((1, w), lambda i: (0, 0))


def _rms_fwd(x, g, name):
    s, d = x.shape
    ts = min(ROW_TILE, s)

    def body(x_ref, g_ref, u_ref):
        xv = x_ref[...]
        r = lax.rsqrt(jnp.mean(xv * xv, axis=-1, keepdims=True) + EPS)
        u_ref[...] = (xv * r * g_ref[...]).astype(bf16)

    return pl.pallas_call(
        body, name=name, grid=(s // ts,), in_specs=[_rowspec(ts, d), _vecspec(d)], out_specs=_rowspec(ts, d),
        out_shape=jax.ShapeDtypeStruct((s, d), bf16), compiler_params=_cp("parallel"))(x, g)


def _rms_bwd_vals(xv, g, dy):
    r = lax.rsqrt(jnp.mean(xv * xv, axis=-1, keepdims=True) + EPS)
    xh = xv * r
    dg = jnp.sum(dy * xh, axis=0, keepdims=True)
    dxh = dy * g
    dx = r * (dxh - xh * jnp.mean(dxh * xh, axis=-1, keepdims=True))
    return dx, dg


def _gate0_fwd(y, z, o, ssd_norm):
    s = y.shape[0]
    ts = min(ROW_TILE, s)
    gw = D_MODEL // N_GROUPS

    def body(y_ref, zs_ref, zf_ref, o_ref, w_ref, cat_ref):
        yg = y_ref[...].astype(f32) * _silu(zs_ref[...].astype(f32))
        for g in range(N_GROUPS):
            seg = yg[:, gw * g:gw * (g + 1)]
            r = lax.rsqrt(jnp.mean(seg * seg, axis=-1, keepdims=True) + EPS)
            cat_ref[:, gw * g:gw * (g + 1)] = (seg * r * w_ref[:, gw * g:gw * (g + 1)]).astype(bf16)
        cat_ref[:, D_MODEL:] = (o_ref[...].astype(f32) * _silu(zf_ref[...].astype(f32))).astype(bf16)

    return pl.pallas_call(
        body, name="gate0_fwd", grid=(s // ts,),
        in_specs=[_rowspec(ts, D_MODEL), _rowspec(ts, D_MODEL, 0), _rowspec(ts, D_MODEL, 1), _rowspec(ts, D_MODEL),
                  _vecspec(D_MODEL)],
        out_specs=_rowspec(ts, 2 * D_MODEL),
        out_shape=jax.ShapeDtypeStruct((s, 2 * D_MODEL), bf16), compiler_params=_cp("parallel"))(y, z, z, o, ssd_norm)


def _post0_pre1(x, out0, g_post0, g_pre1):
    s, d = x.shape
    ts = min(ROW_TILE, s)

    def body(x_ref, o_ref, gp_ref, gn_ref, x1_ref, u1_ref):
        ov = o_ref[...]
        r = lax.rsqrt(jnp.mean(ov * ov, axis=-1, keepdims=True) + EPS)
        x1 = x_ref[...] + ov * r * gp_ref[...]
        x1_ref[...] = x1
        r1 = lax.rsqrt(jnp.mean(x1 * x1, axis=-1, keepdims=True) + EPS)
        u1_ref[...] = (x1 * r1 * gn_ref[...]).astype(bf16)

    return pl.pallas_call(
        body, name="post0_pre1", grid=(s // ts,),
        in_specs=[_rowspec(ts, d), _rowspec(ts, d), _vecspec(d), _vecspec(d)],
        out_specs=[_rowspec(ts, d), _rowspec(ts, d)],
        out_shape=[jax.ShapeDtypeStruct((s, d), f32), jax.ShapeDtypeStruct((s, d), bf16)],
        compiler_params=_cp("parallel"))(x, out0, g_post0, g_pre1)


def _ln_vals(hc, g, b):
    mu = jnp.mean(hc, axis=-1, keepdims=True)
    xc = hc - mu
    rstd = lax.rsqrt(jnp.mean(xc * xc, axis=-1, keepdims=True) + EPS)
    xh = xc * rstd
    return xh, rstd, xh * g + b


def _ln_gate_fwd(hc, proj1, ln_g, ln_b):
    s = hc.shape[0]
    ts = min(ROW_TILE, s)

    def body(hc_ref, z_ref, g_ref, b_ref, h3_ref):
        _, _, ln = _ln_vals(hc_ref[...].astype(f32), g_ref[...], b_ref[...])
        h3_ref[...] = (_silu(ln) * _silu(z_ref[...].astype(f32))).astype(bf16)

    return pl.pallas_call(
        body, name="ln_gate_fwd", grid=(s // ts,),
        in_specs=[_rowspec(ts, D_CONV), _rowspec(ts, D_CONV, 2), _vecspec(D_CONV), _vecspec(D_CONV)],
        out_specs=_rowspec(ts, D_CONV),
        out_shape=jax.ShapeDtypeStruct((s, D_CONV), bf16), compiler_params=_cp("parallel"))(hc, proj1, ln_g, ln_b)


def _final_loss(x1, out1, tgt, g_post1):
    s, d = x1.shape
    ts = min(ROW_TILE, s)

    def body(x1_ref, o_ref, t_ref, g_ref, dy_ref, do_ref, dg_ref, loss_ref):
        i = pl.program_id(0)

        @pl.when(i == 0)
        def _():
            dg_ref[...] = jnp.zeros_like(dg_ref)
            loss_ref[...] = jnp.zeros_like(loss_ref)
        ov = o_ref[...]
        g = g_ref[...]
        r = lax.rsqrt(jnp.mean(ov * ov, axis=-1, keepdims=True) + EPS)
        diff = x1_ref[...] + ov * r * g - t_ref[...]
        row = jnp.mean(diff * diff, axis=-1, keepdims=True)
        loss_ref[...] += jnp.broadcast_to(0.5 * jnp.sum(row, axis=0, keepdims=True), loss_ref.shape)
        dy = diff * (1.0 / d)
        dy_ref[...] = dy
        dx, dg = _rms_bwd_vals(ov, g, dy)
        do_ref[...] = dx.astype(bf16)
        dg_ref[...] += dg

    return pl.pallas_call(
        body, name="final_loss", grid=(s // ts,),
        in_specs=[_rowspec(ts, d), _rowspec(ts, d), _rowspec(ts, d), _vecspec(d)],
        out_specs=[_rowspec(ts, d), _rowspec(ts, d), _vecspec(d), _vecspec(128)],
        out_shape=[jax.ShapeDtypeStruct((s, d), f32), jax.ShapeDtypeStruct((s, d), bf16),
                   jax.ShapeDtypeStruct((1, d), f32), jax.ShapeDtypeStruct((1, 128), f32)],
        compiler_params=_cp("arbitrary"))(x1, out1, tgt, g_post1)


def _ln_gate_bwd(hc, proj1, dh3, ln_g, ln_b):
    s = hc.shape[0]
    ts = min(ROW_TILE, s)

    def body(hc_ref, z_ref, dh_ref, g_ref, b_ref, dhc_ref, dz_ref, dg_ref, db_ref):
        @pl.when(pl.program_id(0) == 0)
        def _():
            dg_ref[...] = jnp.zeros_like(dg_ref)
            db_ref[...] = jnp.zeros_like(db_ref)
        g = g_ref[...]
        xh, rstd, ln = _ln_vals(hc_ref[...].astype(f32), g, b_ref[...])
        zv = z_ref[...].astype(f32)
        dh3 = dh_ref[...].astype(f32)
        dz_ref[...] = (dh3 * _silu(ln) * _dsilu(zv)).astype(bf16)
        dln = dh3 * _silu(zv) * _dsilu(ln)
        dg_ref[...] += jnp.sum(dln * xh, axis=0, keepdims=True)
        db_ref[...] += jnp.sum(dln, axis=0, keepdims=True)
        dxh = dln * g
        dhc = rstd * (dxh - jnp.mean(dxh, axis=-1, keepdims=True) - xh * jnp.mean(dxh * xh, axis=-1, keepdims=True))
        dhc_ref[...] = dhc.astype(bf16)

    return pl.pallas_call(
        body, name="ln_gate_bwd", grid=(s // ts,),
        in_specs=[_rowspec(ts, D_CONV), _rowspec(ts, D_CONV, 2), _rowspec(ts, D_CONV), _vecspec(D_CONV),
                  _vecspec(D_CONV)],
        out_specs=[_rowspec(ts, D_CONV), _rowspec(ts, D_CONV), _vecspec(D_CONV), _vecspec(D_CONV)],
        out_shape=[jax.ShapeDtypeStruct((s, D_CONV), bf16), jax.ShapeDtypeStruct((s, D_CONV), bf16),
                   jax.ShapeDtypeStruct((1, D_CONV), f32), jax.ShapeDtypeStruct((1, D_CONV), f32)],
        compiler_params=_cp("arbitrary"))(hc, proj1, dh3, ln_g, ln_b)


def _mid_bwd(x1, du1, dy, out0, g_pre1, g_post0):
    s, d = x1.shape
    ts = min(ROW_TILE, s)

    def body(x1_ref, du_ref, dy_ref, o_ref, gn_ref, gp_ref, dx1_ref, do_ref, dgn_ref, dgp_ref):
        @pl.when(pl.program_id(0) == 0)
        def _():
            dgn_ref[...] = jnp.zeros_like(dgn_ref)
            dgp_ref[...] = jnp.zeros_like(dgp_ref)
        dxa, dgn = _rms_bwd_vals(x1_ref[...], gn_ref[...], du_ref[...])
        dx1 = dy_ref[...] + dxa
        dx1_ref[...] = dx1
        dgn_ref[...] += dgn
        dxo, dgp = _rms_bwd_vals(o_ref[...], gp_ref[...], dx1)
        do_ref[...] = dxo.astype(bf16)
        dgp_ref[...] += dgp

    return pl.pallas_call(
        body, name="mid_bwd", grid=(s // ts,),
        in_specs=[_rowspec(ts, d)] * 4 + [_vecspec(d), _vecspec(d)],
        out_specs=[_rowspec(ts, d), _rowspec(ts, d), _vecspec(d), _vecspec(d)],
        out_shape=[jax.ShapeDtypeStruct((s, d), f32), jax.ShapeDtypeStruct((s, d), bf16),
                   jax.ShapeDtypeStruct((1, d), f32), jax.ShapeDtypeStruct((1, d), f32)],
        compiler_params=_cp("arbitrary"))(x1, du1, dy, out0, g_pre1, g_post0)


def _first_bwd(x, du0, dx1, g_pre0):
    s, d = x.shape
    ts = min(ROW_TILE, s)

    def body(x_ref, du_ref, dx1_ref, g_ref, dx_ref, dg_ref):
        @pl.when(pl.program_id(0) == 0)
        def _():
            dg_ref[...] = jnp.zeros_like(dg_ref)
        dxa, dg = _rms_bwd_vals(x_ref[...], g_ref[...], du_ref[...])
        dx_ref[...] = dx1_ref[...] + dxa
        dg_ref[...] += dg

    return pl.pallas_call(
        body, name="first_bwd", grid=(s // ts,),
        in_specs=[_rowspec(ts, d)] * 3 + [_vecspec(d)],
        out_specs=[_rowspec(ts, d), _vecspec(d)],
        out_shape=[jax.ShapeDtypeStruct((s, d), f32), jax.ShapeDtypeStruct((1, d), f32)],
        compiler_params=_cp("arbitrary"))(x, du0, dx1, g_pre0)


def _gate0_bwd(y, z, o, dcat, ssd_norm):
    s = y.shape[0]
    ts = min(ROW_TILE, s)
    gw = D_MODEL // N_GROUPS

    def body(y_ref, zs_ref, zf_ref, o_ref, dn_ref, dg_ref, w_ref, dy_ref, do_ref, dz_ref, delta_ref, dw_ref):
        @pl.when(pl.program_id(0) == 0)
        def _():
            dw_ref[...] = jnp.zeros_like(dw_ref)
        yv = y_ref[...].astype(f32)
        zs = zs_ref[...].astype(f32)
        sz = _silu(zs)
        yg = yv * sz
        dyn = dn_ref[...].astype(f32)
        for g in range(N_GROUPS):
            sl = slice(gw * g, gw * (g + 1))
            seg = yg[:, sl]
            r = lax.rsqrt(jnp.mean(seg * seg, axis=-1, keepdims=True) + EPS)
            yh = seg * r
            dn = dyn[:, sl]
            dw_ref[:, sl] += jnp.sum(dn * yh, axis=0, keepdims=True)
            dyh = dn * w_ref[:, sl]
            dyg = r * (dyh - yh * jnp.mean(dyh * yh, axis=-1, keepdims=True))
            dy_ref[:, sl] = (dyg * sz[:, sl]).astype(bf16)
            dz_ref[:, sl] = (dyg * yv[:, sl] * _dsilu(zs[:, sl])).astype(bf16)
        zf = zf_ref[...].astype(f32)
        ov = o_ref[...].astype(f32)
        dog = dg_ref[...].astype(f32)
        dov = (dog * _silu(zf)).astype(bf16)
        do_ref[...] = dov
        dz_ref[:, D_MODEL:] = (dog * ov * _dsilu(zf)).astype(bf16)
        prod = dov.astype(f32) * ov
        lane = lax.broadcasted_iota(jnp.int32, (ts, 128), 1)
        delta = jnp.zeros((ts, 128), f32)
        for h in range(N_HEADS):
            dh = jnp.sum(prod[:, HEAD_DIM * h:HEAD_DIM * (h + 1)], axis=-1, keepdims=True)
            delta = delta + jnp.where(lane == h, dh, 0.0)
        delta_ref[...] = delta.T

    return pl.pallas_call(
        body, name="gate0_bwd", grid=(s // ts,),
        in_specs=[_rowspec(ts, D_MODEL), _rowspec(ts, D_MODEL, 0), _rowspec(ts, D_MODEL, 1), _rowspec(ts, D_MODEL),
                  _rowspec(ts, D_MODEL, 0), _rowspec(ts, D_MODEL, 1), _vecspec(D_MODEL)],
        out_specs=[_rowspec(ts, D_MODEL), _rowspec(ts, D_MODEL), _rowspec(ts, 2 * D_MODEL),
                   pl.BlockSpec((128, ts), lambda i: (0, i)), _vecspec(D_MODEL)],
        out_shape=[jax.ShapeDtypeStruct((s, D_MODEL), bf16), jax.ShapeDtypeStruct((s, D_MODEL), bf16),
                   jax.ShapeDtypeStruct((s, 2 * D_MODEL), bf16), jax.ShapeDtypeStruct((128, s), f32),
                   jax.ShapeDtypeStruct((1, D_MODEL), f32)],
        compiler_params=_cp("arbitrary"))(y, z, z, o, dcat, dcat, ssd_norm)


def _conv_grid(s, c):
    ts, cb = min(CONV_ROW_TILE, s), min(CONV_COL_TILE, c)
    return ts, cb, (c // cb, s // ts)


def _cur(ts, cb, off=0):
    return pl.BlockSpec((ts, cb), lambda c, i: (i, c + off))


def _prev_halo(ts, cb, off=0):
    return pl.BlockSpec((HALO, cb), lambda c, i: (jnp.maximum(i * (ts // HALO) - 1, 0), c + off))


def _next_halo(ts, cb, s, off=0):
    return pl.BlockSpec((HALO, cb), lambda c, i: (jnp.minimum((i + 1) * (ts // HALO), s // HALO - 1), c + off))


def _wspec(k, cb):
    return pl.BlockSpec((k, cb), lambda c, i: (0, c))


def _phases(offsets):
    return sorted({o % 8 for o in offsets} - {0})


def _shift_scratch(offsets, ts, cb):
    return pltpu.VMEM((max(len(_phases(offsets)), 1), ts + HALO - 8, cb), f32)


def _fill_phases(ext_ref, sh_ref, offsets, ts):
    for j, r in enumerate(_phases(offsets)):
        sh_ref[j] = ext_ref[pl.ds(r, ts + HALO - 8), :]


def _slab(ext_ref, sh_ref, offsets, off, start):
    r = off % 8
    a = off - r + start
    if r == 0:
        return ext_ref[a:a + CONV_SUB, :]
    return sh_ref[_phases(offsets).index(r), a:a + CONV_SUB, :]


def _conv_taps(ext_ref, sh_ref, w_ref, b_ref, ts, k_taps, emit):
    offsets = [HALO - (k_taps - 1) + k for k in range(k_taps)]
    _fill_phases(ext_ref, sh_ref, offsets, ts)
    for sb in range(ts // CONV_SUB):
        acc = b_ref[...]
        for k in range(k_taps):
            acc = acc + w_ref[k:k + 1, :] * _slab(ext_ref, sh_ref, offsets, offsets[k], sb * CONV_SUB)
        emit(slice(sb * CONV_SUB, (sb + 1) * CONV_SUB), acc)


def _conv_ssd_fwd(xraw, w, b):
    s, c = xraw.shape
    ts, cb, grid = _conv_grid(s, c)
    offsets = [HALO - (SSD_CONV - 1) + k for k in range(SSD_CONV)]

    def body(x_ref, xh_ref, w_ref, b_ref, pre_ref, act_ref, ext_ref, sh_ref):
        first = pl.program_id(1) == 0
        ext_ref[0:HALO, :] = jnp.where(first, 0.0, xh_ref[...].astype(f32))
        ext_ref[HALO:, :] = x_ref[...].astype(f32)

        def emit(rows, pre):
            pre_ref[rows, :] = pre.astype(bf16)
            act_ref[rows, :] = _silu(pre).astype(bf16)
        _conv_taps(ext_ref, sh_ref, w_ref, b_ref, ts, SSD_CONV, emit)

    return pl.pallas_call(
        body, name="conv_ssd_fwd", grid=grid,
        in_specs=[_cur(ts, cb), _prev_halo(ts, cb), _wspec(SSD_CONV, cb), _wspec(1, cb)],
        out_specs=[_cur(ts, cb), _cur(ts, cb)],
        out_shape=[jax.ShapeDtypeStruct((s, c), bf16)] * 2,
        scratch_shapes=[pltpu.VMEM((HALO + ts, cb), f32), _shift_scratch(offsets, ts, cb)],
        compiler_params=_cp("parallel", "parallel"))(xraw, xraw, w, b)


def _conv_glu_fwd(proj1, w, b):
    s = proj1.shape[0]
    c = D_CONV
    ts, cb, grid = _conv_grid(s, c)
    goff = c // cb

    offsets = [HALO - (CONV_WIDTH - 1) + k for k in range(CONV_WIDTH)]

    def body(v_ref, g_ref, vh_ref, gh_ref, w_ref, b_ref, hc_ref, ext_ref, sh_ref):
        first = pl.program_id(1) == 0
        hh = vh_ref[...].astype(f32) * _sigmoid(gh_ref[...].astype(f32))
        ext_ref[0:HALO, :] = jnp.where(first, 0.0, hh)
        ext_ref[HALO:, :] = v_ref[...].astype(f32) * _sigmoid(g_ref[...].astype(f32))

        def emit(rows, hc):
            hc_ref[rows, :] = hc.astype(bf16)
        _conv_taps(ext_ref, sh_ref, w_ref, b_ref, ts, CONV_WIDTH, emit)

    return pl.pallas_call(
        body, name="conv_glu_fwd", grid=grid,
        in_specs=[_cur(ts, cb), _cur(ts, cb, goff), _prev_halo(ts, cb), _prev_halo(ts, cb, goff),
                  _wspec(CONV_WIDTH, cb), _wspec(1, cb)],
        out_specs=_cur(ts, cb),
        out_shape=jax.ShapeDtypeStruct((s, c), bf16),
        scratch_shapes=[pltpu.VMEM((HALO + ts, cb), f32), _shift_scratch(offsets, ts, cb)],
        compiler_params=_cp("parallel", "parallel"))(proj1, proj1, proj1, proj1, w, b)


def _conv_bwd_offsets(k_taps):
    return [k_taps - 1 - k for k in range(k_taps)], [HALO - (k_taps - 1) + k for k in range(k_taps)]


def _conv_bwd_scratch(k_taps, ts, cb):
    d_offs, x_offs = _conv_bwd_offsets(k_taps)
    return [pltpu.VMEM((ts + HALO, cb), f32), _shift_scratch(d_offs, ts, cb),
            pltpu.VMEM((HALO + ts, cb), f32), _shift_scratch(x_offs, ts, cb),
            pltpu.VMEM((k_taps, 8, cb), f32), pltpu.VMEM((8, cb), f32)]


def _conv_bwd_core(dp, dpn_ref, last, w_ref, scratch, dw_ref, db_ref, ts, k_taps, emit):
    dext_ref, dsh_ref, xext_ref, xsh_ref, dw8_ref, db8_ref = scratch
    d_offs, x_offs = _conv_bwd_offsets(k_taps)
    dext_ref[0:ts, :] = dp
    dext_ref[ts:, :] = jnp.where(last, 0.0, dpn_ref[...].astype(f32))
    _fill_phases(dext_ref, dsh_ref, d_offs, ts)
    _fill_phases(xext_ref, xsh_ref, x_offs, ts)

    @pl.when(pl.program_id(1) == 0)
    def _():
        dw8_ref[...] = jnp.zeros_like(dw8_ref)
        db8_ref[...] = jnp.zeros_like(db8_ref)
    cb = dp.shape[1]
    for sb in range(ts // CONV_SUB):
        start = sb * CONV_SUB
        dpv = dext_ref[start:start + CONV_SUB, :]
        dx = None
        for k in range(k_taps):
            t = w_ref[k:k + 1, :] * _slab(dext_ref, dsh_ref, d_offs, d_offs[k], start)
            dx = t if dx is None else dx + t
            prod = dpv * _slab(xext_ref, xsh_ref, x_offs, x_offs[k], start)
            dw8_ref[k] += jnp.sum(prod.reshape(CONV_SUB // 8, 8, cb), axis=0)
        db8_ref[...] += jnp.sum(dpv.reshape(CONV_SUB // 8, 8, cb), axis=0)
        emit(slice(start, start + CONV_SUB), dx)

    @pl.when(last)
    def _():
        dw_ref[...] = jnp.sum(dw8_ref[...], axis=1)
        db_ref[...] = jnp.sum(db8_ref[...], axis=0, keepdims=True)


def _conv_ssd_bwd(dpre, xraw, w):
    s, c = xraw.shape
    ts, cb, grid = _conv_grid(s, c)
    nb = s // ts

    def body(dp_ref, dpn_ref, x_ref, xh_ref, w_ref, dx_ref, dw_ref, db_ref, *scratch):
        i = pl.program_id(1)
        xext_ref = scratch[2]
        xext_ref[0:HALO, :] = jnp.where(i == 0, 0.0, xh_ref[...].astype(f32))
        xext_ref[HALO:, :] = x_ref[...].astype(f32)

        def emit(rows, dx):
            dx_ref[rows, :] = dx.astype(bf16)
        _conv_bwd_core(dp_ref[...].astype(f32), dpn_ref, i == nb - 1, w_ref, scratch, dw_ref, db_ref, ts, SSD_CONV, emit)

    return pl.pallas_call(
        body, name="conv_ssd_bwd", grid=grid,
        in_specs=[_cur(ts, cb), _next_halo(ts, cb, s), _cur(ts, cb), _prev_halo(ts, cb), _wspec(SSD_CONV, cb)],
        out_specs=[_cur(ts, cb), _wspec(SSD_CONV, cb), _wspec(1, cb)],
        out_shape=[jax.ShapeDtypeStruct((s, c), bf16), jax.ShapeDtypeStruct((SSD_CONV, c), f32),
                   jax.ShapeDtypeStruct((1, c), f32)],
        scratch_shapes=_conv_bwd_scratch(SSD_CONV, ts, cb),
        compiler_params=_cp("parallel", "arbitrary"))(dpre, dpre, xraw, xraw, w)


def _conv_glu_bwd(dhc, proj1, w):
    s = proj1.shape[0]
    c = D_CONV
    ts, cb, grid = _conv_grid(s, c)
    nb = s // ts
    goff = c // cb

    def body(dp_ref, dpn_ref, v_ref, g_ref, vh_ref, gh_ref, w_ref, dv_ref, dg_ref, dw_ref, db_ref, *scratch):
        i = pl.program_id(1)
        xext_ref = scratch[2]
        xext_ref[0:HALO, :] = jnp.where(i == 0, 0.0, vh_ref[...].astype(f32) * _sigmoid(gh_ref[...].astype(f32)))
        xext_ref[HALO:, :] = v_ref[...].astype(f32) * _sigmoid(g_ref[...].astype(f32))

        def emit(rows, dh):
            val = v_ref[rows, :].astype(f32)
            sg = _sigmoid(g_ref[rows, :].astype(f32))
            dv_ref[rows, :] = (dh * sg).astype(bf16)
            dg_ref[rows, :] = (dh * val * sg * (1.0 - sg)).astype(bf16)
        _conv_bwd_core(dp_ref[...].astype(f32), dpn_ref, i == nb - 1, w_ref, scratch, dw_ref, db_ref, ts, CONV_WIDTH, emit)

    return pl.pallas_call(
        body, name="conv_glu_bwd", grid=grid,
        in_specs=[_cur(ts, cb), _next_halo(ts, cb, s), _cur(ts, cb), _cur(ts, cb, goff), _prev_halo(ts, cb),
                  _prev_halo(ts, cb, goff), _wspec(CONV_WIDTH, cb)],
        out_specs=[_cur(ts, cb), _cur(ts, cb), _wspec(CONV_WIDTH, cb), _wspec(1, cb)],
        out_shape=[jax.ShapeDtypeStruct((s, c), bf16), jax.ShapeDtypeStruct((s, c), bf16),
                   jax.ShapeDtypeStruct((CONV_WIDTH, c), f32), jax.ShapeDtypeStruct((1, c), f32)],
        scratch_shapes=_conv_bwd_scratch(CONV_WIDTH, ts, cb),
        compiler_params=_cp("parallel", "arbitrary"))(dhc, dhc, proj1, proj1, proj1, proj1, w)


def _ssd_common(dt_ref, prm_ref):
    l = CHUNK
    dtb = prm_ref[0:1, :]
    a = -jnp.exp(prm_ref[1:2, :])
    dsk = prm_ref[2:3, :]
    zraw = dt_ref[...] + dtb
    dt = _softplus(zraw)
    da = dt * a
    row = lax.broadcasted_iota(jnp.int32, (l, l), 0)
    col = lax.broadcasted_iota(jnp.int32, (l, l), 1)
    causal = row >= col
    cs = _dot(causal.astype(f32), da, precision=HIGHEST)
    return a, dsk, zraw, dt, cs, cs.T, causal, row, col


def _ssd_fwd(act, dtf, prm):
    s = act.shape[0]
    nc = s // CHUNK
    l = CHUNK

    def body(xs_ref, dt_ref, prm_ref, y_ref, hs_ref, st_ref):
        @pl.when(pl.program_id(0) == 0)
        def _():
            st_ref[...] = jnp.zeros_like(st_ref)
        a, dsk, _, dt, cs, cst, causal, _, _ = _ssd_common(dt_ref, prm_ref)
        dtt = dt.T
        for g in range(N_GROUPS):
            bm = xs_ref[:, B_OFF + D_STATE * g:B_OFF + D_STATE * (g + 1)]
            cm = xs_ref[:, C_OFF + D_STATE * g:C_OFF + D_STATE * (g + 1)]
            gmat = _dot(cm, bm, NT)
            bmt = bm.astype(f32).T
            for r in range(HEADS_PER_GROUP):
                h = HEADS_PER_GROUP * g + r
                hsl = slice(HEAD_DIM * h, HEAD_DIM * (h + 1))
                xb = xs_ref[:, hsl]
                csc = cs[:, h:h + 1]
                csr = cst[h:h + 1, :]
                cl = cs[l - 1:l, h:h + 1]
                dtr = dtt[h:h + 1, :]
                dk = jnp.exp(jnp.where(causal, csc - csr, NEG))
                hp = st_ref[h]
                hs_ref[0, h] = hp
                ydiag = _dot((gmat * dk * dtr).astype(bf16), xb)
                yoff = _dot(cm, hp.astype(bf16), NT) * jnp.exp(csc)
                y_ref[:, hsl] = (ydiag + yoff + xb.astype(f32) * dsk[:, h:h + 1]).astype(bf16)
                stt = _dot((bmt * (dtr * jnp.exp(cl - csr))).astype(bf16), xb)
                st_ref[h] = hp * jnp.exp(cl) + stt.T

    return pl.pallas_call(
        body, name="ssd_fwd", grid=(nc,),
        in_specs=[pl.BlockSpec((l, XBC_W), lambda i: (i, 0)), pl.BlockSpec((l, 128), lambda i: (i, 0)),
                  pl.BlockSpec((8, 128), lambda i: (0, 0))],
        out_specs=[pl.BlockSpec((l, D_MODEL), lambda i: (i, 0)),
                   pl.BlockSpec((1, N_HEADS, HEAD_DIM, D_STATE), lambda i: (i, 0, 0, 0))],
        out_shape=[jax.ShapeDtypeStruct((s, D_MODEL), bf16),
                   jax.ShapeDtypeStruct((nc, N_HEADS, HEAD_DIM, D_STATE), f32)],
        scratch_shapes=[pltpu.VMEM((N_HEADS, HEAD_DIM, D_STATE), f32)],
        compiler_params=_cp("arbitrary"))(act, dtf, prm)


def _ssd_bwd(act, pre, dtf, prm, hs, dy):
    s = act.shape[0]
    nc = s // CHUNK
    l = CHUNK

    def body(xs_ref, pre_ref, dt_ref, prm_ref, hs_ref, dy_ref, dpre_ref, ddt_ref, dprm_ref, dh_ref):
        @pl.when(pl.program_id(0) == 0)
        def _():
            dh_ref[...] = jnp.zeros_like(dh_ref)
            dprm_ref[...] = jnp.zeros_like(dprm_ref)
        a, dsk, zraw, dt, cs, cst, causal, row, col = _ssd_common(dt_ref, prm_ref)
        lane = lax.broadcasted_iota(jnp.int32, (l, 128), 1)
        rowl = lax.broadcasted_iota(jnp.int32, (l, 128), 0)
        sub = lax.broadcasted_iota(jnp.int32, (128, l), 0)
        lane1 = lax.broadcasted_iota(jnp.int32, (1, 128), 1)
        dcs_c = jnp.zeros((l, 128), f32)
        dcs_r = jnp.zeros((128, l), f32)
        ddt_c = jnp.zeros((l, 128), f32)
        dd_row = jnp.zeros((1, 128), f32)
        ones_cols = jnp.ones((HEAD_DIM, 128), bf16)

        def lanesum(xv64):
            hi = xv64.astype(bf16)
            lo = (xv64 - hi.astype(f32)).astype(bf16)
            return _dot(hi, ones_cols) + _dot(lo, ones_cols)

        for g in range(N_GROUPS):
            bsl = slice(B_OFF + D_STATE * g, B_OFF + D_STATE * (g + 1))
            csl = slice(C_OFF + D_STATE * g, C_OFF + D_STATE * (g + 1))
            bm = xs_ref[:, bsl]
            cm = xs_ref[:, csl]
            gmat = _dot(cm, bm, NT)
            gmat_t = _dot(bm, cm, NT)
            dgm = jnp.zeros((l, l), f32)
            dbg = jnp.zeros((l, D_STATE), f32)
            dcg = jnp.zeros((l, D_STATE), f32)
            for r in range(HEADS_PER_GROUP):
                h = HEADS_PER_GROUP * g + r
                hsl = slice(HEAD_DIM * h, HEAD_DIM * (h + 1))
                xv = xs_ref[:, hsl].astype(f32)
                dyv = dy_ref[:, hsl].astype(f32)
                dyb = dyv.astype(bf16)
                csc = cs[:, h:h + 1]
                csr = cst[h:h + 1, :]
                cl = cs[l - 1:l, h:h + 1]
                dk = jnp.exp(jnp.where(causal, csc - csr, NEG))
                mf = gmat * dk
                dtc = dt[:, h:h + 1]
                xd = xv * dtc
                xdb = xd.astype(bf16)
                ecs = jnp.exp(csc)
                dec = jnp.exp(cl)
                e = jnp.exp(cl - csc)
                hp = hs_ref[0, h]
                hpb = hp.astype(bf16)
                dhn = dh_ref[h]
                dhnb = dhn.astype(bf16)
                dd_h = jnp.sum(jnp.sum(dyv * xv, axis=1, keepdims=True), axis=0, keepdims=True)
                dx = dyv * dsk[:, h:h + 1]
                ch = _dot(cm, hpb, NT)
                dye = dyv * ecs
                dyeb = dye.astype(bf16)
                dcg = dcg + _dot(dyeb, hpb)
                dhp = _dot(dyeb, cm, TN)
                dcs_b = lanesum(dye * ch)
                dm = _dot(dyb, xdb, NT)
                mft = gmat_t * jnp.exp(jnp.where(row <= col, csr - csc, NEG))
                dxd = _dot(mft.astype(bf16), dyb)
                dgm = dgm + dm * dk
                wmat = dm * mf
                wm_col = jnp.sum(wmat, axis=1, keepdims=True)
                dcs_row = -jnp.sum(wmat, axis=0, keepdims=True)
                ddec = jnp.sum(jnp.sum(hp * dhn, axis=1, keepdims=True), axis=0, keepdims=True)
                dxe = _dot(bm, dhnb, NT)
                dxd = dxd + dxe * e
                xde = xd * e
                de_b = lanesum(dxe * xde)
                dbg = dbg + _dot(xde.astype(bf16), dhnb)
                dcs_b = dcs_b - de_b
                dlast = ddec * dec + jnp.sum(de_b, axis=0, keepdims=True)
                dh_ref[h] = dhp + dec * dhn
                dx = dx + dxd * dtc
                ddt_b = lanesum(dxd * xv)
                is_h = lane == h
                dcs_c = dcs_c + jnp.where(is_h, dcs_b + wm_col, 0.0) + jnp.where(is_h & (rowl == l - 1), dlast, 0.0)
                dcs_r = dcs_r + jnp.where(sub == h, dcs_row, 0.0)
                ddt_c = ddt_c + jnp.where(is_h, ddt_b, 0.0)
                dd_row = dd_row + jnp.where(lane1 == h, dd_h, 0.0)
                dpre_ref[:, hsl] = (dx * _dsilu(pre_ref[:, hsl].astype(f32))).astype(bf16)
            dgb = dgm.astype(bf16)
            dcg = dcg + _dot(dgb, bm)
            dbg = dbg + _dot(dgb, cm, TN)
            dpre_ref[:, bsl] = (dbg * _dsilu(pre_ref[:, bsl].astype(f32))).astype(bf16)
            dpre_ref[:, csl] = (dcg * _dsilu(pre_ref[:, csl].astype(f32))).astype(bf16)
        dcs = dcs_c + dcs_r.T
        dda = _dot((row <= col).astype(f32), dcs, precision=HIGHEST)
        ddt = ddt_c + dda * a
        ddtraw = jnp.where(lane < N_HEADS, ddt * _sigmoid(zraw), 0.0)
        ddt_ref[...] = ddtraw
        dprm_ref[0:1, :] += jnp.sum(ddtraw, axis=0, keepdims=True)
        dprm_ref[1:2, :] += jnp.where(lane1 < N_HEADS, jnp.sum(dda * dt, axis=0, keepdims=True) * a, 0.0)
        dprm_ref[2:3, :] += dd_row

    def rev(i):
        return (nc - 1 - i, 0)

    return pl.pallas_call(
        body, name="ssd_bwd", grid=(nc,),
        in_specs=[pl.BlockSpec((l, XBC_W), rev), pl.BlockSpec((l, XBC_W), rev),
                  pl.BlockSpec((l, 128), rev), pl.BlockSpec((8, 128), lambda i: (0, 0)),
                  pl.BlockSpec((1, N_HEADS, HEAD_DIM, D_STATE), lambda i: (nc - 1 - i, 0, 0, 0)),
                  pl.BlockSpec((l, D_MODEL), rev)],
        out_specs=[pl.BlockSpec((l, XBC_W), rev), pl.BlockSpec((l, 128), rev), pl.BlockSpec((8, 128), lambda i: (0, 0))],
        out_shape=[jax.ShapeDtypeStruct((s, XBC_W), bf16), jax.ShapeDtypeStruct((s, 128), f32),
                   jax.ShapeDtypeStruct((8, 128), f32)],
        scratch_shapes=[pltpu.VMEM((N_HEADS, HEAD_DIM, D_STATE), f32)],
        compiler_params=_cp("arbitrary"))(act, pre, dtf, prm, hs, dy)


def _fox_cumsum(dtf, prm):
    s = dtf.shape[0]
    l = CHUNK

    def body(f_ref, prm_ref, c_ref, carry_ref):
        @pl.when(pl.program_id(0) == 0)
        def _():
            carry_ref[...] = jnp.zeros_like(carry_ref)
        lf = _log_sigmoid(f_ref[...] + prm_ref[3:4, :])
        row = lax.broadcasted_iota(jnp.int32, (l, l), 0)
        col = lax.broadcasted_iota(jnp.int32, (l, l), 1)
        c = _dot((row >= col).astype(f32), lf, precision=HIGHEST) + carry_ref[...]
        c_ref[...] = c
        carry_ref[...] = c[l - 1:l, :]

    return pl.pallas_call(
        body, name="fox_cumsum", grid=(s // l,),
        in_specs=[pl.BlockSpec((l, 128), lambda i: (i, 0)), pl.BlockSpec((8, 128), lambda i: (0, 0))],
        out_specs=pl.BlockSpec((l, 128), lambda i: (i, 0)),
        out_shape=jax.ShapeDtypeStruct((s, 128), f32),
        scratch_shapes=[pltpu.VMEM((1, 128), f32)],
        compiler_params=_cp("arbitrary"))(dtf, prm)


def _position():
    return lax.axis_index("x"), lax.axis_index("y"), lax.axis_index("c")


def _exchange_sems(n):
    return [pltpu.SemaphoreType.DMA((n, N_DEV - 1)), pltpu.SemaphoreType.DMA((n, N_DEV - 1)),
            pltpu.SemaphoreType.DMA((n,))]


def _exchange_copies(g_refs, r_refs, send_sems, recv_sems, local_sems, gather=False):
    n = len(g_refs)
    x, y, cc = _position()
    me = 4 * x + 2 * y + cc

    def src(a, j):
        return g_refs[a] if gather else g_refs[a].at[j]

    local = [pltpu.make_async_copy(src(a, me), r_refs[a].at[me], local_sems.at[a]) for a in range(n)]
    sends, recvs = [], []
    for k in range(1, N_DEV):
        px = 1 - x if k & 4 else x
        py = 1 - y if k & 2 else y
        pc = 1 - cc if k & 1 else cc
        pid = 4 * px + 2 * py + pc
        for a in range(n):
            sends.append(pltpu.make_async_remote_copy(
                src_ref=src(a, pid), dst_ref=r_refs[a].at[me],
                send_sem=send_sems.at[a, k - 1], recv_sem=recv_sems.at[a, k - 1],
                device_id=(px, py, pc), device_id_type=pl.DeviceIdType.MESH))
            recvs.append(pltpu.make_async_remote_copy(
                src_ref=src(a, pid), dst_ref=r_refs[a].at[pid],
                send_sem=send_sems.at[a, k - 1], recv_sem=recv_sems.at[a, k - 1],
                device_id=(px, py, pc), device_id_type=pl.DeviceIdType.MESH))
    return local, sends, recvs


def _exchange_start(copies):
    local, sends, _ = copies
    for cp in local + sends:
        cp.start()


def _exchange_wait(copies):
    local, sends, recvs = copies
    for cp in recvs:
        cp.wait_recv()
    for cp in sends:
        cp.wait_send()
    for cp in local:
        cp.wait()


AUG = HEAD_DIM
N_PAIRS = N_HEADS // 2
V_BLOCK = 2 * D_MODEL // 128


def _split3(x):
    hi = x.astype(bf16)
    r1 = x - hi.astype(f32)
    mid = r1.astype(bf16)
    lo = (r1 - mid.astype(f32)).astype(bf16)
    return hi.astype(f32), mid.astype(f32), lo.astype(f32)


def _fox_prep(qkv, c):
    s = qkv.shape[0]
    ts = min(CONV_ROW_TILE, s)

    def body(q_ref, k_ref, c_ref, qa_ref, ka_ref):
        lane = lax.broadcasted_iota(jnp.int32, (ts, 128), 1)
        low = lane < HEAD_DIM
        for h in range(N_HEADS):
            psl = slice(128 * (h // 2), 128 * (h // 2 + 1))
            qv = q_ref[:, psl].astype(f32) * (HEAD_DIM ** -0.5)
            kv = k_ref[:, psl].astype(f32)
            if h % 2:
                qv = pltpu.roll(qv, HEAD_DIM, 1)
                kv = pltpu.roll(kv, HEAD_DIM, 1)
            hi, mid, lo = _split3(c_ref[:, F_LANE + h:F_LANE + h + 1])
            ones = jnp.where((lane >= AUG + 3) & (lane < AUG + 6), 1.0, 0.0)
            cq = jnp.where(lane == AUG, hi, jnp.where(lane == AUG + 1, mid, jnp.where(lane == AUG + 2, lo, ones)))
            qa_ref[h] = jnp.where(low, qv, cq).astype(bf16)
            onek = jnp.where((lane >= AUG) & (lane < AUG + 3), 1.0, 0.0)
            ck = jnp.where(lane == AUG + 3, -hi, jnp.where(lane == AUG + 4, -mid, jnp.where(lane == AUG + 5, -lo, onek)))
            ka_ref[h] = jnp.where(low, kv, ck).astype(bf16)

    hm = pl.BlockSpec((N_HEADS, ts, 128), lambda i: (0, i, 0))
    return pl.pallas_call(
        body, name="fox_prep", grid=(s // ts,),
        in_specs=[_rowspec(ts, D_MODEL, 0), _rowspec(ts, D_MODEL, 1), _rowspec(ts, 128)],
        out_specs=[hm, hm], out_shape=[jax.ShapeDtypeStruct((N_HEADS, s, 128), bf16)] * 2,
        compiler_params=_cp("parallel"))(qkv, qkv, c)


def _fox_fwd(qa, ka, qkv, ws):
    s = qkv.shape[0]
    t = min(ATTN_FWD_TILE, s)
    nq = s // t
    n = len(ws)

    def body(qa_ref, ka_ref, v_ref, *rest):
        w_refs, (o_ref, lse_ref), wg_refs, sems = rest[:n], rest[n:n + 2], rest[n + 2:2 * n + 2], rest[2 * n + 2:]
        qi = pl.program_id(1)
        copies = _exchange_copies(w_refs, wg_refs, *sems, gather=True)

        @pl.when((pl.program_id(0) == 0) & (qi == 0))
        def _():
            _exchange_start(copies)
        low = lax.broadcasted_iota(jnp.int32, (t, 128), 1) < HEAD_DIM
        row = lax.broadcasted_iota(jnp.int32, (t, t), 0)
        col = lax.broadcasted_iota(jnp.int32, (t, t), 1)

        def tile(ki, carry, diagonal):
            koff = pl.multiple_of(ki * t, t)
            v = v_ref[pl.ds(koff, t), :]
            vh = (jnp.where(low, v, jnp.ones_like(v)), jnp.where(low, jnp.ones_like(v), v))
            new = []
            for r in range(2):
                m_old, acc = carry[r]
                sc = _dot(qa_ref[r], ka_ref[r, pl.ds(koff, t), :], NT)
                if diagonal:
                    sc = jnp.where(col <= row, sc, NEG)
                m_new = jnp.maximum(m_old, jnp.max(sc, axis=1, keepdims=True))
                p = jnp.exp(sc - m_new)
                new.append((m_new, acc * jnp.exp(m_old - m_new) + _dot(p.astype(bf16), vh[r])))
            return tuple(new)

        init = ((jnp.full((t, 1), NEG, f32), jnp.zeros((t, 128), f32)),) * 2
        carry = lax.fori_loop(0, qi, lambda ki, cr: tile(ki, cr, False), init)
        (m_a, acc_a), (m_b, acc_b) = tile(qi, carry, True)
        l_a, l_b = acc_a[:, HEAD_DIM:HEAD_DIM + 1], acc_b[:, 0:1]
        o_ref[...] = jnp.where(low, acc_a / l_a, acc_b / l_b).astype(bf16)
        for r, lse in enumerate((m_a + jnp.log(l_a), m_b + jnp.log(l_b))):
            lse_ref[r] = jnp.broadcast_to(lse, (t, 128)).T[0:1, :]

        @pl.when((pl.program_id(0) == N_PAIRS - 1) & (qi == nq - 1))
        def _():
            _exchange_wait(copies)

    anyspec = pl.BlockSpec(memory_space=pl.ANY)
    outs = pl.pallas_call(
        body, name="fox_fwd", grid=(N_PAIRS, nq),
        in_specs=[pl.BlockSpec((2, t, 128), lambda j, qi: (j, qi, 0)),
                  pl.BlockSpec((2, s, 128), lambda j, qi: (j, 0, 0)),
                  pl.BlockSpec((s, 128), lambda j, qi: (0, V_BLOCK + j))] + [anyspec] * n,
        out_specs=[pl.BlockSpec((t, 128), lambda j, qi: (qi, j)), pl.BlockSpec((2, 1, t), lambda j, qi: (j, 0, qi))]
        + [anyspec] * n,
        out_shape=[jax.ShapeDtypeStruct((s, D_MODEL), bf16), jax.ShapeDtypeStruct((N_HEADS, 1, s), f32)]
        + [jax.ShapeDtypeStruct((N_DEV,) + w.shape, w.dtype) for w in ws],
        scratch_shapes=_exchange_sems(n),
        compiler_params=_cp("arbitrary", "arbitrary"))(qa, ka, qkv, *ws)
    return outs[0], outs[1], outs[2:]


def _fox_bwd(qa, ka, qkv, do, lse, delta, gs):
    s = qkv.shape[0]
    t = min(ATTN_TILE, s)
    nq = s // t
    n = len(gs)

    def body(qa_ref, ka_ref, v_ref, do_ref, lse_ref, dl_ref, *rest):
        g_refs, (dq_ref, dk_ref, dv_ref), r_refs, sems = rest[:n], rest[n:n + 3], rest[n + 3:2 * n + 3], rest[2 * n + 3:]
        ki = pl.program_id(1)
        copies = _exchange_copies(g_refs, r_refs, *sems)

        @pl.when((pl.program_id(0) == 0) & (ki == 0))
        def _():
            _exchange_start(copies)

        @pl.when(ki == 0)
        def _():
            dq_ref[...] = jnp.zeros_like(dq_ref)
        def block(carry, k_lo, n, qoff, diagonal):
            dks, dv = carry
            low = lax.broadcasted_iota(jnp.int32, (n, 128), 1) < HEAD_DIM
            v = v_ref[k_lo:k_lo + n, :]
            zero = jnp.zeros_like(v)
            vh = (jnp.where(low, v, zero), jnp.where(low, zero, v))
            dov = do_ref[pl.ds(qoff, n), :]
            doh = (jnp.where(low, dov, zero), jnp.where(low, zero, dov))
            new_dks = []
            for r in range(2):
                qt = qa_ref[r, pl.ds(qoff, n), :]
                kt = ka_ref[r, k_lo:k_lo + n, :]
                sct = _dot(kt, qt, NT)
                if diagonal:
                    row = lax.broadcasted_iota(jnp.int32, (n, n), 0)
                    col = lax.broadcasted_iota(jnp.int32, (n, n), 1)
                    sct = jnp.where(row <= col, sct, NEG)
                pt = jnp.exp(sct - lse_ref[r, :, pl.ds(qoff, n)])
                dpt = _dot(vh[r], dov, NT)
                dst = (pt * (dpt - dl_ref[r, :, pl.ds(qoff, n)])).astype(bf16)
                dv = dv + _dot(pt.astype(bf16), doh[r])
                new_dks.append(dks[r] + _dot(dst, qt))
                dq_ref[r, pl.ds(qoff, n), :] += _dot(dst, kt, TN)
            return tuple(new_dks), dv

        h = t // 2
        zacc = jnp.zeros((h, 128), f32)
        zero_carry = ((zacc, zacc), zacc)
        q0 = pl.multiple_of(ki * t, t)
        q1 = pl.multiple_of(ki * t + h, h)
        first = block(block(zero_carry, 0, h, q0, True), 0, h, q1, False)
        second = block(zero_carry, h, h, q1, True)
        carry = (tuple(jnp.concatenate([first[0][r], second[0][r]], axis=0) for r in range(2)),
                 jnp.concatenate([first[1], second[1]], axis=0))
        dks, dv = lax.fori_loop(
            ki + 1, nq, lambda qi, cr: block(cr, 0, t, pl.multiple_of(qi * t, t), False), carry)
        dk_ref[0] = dks[0]
        dk_ref[1] = dks[1]
        dv_ref[...] = dv.astype(bf16)

        @pl.when((pl.program_id(0) == N_PAIRS - 1) & (ki == nq - 1))
        def _():
            _exchange_wait(copies)

    anyspec = pl.BlockSpec(memory_space=pl.ANY)
    outs = pl.pallas_call(
        body, name="fox_bwd", grid=(N_PAIRS, nq),
        in_specs=[pl.BlockSpec((2, s, 128), lambda j, ki: (j, 0, 0)),
                  pl.BlockSpec((2, t, 128), lambda j, ki: (j, ki, 0)),
                  pl.BlockSpec((t, 128), lambda j, ki: (ki, V_BLOCK + j)),
                  pl.BlockSpec((s, 128), lambda j, ki: (0, j)),
                  pl.BlockSpec((2, 1, s), lambda j, ki: (j, 0, 0)),
                  pl.BlockSpec((2, 1, s), lambda j, ki: (j, 0, 0))] + [anyspec] * n,
        out_specs=[pl.BlockSpec((2, s, 128), lambda j, ki: (j, 0, 0)),
                   pl.BlockSpec((2, t, 128), lambda j, ki: (j, ki, 0)),
                   pl.BlockSpec((t, 128), lambda j, ki: (ki, j))] + [anyspec] * n,
        out_shape=[jax.ShapeDtypeStruct((N_HEADS, s, 128), f32), jax.ShapeDtypeStruct((N_HEADS, s, 128), f32),
                   jax.ShapeDtypeStruct((s, D_MODEL), bf16)] + [jax.ShapeDtypeStruct(g.shape, g.dtype) for g in gs],
        scratch_shapes=_exchange_sems(n),
        compiler_params=_cp("arbitrary", "arbitrary"))(qa, ka, qkv, do, lse, delta, *gs)
    return outs[0], outs[1], outs[2], outs[3:]


def _fox_bwd_post(dq_hm, dk_hm):
    s = dq_hm.shape[1]
    ts = min(CONV_ROW_TILE, s)

    def body(dq_ref, dk_ref, q_ref, k_ref, dc_ref):
        lane = lax.broadcasted_iota(jnp.int32, (ts, 128), 1)
        dc = jnp.zeros((ts, 128), f32)
        for h in range(N_HEADS):
            hsl = slice(HEAD_DIM * h, HEAD_DIM * (h + 1))
            dqv = dq_ref[h]
            dkv = dk_ref[h]
            q_ref[:, hsl] = (dqv[:, 0:HEAD_DIM] * (HEAD_DIM ** -0.5)).astype(bf16)
            k_ref[:, hsl] = dkv[:, 0:HEAD_DIM].astype(bf16)
            dc = dc + jnp.where(lane == F_LANE + h, dqv[:, AUG:AUG + 1] - dkv[:, AUG + 3:AUG + 4], 0.0)
        dc_ref[...] = dc

    hm = pl.BlockSpec((N_HEADS, ts, 128), lambda i: (0, i, 0))
    return pl.pallas_call(
        body, name="fox_bwd_post", grid=(s // ts,), in_specs=[hm, hm],
        out_specs=[_rowspec(ts, D_MODEL), _rowspec(ts, D_MODEL), _rowspec(ts, 128)],
        out_shape=[jax.ShapeDtypeStruct((s, D_MODEL), bf16), jax.ShapeDtypeStruct((s, D_MODEL), bf16),
                   jax.ShapeDtypeStruct((s, 128), f32)],
        compiler_params=_cp("parallel"))(dq_hm, dk_hm)


def _fox_gate_bwd(dc, dtf, prm, ddt_raw):
    s = dtf.shape[0]
    l = CHUNK
    nb = s // l

    def body(dc_ref, f_ref, prm_ref, ddt_ref, out_ref, dfb_ref, carry_ref):
        @pl.when(pl.program_id(0) == 0)
        def _():
            carry_ref[...] = jnp.zeros_like(carry_ref)
            dfb_ref[...] = jnp.zeros_like(dfb_ref)
        dc = dc_ref[...]
        row = lax.broadcasted_iota(jnp.int32, (l, l), 0)
        col = lax.broadcasted_iota(jnp.int32, (l, l), 1)
        dlf = _dot((row <= col).astype(f32), dc, precision=HIGHEST) + carry_ref[...]
        carry_ref[...] = dlf[0:1, :]
        lane = lax.broadcasted_iota(jnp.int32, (l, 128), 1)
        is_f = (lane >= F_LANE) & (lane < F_LANE + N_HEADS)
        dfr = jnp.where(is_f, dlf * _sigmoid(-(f_ref[...] + prm_ref[3:4, :])), 0.0)
        dfb_ref[...] += jnp.sum(dfr, axis=0, keepdims=True)
        out_ref[...] = ddt_ref[...] + dfr

    def rev(i):
        return (nb - 1 - i, 0)

    return pl.pallas_call(
        body, name="fox_gate_bwd", grid=(nb,),
        in_specs=[pl.BlockSpec((l, 128), rev), pl.BlockSpec((l, 128), rev), pl.BlockSpec((8, 128), lambda i: (0, 0)),
                  pl.BlockSpec((l, 128), rev)],
        out_specs=[pl.BlockSpec((l, 128), rev), pl.BlockSpec((1, 128), lambda i: (0, 0))],
        out_shape=[jax.ShapeDtypeStruct((s, 128), f32), jax.ShapeDtypeStruct((1, 128), f32)],
        scratch_shapes=[pltpu.VMEM((1, 128), f32)],
        compiler_params=_cp("arbitrary"))(dc, dtf, prm, ddt_raw)


def _all_gather(xl, name):
    r, c = xl.shape

    def body(x_ref, out_ref, send_sems, recv_sems, local_sem):
        x, y, cc = _position()
        me, sibling = (x, y, cc), (x, y, 1 - cc)
        chips = [(1 - x, y), (x, 1 - y), (1 - x, 1 - y)]

        def slot(px, py, pc):
            return out_ref.at[4 * px + 2 * py + pc]

        def copy(k, block, to, src=None):
            return pltpu.make_async_remote_copy(
                src_ref=slot(*block) if src is None else src, dst_ref=slot(*block),
                send_sem=send_sems.at[k], recv_sem=recv_sems.at[k],
                device_id=to, device_id_type=pl.DeviceIdType.MESH)

        mine = pltpu.make_async_copy(x_ref, slot(*me), local_sem)
        mine.start()
        first = [copy(0, me, sibling, src=x_ref)]
        first += [copy(1 + j, me, (*chip, cc), src=x_ref) for j, chip in enumerate(chips)]
        for cp in first:
            cp.start()
        passed = [copy(4 + j, (*chip, cc), sibling) for j, chip in enumerate(chips)]
        for j, chip in enumerate(chips):
            copy(1 + j, (*chip, cc), me).wait_recv()
            passed[j].start()
        copy(0, sibling, me).wait_recv()
        for j, chip in enumerate(chips):
            copy(4 + j, (*chip, 1 - cc), me).wait_recv()
        for cp in first + passed:
            cp.wait_send()
        mine.wait()

    return pl.pallas_call(
        body, name=name,
        out_shape=jax.ShapeDtypeStruct((N_DEV, r, c), xl.dtype),
        in_specs=[pl.BlockSpec(memory_space=pl.ANY)], out_specs=pl.BlockSpec(memory_space=pl.ANY),
        scratch_shapes=[pltpu.SemaphoreType.DMA((7,)), pltpu.SemaphoreType.DMA((7,)), pltpu.SemaphoreType.DMA],
    )(xl)


def _sum_parts(parts, name):
    n, r, c = parts.shape

    def body(p_ref, o_ref):
        g = p_ref[0]
        for i in range(1, n):
            g = g + p_ref[i]
        o_ref[...] = g

    return pl.pallas_call(body, name=name, out_shape=jax.ShapeDtypeStruct((r, c), f32))(parts)


def _adamw(w, m, v, parts, name, tr=128, by_columns=False):
    r, c = w.shape
    n = parts.shape[0]
    tr = min(tr, r)
    c1 = 1.0 - ADAM_B1 ** ADAM_STEP
    c2 = 1.0 - ADAM_B2 ** ADAM_STEP

    def body(w_ref, m_ref, v_ref, p_ref, g_ref, d_ref, nm_ref, nv_ref):
        g = p_ref[0].astype(f32)
        for i in range(1, n):
            g = g + p_ref[i].astype(f32)
        g_ref[...] = g
        nm = ADAM_B1 * m_ref[...] + (1.0 - ADAM_B1) * g
        nv = ADAM_B2 * v_ref[...] + (1.0 - ADAM_B2) * (g * g)
        nm_ref[...] = nm
        nv_ref[...] = nv
        d_ref[...] = -ADAM_LR * ((nm / c1) / (jnp.sqrt(nv / c2) + ADAM_EPS) + ADAM_WD * w_ref[...])

    if by_columns:
        blk = pl.BlockSpec((r, 128), lambda i: (0, i))
        pblk = pl.BlockSpec((n, r, 128), lambda i: (0, 0, i))
        steps = c // 128
    else:
        blk = pl.BlockSpec((tr, c), lambda i: (i, 0))
        pblk = pl.BlockSpec((n, tr, c), lambda i: (0, i, 0))
        steps = r // tr
    return pl.pallas_call(
        body, name=name, grid=(steps,),
        in_specs=[blk, blk, blk, pblk],
        out_specs=[blk] * 4, out_shape=[jax.ShapeDtypeStruct((r, c), f32)] * 4,
        compiler_params=_cp("parallel"))(w, m, v, parts)


def _lanes(w):
    return -(-w // 128) * 128


def _pack(arrs):
    rows = []
    for a in arrs:
        k, w = a.shape
        if w % 128:
            a = jnp.pad(a, ((0, 0), (0, _lanes(w) - w)))
        rows.append(a.reshape(-1, 128))
    out = jnp.concatenate(rows, axis=0)
    pad = -out.shape[0] % 8
    return jnp.pad(out, ((0, pad), (0, 0))) if pad else out


def _unpack(packed, shapes):
    outs, off = [], 0
    lead = packed.shape[:-2]
    for k, w in shapes:
        nrow = k * _lanes(w) // 128
        a = packed[..., off:off + nrow, :].reshape(*lead, k, _lanes(w))[..., :w]
        outs.append(a)
        off += nrow
    return outs


def _gathered_cols(a):
    n, k, wl = a.shape
    return jnp.transpose(a, (1, 0, 2)).reshape(k, n * wl)


def _col_shards(a):
    k, w = a.shape
    return jnp.transpose(a.reshape(k, N_DEV, w // N_DEV), (1, 0, 2))


SMALL_PARAMS = (
    ("e_norm_pre", 1, 1024, False), ("e_conv_w", 4, 2048, True), ("e_conv_b", 1, 2048, False),
    ("e_dt_bias", 1, 16, False), ("e_a_log", 1, 16, False), ("e_d_skip", 1, 16, False), ("e_fgate_b", 1, 16, False),
    ("e_ssd_norm", 1, 1024, False), ("e_norm_post", 1, 1024, False), ("o_norm_pre", 1, 1024, True),
    ("o_conv_w", 31, 2048, True), ("o_conv_b", 1, 2048, True), ("o_ln_g", 1, 2048, True), ("o_ln_b", 1, 2048, True),
    ("o_norm_post", 1, 1024, True),
)
BIG_PARAMS = ("e_w_in", "e_w_out", "o_w_in", "o_w_out")
WEIGHT_ORDER = ("e_norm_pre", "e_w_in", "e_conv_w", "e_conv_b", "e_dt_bias", "e_a_log", "e_d_skip", "e_fgate_b",
                "e_ssd_norm", "e_w_out", "e_norm_post", "o_norm_pre", "o_w_in", "o_conv_w", "o_conv_b", "o_ln_g",
                "o_ln_b", "o_w_out", "o_norm_post")
E_IN = 7200
O_IN = 6144


def kernel(x, e_norm_pre, e_w_in, e_conv_w, e_conv_b, e_dt_bias, e_a_log, e_d_skip, e_fgate_b, e_ssd_norm, e_w_out, e_norm_post, o_norm_pre, o_w_in, o_conv_w, o_conv_b, o_ln_g, o_ln_b, o_w_out, o_norm_post, loss_target, m_e_norm_pre, m_e_w_in, m_e_conv_w, m_e_conv_b, m_e_dt_bias, m_e_a_log, m_e_d_skip, m_e_fgate_b, m_e_ssd_norm, m_e_w_out, m_e_norm_post, m_o_norm_pre, m_o_w_in, m_o_conv_w, m_o_conv_b, m_o_ln_g, m_o_ln_b, m_o_w_out, m_o_norm_post, v_e_norm_pre, v_e_w_in, v_e_conv_w, v_e_conv_b, v_e_dt_bias, v_e_a_log, v_e_d_skip, v_e_fgate_b, v_e_ssd_norm, v_e_w_out, v_e_norm_post, v_o_norm_pre, v_o_w_in, v_o_conv_w, v_o_conv_b, v_o_ln_g, v_o_ln_b, v_o_w_out, v_o_norm_post):
    given = dict(locals())
    w_in = {n: given[n] for n in WEIGHT_ORDER}
    m_in = {n: given["m_" + n] for n in WEIGHT_ORDER}
    v_in = {n: given["v_" + n] for n in WEIGHT_ORDER}

    def mat(a):
        return a.reshape(a.shape[-2:])

    xs = mat(x)
    tgt = mat(loss_target)
    xi, yi, ci = _position()
    me = 4 * xi + 2 * yi + ci
    ow = O_IN // N_DEV
    wr = D_CONV // N_DEV

    ew = E_IN // N_DEV
    w_t = _all_gather(jnp.transpose(mat(e_w_in)).astype(bf16), "gather_weights").reshape(E_IN, D_MODEL)
    later_weights = [mat(e_w_out).astype(bf16), mat(o_w_in).astype(bf16), mat(o_w_out).astype(bf16)]
    w_z, w_xbc, w_qkv = w_t[0:2048], w_t[2048:4096], w_t[4112:7184]
    w_dtf = jnp.concatenate([w_t[4096:4112], w_t[7184:7200], jnp.zeros((96, D_MODEL), bf16)], axis=0)

    sharded_small = [(n, k, w) for n, k, w, sh in SMALL_PARAMS if sh]
    sg = _all_gather(_pack([mat(w_in[n]) for n, _, _ in sharded_small]), "gather_small_weights")
    full_small = {n: _gathered_cols(a)
                  for (n, _, _), a in zip(sharded_small, _unpack(sg, [(k, w // N_DEV) for _, k, w in sharded_small]))}
    for n, _, _, sh in SMALL_PARAMS:
        if not sh:
            full_small[n] = mat(w_in[n])
    p = full_small
    prm = jnp.zeros((8, 128), f32)
    prm = prm.at[0, 0:16].set(p["e_dt_bias"][0]).at[1, 0:16].set(p["e_a_log"][0]).at[2, 0:16].set(p["e_d_skip"][0])
    prm = prm.at[3, F_LANE:F_LANE + 16].set(p["e_fgate_b"][0])

    u0 = _rms_fwd(xs, p["e_norm_pre"], "rms_pre0")
    z0 = _mm_nt([(u0, 0, w_z, 0, D_MODEL)], bf16, "proj0_z", tm=1024, tn=1024)
    xraw = _mm_nt([(u0, 0, w_xbc, 0, D_MODEL)], bf16, "proj0_xbc", tm=1024, tn=1024)
    qkv = _mm_nt([(u0, 0, w_qkv, 0, D_MODEL)], bf16, "proj0_qkv", tm=1024, tn=1024)
    dtf = _mm_nt([(u0, 0, w_dtf, 0, D_MODEL)], f32, "proj0_dtf", tm=1024, tn=128)
    pre, act = _conv_ssd_fwd(xraw, p["e_conv_w"], p["e_conv_b"])
    y, hs = _ssd_fwd(act, dtf, prm)
    qa, ka = _fox_prep(qkv, _fox_cumsum(dtf, prm))
    o, lse, (e_w_out_g, o_w_in_g, o_w_out_g) = _fox_fwd(qa, ka, qkv, later_weights)
    e_w_out_f = e_w_out_g.reshape(D_CONV, D_MODEL)
    o_w_in_f = _gathered_cols(o_w_in_g)
    o_w_out_f = o_w_out_g.reshape(D_CONV, D_MODEL)
    cat = _gate0_fwd(y, z0, o, p["e_ssd_norm"])
    out0 = _mm_nn(cat, e_w_out_f, f32, "out0")
    x1, u1 = _post0_pre1(xs, out0, p["e_norm_post"], p["o_norm_pre"])

    proj1 = _mm_nn(u1, o_w_in_f, bf16, "proj1")
    hc = _conv_glu_fwd(proj1, p["o_conv_w"], p["o_conv_b"])
    h3 = _ln_gate_fwd(hc, proj1, p["o_ln_g"], p["o_ln_b"])
    out1 = _mm_nn(h3, o_w_out_f, f32, "out1")
    dy, d_out1, dg_post1, loss_part = _final_loss(x1, out1, tgt, p["o_norm_post"])

    dh3 = _mm_nt([(d_out1, 0, o_w_out_f, 0, D_MODEL)], bf16, "dh3", tm=1024, tn=D_CONV)
    g_o_w_out = _mm_tn(h3, d_out1, "dw_out1")
    dhc, dz1, dg_ln, db_ln = _ln_gate_bwd(hc, proj1, dh3, p["o_ln_g"], p["o_ln_b"])
    dval, dgate, dw_conv1, db_conv1 = _conv_glu_bwd(dhc, proj1, p["o_conv_w"])
    dproj1 = jnp.concatenate([dval, dgate, dz1], axis=1)
    du1 = _mm_nt([(dproj1, 0, o_w_in_f, 0, O_IN)], f32, "du1")
    g_o_w_in = _mm_tn(u1, dproj1, "dw_in1", tn=ow, blocked=True)
    dx1, d_out0, dg_pre1, dg_post0 = _mid_bwd(x1, du1, dy, out0, p["o_norm_pre"], p["e_norm_post"])

    dcat = _mm_nt([(d_out0, 0, e_w_out_f, 0, D_MODEL)], bf16, "dcat", tm=1024, tn=D_CONV)
    g_e_w_out = _mm_tn(cat, d_out0, "dw_out0")
    dy_ssd, do, dz0, delta, dg_ssd_norm = _gate0_bwd(y, z0, o, dcat, p["e_ssd_norm"])
    early = [g_e_w_out.reshape(N_DEV, wr, D_MODEL).astype(bf16), g_o_w_in.astype(bf16),
             g_o_w_out.reshape(N_DEV, wr, D_MODEL).astype(bf16)]
    dq_hm, dk_hm, dv, early_parts = _fox_bwd(qa, ka, qkv, do, lse, delta[0:N_HEADS].reshape(N_HEADS, 1, -1), early)
    dq, dk, dc = _fox_bwd_post(dq_hm, dk_hm)
    dpre, ddt_raw, dprm = _ssd_bwd(act, pre, dtf, prm, hs, dy_ssd)
    ddtf, dfb = _fox_gate_bwd(dc, dtf, prm, ddt_raw)
    dxraw, dw_conv0, db_conv0 = _conv_ssd_bwd(dpre, xraw, p["e_conv_w"])
    gw_dtf = _mm_tn(ddtf, u0, "dw_in0_dtf")
    g_e_w_in_t = jnp.concatenate([
        _mm_tn(dz0, u0, "dw_in0_z"), _mm_tn(dxraw, u0, "dw_in0_xbc"), gw_dtf[0:16],
        _mm_tn(dq, u0, "dw_in0_q"), _mm_tn(dk, u0, "dw_in0_k"), _mm_tn(dv, u0, "dw_in0_v"), gw_dtf[16:32]], axis=0)
    du0, last_parts = _mm_nt(
        [(dz0, 0, w_z, 0, 2048), (dxraw, 0, w_xbc, 0, 2048), (dq, 0, w_qkv, 0, 1024), (dk, 0, w_qkv, 1, 1024),
         (dv, 0, w_qkv, 2, 1024), (ddtf, 0, w_dtf, 0, 128)], f32, "du0", b_kn=True,
        gs=[g_e_w_in_t.astype(bf16).reshape(N_DEV, ew, D_MODEL)])
    grad_x, dg_pre0 = _first_bwd(xs, du0, dx1, p["e_norm_pre"])

    outs = {"e_w_in": tuple(jnp.transpose(r) for r in _adamw(
        jnp.transpose(mat(e_w_in)), jnp.transpose(mat(m_e_w_in)), jnp.transpose(mat(v_e_w_in)), last_parts[0],
        "adamw_e_w_in", by_columns=True))}
    for n, parts in zip(BIG_PARAMS[1:], early_parts):
        outs[n] = _adamw(mat(w_in[n]), mat(m_in[n]), mat(v_in[n]), parts, "adamw_" + n)

    small_grads = {
        "e_norm_pre": dg_pre0, "e_conv_w": dw_conv0, "e_conv_b": db_conv0, "e_dt_bias": dprm[0:1, 0:16],
        "e_a_log": dprm[1:2, 0:16], "e_d_skip": dprm[2:3, 0:16], "e_fgate_b": dfb[:, F_LANE:F_LANE + 16],
        "e_ssd_norm": dg_ssd_norm, "e_norm_post": dg_post0, "o_norm_pre": dg_pre1, "o_conv_w": dw_conv1,
        "o_conv_b": db_conv1, "o_ln_g": dg_ln, "o_ln_b": db_ln, "o_norm_post": dg_post1,
    }
    gathered = _all_gather(_pack([small_grads[n] for n, _, _, _ in SMALL_PARAMS] + [loss_part]), "gather_small_grads")
    summed = _unpack(_sum_parts(gathered, "sum_small_grads"), [(k, w) for _, k, w, _ in SMALL_PARAMS] + [(1, 128)])
    loss = summed[-1][0, 0]
    g_local = []
    for (n, k, w, sh), g in zip(SMALL_PARAMS, summed):
        g_local.append(lax.dynamic_slice_in_dim(g, me * (w // N_DEV), w // N_DEV, axis=1) if sh else g)
    names = [n for n, _, _, _ in SMALL_PARAMS]
    local_shapes = [(k, w // N_DEV if sh else w) for _, k, w, sh in SMALL_PARAMS]
    res = _adamw(_pack([mat(w_in[n]) for n in names]), _pack([mat(m_in[n]) for n in names]),
                 _pack([mat(v_in[n]) for n in names]), _pack(g_local)[None], "adamw_small", tr=8)
    unpacked = [_unpack(r, local_shapes) for r in res]
    for i, n in enumerate(names):
        outs[n] = tuple(u[i] for u in unpacked)

    ret = [loss, grad_x.reshape(x.shape)]
    for j in range(4):
        ret += [outs[n][j].reshape(w_in[n].shape) for n in WEIGHT_ORDER]
    return tuple(ret)
```
